```python
import math
import jax, jax.numpy as jnp
from jax import lax
import numpy as np

D_MODEL = 1024
BATCH = 16
SEQ = 2048
DEPTH = 1

RW_WIDTH = 512
RW_HEAD = 64
RW_HEADS = RW_WIDTH // RW_HEAD
RW_DECAY_RANK = 64
RW_AAA_RANK = 64
RW_GN_EPS = RW_HEAD * 1e-5
GD_WIDTH = 512
GD_HEAD = 128
GD_HEADS = GD_WIDTH // GD_HEAD
GD_CONV = 4
GD_CHUNK = 64
NORM_EPS = 1e-6

RW_SHIFT_COLS = 3 * RW_WIDTH + RW_DECAY_RANK + RW_AAA_RANK
SPLIT_RW_Z = RW_SHIFT_COLS
SPLIT_GD_QKV = SPLIT_RW_Z + RW_WIDTH
SPLIT_GD_Z = SPLIT_GD_QKV + 3 * GD_WIDTH
SPLIT_GD_BETA = SPLIT_GD_Z + GD_WIDTH
SPLIT_GD_ALPHA = SPLIT_GD_BETA + GD_HEADS
SPLIT_GATES = SPLIT_GD_ALPHA + GD_HEADS
IN_COLS = SPLIT_GATES + 2 * D_MODEL

kernel_name = "rwkv7_gdn_gated_parallel_block"


def rms_norm(x, g, eps=NORM_EPS):
    xf = x.astype(jnp.float32)
    y = xf * lax.rsqrt(jnp.mean(xf * xf, axis=-1, keepdims=True) + eps)
    return (y * g.astype(jnp.float32)).astype(x.dtype)


def l2_normalize(x, eps=1e-12):
    xf = x.astype(jnp.float32)
    return (xf * lax.rsqrt(jnp.sum(xf * xf, axis=-1, keepdims=True) + eps)).astype(x.dtype)


def token_shift(p, mu):
    prev = jnp.pad(p, ((0, 0), (1, 0), (0, 0)))[:, :-1]
    return p + (prev - p) * mu


def causal_depthwise_conv(x, w):
    K, C = w.shape
    return lax.conv_general_dilated(
        x, w[:, None, :].astype(x.dtype), window_strides=(1,), padding=[(K - 1, 0)],
        dimension_numbers=("NWC", "WIO", "NWC"), feature_group_count=C)


def rwkv7_recurrence(r, w, k, v, kk, b):
    f32 = jnp.float32
    B, T, H, N = r.shape

    def step(S, inp):
        r_t, w_t, k_t, v_t, kk_t, b_t = inp
        sa = jnp.einsum("bhvk,bhk->bhv", S, -kk_t)
        S = S * w_t[:, :, None, :] + sa[..., :, None] * b_t[..., None, :] + v_t[..., :, None] * k_t[..., None, :]
        return S, jnp.einsum("bhvk,bhk->bhv", S, r_t)

    xs = tuple(jnp.moveaxis(t.astype(f32), 1, 0) for t in (r, w, k, v, kk, b))
    _, y = lax.scan(step, jnp.zeros((B, H, N, N), f32), xs)
    return jnp.moveaxis(y, 0, 1)


def rwkv7_branch(p_rw, z_rw, mu, w0, w2, a0, a2, k_k, k_a, r_k, gn_w, gn_b):
    B, T, _ = p_rw.shape
    xs = token_shift(p_rw, mu)
    r, k, v, wd, ad = jnp.split(
        xs, [RW_WIDTH, 2 * RW_WIDTH, 3 * RW_WIDTH, 3 * RW_WIDTH + RW_DECAY_RANK], axis=-1)
    log_w = -jax.nn.softplus(-(w0 + jnp.tanh(wd) @ w2)) - 0.5
    decay = jnp.exp(-jnp.exp(log_w.astype(jnp.float32)))
    a = jax.nn.sigmoid(a0 + ad @ a2)
    heads = lambda t: t.reshape(B, T, RW_HEADS, RW_HEAD)
    kk = l2_normalize(heads(k * k_k))
    k = k * (1 + (a - 1) * k_a)
    r_h, k_h, v_h, a_h, w_h = heads(r), heads(k), heads(v), heads(a), heads(decay)
    y = rwkv7_recurrence(r_h, w_h, k_h, v_h, kk, kk * a_h)
    mean = jnp.mean(y, axis=-1, keepdims=True)
    var = jnp.mean(jnp.square(y - mean), axis=-1, keepdims=True)
    y = ((y - mean) * lax.rsqrt(var + RW_GN_EPS)).reshape(B, T, RW_WIDTH) * gn_w + gn_b
    bonus = jnp.sum(r_h * k_h * r_k, axis=-1, keepdims=True) * v_h
    y = (y + bonus.reshape(B, T, RW_WIDTH)).astype(p_rw.dtype)
    return y * jax.nn.silu(z_rw)


def chunk_gated_delta_rule(q, k, v, g, beta):
    f32 = jnp.float32
    B, H, T, D = q.shape
    C = GD_CHUNK
    N = T // C
    chunks = lambda t: t.astype(f32).reshape((B, H, N, C) + t.shape[3:])
    q = chunks(q) * (D ** -0.5)
    k, v = chunks(k), chunks(v)
    g = jnp.cumsum(chunks(g), axis=-1)
    beta = chunks(beta)
    k_beta = k * beta[..., None]
    v_beta = v * beta[..., None]
    causal = jnp.tril(jnp.ones((C, C), bool))
    strict = jnp.tril(jnp.ones((C, C), bool), -1)
    decay = jnp.exp(jnp.where(causal, g[..., :, None] - g[..., None, :], -jnp.inf))
    eye = jnp.eye(C, dtype=f32)
    A = jnp.where(strict, jnp.einsum("bhnid,bhnjd->bhnij", k_beta, k) * decay, 0.0)
    t_inv = lax.linalg.triangular_solve(eye + A, jnp.broadcast_to(eye, A.shape),
                                        left_side=True, lower=True, unit_diagonal=True)
    u = t_inv @ v_beta
    w = t_inv @ (k_beta * jnp.exp(g)[..., None])
    qk = jnp.einsum("bhnid,bhnjd->bhnij", q, k) * decay
    q_decayed = q * jnp.exp(g)[..., None]
    g_last = g[..., -1]
    k_to_end = k * jnp.exp(g_last[..., None] - g)[..., None]

    def step(S, inp):
        qd_c, kte_c, u_c, w_c, qk_c, gl_c = inp
        v_new = u_c - w_c @ S
        o = qd_c @ S + qk_c @ v_new
        S = S * jnp.exp(gl_c)[..., None, None] + jnp.einsum("bhck,bhcv->bhkv", kte_c, v_new)
        return S, o

    xs = tuple(jnp.moveaxis(t, 2, 0) for t in (q_decayed, k_to_end, u, w, qk, g_last))
    _, o = lax.scan(step, jnp.zeros((B, H, D, v.shape[-1]), f32), xs)
    return jnp.moveaxis(o, 0, 2).reshape(B, H, T, -1)


def gdn_branch(qkv, z, beta_logit, alpha, conv_w, A_log, dt_bias, o_norm_w):
    B, T, _ = qkv.shape
    f32 = jnp.float32
    qkv = jax.nn.silu(causal_depthwise_conv(qkv, conv_w))
    q, k, v = jnp.split(qkv, 3, axis=-1)
    heads = lambda t: jnp.swapaxes(t.reshape(B, T, GD_HEADS, GD_HEAD), 1, 2)
    q, k, v = l2_normalize(heads(q)), l2_normalize(heads(k)), heads(v)
    beta = jnp.swapaxes(jax.nn.sigmoid(beta_logit.astype(f32)), 1, 2)
    g = -jnp.exp(A_log.astype(f32)) * jax.nn.softplus(alpha.astype(f32) + dt_bias.astype(f32))
    g = jnp.swapaxes(g, 1, 2)
    o = jnp.swapaxes(chunk_gated_delta_rule(q, k, v, g, beta), 1, 2)
    o = rms_norm(o, o_norm_w) * jax.nn.silu(z.reshape(B, T, GD_HEADS, GD_HEAD).astype(f32))
    return o.reshape(B, T, GD_WIDTH).astype(qkv.dtype)


def _fwd_setup_inputs(seed: int = 0) -> dict:
    key = jax.random.key(seed)
    ks = jax.random.split(key, 24)
    f32 = jnp.float32
    L = DEPTH
    nrm = lambda k, shape, scale: jax.random.normal(k, shape, f32) * scale
    x = nrm(ks[0], (BATCH, SEQ, D_MODEL), 1.0)
    norm_in_w = 1.0 + nrm(ks[1], (L, D_MODEL), 0.02)
    w_in = nrm(ks[2], (L, D_MODEL, IN_COLS), D_MODEL ** -0.5)
    rw_mu = jax.random.uniform(ks[3], (L, RW_SHIFT_COLS), f32)
    rw_w0 = jax.random.uniform(ks[4], (L, RW_WIDTH), f32, -5.0, 1.0)
    rw_w2 = nrm(ks[5], (L, RW_DECAY_RANK, RW_WIDTH), 0.1)
    rw_a0 = nrm(ks[6], (L, RW_WIDTH), 0.1)
    rw_a2 = nrm(ks[7], (L, RW_AAA_RANK, RW_WIDTH), 0.1)
    rw_k_k = 0.85 + nrm(ks[8], (L, RW_WIDTH), 0.02)
    rw_k_a = 1.0 + nrm(ks[9], (L, RW_WIDTH), 0.02)
    rw_r_k = nrm(ks[10], (L, RW_HEADS, RW_HEAD), 0.1)
    rw_gn_w = 1.0 + nrm(ks[11], (L, RW_WIDTH), 0.02)
    rw_gn_b = nrm(ks[12], (L, RW_WIDTH), 0.02)
    gd_conv_w = nrm(ks[13], (L, GD_CONV, 3 * GD_WIDTH), GD_CONV ** -0.5)
    gd_A_log = jnp.log(jax.random.uniform(ks[14], (L, GD_HEADS), f32, 1.0, 16.0))
    dt = jnp.exp(jax.random.uniform(ks[15], (L, GD_HEADS), f32, math.log(1e-3), math.log(1e-1)))
    gd_dt_bias = dt + jnp.log(-jnp.expm1(-dt))
    gd_o_norm_w = 1.0 + nrm(ks[16], (L, GD_HEAD), 0.02)
    w_branch_a = nrm(ks[17], (L, RW_WIDTH, D_MODEL), RW_WIDTH ** -0.5)
    w_branch_b = nrm(ks[18], (L, GD_WIDTH, D_MODEL), GD_WIDTH ** -0.5)
    w_out = nrm(ks[19], (L, D_MODEL, D_MODEL), D_MODEL ** -0.5)
    norm_out_w = 1.0 + nrm(ks[20], (D_MODEL,), 0.02)
    return {"x": x, "norm_in_w": norm_in_w, "w_in": w_in, "rw_mu": rw_mu, "rw_w0": rw_w0,
            "rw_w2": rw_w2, "rw_a0": rw_a0, "rw_a2": rw_a2, "rw_k_k": rw_k_k, "rw_k_a": rw_k_a,
            "rw_r_k": rw_r_k, "rw_gn_w": rw_gn_w, "rw_gn_b": rw_gn_b, "gd_conv_w": gd_conv_w,
            "gd_A_log": gd_A_log, "gd_dt_bias": gd_dt_bias, "gd_o_norm_w": gd_o_norm_w,
            "w_branch_a": w_branch_a, "w_branch_b": w_branch_b, "w_out": w_out,
            "norm_out_w": norm_out_w}


def _fwd_reference(x, norm_in_w, w_in, rw_mu, rw_w0, rw_w2, rw_a0, rw_a2, rw_k_k, rw_k_a, rw_r_k,
              rw_gn_w, rw_gn_b, gd_conv_w, gd_A_log, gd_dt_bias, gd_o_norm_w,
              w_branch_a, w_branch_b, w_out, norm_out_w):
    for l in range(DEPTH):
        h = rms_norm(x, norm_in_w[l])
        p = h @ w_in[l]
        p_rw, z_rw, qkv_gd, z_gd, beta_gd, alpha_gd, gates = jnp.split(
            p, [SPLIT_RW_Z, SPLIT_GD_QKV, SPLIT_GD_Z, SPLIT_GD_BETA, SPLIT_GD_ALPHA, SPLIT_GATES],
            axis=-1)
        y_a = rwkv7_branch(p_rw, z_rw, rw_mu[l], rw_w0[l], rw_w2[l], rw_a0[l], rw_a2[l],
                           rw_k_k[l], rw_k_a[l], rw_r_k[l], rw_gn_w[l], rw_gn_b[l])
        y_b = gdn_branch(qkv_gd, z_gd, beta_gd, alpha_gd, gd_conv_w[l], gd_A_log[l],
                         gd_dt_bias[l], gd_o_norm_w[l])
        gate_a, gate_b = jnp.split(gates, 2, axis=-1)
        merged = (jax.nn.sigmoid(gate_a) * (y_a @ w_branch_a[l])
                  + jax.nn.sigmoid(gate_b) * (y_b @ w_branch_b[l]))
        x = x + merged @ w_out[l]
    return rms_norm(x, norm_out_w)


import jax as _jax
import jax.numpy as _jnp

TWIN_FORMAT = 'train_step'
FWD_PARAMS = ['x', 'norm_in_w', 'w_in', 'rw_mu', 'rw_w0', 'rw_w2', 'rw_a0', 'rw_a2', 'rw_k_k', 'rw_k_a', 'rw_r_k', 'rw_gn_w', 'rw_gn_b', 'gd_conv_w', 'gd_A_log', 'gd_dt_bias', 'gd_o_norm_w', 'w_branch_a', 'w_branch_b', 'w_out', 'norm_out_w']
TWIN_WEIGHTS = ['norm_in_w', 'w_in', 'rw_mu', 'rw_w0', 'rw_w2', 'rw_a0', 'rw_a2', 'rw_k_k', 'rw_k_a', 'rw_r_k', 'rw_gn_w', 'rw_gn_b', 'gd_conv_w', 'gd_A_log', 'gd_dt_bias', 'gd_o_norm_w', 'w_branch_a', 'w_branch_b', 'w_out', 'norm_out_w']
TWIN_DIFF_INPUT = 'x'
TWIN_INPUTS = ['x', 'norm_in_w', 'w_in', 'rw_mu', 'rw_w0', 'rw_w2', 'rw_a0', 'rw_a2', 'rw_k_k', 'rw_k_a', 'rw_r_k', 'rw_gn_w', 'rw_gn_b', 'gd_conv_w', 'gd_A_log', 'gd_dt_bias', 'gd_o_norm_w', 'w_branch_a', 'w_branch_b', 'w_out', 'norm_out_w', 'loss_target', 'm_norm_in_w', 'm_w_in', 'm_rw_mu', 'm_rw_w0', 'm_rw_w2', 'm_rw_a0', 'm_rw_a2', 'm_rw_k_k', 'm_rw_k_a', 'm_rw_r_k', 'm_rw_gn_w', 'm_rw_gn_b', 'm_gd_conv_w', 'm_gd_A_log', 'm_gd_dt_bias', 'm_gd_o_norm_w', 'm_w_branch_a', 'm_w_branch_b', 'm_w_out', 'm_norm_out_w', 'v_norm_in_w', 'v_w_in', 'v_rw_mu', 'v_rw_w0', 'v_rw_w2', 'v_rw_a0', 'v_rw_a2', 'v_rw_k_k', 'v_rw_k_a', 'v_rw_r_k', 'v_rw_gn_w', 'v_rw_gn_b', 'v_gd_conv_w', 'v_gd_A_log', 'v_gd_dt_bias', 'v_gd_o_norm_w', 'v_w_branch_a', 'v_w_branch_b', 'v_w_out', 'v_norm_out_w']
TWIN_OUTPUTS = ['loss', 'grad_x', 'grad_norm_in_w', 'grad_w_in', 'grad_rw_mu', 'grad_rw_w0', 'grad_rw_w2', 'grad_rw_a0', 'grad_rw_a2', 'grad_rw_k_k', 'grad_rw_k_a', 'grad_rw_r_k', 'grad_rw_gn_w', 'grad_rw_gn_b', 'grad_gd_conv_w', 'grad_gd_A_log', 'grad_gd_dt_bias', 'grad_gd_o_norm_w', 'grad_w_branch_a', 'grad_w_branch_b', 'grad_w_out', 'grad_norm_out_w', 'delta_norm_in_w', 'delta_w_in', 'delta_rw_mu', 'delta_rw_w0', 'delta_rw_w2', 'delta_rw_a0', 'delta_rw_a2', 'delta_rw_k_k', 'delta_rw_k_a', 'delta_rw_r_k', 'delta_rw_gn_w', 'delta_rw_gn_b', 'delta_gd_conv_w', 'delta_gd_A_log', 'delta_gd_dt_bias', 'delta_gd_o_norm_w', 'delta_w_branch_a', 'delta_w_branch_b', 'delta_w_out', 'delta_norm_out_w', 'new_m_norm_in_w', 'new_m_w_in', 'new_m_rw_mu', 'new_m_rw_w0', 'new_m_rw_w2', 'new_m_rw_a0', 'new_m_rw_a2', 'new_m_rw_k_k', 'new_m_rw_k_a', 'new_m_rw_r_k', 'new_m_rw_gn_w', 'new_m_rw_gn_b', 'new_m_gd_conv_w', 'new_m_gd_A_log', 'new_m_gd_dt_bias', 'new_m_gd_o_norm_w', 'new_m_w_branch_a', 'new_m_w_branch_b', 'new_m_w_out', 'new_m_norm_out_w', 'new_v_norm_in_w', 'new_v_w_in', 'new_v_rw_mu', 'new_v_rw_w0', 'new_v_rw_w2', 'new_v_rw_a0', 'new_v_rw_a2', 'new_v_rw_k_k', 'new_v_rw_k_a', 'new_v_rw_r_k', 'new_v_rw_gn_w', 'new_v_rw_gn_b', 'new_v_gd_conv_w', 'new_v_gd_A_log', 'new_v_gd_dt_bias', 'new_v_gd_o_norm_w', 'new_v_w_branch_a', 'new_v_w_branch_b', 'new_v_w_out', 'new_v_norm_out_w']
TWIN_LEAF_KINDS = {'loss': 'loss', 'grad_x': 'grad_x', 'grad_norm_in_w': 'grad_w', 'grad_w_in': 'grad_w', 'grad_rw_mu': 'grad_w', 'grad_rw_w0': 'grad_w', 'grad_rw_w2': 'grad_w', 'grad_rw_a0': 'grad_w', 'grad_rw_a2': 'grad_w', 'grad_rw_k_k': 'grad_w', 'grad_rw_k_a': 'grad_w', 'grad_rw_r_k': 'grad_w', 'grad_rw_gn_w': 'grad_w', 'grad_rw_gn_b': 'grad_w', 'grad_gd_conv_w': 'grad_w', 'grad_gd_A_log': 'grad_w', 'grad_gd_dt_bias': 'grad_w', 'grad_gd_o_norm_w': 'grad_w', 'grad_w_branch_a': 'grad_w', 'grad_w_branch_b': 'grad_w', 'grad_w_out': 'grad_w', 'grad_norm_out_w': 'grad_w', 'delta_norm_in_w': 'delta_w', 'delta_w_in': 'delta_w', 'delta_rw_mu': 'delta_w', 'delta_rw_w0': 'delta_w', 'delta_rw_w2': 'delta_w', 'delta_rw_a0': 'delta_w', 'delta_rw_a2': 'delta_w', 'delta_rw_k_k': 'delta_w', 'delta_rw_k_a': 'delta_w', 'delta_rw_r_k': 'delta_w', 'delta_rw_gn_w': 'delta_w', 'delta_rw_gn_b': 'delta_w', 'delta_gd_conv_w': 'delta_w', 'delta_gd_A_log': 'delta_w', 'delta_gd_dt_bias': 'delta_w', 'delta_gd_o_norm_w': 'delta_w', 'delta_w_branch_a': 'delta_w', 'delta_w_branch_b': 'delta_w', 'delta_w_out': 'delta_w', 'delta_norm_out_w': 'delta_w', 'new_m_norm_in_w': 'new_m', 'new_m_w_in': 'new_m', 'new_m_rw_mu': 'new_m', 'new_m_rw_w0': 'new_m', 'new_m_rw_w2': 'new_m', 'new_m_rw_a0': 'new_m', 'new_m_rw_a2': 'new_m', 'new_m_rw_k_k': 'new_m', 'new_m_rw_k_a': 'new_m', 'new_m_rw_r_k': 'new_m', 'new_m_rw_gn_w': 'new_m', 'new_m_rw_gn_b': 'new_m', 'new_m_gd_conv_w': 'new_m', 'new_m_gd_A_log': 'new_m', 'new_m_gd_dt_bias': 'new_m', 'new_m_gd_o_norm_w': 'new_m', 'new_m_w_branch_a': 'new_m', 'new_m_w_branch_b': 'new_m', 'new_m_w_out': 'new_m', 'new_m_norm_out_w': 'new_m', 'new_v_norm_in_w': 'new_v', 'new_v_w_in': 'new_v', 'new_v_rw_mu': 'new_v', 'new_v_rw_w0': 'new_v', 'new_v_rw_w2': 'new_v', 'new_v_rw_a0': 'new_v', 'new_v_rw_a2': 'new_v', 'new_v_rw_k_k': 'new_v', 'new_v_rw_k_a': 'new_v', 'new_v_rw_r_k': 'new_v', 'new_v_rw_gn_w': 'new_v', 'new_v_rw_gn_b': 'new_v', 'new_v_gd_conv_w': 'new_v', 'new_v_gd_A_log': 'new_v', 'new_v_gd_dt_bias': 'new_v', 'new_v_gd_o_norm_w': 'new_v', 'new_v_w_branch_a': 'new_v', 'new_v_w_branch_b': 'new_v', 'new_v_w_out': 'new_v', 'new_v_norm_out_w': 'new_v'}


def _forward(args):
    return _fwd_reference(*[args[k] for k in FWD_PARAMS])


def _output_shape():
    out = _jax.eval_shape(lambda: _forward(_fwd_setup_inputs(0)))
    return out.shape, out.dtype

N_MICROBATCH = 1
ADAM_LR = 0.001
ADAM_B1 = 0.9
ADAM_B2 = 0.999
ADAM_EPS = 1e-08
ADAM_WD = 0.01
ADAM_STEP = 10
PER_EXAMPLE_BATCH_AXIS = {'x': 0, 'loss_target': 0}
SHARED_INPUTS = []
_WEIGHT_DTYPES = {'norm_in_w': _jnp.float32, 'w_in': _jnp.float32, 'rw_mu': _jnp.float32, 'rw_w0': _jnp.float32, 'rw_w2': _jnp.float32, 'rw_a0': _jnp.float32, 'rw_a2': _jnp.float32, 'rw_k_k': _jnp.float32, 'rw_k_a': _jnp.float32, 'rw_r_k': _jnp.float32, 'rw_gn_w': _jnp.float32, 'rw_gn_b': _jnp.float32, 'gd_conv_w': _jnp.float32, 'gd_A_log': _jnp.float32, 'gd_dt_bias': _jnp.float32, 'gd_o_norm_w': _jnp.float32, 'w_branch_a': _jnp.float32, 'w_branch_b': _jnp.float32, 'w_out': _jnp.float32, 'norm_out_w': _jnp.float32}
MOMENT_SCALE = {'norm_in_w': 1.494751e-01, 'w_in': 5.883068e-02, 'rw_mu': 1.330761e-01, 'rw_w0': 4.092277e-02, 'rw_w2': 4.957588e-03, 'rw_a0': 3.342875e-02, 'rw_a2': 2.746516e-02, 'rw_k_k': 6.984307e-02, 'rw_k_a': 9.154607e-02, 'rw_r_k': 1.595147e-01, 'rw_gn_w': 7.549279e-02, 'rw_gn_b': 8.422081e-02, 'gd_conv_w': 5.582165e-02, 'gd_A_log': 1.032433e+00, 'gd_dt_bias': 1.004061e+00, 'gd_o_norm_w': 1.775337e-01, 'w_branch_a': 5.352374e-02, 'w_branch_b': 5.205412e-02, 'w_out': 7.425342e-02, 'norm_out_w': 3.202504e+01}


def _to_microbatches(a, axis):
    t = _jnp.moveaxis(a, axis, 0)
    t = t.reshape((N_MICROBATCH, t.shape[0] // N_MICROBATCH) + t.shape[1:])
    return _jnp.moveaxis(t, 1, axis + 1)


def setup_inputs(seed: int = 0) -> dict:
    inp = _fwd_setup_inputs(seed)
    key = _jax.random.fold_in(_jax.random.key(seed), 7919)
    shape, _ = _output_shape()
    out = dict(inp)
    out["loss_target"] = _jax.random.normal(_jax.random.fold_in(key, 0), shape, _jnp.float32)
    for i, name in enumerate(TWIN_WEIGHTS):
        w = inp[name].astype(_jnp.float32)
        if MOMENT_SCALE is None:
            s = _jnp.sqrt(_jnp.mean(_jnp.square(w)) + 1e-30)
        else:
            s = MOMENT_SCALE[name]
        km, kv = _jax.random.split(_jax.random.fold_in(key, i + 1))
        out[name] = w
        out["m_" + name] = s * _jax.random.normal(km, w.shape, _jnp.float32)
        out["v_" + name] = (s * s) * _jax.random.uniform(kv, w.shape, _jnp.float32, 0.5, 1.5)
    if N_MICROBATCH > 1:
        for name, axis in PER_EXAMPLE_BATCH_AXIS.items():
            out[name] = _to_microbatches(out[name], axis)
    return {'x': out['x'], 'norm_in_w': out['norm_in_w'], 'w_in': out['w_in'], 'rw_mu': out['rw_mu'], 'rw_w0': out['rw_w0'], 'rw_w2': out['rw_w2'], 'rw_a0': out['rw_a0'], 'rw_a2': out['rw_a2'], 'rw_k_k': out['rw_k_k'], 'rw_k_a': out['rw_k_a'], 'rw_r_k': out['rw_r_k'], 'rw_gn_w': out['rw_gn_w'], 'rw_gn_b': out['rw_gn_b'], 'gd_conv_w': out['gd_conv_w'], 'gd_A_log': out['gd_A_log'], 'gd_dt_bias': out['gd_dt_bias'], 'gd_o_norm_w': out['gd_o_norm_w'], 'w_branch_a': out['w_branch_a'], 'w_branch_b': out['w_branch_b'], 'w_out': out['w_out'], 'norm_out_w': out['norm_out_w'], 'loss_target': out['loss_target'], 'm_norm_in_w': out['m_norm_in_w'], 'm_w_in': out['m_w_in'], 'm_rw_mu': out['m_rw_mu'], 'm_rw_w0': out['m_rw_w0'], 'm_rw_w2': out['m_rw_w2'], 'm_rw_a0': out['m_rw_a0'], 'm_rw_a2': out['m_rw_a2'], 'm_rw_k_k': out['m_rw_k_k'], 'm_rw_k_a': out['m_rw_k_a'], 'm_rw_r_k': out['m_rw_r_k'], 'm_rw_gn_w': out['m_rw_gn_w'], 'm_rw_gn_b': out['m_rw_gn_b'], 'm_gd_conv_w': out['m_gd_conv_w'], 'm_gd_A_log': out['m_gd_A_log'], 'm_gd_dt_bias': out['m_gd_dt_bias'], 'm_gd_o_norm_w': out['m_gd_o_norm_w'], 'm_w_branch_a': out['m_w_branch_a'], 'm_w_branch_b': out['m_w_branch_b'], 'm_w_out': out['m_w_out'], 'm_norm_out_w': out['m_norm_out_w'], 'v_norm_in_w': out['v_norm_in_w'], 'v_w_in': out['v_w_in'], 'v_rw_mu': out['v_rw_mu'], 'v_rw_w0': out['v_rw_w0'], 'v_rw_w2': out['v_rw_w2'], 'v_rw_a0': out['v_rw_a0'], 'v_rw_a2': out['v_rw_a2'], 'v_rw_k_k': out['v_rw_k_k'], 'v_rw_k_a': out['v_rw_k_a'], 'v_rw_r_k': out['v_rw_r_k'], 'v_rw_gn_w': out['v_rw_gn_w'], 'v_rw_gn_b': out['v_rw_gn_b'], 'v_gd_conv_w': out['v_gd_conv_w'], 'v_gd_A_log': out['v_gd_A_log'], 'v_gd_dt_bias': out['v_gd_dt_bias'], 'v_gd_o_norm_w': out['v_gd_o_norm_w'], 'v_w_branch_a': out['v_w_branch_a'], 'v_w_branch_b': out['v_w_branch_b'], 'v_w_out': out['v_w_out'], 'v_norm_out_w': out['v_norm_out_w']}


def _loss(weights, diff, rest, loss_target):
    with _jax.named_scope("forward"):
        args = {**rest, TWIN_DIFF_INPUT: diff, **{k: w.astype(_WEIGHT_DTYPES[k]) for k, w in weights.items()}}
        y = _forward(args)
    with _jax.named_scope("loss_head"):
        err = _jnp.square(y.astype(_jnp.float32) - loss_target)
        return 0.5 * _jnp.sum(_jnp.mean(err, axis=-1)) if err.ndim else 0.5 * err


def _adamw(w, g, m, v):
    m = ADAM_B1 * m + (1.0 - ADAM_B1) * g
    v = ADAM_B2 * v + (1.0 - ADAM_B2) * _jnp.square(g)
    m_hat = m / (1.0 - ADAM_B1 ** ADAM_STEP)
    v_hat = v / (1.0 - ADAM_B2 ** ADAM_STEP)
    delta = -ADAM_LR * (m_hat / (_jnp.sqrt(v_hat) + ADAM_EPS) + ADAM_WD * w)
    return delta, m, v


def reference(x, norm_in_w, w_in, rw_mu, rw_w0, rw_w2, rw_a0, rw_a2, rw_k_k, rw_k_a, rw_r_k, rw_gn_w, rw_gn_b, gd_conv_w, gd_A_log, gd_dt_bias, gd_o_norm_w, w_branch_a, w_branch_b, w_out, norm_out_w, loss_target, m_norm_in_w, m_w_in, m_rw_mu, m_rw_w0, m_rw_w2, m_rw_a0, m_rw_a2, m_rw_k_k, m_rw_k_a, m_rw_r_k, m_rw_gn_w, m_rw_gn_b, m_gd_conv_w, m_gd_A_log, m_gd_dt_bias, m_gd_o_norm_w, m_w_branch_a, m_w_branch_b, m_w_out, m_norm_out_w, v_norm_in_w, v_w_in, v_rw_mu, v_rw_w0, v_rw_w2, v_rw_a0, v_rw_a2, v_rw_k_k, v_rw_k_a, v_rw_r_k, v_rw_gn_w, v_rw_gn_b, v_gd_conv_w, v_gd_A_log, v_gd_dt_bias, v_gd_o_norm_w, v_w_branch_a, v_w_branch_b, v_w_out, v_norm_out_w):
    given = dict(x=x, norm_in_w=norm_in_w, w_in=w_in, rw_mu=rw_mu, rw_w0=rw_w0, rw_w2=rw_w2, rw_a0=rw_a0, rw_a2=rw_a2, rw_k_k=rw_k_k, rw_k_a=rw_k_a, rw_r_k=rw_r_k, rw_gn_w=rw_gn_w, rw_gn_b=rw_gn_b, gd_conv_w=gd_conv_w, gd_A_log=gd_A_log, gd_dt_bias=gd_dt_bias, gd_o_norm_w=gd_o_norm_w, w_branch_a=w_branch_a, w_branch_b=w_branch_b, w_out=w_out, norm_out_w=norm_out_w, loss_target=loss_target, m_norm_in_w=m_norm_in_w, m_w_in=m_w_in, m_rw_mu=m_rw_mu, m_rw_w0=m_rw_w0, m_rw_w2=m_rw_w2, m_rw_a0=m_rw_a0, m_rw_a2=m_rw_a2, m_rw_k_k=m_rw_k_k, m_rw_k_a=m_rw_k_a, m_rw_r_k=m_rw_r_k, m_rw_gn_w=m_rw_gn_w, m_rw_gn_b=m_rw_gn_b, m_gd_conv_w=m_gd_conv_w, m_gd_A_log=m_gd_A_log, m_gd_dt_bias=m_gd_dt_bias, m_gd_o_norm_w=m_gd_o_norm_w, m_w_branch_a=m_w_branch_a, m_w_branch_b=m_w_branch_b, m_w_out=m_w_out, m_norm_out_w=m_norm_out_w, v_norm_in_w=v_norm_in_w, v_w_in=v_w_in, v_rw_mu=v_rw_mu, v_rw_w0=v_rw_w0, v_rw_w2=v_rw_w2, v_rw_a0=v_rw_a0, v_rw_a2=v_rw_a2, v_rw_k_k=v_rw_k_k, v_rw_k_a=v_rw_k_a, v_rw_r_k=v_rw_r_k, v_rw_gn_w=v_rw_gn_w, v_rw_gn_b=v_rw_gn_b, v_gd_conv_w=v_gd_conv_w, v_gd_A_log=v_gd_A_log, v_gd_dt_bias=v_gd_dt_bias, v_gd_o_norm_w=v_gd_o_norm_w, v_w_branch_a=v_w_branch_a, v_w_branch_b=v_w_branch_b, v_w_out=v_w_out, v_norm_out_w=v_norm_out_w)
    weights = {n: given[n] for n in TWIN_WEIGHTS}
    shared = {n: given[n] for n in SHARED_INPUTS}
    per_example = {n: given[n] for n in ['x']}
    grad_fn = _jax.value_and_grad(_loss, argnums=(0, 1))

    def one_microbatch(ex, loss_target):
        ex = dict(ex)
        diff = ex.pop(TWIN_DIFF_INPUT)
        return grad_fn(weights, diff, {**shared, **ex}, loss_target)

    if N_MICROBATCH == 1:
        loss, (grad_w, grad_x) = one_microbatch(per_example, given["loss_target"])
    else:
        def body(carry, xs):
            loss_sum, grad_sum = carry
            l_k, (gw_k, gx_k) = one_microbatch(xs[0], xs[1])
            with _jax.named_scope("update"):
                return (loss_sum + l_k, _jax.tree.map(_jnp.add, grad_sum, gw_k)), gx_k

        init = (_jnp.zeros((), _jnp.float32), _jax.tree.map(_jnp.zeros_like, weights))
        (loss, grad_w), grad_x = _jax.lax.scan(body, init, (per_example, given["loss_target"]))
    with _jax.named_scope("update"):
        delta_w, new_m, new_v = {}, {}, {}
        for n in TWIN_WEIGHTS:
            delta_w[n], new_m[n], new_v[n] = _adamw(weights[n], grad_w[n], given["m_" + n], given["v_" + n])
    return (loss, grad_x, *[grad_w[n] for n in TWIN_WEIGHTS], *[delta_w[n] for n in TWIN_WEIGHTS],
            *[new_m[n] for n in TWIN_WEIGHTS], *[new_v[n] for n in TWIN_WEIGHTS])
```

```python
import functools

import jax
import jax.numpy as jnp
from jax import lax
from jax.experimental import pallas as pl
from jax.experimental.pallas import tpu as pltpu

F32 = jnp.float32
BF16 = jnp.bfloat16
HI = lax.Precision.HIGHEST

LANES = 128
SUB = 8
CHUNK = 64
N_DEV = 8
VMEM_LIMIT = 56 * 1024 * 1024

D_MODEL = 1024
RW_W = 512
GD_W = 512
RW_SHIFT = 1664
NORM_EPS = 1e-6
RW_GN_EPS = 64 * 1e-5
ADAM_LR, ADAM_B1, ADAM_B2, ADAM_EPS, ADAM_WD, ADAM_STEP = 0.001, 0.9, 0.999, 1e-8, 0.01, 10


def _mm(a, b):
    return lax.dot_general(a, b, (((1,), (0,)), ((), ())), precision=HI, preferred_element_type=F32)


def _mm_nt(a, b):
    return lax.dot_general(a, b, (((1,), (1,)), ((), ())), precision=HI, preferred_element_type=F32)


def _mm_tn(a, b):
    return lax.dot_general(a, b, (((0,), (0,)), ((), ())), precision=HI, preferred_element_type=F32)


def _bdot(a, b, dims):
    return lax.dot_general(a.astype(BF16), b.astype(BF16), (dims, ((), ())), preferred_element_type=F32)


def _iota(shape, d):
    return lax.broadcasted_iota(jnp.int32, shape, d)


def _sigmoid(x):
    return 0.5 * (jnp.tanh(0.5 * x) + 1.0)


def _silu(x):
    return x * _sigmoid(x)


def _softplus(x):
    return jnp.maximum(x, 0.0) + jnp.log(1.0 + jnp.exp(-jnp.abs(x)))


def _seg_ones(seg):
    return ((_iota((LANES, LANES), 0) // seg) == (_iota((LANES, LANES), 1) // seg)).astype(F32)


def _sl(g):
    return slice(g * LANES, (g + 1) * LANES)


def _tri_inverse(m):
    c = CHUNK
    ri, ci = _iota((c, c), 0), _iota((c, c), 1)
    eye = (ri == ci).astype(F32)
    d16 = (ri // 16) == (ci // 16)
    d32 = (ri // 32) == (ci // 32)
    x = jnp.where(d16, -m, 0.0)
    t = eye + x
    p = x
    for _ in range(3):
        p = _mm(p, p)
        t = _mm(t, eye + p)
    q16 = jnp.where(d32 & (~d16), m, 0.0)
    t = t - _mm(_mm(t, q16), t)
    q32 = jnp.where(~d32, m, 0.0)
    t = t - _mm(_mm(t, q32), t)
    return t


def _chunk_fwd(s0, r, lw, k, v, kk, b, *, nsub, scalar_decay):
    c = CHUNK
    ri, ci = _iota((c, c), 0), _iota((c, c), 1)
    incl = ri >= ci
    strict = ri > ci
    cw = _mm(incl.astype(F32), lw)
    cwx = cw - lw
    cw_end = cw[c - 1:c, :]
    hs = LANES // nsub
    lane = _iota((1, LANES), 1)
    kkd = kk * jnp.exp(cwx)
    rd = r * jnp.exp(cw)
    kend = k * jnp.exp(cw_end - cw)
    bend = b * jnp.exp(cw_end - cw)
    if scalar_decay:
        e0 = (lane == 0).astype(F32) * jnp.ones((c, 1), F32)
        g_row = _mm_nt(e0, cw)
        dx = jnp.where(strict, jnp.exp(jnp.minimum(cwx[:, :c] - g_row, 0.0)), 0.0)
        di = jnp.where(incl, jnp.exp(jnp.minimum(cw[:, :c] - g_row, 0.0)), 0.0)
    else:
        kd = k * jnp.exp(-cw)
        bd = b * jnp.exp(-cw)
    w0 = _mm_nt(kkd, s0)
    y0 = _mm_nt(rd, s0)
    sa = jnp.zeros((c, LANES), F32)
    y = jnp.zeros((c, LANES), F32)
    for s in range(nsub):
        mf = ((lane // hs) == s).astype(F32)
        if scalar_decay:
            m_b = _mm_nt(kk * mf, b) * dx
            m_k = _mm_nt(kk * mf, k) * dx
            n_k = _mm_nt(r * mf, k) * di
            n_b = _mm_nt(r * mf, b) * di
        else:
            m_b = jnp.where(strict, _mm_nt(kkd * mf, bd), 0.0)
            m_k = jnp.where(strict, _mm_nt(kkd * mf, kd), 0.0)
            n_k = jnp.where(incl, _mm_nt(rd * mf, kd), 0.0)
            n_b = jnp.where(incl, _mm_nt(rd * mf, bd), 0.0)
        sa_s = _mm(_tri_inverse(m_b), w0 + _mm(m_k, v))
        y_s = y0 + _mm(n_k, v) - _mm(n_b, sa_s)
        sa = sa + sa_s * mf
        y = y + y_s * mf
    s_end = s0 * jnp.exp(cw_end) + _mm_tn(v, kend) - _mm_tn(sa, bend)
    if nsub > 1:
        s_end = jnp.where((_iota((LANES, LANES), 0) // hs) == (_iota((LANES, LANES), 1) // hs), s_end, 0.0)
    return y, s_end


def _rec_fwd(name, r, lw, k, v, kk, b, *, seq, nsub, scalar_decay):
    n, w = r.shape
    ng = w // LANES
    nc = seq // CHUNK
    nb = n // seq

    def body(r_ref, lw_ref, k_ref, v_ref, kk_ref, b_ref, y_ref, s_ref, state):
        @pl.when(pl.program_id(1) == 0)
        def _():
            state[...] = jnp.zeros_like(state)
        for g in range(ng):
            s0 = state[g]
            s_ref[0, g] = s0
            y, s_end = _chunk_fwd(s0, r_ref[:, _sl(g)], lw_ref[:, _sl(g)], k_ref[:, _sl(g)], v_ref[:, _sl(g)],
                                  kk_ref[:, _sl(g)], b_ref[:, _sl(g)], nsub=nsub, scalar_decay=scalar_decay)
            y_ref[:, _sl(g)] = y
            state[g] = s_end

    row = pl.BlockSpec((CHUNK, w), lambda bi, c: (bi * nc + c, 0))
    return pl.pallas_call(
        body, name=name, grid=(nb, nc),
        in_specs=[row] * 6,
        out_specs=[row, pl.BlockSpec((1, ng, LANES, LANES), lambda bi, c: (bi * nc + c, 0, 0, 0))],
        out_shape=[jax.ShapeDtypeStruct((n, w), F32), jax.ShapeDtypeStruct((nb * nc, ng, LANES, LANES), F32)],
        scratch_shapes=[pltpu.VMEM((ng, LANES, LANES), F32)],
        compiler_params=pltpu.CompilerParams(dimension_semantics=("parallel", "arbitrary"), vmem_limit_bytes=VMEM_LIMIT),
    )(r, lw, k, v, kk, b)


def _rec_bwd(name, r, lw, k, v, kk, b, s_save, dy, *, seq, nsub, scalar_decay):
    n, w = r.shape
    ng = w // LANES
    nc = seq // CHUNK
    nb = n // seq

    def body(r_ref, lw_ref, k_ref, v_ref, kk_ref, b_ref, s_ref, dy_ref,
             dr_ref, dlw_ref, dk_ref, dv_ref, dkk_ref, db_ref, dstate):
        @pl.when(pl.program_id(1) == 0)
        def _():
            dstate[...] = jnp.zeros_like(dstate)
        f = functools.partial(_chunk_fwd, nsub=nsub, scalar_decay=scalar_decay)
        for g in range(ng):
            prim = (s_ref[0, g], r_ref[:, _sl(g)], lw_ref[:, _sl(g)], k_ref[:, _sl(g)], v_ref[:, _sl(g)],
                    kk_ref[:, _sl(g)], b_ref[:, _sl(g)])
            _, vjp = jax.vjp(f, *prim)
            ds0, dr, dlw, dk, dv, dkk, db = vjp((dy_ref[:, _sl(g)], dstate[g]))
            dstate[g] = ds0
            dr_ref[:, _sl(g)] = dr
            dlw_ref[:, _sl(g)] = dlw
            dk_ref[:, _sl(g)] = dk
            dv_ref[:, _sl(g)] = dv
            dkk_ref[:, _sl(g)] = dkk
            db_ref[:, _sl(g)] = db

    row = pl.BlockSpec((CHUNK, w), lambda bi, c: (bi * nc + nc - 1 - c, 0))
    return pl.pallas_call(
        body, name=name, grid=(nb, nc),
        in_specs=[row] * 6 + [pl.BlockSpec((1, ng, LANES, LANES), lambda bi, c: (bi * nc + nc - 1 - c, 0, 0, 0)), row],
        out_specs=[row] * 6,
        out_shape=[jax.ShapeDtypeStruct((n, w), F32)] * 6,
        scratch_shapes=[pltpu.VMEM((ng, LANES, LANES), F32)],
        compiler_params=pltpu.CompilerParams(dimension_semantics=("parallel", "arbitrary"), vmem_limit_bytes=VMEM_LIMIT),
    )(r, lw, k, v, kk, b, s_save, dy)


def _shift_down(a, j, halo, is_start):
    tb = a.shape[0]
    rolled = pltpu.roll(a, j, 0)
    hr = jnp.where(is_start, 0.0, pltpu.roll(halo, j, 0))
    first = jnp.where(_iota((SUB, LANES), 0) < j, hr, rolled[0:SUB])
    if tb == SUB:
        return first
    return jnp.concatenate([first, rolled[SUB:]], axis=0)


def _shift_up(d, j, carry, is_end):
    tb = d.shape[0]
    up = pltpu.roll(d, tb - j, 0)
    cr = jnp.where(is_end, 0.0, pltpu.roll(carry, SUB - j, 0))
    last = jnp.where(_iota((SUB, LANES), 0) >= SUB - j, cr, up[tb - SUB:tb])
    if tb == SUB:
        return last
    return jnp.concatenate([up[:tb - SUB], last], axis=0)


def _ngroups(a):
    return a.shape[1] // LANES


def _pw_fwd(name, f, ins, shift, params, out_widths, out_dtypes, *, seq, tb):
    n = ins[0].shape[0]
    nt, tps = n // tb, seq // tb
    ni, npar = len(ins), len(params)

    def body(*refs):
        in_refs = refs[:ni]
        pos = ni
        halo_ref = None
        if shift:
            halo_ref = refs[pos]
            pos += 1
        p_refs = refs[pos:pos + npar]
        out_refs = refs[pos + npar:]
        is_start = (pl.program_id(0) % tps) == 0
        tiles = [[ref[:, _sl(g)] for g in range(_ngroups(ref))] for ref in in_refs]
        prevs = [[_shift_down(tiles[0][g], j, halo_ref[:, _sl(g)], is_start) for g in range(len(tiles[0]))]
                 for j in range(1, shift + 1)]
        pv = [[ref[:, _sl(g)] for g in range(_ngroups(ref))] for ref in p_refs]
        outs = f(tiles, prevs, pv)
        for o_ref, og in zip(out_refs, outs, strict=True):
            for g, t in enumerate(og):
                o_ref[:, _sl(g)] = t.astype(o_ref.dtype)

    in_specs = [pl.BlockSpec((tb, a.shape[1]), lambda i: (i, 0)) for a in ins]
    args = list(ins)
    if shift:
        in_specs.append(pl.BlockSpec((SUB, ins[0].shape[1]), lambda i: (jnp.maximum(i * (tb // SUB) - 1, 0), 0)))
        args.append(ins[0])
    in_specs += [pl.BlockSpec(p.shape, lambda i: (0, 0)) for p in params]
    args += list(params)
    return pl.pallas_call(
        body, name=name, grid=(nt,),
        in_specs=in_specs,
        out_specs=[pl.BlockSpec((tb, w), lambda i: (i, 0)) for w in out_widths],
        out_shape=[jax.ShapeDtypeStruct((n, w), dt) for w, dt in zip(out_widths, out_dtypes, strict=True)],
        compiler_params=pltpu.CompilerParams(dimension_semantics=("parallel",), vmem_limit_bytes=VMEM_LIMIT),
    )(*args)


def _pw_bwd(name, f, ins, shift, params, douts, din_dtypes, *, seq, tb):
    n = ins[0].shape[0]
    nt, tps = n // tb, seq // tb
    ni, npar = len(ins), len(params)
    flat_douts = [d for ds in douts for d in ds]
    nd = len(flat_douts)
    w0 = ins[0].shape[1]

    def body(*refs):
        in_refs = refs[:ni]
        pos = ni
        halo_ref = None
        if shift:
            halo_ref = refs[pos]
            pos += 1
        p_refs = refs[pos:pos + npar]
        pos += npar
        d_refs = refs[pos:pos + nd]
        pos += nd
        din_refs = refs[pos:pos + ni]
        pos += ni
        dp_refs = refs[pos:pos + npar]
        pos += npar
        carry = refs[pos] if shift else None
        step = pl.program_id(0)
        tile = nt - 1 - step
        is_start = (tile % tps) == 0
        is_end = (tile % tps) == tps - 1
        tiles = [[ref[:, _sl(g)] for g in range(_ngroups(ref))] for ref in in_refs]
        prevs = [[_shift_down(tiles[0][g], j, halo_ref[:, _sl(g)], is_start) for g in range(len(tiles[0]))]
                 for j in range(1, shift + 1)]
        pv = [[ref[:, _sl(g)] for g in range(_ngroups(ref))] for ref in p_refs]
        cot, pos_d = [], 0
        for ds in douts:
            grp = d_refs[pos_d:pos_d + len(ds)]
            pos_d += len(ds)
            cot.append([sum(ref[:, _sl(g)].astype(F32) for ref in grp) for g in range(_ngroups(grp[0]))])
        _, vjp = jax.vjp(f, tiles, prevs, pv)
        d_tiles, d_prevs, d_pv = vjp(cot)
        for g in range(len(tiles[0])):
            for j in range(1, shift + 1):
                d_tiles[0][g] = d_tiles[0][g] + _shift_up(d_prevs[j - 1][g], j, carry[j - 1, :, _sl(g)], is_end)
            for j in range(1, shift + 1):
                carry[j - 1, :, _sl(g)] = d_prevs[j - 1][g][0:SUB]
        for ref, dg in zip(din_refs, d_tiles, strict=True):
            for g, t in enumerate(dg):
                ref[:, _sl(g)] = t.astype(ref.dtype)

        @pl.when(step == 0)
        def _():
            for ref in dp_refs:
                ref[...] = jnp.zeros_like(ref)
        for ref, dg in zip(dp_refs, d_pv, strict=True):
            for g, t in enumerate(dg):
                ref[:, _sl(g)] += t

    rev = lambda i: (nt - 1 - i, 0)
    in_specs = [pl.BlockSpec((tb, a.shape[1]), rev) for a in ins]
    args = list(ins)
    if shift:
        in_specs.append(pl.BlockSpec((SUB, w0), lambda i: (jnp.maximum((nt - 1 - i) * (tb // SUB) - 1, 0), 0)))
        args.append(ins[0])
    in_specs += [pl.BlockSpec(p.shape, lambda i: (0, 0)) for p in params]
    args += list(params)
    in_specs += [pl.BlockSpec((tb, d.shape[1]), rev) for d in flat_douts]
    args += flat_douts
    out_specs = [pl.BlockSpec((tb, a.shape[1]), rev) for a in ins] + [pl.BlockSpec(p.shape, lambda i: (0, 0)) for p in params]
    out_shape = ([jax.ShapeDtypeStruct(a.shape, dt) for a, dt in zip(ins, din_dtypes, strict=True)]
                 + [jax.ShapeDtypeStruct(p.shape, F32) for p in params])
    res = pl.pallas_call(
        body, name=name, grid=(nt,),
        in_specs=in_specs, out_specs=out_specs, out_shape=out_shape,
        scratch_shapes=[pltpu.VMEM((shift, SUB, w0), F32)] if shift else [],
        compiler_params=pltpu.CompilerParams(dimension_semantics=("arbitrary",), vmem_limit_bytes=VMEM_LIMIT),
    )(*args)
    return res[:ni], res[ni:]


def _rwkv_prep_f(tiles, prevs, params):
    (p,), (prev,) = tiles, prevs
    mu, w0, w2p, a0, a2p, k_k, k_a = params
    xs = [p[g] + (prev[g] - p[g]) * mu[g] for g in range(13)]
    wdad = xs[12]
    tw = jnp.tanh(wdad)
    e64 = _seg_ones(64)
    r, lw, k2, v, kk, b = [], [], [], [], [], []
    for g in range(4):
        k_g = xs[4 + g]
        lo = w0[g] + _mm(tw, w2p[g])
        lw_g = -jnp.exp(-_softplus(-lo) - 0.5)
        a_g = _sigmoid(a0[g] + _mm(wdad, a2p[g]))
        kkp = k_g * k_k[g]
        kk_g = kkp * lax.rsqrt(_mm(kkp * kkp, e64) + 1e-12)
        r.append(xs[g])
        lw.append(lw_g)
        k2.append(k_g * (1.0 + (a_g - 1.0) * k_a[g]))
        v.append(xs[8 + g])
        kk.append(kk_g)
        b.append(kk_g * a_g)
    return [r, lw, k2, v, kk, b]


def _rwkv_post_f(tiles, prevs, params):
    yrec, r, k2, v, z = tiles
    gn_w, gn_b, r_k = params
    e64 = _seg_ones(64)
    out = []
    for g in range(4):
        mean = _mm(yrec[g], e64) * (1.0 / 64)
        d = yrec[g] - mean
        var = _mm(d * d, e64) * (1.0 / 64)
        yn = d * lax.rsqrt(var + RW_GN_EPS) * gn_w[g] + gn_b[g]
        bonus = _mm(r[g] * k2[g] * r_k[g], e64) * v[g]
        out.append((yn + bonus) * _silu(z[g]))
    return [out]


def _gdn_prep_f(tiles, prevs, params):
    x, (ba,) = tiles
    p1, p2, p3 = prevs
    cw0, cw1, cw2, cw3, a_log, dt_bias = params
    s = [_silu(cw3[g] * x[g] + cw2[g] * p1[g] + cw1[g] * p2[g] + cw0[g] * p3[g]) for g in range(12)]
    row = _iota((LANES, LANES), 0)
    r, lw, k, vv, b = [], [], [], [], []
    for h in range(4):
        q_h, k_h, v_h = s[h], s[4 + h], s[8 + h]
        qn = q_h * lax.rsqrt(jnp.sum(q_h * q_h, axis=-1, keepdims=True) + 1e-12)
        kn = k_h * lax.rsqrt(jnp.sum(k_h * k_h, axis=-1, keepdims=True) + 1e-12)
        beta = _sigmoid(_mm(ba, (row == h).astype(F32)))
        alpha = _mm(ba, (row == 4 + h).astype(F32))
        g_h = -jnp.exp(a_log[h]) * _softplus(alpha + dt_bias[h])
        r.append(qn * (LANES ** -0.5))
        lw.append(g_h)
        k.append(kn)
        vv.append(beta * v_h)
        b.append(jnp.exp(g_h) * beta * kn)
    return [r, lw, k, vv, b]


def _gdn_post_f(tiles, prevs, params):
    o, z = tiles
    ((onw,),) = params
    out = []
    for h in range(4):
        ms = jnp.mean(o[h] * o[h], axis=-1, keepdims=True)
        out.append(o[h] * lax.rsqrt(ms + NORM_EPS) * onw * _silu(z[h]))
    return [out]


def _norm_in(x2, g_in, *, tm):
    n = x2.shape[0]

    def body(x_ref, g_ref, h_ref):
        x = x_ref[...]
        rs = lax.rsqrt(jnp.mean(x * x, axis=-1, keepdims=True) + NORM_EPS)
        h_ref[...] = (x * rs * g_ref[...]).astype(BF16)

    return pl.pallas_call(
        body, name="norm_in", grid=(n // tm,),
        in_specs=[pl.BlockSpec((tm, D_MODEL), lambda i: (i, 0)), pl.BlockSpec((1, D_MODEL), lambda i: (0, 0))],
        out_specs=pl.BlockSpec((tm, D_MODEL), lambda i: (i, 0)),
        out_shape=jax.ShapeDtypeStruct((n, D_MODEL), BF16),
        compiler_params=pltpu.CompilerParams(dimension_semantics=("parallel",), vmem_limit_bytes=VMEM_LIMIT),
    )(x2, g_in)


def _proj(name, h, w, *, tm):
    n, ws = h.shape[0], w.shape[1]

    def body(h_ref, w_ref, o_ref):
        o_ref[...] = jnp.dot(h_ref[...], w_ref[...], preferred_element_type=F32)

    return pl.pallas_call(
        body, name=name, grid=(n // tm,),
        in_specs=[pl.BlockSpec((tm, D_MODEL), lambda i: (i, 0)), pl.BlockSpec((D_MODEL, ws), lambda i: (0, 0))],
        out_specs=pl.BlockSpec((tm, ws), lambda i: (i, 0)),
        out_shape=jax.ShapeDtypeStruct((n, ws), F32),
        compiler_params=pltpu.CompilerParams(dimension_semantics=("parallel",), vmem_limit_bytes=VMEM_LIMIT),
    )(h, w)


def _proj_dw(name, h, dp, *, tm):
    n, ws = dp.shape

    def body(h_ref, d_ref, o_ref):
        @pl.when(pl.program_id(0) == 0)
        def _():
            o_ref[...] = jnp.zeros_like(o_ref)
        o_ref[...] += lax.dot_general(h_ref[...], d_ref[...], (((0,), (0,)), ((), ())), preferred_element_type=F32)

    return pl.pallas_call(
        body, name=name, grid=(n // tm,),
        in_specs=[pl.BlockSpec((tm, D_MODEL), lambda i: (i, 0)), pl.BlockSpec((tm, ws), lambda i: (i, 0))],
        out_specs=pl.BlockSpec((D_MODEL, ws), lambda i: (0, 0)),
        out_shape=jax.ShapeDtypeStruct((D_MODEL, ws), F32),
        compiler_params=pltpu.CompilerParams(dimension_semantics=("arbitrary",), vmem_limit_bytes=VMEM_LIMIT),
    )(h, dp)


def _proj_dx(x2, g_in, d_xo, dps, ws, *, tm):
    n = x2.shape[0]
    ns = len(dps)

    def body(*refs):
        x_ref, g_ref, dxo_ref = refs[:3]
        dp_refs = refs[3:3 + ns]
        w_refs = refs[3 + ns:3 + 2 * ns]
        dx_ref, dg_ref = refs[3 + 2 * ns:]
        dh = jnp.zeros((tm, D_MODEL), F32)
        for d_ref, w_ref in zip(dp_refs, w_refs, strict=True):
            dh = dh + lax.dot_general(d_ref[...], w_ref[...], (((1,), (1,)), ((), ())), preferred_element_type=F32)
        x = x_ref[...]
        rs = lax.rsqrt(jnp.mean(x * x, axis=-1, keepdims=True) + NORM_EPS)
        xn = x * rs
        dxn = dh * g_ref[...]
        dx_ref[...] = dxo_ref[...] + rs * (dxn - xn * jnp.mean(dxn * xn, axis=-1, keepdims=True))

        @pl.when(pl.program_id(0) == 0)
        def _():
            dg_ref[...] = jnp.zeros_like(dg_ref)
        dg_ref[...] += jnp.sum(dh * xn, axis=0, keepdims=True)

    row = pl.BlockSpec((tm, D_MODEL), lambda i: (i, 0))
    return pl.pallas_call(
        body, name="proj_dx", grid=(n // tm,),
        in_specs=([row, pl.BlockSpec((1, D_MODEL), lambda i: (0, 0)), row]
                  + [pl.BlockSpec((tm, d.shape[1]), lambda i: (i, 0)) for d in dps]
                  + [pl.BlockSpec(w.shape, lambda i: (0, 0)) for w in ws]),
        out_specs=[row, pl.BlockSpec((1, D_MODEL), lambda i: (0, 0))],
        out_shape=[jax.ShapeDtypeStruct((n, D_MODEL), F32), jax.ShapeDtypeStruct((1, D_MODEL), F32)],
        compiler_params=pltpu.CompilerParams(dimension_semantics=("arbitrary",), vmem_limit_bytes=VMEM_LIMIT),
    )(x2, g_in, d_xo, *dps, *ws)


def _tail(x2, tgt2, gates, ya, yb, w_a, w_b, w_o, now, *, tr):
    n = x2.shape[0]

    def body(x_ref, t_ref, g_ref, ya_ref, yb_ref, wa_ref, wb_ref, wo_ref, now_ref,
             dya_ref, dyb_ref, dg_ref, dxo_ref, dwa_ref, dwb_ref, dwo_ref, dnow_ref, loss_ref):
        ya16, yb16 = ya_ref[...].astype(BF16), yb_ref[...].astype(BF16)
        ua = jnp.dot(ya16, wa_ref[...], preferred_element_type=F32)
        ub = jnp.dot(yb16, wb_ref[...], preferred_element_type=F32)
        ga = _sigmoid(g_ref[:, :D_MODEL])
        gb = _sigmoid(g_ref[:, D_MODEL:])
        m16 = (ga * ua + gb * ub).astype(BF16)
        xo = x_ref[...] + jnp.dot(m16, wo_ref[...], preferred_element_type=F32)
        rs = lax.rsqrt(jnp.mean(xo * xo, axis=-1, keepdims=True) + NORM_EPS)
        yn = xo * rs
        now_v = now_ref[...]
        err = yn * now_v - t_ref[...]
        dy = err * (1.0 / D_MODEL)
        dyn = dy * now_v
        dxo = rs * (dyn - yn * jnp.mean(dyn * yn, axis=-1, keepdims=True))
        dxo_ref[...] = dxo
        dxo16 = dxo.astype(BF16)
        dm = lax.dot_general(dxo16, wo_ref[...], (((1,), (1,)), ((), ())), preferred_element_type=F32)
        dua16 = (dm * ga).astype(BF16)
        dub16 = (dm * gb).astype(BF16)
        dg_ref[:, :D_MODEL] = (dm * ua * ga * (1.0 - ga)).astype(dg_ref.dtype)
        dg_ref[:, D_MODEL:] = (dm * ub * gb * (1.0 - gb)).astype(dg_ref.dtype)
        dya_ref[...] = lax.dot_general(dua16, wa_ref[...], (((1,), (1,)), ((), ())), preferred_element_type=F32)
        dyb_ref[...] = lax.dot_general(dub16, wb_ref[...], (((1,), (1,)), ((), ())), preferred_element_type=F32)

        @pl.when(pl.program_id(0) == 0)
        def _():
            for ref in (dwa_ref, dwb_ref, dwo_ref, dnow_ref, loss_ref):
                ref[...] = jnp.zeros_like(ref)
        tn = (((0,), (0,)), ((), ()))
        dwo_ref[...] += lax.dot_general(m16, dxo16, tn, preferred_element_type=F32)
        dwa_ref[...] += lax.dot_general(ya16, dua16, tn, preferred_element_type=F32)
        dwb_ref[...] += lax.dot_general(yb16, dub16, tn, preferred_element_type=F32)
        dnow_ref[...] += jnp.sum(dy * yn, axis=0, keepdims=True)
        loss_ref[...] += (0.5 / D_MODEL) * jnp.sum(err * err)

    row = lambda w: pl.BlockSpec((tr, w), lambda i: (i, 0))
    full = lambda a: pl.BlockSpec(a.shape, lambda i: (0, 0))
    return pl.pallas_call(
        body, name="tail", grid=(n // tr,),
        in_specs=[row(D_MODEL), row(D_MODEL), row(2 * D_MODEL), row(RW_W), row(GD_W), full(w_a), full(w_b), full(w_o), full(now)],
        out_specs=[row(RW_W), row(GD_W), row(2 * D_MODEL), row(D_MODEL),
                   pl.BlockSpec((RW_W, D_MODEL), lambda i: (0, 0)), pl.BlockSpec((GD_W, D_MODEL), lambda i: (0, 0)),
                   pl.BlockSpec((D_MODEL, D_MODEL), lambda i: (0, 0)), pl.BlockSpec((1, D_MODEL), lambda i: (0, 0)),
                   pl.BlockSpec((SUB, LANES), lambda i: (0, 0))],
        out_shape=[jax.ShapeDtypeStruct((n, RW_W), F32), jax.ShapeDtypeStruct((n, GD_W), F32),
                   jax.ShapeDtypeStruct((n, 2 * D_MODEL), BF16), jax.ShapeDtypeStruct((n, D_MODEL), F32),
                   jax.ShapeDtypeStruct((RW_W, D_MODEL), F32), jax.ShapeDtypeStruct((GD_W, D_MODEL), F32),
                   jax.ShapeDtypeStruct((D_MODEL, D_MODEL), F32), jax.ShapeDtypeStruct((1, D_MODEL), F32),
                   jax.ShapeDtypeStruct((SUB, LANES), F32)],
        compiler_params=pltpu.CompilerParams(dimension_semantics=("arbitrary",), vmem_limit_bytes=VMEM_LIMIT),
    )(x2, tgt2, gates, ya, yb, w_a, w_b, w_o, now)


def _exchange(name, scatter, gather):
    ns, ng = len(scatter), len(gather)
    na = ns + ng

    def body(*refs):
        src = refs[:na]
        dst = refs[na:2 * na]
        send_sems, recv_sems, local_sems = refs[2 * na:]
        x, y, c = lax.axis_index("x"), lax.axis_index("y"), lax.axis_index("c")
        me = 4 * x + 2 * y + c

        def peer(k):
            px = 1 - x if k & 4 else x
            py = 1 - y if k & 2 else y
            pc = 1 - c if k & 1 else c
            return (px, py, pc), 4 * px + 2 * py + pc

        def copy(a, k):
            dev, idx = peer(k)
            s = src[a].at[idx] if a < ns else src[a]
            return pltpu.make_async_remote_copy(src_ref=s, dst_ref=dst[a].at[me], send_sem=send_sems.at[a, k - 1],
                                                recv_sem=recv_sems.at[a, k - 1], device_id=dev,
                                                device_id_type=pl.DeviceIdType.MESH)

        def arrival(a, k):
            dev, idx = peer(k)
            s = src[a].at[idx] if a < ns else src[a]
            return pltpu.make_async_remote_copy(src_ref=s, dst_ref=dst[a].at[idx], send_sem=send_sems.at[a, k - 1],
                                                recv_sem=recv_sems.at[a, k - 1], device_id=dev,
                                                device_id_type=pl.DeviceIdType.MESH)

        local = [pltpu.make_async_copy(src[a].at[me] if a < ns else src[a], dst[a].at[me], local_sems.at[a])
                 for a in range(na)]
        sends = [copy(a, k) for a in range(na) for k in range(1, N_DEV)]
        for cp in local + sends:
            cp.start()
        for a in range(na):
            for k in range(1, N_DEV):
                arrival(a, k).wait_recv()
        for cp in sends:
            cp.wait_send()
        for cp in local:
            cp.wait()

    arrs = list(scatter) + list(gather)
    out_shape = [jax.ShapeDtypeStruct(a.shape, a.dtype) for a in scatter] + \
                [jax.ShapeDtypeStruct((N_DEV,) + a.shape, a.dtype) for a in gather]
    anyspec = pl.BlockSpec(memory_space=pl.ANY)
    return pl.pallas_call(
        body, name=name,
        in_specs=[anyspec] * na, out_specs=[anyspec] * na, out_shape=out_shape,
        scratch_shapes=[pltpu.SemaphoreType.DMA((na, N_DEV - 1)), pltpu.SemaphoreType.DMA((na, N_DEV - 1)),
                        pltpu.SemaphoreType.DMA((na,))],
    )(*arrs)


def _adam(name, land, w, m, v):
    r, c = w.shape
    tr = 256 if (r % 256 == 0 and r > 256) else r
    c1 = 1.0 / (1.0 - ADAM_B1 ** ADAM_STEP)
    c2 = 1.0 / (1.0 - ADAM_B2 ** ADAM_STEP)

    def body(l_ref, w_ref, m_ref, v_ref, g_out, d_out, m_out, v_out):
        g = l_ref[0]
        for s in range(1, N_DEV):
            g = g + l_ref[s]
        m_new = ADAM_B1 * m_ref[...] + (1.0 - ADAM_B1) * g
        v_new = ADAM_B2 * v_ref[...] + (1.0 - ADAM_B2) * (g * g)
        g_out[...] = g
        m_out[...] = m_new
        v_out[...] = v_new
        d_out[...] = -ADAM_LR * ((m_new * c1) / (jnp.sqrt(v_new * c2) + ADAM_EPS) + ADAM_WD * w_ref[...])

    blk = pl.BlockSpec((tr, c), lambda i: (i, 0))
    return pl.pallas_call(
        body, name=name, grid=(r // tr,),
        in_specs=[pl.BlockSpec((N_DEV, tr, c), lambda i: (0, i, 0)), blk, blk, blk],
        out_specs=[blk] * 4,
        out_shape=[jax.ShapeDtypeStruct((r, c), F32)] * 4,
        compiler_params=pltpu.CompilerParams(dimension_semantics=("parallel",), vmem_limit_bytes=VMEM_LIMIT),
    )(land, w, m, v)


_SMALL = (("norm_in_w", 1024), ("rw_mu", 1664), ("rw_w0", 512), ("rw_a0", 512), ("rw_k_k", 512), ("rw_k_a", 512),
          ("rw_r_k", 512), ("rw_gn_w", 512), ("rw_gn_b", 512), ("gd_A_log", 4), ("gd_dt_bias", 4), ("gd_o_norm_w", 128),
          ("norm_out_w", 1024))
_SMALL_ROWS = 64


def _pack_small(vals):
    rows = []
    for (_, size), a in zip(_SMALL, vals, strict=True):
        flat = a.reshape(-1).astype(F32)
        pad = (-size) % LANES
        if pad:
            flat = jnp.concatenate([flat, jnp.zeros((pad,), F32)])
        rows.append(flat.reshape(-1, LANES))
    used = sum(r.shape[0] for r in rows)
    rows.append(jnp.zeros((_SMALL_ROWS - used, LANES), F32))
    return jnp.concatenate(rows, axis=0)


def _unpack_small(packed, shapes):
    out, row = [], 0
    for (_, size), shp in zip(_SMALL, shapes, strict=True):
        nrow = -(-size // LANES)
        out.append(packed[row:row + nrow].reshape(-1)[:size].reshape(shp))
        row += nrow
    return out


def kernel(x, norm_in_w, w_in, rw_mu, rw_w0, rw_w2, rw_a0, rw_a2, rw_k_k, rw_k_a, rw_r_k, rw_gn_w, rw_gn_b, gd_conv_w, gd_A_log, gd_dt_bias, gd_o_norm_w, w_branch_a, w_branch_b, w_out, norm_out_w, loss_target, m_norm_in_w, m_w_in, m_rw_mu, m_rw_w0, m_rw_w2, m_rw_a0, m_rw_a2, m_rw_k_k, m_rw_k_a, m_rw_r_k, m_rw_gn_w, m_rw_gn_b, m_gd_conv_w, m_gd_A_log, m_gd_dt_bias, m_gd_o_norm_w, m_w_branch_a, m_w_branch_b, m_w_out, m_norm_out_w, v_norm_in_w, v_w_in, v_rw_mu, v_rw_w0, v_rw_w2, v_rw_a0, v_rw_a2, v_rw_k_k, v_rw_k_a, v_rw_r_k, v_rw_gn_w, v_rw_gn_b, v_gd_conv_w, v_gd_A_log, v_gd_dt_bias, v_gd_o_norm_w, v_w_branch_a, v_w_branch_b, v_w_out, v_norm_out_w):
    nb, seq, _ = x.shape
    n = nb * seq
    tm = min(512, n)
    tb = min(256, seq)
    x2 = x.reshape(n, D_MODEL)
    tgt2 = loss_target.reshape(n, D_MODEL)
    cols = w_in.shape[2]
    in_cols = cols * N_DEV

    sharded = [w_in[0], rw_w2[0], rw_a2[0], gd_conv_w[0], w_branch_a[0], w_branch_b[0], w_out[0]]
    g_win, g_w2, g_a2, g_conv, g_wa, g_wb, g_wo = _exchange("gather_weights", [], sharded)
    unshard_cols = lambda a: jnp.transpose(a, (1, 0, 2)).reshape(a.shape[1], N_DEV * a.shape[2])
    w_full = unshard_cols(g_win).astype(BF16)
    seg_bounds = ((0, 1664), (1664, 2176), (2176, 3712), (3712, 4224), (4232, in_cols))
    w_rw, w_zrw, w_qkv, w_zgd, w_gates = [w_full[:, a:b] for a, b in seg_bounds]
    w_ba = jnp.concatenate([w_full[:, 4224:4232], jnp.zeros((D_MODEL, LANES - 8), BF16)], axis=1)
    w2_full, a2_full = unshard_cols(g_w2), unshard_cols(g_a2)
    zeros64 = jnp.zeros((64, RW_W), F32)
    w2p = jnp.concatenate([w2_full, zeros64], axis=0)
    a2p = jnp.concatenate([zeros64, a2_full], axis=0)
    conv_full = unshard_cols(g_conv)
    conv_rows = [conv_full[i:i + 1] for i in range(4)]
    wa_full = unshard_cols(g_wa).astype(BF16)
    wb_full = unshard_cols(g_wb).astype(BF16)
    wo_full = g_wo.reshape(D_MODEL, D_MODEL).astype(BF16)
    a_log_bc = jnp.repeat(gd_A_log, LANES, axis=1)
    dt_bias_bc = jnp.repeat(gd_dt_bias, LANES, axis=1)
    r_k_flat = rw_r_k.reshape(1, RW_W)
    now2 = norm_out_w.reshape(1, D_MODEL)

    h = _norm_in(x2, norm_in_w, tm=tm)
    p_rw = _proj("proj_rw", h, w_rw, tm=tm)
    p_zrw = _proj("proj_zrw", h, w_zrw, tm=tm)
    p_qkv = _proj("proj_qkv", h, w_qkv, tm=tm)
    p_zgd = _proj("proj_zgd", h, w_zgd, tm=tm)
    p_ba = _proj("proj_ba", h, w_ba, tm=tm)
    p_gates = _proj("proj_gates", h, w_gates, tm=tm)

    rw_params = [rw_mu, rw_w0, w2p, rw_a0, a2p, rw_k_k, rw_k_a]
    r_a, lw_a, k_a, v_a, kk_a, b_a = _pw_fwd("rwkv_prep", _rwkv_prep_f, [p_rw], 1, rw_params, [RW_W] * 6, [F32] * 6,
                                             seq=seq, tb=tb)
    y_rec, s_a = _rec_fwd("rwkv_rec", r_a, lw_a, k_a, v_a, kk_a, b_a, seq=seq, nsub=2, scalar_decay=False)
    post_params = [rw_gn_w, rw_gn_b, r_k_flat]
    (y_a,) = _pw_fwd("rwkv_post", _rwkv_post_f, [y_rec, r_a, k_a, v_a, p_zrw], 0, post_params, [RW_W], [F32], seq=seq, tb=tb)

    gd_params = conv_rows + [a_log_bc, dt_bias_bc]
    r_b, lw_b, k_b, v_b, b_b = _pw_fwd("gdn_prep", _gdn_prep_f, [p_qkv, p_ba], 3, gd_params, [GD_W] * 5, [F32] * 5,
                                       seq=seq, tb=tb)
    o_rec, s_b = _rec_fwd("gdn_rec", r_b, lw_b, k_b, v_b, k_b, b_b, seq=seq, nsub=1, scalar_decay=True)
    (y_b,) = _pw_fwd("gdn_post", _gdn_post_f, [o_rec, p_zgd], 0, [gd_o_norm_w], [GD_W], [F32], seq=seq, tb=tb)

    d_ya, d_yb, d_gates, d_xo, dwa, dwb, dwo, d_now, loss_acc = _tail(
        x2, tgt2, p_gates, y_a, y_b, wa_full, wb_full, wo_full, now2, tr=min(256, n))

    (d_o, d_zgd), (d_onw,) = _pw_bwd("gdn_post_bwd", _gdn_post_f, [o_rec, p_zgd], 0, [gd_o_norm_w], [[d_yb]],
                                     [F32, BF16], seq=seq, tb=tb)
    dr_b, dlw_b, dk_b, dv_b, dkk_b, db_b = _rec_bwd("gdn_rec_bwd", r_b, lw_b, k_b, v_b, k_b, b_b, s_b, d_o,
                                                    seq=seq, nsub=1, scalar_decay=True)
    (d_qkv, d_ba), d_gd_params = _pw_bwd("gdn_prep_bwd", _gdn_prep_f, [p_qkv, p_ba], 3, gd_params,
                                         [[dr_b], [dlw_b], [dk_b, dkk_b], [dv_b], [db_b]], [BF16, BF16], seq=seq, tb=tb)

    (d_yrec, dr_p, dk_p, dv_p, d_zrw), d_post_params = _pw_bwd(
        "rwkv_post_bwd", _rwkv_post_f, [y_rec, r_a, k_a, v_a, p_zrw], 0, post_params, [[d_ya]],
        [F32, F32, F32, F32, BF16], seq=seq, tb=tb)
    dr_a, dlw_a, dk_a, dv_a, dkk_a, db_a = _rec_bwd("rwkv_rec_bwd", r_a, lw_a, k_a, v_a, kk_a, b_a, s_a, d_yrec,
                                                    seq=seq, nsub=2, scalar_decay=False)
    (d_prw,), d_rw_params = _pw_bwd("rwkv_prep_bwd", _rwkv_prep_f, [p_rw], 1, rw_params,
                                    [[dr_a, dr_p], [dlw_a], [dk_a, dk_p], [dv_a, dv_p], [dkk_a], [db_a]], [BF16],
                                    seq=seq, tb=tb)

    dps = [d_prw, d_zrw, d_qkv, d_zgd, d_ba, d_gates]
    wsegs = [w_rw, w_zrw, w_qkv, w_zgd, w_ba, w_gates]
    dx2, d_gin = _proj_dx(x2, norm_in_w, d_xo, dps, wsegs, tm=min(256, n))
    dw_rw = _proj_dw("dw_rw", h, d_prw, tm=tm)
    dw_zrw = _proj_dw("dw_zrw", h, d_zrw, tm=tm)
    dw_qkv = _proj_dw("dw_qkv", h, d_qkv, tm=tm)
    dw_zgd = _proj_dw("dw_zgd", h, d_zgd, tm=tm)
    dw_ba = _proj_dw("dw_ba", h, d_ba, tm=tm)
    dw_gates = _proj_dw("dw_gates", h, d_gates, tm=tm)
    dw_in_full = jnp.concatenate([dw_rw, dw_zrw, dw_qkv, dw_zgd, dw_ba[:, :8], dw_gates], axis=1)

    shard_cols = lambda a: jnp.transpose(a.reshape(a.shape[0], N_DEV, a.shape[1] // N_DEV), (1, 0, 2))
    d_mu, d_w0, d_w2p, d_a0, d_a2p, d_kk_, d_ka_ = d_rw_params
    d_gnw, d_gnb, d_rk = d_post_params
    d_conv = jnp.concatenate(d_gd_params[:4], axis=0)
    d_alog = d_gd_params[4].reshape(4, LANES).sum(axis=1).reshape(1, 4)
    d_dtb = d_gd_params[5].reshape(4, LANES).sum(axis=1).reshape(1, 4)
    scat = [shard_cols(dw_in_full), shard_cols(d_w2p[:64]), shard_cols(d_a2p[64:]), shard_cols(d_conv),
            shard_cols(dwa), shard_cols(dwb), dwo.reshape(N_DEV, D_MODEL // N_DEV, D_MODEL)]
    small_g = _pack_small([d_gin, d_mu, d_w0, d_a0, d_kk_, d_ka_, d_rk, d_gnw, d_gnb, d_alog, d_dtb, d_onw, d_now])
    lands = _exchange("reduce_grads", scat, [small_g])

    small_w = [norm_in_w, rw_mu, rw_w0, rw_a0, rw_k_k, rw_k_a, rw_r_k, rw_gn_w, rw_gn_b, gd_A_log, gd_dt_bias, gd_o_norm_w, norm_out_w]
    small_m = [m_norm_in_w, m_rw_mu, m_rw_w0, m_rw_a0, m_rw_k_k, m_rw_k_a, m_rw_r_k, m_rw_gn_w, m_rw_gn_b, m_gd_A_log, m_gd_dt_bias, m_gd_o_norm_w, m_norm_out_w]
    small_v = [v_norm_in_w, v_rw_mu, v_rw_w0, v_rw_a0, v_rw_k_k, v_rw_k_a, v_rw_r_k, v_rw_gn_w, v_rw_gn_b, v_gd_A_log, v_gd_dt_bias, v_gd_o_norm_w, v_norm_out_w]
    small_shapes = [a.shape for a in small_w]
    sm = _adam("adam_small", lands[7], _pack_small(small_w), _pack_small(small_m), _pack_small(small_v))
    sm_g, sm_d, sm_m, sm_v = [dict(zip([nm for nm, _ in _SMALL], _unpack_small(p, small_shapes))) for p in sm]

    big = {}
    for nm, land, w, m, v in (("w_in", lands[0], w_in, m_w_in, v_w_in), ("rw_w2", lands[1], rw_w2, m_rw_w2, v_rw_w2),
                              ("rw_a2", lands[2], rw_a2, m_rw_a2, v_rw_a2),
                              ("gd_conv_w", lands[3], gd_conv_w, m_gd_conv_w, v_gd_conv_w),
                              ("w_branch_a", lands[4], w_branch_a, m_w_branch_a, v_w_branch_a),
                              ("w_branch_b", lands[5], w_branch_b, m_w_branch_b, v_w_branch_b),
                              ("w_out", lands[6], w_out, m_w_out, v_w_out)):
        big[nm] = [o.reshape(w.shape) for o in _adam("adam_" + nm, land, w[0], m[0], v[0])]

    order = ["norm_in_w", "w_in", "rw_mu", "rw_w0", "rw_w2", "rw_a0", "rw_a2", "rw_k_k", "rw_k_a", "rw_r_k", "rw_gn_w",
             "rw_gn_b", "gd_conv_w", "gd_A_log", "gd_dt_bias", "gd_o_norm_w", "w_branch_a", "w_branch_b", "w_out", "norm_out_w"]
    pick = lambda nm, i: big[nm][i] if nm in big else (sm_g, sm_d, sm_m, sm_v)[i][nm]
    loss = lax.psum(loss_acc[0, 0], ("x", "y", "c"))
    grad_x = dx2.reshape(x.shape)
    return (loss, grad_x, *[pick(nm, 0) for nm in order], *[pick(nm, 1) for nm in order],
            *[pick(nm, 2) for nm in order], *[pick(nm, 3) for nm in order])
```

```python
import functools

import jax
import jax.numpy as jnp
from jax import lax
from jax.experimental import pallas as pl
from jax.experimental.pallas import tpu as pltpu

F32 = jnp.float32
BF16 = jnp.bfloat16
HI = lax.Precision.HIGHEST

LANES = 128
SUB = 8
CHUNK = 64
N_DEV = 8
VMEM_LIMIT = 56 * 1024 * 1024

D_MODEL = 1024
RW_W = 512
GD_W = 512
RW_SHIFT = 1664
NORM_EPS = 1e-6
RW_GN_EPS = 64 * 1e-5
ADAM_LR, ADAM_B1, ADAM_B2, ADAM_EPS, ADAM_WD, ADAM_STEP = 0.001, 0.9, 0.999, 1e-8, 0.01, 10


_NN, _NT, _TN = ((1,), (0,)), ((1,), (1,)), ((0,), (0,))


def _dot3(a, b, dims):
    return lax.dot_general(a, b, (dims, ((), ())), precision=lax.Precision.HIGH, preferred_element_type=F32)


def _mm(a, b):
    return _dot3(a, b, _NN)


def _mm_nt(a, b):
    return _dot3(a, b, _NT)


def _mm_tn(a, b):
    return _dot3(a, b, _TN)


def _iota(shape, d):
    return lax.broadcasted_iota(jnp.int32, shape, d)


def _sigmoid(x):
    return 0.5 * (jnp.tanh(0.5 * x) + 1.0)


def _silu(x):
    return x * _sigmoid(x)


def _softplus(x):
    return jnp.maximum(x, 0.0) + jnp.log(1.0 + jnp.exp(-jnp.abs(x)))


def _seg_ones(seg):
    return ((_iota((LANES, LANES), 0) // seg) == (_iota((LANES, LANES), 1) // seg)).astype(F32)


def _sl(g):
    return slice(g * LANES, (g + 1) * LANES)


def _tri_inverse(ms):
    c = CHUNK
    ri, ci = _iota((c, c), 0), _iota((c, c), 1)
    eye = (ri == ci).astype(F32)
    d16 = (ri // 16) == (ci // 16)
    d32 = (ri // 32) == (ci // 32)
    ps = [jnp.where(d16, -m, 0.0) for m in ms]
    ts = [eye + p for p in ps]
    for _ in range(3):
        ps = [_mm(p, p) for p in ps]
        ts = [_mm(t, eye + p) for t, p in zip(ts, ps)]
    for off_diagonal in (d32 & (~d16), ~d32):
        tq = [_mm(t, jnp.where(off_diagonal, m, 0.0)) for t, m in zip(ts, ms)]
        ts = [t - _mm(a, t) for t, a in zip(ts, tq)]
    return ts


def _chunk_fwd(prims, *, nsub, scalar_decay):
    c = CHUNK
    ng = len(prims)
    s0s, rs, lws, ks, vs, kks, bs = [list(t) for t in zip(*prims)]
    ri, ci = _iota((c, c), 0), _iota((c, c), 1)
    incl = ri >= ci
    strict = ri > ci
    tril = incl.astype(F32)
    hs = LANES // nsub
    lane = _iota((1, LANES), 1)
    masks = [((lane // hs) == s).astype(F32) for s in range(nsub)]
    cws = [_mm(tril, lw) for lw in lws]
    cwxs = [cw - lw for cw, lw in zip(cws, lws)]
    ends = [cw[c - 1:c, :] for cw in cws]
    kkds = [kk * jnp.exp(cwx) for kk, cwx in zip(kks, cwxs)]
    rds = [r * jnp.exp(cw) for r, cw in zip(rs, cws)]
    kends = [k * jnp.exp(e - cw) for k, e, cw in zip(ks, ends, cws)]
    bends = [b * jnp.exp(e - cw) for b, e, cw in zip(bs, ends, cws)]
    w0s = [_mm_nt(kkd, s0) for kkd, s0 in zip(kkds, s0s)]
    y0s = [_mm_nt(rd, s0) for rd, s0 in zip(rds, s0s)]
    chains = [(g, s) for g in range(ng) for s in range(nsub)]
    if scalar_decay:
        e0 = (lane == 0).astype(F32) * jnp.ones((c, 1), F32)
        rows = [_mm_nt(e0, cw) for cw in cws]
        dxs = [jnp.where(strict, jnp.exp(jnp.minimum(cwx[:, :c] - row, 0.0)), 0.0) for cwx, row in zip(cwxs, rows)]
        dis = [jnp.where(incl, jnp.exp(jnp.minimum(cw[:, :c] - row, 0.0)), 0.0) for cw, row in zip(cws, rows)]
        m_b = [_mm_nt(kks[g] * masks[s], bs[g]) * dxs[g] for g, s in chains]
        m_k = [_mm_nt(kks[g] * masks[s], ks[g]) * dxs[g] for g, s in chains]
        n_k = [_mm_nt(rs[g] * masks[s], ks[g]) * dis[g] for g, s in chains]
        n_b = [_mm_nt(rs[g] * masks[s], bs[g]) * dis[g] for g, s in chains]
    else:
        kds = [k * jnp.exp(-cw) for k, cw in zip(ks, cws)]
        bds = [b * jnp.exp(-cw) for b, cw in zip(bs, cws)]
        m_b = [jnp.where(strict, _mm_nt(kkds[g] * masks[s], bds[g]), 0.0) for g, s in chains]
        m_k = [jnp.where(strict, _mm_nt(kkds[g] * masks[s], kds[g]), 0.0) for g, s in chains]
        n_k = [jnp.where(incl, _mm_nt(rds[g] * masks[s], kds[g]), 0.0) for g, s in chains]
        n_b = [jnp.where(incl, _mm_nt(rds[g] * masks[s], bds[g]), 0.0) for g, s in chains]
    t_inv = _tri_inverse(m_b)
    rhs = [w0s[g] + _mm(mk, vs[g]) for (g, s), mk in zip(chains, m_k)]
    sa_c = [_mm(t, x) for t, x in zip(t_inv, rhs)]
    y_c = [y0s[g] + _mm(nk, vs[g]) - _mm(nb, sa) for (g, s), nk, nb, sa in zip(chains, n_k, n_b, sa_c)]
    sas = [sum(sa_c[g * nsub + s] * masks[s] for s in range(nsub)) for g in range(ng)]
    ys = [sum(y_c[g * nsub + s] * masks[s] for s in range(nsub)) for g in range(ng)]
    s_ends = [s0 * jnp.exp(e) + _mm_tn(v, kend) - _mm_tn(sa, bend)
              for s0, e, v, kend, sa, bend in zip(s0s, ends, vs, kends, sas, bends)]
    if nsub > 1:
        same_head = (_iota((LANES, LANES), 0) // hs) == (_iota((LANES, LANES), 1) // hs)
        s_ends = [jnp.where(same_head, s_end, 0.0) for s_end in s_ends]
    return list(zip(ys, s_ends))


def _rec_fwd(name, r, lw, k, v, kk, b, *, seq, nsub, scalar_decay):
    n, w = r.shape
    ng = w // LANES
    nc = seq // CHUNK
    nb = n // seq

    def body(r_ref, lw_ref, k_ref, v_ref, kk_ref, b_ref, y_ref, s_ref, state):
        @pl.when(pl.program_id(1) == 0)
        def _():
            state[...] = jnp.zeros_like(state)
        prims = [(state[g], r_ref[:, _sl(g)], lw_ref[:, _sl(g)], k_ref[:, _sl(g)], v_ref[:, _sl(g)],
                  kk_ref[:, _sl(g)], b_ref[:, _sl(g)]) for g in range(ng)]
        outs = _chunk_fwd(prims, nsub=nsub, scalar_decay=scalar_decay)
        for g, (y, s_end) in enumerate(outs):
            s_ref[0, g] = prims[g][0]
            y_ref[:, _sl(g)] = y
            state[g] = s_end

    row = pl.BlockSpec((CHUNK, w), lambda bi, c: (bi * nc + c, 0))
    return pl.pallas_call(
        body, name=name, grid=(nb, nc),
        in_specs=[row] * 6,
        out_specs=[row, pl.BlockSpec((1, ng, LANES, LANES), lambda bi, c: (bi * nc + c, 0, 0, 0))],
        out_shape=[jax.ShapeDtypeStruct((n, w), F32), jax.ShapeDtypeStruct((nb * nc, ng, LANES, LANES), F32)],
        scratch_shapes=[pltpu.VMEM((ng, LANES, LANES), F32)],
        compiler_params=pltpu.CompilerParams(dimension_semantics=("parallel", "arbitrary"), vmem_limit_bytes=VMEM_LIMIT),
    )(r, lw, k, v, kk, b)


def _rec_bwd(name, r, lw, k, v, kk, b, s_save, dy, *, seq, nsub, scalar_decay):
    n, w = r.shape
    ng = w // LANES
    nc = seq // CHUNK
    nb = n // seq

    def body(r_ref, lw_ref, k_ref, v_ref, kk_ref, b_ref, s_ref, dy_ref,
             dr_ref, dlw_ref, dk_ref, dv_ref, dkk_ref, db_ref, dstate):
        @pl.when(pl.program_id(1) == 0)
        def _():
            dstate[...] = jnp.zeros_like(dstate)
        f = functools.partial(_chunk_fwd, nsub=nsub, scalar_decay=scalar_decay)
        prims = [(s_ref[0, g], r_ref[:, _sl(g)], lw_ref[:, _sl(g)], k_ref[:, _sl(g)], v_ref[:, _sl(g)],
                  kk_ref[:, _sl(g)], b_ref[:, _sl(g)]) for g in range(ng)]
        _, vjp = jax.vjp(f, prims)
        (d_prims,) = vjp([(dy_ref[:, _sl(g)], dstate[g]) for g in range(ng)])
        for g, (ds0, dr, dlw, dk, dv, dkk, db) in enumerate(d_prims):
            dstate[g] = ds0
            dr_ref[:, _sl(g)] = dr
            dlw_ref[:, _sl(g)] = dlw
            dk_ref[:, _sl(g)] = dk
            dv_ref[:, _sl(g)] = dv
            dkk_ref[:, _sl(g)] = dkk
            db_ref[:, _sl(g)] = db

    row = pl.BlockSpec((CHUNK, w), lambda bi, c: (bi * nc + nc - 1 - c, 0))
    return pl.pallas_call(
        body, name=name, grid=(nb, nc),
        in_specs=[row] * 6 + [pl.BlockSpec((1, ng, LANES, LANES), lambda bi, c: (bi * nc + nc - 1 - c, 0, 0, 0)), row],
        out_specs=[row] * 6,
        out_shape=[jax.ShapeDtypeStruct((n, w), F32)] * 6,
        scratch_shapes=[pltpu.VMEM((ng, LANES, LANES), F32)],
        compiler_params=pltpu.CompilerParams(dimension_semantics=("parallel", "arbitrary"), vmem_limit_bytes=VMEM_LIMIT),
    )(r, lw, k, v, kk, b, s_save, dy)


def _shift_down(a, j, halo, is_start):
    tb = a.shape[0]
    rolled = pltpu.roll(a, j, 0)
    hr = jnp.where(is_start, 0.0, pltpu.roll(halo, j, 0))
    first = jnp.where(_iota((SUB, LANES), 0) < j, hr, rolled[0:SUB])
    if tb == SUB:
        return first
    return jnp.concatenate([first, rolled[SUB:]], axis=0)


def _shift_up(d, j, carry, is_end):
    tb = d.shape[0]
    up = pltpu.roll(d, tb - j, 0)
    cr = jnp.where(is_end, 0.0, pltpu.roll(carry, SUB - j, 0))
    last = jnp.where(_iota((SUB, LANES), 0) >= SUB - j, cr, up[tb - SUB:tb])
    if tb == SUB:
        return last
    return jnp.concatenate([up[:tb - SUB], last], axis=0)


def _ngroups(a):
    return a.shape[1] // LANES


def _pw_fwd(name, f, ins, shift, params, out_widths, out_dtypes, *, seq, tb):
    n = ins[0].shape[0]
    nt, tps = n // tb, seq // tb
    ni, npar = len(ins), len(params)

    def body(*refs):
        in_refs = refs[:ni]
        pos = ni
        halo_ref = None
        if shift:
            halo_ref = refs[pos]
            pos += 1
        p_refs = refs[pos:pos + npar]
        out_refs = refs[pos + npar:]
        is_start = (pl.program_id(0) % tps) == 0
        tiles = [[ref[:, _sl(g)] for g in range(_ngroups(ref))] for ref in in_refs]
        prevs = [[_shift_down(tiles[0][g], j, halo_ref[:, _sl(g)], is_start) for g in range(len(tiles[0]))]
                 for j in range(1, shift + 1)]
        pv = [[ref[:, _sl(g)] for g in range(_ngroups(ref))] for ref in p_refs]
        outs = f(tiles, prevs, pv)
        for o_ref, og in zip(out_refs, outs, strict=True):
            for g, t in enumerate(og):
                o_ref[:, _sl(g)] = t.astype(o_ref.dtype)

    in_specs = [pl.BlockSpec((tb, a.shape[1]), lambda i: (i, 0)) for a in ins]
    args = list(ins)
    if shift:
        in_specs.append(pl.BlockSpec((SUB, ins[0].shape[1]), lambda i: (jnp.maximum(i * (tb // SUB) - 1, 0), 0)))
        args.append(ins[0])
    in_specs += [pl.BlockSpec(p.shape, lambda i: (0, 0)) for p in params]
    args += list(params)
    return pl.pallas_call(
        body, name=name, grid=(nt,),
        in_specs=in_specs,
        out_specs=[pl.BlockSpec((tb, w), lambda i: (i, 0)) for w in out_widths],
        out_shape=[jax.ShapeDtypeStruct((n, w), dt) for w, dt in zip(out_widths, out_dtypes, strict=True)],
        compiler_params=pltpu.CompilerParams(dimension_semantics=("parallel",), vmem_limit_bytes=VMEM_LIMIT),
    )(*args)


def _pw_bwd(name, f, ins, shift, params, douts, din_dtypes, *, seq, tb):
    n = ins[0].shape[0]
    nt, tps = n // tb, seq // tb
    ni, npar = len(ins), len(params)
    flat_douts = [d for ds in douts for d in ds]
    nd = len(flat_douts)
    w0 = ins[0].shape[1]

    def body(*refs):
        in_refs = refs[:ni]
        pos = ni
        halo_ref = None
        if shift:
            halo_ref = refs[pos]
            pos += 1
        p_refs = refs[pos:pos + npar]
        pos += npar
        d_refs = refs[pos:pos + nd]
        pos += nd
        din_refs = refs[pos:pos + ni]
        pos += ni
        dp_refs = refs[pos:pos + npar]
        pos += npar
        carry = refs[pos] if shift else None
        step = pl.program_id(0)
        tile = nt - 1 - step
        is_start = (tile % tps) == 0
        is_end = (tile % tps) == tps - 1
        tiles = [[ref[:, _sl(g)] for g in range(_ngroups(ref))] for ref in in_refs]
        prevs = [[_shift_down(tiles[0][g], j, halo_ref[:, _sl(g)], is_start) for g in range(len(tiles[0]))]
                 for j in range(1, shift + 1)]
        pv = [[ref[:, _sl(g)] for g in range(_ngroups(ref))] for ref in p_refs]
        cot, pos_d = [], 0
        for ds in douts:
            grp = d_refs[pos_d:pos_d + len(ds)]
            pos_d += len(ds)
            cot.append([sum(ref[:, _sl(g)].astype(F32) for ref in grp) for g in range(_ngroups(grp[0]))])
        _, vjp = jax.vjp(f, tiles, prevs, pv)
        d_tiles, d_prevs, d_pv = vjp(cot)
        for g in range(len(tiles[0])):
            for j in range(1, shift + 1):
                d_tiles[0][g] = d_tiles[0][g] + _shift_up(d_prevs[j - 1][g], j, carry[j - 1, :, _sl(g)], is_end)
            for j in range(1, shift + 1):
                carry[j - 1, :, _sl(g)] = d_prevs[j - 1][g][0:SUB]
        for ref, dg in zip(din_refs, d_tiles, strict=True):
            for g, t in enumerate(dg):
                ref[:, _sl(g)] = t.astype(ref.dtype)

        @pl.when(step == 0)
        def _():
            for ref in dp_refs:
                ref[...] = jnp.zeros_like(ref)
        for ref, dg in zip(dp_refs, d_pv, strict=True):
            for g, t in enumerate(dg):
                ref[:, _sl(g)] += t

    rev = lambda i: (nt - 1 - i, 0)
    in_specs = [pl.BlockSpec((tb, a.shape[1]), rev) for a in ins]
    args = list(ins)
    if shift:
        in_specs.append(pl.BlockSpec((SUB, w0), lambda i: (jnp.maximum((nt - 1 - i) * (tb // SUB) - 1, 0), 0)))
        args.append(ins[0])
    in_specs += [pl.BlockSpec(p.shape, lambda i: (0, 0)) for p in params]
    args += list(params)
    in_specs += [pl.BlockSpec((tb, d.shape[1]), rev) for d in flat_douts]
    args += flat_douts
    out_specs = [pl.BlockSpec((tb, a.shape[1]), rev) for a in ins] + [pl.BlockSpec(p.shape, lambda i: (0, 0)) for p in params]
    out_shape = ([jax.ShapeDtypeStruct(a.shape, dt) for a, dt in zip(ins, din_dtypes, strict=True)]
                 + [jax.ShapeDtypeStruct(p.shape, F32) for p in params])
    res = pl.pallas_call(
        body, name=name, grid=(nt,),
        in_specs=in_specs, out_specs=out_specs, out_shape=out_shape,
        scratch_shapes=[pltpu.VMEM((shift, SUB, w0), F32)] if shift else [],
        compiler_params=pltpu.CompilerParams(dimension_semantics=("arbitrary",), vmem_limit_bytes=VMEM_LIMIT),
    )(*args)
    return res[:ni], res[ni:]


def _rwkv_prep_f(tiles, prevs, params):
    (p,), (prev,) = tiles, prevs
    mu, w0, w2p, a0, a2p, k_k, k_a = params
    xs = [p[g] + (prev[g] - p[g]) * mu[g] for g in range(13)]
    wdad = xs[12]
    tw = jnp.tanh(wdad)
    e64 = _seg_ones(64)
    r, lw, k2, v, kk, b = [], [], [], [], [], []
    for g in range(4):
        k_g = xs[4 + g]
        lo = w0[g] + _mm(tw, w2p[g])
        lw_g = -jnp.exp(-_softplus(-lo) - 0.5)
        a_g = _sigmoid(a0[g] + _mm(wdad, a2p[g]))
        kkp = k_g * k_k[g]
        kk_g = kkp * lax.rsqrt(_mm(kkp * kkp, e64) + 1e-12)
        r.append(xs[g])
        lw.append(lw_g)
        k2.append(k_g * (1.0 + (a_g - 1.0) * k_a[g]))
        v.append(xs[8 + g])
        kk.append(kk_g)
        b.append(kk_g * a_g)
    return [r, lw, k2, v, kk, b]


def _rwkv_post_f(tiles, prevs, params):
    yrec, r, k2, v, z = tiles
    gn_w, gn_b, r_k = params
    e64 = _seg_ones(64)
    out = []
    for g in range(4):
        mean = _mm(yrec[g], e64) * (1.0 / 64)
        d = yrec[g] - mean
        var = _mm(d * d, e64) * (1.0 / 64)
        yn = d * lax.rsqrt(var + RW_GN_EPS) * gn_w[g] + gn_b[g]
        bonus = _mm(r[g] * k2[g] * r_k[g], e64) * v[g]
        out.append((yn + bonus) * _silu(z[g]))
    return [out]


def _gdn_prep_f(tiles, prevs, params):
    x, (ba,) = tiles
    p1, p2, p3 = prevs
    cw0, cw1, cw2, cw3, a_log, dt_bias = params
    s = [_silu(cw3[g] * x[g] + cw2[g] * p1[g] + cw1[g] * p2[g] + cw0[g] * p3[g]) for g in range(12)]
    row = _iota((LANES, LANES), 0)
    r, lw, k, vv, b = [], [], [], [], []
    for h in range(4):
        q_h, k_h, v_h = s[h], s[4 + h], s[8 + h]
        qn = q_h * lax.rsqrt(jnp.sum(q_h * q_h, axis=-1, keepdims=True) + 1e-12)
        kn = k_h * lax.rsqrt(jnp.sum(k_h * k_h, axis=-1, keepdims=True) + 1e-12)
        beta = _sigmoid(_mm(ba, (row == h).astype(F32)))
        alpha = _mm(ba, (row == 4 + h).astype(F32))
        g_h = -jnp.exp(a_log[h]) * _softplus(alpha + dt_bias[h])
        r.append(qn * (LANES ** -0.5))
        lw.append(g_h)
        k.append(kn)
        vv.append(beta * v_h)
        b.append(jnp.exp(g_h) * beta * kn)
    return [r, lw, k, vv, b]


def _gdn_post_f(tiles, prevs, params):
    o, z = tiles
    ((onw,),) = params
    out = []
    for h in range(4):
        ms = jnp.mean(o[h] * o[h], axis=-1, keepdims=True)
        out.append(o[h] * lax.rsqrt(ms + NORM_EPS) * onw * _silu(z[h]))
    return [out]


def _norm_in(x2, g_in, *, tm):
    n = x2.shape[0]

    def body(x_ref, g_ref, h_ref):
        x = x_ref[...]
        rs = lax.rsqrt(jnp.mean(x * x, axis=-1, keepdims=True) + NORM_EPS)
        h_ref[...] = (x * rs * g_ref[...]).astype(BF16)

    return pl.pallas_call(
        body, name="norm_in", grid=(n // tm,),
        in_specs=[pl.BlockSpec((tm, D_MODEL), lambda i: (i, 0)), pl.BlockSpec((1, D_MODEL), lambda i: (0, 0))],
        out_specs=pl.BlockSpec((tm, D_MODEL), lambda i: (i, 0)),
        out_shape=jax.ShapeDtypeStruct((n, D_MODEL), BF16),
        compiler_params=pltpu.CompilerParams(dimension_semantics=("parallel",), vmem_limit_bytes=VMEM_LIMIT),
    )(x2, g_in)


def _proj(name, h, w, *, tm):
    n, ws = h.shape[0], w.shape[1]

    def body(h_ref, w_ref, o_ref):
        o_ref[...] = jnp.dot(h_ref[...], w_ref[...], preferred_element_type=F32)

    return pl.pallas_call(
        body, name=name, grid=(n // tm,),
        in_specs=[pl.BlockSpec((tm, D_MODEL), lambda i: (i, 0)), pl.BlockSpec((D_MODEL, ws), lambda i: (0, 0))],
        out_specs=pl.BlockSpec((tm, ws), lambda i: (i, 0)),
        out_shape=jax.ShapeDtypeStruct((n, ws), F32),
        compiler_params=pltpu.CompilerParams(dimension_semantics=("parallel",), vmem_limit_bytes=VMEM_LIMIT),
    )(h, w)


def _proj_dw(name, h, dp, *, tm):
    n, ws = dp.shape

    def body(h_ref, d_ref, o_ref):
        @pl.when(pl.program_id(0) == 0)
        def _():
            o_ref[...] = jnp.zeros_like(o_ref)
        o_ref[...] += lax.dot_general(h_ref[...], d_ref[...], (((0,), (0,)), ((), ())), preferred_element_type=F32)

    return pl.pallas_call(
        body, name=name, grid=(n // tm,),
        in_specs=[pl.BlockSpec((tm, D_MODEL), lambda i: (i, 0)), pl.BlockSpec((tm, ws), lambda i: (i, 0))],
        out_specs=pl.BlockSpec((D_MODEL, ws), lambda i: (0, 0)),
        out_shape=jax.ShapeDtypeStruct((D_MODEL, ws), F32),
        compiler_params=pltpu.CompilerParams(dimension_semantics=("arbitrary",), vmem_limit_bytes=VMEM_LIMIT),
    )(h, dp)


def _proj_dx(x2, g_in, d_xo, dps, ws, *, tm):
    n = x2.shape[0]
    ns = len(dps)

    def body(*refs):
        x_ref, g_ref, dxo_ref = refs[:3]
        dp_refs = refs[3:3 + ns]
        w_refs = refs[3 + ns:3 + 2 * ns]
        dx_ref, dg_ref = refs[3 + 2 * ns:]
        dh = jnp.zeros((tm, D_MODEL), F32)
        for d_ref, w_ref in zip(dp_refs, w_refs, strict=True):
            dh = dh + lax.dot_general(d_ref[...], w_ref[...], (((1,), (1,)), ((), ())), preferred_element_type=F32)
        x = x_ref[...]
        rs = lax.rsqrt(jnp.mean(x * x, axis=-1, keepdims=True) + NORM_EPS)
        xn = x * rs
        dxn = dh * g_ref[...]
        dx_ref[...] = dxo_ref[...] + rs * (dxn - xn * jnp.mean(dxn * xn, axis=-1, keepdims=True))

        @pl.when(pl.program_id(0) == 0)
        def _():
            dg_ref[...] = jnp.zeros_like(dg_ref)
        dg_ref[...] += jnp.sum(dh * xn, axis=0, keepdims=True)

    row = pl.BlockSpec((tm, D_MODEL), lambda i: (i, 0))
    return pl.pallas_call(
        body, name="proj_dx", grid=(n // tm,),
        in_specs=([row, pl.BlockSpec((1, D_MODEL), lambda i: (0, 0)), row]
                  + [pl.BlockSpec((tm, d.shape[1]), lambda i: (i, 0)) for d in dps]
                  + [pl.BlockSpec(w.shape, lambda i: (0, 0)) for w in ws]),
        out_specs=[row, pl.BlockSpec((1, D_MODEL), lambda i: (0, 0))],
        out_shape=[jax.ShapeDtypeStruct((n, D_MODEL), F32), jax.ShapeDtypeStruct((1, D_MODEL), F32)],
        compiler_params=pltpu.CompilerParams(dimension_semantics=("arbitrary",), vmem_limit_bytes=VMEM_LIMIT),
    )(x2, g_in, d_xo, *dps, *ws)


def _tail(x2, tgt2, gates, ya, yb, w_a, w_b, w_o, now, *, tr):
    n = x2.shape[0]

    def body(x_ref, t_ref, g_ref, ya_ref, yb_ref, wa_ref, wb_ref, wo_ref, now_ref,
             dya_ref, dyb_ref, dg_ref, dxo_ref, dwa_ref, dwb_ref, dwo_ref, dnow_ref, loss_ref):
        ya16, yb16 = ya_ref[...].astype(BF16), yb_ref[...].astype(BF16)
        ua = jnp.dot(ya16, wa_ref[...], preferred_element_type=F32)
        ub = jnp.dot(yb16, wb_ref[...], preferred_element_type=F32)
        ga = _sigmoid(g_ref[:, :D_MODEL])
        gb = _sigmoid(g_ref[:, D_MODEL:])
        m16 = (ga * ua + gb * ub).astype(BF16)
        xo = x_ref[...] + jnp.dot(m16, wo_ref[...], preferred_element_type=F32)
        rs = lax.rsqrt(jnp.mean(xo * xo, axis=-1, keepdims=True) + NORM_EPS)
        yn = xo * rs
        now_v = now_ref[...]
        err = yn * now_v - t_ref[...]
        dy = err * (1.0 / D_MODEL)
        dyn = dy * now_v
        dxo = rs * (dyn - yn * jnp.mean(dyn * yn, axis=-1, keepdims=True))
        dxo_ref[...] = dxo
        dxo16 = dxo.astype(BF16)
        dm = lax.dot_general(dxo16, wo_ref[...], (((1,), (1,)), ((), ())), preferred_element_type=F32)
        dua16 = (dm * ga).astype(BF16)
        dub16 = (dm * gb).astype(BF16)
        dg_ref[:, :D_MODEL] = (dm * ua * ga * (1.0 - ga)).astype(dg_ref.dtype)
        dg_ref[:, D_MODEL:] = (dm * ub * gb * (1.0 - gb)).astype(dg_ref.dtype)
        dya_ref[...] = lax.dot_general(dua16, wa_ref[...], (((1,), (1,)), ((), ())), preferred_element_type=F32)
        dyb_ref[...] = lax.dot_general(dub16, wb_ref[...], (((1,), (1,)), ((), ())), preferred_element_type=F32)

        @pl.when(pl.program_id(0) == 0)
        def _():
            for ref in (dwa_ref, dwb_ref, dwo_ref, dnow_ref, loss_ref):
                ref[...] = jnp.zeros_like(ref)
        tn = (((0,), (0,)), ((), ()))
        dwo_ref[...] += lax.dot_general(m16, dxo16, tn, preferred_element_type=F32)
        dwa_ref[...] += lax.dot_general(ya16, dua16, tn, preferred_element_type=F32)
        dwb_ref[...] += lax.dot_general(yb16, dub16, tn, preferred_element_type=F32)
        dnow_ref[...] += jnp.sum(dy * yn, axis=0, keepdims=True)
        loss_ref[...] += (0.5 / D_MODEL) * jnp.sum(err * err)

    row = lambda w: pl.BlockSpec((tr, w), lambda i: (i, 0))
    full = lambda a: pl.BlockSpec(a.shape, lambda i: (0, 0))
    return pl.pallas_call(
        body, name="tail", grid=(n // tr,),
        in_specs=[row(D_MODEL), row(D_MODEL), row(2 * D_MODEL), row(RW_W), row(GD_W), full(w_a), full(w_b), full(w_o), full(now)],
        out_specs=[row(RW_W), row(GD_W), row(2 * D_MODEL), row(D_MODEL),
                   pl.BlockSpec((RW_W, D_MODEL), lambda i: (0, 0)), pl.BlockSpec((GD_W, D_MODEL), lambda i: (0, 0)),
                   pl.BlockSpec((D_MODEL, D_MODEL), lambda i: (0, 0)), pl.BlockSpec((1, D_MODEL), lambda i: (0, 0)),
                   pl.BlockSpec((SUB, LANES), lambda i: (0, 0))],
        out_shape=[jax.ShapeDtypeStruct((n, RW_W), F32), jax.ShapeDtypeStruct((n, GD_W), F32),
                   jax.ShapeDtypeStruct((n, 2 * D_MODEL), BF16), jax.ShapeDtypeStruct((n, D_MODEL), F32),
                   jax.ShapeDtypeStruct((RW_W, D_MODEL), F32), jax.ShapeDtypeStruct((GD_W, D_MODEL), F32),
                   jax.ShapeDtypeStruct((D_MODEL, D_MODEL), F32), jax.ShapeDtypeStruct((1, D_MODEL), F32),
                   jax.ShapeDtypeStruct((SUB, LANES), F32)],
        compiler_params=pltpu.CompilerParams(dimension_semantics=("arbitrary",), vmem_limit_bytes=VMEM_LIMIT),
    )(x2, tgt2, gates, ya, yb, w_a, w_b, w_o, now)


def _exchange(name, scatter, gather):
    ns, ng = len(scatter), len(gather)
    na = ns + ng

    def body(*refs):
        src = refs[:na]
        dst = refs[na:2 * na]
        send_sems, recv_sems, local_sems = refs[2 * na:]
        x, y, c = lax.axis_index("x"), lax.axis_index("y"), lax.axis_index("c")
        me = 4 * x + 2 * y + c

        def peer(k):
            px = 1 - x if k & 4 else x
            py = 1 - y if k & 2 else y
            pc = 1 - c if k & 1 else c
            return (px, py, pc), 4 * px + 2 * py + pc

        def copy(a, k):
            dev, idx = peer(k)
            s = src[a].at[idx] if a < ns else src[a]
            return pltpu.make_async_remote_copy(src_ref=s, dst_ref=dst[a].at[me], send_sem=send_sems.at[a, k - 1],
                                                recv_sem=recv_sems.at[a, k - 1], device_id=dev,
                                                device_id_type=pl.DeviceIdType.MESH)

        def arrival(a, k):
            dev, idx = peer(k)
            s = src[a].at[idx] if a < ns else src[a]
            return pltpu.make_async_remote_copy(src_ref=s, dst_ref=dst[a].at[idx], send_sem=send_sems.at[a, k - 1],
                                                recv_sem=recv_sems.at[a, k - 1], device_id=dev,
                                                device_id_type=pl.DeviceIdType.MESH)

        local = [pltpu.make_async_copy(src[a].at[me] if a < ns else src[a], dst[a].at[me], local_sems.at[a])
                 for a in range(na)]
        sends = [copy(a, k) for a in range(na) for k in range(1, N_DEV)]
        for cp in local + sends:
            cp.start()
        for a in range(na):
            for k in range(1, N_DEV):
                arrival(a, k).wait_recv()
        for cp in sends:
            cp.wait_send()
        for cp in local:
            cp.wait()

    arrs = list(scatter) + list(gather)
    out_shape = [jax.ShapeDtypeStruct(a.shape, a.dtype) for a in scatter] + \
                [jax.ShapeDtypeStruct((N_DEV,) + a.shape, a.dtype) for a in gather]
    anyspec = pl.BlockSpec(memory_space=pl.ANY)
    return pl.pallas_call(
        body, name=name,
        in_specs=[anyspec] * na, out_specs=[anyspec] * na, out_shape=out_shape,
        scratch_shapes=[pltpu.SemaphoreType.DMA((na, N_DEV - 1)), pltpu.SemaphoreType.DMA((na, N_DEV - 1)),
                        pltpu.SemaphoreType.DMA((na,))],
    )(*arrs)


def _adam(name, land, w, m, v):
    r, c = w.shape
    tr = 256 if (r % 256 == 0 and r > 256) else r
    c1 = 1.0 / (1.0 - ADAM_B1 ** ADAM_STEP)
    c2 = 1.0 / (1.0 - ADAM_B2 ** ADAM_STEP)

    def body(l_ref, w_ref, m_ref, v_ref, g_out, d_out, m_out, v_out):
        g = l_ref[0]
        for s in range(1, N_DEV):
            g = g + l_ref[s]
        m_new = ADAM_B1 * m_ref[...] + (1.0 - ADAM_B1) * g
        v_new = ADAM_B2 * v_ref[...] + (1.0 - ADAM_B2) * (g * g)
        g_out[...] = g
        m_out[...] = m_new
        v_out[...] = v_new
        d_out[...] = -ADAM_LR * ((m_new * c1) / (jnp.sqrt(v_new * c2) + ADAM_EPS) + ADAM_WD * w_ref[...])

    blk = pl.BlockSpec((tr, c), lambda i: (i, 0))
    return pl.pallas_call(
        body, name=name, grid=(r // tr,),
        in_specs=[pl.BlockSpec((N_DEV, tr, c), lambda i: (0, i, 0)), blk, blk, blk],
        out_specs=[blk] * 4,
        out_shape=[jax.ShapeDtypeStruct((r, c), F32)] * 4,
        compiler_params=pltpu.CompilerParams(dimension_semantics=("parallel",), vmem_limit_bytes=VMEM_LIMIT),
    )(land, w, m, v)


_SMALL = (("norm_in_w", 1024), ("rw_mu", 1664), ("rw_w0", 512), ("rw_a0", 512), ("rw_k_k", 512), ("rw_k_a", 512),
          ("rw_r_k", 512), ("rw_gn_w", 512), ("rw_gn_b", 512), ("gd_A_log", 4), ("gd_dt_bias", 4), ("gd_o_norm_w", 128),
          ("norm_out_w", 1024))
_SMALL_ROWS = 64


def _pack_small(vals):
    rows = []
    for (_, size), a in zip(_SMALL, vals, strict=True):
        flat = a.reshape(-1).astype(F32)
        pad = (-size) % LANES
        if pad:
            flat = jnp.concatenate([flat, jnp.zeros((pad,), F32)])
        rows.append(flat.reshape(-1, LANES))
    used = sum(r.shape[0] for r in rows)
    rows.append(jnp.zeros((_SMALL_ROWS - used, LANES), F32))
    return jnp.concatenate(rows, axis=0)


def _unpack_small(packed, shapes):
    out, row = [], 0
    for (_, size), shp in zip(_SMALL, shapes, strict=True):
        nrow = -(-size // LANES)
        out.append(packed[row:row + nrow].reshape(-1)[:size].reshape(shp))
        row += nrow
    return out


def kernel(x, norm_in_w, w_in, rw_mu, rw_w0, rw_w2, rw_a0, rw_a2, rw_k_k, rw_k_a, rw_r_k, rw_gn_w, rw_gn_b, gd_conv_w, gd_A_log, gd_dt_bias, gd_o_norm_w, w_branch_a, w_branch_b, w_out, norm_out_w, loss_target, m_norm_in_w, m_w_in, m_rw_mu, m_rw_w0, m_rw_w2, m_rw_a0, m_rw_a2, m_rw_k_k, m_rw_k_a, m_rw_r_k, m_rw_gn_w, m_rw_gn_b, m_gd_conv_w, m_gd_A_log, m_gd_dt_bias, m_gd_o_norm_w, m_w_branch_a, m_w_branch_b, m_w_out, m_norm_out_w, v_norm_in_w, v_w_in, v_rw_mu, v_rw_w0, v_rw_w2, v_rw_a0, v_rw_a2, v_rw_k_k, v_rw_k_a, v_rw_r_k, v_rw_gn_w, v_rw_gn_b, v_gd_conv_w, v_gd_A_log, v_gd_dt_bias, v_gd_o_norm_w, v_w_branch_a, v_w_branch_b, v_w_out, v_norm_out_w):
    nb, seq, _ = x.shape
    n = nb * seq
    tm = min(512, n)
    tb = min(256, seq)
    x2 = x.reshape(n, D_MODEL)
    tgt2 = loss_target.reshape(n, D_MODEL)
    cols = w_in.shape[2]
    in_cols = cols * N_DEV

    sharded = [w_in[0], rw_w2[0], rw_a2[0], gd_conv_w[0], w_branch_a[0], w_branch_b[0], w_out[0]]
    g_win, g_w2, g_a2, g_conv, g_wa, g_wb, g_wo = _exchange("gather_weights", [], sharded)
    unshard_cols = lambda a: jnp.transpose(a, (1, 0, 2)).reshape(a.shape[1], N_DEV * a.shape[2])
    w_full = unshard_cols(g_win).astype(BF16)
    seg_bounds = ((0, 1664), (1664, 2176), (2176, 3712), (3712, 4224), (4232, in_cols))
    w_rw, w_zrw, w_qkv, w_zgd, w_gates = [w_full[:, a:b] for a, b in seg_bounds]
    w_ba = jnp.concatenate([w_full[:, 4224:4232], jnp.zeros((D_MODEL, LANES - 8), BF16)], axis=1)
    w2_full, a2_full = unshard_cols(g_w2), unshard_cols(g_a2)
    zeros64 = jnp.zeros((64, RW_W), F32)
    w2p = jnp.concatenate([w2_full, zeros64], axis=0)
    a2p = jnp.concatenate([zeros64, a2_full], axis=0)
    conv_full = unshard_cols(g_conv)
    conv_rows = [conv_full[i:i + 1] for i in range(4)]
    wa_full = unshard_cols(g_wa).astype(BF16)
    wb_full = unshard_cols(g_wb).astype(BF16)
    wo_full = g_wo.reshape(D_MODEL, D_MODEL).astype(BF16)
    a_log_bc = jnp.repeat(gd_A_log, LANES, axis=1)
    dt_bias_bc = jnp.repeat(gd_dt_bias, LANES, axis=1)
    r_k_flat = rw_r_k.reshape(1, RW_W)
    now2 = norm_out_w.reshape(1, D_MODEL)

    h = _norm_in(x2, norm_in_w, tm=tm)
    p_rw = _proj("proj_rw", h, w_rw, tm=tm)
    p_zrw = _proj("proj_zrw", h, w_zrw, tm=tm)
    p_qkv = _proj("proj_qkv", h, w_qkv, tm=tm)
    p_zgd = _proj("proj_zgd", h, w_zgd, tm=tm)
    p_ba = _proj("proj_ba", h, w_ba, tm=tm)
    p_gates = _proj("proj_gates", h, w_gates, tm=tm)

    rw_params = [rw_mu, rw_w0, w2p, rw_a0, a2p, rw_k_k, rw_k_a]
    r_a, lw_a, k_a, v_a, kk_a, b_a = _pw_fwd("rwkv_prep", _rwkv_prep_f, [p_rw], 1, rw_params, [RW_W] * 6, [F32] * 6,
                                             seq=seq, tb=tb)
    y_rec, s_a = _rec_fwd("rwkv_rec", r_a, lw_a, k_a, v_a, kk_a, b_a, seq=seq, nsub=2, scalar_decay=False)
    post_params = [rw_gn_w, rw_gn_b, r_k_flat]
    (y_a,) = _pw_fwd("rwkv_post", _rwkv_post_f, [y_rec, r_a, k_a, v_a, p_zrw], 0, post_params, [RW_W], [F32], seq=seq, tb=tb)

    gd_params = conv_rows + [a_log_bc, dt_bias_bc]
    r_b, lw_b, k_b, v_b, b_b = _pw_fwd("gdn_prep", _gdn_prep_f, [p_qkv, p_ba], 3, gd_params, [GD_W] * 5, [F32] * 5,
                                       seq=seq, tb=tb)
    o_rec, s_b = _rec_fwd("gdn_rec", r_b, lw_b, k_b, v_b, k_b, b_b, seq=seq, nsub=1, scalar_decay=True)
    (y_b,) = _pw_fwd("gdn_post", _gdn_post_f, [o_rec, p_zgd], 0, [gd_o_norm_w], [GD_W], [F32], seq=seq, tb=tb)

    d_ya, d_yb, d_gates, d_xo, dwa, dwb, dwo, d_now, loss_acc = _tail(
        x2, tgt2, p_gates, y_a, y_b, wa_full, wb_full, wo_full, now2, tr=min(256, n))

    (d_o, d_zgd), (d_onw,) = _pw_bwd("gdn_post_bwd", _gdn_post_f, [o_rec, p_zgd], 0, [gd_o_norm_w], [[d_yb]],
                                     [F32, BF16], seq=seq, tb=tb)
    dr_b, dlw_b, dk_b, dv_b, dkk_b, db_b = _rec_bwd("gdn_rec_bwd", r_b, lw_b, k_b, v_b, k_b, b_b, s_b, d_o,
                                                    seq=seq, nsub=1, scalar_decay=True)
    (d_qkv, d_ba), d_gd_params = _pw_bwd("gdn_prep_bwd", _gdn_prep_f, [p_qkv, p_ba], 3, gd_params,
                                         [[dr_b], [dlw_b], [dk_b, dkk_b], [dv_b], [db_b]], [BF16, BF16], seq=seq, tb=tb)

    (d_yrec, dr_p, dk_p, dv_p, d_zrw), d_post_params = _pw_bwd(
        "rwkv_post_bwd", _rwkv_post_f, [y_rec, r_a, k_a, v_a, p_zrw], 0, post_params, [[d_ya]],
        [F32, F32, F32, F32, BF16], seq=seq, tb=tb)
    dr_a, dlw_a, dk_a, dv_a, dkk_a, db_a = _rec_bwd("rwkv_rec_bwd", r_a, lw_a, k_a, v_a, kk_a, b_a, s_a, d_yrec,
                                                    seq=seq, nsub=2, scalar_decay=False)
    (d_prw,), d_rw_params = _pw_bwd("rwkv_prep_bwd", _rwkv_prep_f, [p_rw], 1, rw_params,
                                    [[dr_a, dr_p], [dlw_a], [dk_a, dk_p], [dv_a, dv_p], [dkk_a], [db_a]], [BF16],
                                    seq=seq, tb=tb)

    dps = [d_prw, d_zrw, d_qkv, d_zgd, d_ba, d_gates]
    wsegs = [w_rw, w_zrw, w_qkv, w_zgd, w_ba, w_gates]
    dx2, d_gin = _proj_dx(x2, norm_in_w, d_xo, dps, wsegs, tm=min(256, n))
    dw_rw = _proj_dw("dw_rw", h, d_prw, tm=tm)
    dw_zrw = _proj_dw("dw_zrw", h, d_zrw, tm=tm)
    dw_qkv = _proj_dw("dw_qkv", h, d_qkv, tm=tm)
    dw_zgd = _proj_dw("dw_zgd", h, d_zgd, tm=tm)
    dw_ba = _proj_dw("dw_ba", h, d_ba, tm=tm)
    dw_gates = _proj_dw("dw_gates", h, d_gates, tm=tm)
    dw_in_full = jnp.concatenate([dw_rw, dw_zrw, dw_qkv, dw_zgd, dw_ba[:, :8], dw_gates], axis=1)

    shard_cols = lambda a: jnp.transpose(a.reshape(a.shape[0], N_DEV, a.shape[1] // N_DEV), (1, 0, 2))
    d_mu, d_w0, d_w2p, d_a0, d_a2p, d_kk_, d_ka_ = d_rw_params
    d_gnw, d_gnb, d_rk = d_post_params
    d_conv = jnp.concatenate(d_gd_params[:4], axis=0)
    d_alog = d_gd_params[4].reshape(4, LANES).sum(axis=1).reshape(1, 4)
    d_dtb = d_gd_params[5].reshape(4, LANES).sum(axis=1).reshape(1, 4)
    scat = [shard_cols(dw_in_full), shard_cols(d_w2p[:64]), shard_cols(d_a2p[64:]), shard_cols(d_conv),
            shard_cols(dwa), shard_cols(dwb), dwo.reshape(N_DEV, D_MODEL // N_DEV, D_MODEL)]
    small_g = _pack_small([d_gin, d_mu, d_w0, d_a0, d_kk_, d_ka_, d_rk, d_gnw, d_gnb, d_alog, d_dtb, d_onw, d_now])
    lands = _exchange("reduce_grads", scat, [small_g])

    small_w = [norm_in_w, rw_mu, rw_w0, rw_a0, rw_k_k, rw_k_a, rw_r_k, rw_gn_w, rw_gn_b, gd_A_log, gd_dt_bias, gd_o_norm_w, norm_out_w]
    small_m = [m_norm_in_w, m_rw_mu, m_rw_w0, m_rw_a0, m_rw_k_k, m_rw_k_a, m_rw_r_k, m_rw_gn_w, m_rw_gn_b, m_gd_A_log, m_gd_dt_bias, m_gd_o_norm_w, m_norm_out_w]
    small_v = [v_norm_in_w, v_rw_mu, v_rw_w0, v_rw_a0, v_rw_k_k, v_rw_k_a, v_rw_r_k, v_rw_gn_w, v_rw_gn_b, v_gd_A_log, v_gd_dt_bias, v_gd_o_norm_w, v_norm_out_w]
    small_shapes = [a.shape for a in small_w]
    sm = _adam("adam_small", lands[7], _pack_small(small_w), _pack_small(small_m), _pack_small(small_v))
    sm_g, sm_d, sm_m, sm_v = [dict(zip([nm for nm, _ in _SMALL], _unpack_small(p, small_shapes))) for p in sm]

    big = {}
    for nm, land, w, m, v in (("w_in", lands[0], w_in, m_w_in, v_w_in), ("rw_w2", lands[1], rw_w2, m_rw_w2, v_rw_w2),
                              ("rw_a2", lands[2], rw_a2, m_rw_a2, v_rw_a2),
                              ("gd_conv_w", lands[3], gd_conv_w, m_gd_conv_w, v_gd_conv_w),
                              ("w_branch_a", lands[4], w_branch_a, m_w_branch_a, v_w_branch_a),
                              ("w_branch_b", lands[5], w_branch_b, m_w_branch_b, v_w_branch_b),
                              ("w_out", lands[6], w_out, m_w_out, v_w_out)):
        big[nm] = [o.reshape(w.shape) for o in _adam("adam_" + nm, land, w[0], m[0], v[0])]

    order = ["norm_in_w", "w_in", "rw_mu", "rw_w0", "rw_w2", "rw_a0", "rw_a2", "rw_k_k", "rw_k_a", "rw_r_k", "rw_gn_w",
             "rw_gn_b", "gd_conv_w", "gd_A_log", "gd_dt_bias", "gd_o_norm_w", "w_branch_a", "w_branch_b", "w_out", "norm_out_w"]
    pick = lambda nm, i: big[nm][i] if nm in big else (sm_g, sm_d, sm_m, sm_v)[i][nm]
    loss = lax.psum(loss_acc[0, 0], ("x", "y", "c"))
    grad_x = dx2.reshape(x.shape)
    return (loss, grad_x, *[pick(nm, 0) for nm in order], *[pick(nm, 1) for nm in order],
            *[pick(nm, 2) for nm in order], *[pick(nm, 3) for nm in order])
```

```python
import functools

import jax
import jax.numpy as jnp
from jax import lax
from jax.experimental import pallas as pl
from jax.experimental.pallas import tpu as pltpu

F32 = jnp.float32
BF16 = jnp.bfloat16
HI = lax.Precision.HIGHEST

LANES = 128
SUB = 8
CHUNK = 64
N_DEV = 8
VMEM_LIMIT = 56 * 1024 * 1024

D_MODEL = 1024
RW_W = 512
GD_W = 512
RW_SHIFT = 1664
NORM_EPS = 1e-6
RW_GN_EPS = 64 * 1e-5
ADAM_LR, ADAM_B1, ADAM_B2, ADAM_EPS, ADAM_WD, ADAM_STEP = 0.001, 0.9, 0.999, 1e-8, 0.01, 10


_NN, _NT, _TN = ((1,), (0,)), ((1,), (1,)), ((0,), (0,))


def _dot(a, b, dims, passes):
    precision = lax.Precision.HIGH if passes == 3 else lax.Precision.DEFAULT
    return lax.dot_general(a, b, (dims, ((), ())), precision=precision, preferred_element_type=F32)


def _mm(a, b, passes=3):
    return _dot(a, b, _NN, passes)


def _mm_nt(a, b, passes=3):
    return _dot(a, b, _NT, passes)


def _mm_tn(a, b, passes=3):
    return _dot(a, b, _TN, passes)


P_SUM = 3
P_SCORE = 1
P_INV = 1
P_STATE = 1
P_APPLY = 1
P_UPDATE = 1


def _iota(shape, d):
    return lax.broadcasted_iota(jnp.int32, shape, d)


def _sigmoid(x):
    return 0.5 * (jnp.tanh(0.5 * x) + 1.0)


def _silu(x):
    return x * _sigmoid(x)


def _softplus(x):
    return jnp.maximum(x, 0.0) + jnp.log(1.0 + jnp.exp(-jnp.abs(x)))


def _seg_ones(seg):
    return ((_iota((LANES, LANES), 0) // seg) == (_iota((LANES, LANES), 1) // seg)).astype(F32)


def _sl(g):
    return slice(g * LANES, (g + 1) * LANES)


def _tri_inverse(ms):
    c = CHUNK
    ri, ci = _iota((c, c), 0), _iota((c, c), 1)
    eye = (ri == ci).astype(F32)
    d16 = (ri // 16) == (ci // 16)
    d32 = (ri // 32) == (ci // 32)
    ps = [jnp.where(d16, -m, 0.0) for m in ms]
    ts = [eye + p for p in ps]
    for _ in range(3):
        ps = [_mm(p, p, P_INV) for p in ps]
        ts = [_mm(t, eye + p, P_INV) for t, p in zip(ts, ps)]
    for off_diagonal in (d32 & (~d16), ~d32):
        tq = [_mm(t, jnp.where(off_diagonal, m, 0.0), P_INV) for t, m in zip(ts, ms)]
        ts = [t - _mm(a, t, P_INV) for t, a in zip(ts, tq)]
    return ts


def _chunk_fwd(prims, *, nsub, scalar_decay):
    c = CHUNK
    ng = len(prims)
    s0s, rs, lws, ks, vs, kks, bs = [list(t) for t in zip(*prims)]
    ri, ci = _iota((c, c), 0), _iota((c, c), 1)
    incl = ri >= ci
    strict = ri > ci
    tril = incl.astype(F32)
    hs = LANES // nsub
    lane = _iota((1, LANES), 1)
    masks = [((lane // hs) == s).astype(F32) for s in range(nsub)]
    cws = [_mm(tril, lw, P_SUM) for lw in lws]
    cwxs = [cw - lw for cw, lw in zip(cws, lws)]
    ends = [cw[c - 1:c, :] for cw in cws]
    kkds = [kk * jnp.exp(cwx) for kk, cwx in zip(kks, cwxs)]
    rds = [r * jnp.exp(cw) for r, cw in zip(rs, cws)]
    kends = [k * jnp.exp(e - cw) for k, e, cw in zip(ks, ends, cws)]
    bends = [b * jnp.exp(e - cw) for b, e, cw in zip(bs, ends, cws)]
    w0s = [_mm_nt(kkd, s0, P_STATE) for kkd, s0 in zip(kkds, s0s)]
    y0s = [_mm_nt(rd, s0, P_STATE) for rd, s0 in zip(rds, s0s)]
    chains = [(g, s) for g in range(ng) for s in range(nsub)]
    if scalar_decay:
        e0 = (lane == 0).astype(F32) * jnp.ones((c, 1), F32)
        rows = [_mm_nt(e0, cw, P_SUM) for cw in cws]
        dxs = [jnp.where(strict, jnp.exp(jnp.minimum(cwx[:, :c] - row, 0.0)), 0.0) for cwx, row in zip(cwxs, rows)]
        dis = [jnp.where(incl, jnp.exp(jnp.minimum(cw[:, :c] - row, 0.0)), 0.0) for cw, row in zip(cws, rows)]
        m_b = [_mm_nt(kks[g] * masks[s], bs[g], P_SCORE) * dxs[g] for g, s in chains]
        m_k = [_mm_nt(kks[g] * masks[s], ks[g], P_SCORE) * dxs[g] for g, s in chains]
        n_k = [_mm_nt(rs[g] * masks[s], ks[g], P_SCORE) * dis[g] for g, s in chains]
        n_b = [_mm_nt(rs[g] * masks[s], bs[g], P_SCORE) * dis[g] for g, s in chains]
    else:
        kds = [k * jnp.exp(-cw) for k, cw in zip(ks, cws)]
        bds = [b * jnp.exp(-cw) for b, cw in zip(bs, cws)]
        m_b = [jnp.where(strict, _mm_nt(kkds[g] * masks[s], bds[g], P_SCORE), 0.0) for g, s in chains]
        m_k = [jnp.where(strict, _mm_nt(kkds[g] * masks[s], kds[g], P_SCORE), 0.0) for g, s in chains]
        n_k = [jnp.where(incl, _mm_nt(rds[g] * masks[s], kds[g], P_SCORE), 0.0) for g, s in chains]
        n_b = [jnp.where(incl, _mm_nt(rds[g] * masks[s], bds[g], P_SCORE), 0.0) for g, s in chains]
    t_inv = _tri_inverse(m_b)
    rhs = [w0s[g] + _mm(mk, vs[g], P_APPLY) for (g, s), mk in zip(chains, m_k)]
    sa_c = [_mm(t, x, P_APPLY) for t, x in zip(t_inv, rhs)]
    y_c = [y0s[g] + _mm(nk, vs[g], P_APPLY) - _mm(nb, sa, P_APPLY) for (g, s), nk, nb, sa in zip(chains, n_k, n_b, sa_c)]
    sas = [sum(sa_c[g * nsub + s] * masks[s] for s in range(nsub)) for g in range(ng)]
    ys = [sum(y_c[g * nsub + s] * masks[s] for s in range(nsub)) for g in range(ng)]
    s_ends = [s0 * jnp.exp(e) + _mm_tn(v, kend, P_UPDATE) - _mm_tn(sa, bend, P_UPDATE)
              for s0, e, v, kend, sa, bend in zip(s0s, ends, vs, kends, sas, bends)]
    if nsub > 1:
        same_head = (_iota((LANES, LANES), 0) // hs) == (_iota((LANES, LANES), 1) // hs)
        s_ends = [jnp.where(same_head, s_end, 0.0) for s_end in s_ends]
    return list(zip(ys, s_ends))


def _rec_fwd(name, r, lw, k, v, kk, b, *, seq, nsub, scalar_decay):
    n, w = r.shape
    ng = w // LANES
    nc = seq // CHUNK
    nb = n // seq

    def body(r_ref, lw_ref, k_ref, v_ref, kk_ref, b_ref, y_ref, s_ref, state):
        @pl.when(pl.program_id(1) == 0)
        def _():
            state[...] = jnp.zeros_like(state)
        prims = [(state[g], r_ref[:, _sl(g)], lw_ref[:, _sl(g)], k_ref[:, _sl(g)], v_ref[:, _sl(g)],
                  kk_ref[:, _sl(g)], b_ref[:, _sl(g)]) for g in range(ng)]
        outs = _chunk_fwd(prims, nsub=nsub, scalar_decay=scalar_decay)
        for g, (y, s_end) in enumerate(outs):
            s_ref[0, g] = prims[g][0]
            y_ref[:, _sl(g)] = y
            state[g] = s_end

    row = pl.BlockSpec((CHUNK, w), lambda bi, c: (bi * nc + c, 0))
    return pl.pallas_call(
        body, name=name, grid=(nb, nc),
        in_specs=[row] * 6,
        out_specs=[row, pl.BlockSpec((1, ng, LANES, LANES), lambda bi, c: (bi * nc + c, 0, 0, 0))],
        out_shape=[jax.ShapeDtypeStruct((n, w), F32), jax.ShapeDtypeStruct((nb * nc, ng, LANES, LANES), F32)],
        scratch_shapes=[pltpu.VMEM((ng, LANES, LANES), F32)],
        compiler_params=pltpu.CompilerParams(dimension_semantics=("parallel", "arbitrary"), vmem_limit_bytes=VMEM_LIMIT),
    )(r, lw, k, v, kk, b)


def _rec_bwd(name, r, lw, k, v, kk, b, s_save, dy, *, seq, nsub, scalar_decay):
    n, w = r.shape
    ng = w // LANES
    nc = seq // CHUNK
    nb = n // seq

    def body(r_ref, lw_ref, k_ref, v_ref, kk_ref, b_ref, s_ref, dy_ref,
             dr_ref, dlw_ref, dk_ref, dv_ref, dkk_ref, db_ref, dstate):
        @pl.when(pl.program_id(1) == 0)
        def _():
            dstate[...] = jnp.zeros_like(dstate)
        f = functools.partial(_chunk_fwd, nsub=nsub, scalar_decay=scalar_decay)
        prims = [(s_ref[0, g], r_ref[:, _sl(g)], lw_ref[:, _sl(g)], k_ref[:, _sl(g)], v_ref[:, _sl(g)],
                  kk_ref[:, _sl(g)], b_ref[:, _sl(g)]) for g in range(ng)]
        _, vjp = jax.vjp(f, prims)
        (d_prims,) = vjp([(dy_ref[:, _sl(g)], dstate[g]) for g in range(ng)])
        for g, (ds0, dr, dlw, dk, dv, dkk, db) in enumerate(d_prims):
            dstate[g] = ds0
            dr_ref[:, _sl(g)] = dr
            dlw_ref[:, _sl(g)] = dlw
            dk_ref[:, _sl(g)] = dk
            dv_ref[:, _sl(g)] = dv
            dkk_ref[:, _sl(g)] = dkk
            db_ref[:, _sl(g)] = db

    row = pl.BlockSpec((CHUNK, w), lambda bi, c: (bi * nc + nc - 1 - c, 0))
    return pl.pallas_call(
        body, name=name, grid=(nb, nc),
        in_specs=[row] * 6 + [pl.BlockSpec((1, ng, LANES, LANES), lambda bi, c: (bi * nc + nc - 1 - c, 0, 0, 0)), row],
        out_specs=[row] * 6,
        out_shape=[jax.ShapeDtypeStruct((n, w), F32)] * 6,
        scratch_shapes=[pltpu.VMEM((ng, LANES, LANES), F32)],
        compiler_params=pltpu.CompilerParams(dimension_semantics=("parallel", "arbitrary"), vmem_limit_bytes=VMEM_LIMIT),
    )(r, lw, k, v, kk, b, s_save, dy)


def _shift_down(a, j, halo, is_start):
    tb = a.shape[0]
    rolled = pltpu.roll(a, j, 0)
    hr = jnp.where(is_start, 0.0, pltpu.roll(halo, j, 0))
    first = jnp.where(_iota((SUB, LANES), 0) < j, hr, rolled[0:SUB])
    if tb == SUB:
        return first
    return jnp.concatenate([first, rolled[SUB:]], axis=0)


def _shift_up(d, j, carry, is_end):
    tb = d.shape[0]
    up = pltpu.roll(d, tb - j, 0)
    cr = jnp.where(is_end, 0.0, pltpu.roll(carry, SUB - j, 0))
    last = jnp.where(_iota((SUB, LANES), 0) >= SUB - j, cr, up[tb - SUB:tb])
    if tb == SUB:
        return last
    return jnp.concatenate([up[:tb - SUB], last], axis=0)


def _ngroups(a):
    return a.shape[1] // LANES


def _pw_fwd(name, f, ins, shift, params, out_widths, out_dtypes, *, seq, tb):
    n = ins[0].shape[0]
    nt, tps = n // tb, seq // tb
    ni, npar = len(ins), len(params)

    def body(*refs):
        in_refs = refs[:ni]
        pos = ni
        halo_ref = None
        if shift:
            halo_ref = refs[pos]
            pos += 1
        p_refs = refs[pos:pos + npar]
        out_refs = refs[pos + npar:]
        is_start = (pl.program_id(0) % tps) == 0
        tiles = [[ref[:, _sl(g)] for g in range(_ngroups(ref))] for ref in in_refs]
        prevs = [[_shift_down(tiles[0][g], j, halo_ref[:, _sl(g)], is_start) for g in range(len(tiles[0]))]
                 for j in range(1, shift + 1)]
        pv = [[ref[:, _sl(g)] for g in range(_ngroups(ref))] for ref in p_refs]
        outs = f(tiles, prevs, pv)
        for o_ref, og in zip(out_refs, outs, strict=True):
            for g, t in enumerate(og):
                o_ref[:, _sl(g)] = t.astype(o_ref.dtype)

    in_specs = [pl.BlockSpec((tb, a.shape[1]), lambda i: (i, 0)) for a in ins]
    args = list(ins)
    if shift:
        in_specs.append(pl.BlockSpec((SUB, ins[0].shape[1]), lambda i: (jnp.maximum(i * (tb // SUB) - 1, 0), 0)))
        args.append(ins[0])
    in_specs += [pl.BlockSpec(p.shape, lambda i: (0, 0)) for p in params]
    args += list(params)
    return pl.pallas_call(
        body, name=name, grid=(nt,),
        in_specs=in_specs,
        out_specs=[pl.BlockSpec((tb, w), lambda i: (i, 0)) for w in out_widths],
        out_shape=[jax.ShapeDtypeStruct((n, w), dt) for w, dt in zip(out_widths, out_dtypes, strict=True)],
        compiler_params=pltpu.CompilerParams(dimension_semantics=("parallel",), vmem_limit_bytes=VMEM_LIMIT),
    )(*args)


def _pw_bwd(name, f, ins, shift, params, douts, din_dtypes, *, seq, tb):
    n = ins[0].shape[0]
    nt, tps = n // tb, seq // tb
    ni, npar = len(ins), len(params)
    flat_douts = [d for ds in douts for d in ds]
    nd = len(flat_douts)
    w0 = ins[0].shape[1]

    def body(*refs):
        in_refs = refs[:ni]
        pos = ni
        halo_ref = None
        if shift:
            halo_ref = refs[pos]
            pos += 1
        p_refs = refs[pos:pos + npar]
        pos += npar
        d_refs = refs[pos:pos + nd]
        pos += nd
        din_refs = refs[pos:pos + ni]
        pos += ni
        dp_refs = refs[pos:pos + npar]
        pos += npar
        carry = refs[pos] if shift else None
        step = pl.program_id(0)
        tile = nt - 1 - step
        is_start = (tile % tps) == 0
        is_end = (tile % tps) == tps - 1
        tiles = [[ref[:, _sl(g)] for g in range(_ngroups(ref))] for ref in in_refs]
        prevs = [[_shift_down(tiles[0][g], j, halo_ref[:, _sl(g)], is_start) for g in range(len(tiles[0]))]
                 for j in range(1, shift + 1)]
        pv = [[ref[:, _sl(g)] for g in range(_ngroups(ref))] for ref in p_refs]
        cot, pos_d = [], 0
        for ds in douts:
            grp = d_refs[pos_d:pos_d + len(ds)]
            pos_d += len(ds)
            cot.append([sum(ref[:, _sl(g)].astype(F32) for ref in grp) for g in range(_ngroups(grp[0]))])
        _, vjp = jax.vjp(f, tiles, prevs, pv)
        d_tiles, d_prevs, d_pv = vjp(cot)
        for g in range(len(tiles[0])):
            for j in range(1, shift + 1):
                d_tiles[0][g] = d_tiles[0][g] + _shift_up(d_prevs[j - 1][g], j, carry[j - 1, :, _sl(g)], is_end)
            for j in range(1, shift + 1):
                carry[j - 1, :, _sl(g)] = d_prevs[j - 1][g][0:SUB]
        for ref, dg in zip(din_refs, d_tiles, strict=True):
            for g, t in enumerate(dg):
                ref[:, _sl(g)] = t.astype(ref.dtype)

        @pl.when(step == 0)
        def _():
            for ref in dp_refs:
                ref[...] = jnp.zeros_like(ref)
        for ref, dg in zip(dp_refs, d_pv, strict=True):
            for g, t in enumerate(dg):
                ref[:, _sl(g)] += t

    rev = lambda i: (nt - 1 - i, 0)
    in_specs = [pl.BlockSpec((tb, a.shape[1]), rev) for a in ins]
    args = list(ins)
    if shift:
        in_specs.append(pl.BlockSpec((SUB, w0), lambda i: (jnp.maximum((nt - 1 - i) * (tb // SUB) - 1, 0), 0)))
        args.append(ins[0])
    in_specs += [pl.BlockSpec(p.shape, lambda i: (0, 0)) for p in params]
    args += list(params)
    in_specs += [pl.BlockSpec((tb, d.shape[1]), rev) for d in flat_douts]
    args += flat_douts
    out_specs = [pl.BlockSpec((tb, a.shape[1]), rev) for a in ins] + [pl.BlockSpec(p.shape, lambda i: (0, 0)) for p in params]
    out_shape = ([jax.ShapeDtypeStruct(a.shape, dt) for a, dt in zip(ins, din_dtypes, strict=True)]
                 + [jax.ShapeDtypeStruct(p.shape, F32) for p in params])
    res = pl.pallas_call(
        body, name=name, grid=(nt,),
        in_specs=in_specs, out_specs=out_specs, out_shape=out_shape,
        scratch_shapes=[pltpu.VMEM((shift, SUB, w0), F32)] if shift else [],
        compiler_params=pltpu.CompilerParams(dimension_semantics=("arbitrary",), vmem_limit_bytes=VMEM_LIMIT),
    )(*args)
    return res[:ni], res[ni:]


def _rwkv_prep_f(tiles, prevs, params):
    (p,), (prev,) = tiles, prevs
    mu, w0, w2p, a0, a2p, k_k, k_a = params
    xs = [p[g] + (prev[g] - p[g]) * mu[g] for g in range(13)]
    wdad = xs[12]
    tw = jnp.tanh(wdad)
    e64 = _seg_ones(64)
    r, lw, k2, v, kk, b = [], [], [], [], [], []
    for g in range(4):
        k_g = xs[4 + g]
        lo = w0[g] + _mm(tw, w2p[g])
        lw_g = -jnp.exp(-_softplus(-lo) - 0.5)
        a_g = _sigmoid(a0[g] + _mm(wdad, a2p[g]))
        kkp = k_g * k_k[g]
        kk_g = kkp * lax.rsqrt(_mm(kkp * kkp, e64) + 1e-12)
        r.append(xs[g])
        lw.append(lw_g)
        k2.append(k_g * (1.0 + (a_g - 1.0) * k_a[g]))
        v.append(xs[8 + g])
        kk.append(kk_g)
        b.append(kk_g * a_g)
    return [r, lw, k2, v, kk, b]


def _rwkv_post_f(tiles, prevs, params):
    yrec, r, k2, v, z = tiles
    gn_w, gn_b, r_k = params
    e64 = _seg_ones(64)
    out = []
    for g in range(4):
        mean = _mm(yrec[g], e64) * (1.0 / 64)
        d = yrec[g] - mean
        var = _mm(d * d, e64) * (1.0 / 64)
        yn = d * lax.rsqrt(var + RW_GN_EPS) * gn_w[g] + gn_b[g]
        bonus = _mm(r[g] * k2[g] * r_k[g], e64) * v[g]
        out.append((yn + bonus) * _silu(z[g]))
    return [out]


def _gdn_prep_f(tiles, prevs, params):
    x, (ba,) = tiles
    p1, p2, p3 = prevs
    cw0, cw1, cw2, cw3, a_log, dt_bias = params
    s = [_silu(cw3[g] * x[g] + cw2[g] * p1[g] + cw1[g] * p2[g] + cw0[g] * p3[g]) for g in range(12)]
    row = _iota((LANES, LANES), 0)
    r, lw, k, vv, b = [], [], [], [], []
    for h in range(4):
        q_h, k_h, v_h = s[h], s[4 + h], s[8 + h]
        qn = q_h * lax.rsqrt(jnp.sum(q_h * q_h, axis=-1, keepdims=True) + 1e-12)
        kn = k_h * lax.rsqrt(jnp.sum(k_h * k_h, axis=-1, keepdims=True) + 1e-12)
        beta = _sigmoid(_mm(ba, (row == h).astype(F32)))
        alpha = _mm(ba, (row == 4 + h).astype(F32))
        g_h = -jnp.exp(a_log[h]) * _softplus(alpha + dt_bias[h])
        r.append(qn * (LANES ** -0.5))
        lw.append(g_h)
        k.append(kn)
        vv.append(beta * v_h)
        b.append(jnp.exp(g_h) * beta * kn)
    return [r, lw, k, vv, b]


def _gdn_post_f(tiles, prevs, params):
    o, z = tiles
    ((onw,),) = params
    out = []
    for h in range(4):
        ms = jnp.mean(o[h] * o[h], axis=-1, keepdims=True)
        out.append(o[h] * lax.rsqrt(ms + NORM_EPS) * onw * _silu(z[h]))
    return [out]


def _norm_in(x2, g_in, *, tm):
    n = x2.shape[0]

    def body(x_ref, g_ref, h_ref):
        x = x_ref[...]
        rs = lax.rsqrt(jnp.mean(x * x, axis=-1, keepdims=True) + NORM_EPS)
        h_ref[...] = (x * rs * g_ref[...]).astype(BF16)

    return pl.pallas_call(
        body, name="norm_in", grid=(n // tm,),
        in_specs=[pl.BlockSpec((tm, D_MODEL), lambda i: (i, 0)), pl.BlockSpec((1, D_MODEL), lambda i: (0, 0))],
        out_specs=pl.BlockSpec((tm, D_MODEL), lambda i: (i, 0)),
        out_shape=jax.ShapeDtypeStruct((n, D_MODEL), BF16),
        compiler_params=pltpu.CompilerParams(dimension_semantics=("parallel",), vmem_limit_bytes=VMEM_LIMIT),
    )(x2, g_in)


def _proj(name, h, w, *, tm):
    n, ws = h.shape[0], w.shape[1]

    def body(h_ref, w_ref, o_ref):
        o_ref[...] = jnp.dot(h_ref[...], w_ref[...], preferred_element_type=F32)

    return pl.pallas_call(
        body, name=name, grid=(n // tm,),
        in_specs=[pl.BlockSpec((tm, D_MODEL), lambda i: (i, 0)), pl.BlockSpec((D_MODEL, ws), lambda i: (0, 0))],
        out_specs=pl.BlockSpec((tm, ws), lambda i: (i, 0)),
        out_shape=jax.ShapeDtypeStruct((n, ws), F32),
        compiler_params=pltpu.CompilerParams(dimension_semantics=("parallel",), vmem_limit_bytes=VMEM_LIMIT),
    )(h, w)


def _proj_dw(name, h, dp, *, tm):
    n, ws = dp.shape

    def body(h_ref, d_ref, o_ref):
        @pl.when(pl.program_id(0) == 0)
        def _():
            o_ref[...] = jnp.zeros_like(o_ref)
        o_ref[...] += lax.dot_general(h_ref[...], d_ref[...], (((0,), (0,)), ((), ())), preferred_element_type=F32)

    return pl.pallas_call(
        body, name=name, grid=(n // tm,),
        in_specs=[pl.BlockSpec((tm, D_MODEL), lambda i: (i, 0)), pl.BlockSpec((tm, ws), lambda i: (i, 0))],
        out_specs=pl.BlockSpec((D_MODEL, ws), lambda i: (0, 0)),
        out_shape=jax.ShapeDtypeStruct((D_MODEL, ws), F32),
        compiler_params=pltpu.CompilerParams(dimension_semantics=("arbitrary",), vmem_limit_bytes=VMEM_LIMIT),
    )(h, dp)


def _proj_dx(x2, g_in, d_xo, dps, ws, *, tm):
    n = x2.shape[0]
    ns = len(dps)

    def body(*refs):
        x_ref, g_ref, dxo_ref = refs[:3]
        dp_refs = refs[3:3 + ns]
        w_refs = refs[3 + ns:3 + 2 * ns]
        dx_ref, dg_ref = refs[3 + 2 * ns:]
        dh = jnp.zeros((tm, D_MODEL), F32)
        for d_ref, w_ref in zip(dp_refs, w_refs, strict=True):
            dh = dh + lax.dot_general(d_ref[...], w_ref[...], (((1,), (1,)), ((), ())), preferred_element_type=F32)
        x = x_ref[...]
        rs = lax.rsqrt(jnp.mean(x * x, axis=-1, keepdims=True) + NORM_EPS)
        xn = x * rs
        dxn = dh * g_ref[...]
        dx_ref[...] = dxo_ref[...] + rs * (dxn - xn * jnp.mean(dxn * xn, axis=-1, keepdims=True))

        @pl.when(pl.program_id(0) == 0)
        def _():
            dg_ref[...] = jnp.zeros_like(dg_ref)
        dg_ref[...] += jnp.sum(dh * xn, axis=0, keepdims=True)

    row = pl.BlockSpec((tm, D_MODEL), lambda i: (i, 0))
    return pl.pallas_call(
        body, name="proj_dx", grid=(n // tm,),
        in_specs=([row, pl.BlockSpec((1, D_MODEL), lambda i: (0, 0)), row]
                  + [pl.BlockSpec((tm, d.shape[1]), lambda i: (i, 0)) for d in dps]
                  + [pl.BlockSpec(w.shape, lambda i: (0, 0)) for w in ws]),
        out_specs=[row, pl.BlockSpec((1, D_MODEL), lambda i: (0, 0))],
        out_shape=[jax.ShapeDtypeStruct((n, D_MODEL), F32), jax.ShapeDtypeStruct((1, D_MODEL), F32)],
        compiler_params=pltpu.CompilerParams(dimension_semantics=("arbitrary",), vmem_limit_bytes=VMEM_LIMIT),
    )(x2, g_in, d_xo, *dps, *ws)


def _tail(x2, tgt2, gates, ya, yb, w_a, w_b, w_o, now, *, tr):
    n = x2.shape[0]

    def body(x_ref, t_ref, g_ref, ya_ref, yb_ref, wa_ref, wb_ref, wo_ref, now_ref,
             dya_ref, dyb_ref, dg_ref, dxo_ref, dwa_ref, dwb_ref, dwo_ref, dnow_ref, loss_ref):
        ya16, yb16 = ya_ref[...].astype(BF16), yb_ref[...].astype(BF16)
        ua = jnp.dot(ya16, wa_ref[...], preferred_element_type=F32)
        ub = jnp.dot(yb16, wb_ref[...], preferred_element_type=F32)
        ga = _sigmoid(g_ref[:, :D_MODEL])
        gb = _sigmoid(g_ref[:, D_MODEL:])
        m16 = (ga * ua + gb * ub).astype(BF16)
        xo = x_ref[...] + jnp.dot(m16, wo_ref[...], preferred_element_type=F32)
        rs = lax.rsqrt(jnp.mean(xo * xo, axis=-1, keepdims=True) + NORM_EPS)
        yn = xo * rs
        now_v = now_ref[...]
        err = yn * now_v - t_ref[...]
        dy = err * (1.0 / D_MODEL)
        dyn = dy * now_v
        dxo = rs * (dyn - yn * jnp.mean(dyn * yn, axis=-1, keepdims=True))
        dxo_ref[...] = dxo
        dxo16 = dxo.astype(BF16)
        dm = lax.dot_general(dxo16, wo_ref[...], (((1,), (1,)), ((), ())), preferred_element_type=F32)
        dua16 = (dm * ga).astype(BF16)
        dub16 = (dm * gb).astype(BF16)
        dg_ref[:, :D_MODEL] = (dm * ua * ga * (1.0 - ga)).astype(dg_ref.dtype)
        dg_ref[:, D_MODEL:] = (dm * ub * gb * (1.0 - gb)).astype(dg_ref.dtype)
        dya_ref[...] = lax.dot_general(dua16, wa_ref[...], (((1,), (1,)), ((), ())), preferred_element_type=F32)
        dyb_ref[...] = lax.dot_general(dub16, wb_ref[...], (((1,), (1,)), ((), ())), preferred_element_type=F32)

        @pl.when(pl.program_id(0) == 0)
        def _():
            for ref in (dwa_ref, dwb_ref, dwo_ref, dnow_ref, loss_ref):
                ref[...] = jnp.zeros_like(ref)
        tn = (((0,), (0,)), ((), ()))
        dwo_ref[...] += lax.dot_general(m16, dxo16, tn, preferred_element_type=F32)
        dwa_ref[...] += lax.dot_general(ya16, dua16, tn, preferred_element_type=F32)
        dwb_ref[...] += lax.dot_general(yb16, dub16, tn, preferred_element_type=F32)
        dnow_ref[...] += jnp.sum(dy * yn, axis=0, keepdims=True)
        loss_ref[...] += (0.5 / D_MODEL) * jnp.sum(err * err)

    row = lambda w: pl.BlockSpec((tr, w), lambda i: (i, 0))
    full = lambda a: pl.BlockSpec(a.shape, lambda i: (0, 0))
    return pl.pallas_call(
        body, name="tail", grid=(n // tr,),
        in_specs=[row(D_MODEL), row(D_MODEL), row(2 * D_MODEL), row(RW_W), row(GD_W), full(w_a), full(w_b), full(w_o), full(now)],
        out_specs=[row(RW_W), row(GD_W), row(2 * D_MODEL), row(D_MODEL),
                   pl.BlockSpec((RW_W, D_MODEL), lambda i: (0, 0)), pl.BlockSpec((GD_W, D_MODEL), lambda i: (0, 0)),
                   pl.BlockSpec((D_MODEL, D_MODEL), lambda i: (0, 0)), pl.BlockSpec((1, D_MODEL), lambda i: (0, 0)),
                   pl.BlockSpec((SUB, LANES), lambda i: (0, 0))],
        out_shape=[jax.ShapeDtypeStruct((n, RW_W), F32), jax.ShapeDtypeStruct((n, GD_W), F32),
                   jax.ShapeDtypeStruct((n, 2 * D_MODEL), BF16), jax.ShapeDtypeStruct((n, D_MODEL), F32),
                   jax.ShapeDtypeStruct((RW_W, D_MODEL), F32), jax.ShapeDtypeStruct((GD_W, D_MODEL), F32),
                   jax.ShapeDtypeStruct((D_MODEL, D_MODEL), F32), jax.ShapeDtypeStruct((1, D_MODEL), F32),
                   jax.ShapeDtypeStruct((SUB, LANES), F32)],
        compiler_params=pltpu.CompilerParams(dimension_semantics=("arbitrary",), vmem_limit_bytes=VMEM_LIMIT),
    )(x2, tgt2, gates, ya, yb, w_a, w_b, w_o, now)


def _exchange(name, axes, scatter, gather):
    ns, ng = len(scatter), len(gather)
    na = ns + ng
    gs = 2 ** len(axes)

    def body(*refs):
        src = refs[:na]
        dst = refs[na:2 * na]
        send_sems, recv_sems, local_sems = refs[2 * na:]
        mine = {ax: lax.axis_index(ax) for ax in ("x", "y", "c")}

        def peer(k):
            co = dict(mine)
            for i, ax in enumerate(axes):
                if (k >> (len(axes) - 1 - i)) & 1:
                    co[ax] = 1 - co[ax]
            idx = 0
            for ax in axes:
                idx = 2 * idx + co[ax]
            return (co["x"], co["y"], co["c"]), idx

        _, me = peer(0)

        def copy(a, k, landing):
            dev, idx = peer(k)
            s = src[a].at[idx] if a < ns else src[a]
            return pltpu.make_async_remote_copy(src_ref=s, dst_ref=dst[a].at[idx if landing else me],
                                                send_sem=send_sems.at[a, k - 1], recv_sem=recv_sems.at[a, k - 1],
                                                device_id=dev, device_id_type=pl.DeviceIdType.MESH)

        local = [pltpu.make_async_copy(src[a].at[me] if a < ns else src[a], dst[a].at[me], local_sems.at[a])
                 for a in range(na)]
        sends = [copy(a, k, False) for a in range(na) for k in range(1, gs)]
        for cp in local + sends:
            cp.start()
        for a in range(na):
            for k in range(1, gs):
                copy(a, k, True).wait_recv()
        for cp in sends:
            cp.wait_send()
        for cp in local:
            cp.wait()

    arrs = list(scatter) + list(gather)
    out_shape = [jax.ShapeDtypeStruct(a.shape, a.dtype) for a in scatter] + \
                [jax.ShapeDtypeStruct((gs,) + a.shape, a.dtype) for a in gather]
    anyspec = pl.BlockSpec(memory_space=pl.ANY)
    return pl.pallas_call(
        body, name=name,
        in_specs=[anyspec] * na, out_specs=[anyspec] * na, out_shape=out_shape,
        scratch_shapes=[pltpu.SemaphoreType.DMA((na, gs - 1)), pltpu.SemaphoreType.DMA((na, gs - 1)),
                        pltpu.SemaphoreType.DMA((na,))],
    )(*arrs)


def _sum_slots(name, land):
    ns, r, c = land.shape
    tr = 256 if (r % 256 == 0 and r > 256) else r

    def body(l_ref, o_ref):
        acc = l_ref[0]
        for s in range(1, ns):
            acc = acc + l_ref[s]
        o_ref[...] = acc

    return pl.pallas_call(
        body, name=name, grid=(r // tr,),
        in_specs=[pl.BlockSpec((ns, tr, c), lambda i: (0, i, 0))],
        out_specs=pl.BlockSpec((tr, c), lambda i: (i, 0)),
        out_shape=jax.ShapeDtypeStruct((r, c), F32),
        compiler_params=pltpu.CompilerParams(dimension_semantics=("parallel",), vmem_limit_bytes=VMEM_LIMIT),
    )(land)


def _adam(name, land, w, m, v):
    r, c = w.shape
    nslot = land.shape[0]
    tr = 256 if (r % 256 == 0 and r > 256) else r
    c1 = 1.0 / (1.0 - ADAM_B1 ** ADAM_STEP)
    c2 = 1.0 / (1.0 - ADAM_B2 ** ADAM_STEP)

    def body(l_ref, w_ref, m_ref, v_ref, g_out, d_out, m_out, v_out):
        g = l_ref[0]
        for s in range(1, nslot):
            g = g + l_ref[s]
        m_new = ADAM_B1 * m_ref[...] + (1.0 - ADAM_B1) * g
        v_new = ADAM_B2 * v_ref[...] + (1.0 - ADAM_B2) * (g * g)
        g_out[...] = g
        m_out[...] = m_new
        v_out[...] = v_new
        d_out[...] = -ADAM_LR * ((m_new * c1) / (jnp.sqrt(v_new * c2) + ADAM_EPS) + ADAM_WD * w_ref[...])

    blk = pl.BlockSpec((tr, c), lambda i: (i, 0))
    return pl.pallas_call(
        body, name=name, grid=(r // tr,),
        in_specs=[pl.BlockSpec((nslot, tr, c), lambda i: (0, i, 0)), blk, blk, blk],
        out_specs=[blk] * 4,
        out_shape=[jax.ShapeDtypeStruct((r, c), F32)] * 4,
        compiler_params=pltpu.CompilerParams(dimension_semantics=("parallel",), vmem_limit_bytes=VMEM_LIMIT),
    )(land, w, m, v)


_SMALL = (("norm_in_w", 1024), ("rw_mu", 1664), ("rw_w0", 512), ("rw_a0", 512), ("rw_k_k", 512), ("rw_k_a", 512),
          ("rw_r_k", 512), ("rw_gn_w", 512), ("rw_gn_b", 512), ("gd_A_log", 4), ("gd_dt_bias", 4), ("gd_o_norm_w", 128),
          ("norm_out_w", 1024))
_SMALL_ROWS = 64


def _pack_small(vals):
    rows = []
    for (_, size), a in zip(_SMALL, vals, strict=True):
        flat = a.reshape(-1).astype(F32)
        pad = (-size) % LANES
        if pad:
            flat = jnp.concatenate([flat, jnp.zeros((pad,), F32)])
        rows.append(flat.reshape(-1, LANES))
    used = sum(r.shape[0] for r in rows)
    rows.append(jnp.zeros((_SMALL_ROWS - used, LANES), F32))
    return jnp.concatenate(rows, axis=0)


def _unpack_small(packed, shapes):
    out, row = [], 0
    for (_, size), shp in zip(_SMALL, shapes, strict=True):
        nrow = -(-size // LANES)
        out.append(packed[row:row + nrow].reshape(-1)[:size].reshape(shp))
        row += nrow
    return out


def kernel(x, norm_in_w, w_in, rw_mu, rw_w0, rw_w2, rw_a0, rw_a2, rw_k_k, rw_k_a, rw_r_k, rw_gn_w, rw_gn_b, gd_conv_w, gd_A_log, gd_dt_bias, gd_o_norm_w, w_branch_a, w_branch_b, w_out, norm_out_w, loss_target, m_norm_in_w, m_w_in, m_rw_mu, m_rw_w0, m_rw_w2, m_rw_a0, m_rw_a2, m_rw_k_k, m_rw_k_a, m_rw_r_k, m_rw_gn_w, m_rw_gn_b, m_gd_conv_w, m_gd_A_log, m_gd_dt_bias, m_gd_o_norm_w, m_w_branch_a, m_w_branch_b, m_w_out, m_norm_out_w, v_norm_in_w, v_w_in, v_rw_mu, v_rw_w0, v_rw_w2, v_rw_a0, v_rw_a2, v_rw_k_k, v_rw_k_a, v_rw_r_k, v_rw_gn_w, v_rw_gn_b, v_gd_conv_w, v_gd_A_log, v_gd_dt_bias, v_gd_o_norm_w, v_w_branch_a, v_w_branch_b, v_w_out, v_norm_out_w):
    nb, seq, _ = x.shape
    n = nb * seq
    tm = min(512, n)
    tb = min(256, seq)
    x2 = x.reshape(n, D_MODEL)
    tgt2 = loss_target.reshape(n, D_MODEL)
    cols = w_in.shape[2]
    in_cols = cols * N_DEV

    sharded = [w_in[0].astype(BF16), rw_w2[0], rw_a2[0], gd_conv_w[0], w_branch_a[0].astype(BF16),
               w_branch_b[0].astype(BF16), w_out[0].astype(BF16)]
    by_chip = _exchange("gather_chips", ("x", "y"), [], sharded)
    g_win, g_w2, g_a2, g_conv, g_wa, g_wb, g_wo = _exchange("gather_cores", ("c",), [], by_chip)
    unshard_cols = lambda a: jnp.transpose(a, (2, 1, 0, 3)).reshape(a.shape[2], N_DEV * a.shape[3])
    w_full = unshard_cols(g_win)
    seg_bounds = ((0, 1664), (1664, 2176), (2176, 3712), (3712, 4224), (4232, in_cols))
    w_rw, w_zrw, w_qkv, w_zgd, w_gates = [w_full[:, a:b] for a, b in seg_bounds]
    w_ba = jnp.concatenate([w_full[:, 4224:4232], jnp.zeros((D_MODEL, LANES - 8), BF16)], axis=1)
    w2_full, a2_full = unshard_cols(g_w2), unshard_cols(g_a2)
    zeros64 = jnp.zeros((64, RW_W), F32)
    w2p = jnp.concatenate([w2_full, zeros64], axis=0)
    a2p = jnp.concatenate([zeros64, a2_full], axis=0)
    conv_full = unshard_cols(g_conv)
    conv_rows = [conv_full[i:i + 1] for i in range(4)]
    wa_full = unshard_cols(g_wa)
    wb_full = unshard_cols(g_wb)
    wo_full = jnp.transpose(g_wo, (1, 0, 2, 3)).reshape(D_MODEL, D_MODEL)
    a_log_bc = jnp.repeat(gd_A_log, LANES, axis=1)
    dt_bias_bc = jnp.repeat(gd_dt_bias, LANES, axis=1)
    r_k_flat = rw_r_k.reshape(1, RW_W)
    now2 = norm_out_w.reshape(1, D_MODEL)

    h = _norm_in(x2, norm_in_w, tm=tm)
    p_rw = _proj("proj_rw", h, w_rw, tm=tm)
    p_zrw = _proj("proj_zrw", h, w_zrw, tm=tm)
    p_qkv = _proj("proj_qkv", h, w_qkv, tm=tm)
    p_zgd = _proj("proj_zgd", h, w_zgd, tm=tm)
    p_ba = _proj("proj_ba", h, w_ba, tm=tm)
    p_gates = _proj("proj_gates", h, w_gates, tm=tm)

    rw_params = [rw_mu, rw_w0, w2p, rw_a0, a2p, rw_k_k, rw_k_a]
    r_a, lw_a, k_a, v_a, kk_a, b_a = _pw_fwd("rwkv_prep", _rwkv_prep_f, [p_rw], 1, rw_params, [RW_W] * 6, [F32] * 6,
                                             seq=seq, tb=tb)
    y_rec, s_a = _rec_fwd("rwkv_rec", r_a, lw_a, k_a, v_a, kk_a, b_a, seq=seq, nsub=2, scalar_decay=False)
    post_params = [rw_gn_w, rw_gn_b, r_k_flat]
    (y_a,) = _pw_fwd("rwkv_post", _rwkv_post_f, [y_rec, r_a, k_a, v_a, p_zrw], 0, post_params, [RW_W], [F32], seq=seq, tb=tb)

    gd_params = conv_rows + [a_log_bc, dt_bias_bc]
    r_b, lw_b, k_b, v_b, b_b = _pw_fwd("gdn_prep", _gdn_prep_f, [p_qkv, p_ba], 3, gd_params, [GD_W] * 5, [F32] * 5,
                                       seq=seq, tb=tb)
    o_rec, s_b = _rec_fwd("gdn_rec", r_b, lw_b, k_b, v_b, k_b, b_b, seq=seq, nsub=1, scalar_decay=True)
    (y_b,) = _pw_fwd("gdn_post", _gdn_post_f, [o_rec, p_zgd], 0, [gd_o_norm_w], [GD_W], [F32], seq=seq, tb=tb)

    d_ya, d_yb, d_gates, d_xo, dwa, dwb, dwo, d_now, loss_acc = _tail(
        x2, tgt2, p_gates, y_a, y_b, wa_full, wb_full, wo_full, now2, tr=min(256, n))

    (d_o, d_zgd), (d_onw,) = _pw_bwd("gdn_post_bwd", _gdn_post_f, [o_rec, p_zgd], 0, [gd_o_norm_w], [[d_yb]],
                                     [F32, BF16], seq=seq, tb=tb)
    dr_b, dlw_b, dk_b, dv_b, dkk_b, db_b = _rec_bwd("gdn_rec_bwd", r_b, lw_b, k_b, v_b, k_b, b_b, s_b, d_o,
                                                    seq=seq, nsub=1, scalar_decay=True)
    (d_qkv, d_ba), d_gd_params = _pw_bwd("gdn_prep_bwd", _gdn_prep_f, [p_qkv, p_ba], 3, gd_params,
                                         [[dr_b], [dlw_b], [dk_b, dkk_b], [dv_b], [db_b]], [BF16, BF16], seq=seq, tb=tb)

    (d_yrec, dr_p, dk_p, dv_p, d_zrw), d_post_params = _pw_bwd(
        "rwkv_post_bwd", _rwkv_post_f, [y_rec, r_a, k_a, v_a, p_zrw], 0, post_params, [[d_ya]],
        [F32, F32, F32, F32, BF16], seq=seq, tb=tb)
    dr_a, dlw_a, dk_a, dv_a, dkk_a, db_a = _rec_bwd("rwkv_rec_bwd", r_a, lw_a, k_a, v_a, kk_a, b_a, s_a, d_yrec,
                                                    seq=seq, nsub=2, scalar_decay=False)
    (d_prw,), d_rw_params = _pw_bwd("rwkv_prep_bwd", _rwkv_prep_f, [p_rw], 1, rw_params,
                                    [[dr_a, dr_p], [dlw_a], [dk_a, dk_p], [dv_a, dv_p], [dkk_a], [db_a]], [BF16],
                                    seq=seq, tb=tb)

    dps = [d_prw, d_zrw, d_qkv, d_zgd, d_ba, d_gates]
    wsegs = [w_rw, w_zrw, w_qkv, w_zgd, w_ba, w_gates]
    dx2, d_gin = _proj_dx(x2, norm_in_w, d_xo, dps, wsegs, tm=min(256, n))
    dw_rw = _proj_dw("dw_rw", h, d_prw, tm=tm)
    dw_zrw = _proj_dw("dw_zrw", h, d_zrw, tm=tm)
    dw_qkv = _proj_dw("dw_qkv", h, d_qkv, tm=tm)
    dw_zgd = _proj_dw("dw_zgd", h, d_zgd, tm=tm)
    dw_ba = _proj_dw("dw_ba", h, d_ba, tm=tm)
    dw_gates = _proj_dw("dw_gates", h, d_gates, tm=tm)
    dw_in_full = jnp.concatenate([dw_rw, dw_zrw, dw_qkv, dw_zgd, dw_ba[:, :8], dw_gates], axis=1)

    shard_cols = lambda a: jnp.transpose(a.reshape(a.shape[0], 4, 2, a.shape[1] // N_DEV), (2, 1, 0, 3))
    d_mu, d_w0, d_w2p, d_a0, d_a2p, d_kk_, d_ka_ = d_rw_params
    d_gnw, d_gnb, d_rk = d_post_params
    d_conv = jnp.concatenate(d_gd_params[:4], axis=0)
    d_alog = d_gd_params[4].reshape(4, LANES).sum(axis=1).reshape(1, 4)
    d_dtb = d_gd_params[5].reshape(4, LANES).sum(axis=1).reshape(1, 4)
    scat = [shard_cols(dw_in_full), shard_cols(d_w2p[:64]), shard_cols(d_a2p[64:]), shard_cols(d_conv),
            shard_cols(dwa), shard_cols(dwb),
            jnp.transpose(dwo.reshape(4, 2, D_MODEL // N_DEV, D_MODEL), (1, 0, 2, 3))]
    small_g = _pack_small([d_gin, d_mu, d_w0, d_a0, d_kk_, d_ka_, d_rk, d_gnw, d_gnb, d_alog, d_dtb, d_onw, d_now])
    pair = _exchange("reduce_cores", ("c",), scat, [small_g])
    part = [_sum_slots("pair_sum_%d" % i, a.reshape(2, -1, a.shape[-1])).reshape(a.shape[1:]) for i, a in enumerate(pair)]
    lands = _exchange("reduce_chips", ("x", "y"), part[:7], [part[7]])

    small_w = [norm_in_w, rw_mu, rw_w0, rw_a0, rw_k_k, rw_k_a, rw_r_k, rw_gn_w, rw_gn_b, gd_A_log, gd_dt_bias, gd_o_norm_w, norm_out_w]
    small_m = [m_norm_in_w, m_rw_mu, m_rw_w0, m_rw_a0, m_rw_k_k, m_rw_k_a, m_rw_r_k, m_rw_gn_w, m_rw_gn_b, m_gd_A_log, m_gd_dt_bias, m_gd_o_norm_w, m_norm_out_w]
    small_v = [v_norm_in_w, v_rw_mu, v_rw_w0, v_rw_a0, v_rw_k_k, v_rw_k_a, v_rw_r_k, v_rw_gn_w, v_rw_gn_b, v_gd_A_log, v_gd_dt_bias, v_gd_o_norm_w, v_norm_out_w]
    small_shapes = [a.shape for a in small_w]
    sm = _adam("adam_small", lands[7], _pack_small(small_w), _pack_small(small_m), _pack_small(small_v))
    sm_g, sm_d, sm_m, sm_v = [dict(zip([nm for nm, _ in _SMALL], _unpack_small(p, small_shapes))) for p in sm]

    big = {}
    for nm, land, w, m, v in (("w_in", lands[0], w_in, m_w_in, v_w_in), ("rw_w2", lands[1], rw_w2, m_rw_w2, v_rw_w2),
                              ("rw_a2", lands[2], rw_a2, m_rw_a2, v_rw_a2),
                              ("gd_conv_w", lands[3], gd_conv_w, m_gd_conv_w, v_gd_conv_w),
                              ("w_branch_a", lands[4], w_branch_a, m_w_branch_a, v_w_branch_a),
                              ("w_branch_b", lands[5], w_branch_b, m_w_branch_b, v_w_branch_b),
                              ("w_out", lands[6], w_out, m_w_out, v_w_out)):
        big[nm] = [o.reshape(w.shape) for o in _adam("adam_" + nm, land, w[0], m[0], v[0])]

    order = ["norm_in_w", "w_in", "rw_mu", "rw_w0", "rw_w2", "rw_a0", "rw_a2", "rw_k_k", "rw_k_a", "rw_r_k", "rw_gn_w",
             "rw_gn_b", "gd_conv_w", "gd_A_log", "gd_dt_bias", "gd_o_norm_w", "w_branch_a", "w_branch_b", "w_out", "norm_out_w"]
    pick = lambda nm, i: big[nm][i] if nm in big else (sm_g, sm_d, sm_m, sm_v)[i][nm]
    loss = lax.psum(loss_acc[0, 0], ("x", "y", "c"))
    grad_x = dx2.reshape(x.shape)
    return (loss, grad_x, *[pick(nm, 0) for nm in order], *[pick(nm, 1) for nm in order],
            *[pick(nm, 2) for nm in order], *[pick(nm, 3) for nm in order])
```

```python
import functools

import jax
import jax.numpy as jnp
from jax import lax
from jax.experimental import pallas as pl
from jax.experimental.pallas import tpu as pltpu

F32 = jnp.float32
BF16 = jnp.bfloat16
HI = lax.Precision.HIGHEST

LANES = 128
SUB = 8
CHUNK = 64
N_DEV = 8
VMEM_LIMIT = 56 * 1024 * 1024

D_MODEL = 1024
RW_W = 512
GD_W = 512
RW_SHIFT = 1664
NORM_EPS = 1e-6
RW_GN_EPS = 64 * 1e-5
ADAM_LR, ADAM_B1, ADAM_B2, ADAM_EPS, ADAM_WD, ADAM_STEP = 0.001, 0.9, 0.999, 1e-8, 0.01, 10


_NN, _NT, _TN = ((1,), (0,)), ((1,), (1,)), ((0,), (0,))


def _dot(a, b, dims, passes):
    precision = lax.Precision.HIGH if passes == 3 else lax.Precision.DEFAULT
    return lax.dot_general(a, b, (dims, ((), ())), precision=precision, preferred_element_type=F32)


def _mm(a, b, passes=3):
    return _dot(a, b, _NN, passes)


def _mm_nt(a, b, passes=3):
    return _dot(a, b, _NT, passes)


def _mm_tn(a, b, passes=3):
    return _dot(a, b, _TN, passes)


P_SUM = 3
P_SCORE = 1
P_INV = 1
P_STATE = 1
P_APPLY = 1
P_UPDATE = 1


def _iota(shape, d):
    return lax.broadcasted_iota(jnp.int32, shape, d)


def _sigmoid(x):
    return 0.5 * (jnp.tanh(0.5 * x) + 1.0)


def _silu(x):
    return x * _sigmoid(x)


def _softplus(x):
    return jnp.maximum(x, 0.0) + jnp.log(1.0 + jnp.exp(-jnp.abs(x)))


def _seg_ones(seg):
    return ((_iota((LANES, LANES), 0) // seg) == (_iota((LANES, LANES), 1) // seg)).astype(F32)


def _sl(g):
    return slice(g * LANES, (g + 1) * LANES)


def _tri_inverse(ms):
    c = CHUNK
    ri, ci = _iota((c, c), 0), _iota((c, c), 1)
    eye = (ri == ci).astype(F32)
    d16 = (ri // 16) == (ci // 16)
    d32 = (ri // 32) == (ci // 32)
    ps = [jnp.where(d16, -m, 0.0) for m in ms]
    ts = [eye + p for p in ps]
    for _ in range(3):
        ps = [_mm(p, p, P_INV) for p in ps]
        ts = [_mm(t, eye + p, P_INV) for t, p in zip(ts, ps)]
    for off_diagonal in (d32 & (~d16), ~d32):
        tq = [_mm(t, jnp.where(off_diagonal, m, 0.0), P_INV) for t, m in zip(ts, ms)]
        ts = [t - _mm(a, t, P_INV) for t, a in zip(ts, tq)]
    return ts


def _chunk_fwd(prims, *, nsub, scalar_decay):
    c = CHUNK
    ng = len(prims)
    s0s, rs, lws, ks, vs, kks, bs = [list(t) for t in zip(*prims)]
    ri, ci = _iota((c, c), 0), _iota((c, c), 1)
    incl = ri >= ci
    strict = ri > ci
    tril = incl.astype(F32)
    hs = LANES // nsub
    lane = _iota((1, LANES), 1)
    masks = [((lane // hs) == s).astype(F32) for s in range(nsub)]
    cws = [_mm(tril, lw, P_SUM) for lw in lws]
    cwxs = [cw - lw for cw, lw in zip(cws, lws)]
    ends = [cw[c - 1:c, :] for cw in cws]
    kkds = [kk * jnp.exp(cwx) for kk, cwx in zip(kks, cwxs)]
    rds = [r * jnp.exp(cw) for r, cw in zip(rs, cws)]
    kends = [k * jnp.exp(e - cw) for k, e, cw in zip(ks, ends, cws)]
    bends = [b * jnp.exp(e - cw) for b, e, cw in zip(bs, ends, cws)]
    w0s = [_mm_nt(kkd, s0, P_STATE) for kkd, s0 in zip(kkds, s0s)]
    y0s = [_mm_nt(rd, s0, P_STATE) for rd, s0 in zip(rds, s0s)]
    chains = [(g, s) for g in range(ng) for s in range(nsub)]
    if scalar_decay:
        e0 = (lane == 0).astype(F32) * jnp.ones((c, 1), F32)
        rows = [_mm_nt(e0, cw, P_SUM) for cw in cws]
        dxs = [jnp.where(strict, jnp.exp(jnp.minimum(cwx[:, :c] - row, 0.0)), 0.0) for cwx, row in zip(cwxs, rows)]
        dis = [jnp.where(incl, jnp.exp(jnp.minimum(cw[:, :c] - row, 0.0)), 0.0) for cw, row in zip(cws, rows)]
        m_b = [_mm_nt(kks[g] * masks[s], bs[g], P_SCORE) * dxs[g] for g, s in chains]
        m_k = [_mm_nt(kks[g] * masks[s], ks[g], P_SCORE) * dxs[g] for g, s in chains]
        n_k = [_mm_nt(rs[g] * masks[s], ks[g], P_SCORE) * dis[g] for g, s in chains]
        n_b = [_mm_nt(rs[g] * masks[s], bs[g], P_SCORE) * dis[g] for g, s in chains]
    else:
        kds = [k * jnp.exp(-cw) for k, cw in zip(ks, cws)]
        bds = [b * jnp.exp(-cw) for b, cw in zip(bs, cws)]
        m_b = [jnp.where(strict, _mm_nt(kkds[g] * masks[s], bds[g], P_SCORE), 0.0) for g, s in chains]
        m_k = [jnp.where(strict, _mm_nt(kkds[g] * masks[s], kds[g], P_SCORE), 0.0) for g, s in chains]
        n_k = [jnp.where(incl, _mm_nt(rds[g] * masks[s], kds[g], P_SCORE), 0.0) for g, s in chains]
        n_b = [jnp.where(incl, _mm_nt(rds[g] * masks[s], bds[g], P_SCORE), 0.0) for g, s in chains]
    t_inv = _tri_inverse(m_b)
    rhs = [w0s[g] + _mm(mk, vs[g], P_APPLY) for (g, s), mk in zip(chains, m_k)]
    sa_c = [_mm(t, x, P_APPLY) for t, x in zip(t_inv, rhs)]
    y_c = [y0s[g] + _mm(nk, vs[g], P_APPLY) - _mm(nb, sa, P_APPLY) for (g, s), nk, nb, sa in zip(chains, n_k, n_b, sa_c)]
    sas = [sum(sa_c[g * nsub + s] * masks[s] for s in range(nsub)) for g in range(ng)]
    ys = [sum(y_c[g * nsub + s] * masks[s] for s in range(nsub)) for g in range(ng)]
    s_ends = [s0 * jnp.exp(e) + _mm_tn(v, kend, P_UPDATE) - _mm_tn(sa, bend, P_UPDATE)
              for s0, e, v, kend, sa, bend in zip(s0s, ends, vs, kends, sas, bends)]
    if nsub > 1:
        same_head = (_iota((LANES, LANES), 0) // hs) == (_iota((LANES, LANES), 1) // hs)
        s_ends = [jnp.where(same_head, s_end, 0.0) for s_end in s_ends]
    return list(zip(ys, s_ends))


def _rec_fwd(name, r, lw, k, v, kk, b, *, seq, nsub, scalar_decay):
    n, w = r.shape
    ng = w // LANES
    nc = seq // CHUNK
    nb = n // seq

    def body(r_ref, lw_ref, k_ref, v_ref, kk_ref, b_ref, y_ref, s_ref, state):
        @pl.when(pl.program_id(1) == 0)
        def _():
            state[...] = jnp.zeros_like(state)
        prims = [(state[g], r_ref[:, _sl(g)], lw_ref[:, _sl(g)], k_ref[:, _sl(g)], v_ref[:, _sl(g)],
                  kk_ref[:, _sl(g)], b_ref[:, _sl(g)]) for g in range(ng)]
        outs = _chunk_fwd(prims, nsub=nsub, scalar_decay=scalar_decay)
        for g, (y, s_end) in enumerate(outs):
            s_ref[0, g] = prims[g][0]
            y_ref[:, _sl(g)] = y
            state[g] = s_end

    row = pl.BlockSpec((CHUNK, w), lambda bi, c: (bi * nc + c, 0))
    return pl.pallas_call(
        body, name=name, grid=(nb, nc),
        in_specs=[row] * 6,
        out_specs=[row, pl.BlockSpec((1, ng, LANES, LANES), lambda bi, c: (bi * nc + c, 0, 0, 0))],
        out_shape=[jax.ShapeDtypeStruct((n, w), F32), jax.ShapeDtypeStruct((nb * nc, ng, LANES, LANES), F32)],
        scratch_shapes=[pltpu.VMEM((ng, LANES, LANES), F32)],
        compiler_params=pltpu.CompilerParams(dimension_semantics=("parallel", "arbitrary"), vmem_limit_bytes=VMEM_LIMIT),
    )(r, lw, k, v, kk, b)


def _rec_bwd(name, r, lw, k, v, kk, b, s_save, dy, *, seq, nsub, scalar_decay):
    n, w = r.shape
    ng = w // LANES
    nc = seq // CHUNK
    nb = n // seq

    def body(r_ref, lw_ref, k_ref, v_ref, kk_ref, b_ref, s_ref, dy_ref,
             dr_ref, dlw_ref, dk_ref, dv_ref, dkk_ref, db_ref, dstate):
        @pl.when(pl.program_id(1) == 0)
        def _():
            dstate[...] = jnp.zeros_like(dstate)
        f = functools.partial(_chunk_fwd, nsub=nsub, scalar_decay=scalar_decay)
        prims = [(s_ref[0, g], r_ref[:, _sl(g)], lw_ref[:, _sl(g)], k_ref[:, _sl(g)], v_ref[:, _sl(g)],
                  kk_ref[:, _sl(g)], b_ref[:, _sl(g)]) for g in range(ng)]
        _, vjp = jax.vjp(f, prims)
        (d_prims,) = vjp([(dy_ref[:, _sl(g)], dstate[g]) for g in range(ng)])
        for g, (ds0, dr, dlw, dk, dv, dkk, db) in enumerate(d_prims):
            dstate[g] = ds0
            dr_ref[:, _sl(g)] = dr
            dlw_ref[:, _sl(g)] = dlw
            dk_ref[:, _sl(g)] = dk
            dv_ref[:, _sl(g)] = dv
            dkk_ref[:, _sl(g)] = dkk
            db_ref[:, _sl(g)] = db

    row = pl.BlockSpec((CHUNK, w), lambda bi, c: (bi * nc + nc - 1 - c, 0))
    return pl.pallas_call(
        body, name=name, grid=(nb, nc),
        in_specs=[row] * 6 + [pl.BlockSpec((1, ng, LANES, LANES), lambda bi, c: (bi * nc + nc - 1 - c, 0, 0, 0)), row],
        out_specs=[row] * 6,
        out_shape=[jax.ShapeDtypeStruct((n, w), F32)] * 6,
        scratch_shapes=[pltpu.VMEM((ng, LANES, LANES), F32)],
        compiler_params=pltpu.CompilerParams(dimension_semantics=("parallel", "arbitrary"), vmem_limit_bytes=VMEM_LIMIT),
    )(r, lw, k, v, kk, b, s_save, dy)


def _shift_down(a, j, halo, is_start):
    tb = a.shape[0]
    rolled = pltpu.roll(a, j, 0)
    hr = jnp.where(is_start, 0.0, pltpu.roll(halo, j, 0))
    first = jnp.where(_iota((SUB, LANES), 0) < j, hr, rolled[0:SUB])
    if tb == SUB:
        return first
    return jnp.concatenate([first, rolled[SUB:]], axis=0)


def _shift_up(d, j, carry, is_end):
    tb = d.shape[0]
    up = pltpu.roll(d, tb - j, 0)
    cr = jnp.where(is_end, 0.0, pltpu.roll(carry, SUB - j, 0))
    last = jnp.where(_iota((SUB, LANES), 0) >= SUB - j, cr, up[tb - SUB:tb])
    if tb == SUB:
        return last
    return jnp.concatenate([up[:tb - SUB], last], axis=0)


def _ngroups(a):
    return a.shape[1] // LANES


def _pw_fwd(name, f, ins, shift, params, out_widths, out_dtypes, *, seq, tb):
    n = ins[0].shape[0]
    nt, tps = n // tb, seq // tb
    ni, npar = len(ins), len(params)

    def body(*refs):
        in_refs = refs[:ni]
        pos = ni
        halo_ref = None
        if shift:
            halo_ref = refs[pos]
            pos += 1
        p_refs = refs[pos:pos + npar]
        out_refs = refs[pos + npar:]
        is_start = (pl.program_id(0) % tps) == 0
        tiles = [[ref[:, _sl(g)] for g in range(_ngroups(ref))] for ref in in_refs]
        prevs = [[_shift_down(tiles[0][g], j, halo_ref[:, _sl(g)], is_start) for g in range(len(tiles[0]))]
                 for j in range(1, shift + 1)]
        pv = [[ref[:, _sl(g)] for g in range(_ngroups(ref))] for ref in p_refs]
        outs = f(tiles, prevs, pv)
        for o_ref, og in zip(out_refs, outs, strict=True):
            for g, t in enumerate(og):
                o_ref[:, _sl(g)] = t.astype(o_ref.dtype)

    in_specs = [pl.BlockSpec((tb, a.shape[1]), lambda i: (i, 0)) for a in ins]
    args = list(ins)
    if shift:
        in_specs.append(pl.BlockSpec((SUB, ins[0].shape[1]), lambda i: (jnp.maximum(i * (tb // SUB) - 1, 0), 0)))
        args.append(ins[0])
    in_specs += [pl.BlockSpec(p.shape, lambda i: (0, 0)) for p in params]
    args += list(params)
    return pl.pallas_call(
        body, name=name, grid=(nt,),
        in_specs=in_specs,
        out_specs=[pl.BlockSpec((tb, w), lambda i: (i, 0)) for w in out_widths],
        out_shape=[jax.ShapeDtypeStruct((n, w), dt) for w, dt in zip(out_widths, out_dtypes, strict=True)],
        compiler_params=pltpu.CompilerParams(dimension_semantics=("parallel",), vmem_limit_bytes=VMEM_LIMIT),
    )(*args)


def _pw_bwd(name, f, ins, shift, params, douts, din_dtypes, *, seq, tb):
    n = ins[0].shape[0]
    nt, tps = n // tb, seq // tb
    ni, npar = len(ins), len(params)
    flat_douts = [d for ds in douts for d in ds]
    nd = len(flat_douts)
    w0 = ins[0].shape[1]

    def body(*refs):
        in_refs = refs[:ni]
        pos = ni
        halo_ref = None
        if shift:
            halo_ref = refs[pos]
            pos += 1
        p_refs = refs[pos:pos + npar]
        pos += npar
        d_refs = refs[pos:pos + nd]
        pos += nd
        din_refs = refs[pos:pos + ni]
        pos += ni
        dp_refs = refs[pos:pos + npar]
        pos += npar
        carry = refs[pos] if shift else None
        step = pl.program_id(0)
        tile = nt - 1 - step
        is_start = (tile % tps) == 0
        is_end = (tile % tps) == tps - 1
        tiles = [[ref[:, _sl(g)] for g in range(_ngroups(ref))] for ref in in_refs]
        prevs = [[_shift_down(tiles[0][g], j, halo_ref[:, _sl(g)], is_start) for g in range(len(tiles[0]))]
                 for j in range(1, shift + 1)]
        pv = [[ref[:, _sl(g)] for g in range(_ngroups(ref))] for ref in p_refs]
        cot, pos_d = [], 0
        for ds in douts:
            grp = d_refs[pos_d:pos_d + len(ds)]
            pos_d += len(ds)
            cot.append([sum(ref[:, _sl(g)].astype(F32) for ref in grp) for g in range(_ngroups(grp[0]))])
        _, vjp = jax.vjp(f, tiles, prevs, pv)
        d_tiles, d_prevs, d_pv = vjp(cot)
        for g in range(len(tiles[0])):
            for j in range(1, shift + 1):
                d_tiles[0][g] = d_tiles[0][g] + _shift_up(d_prevs[j - 1][g], j, carry[j - 1, :, _sl(g)], is_end)
            for j in range(1, shift + 1):
                carry[j - 1, :, _sl(g)] = d_prevs[j - 1][g][0:SUB]
        for ref, dg in zip(din_refs, d_tiles, strict=True):
            for g, t in enumerate(dg):
                ref[:, _sl(g)] = t.astype(ref.dtype)

        @pl.when(step == 0)
        def _():
            for ref in dp_refs:
                ref[...] = jnp.zeros_like(ref)
        for ref, dg in zip(dp_refs, d_pv, strict=True):
            for g, t in enumerate(dg):
                ref[:, _sl(g)] += t

    rev = lambda i: (nt - 1 - i, 0)
    in_specs = [pl.BlockSpec((tb, a.shape[1]), rev) for a in ins]
    args = list(ins)
    if shift:
        in_specs.append(pl.BlockSpec((SUB, w0), lambda i: (jnp.maximum((nt - 1 - i) * (tb // SUB) - 1, 0), 0)))
        args.append(ins[0])
    in_specs += [pl.BlockSpec(p.shape, lambda i: (0, 0)) for p in params]
    args += list(params)
    in_specs += [pl.BlockSpec((tb, d.shape[1]), rev) for d in flat_douts]
    args += flat_douts
    out_specs = [pl.BlockSpec((tb, a.shape[1]), rev) for a in ins] + [pl.BlockSpec(p.shape, lambda i: (0, 0)) for p in params]
    out_shape = ([jax.ShapeDtypeStruct(a.shape, dt) for a, dt in zip(ins, din_dtypes, strict=True)]
                 + [jax.ShapeDtypeStruct(p.shape, F32) for p in params])
    res = pl.pallas_call(
        body, name=name, grid=(nt,),
        in_specs=in_specs, out_specs=out_specs, out_shape=out_shape,
        scratch_shapes=[pltpu.VMEM((shift, SUB, w0), F32)] if shift else [],
        compiler_params=pltpu.CompilerParams(dimension_semantics=("arbitrary",), vmem_limit_bytes=VMEM_LIMIT),
    )(*args)
    return res[:ni], res[ni:]


def _rwkv_prep_f(tiles, prevs, params):
    (p,), (prev,) = tiles, prevs
    mu, w0, w2p, a0, a2p, k_k, k_a = params
    xs = [p[g] + (prev[g] - p[g]) * mu[g] for g in range(13)]
    wdad = xs[12]
    tw = jnp.tanh(wdad)
    e64 = _seg_ones(64)
    r, lw, k2, v, kk, b = [], [], [], [], [], []
    for g in range(4):
        k_g = xs[4 + g]
        lo = w0[g] + _mm(tw, w2p[g])
        lw_g = -jnp.exp(-_softplus(-lo) - 0.5)
        a_g = _sigmoid(a0[g] + _mm(wdad, a2p[g]))
        kkp = k_g * k_k[g]
        kk_g = kkp * lax.rsqrt(_mm(kkp * kkp, e64) + 1e-12)
        r.append(xs[g])
        lw.append(lw_g)
        k2.append(k_g * (1.0 + (a_g - 1.0) * k_a[g]))
        v.append(xs[8 + g])
        kk.append(kk_g)
        b.append(kk_g * a_g)
    return [r, lw, k2, v, kk, b]


def _rwkv_post_f(tiles, prevs, params):
    yrec, r, k2, v, z = tiles
    gn_w, gn_b, r_k = params
    e64 = _seg_ones(64)
    out = []
    for g in range(4):
        mean = _mm(yrec[g], e64) * (1.0 / 64)
        d = yrec[g] - mean
        var = _mm(d * d, e64) * (1.0 / 64)
        yn = d * lax.rsqrt(var + RW_GN_EPS) * gn_w[g] + gn_b[g]
        bonus = _mm(r[g] * k2[g] * r_k[g], e64) * v[g]
        out.append((yn + bonus) * _silu(z[g]))
    return [out]


def _gdn_prep_f(tiles, prevs, params):
    x, (ba,) = tiles
    p1, p2, p3 = prevs
    cw0, cw1, cw2, cw3, a_log, dt_bias = params
    s = [_silu(cw3[g] * x[g] + cw2[g] * p1[g] + cw1[g] * p2[g] + cw0[g] * p3[g]) for g in range(12)]
    row = _iota((LANES, LANES), 0)
    r, lw, k, vv, b = [], [], [], [], []
    for h in range(4):
        q_h, k_h, v_h = s[h], s[4 + h], s[8 + h]
        qn = q_h * lax.rsqrt(jnp.sum(q_h * q_h, axis=-1, keepdims=True) + 1e-12)
        kn = k_h * lax.rsqrt(jnp.sum(k_h * k_h, axis=-1, keepdims=True) + 1e-12)
        beta = _sigmoid(_mm(ba, (row == h).astype(F32)))
        alpha = _mm(ba, (row == 4 + h).astype(F32))
        g_h = -jnp.exp(a_log[h]) * _softplus(alpha + dt_bias[h])
        r.append(qn * (LANES ** -0.5))
        lw.append(g_h)
        k.append(kn)
        vv.append(beta * v_h)
        b.append(jnp.exp(g_h) * beta * kn)
    return [r, lw, k, vv, b]


def _gdn_post_f(tiles, prevs, params):
    o, z = tiles
    ((onw,),) = params
    out = []
    for h in range(4):
        ms = jnp.mean(o[h] * o[h], axis=-1, keepdims=True)
        out.append(o[h] * lax.rsqrt(ms + NORM_EPS) * onw * _silu(z[h]))
    return [out]


def _norm_in(x2, g_in, *, tm):
    n = x2.shape[0]

    def body(x_ref, g_ref, h_ref):
        x = x_ref[...]
        rs = lax.rsqrt(jnp.mean(x * x, axis=-1, keepdims=True) + NORM_EPS)
        h_ref[...] = (x * rs * g_ref[...]).astype(BF16)

    return pl.pallas_call(
        body, name="norm_in", grid=(n // tm,),
        in_specs=[pl.BlockSpec((tm, D_MODEL), lambda i: (i, 0)), pl.BlockSpec((1, D_MODEL), lambda i: (0, 0))],
        out_specs=pl.BlockSpec((tm, D_MODEL), lambda i: (i, 0)),
        out_shape=jax.ShapeDtypeStruct((n, D_MODEL), BF16),
        compiler_params=pltpu.CompilerParams(dimension_semantics=("parallel",), vmem_limit_bytes=VMEM_LIMIT),
    )(x2, g_in)


def _proj(name, h, w, *, tm):
    n, ws = h.shape[0], w.shape[1]

    def body(h_ref, w_ref, o_ref):
        o_ref[...] = jnp.dot(h_ref[...], w_ref[...], preferred_element_type=F32)

    return pl.pallas_call(
        body, name=name, grid=(n // tm,),
        in_specs=[pl.BlockSpec((tm, D_MODEL), lambda i: (i, 0)), pl.BlockSpec((D_MODEL, ws), lambda i: (0, 0))],
        out_specs=pl.BlockSpec((tm, ws), lambda i: (i, 0)),
        out_shape=jax.ShapeDtypeStruct((n, ws), F32),
        compiler_params=pltpu.CompilerParams(dimension_semantics=("parallel",), vmem_limit_bytes=VMEM_LIMIT),
    )(h, w)


def _proj_dw(name, h, dp, *, tm):
    n, ws = dp.shape

    def body(h_ref, d_ref, o_ref):
        @pl.when(pl.program_id(0) == 0)
        def _():
            o_ref[...] = jnp.zeros_like(o_ref)
        o_ref[...] += lax.dot_general(h_ref[...], d_ref[...], (((0,), (0,)), ((), ())), preferred_element_type=F32)

    return pl.pallas_call(
        body, name=name, grid=(n // tm,),
        in_specs=[pl.BlockSpec((tm, D_MODEL), lambda i: (i, 0)), pl.BlockSpec((tm, ws), lambda i: (i, 0))],
        out_specs=pl.BlockSpec((D_MODEL, ws), lambda i: (0, 0)),
        out_shape=jax.ShapeDtypeStruct((D_MODEL, ws), F32),
        compiler_params=pltpu.CompilerParams(dimension_semantics=("arbitrary",), vmem_limit_bytes=VMEM_LIMIT),
    )(h, dp)


def _proj_dx(x2, g_in, d_xo, dps, ws, *, tm):
    n = x2.shape[0]
    ns = len(dps)

    def body(*refs):
        x_ref, g_ref, dxo_ref = refs[:3]
        dp_refs = refs[3:3 + ns]
        w_refs = refs[3 + ns:3 + 2 * ns]
        dx_ref, dg_ref = refs[3 + 2 * ns:]
        dh = jnp.zeros((tm, D_MODEL), F32)
        for d_ref, w_ref in zip(dp_refs, w_refs, strict=True):
            dh = dh + lax.dot_general(d_ref[...], w_ref[...], (((1,), (1,)), ((), ())), preferred_element_type=F32)
        x = x_ref[...]
        rs = lax.rsqrt(jnp.mean(x * x, axis=-1, keepdims=True) + NORM_EPS)
        xn = x * rs
        dxn = dh * g_ref[...]
        dx_ref[...] = dxo_ref[...] + rs * (dxn - xn * jnp.mean(dxn * xn, axis=-1, keepdims=True))

        @pl.when(pl.program_id(0) == 0)
        def _():
            dg_ref[...] = jnp.zeros_like(dg_ref)
        dg_ref[...] += jnp.sum(dh * xn, axis=0, keepdims=True)

    row = pl.BlockSpec((tm, D_MODEL), lambda i: (i, 0))
    return pl.pallas_call(
        body, name="proj_dx", grid=(n // tm,),
        in_specs=([row, pl.BlockSpec((1, D_MODEL), lambda i: (0, 0)), row]
                  + [pl.BlockSpec((tm, d.shape[1]), lambda i: (i, 0)) for d in dps]
                  + [pl.BlockSpec(w.shape, lambda i: (0, 0)) for w in ws]),
        out_specs=[row, pl.BlockSpec((1, D_MODEL), lambda i: (0, 0))],
        out_shape=[jax.ShapeDtypeStruct((n, D_MODEL), F32), jax.ShapeDtypeStruct((1, D_MODEL), F32)],
        compiler_params=pltpu.CompilerParams(dimension_semantics=("arbitrary",), vmem_limit_bytes=VMEM_LIMIT),
    )(x2, g_in, d_xo, *dps, *ws)


def _tail(x2, tgt2, gates, ya, yb, w_a, w_b, w_o, now, *, tr):
    n = x2.shape[0]

    def body(x_ref, t_ref, g_ref, ya_ref, yb_ref, wa_ref, wb_ref, wo_ref, now_ref,
             dya_ref, dyb_ref, dg_ref, dxo_ref, dwa_ref, dwb_ref, dwo_ref, dnow_ref, loss_ref):
        ya16, yb16 = ya_ref[...].astype(BF16), yb_ref[...].astype(BF16)
        ua = jnp.dot(ya16, wa_ref[...], preferred_element_type=F32)
        ub = jnp.dot(yb16, wb_ref[...], preferred_element_type=F32)
        ga = _sigmoid(g_ref[:, :D_MODEL])
        gb = _sigmoid(g_ref[:, D_MODEL:])
        m16 = (ga * ua + gb * ub).astype(BF16)
        xo = x_ref[...] + jnp.dot(m16, wo_ref[...], preferred_element_type=F32)
        rs = lax.rsqrt(jnp.mean(xo * xo, axis=-1, keepdims=True) + NORM_EPS)
        yn = xo * rs
        now_v = now_ref[...]
        err = yn * now_v - t_ref[...]
        dy = err * (1.0 / D_MODEL)
        dyn = dy * now_v
        dxo = rs * (dyn - yn * jnp.mean(dyn * yn, axis=-1, keepdims=True))
        dxo_ref[...] = dxo
        dxo16 = dxo.astype(BF16)
        dm = lax.dot_general(dxo16, wo_ref[...], (((1,), (1,)), ((), ())), preferred_element_type=F32)
        dua16 = (dm * ga).astype(BF16)
        dub16 = (dm * gb).astype(BF16)
        dg_ref[:, :D_MODEL] = (dm * ua * ga * (1.0 - ga)).astype(dg_ref.dtype)
        dg_ref[:, D_MODEL:] = (dm * ub * gb * (1.0 - gb)).astype(dg_ref.dtype)
        dya_ref[...] = lax.dot_general(dua16, wa_ref[...], (((1,), (1,)), ((), ())), preferred_element_type=F32)
        dyb_ref[...] = lax.dot_general(dub16, wb_ref[...], (((1,), (1,)), ((), ())), preferred_element_type=F32)

        @pl.when(pl.program_id(0) == 0)
        def _():
            for ref in (dwa_ref, dwb_ref, dwo_ref, dnow_ref, loss_ref):
                ref[...] = jnp.zeros_like(ref)
        tn = (((0,), (0,)), ((), ()))
        dwo_ref[...] += lax.dot_general(m16, dxo16, tn, preferred_element_type=F32)
        dwa_ref[...] += lax.dot_general(ya16, dua16, tn, preferred_element_type=F32)
        dwb_ref[...] += lax.dot_general(yb16, dub16, tn, preferred_element_type=F32)
        dnow_ref[...] += jnp.sum(dy * yn, axis=0, keepdims=True)
        loss_ref[...] += (0.5 / D_MODEL) * jnp.sum(err * err)

    row = lambda w: pl.BlockSpec((tr, w), lambda i: (i, 0))
    full = lambda a: pl.BlockSpec(a.shape, lambda i: (0, 0))
    return pl.pallas_call(
        body, name="tail", grid=(n // tr,),
        in_specs=[row(D_MODEL), row(D_MODEL), row(2 * D_MODEL), row(RW_W), row(GD_W), full(w_a), full(w_b), full(w_o), full(now)],
        out_specs=[row(RW_W), row(GD_W), row(2 * D_MODEL), row(D_MODEL),
                   pl.BlockSpec((RW_W, D_MODEL), lambda i: (0, 0)), pl.BlockSpec((GD_W, D_MODEL), lambda i: (0, 0)),
                   pl.BlockSpec((D_MODEL, D_MODEL), lambda i: (0, 0)), pl.BlockSpec((1, D_MODEL), lambda i: (0, 0)),
                   pl.BlockSpec((SUB, LANES), lambda i: (0, 0))],
        out_shape=[jax.ShapeDtypeStruct((n, RW_W), F32), jax.ShapeDtypeStruct((n, GD_W), F32),
                   jax.ShapeDtypeStruct((n, 2 * D_MODEL), BF16), jax.ShapeDtypeStruct((n, D_MODEL), F32),
                   jax.ShapeDtypeStruct((RW_W, D_MODEL), F32), jax.ShapeDtypeStruct((GD_W, D_MODEL), F32),
                   jax.ShapeDtypeStruct((D_MODEL, D_MODEL), F32), jax.ShapeDtypeStruct((1, D_MODEL), F32),
                   jax.ShapeDtypeStruct((SUB, LANES), F32)],
        compiler_params=pltpu.CompilerParams(dimension_semantics=("arbitrary",), vmem_limit_bytes=VMEM_LIMIT),
    )(x2, tgt2, gates, ya, yb, w_a, w_b, w_o, now)


def _exchange(name, axes, scatter, gather):
    ns, ng = len(scatter), len(gather)
    na = ns + ng
    gs = 2 ** len(axes)
    arrs = list(scatter) + list(gather)

    def body(*refs):
        src = refs[:na]
        dst = refs[na:2 * na]
        send_sems, recv_sems = refs[2 * na:]
        mine = {ax: lax.axis_index(ax) for ax in ("x", "y", "c")}

        def peer(k):
            co = dict(mine)
            for i, ax in enumerate(axes):
                if (k >> (len(axes) - 1 - i)) & 1:
                    co[ax] = 1 - co[ax]
            idx = 0
            for ax in axes:
                idx = 2 * idx + co[ax]
            return (co["x"], co["y"], co["c"]), idx

        _, me = peer(0)

        def copy(a, k, landing):
            dev, idx = peer(k)
            s = src[a].at[idx] if a < ns else src[a]
            return pltpu.make_async_remote_copy(src_ref=s, dst_ref=dst[a].at[idx if landing else me],
                                                send_sem=send_sems.at[a, k - 1], recv_sem=recv_sems.at[a, k - 1],
                                                device_id=dev, device_id_type=pl.DeviceIdType.MESH)

        sends = [copy(a, k, False) for a in range(na) for k in range(1, gs)]
        for cp in sends:
            cp.start()
        for a in range(na):
            for k in range(1, gs):
                copy(a, k, True).wait_recv()
        for cp in sends:
            cp.wait_send()

    out_shape = [jax.ShapeDtypeStruct(a.shape, a.dtype) for a in scatter] + \
                [jax.ShapeDtypeStruct((gs,) + a.shape, a.dtype) for a in gather]
    anyspec = pl.BlockSpec(memory_space=pl.ANY)
    lands = pl.pallas_call(
        body, name=name,
        in_specs=[anyspec] * na, out_specs=[anyspec] * na, out_shape=out_shape,
        scratch_shapes=[pltpu.SemaphoreType.DMA((na, gs - 1)), pltpu.SemaphoreType.DMA((na, gs - 1))],
    )(*arrs)
    me = 0
    for ax in axes:
        me = 2 * me + lax.axis_index(ax)
    kept = [lax.dynamic_index_in_dim(a, me, 0, keepdims=False) for a in scatter] + list(gather)
    return [lax.dynamic_update_index_in_dim(land, mine, me, 0) for land, mine in zip(lands, kept)]


def _sum_slots(name, land):
    ns, r, c = land.shape
    tr = 256 if (r % 256 == 0 and r > 256) else r

    def body(l_ref, o_ref):
        acc = l_ref[0]
        for s in range(1, ns):
            acc = acc + l_ref[s]
        o_ref[...] = acc

    return pl.pallas_call(
        body, name=name, grid=(r // tr,),
        in_specs=[pl.BlockSpec((ns, tr, c), lambda i: (0, i, 0))],
        out_specs=pl.BlockSpec((tr, c), lambda i: (i, 0)),
        out_shape=jax.ShapeDtypeStruct((r, c), F32),
        compiler_params=pltpu.CompilerParams(dimension_semantics=("parallel",), vmem_limit_bytes=VMEM_LIMIT),
    )(land)


def _adam(name, land, w, m, v):
    r, c = w.shape
    nslot = land.shape[0]
    tr = 256 if (r % 256 == 0 and r > 256) else r

    def body(l_ref, w_ref, m_ref, v_ref, g_out, d_out, m_out, v_out):
        g = l_ref[0]
        for s in range(1, nslot):
            g = g + l_ref[s]
        g_out[...] = g
        d_out[...], m_out[...], v_out[...] = _adam_math(g, w_ref[...], m_ref[...], v_ref[...])

    blk = pl.BlockSpec((tr, c), lambda i: (i, 0))
    return pl.pallas_call(
        body, name=name, grid=(r // tr,),
        in_specs=[pl.BlockSpec((nslot, tr, c), lambda i: (0, i, 0)), blk, blk, blk],
        out_specs=[blk] * 4,
        out_shape=[jax.ShapeDtypeStruct((r, c), F32)] * 4,
        compiler_params=pltpu.CompilerParams(dimension_semantics=("parallel",), vmem_limit_bytes=VMEM_LIMIT),
    )(land, w, m, v)


def _adam_math(g, w, m, v):
    c1 = 1.0 / (1.0 - ADAM_B1 ** ADAM_STEP)
    c2 = 1.0 / (1.0 - ADAM_B2 ** ADAM_STEP)
    m_new = ADAM_B1 * m + (1.0 - ADAM_B1) * g
    v_new = ADAM_B2 * v + (1.0 - ADAM_B2) * (g * g)
    return -ADAM_LR * ((m_new * c1) / (jnp.sqrt(v_new * c2) + ADAM_EPS) + ADAM_WD * w), m_new, v_new


def _adam_small(land, ws, ms, vs):
    npar = len(ws)
    nslot = land.shape[0]

    def body(*refs):
        l_ref = refs[0]
        w_refs, m_refs, v_refs = refs[1:1 + npar], refs[1 + npar:1 + 2 * npar], refs[1 + 2 * npar:1 + 3 * npar]
        outs = refs[1 + 3 * npar:1 + 7 * npar]
        g_rows = refs[1 + 7 * npar]
        g = l_ref[0]
        for s in range(1, nslot):
            g = g + l_ref[s]
        g_rows[...] = g
        row = 0
        for i, (_, size) in enumerate(_SMALL):
            for j in range(-(-size // LANES)):
                width = min(LANES, size - j * LANES)
                cols = slice(j * LANES, j * LANES + width)
                g_ij = g_rows[row:row + 1, 0:width]
                delta, m_new, v_new = _adam_math(g_ij, w_refs[i][:, cols], m_refs[i][:, cols], v_refs[i][:, cols])
                for ref, val in zip(outs[4 * i:4 * i + 4], (g_ij, delta, m_new, v_new)):
                    ref[:, cols] = val
                row += 1

    full = lambda a: pl.BlockSpec(a.shape, lambda: (0,) * a.ndim)
    res = pl.pallas_call(
        body, name="adam_small",
        in_specs=[full(land)] + [full(a) for a in list(ws) + list(ms) + list(vs)],
        out_specs=[full(w) for w in ws for _ in range(4)],
        out_shape=[jax.ShapeDtypeStruct(w.shape, F32) for w in ws for _ in range(4)],
        scratch_shapes=[pltpu.VMEM(land.shape[1:], F32)],
    )(land, *ws, *ms, *vs)
    return [res[4 * i:4 * i + 4] for i in range(npar)]


_SMALL = (("norm_in_w", 1024), ("rw_mu", 1664), ("rw_w0", 512), ("rw_a0", 512), ("rw_k_k", 512), ("rw_k_a", 512),
          ("rw_r_k", 512), ("rw_gn_w", 512), ("rw_gn_b", 512), ("gd_A_log", 4), ("gd_dt_bias", 4), ("gd_o_norm_w", 128),
          ("norm_out_w", 1024))
_SMALL_ROWS = 64


def _pack_small(vals):
    rows = []
    for (_, size), a in zip(_SMALL, vals, strict=True):
        flat = a.reshape(-1).astype(F32)
        pad = (-size) % LANES
        if pad:
            flat = jnp.concatenate([flat, jnp.zeros((pad,), F32)])
        rows.append(flat.reshape(-1, LANES))
    used = sum(r.shape[0] for r in rows)
    rows.append(jnp.zeros((_SMALL_ROWS - used, LANES), F32))
    return jnp.concatenate(rows, axis=0)


def kernel(x, norm_in_w, w_in, rw_mu, rw_w0, rw_w2, rw_a0, rw_a2, rw_k_k, rw_k_a, rw_r_k, rw_gn_w, rw_gn_b, gd_conv_w, gd_A_log, gd_dt_bias, gd_o_norm_w, w_branch_a, w_branch_b, w_out, norm_out_w, loss_target, m_norm_in_w, m_w_in, m_rw_mu, m_rw_w0, m_rw_w2, m_rw_a0, m_rw_a2, m_rw_k_k, m_rw_k_a, m_rw_r_k, m_rw_gn_w, m_rw_gn_b, m_gd_conv_w, m_gd_A_log, m_gd_dt_bias, m_gd_o_norm_w, m_w_branch_a, m_w_branch_b, m_w_out, m_norm_out_w, v_norm_in_w, v_w_in, v_rw_mu, v_rw_w0, v_rw_w2, v_rw_a0, v_rw_a2, v_rw_k_k, v_rw_k_a, v_rw_r_k, v_rw_gn_w, v_rw_gn_b, v_gd_conv_w, v_gd_A_log, v_gd_dt_bias, v_gd_o_norm_w, v_w_branch_a, v_w_branch_b, v_w_out, v_norm_out_w):
    nb, seq, _ = x.shape
    n = nb * seq
    tm = min(512, n)
    tb = min(256, seq)
    x2 = x.reshape(n, D_MODEL)
    tgt2 = loss_target.reshape(n, D_MODEL)
    cols = w_in.shape[2]
    in_cols = cols * N_DEV

    sharded = [w_in[0].astype(BF16), rw_w2[0], rw_a2[0], gd_conv_w[0], w_branch_a[0].astype(BF16),
               w_branch_b[0].astype(BF16), w_out[0].astype(BF16)]
    by_chip = _exchange("gather_chips", ("x", "y"), [], sharded)
    both = _exchange("gather_cores", ("c",), [], [a.reshape(-1, a.shape[-1]) for a in by_chip])
    g_win, g_w2, g_a2, g_conv, g_wa, g_wb, g_wo = [a.reshape(2, 4, -1, a.shape[-1]) for a in both]
    unshard_cols = lambda a: jnp.transpose(a, (2, 1, 0, 3)).reshape(a.shape[2], N_DEV * a.shape[3])
    w_full = unshard_cols(g_win)
    seg_bounds = ((0, 1664), (1664, 2176), (2176, 3712), (3712, 4224), (4232, in_cols))
    w_rw, w_zrw, w_qkv, w_zgd, w_gates = [w_full[:, a:b] for a, b in seg_bounds]
    w_ba = jnp.concatenate([w_full[:, 4224:4232], jnp.zeros((D_MODEL, LANES - 8), BF16)], axis=1)
    w2_full, a2_full = unshard_cols(g_w2), unshard_cols(g_a2)
    zeros64 = jnp.zeros((64, RW_W), F32)
    w2p = jnp.concatenate([w2_full, zeros64], axis=0)
    a2p = jnp.concatenate([zeros64, a2_full], axis=0)
    conv_full = unshard_cols(g_conv)
    conv_rows = [conv_full[i:i + 1] for i in range(4)]
    wa_full = unshard_cols(g_wa)
    wb_full = unshard_cols(g_wb)
    wo_full = jnp.transpose(g_wo, (1, 0, 2, 3)).reshape(D_MODEL, D_MODEL)
    a_log_bc = jnp.repeat(gd_A_log, LANES, axis=1)
    dt_bias_bc = jnp.repeat(gd_dt_bias, LANES, axis=1)
    r_k_flat = rw_r_k.reshape(1, RW_W)
    now2 = norm_out_w.reshape(1, D_MODEL)

    h = _norm_in(x2, norm_in_w, tm=tm)
    p_rw = _proj("proj_rw", h, w_rw, tm=tm)
    p_zrw = _proj("proj_zrw", h, w_zrw, tm=tm)
    p_qkv = _proj("proj_qkv", h, w_qkv, tm=tm)
    p_zgd = _proj("proj_zgd", h, w_zgd, tm=tm)
    p_ba = _proj("proj_ba", h, w_ba, tm=tm)
    p_gates = _proj("proj_gates", h, w_gates, tm=tm)

    rw_params = [rw_mu, rw_w0, w2p, rw_a0, a2p, rw_k_k, rw_k_a]
    r_a, lw_a, k_a, v_a, kk_a, b_a = _pw_fwd("rwkv_prep", _rwkv_prep_f, [p_rw], 1, rw_params, [RW_W] * 6, [F32] * 6,
                                             seq=seq, tb=tb)
    y_rec, s_a = _rec_fwd("rwkv_rec", r_a, lw_a, k_a, v_a, kk_a, b_a, seq=seq, nsub=2, scalar_decay=False)
    post_params = [rw_gn_w, rw_gn_b, r_k_flat]
    (y_a,) = _pw_fwd("rwkv_post", _rwkv_post_f, [y_rec, r_a, k_a, v_a, p_zrw], 0, post_params, [RW_W], [F32], seq=seq, tb=tb)

    gd_params = conv_rows + [a_log_bc, dt_bias_bc]
    r_b, lw_b, k_b, v_b, b_b = _pw_fwd("gdn_prep", _gdn_prep_f, [p_qkv, p_ba], 3, gd_params, [GD_W] * 5, [F32] * 5,
                                       seq=seq, tb=tb)
    o_rec, s_b = _rec_fwd("gdn_rec", r_b, lw_b, k_b, v_b, k_b, b_b, seq=seq, nsub=1, scalar_decay=True)
    (y_b,) = _pw_fwd("gdn_post", _gdn_post_f, [o_rec, p_zgd], 0, [gd_o_norm_w], [GD_W], [F32], seq=seq, tb=tb)

    d_ya, d_yb, d_gates, d_xo, dwa, dwb, dwo, d_now, loss_acc = _tail(
        x2, tgt2, p_gates, y_a, y_b, wa_full, wb_full, wo_full, now2, tr=min(256, n))

    (d_o, d_zgd), (d_onw,) = _pw_bwd("gdn_post_bwd", _gdn_post_f, [o_rec, p_zgd], 0, [gd_o_norm_w], [[d_yb]],
                                     [F32, BF16], seq=seq, tb=tb)
    dr_b, dlw_b, dk_b, dv_b, dkk_b, db_b = _rec_bwd("gdn_rec_bwd", r_b, lw_b, k_b, v_b, k_b, b_b, s_b, d_o,
                                                    seq=seq, nsub=1, scalar_decay=True)
    (d_qkv, d_ba), d_gd_params = _pw_bwd("gdn_prep_bwd", _gdn_prep_f, [p_qkv, p_ba], 3, gd_params,
                                         [[dr_b], [dlw_b], [dk_b, dkk_b], [dv_b], [db_b]], [BF16, BF16], seq=seq, tb=tb)

    (d_yrec, dr_p, dk_p, dv_p, d_zrw), d_post_params = _pw_bwd(
        "rwkv_post_bwd", _rwkv_post_f, [y_rec, r_a, k_a, v_a, p_zrw], 0, post_params, [[d_ya]],
        [F32, F32, F32, F32, BF16], seq=seq, tb=tb)
    dr_a, dlw_a, dk_a, dv_a, dkk_a, db_a = _rec_bwd("rwkv_rec_bwd", r_a, lw_a, k_a, v_a, kk_a, b_a, s_a, d_yrec,
                                                    seq=seq, nsub=2, scalar_decay=False)
    (d_prw,), d_rw_params = _pw_bwd("rwkv_prep_bwd", _rwkv_prep_f, [p_rw], 1, rw_params,
                                    [[dr_a, dr_p], [dlw_a], [dk_a, dk_p], [dv_a, dv_p], [dkk_a], [db_a]], [BF16],
                                    seq=seq, tb=tb)

    dps = [d_prw, d_zrw, d_qkv, d_zgd, d_ba, d_gates]
    wsegs = [w_rw, w_zrw, w_qkv, w_zgd, w_ba, w_gates]
    dx2, d_gin = _proj_dx(x2, norm_in_w, d_xo, dps, wsegs, tm=min(256, n))
    dw_rw = _proj_dw("dw_rw", h, d_prw, tm=tm)
    dw_zrw = _proj_dw("dw_zrw", h, d_zrw, tm=tm)
    dw_qkv = _proj_dw("dw_qkv", h, d_qkv, tm=tm)
    dw_zgd = _proj_dw("dw_zgd", h, d_zgd, tm=tm)
    dw_ba = _proj_dw("dw_ba", h, d_ba, tm=tm)
    dw_gates = _proj_dw("dw_gates", h, d_gates, tm=tm)
    dw_in_full = jnp.concatenate([dw_rw, dw_zrw, dw_qkv, dw_zgd, dw_ba[:, :8], dw_gates], axis=1)

    shard_cols = lambda a: jnp.transpose(a.reshape(a.shape[0], 4, 2, a.shape[1] // N_DEV), (2, 1, 0, 3))
    d_mu, d_w0, d_w2p, d_a0, d_a2p, d_kk_, d_ka_ = d_rw_params
    d_gnw, d_gnb, d_rk = d_post_params
    d_conv = jnp.concatenate(d_gd_params[:4], axis=0)
    d_alog = d_gd_params[4].reshape(4, LANES).sum(axis=1).reshape(1, 4)
    d_dtb = d_gd_params[5].reshape(4, LANES).sum(axis=1).reshape(1, 4)
    scat = [shard_cols(dw_in_full), shard_cols(d_w2p[:64]), shard_cols(d_a2p[64:]), shard_cols(d_conv),
            shard_cols(dwa), shard_cols(dwb),
            jnp.transpose(dwo.reshape(4, 2, D_MODEL // N_DEV, D_MODEL), (1, 0, 2, 3))]
    small_g = _pack_small([d_gin, d_mu, d_w0, d_a0, d_kk_, d_ka_, d_rk, d_gnw, d_gnb, d_alog, d_dtb, d_onw, d_now])
    pair = _exchange("reduce_cores", ("c",), [a.reshape(2, -1, a.shape[-1]) for a in scat], [small_g])
    part = [_sum_slots("pair_sum_%d" % i, a) for i, a in enumerate(pair)]
    lands = _exchange("reduce_chips", ("x", "y"), [p.reshape(s.shape[1:]) for p, s in zip(part[:7], scat)], [part[7]])

    small_w = [norm_in_w, rw_mu, rw_w0, rw_a0, rw_k_k, rw_k_a, rw_r_k, rw_gn_w, rw_gn_b, gd_A_log, gd_dt_bias, gd_o_norm_w, norm_out_w]
    small_m = [m_norm_in_w, m_rw_mu, m_rw_w0, m_rw_a0, m_rw_k_k, m_rw_k_a, m_rw_r_k, m_rw_gn_w, m_rw_gn_b, m_gd_A_log, m_gd_dt_bias, m_gd_o_norm_w, m_norm_out_w]
    small_v = [v_norm_in_w, v_rw_mu, v_rw_w0, v_rw_a0, v_rw_k_k, v_rw_k_a, v_rw_r_k, v_rw_gn_w, v_rw_gn_b, v_gd_A_log, v_gd_dt_bias, v_gd_o_norm_w, v_norm_out_w]
    flat = lambda arrs: [a.reshape(1, -1) for a in arrs]
    sm = _adam_small(lands[7], flat(small_w), flat(small_m), flat(small_v))
    sm_g, sm_d, sm_m, sm_v = [{nm: res[i].reshape(w.shape) for (nm, _), res, w in zip(_SMALL, sm, small_w)}
                              for i in range(4)]

    big = {}
    for nm, land, w, m, v in (("w_in", lands[0], w_in, m_w_in, v_w_in), ("rw_w2", lands[1], rw_w2, m_rw_w2, v_rw_w2),
                              ("rw_a2", lands[2], rw_a2, m_rw_a2, v_rw_a2),
                              ("gd_conv_w", lands[3], gd_conv_w, m_gd_conv_w, v_gd_conv_w),
                              ("w_branch_a", lands[4], w_branch_a, m_w_branch_a, v_w_branch_a),
                              ("w_branch_b", lands[5], w_branch_b, m_w_branch_b, v_w_branch_b),
                              ("w_out", lands[6], w_out, m_w_out, v_w_out)):
        big[nm] = [o.reshape(w.shape) for o in _adam("adam_" + nm, land, w[0], m[0], v[0])]

    order = ["norm_in_w", "w_in", "rw_mu", "rw_w0", "rw_w2", "rw_a0", "rw_a2", "rw_k_k", "rw_k_a", "rw_r_k", "rw_gn_w",
             "rw_gn_b", "gd_conv_w", "gd_A_log", "gd_dt_bias", "gd_o_norm_w", "w_branch_a", "w_branch_b", "w_out", "norm_out_w"]
    pick = lambda nm, i: big[nm][i] if nm in big else (sm_g, sm_d, sm_m, sm_v)[i][nm]
    loss = lax.psum(loss_acc[0, 0], ("x", "y", "c"))
    grad_x = dx2.reshape(x.shape)
    return (loss, grad_x, *[pick(nm, 0) for nm in order], *[pick(nm, 1) for nm in order],
            *[pick(nm, 2) for nm in order], *[pick(nm, 3) for nm in order])
```

```python
import functools

import jax
import jax.numpy as jnp
from jax import lax
from jax.experimental import pallas as pl
from jax.experimental.pallas import tpu as pltpu

F32 = jnp.float32
BF16 = jnp.bfloat16
HI = lax.Precision.HIGHEST

LANES = 128
SUB = 8
CHUNK = 64
N_DEV = 8
VMEM_LIMIT = 56 * 1024 * 1024

D_MODEL = 1024
RW_W = 512
GD_W = 512
RW_SHIFT = 1664
NORM_EPS = 1e-6
RW_GN_EPS = 64 * 1e-5
ADAM_LR, ADAM_B1, ADAM_B2, ADAM_EPS, ADAM_WD, ADAM_STEP = 0.001, 0.9, 0.999, 1e-8, 0.01, 10


_NN, _NT, _TN = ((1,), (0,)), ((1,), (1,)), ((0,), (0,))


def _dot(a, b, dims, passes):
    precision = lax.Precision.HIGH if passes == 3 else lax.Precision.DEFAULT
    return lax.dot_general(a, b, (dims, ((), ())), precision=precision, preferred_element_type=F32)


def _mm(a, b, passes=3):
    return _dot(a, b, _NN, passes)


def _mm_nt(a, b, passes=3):
    return _dot(a, b, _NT, passes)


def _mm_tn(a, b, passes=3):
    return _dot(a, b, _TN, passes)


P_SUM = 3
P_SCORE = 1
P_INV = 1
P_STATE = 1
P_APPLY = 1
P_UPDATE = 1


def _stack_rows(blocks):
    return jnp.concatenate(blocks, axis=0)


def _split_rows(x, n):
    r = x.shape[0] // n

    @jax.custom_vjp
    def split(x):
        return tuple(x[i * r:(i + 1) * r] for i in range(n))

    split.defvjp(lambda x: (split(x), None), lambda _, gs: (jnp.concatenate(gs, axis=0),))
    return split(x)


def _iota(shape, d):
    return lax.broadcasted_iota(jnp.int32, shape, d)


def _sigmoid(x):
    return 0.5 * (jnp.tanh(0.5 * x) + 1.0)


def _silu(x):
    return x * _sigmoid(x)


def _softplus(x):
    return jnp.maximum(x, 0.0) + jnp.log(1.0 + jnp.exp(-jnp.abs(x)))


def _seg_ones(seg):
    return ((_iota((LANES, LANES), 0) // seg) == (_iota((LANES, LANES), 1) // seg)).astype(F32)


def _sl(g):
    return slice(g * LANES, (g + 1) * LANES)


def _tri_inverse(ms):
    c = CHUNK
    ri, ci = _iota((c, c), 0), _iota((c, c), 1)
    eye = (ri == ci).astype(F32)
    d16 = (ri // 16) == (ci // 16)
    d32 = (ri // 32) == (ci // 32)
    ps = [jnp.where(d16, -m, 0.0) for m in ms]
    ts = [eye + p for p in ps]
    for _ in range(3):
        ps = [_mm(p, p, P_INV) for p in ps]
        ts = [_mm(t, eye + p, P_INV) for t, p in zip(ts, ps)]
    for off_diagonal in (d32 & (~d16), ~d32):
        tq = [_mm(t, jnp.where(off_diagonal, m, 0.0), P_INV) for t, m in zip(ts, ms)]
        ts = [t - _mm(a, t, P_INV) for t, a in zip(ts, tq)]
    return ts


def _chunk_fwd(prims, *, nsub, scalar_decay):
    c = CHUNK
    ng = len(prims)
    s0s, rs, lws, ks, vs, kks, bs = [list(t) for t in zip(*prims)]
    ri, ci = _iota((c, c), 0), _iota((c, c), 1)
    incl = ri >= ci
    strict = ri > ci
    tril = incl.astype(F32)
    hs = LANES // nsub
    lane = _iota((1, LANES), 1)
    masks = [((lane // hs) == s).astype(F32) for s in range(nsub)]
    cws = [_mm(tril, lw, P_SUM) for lw in lws]
    cwxs = [cw - lw for cw, lw in zip(cws, lws)]
    ends = [cw[c - 1:c, :] for cw in cws]
    kkds = [kk * jnp.exp(cwx) for kk, cwx in zip(kks, cwxs)]
    rds = [r * jnp.exp(cw) for r, cw in zip(rs, cws)]
    kends = [k * jnp.exp(e - cw) for k, e, cw in zip(ks, ends, cws)]
    bends = [b * jnp.exp(e - cw) for b, e, cw in zip(bs, ends, cws)]
    state_terms = [_split_rows(_mm_nt(_stack_rows([kkd, rd]), s0, P_STATE), 2) for kkd, rd, s0 in zip(kkds, rds, s0s)]
    w0s, y0s = [t[0] for t in state_terms], [t[1] for t in state_terms]
    chains = [(g, s) for g in range(ng) for s in range(nsub)]
    if scalar_decay:
        e0 = (lane == 0).astype(F32) * jnp.ones((c, 1), F32)
        rows = [_mm_nt(e0, cw, P_SUM) for cw in cws]
        dxs = [jnp.where(strict, jnp.exp(jnp.minimum(cwx[:, :c] - row, 0.0)), 0.0) for cwx, row in zip(cwxs, rows)]
        dis = [jnp.where(incl, jnp.exp(jnp.minimum(cw[:, :c] - row, 0.0)), 0.0) for cw, row in zip(cws, rows)]
        lefts = [_stack_rows([kk * m for m in masks] + [r * m for m in masks]) for kk, r in zip(kks, rs)]
        on_b = [_split_rows(_mm_nt(left, b, P_SCORE), 2 * nsub) for left, b in zip(lefts, bs)]
        on_k = [_split_rows(_mm_nt(left, k, P_SCORE), 2 * nsub) for left, k in zip(lefts, ks)]
        m_b = [on_b[g][s] * dxs[g] for g, s in chains]
        m_k = [on_k[g][s] * dxs[g] for g, s in chains]
        n_k = [on_k[g][nsub + s] * dis[g] for g, s in chains]
        n_b = [on_b[g][nsub + s] * dis[g] for g, s in chains]
    else:
        kds = [k * jnp.exp(-cw) for k, cw in zip(ks, cws)]
        bds = [b * jnp.exp(-cw) for b, cw in zip(bs, cws)]
        lefts = [_stack_rows([kkd * m for m in masks] + [rd * m for m in masks]) for kkd, rd in zip(kkds, rds)]
        on_b = [_split_rows(_mm_nt(left, bd, P_SCORE), 2 * nsub) for left, bd in zip(lefts, bds)]
        on_k = [_split_rows(_mm_nt(left, kd, P_SCORE), 2 * nsub) for left, kd in zip(lefts, kds)]
        m_b = [jnp.where(strict, on_b[g][s], 0.0) for g, s in chains]
        m_k = [jnp.where(strict, on_k[g][s], 0.0) for g, s in chains]
        n_k = [jnp.where(incl, on_k[g][nsub + s], 0.0) for g, s in chains]
        n_b = [jnp.where(incl, on_b[g][nsub + s], 0.0) for g, s in chains]
    t_inv = _tri_inverse(m_b)
    on_v = [_split_rows(_mm(_stack_rows([mk, nk]), vs[g], P_APPLY), 2) for (g, s), mk, nk in zip(chains, m_k, n_k)]
    sa_c = [_mm(t, w0s[g] + mv[0], P_APPLY) for (g, s), t, mv in zip(chains, t_inv, on_v)]
    y_c = [y0s[g] + mv[1] - _mm(nb, sa, P_APPLY) for (g, s), mv, nb, sa in zip(chains, on_v, n_b, sa_c)]
    sas = [sum(sa_c[g * nsub + s] * masks[s] for s in range(nsub)) for g in range(ng)]
    ys = [sum(y_c[g * nsub + s] * masks[s] for s in range(nsub)) for g in range(ng)]
    s_ends = [s0 * jnp.exp(e) + _mm_tn(_stack_rows([v, -sa]), _stack_rows([kend, bend]), P_UPDATE)
              for s0, e, v, kend, sa, bend in zip(s0s, ends, vs, kends, sas, bends)]
    if nsub > 1:
        same_head = (_iota((LANES, LANES), 0) // hs) == (_iota((LANES, LANES), 1) // hs)
        s_ends = [jnp.where(same_head, s_end, 0.0) for s_end in s_ends]
    return list(zip(ys, s_ends))


def _rec_fwd(name, r, lw, k, v, kk, b, *, seq, nsub, scalar_decay):
    n, w = r.shape
    ng = w // LANES
    nc = seq // CHUNK
    nb = n // seq

    def body(r_ref, lw_ref, k_ref, v_ref, kk_ref, b_ref, y_ref, s_ref, state):
        @pl.when(pl.program_id(0) == 0)
        def _():
            state[...] = jnp.zeros_like(state)
        prims = [(state[bi * ng + g], r_ref[bi, :, _sl(g)], lw_ref[bi, :, _sl(g)], k_ref[bi, :, _sl(g)],
                  v_ref[bi, :, _sl(g)], kk_ref[bi, :, _sl(g)], b_ref[bi, :, _sl(g)])
                 for bi in range(nb) for g in range(ng)]
        outs = _chunk_fwd(prims, nsub=nsub, scalar_decay=scalar_decay)
        for i, (y, s_end) in enumerate(outs):
            s_ref[0, i] = prims[i][0]
            y_ref[i // ng, :, _sl(i % ng)] = y
            state[i] = s_end

    row = pl.BlockSpec((nb, CHUNK, w), lambda c: (0, c, 0))
    seqs = lambda a: a.reshape(nb, seq, w)
    y, s_save = pl.pallas_call(
        body, name=name, grid=(nc,),
        in_specs=[row] * 6,
        out_specs=[row, pl.BlockSpec((1, nb * ng, LANES, LANES), lambda c: (c, 0, 0, 0))],
        out_shape=[jax.ShapeDtypeStruct((nb, seq, w), F32), jax.ShapeDtypeStruct((nc, nb * ng, LANES, LANES), F32)],
        scratch_shapes=[pltpu.VMEM((nb * ng, LANES, LANES), F32)],
        compiler_params=pltpu.CompilerParams(dimension_semantics=("arbitrary",), vmem_limit_bytes=VMEM_LIMIT),
    )(seqs(r), seqs(lw), seqs(k), seqs(v), seqs(kk), seqs(b))
    return y.reshape(n, w), s_save


def _rec_bwd(name, r, lw, k, v, kk, b, s_save, dy, *, seq, nsub, scalar_decay):
    n, w = r.shape
    ng = w // LANES
    nc = seq // CHUNK
    nb = n // seq

    def body(r_ref, lw_ref, k_ref, v_ref, kk_ref, b_ref, s_ref, dy_ref,
             dr_ref, dlw_ref, dk_ref, dv_ref, dkk_ref, db_ref, dstate):
        @pl.when(pl.program_id(0) == 0)
        def _():
            dstate[...] = jnp.zeros_like(dstate)
        f = functools.partial(_chunk_fwd, nsub=nsub, scalar_decay=scalar_decay)
        chains = [(bi, g) for bi in range(nb) for g in range(ng)]
        prims = [(s_ref[0, bi * ng + g], r_ref[bi, :, _sl(g)], lw_ref[bi, :, _sl(g)], k_ref[bi, :, _sl(g)],
                  v_ref[bi, :, _sl(g)], kk_ref[bi, :, _sl(g)], b_ref[bi, :, _sl(g)]) for bi, g in chains]
        _, vjp = jax.vjp(f, prims)
        (d_prims,) = vjp([(dy_ref[bi, :, _sl(g)], dstate[bi * ng + g]) for bi, g in chains])
        for (bi, g), (ds0, dr, dlw, dk, dv, dkk, db) in zip(chains, d_prims):
            dstate[bi * ng + g] = ds0
            dr_ref[bi, :, _sl(g)] = dr
            dlw_ref[bi, :, _sl(g)] = dlw
            dk_ref[bi, :, _sl(g)] = dk
            dv_ref[bi, :, _sl(g)] = dv
            dkk_ref[bi, :, _sl(g)] = dkk
            db_ref[bi, :, _sl(g)] = db

    row = pl.BlockSpec((nb, CHUNK, w), lambda c: (0, nc - 1 - c, 0))
    seqs = lambda a: a.reshape(nb, seq, w)
    grads = pl.pallas_call(
        body, name=name, grid=(nc,),
        in_specs=[row] * 6 + [pl.BlockSpec((1, nb * ng, LANES, LANES), lambda c: (nc - 1 - c, 0, 0, 0)), row],
        out_specs=[row] * 6,
        out_shape=[jax.ShapeDtypeStruct((nb, seq, w), F32)] * 6,
        scratch_shapes=[pltpu.VMEM((nb * ng, LANES, LANES), F32)],
        compiler_params=pltpu.CompilerParams(dimension_semantics=("arbitrary",), vmem_limit_bytes=VMEM_LIMIT),
    )(seqs(r), seqs(lw), seqs(k), seqs(v), seqs(kk), seqs(b), s_save, seqs(dy))
    return [g.reshape(n, w) for g in grads]


def _shift_down(a, j, halo, is_start):
    tb = a.shape[0]
    rolled = pltpu.roll(a, j, 0)
    hr = jnp.where(is_start, 0.0, pltpu.roll(halo, j, 0))
    first = jnp.where(_iota((SUB, LANES), 0) < j, hr, rolled[0:SUB])
    if tb == SUB:
        return first
    return jnp.concatenate([first, rolled[SUB:]], axis=0)


def _shift_up(d, j, carry, is_end):
    tb = d.shape[0]
    up = pltpu.roll(d, tb - j, 0)
    cr = jnp.where(is_end, 0.0, pltpu.roll(carry, SUB - j, 0))
    last = jnp.where(_iota((SUB, LANES), 0) >= SUB - j, cr, up[tb - SUB:tb])
    if tb == SUB:
        return last
    return jnp.concatenate([up[:tb - SUB], last], axis=0)


def _ngroups(a):
    return a.shape[1] // LANES


def _pw_fwd(name, f, ins, shift, params, out_widths, out_dtypes, *, seq, tb):
    n = ins[0].shape[0]
    nt, tps = n // tb, seq // tb
    ni, npar = len(ins), len(params)

    def body(*refs):
        in_refs = refs[:ni]
        pos = ni
        halo_ref = None
        if shift:
            halo_ref = refs[pos]
            pos += 1
        p_refs = refs[pos:pos + npar]
        out_refs = refs[pos + npar:]
        is_start = (pl.program_id(0) % tps) == 0
        tiles = [[ref[:, _sl(g)] for g in range(_ngroups(ref))] for ref in in_refs]
        prevs = [[_shift_down(tiles[0][g], j, halo_ref[:, _sl(g)], is_start) for g in range(len(tiles[0]))]
                 for j in range(1, shift + 1)]
        pv = [[ref[:, _sl(g)] for g in range(_ngroups(ref))] for ref in p_refs]
        outs = f(tiles, prevs, pv)
        for o_ref, og in zip(out_refs, outs, strict=True):
            for g, t in enumerate(og):
                o_ref[:, _sl(g)] = t.astype(o_ref.dtype)

    in_specs = [pl.BlockSpec((tb, a.shape[1]), lambda i: (i, 0)) for a in ins]
    args = list(ins)
    if shift:
        in_specs.append(pl.BlockSpec((SUB, ins[0].shape[1]), lambda i: (jnp.maximum(i * (tb // SUB) - 1, 0), 0)))
        args.append(ins[0])
    in_specs += [pl.BlockSpec(p.shape, lambda i: (0, 0)) for p in params]
    args += list(params)
    return pl.pallas_call(
        body, name=name, grid=(nt,),
        in_specs=in_specs,
        out_specs=[pl.BlockSpec((tb, w), lambda i: (i, 0)) for w in out_widths],
        out_shape=[jax.ShapeDtypeStruct((n, w), dt) for w, dt in zip(out_widths, out_dtypes, strict=True)],
        compiler_params=pltpu.CompilerParams(dimension_semantics=("parallel",), vmem_limit_bytes=VMEM_LIMIT),
    )(*args)


def _pw_bwd(name, f, ins, shift, params, douts, din_dtypes, *, seq, tb):
    n = ins[0].shape[0]
    nt, tps = n // tb, seq // tb
    ni, npar = len(ins), len(params)
    flat_douts = [d for ds in douts for d in ds]
    nd = len(flat_douts)
    w0 = ins[0].shape[1]

    def body(*refs):
        in_refs = refs[:ni]
        pos = ni
        halo_ref = None
        if shift:
            halo_ref = refs[pos]
            pos += 1
        p_refs = refs[pos:pos + npar]
        pos += npar
        d_refs = refs[pos:pos + nd]
        pos += nd
        din_refs = refs[pos:pos + ni]
        pos += ni
        dp_refs = refs[pos:pos + npar]
        pos += npar
        carry = refs[pos] if shift else None
        step = pl.program_id(0)
        tile = nt - 1 - step
        is_start = (tile % tps) == 0
        is_end = (tile % tps) == tps - 1
        tiles = [[ref[:, _sl(g)] for g in range(_ngroups(ref))] for ref in in_refs]
        prevs = [[_shift_down(tiles[0][g], j, halo_ref[:, _sl(g)], is_start) for g in range(len(tiles[0]))]
                 for j in range(1, shift + 1)]
        pv = [[ref[:, _sl(g)] for g in range(_ngroups(ref))] for ref in p_refs]
        cot, pos_d = [], 0
        for ds in douts:
            grp = d_refs[pos_d:pos_d + len(ds)]
            pos_d += len(ds)
            cot.append([sum(ref[:, _sl(g)].astype(F32) for ref in grp) for g in range(_ngroups(grp[0]))])
        _, vjp = jax.vjp(f, tiles, prevs, pv)
        d_tiles, d_prevs, d_pv = vjp(cot)
        for g in range(len(tiles[0])):
            for j in range(1, shift + 1):
                d_tiles[0][g] = d_tiles[0][g] + _shift_up(d_prevs[j - 1][g], j, carry[j - 1, :, _sl(g)], is_end)
            for j in range(1, shift + 1):
                carry[j - 1, :, _sl(g)] = d_prevs[j - 1][g][0:SUB]
        for ref, dg in zip(din_refs, d_tiles, strict=True):
            for g, t in enumerate(dg):
                ref[:, _sl(g)] = t.astype(ref.dtype)

        @pl.when(step == 0)
        def _():
            for ref in dp_refs:
                ref[...] = jnp.zeros_like(ref)
        for ref, dg in zip(dp_refs, d_pv, strict=True):
            for g, t in enumerate(dg):
                ref[:, _sl(g)] += t

    rev = lambda i: (nt - 1 - i, 0)
    in_specs = [pl.BlockSpec((tb, a.shape[1]), rev) for a in ins]
    args = list(ins)
    if shift:
        in_specs.append(pl.BlockSpec((SUB, w0), lambda i: (jnp.maximum((nt - 1 - i) * (tb // SUB) - 1, 0), 0)))
        args.append(ins[0])
    in_specs += [pl.BlockSpec(p.shape, lambda i: (0, 0)) for p in params]
    args += list(params)
    in_specs += [pl.BlockSpec((tb, d.shape[1]), rev) for d in flat_douts]
    args += flat_douts
    out_specs = [pl.BlockSpec((tb, a.shape[1]), rev) for a in ins] + [pl.BlockSpec(p.shape, lambda i: (0, 0)) for p in params]
    out_shape = ([jax.ShapeDtypeStruct(a.shape, dt) for a, dt in zip(ins, din_dtypes, strict=True)]
                 + [jax.ShapeDtypeStruct(p.shape, F32) for p in params])
    res = pl.pallas_call(
        body, name=name, grid=(nt,),
        in_specs=in_specs, out_specs=out_specs, out_shape=out_shape,
        scratch_shapes=[pltpu.VMEM((shift, SUB, w0), F32)] if shift else [],
        compiler_params=pltpu.CompilerParams(dimension_semantics=("arbitrary",), vmem_limit_bytes=VMEM_LIMIT),
    )(*args)
    return res[:ni], res[ni:]


def _rwkv_prep_f(tiles, prevs, params):
    (p,), (prev,) = tiles, prevs
    mu, w0, w2p, a0, a2p, k_k, k_a = params
    xs = [p[g] + (prev[g] - p[g]) * mu[g] for g in range(13)]
    wdad = xs[12]
    tw = jnp.tanh(wdad)
    e64 = _seg_ones(64)
    r, lw, k2, v, kk, b = [], [], [], [], [], []
    for g in range(4):
        k_g = xs[4 + g]
        lo = w0[g] + _mm(tw, w2p[g])
        lw_g = -jnp.exp(-_softplus(-lo) - 0.5)
        a_g = _sigmoid(a0[g] + _mm(wdad, a2p[g]))
        kkp = k_g * k_k[g]
        kk_g = kkp * lax.rsqrt(_mm(kkp * kkp, e64) + 1e-12)
        r.append(xs[g])
        lw.append(lw_g)
        k2.append(k_g * (1.0 + (a_g - 1.0) * k_a[g]))
        v.append(xs[8 + g])
        kk.append(kk_g)
        b.append(kk_g * a_g)
    return [r, lw, k2, v, kk, b]


def _rwkv_post_f(tiles, prevs, params):
    yrec, r, k2, v, z = tiles
    gn_w, gn_b, r_k = params
    e64 = _seg_ones(64)
    out = []
    for g in range(4):
        mean = _mm(yrec[g], e64) * (1.0 / 64)
        d = yrec[g] - mean
        var = _mm(d * d, e64) * (1.0 / 64)
        yn = d * lax.rsqrt(var + RW_GN_EPS) * gn_w[g] + gn_b[g]
        bonus = _mm(r[g] * k2[g] * r_k[g], e64) * v[g]
        out.append((yn + bonus) * _silu(z[g]))
    return [out]


def _gdn_prep_f(tiles, prevs, params):
    x, (ba,) = tiles
    p1, p2, p3 = prevs
    cw0, cw1, cw2, cw3, a_log, dt_bias = params
    s = [_silu(cw3[g] * x[g] + cw2[g] * p1[g] + cw1[g] * p2[g] + cw0[g] * p3[g]) for g in range(12)]
    row = _iota((LANES, LANES), 0)
    r, lw, k, vv, b = [], [], [], [], []
    for h in range(4):
        q_h, k_h, v_h = s[h], s[4 + h], s[8 + h]
        qn = q_h * lax.rsqrt(jnp.sum(q_h * q_h, axis=-1, keepdims=True) + 1e-12)
        kn = k_h * lax.rsqrt(jnp.sum(k_h * k_h, axis=-1, keepdims=True) + 1e-12)
        beta = _sigmoid(_mm(ba, (row == h).astype(F32)))
        alpha = _mm(ba, (row == 4 + h).astype(F32))
        g_h = -jnp.exp(a_log[h]) * _softplus(alpha + dt_bias[h])
        r.append(qn * (LANES ** -0.5))
        lw.append(g_h)
        k.append(kn)
        vv.append(beta * v_h)
        b.append(jnp.exp(g_h) * beta * kn)
    return [r, lw, k, vv, b]


def _gdn_post_f(tiles, prevs, params):
    o, z = tiles
    ((onw,),) = params
    out = []
    for h in range(4):
        ms = jnp.mean(o[h] * o[h], axis=-1, keepdims=True)
        out.append(o[h] * lax.rsqrt(ms + NORM_EPS) * onw * _silu(z[h]))
    return [out]


def _norm_in(x2, g_in, *, tm):
    n = x2.shape[0]

    def body(x_ref, g_ref, h_ref):
        x = x_ref[...]
        rs = lax.rsqrt(jnp.mean(x * x, axis=-1, keepdims=True) + NORM_EPS)
        h_ref[...] = (x * rs * g_ref[...]).astype(BF16)

    return pl.pallas_call(
        body, name="norm_in", grid=(n // tm,),
        in_specs=[pl.BlockSpec((tm, D_MODEL), lambda i: (i, 0)), pl.BlockSpec((1, D_MODEL), lambda i: (0, 0))],
        out_specs=pl.BlockSpec((tm, D_MODEL), lambda i: (i, 0)),
        out_shape=jax.ShapeDtypeStruct((n, D_MODEL), BF16),
        compiler_params=pltpu.CompilerParams(dimension_semantics=("parallel",), vmem_limit_bytes=VMEM_LIMIT),
    )(x2, g_in)


def _proj(name, h, w, *, tm):
    n, ws = h.shape[0], w.shape[1]

    def body(h_ref, w_ref, o_ref):
        o_ref[...] = jnp.dot(h_ref[...], w_ref[...], preferred_element_type=F32)

    return pl.pallas_call(
        body, name=name, grid=(n // tm,),
        in_specs=[pl.BlockSpec((tm, D_MODEL), lambda i: (i, 0)), pl.BlockSpec((D_MODEL, ws), lambda i: (0, 0))],
        out_specs=pl.BlockSpec((tm, ws), lambda i: (i, 0)),
        out_shape=jax.ShapeDtypeStruct((n, ws), F32),
        compiler_params=pltpu.CompilerParams(dimension_semantics=("parallel",), vmem_limit_bytes=VMEM_LIMIT),
    )(h, w)


def _proj_dw(name, h, dp, *, tm):
    n, ws = dp.shape

    def body(h_ref, d_ref, o_ref):
        @pl.when(pl.program_id(0) == 0)
        def _():
            o_ref[...] = jnp.zeros_like(o_ref)
        o_ref[...] += lax.dot_general(h_ref[...], d_ref[...], (((0,), (0,)), ((), ())), preferred_element_type=F32)

    return pl.pallas_call(
        body, name=name, grid=(n // tm,),
        in_specs=[pl.BlockSpec((tm, D_MODEL), lambda i: (i, 0)), pl.BlockSpec((tm, ws), lambda i: (i, 0))],
        out_specs=pl.BlockSpec((D_MODEL, ws), lambda i: (0, 0)),
        out_shape=jax.ShapeDtypeStruct((D_MODEL, ws), F32),
        compiler_params=pltpu.CompilerParams(dimension_semantics=("arbitrary",), vmem_limit_bytes=VMEM_LIMIT),
    )(h, dp)


def _proj_dx(x2, g_in, d_xo, dps, ws, *, tm):
    n = x2.shape[0]
    ns = len(dps)

    def body(*refs):
        x_ref, g_ref, dxo_ref = refs[:3]
        dp_refs = refs[3:3 + ns]
        w_refs = refs[3 + ns:3 + 2 * ns]
        dx_ref, dg_ref = refs[3 + 2 * ns:]
        dh = jnp.zeros((tm, D_MODEL), F32)
        for d_ref, w_ref in zip(dp_refs, w_refs, strict=True):
            dh = dh + lax.dot_general(d_ref[...], w_ref[...], (((1,), (1,)), ((), ())), preferred_element_type=F32)
        x = x_ref[...]
        rs = lax.rsqrt(jnp.mean(x * x, axis=-1, keepdims=True) + NORM_EPS)
        xn = x * rs
        dxn = dh * g_ref[...]
        dx_ref[...] = dxo_ref[...] + rs * (dxn - xn * jnp.mean(dxn * xn, axis=-1, keepdims=True))

        @pl.when(pl.program_id(0) == 0)
        def _():
            dg_ref[...] = jnp.zeros_like(dg_ref)
        dg_ref[...] += jnp.sum(dh * xn, axis=0, keepdims=True)

    row = pl.BlockSpec((tm, D_MODEL), lambda i: (i, 0))
    return pl.pallas_call(
        body, name="proj_dx", grid=(n // tm,),
        in_specs=([row, pl.BlockSpec((1, D_MODEL), lambda i: (0, 0)), row]
                  + [pl.BlockSpec((tm, d.shape[1]), lambda i: (i, 0)) for d in dps]
                  + [pl.BlockSpec(w.shape, lambda i: (0, 0)) for w in ws]),
        out_specs=[row, pl.BlockSpec((1, D_MODEL), lambda i: (0, 0))],
        out_shape=[jax.ShapeDtypeStruct((n, D_MODEL), F32), jax.ShapeDtypeStruct((1, D_MODEL), F32)],
        compiler_params=pltpu.CompilerParams(dimension_semantics=("arbitrary",), vmem_limit_bytes=VMEM_LIMIT),
    )(x2, g_in, d_xo, *dps, *ws)


def _tail(x2, tgt2, gates, ya, yb, w_a, w_b, w_o, now, *, tr):
    n = x2.shape[0]

    def body(x_ref, t_ref, g_ref, ya_ref, yb_ref, wa_ref, wb_ref, wo_ref, now_ref,
             dya_ref, dyb_ref, dg_ref, dxo_ref, dwa_ref, dwb_ref, dwo_ref, dnow_ref, loss_ref):
        ya16, yb16 = ya_ref[...].astype(BF16), yb_ref[...].astype(BF16)
        ua = jnp.dot(ya16, wa_ref[...], preferred_element_type=F32)
        ub = jnp.dot(yb16, wb_ref[...], preferred_element_type=F32)
        ga = _sigmoid(g_ref[:, :D_MODEL])
        gb = _sigmoid(g_ref[:, D_MODEL:])
        m16 = (ga * ua + gb * ub).astype(BF16)
        xo = x_ref[...] + jnp.dot(m16, wo_ref[...], preferred_element_type=F32)
        rs = lax.rsqrt(jnp.mean(xo * xo, axis=-1, keepdims=True) + NORM_EPS)
        yn = xo * rs
        now_v = now_ref[...]
        err = yn * now_v - t_ref[...]
        dy = err * (1.0 / D_MODEL)
        dyn = dy * now_v
        dxo = rs * (dyn - yn * jnp.mean(dyn * yn, axis=-1, keepdims=True))
        dxo_ref[...] = dxo
        dxo16 = dxo.astype(BF16)
        dm = lax.dot_general(dxo16, wo_ref[...], (((1,), (1,)), ((), ())), preferred_element_type=F32)
        dua16 = (dm * ga).astype(BF16)
        dub16 = (dm * gb).astype(BF16)
        dg_ref[:, :D_MODEL] = (dm * ua * ga * (1.0 - ga)).astype(dg_ref.dtype)
        dg_ref[:, D_MODEL:] = (dm * ub * gb * (1.0 - gb)).astype(dg_ref.dtype)
        dya_ref[...] = lax.dot_general(dua16, wa_ref[...], (((1,), (1,)), ((), ())), preferred_element_type=F32)
        dyb_ref[...] = lax.dot_general(dub16, wb_ref[...], (((1,), (1,)), ((), ())), preferred_element_type=F32)

        @pl.when(pl.program_id(0) == 0)
        def _():
            for ref in (dwa_ref, dwb_ref, dwo_ref, dnow_ref, loss_ref):
                ref[...] = jnp.zeros_like(ref)
        tn = (((0,), (0,)), ((), ()))
        dwo_ref[...] += lax.dot_general(m16, dxo16, tn, preferred_element_type=F32)
        dwa_ref[...] += lax.dot_general(ya16, dua16, tn, preferred_element_type=F32)
        dwb_ref[...] += lax.dot_general(yb16, dub16, tn, preferred_element_type=F32)
        dnow_ref[...] += jnp.sum(dy * yn, axis=0, keepdims=True)
        loss_ref[...] += (0.5 / D_MODEL) * jnp.sum(err * err)

    row = lambda w: pl.BlockSpec((tr, w), lambda i: (i, 0))
    full = lambda a: pl.BlockSpec(a.shape, lambda i: (0, 0))
    return pl.pallas_call(
        body, name="tail", grid=(n // tr,),
        in_specs=[row(D_MODEL), row(D_MODEL), row(2 * D_MODEL), row(RW_W), row(GD_W), full(w_a), full(w_b), full(w_o), full(now)],
        out_specs=[row(RW_W), row(GD_W), row(2 * D_MODEL), row(D_MODEL),
                   pl.BlockSpec((RW_W, D_MODEL), lambda i: (0, 0)), pl.BlockSpec((GD_W, D_MODEL), lambda i: (0, 0)),
                   pl.BlockSpec((D_MODEL, D_MODEL), lambda i: (0, 0)), pl.BlockSpec((1, D_MODEL), lambda i: (0, 0)),
                   pl.BlockSpec((SUB, LANES), lambda i: (0, 0))],
        out_shape=[jax.ShapeDtypeStruct((n, RW_W), F32), jax.ShapeDtypeStruct((n, GD_W), F32),
                   jax.ShapeDtypeStruct((n, 2 * D_MODEL), BF16), jax.ShapeDtypeStruct((n, D_MODEL), F32),
                   jax.ShapeDtypeStruct((RW_W, D_MODEL), F32), jax.ShapeDtypeStruct((GD_W, D_MODEL), F32),
                   jax.ShapeDtypeStruct((D_MODEL, D_MODEL), F32), jax.ShapeDtypeStruct((1, D_MODEL), F32),
                   jax.ShapeDtypeStruct((SUB, LANES), F32)],
        compiler_params=pltpu.CompilerParams(dimension_semantics=("arbitrary",), vmem_limit_bytes=VMEM_LIMIT),
    )(x2, tgt2, gates, ya, yb, w_a, w_b, w_o, now)


def _exchange(name, axes, scatter, gather):
    ns, ng = len(scatter), len(gather)
    na = ns + ng
    gs = 2 ** len(axes)
    arrs = list(scatter) + list(gather)

    def body(*refs):
        src = refs[:na]
        dst = refs[na:2 * na]
        send_sems, recv_sems = refs[2 * na:]
        mine = {ax: lax.axis_index(ax) for ax in ("x", "y", "c")}

        def peer(k):
            co = dict(mine)
            for i, ax in enumerate(axes):
                if (k >> (len(axes) - 1 - i)) & 1:
                    co[ax] = 1 - co[ax]
            idx = 0
            for ax in axes:
                idx = 2 * idx + co[ax]
            return (co["x"], co["y"], co["c"]), idx

        _, me = peer(0)

        def copy(a, k, landing):
            dev, idx = peer(k)
            s = src[a].at[idx] if a < ns else src[a]
            return pltpu.make_async_remote_copy(src_ref=s, dst_ref=dst[a].at[idx if landing else me],
                                                send_sem=send_sems.at[a, k - 1], recv_sem=recv_sems.at[a, k - 1],
                                                device_id=dev, device_id_type=pl.DeviceIdType.MESH)

        sends = [copy(a, k, False) for a in range(na) for k in range(1, gs)]
        for cp in sends:
            cp.start()
        for a in range(na):
            for k in range(1, gs):
                copy(a, k, True).wait_recv()
        for cp in sends:
            cp.wait_send()

    out_shape = [jax.ShapeDtypeStruct(a.shape, a.dtype) for a in scatter] + \
                [jax.ShapeDtypeStruct((gs,) + a.shape, a.dtype) for a in gather]
    anyspec = pl.BlockSpec(memory_space=pl.ANY)
    lands = pl.pallas_call(
        body, name=name,
        in_specs=[anyspec] * na, out_specs=[anyspec] * na, out_shape=out_shape,
        scratch_shapes=[pltpu.SemaphoreType.DMA((na, gs - 1)), pltpu.SemaphoreType.DMA((na, gs - 1))],
    )(*arrs)
    me = 0
    for ax in axes:
        me = 2 * me + lax.axis_index(ax)
    kept = [lax.dynamic_index_in_dim(a, me, 0, keepdims=False) for a in scatter] + list(gather)
    return [lax.dynamic_update_index_in_dim(land, mine, me, 0) for land, mine in zip(lands, kept)]


def _sum_slots(name, land):
    ns, r, c = land.shape
    tr = 256 if (r % 256 == 0 and r > 256) else r

    def body(l_ref, o_ref):
        acc = l_ref[0]
        for s in range(1, ns):
            acc = acc + l_ref[s]
        o_ref[...] = acc

    return pl.pallas_call(
        body, name=name, grid=(r // tr,),
        in_specs=[pl.BlockSpec((ns, tr, c), lambda i: (0, i, 0))],
        out_specs=pl.BlockSpec((tr, c), lambda i: (i, 0)),
        out_shape=jax.ShapeDtypeStruct((r, c), F32),
        compiler_params=pltpu.CompilerParams(dimension_semantics=("parallel",), vmem_limit_bytes=VMEM_LIMIT),
    )(land)


def _adam(name, land, w, m, v):
    r, c = w.shape
    nslot = land.shape[0]
    tr = 256 if (r % 256 == 0 and r > 256) else r

    def body(l_ref, w_ref, m_ref, v_ref, g_out, d_out, m_out, v_out):
        g = l_ref[0]
        for s in range(1, nslot):
            g = g + l_ref[s]
        g_out[...] = g
        d_out[...], m_out[...], v_out[...] = _adam_math(g, w_ref[...], m_ref[...], v_ref[...])

    blk = pl.BlockSpec((tr, c), lambda i: (i, 0))
    return pl.pallas_call(
        body, name=name, grid=(r // tr,),
        in_specs=[pl.BlockSpec((nslot, tr, c), lambda i: (0, i, 0)), blk, blk, blk],
        out_specs=[blk] * 4,
        out_shape=[jax.ShapeDtypeStruct((r, c), F32)] * 4,
        compiler_params=pltpu.CompilerParams(dimension_semantics=("parallel",), vmem_limit_bytes=VMEM_LIMIT),
    )(land, w, m, v)


def _adam_math(g, w, m, v):
    c1 = 1.0 / (1.0 - ADAM_B1 ** ADAM_STEP)
    c2 = 1.0 / (1.0 - ADAM_B2 ** ADAM_STEP)
    m_new = ADAM_B1 * m + (1.0 - ADAM_B1) * g
    v_new = ADAM_B2 * v + (1.0 - ADAM_B2) * (g * g)
    return -ADAM_LR * ((m_new * c1) / (jnp.sqrt(v_new * c2) + ADAM_EPS) + ADAM_WD * w), m_new, v_new


def _adam_small(land, ws, ms, vs):
    npar = len(ws)
    nslot = land.shape[0]

    def body(*refs):
        l_ref = refs[0]
        w_refs, m_refs, v_refs = refs[1:1 + npar], refs[1 + npar:1 + 2 * npar], refs[1 + 2 * npar:1 + 3 * npar]
        outs = refs[1 + 3 * npar:1 + 7 * npar]
        g_rows = refs[1 + 7 * npar]
        g = l_ref[0]
        for s in range(1, nslot):
            g = g + l_ref[s]
        g_rows[...] = g
        row = 0
        for i, (_, size) in enumerate(_SMALL):
            for j in range(-(-size // LANES)):
                width = min(LANES, size - j * LANES)
                cols = slice(j * LANES, j * LANES + width)
                g_ij = g_rows[row:row + 1, 0:width]
                delta, m_new, v_new = _adam_math(g_ij, w_refs[i][:, cols], m_refs[i][:, cols], v_refs[i][:, cols])
                for ref, val in zip(outs[4 * i:4 * i + 4], (g_ij, delta, m_new, v_new)):
                    ref[:, cols] = val
                row += 1

    full = lambda a: pl.BlockSpec(a.shape, lambda: (0,) * a.ndim)
    res = pl.pallas_call(
        body, name="adam_small",
        in_specs=[full(land)] + [full(a) for a in list(ws) + list(ms) + list(vs)],
        out_specs=[full(w) for w in ws for _ in range(4)],
        out_shape=[jax.ShapeDtypeStruct(w.shape, F32) for w in ws for _ in range(4)],
        scratch_shapes=[pltpu.VMEM(land.shape[1:], F32)],
    )(land, *ws, *ms, *vs)
    return [res[4 * i:4 * i + 4] for i in range(npar)]


_SMALL = (("norm_in_w", 1024), ("rw_mu", 1664), ("rw_w0", 512), ("rw_a0", 512), ("rw_k_k", 512), ("rw_k_a", 512),
          ("rw_r_k", 512), ("rw_gn_w", 512), ("rw_gn_b", 512), ("gd_A_log", 4), ("gd_dt_bias", 4), ("gd_o_norm_w", 128),
          ("norm_out_w", 1024))
_SMALL_ROWS = 64


def _pack_small(vals):
    rows = []
    for (_, size), a in zip(_SMALL, vals, strict=True):
        flat = a.reshape(-1).astype(F32)
        pad = (-size) % LANES
        if pad:
            flat = jnp.concatenate([flat, jnp.zeros((pad,), F32)])
        rows.append(flat.reshape(-1, LANES))
    used = sum(r.shape[0] for r in rows)
    rows.append(jnp.zeros((_SMALL_ROWS - used, LANES), F32))
    return jnp.concatenate(rows, axis=0)


def kernel(x, norm_in_w, w_in, rw_mu, rw_w0, rw_w2, rw_a0, rw_a2, rw_k_k, rw_k_a, rw_r_k, rw_gn_w, rw_gn_b, gd_conv_w, gd_A_log, gd_dt_bias, gd_o_norm_w, w_branch_a, w_branch_b, w_out, norm_out_w, loss_target, m_norm_in_w, m_w_in, m_rw_mu, m_rw_w0, m_rw_w2, m_rw_a0, m_rw_a2, m_rw_k_k, m_rw_k_a, m_rw_r_k, m_rw_gn_w, m_rw_gn_b, m_gd_conv_w, m_gd_A_log, m_gd_dt_bias, m_gd_o_norm_w, m_w_branch_a, m_w_branch_b, m_w_out, m_norm_out_w, v_norm_in_w, v_w_in, v_rw_mu, v_rw_w0, v_rw_w2, v_rw_a0, v_rw_a2, v_rw_k_k, v_rw_k_a, v_rw_r_k, v_rw_gn_w, v_rw_gn_b, v_gd_conv_w, v_gd_A_log, v_gd_dt_bias, v_gd_o_norm_w, v_w_branch_a, v_w_branch_b, v_w_out, v_norm_out_w):
    nb, seq, _ = x.shape
    n = nb * seq
    tm = min(512, n)
    tb = min(256, seq)
    x2 = x.reshape(n, D_MODEL)
    tgt2 = loss_target.reshape(n, D_MODEL)
    cols = w_in.shape[2]
    in_cols = cols * N_DEV

    sharded = [w_in[0].astype(BF16), rw_w2[0], rw_a2[0], gd_conv_w[0], w_branch_a[0].astype(BF16),
               w_branch_b[0].astype(BF16), w_out[0].astype(BF16)]
    by_chip = _exchange("gather_chips", ("x", "y"), [], sharded)
    both = _exchange("gather_cores", ("c",), [], [a.reshape(-1, a.shape[-1]) for a in by_chip])
    g_win, g_w2, g_a2, g_conv, g_wa, g_wb, g_wo = [a.reshape(2, 4, -1, a.shape[-1]) for a in both]
    unshard_cols = lambda a: jnp.transpose(a, (2, 1, 0, 3)).reshape(a.shape[2], N_DEV * a.shape[3])
    w_full = unshard_cols(g_win)
    seg_bounds = ((0, 1664), (1664, 2176), (2176, 3712), (3712, 4224), (4232, in_cols))
    w_rw, w_zrw, w_qkv, w_zgd, w_gates = [w_full[:, a:b] for a, b in seg_bounds]
    w_ba = jnp.concatenate([w_full[:, 4224:4232], jnp.zeros((D_MODEL, LANES - 8), BF16)], axis=1)
    w2_full, a2_full = unshard_cols(g_w2), unshard_cols(g_a2)
    zeros64 = jnp.zeros((64, RW_W), F32)
    w2p = jnp.concatenate([w2_full, zeros64], axis=0)
    a2p = jnp.concatenate([zeros64, a2_full], axis=0)
    conv_full = unshard_cols(g_conv)
    conv_rows = [conv_full[i:i + 1] for i in range(4)]
    wa_full = unshard_cols(g_wa)
    wb_full = unshard_cols(g_wb)
    wo_full = jnp.transpose(g_wo, (1, 0, 2, 3)).reshape(D_MODEL, D_MODEL)
    a_log_bc = jnp.repeat(gd_A_log, LANES, axis=1)
    dt_bias_bc = jnp.repeat(gd_dt_bias, LANES, axis=1)
    r_k_flat = rw_r_k.reshape(1, RW_W)
    now2 = norm_out_w.reshape(1, D_MODEL)

    h = _norm_in(x2, norm_in_w, tm=tm)
    p_rw = _proj("proj_rw", h, w_rw, tm=tm)
    p_zrw = _proj("proj_zrw", h, w_zrw, tm=tm)
    p_qkv = _proj("proj_qkv", h, w_qkv, tm=tm)
    p_zgd = _proj("proj_zgd", h, w_zgd, tm=tm)
    p_ba = _proj("proj_ba", h, w_ba, tm=tm)
    p_gates = _proj("proj_gates", h, w_gates, tm=tm)

    rw_params = [rw_mu, rw_w0, w2p, rw_a0, a2p, rw_k_k, rw_k_a]
    r_a, lw_a, k_a, v_a, kk_a, b_a = _pw_fwd("rwkv_prep", _rwkv_prep_f, [p_rw], 1, rw_params, [RW_W] * 6, [F32] * 6,
                                             seq=seq, tb=tb)
    y_rec, s_a = _rec_fwd("rwkv_rec", r_a, lw_a, k_a, v_a, kk_a, b_a, seq=seq, nsub=2, scalar_decay=False)
    post_params = [rw_gn_w, rw_gn_b, r_k_flat]
    (y_a,) = _pw_fwd("rwkv_post", _rwkv_post_f, [y_rec, r_a, k_a, v_a, p_zrw], 0, post_params, [RW_W], [F32], seq=seq, tb=tb)

    gd_params = conv_rows + [a_log_bc, dt_bias_bc]
    r_b, lw_b, k_b, v_b, b_b = _pw_fwd("gdn_prep", _gdn_prep_f, [p_qkv, p_ba], 3, gd_params, [GD_W] * 5, [F32] * 5,
                                       seq=seq, tb=tb)
    o_rec, s_b = _rec_fwd("gdn_rec", r_b, lw_b, k_b, v_b, k_b, b_b, seq=seq, nsub=1, scalar_decay=True)
    (y_b,) = _pw_fwd("gdn_post", _gdn_post_f, [o_rec, p_zgd], 0, [gd_o_norm_w], [GD_W], [F32], seq=seq, tb=tb)

    d_ya, d_yb, d_gates, d_xo, dwa, dwb, dwo, d_now, loss_acc = _tail(
        x2, tgt2, p_gates, y_a, y_b, wa_full, wb_full, wo_full, now2, tr=min(256, n))

    (d_o, d_zgd), (d_onw,) = _pw_bwd("gdn_post_bwd", _gdn_post_f, [o_rec, p_zgd], 0, [gd_o_norm_w], [[d_yb]],
                                     [F32, BF16], seq=seq, tb=tb)
    dr_b, dlw_b, dk_b, dv_b, dkk_b, db_b = _rec_bwd("gdn_rec_bwd", r_b, lw_b, k_b, v_b, k_b, b_b, s_b, d_o,
                                                    seq=seq, nsub=1, scalar_decay=True)
    (d_qkv, d_ba), d_gd_params = _pw_bwd("gdn_prep_bwd", _gdn_prep_f, [p_qkv, p_ba], 3, gd_params,
                                         [[dr_b], [dlw_b], [dk_b, dkk_b], [dv_b], [db_b]], [BF16, BF16], seq=seq, tb=tb)

    (d_yrec, dr_p, dk_p, dv_p, d_zrw), d_post_params = _pw_bwd(
        "rwkv_post_bwd", _rwkv_post_f, [y_rec, r_a, k_a, v_a, p_zrw], 0, post_params, [[d_ya]],
        [F32, F32, F32, F32, BF16], seq=seq, tb=tb)
    dr_a, dlw_a, dk_a, dv_a, dkk_a, db_a = _rec_bwd("rwkv_rec_bwd", r_a, lw_a, k_a, v_a, kk_a, b_a, s_a, d_yrec,
                                                    seq=seq, nsub=2, scalar_decay=False)
    (d_prw,), d_rw_params = _pw_bwd("rwkv_prep_bwd", _rwkv_prep_f, [p_rw], 1, rw_params,
                                    [[dr_a, dr_p], [dlw_a], [dk_a, dk_p], [dv_a, dv_p], [dkk_a], [db_a]], [BF16],
                                    seq=seq, tb=tb)

    dps = [d_prw, d_zrw, d_qkv, d_zgd, d_ba, d_gates]
    wsegs = [w_rw, w_zrw, w_qkv, w_zgd, w_ba, w_gates]
    dx2, d_gin = _proj_dx(x2, norm_in_w, d_xo, dps, wsegs, tm=min(256, n))
    dw_rw = _proj_dw("dw_rw", h, d_prw, tm=tm)
    dw_zrw = _proj_dw("dw_zrw", h, d_zrw, tm=tm)
    dw_qkv = _proj_dw("dw_qkv", h, d_qkv, tm=tm)
    dw_zgd = _proj_dw("dw_zgd", h, d_zgd, tm=tm)
    dw_ba = _proj_dw("dw_ba", h, d_ba, tm=tm)
    dw_gates = _proj_dw("dw_gates", h, d_gates, tm=tm)
    dw_in_full = jnp.concatenate([dw_rw, dw_zrw, dw_qkv, dw_zgd, dw_ba[:, :8], dw_gates], axis=1)

    shard_cols = lambda a: jnp.transpose(a.reshape(a.shape[0], 4, 2, a.shape[1] // N_DEV), (2, 1, 0, 3))
    d_mu, d_w0, d_w2p, d_a0, d_a2p, d_kk_, d_ka_ = d_rw_params
    d_gnw, d_gnb, d_rk = d_post_params
    d_conv = jnp.concatenate(d_gd_params[:4], axis=0)
    d_alog = d_gd_params[4].reshape(4, LANES).sum(axis=1).reshape(1, 4)
    d_dtb = d_gd_params[5].reshape(4, LANES).sum(axis=1).reshape(1, 4)
    scat = [shard_cols(dw_in_full), shard_cols(d_w2p[:64]), shard_cols(d_a2p[64:]), shard_cols(d_conv),
            shard_cols(dwa), shard_cols(dwb),
            jnp.transpose(dwo.reshape(4, 2, D_MODEL // N_DEV, D_MODEL), (1, 0, 2, 3))]
    small_g = _pack_small([d_gin, d_mu, d_w0, d_a0, d_kk_, d_ka_, d_rk, d_gnw, d_gnb, d_alog, d_dtb, d_onw, d_now])
    pair = _exchange("reduce_cores", ("c",), [a.reshape(2, -1, a.shape[-1]) for a in scat], [small_g])
    part = [_sum_slots("pair_sum_%d" % i, a) for i, a in enumerate(pair)]
    lands = _exchange("reduce_chips", ("x", "y"), [p.reshape(s.shape[1:]) for p, s in zip(part[:7], scat)], [part[7]])

    small_w = [norm_in_w, rw_mu, rw_w0, rw_a0, rw_k_k, rw_k_a, rw_r_k, rw_gn_w, rw_gn_b, gd_A_log, gd_dt_bias, gd_o_norm_w, norm_out_w]
    small_m = [m_norm_in_w, m_rw_mu, m_rw_w0, m_rw_a0, m_rw_k_k, m_rw_k_a, m_rw_r_k, m_rw_gn_w, m_rw_gn_b, m_gd_A_log, m_gd_dt_bias, m_gd_o_norm_w, m_norm_out_w]
    small_v = [v_norm_in_w, v_rw_mu, v_rw_w0, v_rw_a0, v_rw_k_k, v_rw_k_a, v_rw_r_k, v_rw_gn_w, v_rw_gn_b, v_gd_A_log, v_gd_dt_bias, v_gd_o_norm_w, v_norm_out_w]
    flat = lambda arrs: [a.reshape(1, -1) for a in arrs]
    sm = _adam_small(lands[7], flat(small_w), flat(small_m), flat(small_v))
    sm_g, sm_d, sm_m, sm_v = [{nm: res[i].reshape(w.shape) for (nm, _), res, w in zip(_SMALL, sm, small_w)}
                              for i in range(4)]

    big = {}
    for nm, land, w, m, v in (("w_in", lands[0], w_in, m_w_in, v_w_in), ("rw_w2", lands[1], rw_w2, m_rw_w2, v_rw_w2),
                              ("rw_a2", lands[2], rw_a2, m_rw_a2, v_rw_a2),
                              ("gd_conv_w", lands[3], gd_conv_w, m_gd_conv_w, v_gd_conv_w),
                              ("w_branch_a", lands[4], w_branch_a, m_w_branch_a, v_w_branch_a),
                              ("w_branch_b", lands[5], w_branch_b, m_w_branch_b, v_w_branch_b),
                              ("w_out", lands[6], w_out, m_w_out, v_w_out)):
        big[nm] = [o.reshape(w.shape) for o in _adam("adam_" + nm, land, w[0], m[0], v[0])]

    order = ["norm_in_w", "w_in", "rw_mu", "rw_w0", "rw_w2", "rw_a0", "rw_a2", "rw_k_k", "rw_k_a", "rw_r_k", "rw_gn_w",
             "rw_gn_b", "gd_conv_w", "gd_A_log", "gd_dt_bias", "gd_o_norm_w", "w_branch_a", "w_branch_b", "w_out", "norm_out_w"]
    pick = lambda nm, i: big[nm][i] if nm in big else (sm_g, sm_d, sm_m, sm_v)[i][nm]
    loss = lax.psum(loss_acc[0, 0], ("x", "y", "c"))
    grad_x = dx2.reshape(x.shape)
    return (loss, grad_x, *[pick(nm, 0) for nm in order], *[pick(nm, 1) for nm in order],
            *[pick(nm, 2) for nm in order], *[pick(nm, 3) for nm in order])
```

```python
import functools

import jax
import jax.numpy as jnp
from jax import lax
from jax.experimental import pallas as pl
from jax.experimental.pallas import tpu as pltpu

F32 = jnp.float32
BF16 = jnp.bfloat16
HI = lax.Precision.HIGHEST

LANES = 128
SUB = 8
CHUNK = 64
N_DEV = 8
VMEM_LIMIT = 56 * 1024 * 1024

D_MODEL = 1024
RW_W = 512
GD_W = 512
RW_SHIFT = 1664
NORM_EPS = 1e-6
RW_GN_EPS = 64 * 1e-5
ADAM_LR, ADAM_B1, ADAM_B2, ADAM_EPS, ADAM_WD, ADAM_STEP = 0.001, 0.9, 0.999, 1e-8, 0.01, 10


_NN, _NT, _TN = ((1,), (0,)), ((1,), (1,)), ((0,), (0,))


def _dot(a, b, dims, passes):
    precision = lax.Precision.HIGH if passes == 3 else lax.Precision.DEFAULT
    return lax.dot_general(a, b, (dims, ((), ())), precision=precision, preferred_element_type=F32)


def _mm(a, b, passes=3):
    return _dot(a, b, _NN, passes)


def _mm_nt(a, b, passes=3):
    return _dot(a, b, _NT, passes)


def _mm_tn(a, b, passes=3):
    return _dot(a, b, _TN, passes)


P_SUM = 3
P_SCORE = 1
P_INV = 1
P_STATE = 1
P_APPLY = 1
P_UPDATE = 1


def _stack_rows(blocks):
    return jnp.concatenate(blocks, axis=0)


def _split_rows(x, n):
    r = x.shape[0] // n

    @jax.custom_vjp
    def split(x):
        return tuple(x[i * r:(i + 1) * r] for i in range(n))

    split.defvjp(lambda x: (split(x), None), lambda _, gs: (jnp.concatenate(gs, axis=0),))
    return split(x)


def _iota(shape, d):
    return lax.broadcasted_iota(jnp.int32, shape, d)


def _sigmoid(x):
    return 0.5 * (jnp.tanh(0.5 * x) + 1.0)


def _silu(x):
    return x * _sigmoid(x)


def _softplus(x):
    return jnp.maximum(x, 0.0) + jnp.log(1.0 + jnp.exp(-jnp.abs(x)))


def _seg_ones(seg):
    return ((_iota((LANES, LANES), 0) // seg) == (_iota((LANES, LANES), 1) // seg)).astype(F32)


def _sl(g):
    return slice(g * LANES, (g + 1) * LANES)


def _tri_inverse(ms):
    c = CHUNK
    ri, ci = _iota((c, c), 0), _iota((c, c), 1)
    eye = (ri == ci).astype(F32)
    d16 = (ri // 16) == (ci // 16)
    d32 = (ri // 32) == (ci // 32)
    ps = [jnp.where(d16, -m, 0.0) for m in ms]
    ts = [eye + p for p in ps]
    for _ in range(3):
        ps = [_mm(p, p, P_INV) for p in ps]
        ts = [_mm(t, eye + p, P_INV) for t, p in zip(ts, ps)]
    for off_diagonal in (d32 & (~d16), ~d32):
        tq = [_mm(t, jnp.where(off_diagonal, m, 0.0), P_INV) for t, m in zip(ts, ms)]
        ts = [t - _mm(a, t, P_INV) for t, a in zip(ts, tq)]
    return ts


def _chunk_fwd(prims, *, nsub, scalar_decay):
    c = CHUNK
    ng = len(prims)
    s0s, rs, lws, ks, vs, kks, bs = [list(t) for t in zip(*prims)]
    ri, ci = _iota((c, c), 0), _iota((c, c), 1)
    incl = ri >= ci
    strict = ri > ci
    tril = incl.astype(F32)
    hs = LANES // nsub
    lane = _iota((1, LANES), 1)
    masks = [((lane // hs) == s).astype(F32) for s in range(nsub)]
    cws = [_mm(tril, lw, P_SUM) for lw in lws]
    cwxs = [cw - lw for cw, lw in zip(cws, lws)]
    ends = [cw[c - 1:c, :] for cw in cws]
    kkds = [kk * jnp.exp(cwx) for kk, cwx in zip(kks, cwxs)]
    rds = [r * jnp.exp(cw) for r, cw in zip(rs, cws)]
    kends = [k * jnp.exp(e - cw) for k, e, cw in zip(ks, ends, cws)]
    bends = [b * jnp.exp(e - cw) for b, e, cw in zip(bs, ends, cws)]
    state_terms = [_split_rows(_mm_nt(_stack_rows([kkd, rd]), s0, P_STATE), 2) for kkd, rd, s0 in zip(kkds, rds, s0s)]
    w0s, y0s = [t[0] for t in state_terms], [t[1] for t in state_terms]
    chains = [(g, s) for g in range(ng) for s in range(nsub)]
    if scalar_decay:
        e0 = (lane == 0).astype(F32) * jnp.ones((c, 1), F32)
        rows = [_mm_nt(e0, cw, P_SUM) for cw in cws]
        dxs = [jnp.where(strict, jnp.exp(jnp.minimum(cwx[:, :c] - row, 0.0)), 0.0) for cwx, row in zip(cwxs, rows)]
        dis = [jnp.where(incl, jnp.exp(jnp.minimum(cw[:, :c] - row, 0.0)), 0.0) for cw, row in zip(cws, rows)]
        lefts = [_stack_rows([kk * m for m in masks] + [r * m for m in masks]) for kk, r in zip(kks, rs)]
        on_b = [_split_rows(_mm_nt(left, b, P_SCORE), 2 * nsub) for left, b in zip(lefts, bs)]
        on_k = [_split_rows(_mm_nt(left, k, P_SCORE), 2 * nsub) for left, k in zip(lefts, ks)]
        m_b = [on_b[g][s] * dxs[g] for g, s in chains]
        m_k = [on_k[g][s] * dxs[g] for g, s in chains]
        n_k = [on_k[g][nsub + s] * dis[g] for g, s in chains]
        n_b = [on_b[g][nsub + s] * dis[g] for g, s in chains]
    else:
        kds = [k * jnp.exp(-cw) for k, cw in zip(ks, cws)]
        bds = [b * jnp.exp(-cw) for b, cw in zip(bs, cws)]
        lefts = [_stack_rows([kkd * m for m in masks] + [rd * m for m in masks]) for kkd, rd in zip(kkds, rds)]
        on_b = [_split_rows(_mm_nt(left, bd, P_SCORE), 2 * nsub) for left, bd in zip(lefts, bds)]
        on_k = [_split_rows(_mm_nt(left, kd, P_SCORE), 2 * nsub) for left, kd in zip(lefts, kds)]
        m_b = [jnp.where(strict, on_b[g][s], 0.0) for g, s in chains]
        m_k = [jnp.where(strict, on_k[g][s], 0.0) for g, s in chains]
        n_k = [jnp.where(incl, on_k[g][nsub + s], 0.0) for g, s in chains]
        n_b = [jnp.where(incl, on_b[g][nsub + s], 0.0) for g, s in chains]
    t_inv = _tri_inverse(m_b)
    on_v = [_split_rows(_mm(_stack_rows([mk, nk]), vs[g], P_APPLY), 2) for (g, s), mk, nk in zip(chains, m_k, n_k)]
    sa_c = [_mm(t, w0s[g] + mv[0], P_APPLY) for (g, s), t, mv in zip(chains, t_inv, on_v)]
    y_c = [y0s[g] + mv[1] - _mm(nb, sa, P_APPLY) for (g, s), mv, nb, sa in zip(chains, on_v, n_b, sa_c)]
    sas = [sum(sa_c[g * nsub + s] * masks[s] for s in range(nsub)) for g in range(ng)]
    ys = [sum(y_c[g * nsub + s] * masks[s] for s in range(nsub)) for g in range(ng)]
    s_ends = [s0 * jnp.exp(e) + _mm_tn(_stack_rows([v, -sa]), _stack_rows([kend, bend]), P_UPDATE)
              for s0, e, v, kend, sa, bend in zip(s0s, ends, vs, kends, sas, bends)]
    if nsub > 1:
        same_head = (_iota((LANES, LANES), 0) // hs) == (_iota((LANES, LANES), 1) // hs)
        s_ends = [jnp.where(same_head, s_end, 0.0) for s_end in s_ends]
    return list(zip(ys, s_ends))


def _rec_fwd(name, r, lw, k, v, kk, b, *, seq, nsub, scalar_decay):
    n, w = r.shape
    ng = w // LANES
    nc = seq // CHUNK
    nb = n // seq

    def body(r_ref, lw_ref, k_ref, v_ref, kk_ref, b_ref, y_ref, s_ref, state):
        @pl.when(pl.program_id(0) == 0)
        def _():
            state[...] = jnp.zeros_like(state)
        prims = [(state[bi * ng + g], r_ref[bi, :, _sl(g)], lw_ref[bi, :, _sl(g)], k_ref[bi, :, _sl(g)],
                  v_ref[bi, :, _sl(g)], kk_ref[bi, :, _sl(g)], b_ref[bi, :, _sl(g)])
                 for bi in range(nb) for g in range(ng)]
        outs = _chunk_fwd(prims, nsub=nsub, scalar_decay=scalar_decay)
        for i, (y, s_end) in enumerate(outs):
            s_ref[0, i] = prims[i][0]
            y_ref[i // ng, :, _sl(i % ng)] = y
            state[i] = s_end

    row = pl.BlockSpec((nb, CHUNK, w), lambda c: (0, c, 0))
    seqs = lambda a: a.reshape(nb, seq, w)
    y, s_save = pl.pallas_call(
        body, name=name, grid=(nc,),
        in_specs=[row] * 6,
        out_specs=[row, pl.BlockSpec((1, nb * ng, LANES, LANES), lambda c: (c, 0, 0, 0))],
        out_shape=[jax.ShapeDtypeStruct((nb, seq, w), F32), jax.ShapeDtypeStruct((nc, nb * ng, LANES, LANES), F32)],
        scratch_shapes=[pltpu.VMEM((nb * ng, LANES, LANES), F32)],
        compiler_params=pltpu.CompilerParams(dimension_semantics=("arbitrary",), vmem_limit_bytes=VMEM_LIMIT),
    )(seqs(r), seqs(lw), seqs(k), seqs(v), seqs(kk), seqs(b))
    return y.reshape(n, w), s_save


def _rec_bwd(name, r, lw, k, v, kk, b, s_save, dy, *, seq, nsub, scalar_decay):
    n, w = r.shape
    ng = w // LANES
    nc = seq // CHUNK
    nb = n // seq

    def body(r_ref, lw_ref, k_ref, v_ref, kk_ref, b_ref, s_ref, dy_ref,
             dr_ref, dlw_ref, dk_ref, dv_ref, dkk_ref, db_ref, dstate):
        @pl.when(pl.program_id(0) == 0)
        def _():
            dstate[...] = jnp.zeros_like(dstate)
        f = functools.partial(_chunk_fwd, nsub=nsub, scalar_decay=scalar_decay)
        chains = [(bi, g) for bi in range(nb) for g in range(ng)]
        prims = [(s_ref[0, bi * ng + g], r_ref[bi, :, _sl(g)], lw_ref[bi, :, _sl(g)], k_ref[bi, :, _sl(g)],
                  v_ref[bi, :, _sl(g)], kk_ref[bi, :, _sl(g)], b_ref[bi, :, _sl(g)]) for bi, g in chains]
        _, vjp = jax.vjp(f, prims)
        (d_prims,) = vjp([(dy_ref[bi, :, _sl(g)], dstate[bi * ng + g]) for bi, g in chains])
        for (bi, g), (ds0, dr, dlw, dk, dv, dkk, db) in zip(chains, d_prims):
            dstate[bi * ng + g] = ds0
            dr_ref[bi, :, _sl(g)] = dr
            dlw_ref[bi, :, _sl(g)] = dlw
            dk_ref[bi, :, _sl(g)] = dk
            dv_ref[bi, :, _sl(g)] = dv
            dkk_ref[bi, :, _sl(g)] = dkk
            db_ref[bi, :, _sl(g)] = db

    row = pl.BlockSpec((nb, CHUNK, w), lambda c: (0, nc - 1 - c, 0))
    seqs = lambda a: a.reshape(nb, seq, w)
    grads = pl.pallas_call(
        body, name=name, grid=(nc,),
        in_specs=[row] * 6 + [pl.BlockSpec((1, nb * ng, LANES, LANES), lambda c: (nc - 1 - c, 0, 0, 0)), row],
        out_specs=[row] * 6,
        out_shape=[jax.ShapeDtypeStruct((nb, seq, w), F32)] * 6,
        scratch_shapes=[pltpu.VMEM((nb * ng, LANES, LANES), F32)],
        compiler_params=pltpu.CompilerParams(dimension_semantics=("arbitrary",), vmem_limit_bytes=VMEM_LIMIT),
    )(seqs(r), seqs(lw), seqs(k), seqs(v), seqs(kk), seqs(b), s_save, seqs(dy))
    return [g.reshape(n, w) for g in grads]


def _shift_down(a, j, halo, is_start):
    tb = a.shape[0]
    rolled = pltpu.roll(a, j, 0)
    hr = jnp.where(is_start, 0.0, pltpu.roll(halo, j, 0))
    first = jnp.where(_iota((SUB, LANES), 0) < j, hr, rolled[0:SUB])
    if tb == SUB:
        return first
    return jnp.concatenate([first, rolled[SUB:]], axis=0)


def _shift_up(d, j, carry, is_end):
    tb = d.shape[0]
    up = pltpu.roll(d, tb - j, 0)
    cr = jnp.where(is_end, 0.0, pltpu.roll(carry, SUB - j, 0))
    last = jnp.where(_iota((SUB, LANES), 0) >= SUB - j, cr, up[tb - SUB:tb])
    if tb == SUB:
        return last
    return jnp.concatenate([up[:tb - SUB], last], axis=0)


def _ngroups(a):
    return a.shape[1] // LANES


def _pw_fwd(name, f, ins, shift, params, out_widths, out_dtypes, *, seq, tb):
    n = ins[0].shape[0]
    nt, tps = n // tb, seq // tb
    ni, npar = len(ins), len(params)

    def body(*refs):
        in_refs = refs[:ni]
        pos = ni
        halo_ref = None
        if shift:
            halo_ref = refs[pos]
            pos += 1
        p_refs = refs[pos:pos + npar]
        out_refs = refs[pos + npar:]
        is_start = (pl.program_id(0) % tps) == 0
        tiles = [[ref[:, _sl(g)] for g in range(_ngroups(ref))] for ref in in_refs]
        prevs = [[_shift_down(tiles[0][g], j, halo_ref[:, _sl(g)], is_start) for g in range(len(tiles[0]))]
                 for j in range(1, shift + 1)]
        pv = [[ref[:, _sl(g)] for g in range(_ngroups(ref))] for ref in p_refs]
        outs = f(tiles, prevs, pv)
        for o_ref, og in zip(out_refs, outs, strict=True):
            for g, t in enumerate(og):
                o_ref[:, _sl(g)] = t.astype(o_ref.dtype)

    in_specs = [pl.BlockSpec((tb, a.shape[1]), lambda i: (i, 0)) for a in ins]
    args = list(ins)
    if shift:
        in_specs.append(pl.BlockSpec((SUB, ins[0].shape[1]), lambda i: (jnp.maximum(i * (tb // SUB) - 1, 0), 0)))
        args.append(ins[0])
    in_specs += [pl.BlockSpec(p.shape, lambda i: (0, 0)) for p in params]
    args += list(params)
    return pl.pallas_call(
        body, name=name, grid=(nt,),
        in_specs=in_specs,
        out_specs=[pl.BlockSpec((tb, w), lambda i: (i, 0)) for w in out_widths],
        out_shape=[jax.ShapeDtypeStruct((n, w), dt) for w, dt in zip(out_widths, out_dtypes, strict=True)],
        compiler_params=pltpu.CompilerParams(dimension_semantics=("parallel",), vmem_limit_bytes=VMEM_LIMIT),
    )(*args)


def _pw_bwd(name, f, ins, shift, params, douts, din_dtypes, *, seq, tb):
    n = ins[0].shape[0]
    nt, tps = n // tb, seq // tb
    ni, npar = len(ins), len(params)
    flat_douts = [d for ds in douts for d in ds]
    nd = len(flat_douts)
    w0 = ins[0].shape[1]

    def body(*refs):
        in_refs = refs[:ni]
        pos = ni
        halo_ref = None
        if shift:
            halo_ref = refs[pos]
            pos += 1
        p_refs = refs[pos:pos + npar]
        pos += npar
        d_refs = refs[pos:pos + nd]
        pos += nd
        din_refs = refs[pos:pos + ni]
        pos += ni
        dp_refs = refs[pos:pos + npar]
        pos += npar
        carry = refs[pos] if shift else None
        step = pl.program_id(0)
        tile = nt - 1 - step
        is_start = (tile % tps) == 0
        is_end = (tile % tps) == tps - 1
        tiles = [[ref[:, _sl(g)] for g in range(_ngroups(ref))] for ref in in_refs]
        prevs = [[_shift_down(tiles[0][g], j, halo_ref[:, _sl(g)], is_start) for g in range(len(tiles[0]))]
                 for j in range(1, shift + 1)]
        pv = [[ref[:, _sl(g)] for g in range(_ngroups(ref))] for ref in p_refs]
        cot, pos_d = [], 0
        for ds in douts:
            grp = d_refs[pos_d:pos_d + len(ds)]
            pos_d += len(ds)
            cot.append([sum(ref[:, _sl(g)].astype(F32) for ref in grp) for g in range(_ngroups(grp[0]))])
        _, vjp = jax.vjp(f, tiles, prevs, pv)
        d_tiles, d_prevs, d_pv = vjp(cot)
        for g in range(len(tiles[0])):
            for j in range(1, shift + 1):
                d_tiles[0][g] = d_tiles[0][g] + _shift_up(d_prevs[j - 1][g], j, carry[j - 1, :, _sl(g)], is_end)
            for j in range(1, shift + 1):
                carry[j - 1, :, _sl(g)] = d_prevs[j - 1][g][0:SUB]
        for ref, dg in zip(din_refs, d_tiles, strict=True):
            for g, t in enumerate(dg):
                ref[:, _sl(g)] = t.astype(ref.dtype)

        @pl.when(step == 0)
        def _():
            for ref in dp_refs:
                ref[...] = jnp.zeros_like(ref)
        for ref, dg in zip(dp_refs, d_pv, strict=True):
            for g, t in enumerate(dg):
                ref[:, _sl(g)] += t

    rev = lambda i: (nt - 1 - i, 0)
    in_specs = [pl.BlockSpec((tb, a.shape[1]), rev) for a in ins]
    args = list(ins)
    if shift:
        in_specs.append(pl.BlockSpec((SUB, w0), lambda i: (jnp.maximum((nt - 1 - i) * (tb // SUB) - 1, 0), 0)))
        args.append(ins[0])
    in_specs += [pl.BlockSpec(p.shape, lambda i: (0, 0)) for p in params]
    args += list(params)
    in_specs += [pl.BlockSpec((tb, d.shape[1]), rev) for d in flat_douts]
    args += flat_douts
    out_specs = [pl.BlockSpec((tb, a.shape[1]), rev) for a in ins] + [pl.BlockSpec(p.shape, lambda i: (0, 0)) for p in params]
    out_shape = ([jax.ShapeDtypeStruct(a.shape, dt) for a, dt in zip(ins, din_dtypes, strict=True)]
                 + [jax.ShapeDtypeStruct(p.shape, F32) for p in params])
    res = pl.pallas_call(
        body, name=name, grid=(nt,),
        in_specs=in_specs, out_specs=out_specs, out_shape=out_shape,
        scratch_shapes=[pltpu.VMEM((shift, SUB, w0), F32)] if shift else [],
        compiler_params=pltpu.CompilerParams(dimension_semantics=("arbitrary",), vmem_limit_bytes=VMEM_LIMIT),
    )(*args)
    return res[:ni], res[ni:]


def _rwkv_prep_f(tiles, prevs, params):
    (p,), (prev,) = tiles, prevs
    mu, w0, w2p, a0, a2p, k_k, k_a = params
    xs = [p[g] + (prev[g] - p[g]) * mu[g] for g in range(13)]
    wdad = xs[12]
    tw = jnp.tanh(wdad)
    e64 = _seg_ones(64)
    r, lw, k2, v, kk, b = [], [], [], [], [], []
    for g in range(4):
        k_g = xs[4 + g]
        lo = w0[g] + _mm(tw, w2p[g])
        lw_g = -jnp.exp(-_softplus(-lo) - 0.5)
        a_g = _sigmoid(a0[g] + _mm(wdad, a2p[g]))
        kkp = k_g * k_k[g]
        kk_g = kkp * lax.rsqrt(_mm(kkp * kkp, e64) + 1e-12)
        r.append(xs[g])
        lw.append(lw_g)
        k2.append(k_g * (1.0 + (a_g - 1.0) * k_a[g]))
        v.append(xs[8 + g])
        kk.append(kk_g)
        b.append(kk_g * a_g)
    return [r, lw, k2, v, kk, b]


def _rwkv_post_f(tiles, prevs, params):
    yrec, r, k2, v, z = tiles
    gn_w, gn_b, r_k = params
    e64 = _seg_ones(64)
    out = []
    for g in range(4):
        mean = _mm(yrec[g], e64) * (1.0 / 64)
        d = yrec[g] - mean
        var = _mm(d * d, e64) * (1.0 / 64)
        yn = d * lax.rsqrt(var + RW_GN_EPS) * gn_w[g] + gn_b[g]
        bonus = _mm(r[g] * k2[g] * r_k[g], e64) * v[g]
        out.append((yn + bonus) * _silu(z[g]))
    return [out]


def _gdn_prep_f(tiles, prevs, params):
    x, (ba,) = tiles
    p1, p2, p3 = prevs
    cw0, cw1, cw2, cw3, a_log, dt_bias = params
    s = [_silu(cw3[g] * x[g] + cw2[g] * p1[g] + cw1[g] * p2[g] + cw0[g] * p3[g]) for g in range(12)]
    row = _iota((LANES, LANES), 0)
    r, lw, k, vv, b = [], [], [], [], []
    for h in range(4):
        q_h, k_h, v_h = s[h], s[4 + h], s[8 + h]
        qn = q_h * lax.rsqrt(jnp.sum(q_h * q_h, axis=-1, keepdims=True) + 1e-12)
        kn = k_h * lax.rsqrt(jnp.sum(k_h * k_h, axis=-1, keepdims=True) + 1e-12)
        beta = _sigmoid(_mm(ba, (row == h).astype(F32)))
        alpha = _mm(ba, (row == 4 + h).astype(F32))
        g_h = -jnp.exp(a_log[h]) * _softplus(alpha + dt_bias[h])
        r.append(qn * (LANES ** -0.5))
        lw.append(g_h)
        k.append(kn)
        vv.append(beta * v_h)
        b.append(jnp.exp(g_h) * beta * kn)
    return [r, lw, k, vv, b]


def _gdn_post_f(tiles, prevs, params):
    o, z = tiles
    ((onw,),) = params
    out = []
    for h in range(4):
        ms = jnp.mean(o[h] * o[h], axis=-1, keepdims=True)
        out.append(o[h] * lax.rsqrt(ms + NORM_EPS) * onw * _silu(z[h]))
    return [out]


def _norm_in(x2, g_in, *, tm):
    n = x2.shape[0]

    def body(x_ref, g_ref, h_ref):
        x = x_ref[...]
        rs = lax.rsqrt(jnp.mean(x * x, axis=-1, keepdims=True) + NORM_EPS)
        h_ref[...] = (x * rs * g_ref[...]).astype(BF16)

    return pl.pallas_call(
        body, name="norm_in", grid=(n // tm,),
        in_specs=[pl.BlockSpec((tm, D_MODEL), lambda i: (i, 0)), pl.BlockSpec((1, D_MODEL), lambda i: (0, 0))],
        out_specs=pl.BlockSpec((tm, D_MODEL), lambda i: (i, 0)),
        out_shape=jax.ShapeDtypeStruct((n, D_MODEL), BF16),
        compiler_params=pltpu.CompilerParams(dimension_semantics=("parallel",), vmem_limit_bytes=VMEM_LIMIT),
    )(x2, g_in)


def _proj(name, h, wt, *, tm):
    n, ws = h.shape[0], wt.shape[0]

    def body(h_ref, w_ref, o_ref):
        o_ref[...] = lax.dot_general(h_ref[...], w_ref[...], (_NT, ((), ())), preferred_element_type=F32)

    return pl.pallas_call(
        body, name=name, grid=(n // tm,),
        in_specs=[pl.BlockSpec((tm, D_MODEL), lambda i: (i, 0)), pl.BlockSpec((ws, D_MODEL), lambda i: (0, 0))],
        out_specs=pl.BlockSpec((tm, ws), lambda i: (i, 0)),
        out_shape=jax.ShapeDtypeStruct((n, ws), F32),
        compiler_params=pltpu.CompilerParams(dimension_semantics=("parallel",), vmem_limit_bytes=VMEM_LIMIT),
    )(h, wt)


def _proj_dw(name, h, dp, *, tm):
    n, ws = dp.shape

    def body(h_ref, d_ref, o_ref):
        @pl.when(pl.program_id(0) == 0)
        def _():
            o_ref[...] = jnp.zeros_like(o_ref)
        o_ref[...] += lax.dot_general(d_ref[...], h_ref[...], (_TN, ((), ())), preferred_element_type=F32)

    return pl.pallas_call(
        body, name=name, grid=(n // tm,),
        in_specs=[pl.BlockSpec((tm, D_MODEL), lambda i: (i, 0)), pl.BlockSpec((tm, ws), lambda i: (i, 0))],
        out_specs=pl.BlockSpec((ws, D_MODEL), lambda i: (0, 0)),
        out_shape=jax.ShapeDtypeStruct((ws, D_MODEL), F32),
        compiler_params=pltpu.CompilerParams(dimension_semantics=("arbitrary",), vmem_limit_bytes=VMEM_LIMIT),
    )(h, dp)


def _proj_dx(x2, g_in, d_xo, dps, ws, *, tm):
    n = x2.shape[0]
    ns = len(dps)

    def body(*refs):
        x_ref, g_ref, dxo_ref = refs[:3]
        dp_refs = refs[3:3 + ns]
        w_refs = refs[3 + ns:3 + 2 * ns]
        dx_ref, dg_ref = refs[3 + 2 * ns:]
        dh = jnp.zeros((tm, D_MODEL), F32)
        for d_ref, w_ref in zip(dp_refs, w_refs, strict=True):
            dh = dh + jnp.dot(d_ref[...], w_ref[...], preferred_element_type=F32)
        x = x_ref[...]
        rs = lax.rsqrt(jnp.mean(x * x, axis=-1, keepdims=True) + NORM_EPS)
        xn = x * rs
        dxn = dh * g_ref[...]
        dx_ref[...] = dxo_ref[...] + rs * (dxn - xn * jnp.mean(dxn * xn, axis=-1, keepdims=True))

        @pl.when(pl.program_id(0) == 0)
        def _():
            dg_ref[...] = jnp.zeros_like(dg_ref)
        dg_ref[...] += jnp.sum(dh * xn, axis=0, keepdims=True)

    row = pl.BlockSpec((tm, D_MODEL), lambda i: (i, 0))
    return pl.pallas_call(
        body, name="proj_dx", grid=(n // tm,),
        in_specs=([row, pl.BlockSpec((1, D_MODEL), lambda i: (0, 0)), row]
                  + [pl.BlockSpec((tm, d.shape[1]), lambda i: (i, 0)) for d in dps]
                  + [pl.BlockSpec(w.shape, lambda i: (0, 0)) for w in ws]),
        out_specs=[row, pl.BlockSpec((1, D_MODEL), lambda i: (0, 0))],
        out_shape=[jax.ShapeDtypeStruct((n, D_MODEL), F32), jax.ShapeDtypeStruct((1, D_MODEL), F32)],
        compiler_params=pltpu.CompilerParams(dimension_semantics=("arbitrary",), vmem_limit_bytes=VMEM_LIMIT),
    )(x2, g_in, d_xo, *dps, *ws)


def _tail(x2, tgt2, gates, ya, yb, w_a, w_b, w_o, now, *, tr):
    n = x2.shape[0]

    def body(x_ref, t_ref, g_ref, ya_ref, yb_ref, wa_ref, wb_ref, wo_ref, now_ref,
             dya_ref, dyb_ref, dg_ref, dxo_ref, dwa_ref, dwb_ref, dwo_ref, dnow_ref, loss_ref):
        ya16, yb16 = ya_ref[...].astype(BF16), yb_ref[...].astype(BF16)
        ua = jnp.dot(ya16, wa_ref[...], preferred_element_type=F32)
        ub = jnp.dot(yb16, wb_ref[...], preferred_element_type=F32)
        ga = _sigmoid(g_ref[:, :D_MODEL])
        gb = _sigmoid(g_ref[:, D_MODEL:])
        m16 = (ga * ua + gb * ub).astype(BF16)
        xo = x_ref[...] + jnp.dot(m16, wo_ref[...], preferred_element_type=F32)
        rs = lax.rsqrt(jnp.mean(xo * xo, axis=-1, keepdims=True) + NORM_EPS)
        yn = xo * rs
        now_v = now_ref[...]
        err = yn * now_v - t_ref[...]
        dy = err * (1.0 / D_MODEL)
        dyn = dy * now_v
        dxo = rs * (dyn - yn * jnp.mean(dyn * yn, axis=-1, keepdims=True))
        dxo_ref[...] = dxo
        dxo16 = dxo.astype(BF16)
        dm = lax.dot_general(dxo16, wo_ref[...], (((1,), (1,)), ((), ())), preferred_element_type=F32)
        dua16 = (dm * ga).astype(BF16)
        dub16 = (dm * gb).astype(BF16)
        dg_ref[:, :D_MODEL] = (dm * ua * ga * (1.0 - ga)).astype(dg_ref.dtype)
        dg_ref[:, D_MODEL:] = (dm * ub * gb * (1.0 - gb)).astype(dg_ref.dtype)
        dya_ref[...] = lax.dot_general(dua16, wa_ref[...], (((1,), (1,)), ((), ())), preferred_element_type=F32)
        dyb_ref[...] = lax.dot_general(dub16, wb_ref[...], (((1,), (1,)), ((), ())), preferred_element_type=F32)

        @pl.when(pl.program_id(0) == 0)
        def _():
            for ref in (dwa_ref, dwb_ref, dwo_ref, dnow_ref, loss_ref):
                ref[...] = jnp.zeros_like(ref)
        tn = (((0,), (0,)), ((), ()))
        dwo_ref[...] += lax.dot_general(m16, dxo16, tn, preferred_element_type=F32)
        dwa_ref[...] += lax.dot_general(ya16, dua16, tn, preferred_element_type=F32)
        dwb_ref[...] += lax.dot_general(yb16, dub16, tn, preferred_element_type=F32)
        dnow_ref[...] += jnp.sum(dy * yn, axis=0, keepdims=True)
        loss_ref[...] += (0.5 / D_MODEL) * jnp.sum(err * err)

    row = lambda w: pl.BlockSpec((tr, w), lambda i: (i, 0))
    full = lambda a: pl.BlockSpec(a.shape, lambda i: (0, 0))
    return pl.pallas_call(
        body, name="tail", grid=(n // tr,),
        in_specs=[row(D_MODEL), row(D_MODEL), row(2 * D_MODEL), row(RW_W), row(GD_W), full(w_a), full(w_b), full(w_o), full(now)],
        out_specs=[row(RW_W), row(GD_W), row(2 * D_MODEL), row(D_MODEL),
                   pl.BlockSpec((RW_W, D_MODEL), lambda i: (0, 0)), pl.BlockSpec((GD_W, D_MODEL), lambda i: (0, 0)),
                   pl.BlockSpec((D_MODEL, D_MODEL), lambda i: (0, 0)), pl.BlockSpec((1, D_MODEL), lambda i: (0, 0)),
                   pl.BlockSpec((SUB, LANES), lambda i: (0, 0))],
        out_shape=[jax.ShapeDtypeStruct((n, RW_W), F32), jax.ShapeDtypeStruct((n, GD_W), F32),
                   jax.ShapeDtypeStruct((n, 2 * D_MODEL), BF16), jax.ShapeDtypeStruct((n, D_MODEL), F32),
                   jax.ShapeDtypeStruct((RW_W, D_MODEL), F32), jax.ShapeDtypeStruct((GD_W, D_MODEL), F32),
                   jax.ShapeDtypeStruct((D_MODEL, D_MODEL), F32), jax.ShapeDtypeStruct((1, D_MODEL), F32),
                   jax.ShapeDtypeStruct((SUB, LANES), F32)],
        compiler_params=pltpu.CompilerParams(dimension_semantics=("arbitrary",), vmem_limit_bytes=VMEM_LIMIT),
    )(x2, tgt2, gates, ya, yb, w_a, w_b, w_o, now)


def _exchange(name, axes, scatter, gather):
    ns, ng = len(scatter), len(gather)
    na = ns + ng
    gs = 2 ** len(axes)
    arrs = list(scatter) + list(gather)

    def body(*refs):
        src = refs[:na]
        dst = refs[na:2 * na]
        send_sems, recv_sems = refs[2 * na:]
        mine = {ax: lax.axis_index(ax) for ax in ("x", "y", "c")}

        def peer(k):
            co = dict(mine)
            for i, ax in enumerate(axes):
                if (k >> (len(axes) - 1 - i)) & 1:
                    co[ax] = 1 - co[ax]
            idx = 0
            for ax in axes:
                idx = 2 * idx + co[ax]
            return (co["x"], co["y"], co["c"]), idx

        _, me = peer(0)

        def copy(a, k, landing):
            dev, idx = peer(k)
            s = src[a].at[idx] if a < ns else src[a]
            return pltpu.make_async_remote_copy(src_ref=s, dst_ref=dst[a].at[idx if landing else me],
                                                send_sem=send_sems.at[a, k - 1], recv_sem=recv_sems.at[a, k - 1],
                                                device_id=dev, device_id_type=pl.DeviceIdType.MESH)

        sends = [copy(a, k, False) for a in range(na) for k in range(1, gs)]
        for cp in sends:
            cp.start()
        for a in range(na):
            for k in range(1, gs):
                copy(a, k, True).wait_recv()
        for cp in sends:
            cp.wait_send()

    out_shape = [jax.ShapeDtypeStruct(a.shape, a.dtype) for a in scatter] + \
                [jax.ShapeDtypeStruct((gs,) + a.shape, a.dtype) for a in gather]
    anyspec = pl.BlockSpec(memory_space=pl.ANY)
    lands = pl.pallas_call(
        body, name=name,
        in_specs=[anyspec] * na, out_specs=[anyspec] * na, out_shape=out_shape,
        scratch_shapes=[pltpu.SemaphoreType.DMA((na, gs - 1)), pltpu.SemaphoreType.DMA((na, gs - 1))],
    )(*arrs)
    me = 0
    for ax in axes:
        me = 2 * me + lax.axis_index(ax)
    kept = [lax.dynamic_index_in_dim(a, me, 0, keepdims=False) for a in scatter] + list(gather)
    return [lax.dynamic_update_index_in_dim(land, mine, me, 0) for land, mine in zip(lands, kept)]


def _sum_slots(name, land, out_dtype):
    ns, nq, r, c = land.shape

    def body(l_ref, o_ref):
        acc = l_ref[0, 0].astype(F32)
        for s in range(1, ns):
            acc = acc + l_ref[s, 0].astype(F32)
        o_ref[0] = acc.astype(o_ref.dtype)

    return pl.pallas_call(
        body, name=name, grid=(nq,),
        in_specs=[pl.BlockSpec((ns, 1, r, c), lambda i: (0, i, 0, 0))],
        out_specs=pl.BlockSpec((1, r, c), lambda i: (i, 0, 0)),
        out_shape=jax.ShapeDtypeStruct((nq, r, c), out_dtype),
        compiler_params=pltpu.CompilerParams(dimension_semantics=("parallel",), vmem_limit_bytes=VMEM_LIMIT),
    )(land)


def _adam(name, land, w, m, v):
    r, c = w.shape
    nslot = land.shape[0]
    tr = 256 if (r % 256 == 0 and r > 256) else r

    def body(l_ref, w_ref, m_ref, v_ref, g_out, d_out, m_out, v_out):
        g = l_ref[0].astype(F32)
        for s in range(1, nslot):
            g = g + l_ref[s].astype(F32)
        g_out[...] = g
        d_out[...], m_out[...], v_out[...] = _adam_math(g, w_ref[...], m_ref[...], v_ref[...])

    blk = pl.BlockSpec((tr, c), lambda i: (i, 0))
    return pl.pallas_call(
        body, name=name, grid=(r // tr,),
        in_specs=[pl.BlockSpec((nslot, tr, c), lambda i: (0, i, 0)), blk, blk, blk],
        out_specs=[blk] * 4,
        out_shape=[jax.ShapeDtypeStruct((r, c), F32)] * 4,
        compiler_params=pltpu.CompilerParams(dimension_semantics=("parallel",), vmem_limit_bytes=VMEM_LIMIT),
    )(land, w, m, v)


def _adam_math(g, w, m, v):
    c1 = 1.0 / (1.0 - ADAM_B1 ** ADAM_STEP)
    c2 = 1.0 / (1.0 - ADAM_B2 ** ADAM_STEP)
    m_new = ADAM_B1 * m + (1.0 - ADAM_B1) * g
    v_new = ADAM_B2 * v + (1.0 - ADAM_B2) * (g * g)
    return -ADAM_LR * ((m_new * c1) / (jnp.sqrt(v_new * c2) + ADAM_EPS) + ADAM_WD * w), m_new, v_new


def _adam_small(land, ws, ms, vs):
    npar = len(ws)
    nslot = land.shape[0]

    def body(*refs):
        l_ref = refs[0]
        w_refs, m_refs, v_refs = refs[1:1 + npar], refs[1 + npar:1 + 2 * npar], refs[1 + 2 * npar:1 + 3 * npar]
        outs = refs[1 + 3 * npar:1 + 7 * npar]
        g_rows = refs[1 + 7 * npar]
        g = l_ref[0]
        for s in range(1, nslot):
            g = g + l_ref[s]
        g_rows[...] = g
        row = 0
        for i, (_, size) in enumerate(_SMALL):
            for j in range(-(-size // LANES)):
                width = min(LANES, size - j * LANES)
                cols = slice(j * LANES, j * LANES + width)
                g_ij = g_rows[row:row + 1, 0:width]
                delta, m_new, v_new = _adam_math(g_ij, w_refs[i][:, cols], m_refs[i][:, cols], v_refs[i][:, cols])
                for ref, val in zip(outs[4 * i:4 * i + 4], (g_ij, delta, m_new, v_new)):
                    ref[:, cols] = val
                row += 1

    full = lambda a: pl.BlockSpec(a.shape, lambda: (0,) * a.ndim)
    res = pl.pallas_call(
        body, name="adam_small",
        in_specs=[full(land)] + [full(a) for a in list(ws) + list(ms) + list(vs)],
        out_specs=[full(w) for w in ws for _ in range(4)],
        out_shape=[jax.ShapeDtypeStruct(w.shape, F32) for w in ws for _ in range(4)],
        scratch_shapes=[pltpu.VMEM(land.shape[1:], F32)],
    )(land, *ws, *ms, *vs)
    return [res[4 * i:4 * i + 4] for i in range(npar)]


_SMALL = (("norm_in_w", 1024), ("rw_mu", 1664), ("rw_w0", 512), ("rw_a0", 512), ("rw_k_k", 512), ("rw_k_a", 512),
          ("rw_r_k", 512), ("rw_gn_w", 512), ("rw_gn_b", 512), ("gd_A_log", 4), ("gd_dt_bias", 4), ("gd_o_norm_w", 128),
          ("norm_out_w", 1024))
_SMALL_ROWS = 64


def _pack_small(vals):
    rows = []
    for (_, size), a in zip(_SMALL, vals, strict=True):
        flat = a.reshape(-1).astype(F32)
        pad = (-size) % LANES
        if pad:
            flat = jnp.concatenate([flat, jnp.zeros((pad,), F32)])
        rows.append(flat.reshape(-1, LANES))
    used = sum(r.shape[0] for r in rows)
    rows.append(jnp.zeros((_SMALL_ROWS - used, LANES), F32))
    return jnp.concatenate(rows, axis=0)


def kernel(x, norm_in_w, w_in, rw_mu, rw_w0, rw_w2, rw_a0, rw_a2, rw_k_k, rw_k_a, rw_r_k, rw_gn_w, rw_gn_b, gd_conv_w, gd_A_log, gd_dt_bias, gd_o_norm_w, w_branch_a, w_branch_b, w_out, norm_out_w, loss_target, m_norm_in_w, m_w_in, m_rw_mu, m_rw_w0, m_rw_w2, m_rw_a0, m_rw_a2, m_rw_k_k, m_rw_k_a, m_rw_r_k, m_rw_gn_w, m_rw_gn_b, m_gd_conv_w, m_gd_A_log, m_gd_dt_bias, m_gd_o_norm_w, m_w_branch_a, m_w_branch_b, m_w_out, m_norm_out_w, v_norm_in_w, v_w_in, v_rw_mu, v_rw_w0, v_rw_w2, v_rw_a0, v_rw_a2, v_rw_k_k, v_rw_k_a, v_rw_r_k, v_rw_gn_w, v_rw_gn_b, v_gd_conv_w, v_gd_A_log, v_gd_dt_bias, v_gd_o_norm_w, v_w_branch_a, v_w_branch_b, v_w_out, v_norm_out_w):
    nb, seq, _ = x.shape
    n = nb * seq
    tm = min(512, n)
    tb = min(256, seq)
    x2 = x.reshape(n, D_MODEL)
    tgt2 = loss_target.reshape(n, D_MODEL)
    cols = w_in.shape[2]
    in_cols = cols * N_DEV

    sharded = [w_in[0].T.astype(BF16), rw_w2[0], rw_a2[0], gd_conv_w[0], w_branch_a[0].astype(BF16),
               w_branch_b[0].astype(BF16), w_out[0].astype(BF16)]
    by_chip = _exchange("gather_chips", ("x", "y"), [], sharded)
    g_win, g_w2, g_a2, g_conv, g_wa, g_wb, g_wo = _exchange("gather_cores", ("c",), [], by_chip)
    unshard_rows = lambda a: jnp.transpose(a, (1, 0, 2, 3)).reshape(N_DEV * a.shape[2], a.shape[3])
    unshard_cols = lambda a: jnp.transpose(a, (2, 1, 0, 3)).reshape(a.shape[2], N_DEV * a.shape[3])
    wt_full = unshard_rows(g_win)
    seg_bounds = ((0, 1664), (1664, 2176), (2176, 3712), (3712, 4224), (4232, in_cols))
    w_rw, w_zrw, w_qkv, w_zgd, w_gates = [wt_full[a:b] for a, b in seg_bounds]
    w_ba = jnp.concatenate([wt_full[4224:4232], jnp.zeros((LANES - 8, D_MODEL), BF16)], axis=0)
    w2_full, a2_full = unshard_cols(g_w2), unshard_cols(g_a2)
    zeros64 = jnp.zeros((64, RW_W), F32)
    w2p = jnp.concatenate([w2_full, zeros64], axis=0)
    a2p = jnp.concatenate([zeros64, a2_full], axis=0)
    conv_full = unshard_cols(g_conv)
    conv_rows = [conv_full[i:i + 1] for i in range(4)]
    wa_full = unshard_cols(g_wa)
    wb_full = unshard_cols(g_wb)
    wo_full = unshard_rows(g_wo)
    a_log_bc = jnp.repeat(gd_A_log, LANES, axis=1)
    dt_bias_bc = jnp.repeat(gd_dt_bias, LANES, axis=1)
    r_k_flat = rw_r_k.reshape(1, RW_W)
    now2 = norm_out_w.reshape(1, D_MODEL)

    h = _norm_in(x2, norm_in_w, tm=tm)
    p_rw = _proj("proj_rw", h, w_rw, tm=tm)
    p_zrw = _proj("proj_zrw", h, w_zrw, tm=tm)
    p_qkv = _proj("proj_qkv", h, w_qkv, tm=tm)
    p_zgd = _proj("proj_zgd", h, w_zgd, tm=tm)
    p_ba = _proj("proj_ba", h, w_ba, tm=tm)
    p_gates = _proj("proj_gates", h, w_gates, tm=tm)

    rw_params = [rw_mu, rw_w0, w2p, rw_a0, a2p, rw_k_k, rw_k_a]
    r_a, lw_a, k_a, v_a, kk_a, b_a = _pw_fwd("rwkv_prep", _rwkv_prep_f, [p_rw], 1, rw_params, [RW_W] * 6, [F32] * 6,
                                             seq=seq, tb=tb)
    y_rec, s_a = _rec_fwd("rwkv_rec", r_a, lw_a, k_a, v_a, kk_a, b_a, seq=seq, nsub=2, scalar_decay=False)
    post_params = [rw_gn_w, rw_gn_b, r_k_flat]
    (y_a,) = _pw_fwd("rwkv_post", _rwkv_post_f, [y_rec, r_a, k_a, v_a, p_zrw], 0, post_params, [RW_W], [F32], seq=seq, tb=tb)

    gd_params = conv_rows + [a_log_bc, dt_bias_bc]
    r_b, lw_b, k_b, v_b, b_b = _pw_fwd("gdn_prep", _gdn_prep_f, [p_qkv, p_ba], 3, gd_params, [GD_W] * 5, [F32] * 5,
                                       seq=seq, tb=tb)
    o_rec, s_b = _rec_fwd("gdn_rec", r_b, lw_b, k_b, v_b, k_b, b_b, seq=seq, nsub=1, scalar_decay=True)
    (y_b,) = _pw_fwd("gdn_post", _gdn_post_f, [o_rec, p_zgd], 0, [gd_o_norm_w], [GD_W], [F32], seq=seq, tb=tb)

    d_ya, d_yb, d_gates, d_xo, dwa, dwb, dwo, d_now, loss_acc = _tail(
        x2, tgt2, p_gates, y_a, y_b, wa_full, wb_full, wo_full, now2, tr=min(256, n))

    (d_o, d_zgd), (d_onw,) = _pw_bwd("gdn_post_bwd", _gdn_post_f, [o_rec, p_zgd], 0, [gd_o_norm_w], [[d_yb]],
                                     [F32, BF16], seq=seq, tb=tb)
    dr_b, dlw_b, dk_b, dv_b, dkk_b, db_b = _rec_bwd("gdn_rec_bwd", r_b, lw_b, k_b, v_b, k_b, b_b, s_b, d_o,
                                                    seq=seq, nsub=1, scalar_decay=True)
    (d_qkv, d_ba), d_gd_params = _pw_bwd("gdn_prep_bwd", _gdn_prep_f, [p_qkv, p_ba], 3, gd_params,
                                         [[dr_b], [dlw_b], [dk_b, dkk_b], [dv_b], [db_b]], [BF16, BF16], seq=seq, tb=tb)

    (d_yrec, dr_p, dk_p, dv_p, d_zrw), d_post_params = _pw_bwd(
        "rwkv_post_bwd", _rwkv_post_f, [y_rec, r_a, k_a, v_a, p_zrw], 0, post_params, [[d_ya]],
        [F32, F32, F32, F32, BF16], seq=seq, tb=tb)
    dr_a, dlw_a, dk_a, dv_a, dkk_a, db_a = _rec_bwd("rwkv_rec_bwd", r_a, lw_a, k_a, v_a, kk_a, b_a, s_a, d_yrec,
                                                    seq=seq, nsub=2, scalar_decay=False)
    (d_prw,), d_rw_params = _pw_bwd("rwkv_prep_bwd", _rwkv_prep_f, [p_rw], 1, rw_params,
                                    [[dr_a, dr_p], [dlw_a], [dk_a, dk_p], [dv_a, dv_p], [dkk_a], [db_a]], [BF16],
                                    seq=seq, tb=tb)

    dps = [d_prw, d_zrw, d_qkv, d_zgd, d_ba, d_gates]
    wsegs = [w_rw, w_zrw, w_qkv, w_zgd, w_ba, w_gates]
    dx2, d_gin = _proj_dx(x2, norm_in_w, d_xo, dps, wsegs, tm=min(256, n))
    dw_rw = _proj_dw("dw_rw", h, d_prw, tm=tm)
    dw_zrw = _proj_dw("dw_zrw", h, d_zrw, tm=tm)
    dw_qkv = _proj_dw("dw_qkv", h, d_qkv, tm=tm)
    dw_zgd = _proj_dw("dw_zgd", h, d_zgd, tm=tm)
    dw_ba = _proj_dw("dw_ba", h, d_ba, tm=tm)
    dw_gates = _proj_dw("dw_gates", h, d_gates, tm=tm)
    dwt_in_full = jnp.concatenate([dw_rw, dw_zrw, dw_qkv, dw_zgd, dw_ba[:8], dw_gates], axis=0)

    shard_cols = lambda a: jnp.transpose(a.reshape(a.shape[0], 4, 2, a.shape[1] // N_DEV), (2, 1, 0, 3))
    shard_rows = lambda a: jnp.transpose(a.reshape(4, 2, a.shape[0] // N_DEV, a.shape[1]), (1, 0, 2, 3))
    d_mu, d_w0, d_w2p, d_a0, d_a2p, d_kk_, d_ka_ = d_rw_params
    d_gnw, d_gnb, d_rk = d_post_params
    d_conv = jnp.concatenate(d_gd_params[:4], axis=0)
    d_alog = d_gd_params[4].reshape(4, LANES).sum(axis=1).reshape(1, 4)
    d_dtb = d_gd_params[5].reshape(4, LANES).sum(axis=1).reshape(1, 4)
    scat = [shard_rows(dwt_in_full), shard_cols(d_w2p[:64]), shard_cols(d_a2p[64:]), shard_cols(d_conv),
            shard_cols(dwa), shard_cols(dwb), shard_rows(dwo)]
    small_g = _pack_small([d_gin, d_mu, d_w0, d_a0, d_kk_, d_ka_, d_rk, d_gnw, d_gnb, d_alog, d_dtb, d_onw, d_now])
    pair = _exchange("reduce_cores", ("c",), scat, [small_g])
    part = [_sum_slots("pair_sum_%d" % i, a, BF16) for i, a in enumerate(pair[:7])]
    part_small = _sum_slots("pair_sum_small", pair[7][:, None], F32)[0]
    lands = _exchange("reduce_chips", ("x", "y"), part, [part_small])
    gt_w_in = _sum_slots("sum_w_in", lands[0][:, None], F32)[0]
    lands[0] = gt_w_in.T[None]

    small_w = [norm_in_w, rw_mu, rw_w0, rw_a0, rw_k_k, rw_k_a, rw_r_k, rw_gn_w, rw_gn_b, gd_A_log, gd_dt_bias, gd_o_norm_w, norm_out_w]
    small_m = [m_norm_in_w, m_rw_mu, m_rw_w0, m_rw_a0, m_rw_k_k, m_rw_k_a, m_rw_r_k, m_rw_gn_w, m_rw_gn_b, m_gd_A_log, m_gd_dt_bias, m_gd_o_norm_w, m_norm_out_w]
    small_v = [v_norm_in_w, v_rw_mu, v_rw_w0, v_rw_a0, v_rw_k_k, v_rw_k_a, v_rw_r_k, v_rw_gn_w, v_rw_gn_b, v_gd_A_log, v_gd_dt_bias, v_gd_o_norm_w, v_norm_out_w]
    flat = lambda arrs: [a.reshape(1, -1) for a in arrs]
    sm = _adam_small(lands[7], flat(small_w), flat(small_m), flat(small_v))
    sm_g, sm_d, sm_m, sm_v = [{nm: res[i].reshape(w.shape) for (nm, _), res, w in zip(_SMALL, sm, small_w)}
                              for i in range(4)]

    big = {}
    for nm, land, w, m, v in (("w_in", lands[0], w_in, m_w_in, v_w_in), ("rw_w2", lands[1], rw_w2, m_rw_w2, v_rw_w2),
                              ("rw_a2", lands[2], rw_a2, m_rw_a2, v_rw_a2),
                              ("gd_conv_w", lands[3], gd_conv_w, m_gd_conv_w, v_gd_conv_w),
                              ("w_branch_a", lands[4], w_branch_a, m_w_branch_a, v_w_branch_a),
                              ("w_branch_b", lands[5], w_branch_b, m_w_branch_b, v_w_branch_b),
                              ("w_out", lands[6], w_out, m_w_out, v_w_out)):
        big[nm] = [o.reshape(w.shape) for o in _adam("adam_" + nm, land, w[0], m[0], v[0])]

    order = ["norm_in_w", "w_in", "rw_mu", "rw_w0", "rw_w2", "rw_a0", "rw_a2", "rw_k_k", "rw_k_a", "rw_r_k", "rw_gn_w",
             "rw_gn_b", "gd_conv_w", "gd_A_log", "gd_dt_bias", "gd_o_norm_w", "w_branch_a", "w_branch_b", "w_out", "norm_out_w"]
    pick = lambda nm, i: big[nm][i] if nm in big else (sm_g, sm_d, sm_m, sm_v)[i][nm]
    loss = lax.psum(loss_acc[0, 0], ("x", "y", "c"))
    grad_x = dx2.reshape(x.shape)
    return (loss, grad_x, *[pick(nm, 0) for nm in order], *[pick(nm, 1) for nm in order],
            *[pick(nm, 2) for nm in order], *[pick(nm, 3) for nm in order])
```

```python
import functools

import jax
import jax.numpy as jnp
from jax import lax
from jax.experimental import pallas as pl
from jax.experimental.pallas import tpu as pltpu

F32 = jnp.float32
BF16 = jnp.bfloat16
HI = lax.Precision.HIGHEST

LANES = 128
SUB = 8
CHUNK = 64
N_DEV = 8
VMEM_LIMIT = 56 * 1024 * 1024

D_MODEL = 1024
RW_W = 512
GD_W = 512
RW_SHIFT = 1664
NORM_EPS = 1e-6
RW_GN_EPS = 64 * 1e-5
ADAM_LR, ADAM_B1, ADAM_B2, ADAM_EPS, ADAM_WD, ADAM_STEP = 0.001, 0.9, 0.999, 1e-8, 0.01, 10


_NN, _NT, _TN = ((1,), (0,)), ((1,), (1,)), ((0,), (0,))


def _dot(a, b, dims, passes):
    precision = lax.Precision.HIGH if passes == 3 else lax.Precision.DEFAULT
    return lax.dot_general(a, b, (dims, ((), ())), precision=precision, preferred_element_type=F32)


def _mm(a, b, passes=3):
    return _dot(a, b, _NN, passes)


def _mm_nt(a, b, passes=3):
    return _dot(a, b, _NT, passes)


def _mm_tn(a, b, passes=3):
    return _dot(a, b, _TN, passes)


P_SUM = 3
P_SCORE = 1
P_INV = 1
P_STATE = 1
P_APPLY = 1
P_UPDATE = 1


def _stack_rows(blocks):
    return jnp.concatenate(blocks, axis=0)


def _split_rows(x, n):
    r = x.shape[0] // n

    @jax.custom_vjp
    def split(x):
        return tuple(x[i * r:(i + 1) * r] for i in range(n))

    split.defvjp(lambda x: (split(x), None), lambda _, gs: (jnp.concatenate(gs, axis=0),))
    return split(x)


def _iota(shape, d):
    return lax.broadcasted_iota(jnp.int32, shape, d)


def _sigmoid(x):
    return 0.5 * (jnp.tanh(0.5 * x) + 1.0)


def _silu(x):
    return x * _sigmoid(x)


def _softplus(x):
    return jnp.maximum(x, 0.0) + jnp.log(1.0 + jnp.exp(-jnp.abs(x)))


def _seg_ones(seg):
    return ((_iota((LANES, LANES), 0) // seg) == (_iota((LANES, LANES), 1) // seg)).astype(F32)


def _sl(g):
    return slice(g * LANES, (g + 1) * LANES)


@jax.custom_vjp
def _tri_inverse(ms):
    return _tri_inverse_chain(ms)


def _tri_inverse_bwd(ts, dts):
    return ([-_mm_nt(_mm_tn(t, dt, P_INV), t, P_INV) for t, dt in zip(ts, dts)],)


def _tri_inverse_chain(ms):
    c = CHUNK
    ri, ci = _iota((c, c), 0), _iota((c, c), 1)
    eye = (ri == ci).astype(F32)
    d16 = (ri // 16) == (ci // 16)
    d32 = (ri // 32) == (ci // 32)
    ps = [jnp.where(d16, -m, 0.0) for m in ms]
    ts = [eye + p for p in ps]
    for _ in range(3):
        ps = [_mm(p, p, P_INV) for p in ps]
        ts = [_mm(t, eye + p, P_INV) for t, p in zip(ts, ps)]
    for off_diagonal in (d32 & (~d16), ~d32):
        tq = [_mm(t, jnp.where(off_diagonal, m, 0.0), P_INV) for t, m in zip(ts, ms)]
        ts = [t - _mm(a, t, P_INV) for t, a in zip(ts, tq)]
    return ts


_tri_inverse.defvjp(lambda ms: (lambda ts: (ts, ts))(_tri_inverse_chain(ms)), _tri_inverse_bwd)


@jax.custom_vjp
def _known_inverse(ms, ts):
    return ts


_known_inverse.defvjp(lambda ms, ts: (ts, ts),
                      lambda ts, dts: (_tri_inverse_bwd(ts, dts)[0], [jnp.zeros_like(t) for t in ts]))


def _chunk_fwd(prims, *, nsub, scalar_decay, inverses=None):
    c = CHUNK
    ng = len(prims)
    s0s, rs, lws, ks, vs, kks, bs = [list(t) for t in zip(*prims)]
    ri, ci = _iota((c, c), 0), _iota((c, c), 1)
    incl = ri >= ci
    strict = ri > ci
    tril = incl.astype(F32)
    hs = LANES // nsub
    lane = _iota((1, LANES), 1)
    masks = [((lane // hs) == s).astype(F32) for s in range(nsub)]
    cws = [_mm(tril, lw, P_SUM) for lw in lws]
    cwxs = [cw - lw for cw, lw in zip(cws, lws)]
    ends = [cw[c - 1:c, :] for cw in cws]
    kkds = [kk * jnp.exp(cwx) for kk, cwx in zip(kks, cwxs)]
    rds = [r * jnp.exp(cw) for r, cw in zip(rs, cws)]
    kends = [k * jnp.exp(e - cw) for k, e, cw in zip(ks, ends, cws)]
    bends = [b * jnp.exp(e - cw) for b, e, cw in zip(bs, ends, cws)]
    state_terms = [_split_rows(_mm_nt(_stack_rows([kkd, rd]), s0, P_STATE), 2) for kkd, rd, s0 in zip(kkds, rds, s0s)]
    w0s, y0s = [t[0] for t in state_terms], [t[1] for t in state_terms]
    chains = [(g, s) for g in range(ng) for s in range(nsub)]
    if scalar_decay:
        e0 = (lane == 0).astype(F32) * jnp.ones((c, 1), F32)
        rows = [_mm_nt(e0, cw, P_SUM) for cw in cws]
        dxs = [jnp.where(strict, jnp.exp(jnp.minimum(cwx[:, :c] - row, 0.0)), 0.0) for cwx, row in zip(cwxs, rows)]
        dis = [jnp.where(incl, jnp.exp(jnp.minimum(cw[:, :c] - row, 0.0)), 0.0) for cw, row in zip(cws, rows)]
        lefts = [_stack_rows([kk * m for m in masks] + [r * m for m in masks]) for kk, r in zip(kks, rs)]
        on_b = [_split_rows(_mm_nt(left, b, P_SCORE), 2 * nsub) for left, b in zip(lefts, bs)]
        on_k = [_split_rows(_mm_nt(left, k, P_SCORE), 2 * nsub) for left, k in zip(lefts, ks)]
        m_b = [on_b[g][s] * dxs[g] for g, s in chains]
        m_k = [on_k[g][s] * dxs[g] for g, s in chains]
        n_k = [on_k[g][nsub + s] * dis[g] for g, s in chains]
        n_b = [on_b[g][nsub + s] * dis[g] for g, s in chains]
    else:
        kds = [k * jnp.exp(-cw) for k, cw in zip(ks, cws)]
        bds = [b * jnp.exp(-cw) for b, cw in zip(bs, cws)]
        lefts = [_stack_rows([kkd * m for m in masks] + [rd * m for m in masks]) for kkd, rd in zip(kkds, rds)]
        on_b = [_split_rows(_mm_nt(left, bd, P_SCORE), 2 * nsub) for left, bd in zip(lefts, bds)]
        on_k = [_split_rows(_mm_nt(left, kd, P_SCORE), 2 * nsub) for left, kd in zip(lefts, kds)]
        m_b = [jnp.where(strict, on_b[g][s], 0.0) for g, s in chains]
        m_k = [jnp.where(strict, on_k[g][s], 0.0) for g, s in chains]
        n_k = [jnp.where(incl, on_k[g][nsub + s], 0.0) for g, s in chains]
        n_b = [jnp.where(incl, on_b[g][nsub + s], 0.0) for g, s in chains]
    t_inv = _tri_inverse(m_b) if inverses is None else _known_inverse(m_b, inverses)
    on_v = [_split_rows(_mm(_stack_rows([mk, nk]), vs[g], P_APPLY), 2) for (g, s), mk, nk in zip(chains, m_k, n_k)]
    sa_c = [_mm(t, w0s[g] + mv[0], P_APPLY) for (g, s), t, mv in zip(chains, t_inv, on_v)]
    y_c = [y0s[g] + mv[1] - _mm(nb, sa, P_APPLY) for (g, s), mv, nb, sa in zip(chains, on_v, n_b, sa_c)]
    sas = [sum(sa_c[g * nsub + s] * masks[s] for s in range(nsub)) for g in range(ng)]
    ys = [sum(y_c[g * nsub + s] * masks[s] for s in range(nsub)) for g in range(ng)]
    s_ends = [s0 * jnp.exp(e) + _mm_tn(_stack_rows([v, -sa]), _stack_rows([kend, bend]), P_UPDATE)
              for s0, e, v, kend, sa, bend in zip(s0s, ends, vs, kends, sas, bends)]
    if nsub > 1:
        same_head = (_iota((LANES, LANES), 0) // hs) == (_iota((LANES, LANES), 1) // hs)
        s_ends = [jnp.where(same_head, s_end, 0.0) for s_end in s_ends]
    return list(zip(ys, s_ends)), t_inv


def _rec_fwd(name, r, lw, k, v, kk, b, *, seq, nsub, scalar_decay):
    n, w = r.shape
    ng = w // LANES
    nc = seq // CHUNK
    nb = n // seq
    nt = nb * ng * nsub

    def body(r_ref, lw_ref, k_ref, v_ref, kk_ref, b_ref, y_ref, s_ref, t_ref, state):
        @pl.when(pl.program_id(0) == 0)
        def _():
            state[...] = jnp.zeros_like(state)
        prims = [(state[bi * ng + g], r_ref[bi, :, _sl(g)], lw_ref[bi, :, _sl(g)], k_ref[bi, :, _sl(g)],
                  v_ref[bi, :, _sl(g)], kk_ref[bi, :, _sl(g)], b_ref[bi, :, _sl(g)])
                 for bi in range(nb) for g in range(ng)]
        outs, t_inv = _chunk_fwd(prims, nsub=nsub, scalar_decay=scalar_decay)
        for i, (y, s_end) in enumerate(outs):
            s_ref[0, i] = prims[i][0]
            y_ref[i // ng, :, _sl(i % ng)] = y
            state[i] = s_end
        for i, t in enumerate(t_inv):
            t_ref[0, i] = t

    row = pl.BlockSpec((nb, CHUNK, w), lambda c: (0, c, 0))
    seqs = lambda a: a.reshape(nb, seq, w)
    y, s_save, t_save = pl.pallas_call(
        body, name=name, grid=(nc,),
        in_specs=[row] * 6,
        out_specs=[row, pl.BlockSpec((1, nb * ng, LANES, LANES), lambda c: (c, 0, 0, 0)),
                   pl.BlockSpec((1, nt, CHUNK, CHUNK), lambda c: (c, 0, 0, 0))],
        out_shape=[jax.ShapeDtypeStruct((nb, seq, w), F32), jax.ShapeDtypeStruct((nc, nb * ng, LANES, LANES), F32),
                   jax.ShapeDtypeStruct((nc, nt, CHUNK, CHUNK), F32)],
        scratch_shapes=[pltpu.VMEM((nb * ng, LANES, LANES), F32)],
        compiler_params=pltpu.CompilerParams(dimension_semantics=("arbitrary",), vmem_limit_bytes=VMEM_LIMIT),
    )(seqs(r), seqs(lw), seqs(k), seqs(v), seqs(kk), seqs(b))
    return y.reshape(n, w), (s_save, t_save)


def _rec_bwd(name, r, lw, k, v, kk, b, saved, dy, *, seq, nsub, scalar_decay):
    n, w = r.shape
    ng = w // LANES
    nc = seq // CHUNK
    nb = n // seq
    nt = nb * ng * nsub
    s_save, t_save = saved

    def body(r_ref, lw_ref, k_ref, v_ref, kk_ref, b_ref, s_ref, t_ref, dy_ref,
             dr_ref, dlw_ref, dk_ref, dv_ref, dkk_ref, db_ref, dstate):
        @pl.when(pl.program_id(0) == 0)
        def _():
            dstate[...] = jnp.zeros_like(dstate)
        inverses = [t_ref[0, i] for i in range(nt)]
        f = lambda p: _chunk_fwd(p, nsub=nsub, scalar_decay=scalar_decay, inverses=inverses)[0]
        chains = [(bi, g) for bi in range(nb) for g in range(ng)]
        prims = [(s_ref[0, bi * ng + g], r_ref[bi, :, _sl(g)], lw_ref[bi, :, _sl(g)], k_ref[bi, :, _sl(g)],
                  v_ref[bi, :, _sl(g)], kk_ref[bi, :, _sl(g)], b_ref[bi, :, _sl(g)]) for bi, g in chains]
        _, vjp = jax.vjp(f, prims)
        (d_prims,) = vjp([(dy_ref[bi, :, _sl(g)], dstate[bi * ng + g]) for bi, g in chains])
        for (bi, g), (ds0, dr, dlw, dk, dv, dkk, db) in zip(chains, d_prims):
            dstate[bi * ng + g] = ds0
            dr_ref[bi, :, _sl(g)] = dr
            dlw_ref[bi, :, _sl(g)] = dlw
            dk_ref[bi, :, _sl(g)] = dk
            dv_ref[bi, :, _sl(g)] = dv
            dkk_ref[bi, :, _sl(g)] = dkk
            db_ref[bi, :, _sl(g)] = db

    row = pl.BlockSpec((nb, CHUNK, w), lambda c: (0, nc - 1 - c, 0))
    seqs = lambda a: a.reshape(nb, seq, w)
    grads = pl.pallas_call(
        body, name=name, grid=(nc,),
        in_specs=[row] * 6 + [pl.BlockSpec((1, nb * ng, LANES, LANES), lambda c: (nc - 1 - c, 0, 0, 0)),
                              pl.BlockSpec((1, nt, CHUNK, CHUNK), lambda c: (nc - 1 - c, 0, 0, 0)), row],
        out_specs=[row] * 6,
        out_shape=[jax.ShapeDtypeStruct((nb, seq, w), F32)] * 6,
        scratch_shapes=[pltpu.VMEM((nb * ng, LANES, LANES), F32)],
        compiler_params=pltpu.CompilerParams(dimension_semantics=("arbitrary",), vmem_limit_bytes=VMEM_LIMIT),
    )(seqs(r), seqs(lw), seqs(k), seqs(v), seqs(kk), seqs(b), s_save, t_save, seqs(dy))
    return [g.reshape(n, w) for g in grads]


def _shift_down(a, j, halo, is_start):
    tb = a.shape[0]
    rolled = pltpu.roll(a, j, 0)
    hr = jnp.where(is_start, 0.0, pltpu.roll(halo, j, 0))
    first = jnp.where(_iota((SUB, LANES), 0) < j, hr, rolled[0:SUB])
    if tb == SUB:
        return first
    return jnp.concatenate([first, rolled[SUB:]], axis=0)


def _shift_up(d, j, carry, is_end):
    tb = d.shape[0]
    up = pltpu.roll(d, tb - j, 0)
    cr = jnp.where(is_end, 0.0, pltpu.roll(carry, SUB - j, 0))
    last = jnp.where(_iota((SUB, LANES), 0) >= SUB - j, cr, up[tb - SUB:tb])
    if tb == SUB:
        return last
    return jnp.concatenate([up[:tb - SUB], last], axis=0)


def _ngroups(a):
    return a.shape[1] // LANES


def _pw_fwd(name, f, ins, shift, params, out_widths, out_dtypes, *, seq, tb):
    n = ins[0].shape[0]
    nt, tps = n // tb, seq // tb
    ni, npar = len(ins), len(params)

    def body(*refs):
        in_refs = refs[:ni]
        pos = ni
        halo_ref = None
        if shift:
            halo_ref = refs[pos]
            pos += 1
        p_refs = refs[pos:pos + npar]
        out_refs = refs[pos + npar:]
        is_start = (pl.program_id(0) % tps) == 0
        tiles = [[ref[:, _sl(g)] for g in range(_ngroups(ref))] for ref in in_refs]
        prevs = [[_shift_down(tiles[0][g], j, halo_ref[:, _sl(g)], is_start) for g in range(len(tiles[0]))]
                 for j in range(1, shift + 1)]
        pv = [[ref[:, _sl(g)] for g in range(_ngroups(ref))] for ref in p_refs]
        outs = f(tiles, prevs, pv)
        for o_ref, og in zip(out_refs, outs, strict=True):
            for g, t in enumerate(og):
                o_ref[:, _sl(g)] = t.astype(o_ref.dtype)

    in_specs = [pl.BlockSpec((tb, a.shape[1]), lambda i: (i, 0)) for a in ins]
    args = list(ins)
    if shift:
        in_specs.append(pl.BlockSpec((SUB, ins[0].shape[1]), lambda i: (jnp.maximum(i * (tb // SUB) - 1, 0), 0)))
        args.append(ins[0])
    in_specs += [pl.BlockSpec(p.shape, lambda i: (0, 0)) for p in params]
    args += list(params)
    return pl.pallas_call(
        body, name=name, grid=(nt,),
        in_specs=in_specs,
        out_specs=[pl.BlockSpec((tb, w), lambda i: (i, 0)) for w in out_widths],
        out_shape=[jax.ShapeDtypeStruct((n, w), dt) for w, dt in zip(out_widths, out_dtypes, strict=True)],
        compiler_params=pltpu.CompilerParams(dimension_semantics=("parallel",), vmem_limit_bytes=VMEM_LIMIT),
    )(*args)


def _pw_bwd(name, f, ins, shift, params, douts, din_dtypes, *, seq, tb):
    n = ins[0].shape[0]
    nt, tps = n // tb, seq // tb
    ni, npar = len(ins), len(params)
    flat_douts = [d for ds in douts for d in ds]
    nd = len(flat_douts)
    w0 = ins[0].shape[1]

    def body(*refs):
        in_refs = refs[:ni]
        pos = ni
        halo_ref = None
        if shift:
            halo_ref = refs[pos]
            pos += 1
        p_refs = refs[pos:pos + npar]
        pos += npar
        d_refs = refs[pos:pos + nd]
        pos += nd
        din_refs = refs[pos:pos + ni]
        pos += ni
        dp_refs = refs[pos:pos + npar]
        pos += npar
        carry = refs[pos] if shift else None
        step = pl.program_id(0)
        tile = nt - 1 - step
        is_start = (tile % tps) == 0
        is_end = (tile % tps) == tps - 1
        tiles = [[ref[:, _sl(g)] for g in range(_ngroups(ref))] for ref in in_refs]
        prevs = [[_shift_down(tiles[0][g], j, halo_ref[:, _sl(g)], is_start) for g in range(len(tiles[0]))]
                 for j in range(1, shift + 1)]
        pv = [[ref[:, _sl(g)] for g in range(_ngroups(ref))] for ref in p_refs]
        cot, pos_d = [], 0
        for ds in douts:
            grp = d_refs[pos_d:pos_d + len(ds)]
            pos_d += len(ds)
            cot.append([sum(ref[:, _sl(g)].astype(F32) for ref in grp) for g in range(_ngroups(grp[0]))])
        _, vjp = jax.vjp(f, tiles, prevs, pv)
        d_tiles, d_prevs, d_pv = vjp(cot)
        for g in range(len(tiles[0])):
            for j in range(1, shift + 1):
                d_tiles[0][g] = d_tiles[0][g] + _shift_up(d_prevs[j - 1][g], j, carry[j - 1, :, _sl(g)], is_end)
            for j in range(1, shift + 1):
                carry[j - 1, :, _sl(g)] = d_prevs[j - 1][g][0:SUB]
        for ref, dg in zip(din_refs, d_tiles, strict=True):
            for g, t in enumerate(dg):
                ref[:, _sl(g)] = t.astype(ref.dtype)

        @pl.when(step == 0)
        def _():
            for ref in dp_refs:
                ref[...] = jnp.zeros_like(ref)
        for ref, dg in zip(dp_refs, d_pv, strict=True):
            for g, t in enumerate(dg):
                ref[:, _sl(g)] += t

    rev = lambda i: (nt - 1 - i, 0)
    in_specs = [pl.BlockSpec((tb, a.shape[1]), rev) for a in ins]
    args = list(ins)
    if shift:
        in_specs.append(pl.BlockSpec((SUB, w0), lambda i: (jnp.maximum((nt - 1 - i) * (tb // SUB) - 1, 0), 0)))
        args.append(ins[0])
    in_specs += [pl.BlockSpec(p.shape, lambda i: (0, 0)) for p in params]
    args += list(params)
    in_specs += [pl.BlockSpec((tb, d.shape[1]), rev) for d in flat_douts]
    args += flat_douts
    out_specs = [pl.BlockSpec((tb, a.shape[1]), rev) for a in ins] + [pl.BlockSpec(p.shape, lambda i: (0, 0)) for p in params]
    out_shape = ([jax.ShapeDtypeStruct(a.shape, dt) for a, dt in zip(ins, din_dtypes, strict=True)]
                 + [jax.ShapeDtypeStruct(p.shape, F32) for p in params])
    res = pl.pallas_call(
        body, name=name, grid=(nt,),
        in_specs=in_specs, out_specs=out_specs, out_shape=out_shape,
        scratch_shapes=[pltpu.VMEM((shift, SUB, w0), F32)] if shift else [],
        compiler_params=pltpu.CompilerParams(dimension_semantics=("arbitrary",), vmem_limit_bytes=VMEM_LIMIT),
    )(*args)
    return res[:ni], res[ni:]


def _rwkv_prep_f(tiles, prevs, params):
    (p,), (prev,) = tiles, prevs
    mu, w0, w2p, a0, a2p, k_k, k_a = params
    xs = [p[g] + (prev[g] - p[g]) * mu[g] for g in range(13)]
    wdad = xs[12]
    tw = jnp.tanh(wdad)
    e64 = _seg_ones(64)
    r, lw, k2, v, kk, b = [], [], [], [], [], []
    for g in range(4):
        k_g = xs[4 + g]
        lo = w0[g] + _mm(tw, w2p[g])
        lw_g = -jnp.exp(-_softplus(-lo) - 0.5)
        a_g = _sigmoid(a0[g] + _mm(wdad, a2p[g]))
        kkp = k_g * k_k[g]
        kk_g = kkp * lax.rsqrt(_mm(kkp * kkp, e64) + 1e-12)
        r.append(xs[g])
        lw.append(lw_g)
        k2.append(k_g * (1.0 + (a_g - 1.0) * k_a[g]))
        v.append(xs[8 + g])
        kk.append(kk_g)
        b.append(kk_g * a_g)
    return [r, lw, k2, v, kk, b]


def _rwkv_post_f(tiles, prevs, params):
    yrec, r, k2, v, z = tiles
    gn_w, gn_b, r_k = params
    e64 = _seg_ones(64)
    out = []
    for g in range(4):
        mean = _mm(yrec[g], e64) * (1.0 / 64)
        d = yrec[g] - mean
        var = _mm(d * d, e64) * (1.0 / 64)
        yn = d * lax.rsqrt(var + RW_GN_EPS) * gn_w[g] + gn_b[g]
        bonus = _mm(r[g] * k2[g] * r_k[g], e64) * v[g]
        out.append((yn + bonus) * _silu(z[g]))
    return [out]


def _gdn_prep_f(tiles, prevs, params):
    x, (ba,) = tiles
    p1, p2, p3 = prevs
    cw0, cw1, cw2, cw3, a_log, dt_bias = params
    s = [_silu(cw3[g] * x[g] + cw2[g] * p1[g] + cw1[g] * p2[g] + cw0[g] * p3[g]) for g in range(12)]
    row = _iota((LANES, LANES), 0)
    r, lw, k, vv, b = [], [], [], [], []
    for h in range(4):
        q_h, k_h, v_h = s[h], s[4 + h], s[8 + h]
        qn = q_h * lax.rsqrt(jnp.sum(q_h * q_h, axis=-1, keepdims=True) + 1e-12)
        kn = k_h * lax.rsqrt(jnp.sum(k_h * k_h, axis=-1, keepdims=True) + 1e-12)
        beta = _sigmoid(_mm(ba, (row == h).astype(F32)))
        alpha = _mm(ba, (row == 4 + h).astype(F32))
        g_h = -jnp.exp(a_log[h]) * _softplus(alpha + dt_bias[h])
        r.append(qn * (LANES ** -0.5))
        lw.append(g_h)
        k.append(kn)
        vv.append(beta * v_h)
        b.append(jnp.exp(g_h) * beta * kn)
    return [r, lw, k, vv, b]


def _gdn_post_f(tiles, prevs, params):
    o, z = tiles
    ((onw,),) = params
    out = []
    for h in range(4):
        ms = jnp.mean(o[h] * o[h], axis=-1, keepdims=True)
        out.append(o[h] * lax.rsqrt(ms + NORM_EPS) * onw * _silu(z[h]))
    return [out]


def _norm_in(x2, g_in, *, tm):
    n = x2.shape[0]

    def body(x_ref, g_ref, h_ref):
        x = x_ref[...]
        rs = lax.rsqrt(jnp.mean(x * x, axis=-1, keepdims=True) + NORM_EPS)
        h_ref[...] = (x * rs * g_ref[...]).astype(BF16)

    return pl.pallas_call(
        body, name="norm_in", grid=(n // tm,),
        in_specs=[pl.BlockSpec((tm, D_MODEL), lambda i: (i, 0)), pl.BlockSpec((1, D_MODEL), lambda i: (0, 0))],
        out_specs=pl.BlockSpec((tm, D_MODEL), lambda i: (i, 0)),
        out_shape=jax.ShapeDtypeStruct((n, D_MODEL), BF16),
        compiler_params=pltpu.CompilerParams(dimension_semantics=("parallel",), vmem_limit_bytes=VMEM_LIMIT),
    )(x2, g_in)


def _proj(name, h, wt, *, tm):
    n, ws = h.shape[0], wt.shape[0]

    def body(h_ref, w_ref, o_ref):
        o_ref[...] = lax.dot_general(h_ref[...], w_ref[...], (_NT, ((), ())), preferred_element_type=F32)

    return pl.pallas_call(
        body, name=name, grid=(n // tm,),
        in_specs=[pl.BlockSpec((tm, D_MODEL), lambda i: (i, 0)), pl.BlockSpec((ws, D_MODEL), lambda i: (0, 0))],
        out_specs=pl.BlockSpec((tm, ws), lambda i: (i, 0)),
        out_shape=jax.ShapeDtypeStruct((n, ws), F32),
        compiler_params=pltpu.CompilerParams(dimension_semantics=("parallel",), vmem_limit_bytes=VMEM_LIMIT),
    )(h, wt)


def _proj_dw(name, h, dp, *, tm):
    n, ws = dp.shape

    def body(h_ref, d_ref, o_ref):
        @pl.when(pl.program_id(0) == 0)
        def _():
            o_ref[...] = jnp.zeros_like(o_ref)
        o_ref[...] += lax.dot_general(d_ref[...], h_ref[...], (_TN, ((), ())), preferred_element_type=F32)

    return pl.pallas_call(
        body, name=name, grid=(n // tm,),
        in_specs=[pl.BlockSpec((tm, D_MODEL), lambda i: (i, 0)), pl.BlockSpec((tm, ws), lambda i: (i, 0))],
        out_specs=pl.BlockSpec((ws, D_MODEL), lambda i: (0, 0)),
        out_shape=jax.ShapeDtypeStruct((ws, D_MODEL), F32),
        compiler_params=pltpu.CompilerParams(dimension_semantics=("arbitrary",), vmem_limit_bytes=VMEM_LIMIT),
    )(h, dp)


def _proj_dx(x2, g_in, d_xo, dps, ws, *, tm):
    n = x2.shape[0]
    ns = len(dps)

    def body(*refs):
        x_ref, g_ref, dxo_ref = refs[:3]
        dp_refs = refs[3:3 + ns]
        w_refs = refs[3 + ns:3 + 2 * ns]
        dx_ref, dg_ref = refs[3 + 2 * ns:]
        dh = jnp.zeros((tm, D_MODEL), F32)
        for d_ref, w_ref in zip(dp_refs, w_refs, strict=True):
            dh = dh + jnp.dot(d_ref[...], w_ref[...], preferred_element_type=F32)
        x = x_ref[...]
        rs = lax.rsqrt(jnp.mean(x * x, axis=-1, keepdims=True) + NORM_EPS)
        xn = x * rs
        dxn = dh * g_ref[...]
        dx_ref[...] = dxo_ref[...] + rs * (dxn - xn * jnp.mean(dxn * xn, axis=-1, keepdims=True))

        @pl.when(pl.program_id(0) == 0)
        def _():
            dg_ref[...] = jnp.zeros_like(dg_ref)
        dg_ref[...] += jnp.sum(dh * xn, axis=0, keepdims=True)

    row = pl.BlockSpec((tm, D_MODEL), lambda i: (i, 0))
    return pl.pallas_call(
        body, name="proj_dx", grid=(n // tm,),
        in_specs=([row, pl.BlockSpec((1, D_MODEL), lambda i: (0, 0)), row]
                  + [pl.BlockSpec((tm, d.shape[1]), lambda i: (i, 0)) for d in dps]
                  + [pl.BlockSpec(w.shape, lambda i: (0, 0)) for w in ws]),
        out_specs=[row, pl.BlockSpec((1, D_MODEL), lambda i: (0, 0))],
        out_shape=[jax.ShapeDtypeStruct((n, D_MODEL), F32), jax.ShapeDtypeStruct((1, D_MODEL), F32)],
        compiler_params=pltpu.CompilerParams(dimension_semantics=("arbitrary",), vmem_limit_bytes=VMEM_LIMIT),
    )(x2, g_in, d_xo, *dps, *ws)


def _tail(x2, tgt2, gates, ya, yb, w_a, w_b, w_o, now, *, tr):
    n = x2.shape[0]

    def body(x_ref, t_ref, g_ref, ya_ref, yb_ref, wa_ref, wb_ref, wo_ref, now_ref,
             dya_ref, dyb_ref, dg_ref, dxo_ref, dwa_ref, dwb_ref, dwo_ref, dnow_ref, loss_ref):
        ya16, yb16 = ya_ref[...].astype(BF16), yb_ref[...].astype(BF16)
        ua = jnp.dot(ya16, wa_ref[...], preferred_element_type=F32)
        ub = jnp.dot(yb16, wb_ref[...], preferred_element_type=F32)
        ga = _sigmoid(g_ref[:, :D_MODEL])
        gb = _sigmoid(g_ref[:, D_MODEL:])
        m16 = (ga * ua + gb * ub).astype(BF16)
        xo = x_ref[...] + jnp.dot(m16, wo_ref[...], preferred_element_type=F32)
        rs = lax.rsqrt(jnp.mean(xo * xo, axis=-1, keepdims=True) + NORM_EPS)
        yn = xo * rs
        now_v = now_ref[...]
        err = yn * now_v - t_ref[...]
        dy = err * (1.0 / D_MODEL)
        dyn = dy * now_v
        dxo = rs * (dyn - yn * jnp.mean(dyn * yn, axis=-1, keepdims=True))
        dxo_ref[...] = dxo
        dxo16 = dxo.astype(BF16)
        dm = lax.dot_general(dxo16, wo_ref[...], (((1,), (1,)), ((), ())), preferred_element_type=F32)
        dua16 = (dm * ga).astype(BF16)
        dub16 = (dm * gb).astype(BF16)
        dg_ref[:, :D_MODEL] = (dm * ua * ga * (1.0 - ga)).astype(dg_ref.dtype)
        dg_ref[:, D_MODEL:] = (dm * ub * gb * (1.0 - gb)).astype(dg_ref.dtype)
        dya_ref[...] = lax.dot_general(dua16, wa_ref[...], (((1,), (1,)), ((), ())), preferred_element_type=F32)
        dyb_ref[...] = lax.dot_general(dub16, wb_ref[...], (((1,), (1,)), ((), ())), preferred_element_type=F32)

        @pl.when(pl.program_id(0) == 0)
        def _():
            for ref in (dwa_ref, dwb_ref, dwo_ref, dnow_ref, loss_ref):
                ref[...] = jnp.zeros_like(ref)
        tn = (((0,), (0,)), ((), ()))
        dwo_ref[...] += lax.dot_general(m16, dxo16, tn, preferred_element_type=F32)
        dwa_ref[...] += lax.dot_general(ya16, dua16, tn, preferred_element_type=F32)
        dwb_ref[...] += lax.dot_general(yb16, dub16, tn, preferred_element_type=F32)
        dnow_ref[...] += jnp.sum(dy * yn, axis=0, keepdims=True)
        loss_ref[...] += (0.5 / D_MODEL) * jnp.sum(err * err)

    row = lambda w: pl.BlockSpec((tr, w), lambda i: (i, 0))
    full = lambda a: pl.BlockSpec(a.shape, lambda i: (0, 0))
    return pl.pallas_call(
        body, name="tail", grid=(n // tr,),
        in_specs=[row(D_MODEL), row(D_MODEL), row(2 * D_MODEL), row(RW_W), row(GD_W), full(w_a), full(w_b), full(w_o), full(now)],
        out_specs=[row(RW_W), row(GD_W), row(2 * D_MODEL), row(D_MODEL),
                   pl.BlockSpec((RW_W, D_MODEL), lambda i: (0, 0)), pl.BlockSpec((GD_W, D_MODEL), lambda i: (0, 0)),
                   pl.BlockSpec((D_MODEL, D_MODEL), lambda i: (0, 0)), pl.BlockSpec((1, D_MODEL), lambda i: (0, 0)),
                   pl.BlockSpec((SUB, LANES), lambda i: (0, 0))],
        out_shape=[jax.ShapeDtypeStruct((n, RW_W), F32), jax.ShapeDtypeStruct((n, GD_W), F32),
                   jax.ShapeDtypeStruct((n, 2 * D_MODEL), BF16), jax.ShapeDtypeStruct((n, D_MODEL), F32),
                   jax.ShapeDtypeStruct((RW_W, D_MODEL), F32), jax.ShapeDtypeStruct((GD_W, D_MODEL), F32),
                   jax.ShapeDtypeStruct((D_MODEL, D_MODEL), F32), jax.ShapeDtypeStruct((1, D_MODEL), F32),
                   jax.ShapeDtypeStruct((SUB, LANES), F32)],
        compiler_params=pltpu.CompilerParams(dimension_semantics=("arbitrary",), vmem_limit_bytes=VMEM_LIMIT),
    )(x2, tgt2, gates, ya, yb, w_a, w_b, w_o, now)


def _exchange(name, axes, scatter, gather):
    ns, ng = len(scatter), len(gather)
    na = ns + ng
    gs = 2 ** len(axes)
    arrs = list(scatter) + list(gather)

    def body(*refs):
        src = refs[:na]
        dst = refs[na:2 * na]
        send_sems, recv_sems = refs[2 * na:]
        mine = {ax: lax.axis_index(ax) for ax in ("x", "y", "c")}

        def peer(k):
            co = dict(mine)
            for i, ax in enumerate(axes):
                if (k >> (len(axes) - 1 - i)) & 1:
                    co[ax] = 1 - co[ax]
            idx = 0
            for ax in axes:
                idx = 2 * idx + co[ax]
            return (co["x"], co["y"], co["c"]), idx

        _, me = peer(0)

        def copy(a, k, landing):
            dev, idx = peer(k)
            s = src[a].at[idx] if a < ns else src[a]
            return pltpu.make_async_remote_copy(src_ref=s, dst_ref=dst[a].at[idx if landing else me],
                                                send_sem=send_sems.at[a, k - 1], recv_sem=recv_sems.at[a, k - 1],
                                                device_id=dev, device_id_type=pl.DeviceIdType.MESH)

        sends = [copy(a, k, False) for a in range(na) for k in range(1, gs)]
        for cp in sends:
            cp.start()
        for a in range(na):
            for k in range(1, gs):
                copy(a, k, True).wait_recv()
        for cp in sends:
            cp.wait_send()

    out_shape = [jax.ShapeDtypeStruct(a.shape, a.dtype) for a in scatter] + \
                [jax.ShapeDtypeStruct((gs,) + a.shape, a.dtype) for a in gather]
    anyspec = pl.BlockSpec(memory_space=pl.ANY)
    lands = pl.pallas_call(
        body, name=name,
        in_specs=[anyspec] * na, out_specs=[anyspec] * na, out_shape=out_shape,
        scratch_shapes=[pltpu.SemaphoreType.DMA((na, gs - 1)), pltpu.SemaphoreType.DMA((na, gs - 1))],
    )(*arrs)
    me = 0
    for ax in axes:
        me = 2 * me + lax.axis_index(ax)
    kept = [lax.dynamic_index_in_dim(a, me, 0, keepdims=False) for a in scatter] + list(gather)
    return [lax.dynamic_update_index_in_dim(land, mine, me, 0) for land, mine in zip(lands, kept)]


def _sum_slots(name, land, out_dtype):
    ns, nq, r, c = land.shape

    def body(l_ref, o_ref):
        acc = l_ref[0, 0].astype(F32)
        for s in range(1, ns):
            acc = acc + l_ref[s, 0].astype(F32)
        o_ref[0] = acc.astype(o_ref.dtype)

    return pl.pallas_call(
        body, name=name, grid=(nq,),
        in_specs=[pl.BlockSpec((ns, 1, r, c), lambda i: (0, i, 0, 0))],
        out_specs=pl.BlockSpec((1, r, c), lambda i: (i, 0, 0)),
        out_shape=jax.ShapeDtypeStruct((nq, r, c), out_dtype),
        compiler_params=pltpu.CompilerParams(dimension_semantics=("parallel",), vmem_limit_bytes=VMEM_LIMIT),
    )(land)


def _adam(name, land, w, m, v):
    r, c = w.shape
    nslot = land.shape[0]
    tr = 256 if (r % 256 == 0 and r > 256) else r

    def body(l_ref, w_ref, m_ref, v_ref, g_out, d_out, m_out, v_out):
        g = l_ref[0].astype(F32)
        for s in range(1, nslot):
            g = g + l_ref[s].astype(F32)
        g_out[...] = g
        d_out[...], m_out[...], v_out[...] = _adam_math(g, w_ref[...], m_ref[...], v_ref[...])

    blk = pl.BlockSpec((tr, c), lambda i: (i, 0))
    return pl.pallas_call(
        body, name=name, grid=(r // tr,),
        in_specs=[pl.BlockSpec((nslot, tr, c), lambda i: (0, i, 0)), blk, blk, blk],
        out_specs=[blk] * 4,
        out_shape=[jax.ShapeDtypeStruct((r, c), F32)] * 4,
        compiler_params=pltpu.CompilerParams(dimension_semantics=("parallel",), vmem_limit_bytes=VMEM_LIMIT),
    )(land, w, m, v)


def _adam_math(g, w, m, v):
    c1 = 1.0 / (1.0 - ADAM_B1 ** ADAM_STEP)
    c2 = 1.0 / (1.0 - ADAM_B2 ** ADAM_STEP)
    m_new = ADAM_B1 * m + (1.0 - ADAM_B1) * g
    v_new = ADAM_B2 * v + (1.0 - ADAM_B2) * (g * g)
    return -ADAM_LR * ((m_new * c1) / (jnp.sqrt(v_new * c2) + ADAM_EPS) + ADAM_WD * w), m_new, v_new


def _adam_small(land, ws, ms, vs):
    npar = len(ws)
    nslot = land.shape[0]

    def body(*refs):
        l_ref = refs[0]
        w_refs, m_refs, v_refs = refs[1:1 + npar], refs[1 + npar:1 + 2 * npar], refs[1 + 2 * npar:1 + 3 * npar]
        outs = refs[1 + 3 * npar:1 + 7 * npar]
        g_rows = refs[1 + 7 * npar]
        g = l_ref[0]
        for s in range(1, nslot):
            g = g + l_ref[s]
        g_rows[...] = g
        row = 0
        for i, (_, size) in enumerate(_SMALL):
            for j in range(-(-size // LANES)):
                width = min(LANES, size - j * LANES)
                cols = slice(j * LANES, j * LANES + width)
                g_ij = g_rows[row:row + 1, 0:width]
                delta, m_new, v_new = _adam_math(g_ij, w_refs[i][:, cols], m_refs[i][:, cols], v_refs[i][:, cols])
                for ref, val in zip(outs[4 * i:4 * i + 4], (g_ij, delta, m_new, v_new)):
                    ref[:, cols] = val
                row += 1

    full = lambda a: pl.BlockSpec(a.shape, lambda: (0,) * a.ndim)
    res = pl.pallas_call(
        body, name="adam_small",
        in_specs=[full(land)] + [full(a) for a in list(ws) + list(ms) + list(vs)],
        out_specs=[full(w) for w in ws for _ in range(4)],
        out_shape=[jax.ShapeDtypeStruct(w.shape, F32) for w in ws for _ in range(4)],
        scratch_shapes=[pltpu.VMEM(land.shape[1:], F32)],
    )(land, *ws, *ms, *vs)
    return [res[4 * i:4 * i + 4] for i in range(npar)]


_SMALL = (("norm_in_w", 1024), ("rw_mu", 1664), ("rw_w0", 512), ("rw_a0", 512), ("rw_k_k", 512), ("rw_k_a", 512),
          ("rw_r_k", 512), ("rw_gn_w", 512), ("rw_gn_b", 512), ("gd_A_log", 4), ("gd_dt_bias", 4), ("gd_o_norm_w", 128),
          ("norm_out_w", 1024))
_SMALL_ROWS = 64


def _pack_small(vals):
    rows = []
    for (_, size), a in zip(_SMALL, vals, strict=True):
        flat = a.reshape(-1).astype(F32)
        pad = (-size) % LANES
        if pad:
            flat = jnp.concatenate([flat, jnp.zeros((pad,), F32)])
        rows.append(flat.reshape(-1, LANES))
    used = sum(r.shape[0] for r in rows)
    rows.append(jnp.zeros((_SMALL_ROWS - used, LANES), F32))
    return jnp.concatenate(rows, axis=0)


def kernel(x, norm_in_w, w_in, rw_mu, rw_w0, rw_w2, rw_a0, rw_a2, rw_k_k, rw_k_a, rw_r_k, rw_gn_w, rw_gn_b, gd_conv_w, gd_A_log, gd_dt_bias, gd_o_norm_w, w_branch_a, w_branch_b, w_out, norm_out_w, loss_target, m_norm_in_w, m_w_in, m_rw_mu, m_rw_w0, m_rw_w2, m_rw_a0, m_rw_a2, m_rw_k_k, m_rw_k_a, m_rw_r_k, m_rw_gn_w, m_rw_gn_b, m_gd_conv_w, m_gd_A_log, m_gd_dt_bias, m_gd_o_norm_w, m_w_branch_a, m_w_branch_b, m_w_out, m_norm_out_w, v_norm_in_w, v_w_in, v_rw_mu, v_rw_w0, v_rw_w2, v_rw_a0, v_rw_a2, v_rw_k_k, v_rw_k_a, v_rw_r_k, v_rw_gn_w, v_rw_gn_b, v_gd_conv_w, v_gd_A_log, v_gd_dt_bias, v_gd_o_norm_w, v_w_branch_a, v_w_branch_b, v_w_out, v_norm_out_w):
    nb, seq, _ = x.shape
    n = nb * seq
    tm = min(512, n)
    tb = min(256, seq)
    x2 = x.reshape(n, D_MODEL)
    tgt2 = loss_target.reshape(n, D_MODEL)
    cols = w_in.shape[2]
    in_cols = cols * N_DEV

    sharded = [w_in[0].T.astype(BF16), rw_w2[0], rw_a2[0], gd_conv_w[0], w_branch_a[0].astype(BF16),
               w_branch_b[0].astype(BF16), w_out[0].astype(BF16)]
    by_chip = _exchange("gather_chips", ("x", "y"), [], sharded)
    g_win, g_w2, g_a2, g_conv, g_wa, g_wb, g_wo = _exchange("gather_cores", ("c",), [], by_chip)
    unshard_rows = lambda a: jnp.transpose(a, (1, 0, 2, 3)).reshape(N_DEV * a.shape[2], a.shape[3])
    unshard_cols = lambda a: jnp.transpose(a, (2, 1, 0, 3)).reshape(a.shape[2], N_DEV * a.shape[3])
    wt_full = unshard_rows(g_win)
    seg_bounds = ((0, 1664), (1664, 2176), (2176, 3712), (3712, 4224), (4232, in_cols))
    w_rw, w_zrw, w_qkv, w_zgd, w_gates = [wt_full[a:b] for a, b in seg_bounds]
    w_ba = jnp.concatenate([wt_full[4224:4232], jnp.zeros((LANES - 8, D_MODEL), BF16)], axis=0)
    w2_full, a2_full = unshard_cols(g_w2), unshard_cols(g_a2)
    zeros64 = jnp.zeros((64, RW_W), F32)
    w2p = jnp.concatenate([w2_full, zeros64], axis=0)
    a2p = jnp.concatenate([zeros64, a2_full], axis=0)
    conv_full = unshard_cols(g_conv)
    conv_rows = [conv_full[i:i + 1] for i in range(4)]
    wa_full = unshard_cols(g_wa)
    wb_full = unshard_cols(g_wb)
    wo_full = unshard_rows(g_wo)
    a_log_bc = jnp.repeat(gd_A_log, LANES, axis=1)
    dt_bias_bc = jnp.repeat(gd_dt_bias, LANES, axis=1)
    r_k_flat = rw_r_k.reshape(1, RW_W)
    now2 = norm_out_w.reshape(1, D_MODEL)

    h = _norm_in(x2, norm_in_w, tm=tm)
    p_rw = _proj("proj_rw", h, w_rw, tm=tm)
    p_zrw = _proj("proj_zrw", h, w_zrw, tm=tm)
    p_qkv = _proj("proj_qkv", h, w_qkv, tm=tm)
    p_zgd = _proj("proj_zgd", h, w_zgd, tm=tm)
    p_ba = _proj("proj_ba", h, w_ba, tm=tm)
    p_gates = _proj("proj_gates", h, w_gates, tm=tm)

    rw_params = [rw_mu, rw_w0, w2p, rw_a0, a2p, rw_k_k, rw_k_a]
    r_a, lw_a, k_a, v_a, kk_a, b_a = _pw_fwd("rwkv_prep", _rwkv_prep_f, [p_rw], 1, rw_params, [RW_W] * 6, [F32] * 6,
                                             seq=seq, tb=tb)
    y_rec, s_a = _rec_fwd("rwkv_rec", r_a, lw_a, k_a, v_a, kk_a, b_a, seq=seq, nsub=2, scalar_decay=False)
    post_params = [rw_gn_w, rw_gn_b, r_k_flat]
    (y_a,) = _pw_fwd("rwkv_post", _rwkv_post_f, [y_rec, r_a, k_a, v_a, p_zrw], 0, post_params, [RW_W], [F32], seq=seq, tb=tb)

    gd_params = conv_rows + [a_log_bc, dt_bias_bc]
    r_b, lw_b, k_b, v_b, b_b = _pw_fwd("gdn_prep", _gdn_prep_f, [p_qkv, p_ba], 3, gd_params, [GD_W] * 5, [F32] * 5,
                                       seq=seq, tb=tb)
    o_rec, s_b = _rec_fwd("gdn_rec", r_b, lw_b, k_b, v_b, k_b, b_b, seq=seq, nsub=1, scalar_decay=True)
    (y_b,) = _pw_fwd("gdn_post", _gdn_post_f, [o_rec, p_zgd], 0, [gd_o_norm_w], [GD_W], [F32], seq=seq, tb=tb)

    d_ya, d_yb, d_gates, d_xo, dwa, dwb, dwo, d_now, loss_acc = _tail(
        x2, tgt2, p_gates, y_a, y_b, wa_full, wb_full, wo_full, now2, tr=min(256, n))

    (d_o, d_zgd), (d_onw,) = _pw_bwd("gdn_post_bwd", _gdn_post_f, [o_rec, p_zgd], 0, [gd_o_norm_w], [[d_yb]],
                                     [F32, BF16], seq=seq, tb=tb)
    dr_b, dlw_b, dk_b, dv_b, dkk_b, db_b = _rec_bwd("gdn_rec_bwd", r_b, lw_b, k_b, v_b, k_b, b_b, s_b, d_o,
                                                    seq=seq, nsub=1, scalar_decay=True)
    (d_qkv, d_ba), d_gd_params = _pw_bwd("gdn_prep_bwd", _gdn_prep_f, [p_qkv, p_ba], 3, gd_params,
                                         [[dr_b], [dlw_b], [dk_b, dkk_b], [dv_b], [db_b]], [BF16, BF16], seq=seq, tb=tb)

    (d_yrec, dr_p, dk_p, dv_p, d_zrw), d_post_params = _pw_bwd(
        "rwkv_post_bwd", _rwkv_post_f, [y_rec, r_a, k_a, v_a, p_zrw], 0, post_params, [[d_ya]],
        [F32, F32, F32, F32, BF16], seq=seq, tb=tb)
    dr_a, dlw_a, dk_a, dv_a, dkk_a, db_a = _rec_bwd("rwkv_rec_bwd", r_a, lw_a, k_a, v_a, kk_a, b_a, s_a, d_yrec,
                                                    seq=seq, nsub=2, scalar_decay=False)
    (d_prw,), d_rw_params = _pw_bwd("rwkv_prep_bwd", _rwkv_prep_f, [p_rw], 1, rw_params,
                                    [[dr_a, dr_p], [dlw_a], [dk_a, dk_p], [dv_a, dv_p], [dkk_a], [db_a]], [BF16],
                                    seq=seq, tb=tb)

    dps = [d_prw, d_zrw, d_qkv, d_zgd, d_ba, d_gates]
    wsegs = [w_rw, w_zrw, w_qkv, w_zgd, w_ba, w_gates]
    dx2, d_gin = _proj_dx(x2, norm_in_w, d_xo, dps, wsegs, tm=min(256, n))
    dw_rw = _proj_dw("dw_rw", h, d_prw, tm=tm)
    dw_zrw = _proj_dw("dw_zrw", h, d_zrw, tm=tm)
    dw_qkv = _proj_dw("dw_qkv", h, d_qkv, tm=tm)
    dw_zgd = _proj_dw("dw_zgd", h, d_zgd, tm=tm)
    dw_ba = _proj_dw("dw_ba", h, d_ba, tm=tm)
    dw_gates = _proj_dw("dw_gates", h, d_gates, tm=tm)
    dwt_in_full = jnp.concatenate([dw_rw, dw_zrw, dw_qkv, dw_zgd, dw_ba[:8], dw_gates], axis=0)

    shard_cols = lambda a: jnp.transpose(a.reshape(a.shape[0], 4, 2, a.shape[1] // N_DEV), (2, 1, 0, 3))
    shard_rows = lambda a: jnp.transpose(a.reshape(4, 2, a.shape[0] // N_DEV, a.shape[1]), (1, 0, 2, 3))
    d_mu, d_w0, d_w2p, d_a0, d_a2p, d_kk_, d_ka_ = d_rw_params
    d_gnw, d_gnb, d_rk = d_post_params
    d_conv = jnp.concatenate(d_gd_params[:4], axis=0)
    d_alog = d_gd_params[4].reshape(4, LANES).sum(axis=1).reshape(1, 4)
    d_dtb = d_gd_params[5].reshape(4, LANES).sum(axis=1).reshape(1, 4)
    scat = [shard_rows(dwt_in_full), shard_cols(d_w2p[:64]), shard_cols(d_a2p[64:]), shard_cols(d_conv),
            shard_cols(dwa), shard_cols(dwb), shard_rows(dwo)]
    small_g = _pack_small([d_gin, d_mu, d_w0, d_a0, d_kk_, d_ka_, d_rk, d_gnw, d_gnb, d_alog, d_dtb, d_onw, d_now])
    pair = _exchange("reduce_cores", ("c",), scat, [small_g])
    part = [_sum_slots("pair_sum_%d" % i, a, BF16) for i, a in enumerate(pair[:7])]
    part_small = _sum_slots("pair_sum_small", pair[7][:, None], F32)[0]
    lands = _exchange("reduce_chips", ("x", "y"), part, [part_small])
    gt_w_in = _sum_slots("sum_w_in", lands[0][:, None], F32)[0]
    lands[0] = gt_w_in.T[None]

    small_w = [norm_in_w, rw_mu, rw_w0, rw_a0, rw_k_k, rw_k_a, rw_r_k, rw_gn_w, rw_gn_b, gd_A_log, gd_dt_bias, gd_o_norm_w, norm_out_w]
    small_m = [m_norm_in_w, m_rw_mu, m_rw_w0, m_rw_a0, m_rw_k_k, m_rw_k_a, m_rw_r_k, m_rw_gn_w, m_rw_gn_b, m_gd_A_log, m_gd_dt_bias, m_gd_o_norm_w, m_norm_out_w]
    small_v = [v_norm_in_w, v_rw_mu, v_rw_w0, v_rw_a0, v_rw_k_k, v_rw_k_a, v_rw_r_k, v_rw_gn_w, v_rw_gn_b, v_gd_A_log, v_gd_dt_bias, v_gd_o_norm_w, v_norm_out_w]
    flat = lambda arrs: [a.reshape(1, -1) for a in arrs]
    sm = _adam_small(lands[7], flat(small_w), flat(small_m), flat(small_v))
    sm_g, sm_d, sm_m, sm_v = [{nm: res[i].reshape(w.shape) for (nm, _), res, w in zip(_SMALL, sm, small_w)}
                              for i in range(4)]

    big = {}
    for nm, land, w, m, v in (("w_in", lands[0], w_in, m_w_in, v_w_in), ("rw_w2", lands[1], rw_w2, m_rw_w2, v_rw_w2),
                              ("rw_a2", lands[2], rw_a2, m_rw_a2, v_rw_a2),
                              ("gd_conv_w", lands[3], gd_conv_w, m_gd_conv_w, v_gd_conv_w),
                              ("w_branch_a", lands[4], w_branch_a, m_w_branch_a, v_w_branch_a),
                              ("w_branch_b", lands[5], w_branch_b, m_w_branch_b, v_w_branch_b),
                              ("w_out", lands[6], w_out, m_w_out, v_w_out)):
        big[nm] = [o.reshape(w.shape) for o in _adam("adam_" + nm, land, w[0], m[0], v[0])]

    order = ["norm_in_w", "w_in", "rw_mu", "rw_w0", "rw_w2", "rw_a0", "rw_a2", "rw_k_k", "rw_k_a", "rw_r_k", "rw_gn_w",
             "rw_gn_b", "gd_conv_w", "gd_A_log", "gd_dt_bias", "gd_o_norm_w", "w_branch_a", "w_branch_b", "w_out", "norm_out_w"]
    pick = lambda nm, i: big[nm][i] if nm in big else (sm_g, sm_d, sm_m, sm_v)[i][nm]
    loss = lax.psum(loss_acc[0, 0], ("x", "y", "c"))
    grad_x = dx2.reshape(x.shape)
    return (loss, grad_x, *[pick(nm, 0) for nm in order], *[pick(nm, 1) for nm in order],
            *[pick(nm, 2) for nm in order], *[pick(nm, 3) for nm in order])
```

```python
import functools

import jax
import jax.numpy as jnp
from jax import lax
from jax.experimental import pallas as pl
from jax.experimental.pallas import tpu as pltpu

F32 = jnp.float32
BF16 = jnp.bfloat16
HI = lax.Precision.HIGHEST

LANES = 128
SUB = 8
CHUNK = 64
N_DEV = 8
VMEM_LIMIT = 56 * 1024 * 1024

D_MODEL = 1024
RW_W = 512
GD_W = 512
RW_SHIFT = 1664
NORM_EPS = 1e-6
RW_GN_EPS = 64 * 1e-5
ADAM_LR, ADAM_B1, ADAM_B2, ADAM_EPS, ADAM_WD, ADAM_STEP = 0.001, 0.9, 0.999, 1e-8, 0.01, 10


_NN, _NT, _TN = ((1,), (0,)), ((1,), (1,)), ((0,), (0,))


def _dot(a, b, dims, passes):
    precision = lax.Precision.HIGH if passes == 3 else lax.Precision.DEFAULT
    return lax.dot_general(a, b, (dims, ((), ())), precision=precision, preferred_element_type=F32)


def _mm(a, b, passes=3):
    return _dot(a, b, _NN, passes)


def _mm_nt(a, b, passes=3):
    return _dot(a, b, _NT, passes)


def _mm_tn(a, b, passes=3):
    return _dot(a, b, _TN, passes)


P_SUM = 3
P_SCORE = 1
P_INV = 1
P_STATE = 1
P_APPLY = 1
P_UPDATE = 1
P_POINT = 1


def _stack_rows(blocks):
    return jnp.concatenate(blocks, axis=0)


def _split_rows(x, n):
    r = x.shape[0] // n

    @jax.custom_vjp
    def split(x):
        return tuple(x[i * r:(i + 1) * r] for i in range(n))

    split.defvjp(lambda x: (split(x), None), lambda _, gs: (jnp.concatenate(gs, axis=0),))
    return split(x)


def _iota(shape, d):
    return lax.broadcasted_iota(jnp.int32, shape, d)


def _sigmoid(x):
    return 0.5 * (jnp.tanh(0.5 * x) + 1.0)


def _silu(x):
    return x * _sigmoid(x)


def _softplus(x):
    return jnp.maximum(x, 0.0) + jnp.log(1.0 + jnp.exp(-jnp.abs(x)))


def _seg_ones(seg):
    return ((_iota((LANES, LANES), 0) // seg) == (_iota((LANES, LANES), 1) // seg)).astype(F32)


def _sl(g):
    return slice(g * LANES, (g + 1) * LANES)


@jax.custom_vjp
def _tri_inverse(ms):
    return _tri_inverse_chain(ms)


def _tri_inverse_bwd(ts, dts):
    return ([-_mm_nt(_mm_tn(t, dt, P_INV), t, P_INV) for t, dt in zip(ts, dts)],)


def _tri_inverse_chain(ms):
    c = CHUNK
    ri, ci = _iota((c, c), 0), _iota((c, c), 1)
    eye = (ri == ci).astype(F32)
    d16 = (ri // 16) == (ci // 16)
    d32 = (ri // 32) == (ci // 32)
    ps = [jnp.where(d16, -m, 0.0) for m in ms]
    ts = [eye + p for p in ps]
    for _ in range(3):
        ps = [_mm(p, p, P_INV) for p in ps]
        ts = [_mm(t, eye + p, P_INV) for t, p in zip(ts, ps)]
    for off_diagonal in (d32 & (~d16), ~d32):
        tq = [_mm(t, jnp.where(off_diagonal, m, 0.0), P_INV) for t, m in zip(ts, ms)]
        ts = [t - _mm(a, t, P_INV) for t, a in zip(ts, tq)]
    return ts


_tri_inverse.defvjp(lambda ms: (lambda ts: (ts, ts))(_tri_inverse_chain(ms)), _tri_inverse_bwd)


@jax.custom_vjp
def _known_inverse(ms, ts):
    return ts


_known_inverse.defvjp(lambda ms, ts: (ts, ts),
                      lambda ts, dts: (_tri_inverse_bwd(ts, dts)[0], [jnp.zeros_like(t) for t in ts]))


def _chunk_fwd(prims, *, nsub, scalar_decay, inverses=None):
    c = CHUNK
    ng = len(prims)
    s0s, rs, lws, ks, vs, kks, bs = [list(t) for t in zip(*prims)]
    ri, ci = _iota((c, c), 0), _iota((c, c), 1)
    incl = ri >= ci
    strict = ri > ci
    tril = incl.astype(F32)
    hs = LANES // nsub
    lane = _iota((1, LANES), 1)
    masks = [((lane // hs) == s).astype(F32) for s in range(nsub)] if nsub > 1 else [1.0]
    cws = [_mm(tril, lw, P_SUM) for lw in lws]
    cwxs = [cw - lw for cw, lw in zip(cws, lws)]
    ends = [cw[c - 1:c, :] for cw in cws]
    kkds = [kk * jnp.exp(cwx) for kk, cwx in zip(kks, cwxs)]
    rds = [r * jnp.exp(cw) for r, cw in zip(rs, cws)]
    kends = [k * jnp.exp(e - cw) for k, e, cw in zip(ks, ends, cws)]
    bends = [b * jnp.exp(e - cw) for b, e, cw in zip(bs, ends, cws)]
    state_terms = [_split_rows(_mm_nt(_stack_rows([kkd, rd]), s0, P_STATE), 2) for kkd, rd, s0 in zip(kkds, rds, s0s)]
    w0s, y0s = [t[0] for t in state_terms], [t[1] for t in state_terms]
    chains = [(g, s) for g in range(ng) for s in range(nsub)]
    if scalar_decay:
        e0 = (lane == 0).astype(F32) * jnp.ones((c, 1), F32)
        rows = [_mm_nt(e0, cw, P_SUM) for cw in cws]
        dxs = [jnp.where(strict, jnp.exp(jnp.minimum(cwx[:, :c] - row, 0.0)), 0.0) for cwx, row in zip(cwxs, rows)]
        dis = [jnp.where(incl, jnp.exp(jnp.minimum(cw[:, :c] - row, 0.0)), 0.0) for cw, row in zip(cws, rows)]
        lefts = [_stack_rows([kk * m for m in masks] + [r * m for m in masks]) for kk, r in zip(kks, rs)]
        on_b = [_split_rows(_mm_nt(left, b, P_SCORE), 2 * nsub) for left, b in zip(lefts, bs)]
        on_k = [_split_rows(_mm_nt(left, k, P_SCORE), 2 * nsub) for left, k in zip(lefts, ks)]
        m_b = [on_b[g][s] * dxs[g] for g, s in chains]
        m_k = [on_k[g][s] * dxs[g] for g, s in chains]
        n_k = [on_k[g][nsub + s] * dis[g] for g, s in chains]
        n_b = [on_b[g][nsub + s] * dis[g] for g, s in chains]
    else:
        kds = [k * jnp.exp(-cw) for k, cw in zip(ks, cws)]
        bds = [b * jnp.exp(-cw) for b, cw in zip(bs, cws)]
        lefts = [_stack_rows([kkd * m for m in masks] + [rd * m for m in masks]) for kkd, rd in zip(kkds, rds)]
        on_b = [_split_rows(_mm_nt(left, bd, P_SCORE), 2 * nsub) for left, bd in zip(lefts, bds)]
        on_k = [_split_rows(_mm_nt(left, kd, P_SCORE), 2 * nsub) for left, kd in zip(lefts, kds)]
        m_b = [jnp.where(strict, on_b[g][s], 0.0) for g, s in chains]
        m_k = [jnp.where(strict, on_k[g][s], 0.0) for g, s in chains]
        n_k = [jnp.where(incl, on_k[g][nsub + s], 0.0) for g, s in chains]
        n_b = [jnp.where(incl, on_b[g][nsub + s], 0.0) for g, s in chains]
    t_inv = _tri_inverse(m_b) if inverses is None else _known_inverse(m_b, inverses)
    on_v = [_split_rows(_mm(_stack_rows([mk, nk]), vs[g], P_APPLY), 2) for (g, s), mk, nk in zip(chains, m_k, n_k)]
    sa_c = [_mm(t, w0s[g] + mv[0], P_APPLY) for (g, s), t, mv in zip(chains, t_inv, on_v)]
    y_c = [y0s[g] + mv[1] - _mm(nb, sa, P_APPLY) for (g, s), mv, nb, sa in zip(chains, on_v, n_b, sa_c)]
    per_group = lambda xs: [functools.reduce(lambda p, q: p + q, [xs[g * nsub + s] * masks[s] for s in range(nsub)])
                            for g in range(ng)]
    sas, ys = per_group(sa_c), per_group(y_c)
    s_ends = [s0 * jnp.exp(e) + _mm_tn(_stack_rows([v, -sa]), _stack_rows([kend, bend]), P_UPDATE)
              for s0, e, v, kend, sa, bend in zip(s0s, ends, vs, kends, sas, bends)]
    if nsub > 1:
        same_head = (_iota((LANES, LANES), 0) // hs) == (_iota((LANES, LANES), 1) // hs)
        s_ends = [jnp.where(same_head, s_end, 0.0) for s_end in s_ends]
    return list(zip(ys, s_ends)), t_inv


def _rec_fwd(name, r, lw, k, v, kk, b, *, seq, nsub, scalar_decay):
    n, w = r.shape
    ng = w // LANES
    nc = seq // CHUNK
    nb = n // seq
    nt = nb * ng * nsub

    def body(r_ref, lw_ref, k_ref, v_ref, kk_ref, b_ref, y_ref, s_ref, t_ref, state):
        @pl.when(pl.program_id(0) == 0)
        def _():
            state[...] = jnp.zeros_like(state)
        prims = [(state[bi * ng + g], r_ref[bi, :, _sl(g)], lw_ref[bi, :, _sl(g)], k_ref[bi, :, _sl(g)],
                  v_ref[bi, :, _sl(g)], kk_ref[bi, :, _sl(g)], b_ref[bi, :, _sl(g)])
                 for bi in range(nb) for g in range(ng)]
        outs, t_inv = _chunk_fwd(prims, nsub=nsub, scalar_decay=scalar_decay)
        for i, (y, s_end) in enumerate(outs):
            s_ref[0, i] = prims[i][0]
            y_ref[i // ng, :, _sl(i % ng)] = y
            state[i] = s_end
        for i, t in enumerate(t_inv):
            t_ref[0, i] = t

    row = pl.BlockSpec((nb, CHUNK, w), lambda c: (0, c, 0))
    seqs = lambda a: a.reshape(nb, seq, w)
    y, s_save, t_save = pl.pallas_call(
        body, name=name, grid=(nc,),
        in_specs=[row] * 6,
        out_specs=[row, pl.BlockSpec((1, nb * ng, LANES, LANES), lambda c: (c, 0, 0, 0)),
                   pl.BlockSpec((1, nt, CHUNK, CHUNK), lambda c: (c, 0, 0, 0))],
        out_shape=[jax.ShapeDtypeStruct((nb, seq, w), F32), jax.ShapeDtypeStruct((nc, nb * ng, LANES, LANES), F32),
                   jax.ShapeDtypeStruct((nc, nt, CHUNK, CHUNK), F32)],
        scratch_shapes=[pltpu.VMEM((nb * ng, LANES, LANES), F32)],
        compiler_params=pltpu.CompilerParams(dimension_semantics=("arbitrary",), vmem_limit_bytes=VMEM_LIMIT),
    )(seqs(r), seqs(lw), seqs(k), seqs(v), seqs(kk), seqs(b))
    return y.reshape(n, w), (s_save, t_save)


def _rec_bwd(name, r, lw, k, v, kk, b, saved, dy, *, seq, nsub, scalar_decay):
    n, w = r.shape
    ng = w // LANES
    nc = seq // CHUNK
    nb = n // seq
    nt = nb * ng * nsub
    s_save, t_save = saved

    def body(r_ref, lw_ref, k_ref, v_ref, kk_ref, b_ref, s_ref, t_ref, dy_ref,
             dr_ref, dlw_ref, dk_ref, dv_ref, dkk_ref, db_ref, dstate):
        @pl.when(pl.program_id(0) == 0)
        def _():
            dstate[...] = jnp.zeros_like(dstate)
        inverses = [t_ref[0, i] for i in range(nt)]
        f = lambda p: _chunk_fwd(p, nsub=nsub, scalar_decay=scalar_decay, inverses=inverses)[0]
        chains = [(bi, g) for bi in range(nb) for g in range(ng)]
        prims = [(s_ref[0, bi * ng + g], r_ref[bi, :, _sl(g)], lw_ref[bi, :, _sl(g)], k_ref[bi, :, _sl(g)],
                  v_ref[bi, :, _sl(g)], kk_ref[bi, :, _sl(g)], b_ref[bi, :, _sl(g)]) for bi, g in chains]
        _, vjp = jax.vjp(f, prims)
        (d_prims,) = vjp([(dy_ref[bi, :, _sl(g)], dstate[bi * ng + g]) for bi, g in chains])
        for (bi, g), (ds0, dr, dlw, dk, dv, dkk, db) in zip(chains, d_prims):
            dstate[bi * ng + g] = ds0
            dr_ref[bi, :, _sl(g)] = dr
            dlw_ref[bi, :, _sl(g)] = dlw
            dk_ref[bi, :, _sl(g)] = dk
            dv_ref[bi, :, _sl(g)] = dv
            dkk_ref[bi, :, _sl(g)] = dkk
            db_ref[bi, :, _sl(g)] = db

    row = pl.BlockSpec((nb, CHUNK, w), lambda c: (0, nc - 1 - c, 0))
    seqs = lambda a: a.reshape(nb, seq, w)
    grads = pl.pallas_call(
        body, name=name, grid=(nc,),
        in_specs=[row] * 6 + [pl.BlockSpec((1, nb * ng, LANES, LANES), lambda c: (nc - 1 - c, 0, 0, 0)),
                              pl.BlockSpec((1, nt, CHUNK, CHUNK), lambda c: (nc - 1 - c, 0, 0, 0)), row],
        out_specs=[row] * 6,
        out_shape=[jax.ShapeDtypeStruct((nb, seq, w), F32)] * 6,
        scratch_shapes=[pltpu.VMEM((nb * ng, LANES, LANES), F32)],
        compiler_params=pltpu.CompilerParams(dimension_semantics=("arbitrary",), vmem_limit_bytes=VMEM_LIMIT),
    )(seqs(r), seqs(lw), seqs(k), seqs(v), seqs(kk), seqs(b), s_save, t_save, seqs(dy))
    return [g.reshape(n, w) for g in grads]


def _shift_down(a, j, halo, is_start):
    tb = a.shape[0]
    rolled = pltpu.roll(a, j, 0)
    hr = jnp.where(is_start, 0.0, pltpu.roll(halo, j, 0))
    first = jnp.where(_iota((SUB, LANES), 0) < j, hr, rolled[0:SUB])
    if tb == SUB:
        return first
    return jnp.concatenate([first, rolled[SUB:]], axis=0)


def _shift_up(d, j, carry, is_end):
    tb = d.shape[0]
    up = pltpu.roll(d, tb - j, 0)
    cr = jnp.where(is_end, 0.0, pltpu.roll(carry, SUB - j, 0))
    last = jnp.where(_iota((SUB, LANES), 0) >= SUB - j, cr, up[tb - SUB:tb])
    if tb == SUB:
        return last
    return jnp.concatenate([up[:tb - SUB], last], axis=0)


def _ngroups(a):
    return a.shape[1] // LANES


def _pw_fwd(name, f, ins, shift, params, out_widths, out_dtypes, *, seq, tb):
    n = ins[0].shape[0]
    nt, tps = n // tb, seq // tb
    ni, npar = len(ins), len(params)

    def body(*refs):
        in_refs = refs[:ni]
        pos = ni
        halo_ref = None
        if shift:
            halo_ref = refs[pos]
            pos += 1
        p_refs = refs[pos:pos + npar]
        out_refs = refs[pos + npar:]
        is_start = (pl.program_id(0) % tps) == 0
        tiles = [[ref[:, _sl(g)] for g in range(_ngroups(ref))] for ref in in_refs]
        prevs = [[_shift_down(tiles[0][g], j, halo_ref[:, _sl(g)], is_start) for g in range(len(tiles[0]))]
                 for j in range(1, shift + 1)]
        pv = [[ref[:, _sl(g)] for g in range(_ngroups(ref))] for ref in p_refs]
        outs = f(tiles, prevs, pv)
        for o_ref, og in zip(out_refs, outs, strict=True):
            for g, t in enumerate(og):
                o_ref[:, _sl(g)] = t.astype(o_ref.dtype)

    in_specs = [pl.BlockSpec((tb, a.shape[1]), lambda i: (i, 0)) for a in ins]
    args = list(ins)
    if shift:
        in_specs.append(pl.BlockSpec((SUB, ins[0].shape[1]), lambda i: (jnp.maximum(i * (tb // SUB) - 1, 0), 0)))
        args.append(ins[0])
    in_specs += [pl.BlockSpec(p.shape, lambda i: (0, 0)) for p in params]
    args += list(params)
    return pl.pallas_call(
        body, name=name, grid=(nt,),
        in_specs=in_specs,
        out_specs=[pl.BlockSpec((tb, w), lambda i: (i, 0)) for w in out_widths],
        out_shape=[jax.ShapeDtypeStruct((n, w), dt) for w, dt in zip(out_widths, out_dtypes, strict=True)],
        compiler_params=pltpu.CompilerParams(dimension_semantics=("parallel",), vmem_limit_bytes=VMEM_LIMIT),
    )(*args)


def _pw_bwd(name, f, ins, shift, params, douts, din_dtypes, *, seq, tb):
    n = ins[0].shape[0]
    nt, tps = n // tb, seq // tb
    ni, npar = len(ins), len(params)
    flat_douts = [d for ds in douts for d in ds]
    nd = len(flat_douts)
    w0 = ins[0].shape[1]

    def body(*refs):
        in_refs = refs[:ni]
        pos = ni
        halo_ref = None
        if shift:
            halo_ref = refs[pos]
            pos += 1
        p_refs = refs[pos:pos + npar]
        pos += npar
        d_refs = refs[pos:pos + nd]
        pos += nd
        din_refs = refs[pos:pos + ni]
        pos += ni
        dp_refs = refs[pos:pos + npar]
        pos += npar
        carry = refs[pos] if shift else None
        step = pl.program_id(0)
        tile = nt - 1 - step
        is_start = (tile % tps) == 0
        is_end = (tile % tps) == tps - 1
        tiles = [[ref[:, _sl(g)] for g in range(_ngroups(ref))] for ref in in_refs]
        prevs = [[_shift_down(tiles[0][g], j, halo_ref[:, _sl(g)], is_start) for g in range(len(tiles[0]))]
                 for j in range(1, shift + 1)]
        pv = [[ref[:, _sl(g)] for g in range(_ngroups(ref))] for ref in p_refs]
        cot, pos_d = [], 0
        for ds in douts:
            grp = d_refs[pos_d:pos_d + len(ds)]
            pos_d += len(ds)
            cot.append([functools.reduce(lambda p, q: p + q, [ref[:, _sl(g)].astype(F32) for ref in grp])
                        for g in range(_ngroups(grp[0]))])
        _, vjp = jax.vjp(f, tiles, prevs, pv)
        d_tiles, d_prevs, d_pv = vjp(cot)
        for g in range(len(tiles[0])):
            for j in range(1, shift + 1):
                d_tiles[0][g] = d_tiles[0][g] + _shift_up(d_prevs[j - 1][g], j, carry[j - 1, :, _sl(g)], is_end)
            for j in range(1, shift + 1):
                carry[j - 1, :, _sl(g)] = d_prevs[j - 1][g][0:SUB]
        for ref, dg in zip(din_refs, d_tiles, strict=True):
            for g, t in enumerate(dg):
                ref[:, _sl(g)] = t.astype(ref.dtype)

        @pl.when(step == 0)
        def _():
            for ref in dp_refs:
                ref[...] = jnp.zeros_like(ref)
        for ref, dg in zip(dp_refs, d_pv, strict=True):
            for g, t in enumerate(dg):
                ref[:, _sl(g)] += t

    rev = lambda i: (nt - 1 - i, 0)
    in_specs = [pl.BlockSpec((tb, a.shape[1]), rev) for a in ins]
    args = list(ins)
    if shift:
        in_specs.append(pl.BlockSpec((SUB, w0), lambda i: (jnp.maximum((nt - 1 - i) * (tb // SUB) - 1, 0), 0)))
        args.append(ins[0])
    in_specs += [pl.BlockSpec(p.shape, lambda i: (0, 0)) for p in params]
    args += list(params)
    in_specs += [pl.BlockSpec((tb, d.shape[1]), rev) for d in flat_douts]
    args += flat_douts
    out_specs = [pl.BlockSpec((tb, a.shape[1]), rev) for a in ins] + [pl.BlockSpec(p.shape, lambda i: (0, 0)) for p in params]
    out_shape = ([jax.ShapeDtypeStruct(a.shape, dt) for a, dt in zip(ins, din_dtypes, strict=True)]
                 + [jax.ShapeDtypeStruct(p.shape, F32) for p in params])
    res = pl.pallas_call(
        body, name=name, grid=(nt,),
        in_specs=in_specs, out_specs=out_specs, out_shape=out_shape,
        scratch_shapes=[pltpu.VMEM((shift, SUB, w0), F32)] if shift else [],
        compiler_params=pltpu.CompilerParams(dimension_semantics=("arbitrary",), vmem_limit_bytes=VMEM_LIMIT),
    )(*args)
    return res[:ni], res[ni:]


def _rwkv_prep_f(tiles, prevs, params):
    (p,), (prev,) = tiles, prevs
    mu, w0, w2p, a0, a2p, k_k, k_a = params
    xs = [p[g] + (prev[g] - p[g]) * mu[g] for g in range(13)]
    wdad = xs[12]
    tw = jnp.tanh(wdad)
    e64 = _seg_ones(64)
    r, lw, k2, v, kk, b = [], [], [], [], [], []
    for g in range(4):
        k_g = xs[4 + g]
        lo = w0[g] + _mm(tw, w2p[g], P_POINT)
        lw_g = -jnp.exp(-_softplus(-lo) - 0.5)
        a_g = _sigmoid(a0[g] + _mm(wdad, a2p[g], P_POINT))
        kkp = k_g * k_k[g]
        kk_g = kkp * lax.rsqrt(_mm(kkp * kkp, e64, P_POINT) + 1e-12)
        r.append(xs[g])
        lw.append(lw_g)
        k2.append(k_g * (1.0 + (a_g - 1.0) * k_a[g]))
        v.append(xs[8 + g])
        kk.append(kk_g)
        b.append(kk_g * a_g)
    return [r, lw, k2, v, kk, b]


def _rwkv_post_f(tiles, prevs, params):
    yrec, r, k2, v, z = tiles
    gn_w, gn_b, r_k = params
    e64 = _seg_ones(64)
    out = []
    for g in range(4):
        mean = _mm(yrec[g], e64, P_POINT) * (1.0 / 64)
        d = yrec[g] - mean
        var = _mm(d * d, e64, P_POINT) * (1.0 / 64)
        yn = d * lax.rsqrt(var + RW_GN_EPS) * gn_w[g] + gn_b[g]
        bonus = _mm(r[g] * k2[g] * r_k[g], e64, P_POINT) * v[g]
        out.append((yn + bonus) * _silu(z[g]))
    return [out]


def _gdn_prep_f(tiles, prevs, params):
    x, (ba,) = tiles
    p1, p2, p3 = prevs
    cw0, cw1, cw2, cw3, a_log, dt_bias = params
    s = [_silu(cw3[g] * x[g] + cw2[g] * p1[g] + cw1[g] * p2[g] + cw0[g] * p3[g]) for g in range(12)]
    row = _iota((LANES, LANES), 0)
    r, lw, k, vv, b = [], [], [], [], []
    for h in range(4):
        q_h, k_h, v_h = s[h], s[4 + h], s[8 + h]
        qn = q_h * lax.rsqrt(jnp.sum(q_h * q_h, axis=-1, keepdims=True) + 1e-12)
        kn = k_h * lax.rsqrt(jnp.sum(k_h * k_h, axis=-1, keepdims=True) + 1e-12)
        beta = _sigmoid(_mm(ba, (row == h).astype(F32)))
        alpha = _mm(ba, (row == 4 + h).astype(F32))
        g_h = -jnp.exp(a_log[h]) * _softplus(alpha + dt_bias[h])
        r.append(qn * (LANES ** -0.5))
        lw.append(g_h)
        k.append(kn)
        vv.append(beta * v_h)
        b.append(jnp.exp(g_h) * beta * kn)
    return [r, lw, k, vv, b]


def _gdn_post_f(tiles, prevs, params):
    o, z = tiles
    ((onw,),) = params
    out = []
    for h in range(4):
        ms = jnp.mean(o[h] * o[h], axis=-1, keepdims=True)
        out.append(o[h] * lax.rsqrt(ms + NORM_EPS) * onw * _silu(z[h]))
    return [out]


def _norm_in(x2, g_in, *, tm):
    n = x2.shape[0]

    def body(x_ref, g_ref, h_ref):
        x = x_ref[...]
        rs = lax.rsqrt(jnp.mean(x * x, axis=-1, keepdims=True) + NORM_EPS)
        h_ref[...] = (x * rs * g_ref[...]).astype(BF16)

    return pl.pallas_call(
        body, name="norm_in", grid=(n // tm,),
        in_specs=[pl.BlockSpec((tm, D_MODEL), lambda i: (i, 0)), pl.BlockSpec((1, D_MODEL), lambda i: (0, 0))],
        out_specs=pl.BlockSpec((tm, D_MODEL), lambda i: (i, 0)),
        out_shape=jax.ShapeDtypeStruct((n, D_MODEL), BF16),
        compiler_params=pltpu.CompilerParams(dimension_semantics=("parallel",), vmem_limit_bytes=VMEM_LIMIT),
    )(x2, g_in)


def _proj(name, h, wt, *, tm):
    n, ws = h.shape[0], wt.shape[0]

    def body(h_ref, w_ref, o_ref):
        o_ref[...] = lax.dot_general(h_ref[...], w_ref[...], (_NT, ((), ())), preferred_element_type=F32)

    return pl.pallas_call(
        body, name=name, grid=(n // tm,),
        in_specs=[pl.BlockSpec((tm, D_MODEL), lambda i: (i, 0)), pl.BlockSpec((ws, D_MODEL), lambda i: (0, 0))],
        out_specs=pl.BlockSpec((tm, ws), lambda i: (i, 0)),
        out_shape=jax.ShapeDtypeStruct((n, ws), F32),
        compiler_params=pltpu.CompilerParams(dimension_semantics=("parallel",), vmem_limit_bytes=VMEM_LIMIT),
    )(h, wt)


def _proj_dw(name, h, dp, *, tm):
    n, ws = dp.shape

    def body(h_ref, d_ref, o_ref):
        @pl.when(pl.program_id(0) == 0)
        def _():
            o_ref[...] = jnp.zeros_like(o_ref)
        o_ref[...] += lax.dot_general(d_ref[...], h_ref[...], (_TN, ((), ())), preferred_element_type=F32)

    return pl.pallas_call(
        body, name=name, grid=(n // tm,),
        in_specs=[pl.BlockSpec((tm, D_MODEL), lambda i: (i, 0)), pl.BlockSpec((tm, ws), lambda i: (i, 0))],
        out_specs=pl.BlockSpec((ws, D_MODEL), lambda i: (0, 0)),
        out_shape=jax.ShapeDtypeStruct((ws, D_MODEL), F32),
        compiler_params=pltpu.CompilerParams(dimension_semantics=("arbitrary",), vmem_limit_bytes=VMEM_LIMIT),
    )(h, dp)


def _proj_dx(x2, g_in, d_xo, dps, ws, *, tm):
    n = x2.shape[0]
    ns = len(dps)

    def body(*refs):
        x_ref, g_ref, dxo_ref = refs[:3]
        dp_refs = refs[3:3 + ns]
        w_refs = refs[3 + ns:3 + 2 * ns]
        dx_ref, dg_ref = refs[3 + 2 * ns:]
        dh = jnp.zeros((tm, D_MODEL), F32)
        for d_ref, w_ref in zip(dp_refs, w_refs, strict=True):
            dh = dh + jnp.dot(d_ref[...], w_ref[...], preferred_element_type=F32)
        x = x_ref[...]
        rs = lax.rsqrt(jnp.mean(x * x, axis=-1, keepdims=True) + NORM_EPS)
        xn = x * rs
        dxn = dh * g_ref[...]
        dx_ref[...] = dxo_ref[...] + rs * (dxn - xn * jnp.mean(dxn * xn, axis=-1, keepdims=True))

        @pl.when(pl.program_id(0) == 0)
        def _():
            dg_ref[...] = jnp.zeros_like(dg_ref)
        dg_ref[...] += jnp.sum(dh * xn, axis=0, keepdims=True)

    row = pl.BlockSpec((tm, D_MODEL), lambda i: (i, 0))
    return pl.pallas_call(
        body, name="proj_dx", grid=(n // tm,),
        in_specs=([row, pl.BlockSpec((1, D_MODEL), lambda i: (0, 0)), row]
                  + [pl.BlockSpec((tm, d.shape[1]), lambda i: (i, 0)) for d in dps]
                  + [pl.BlockSpec(w.shape, lambda i: (0, 0)) for w in ws]),
        out_specs=[row, pl.BlockSpec((1, D_MODEL), lambda i: (0, 0))],
        out_shape=[jax.ShapeDtypeStruct((n, D_MODEL), F32), jax.ShapeDtypeStruct((1, D_MODEL), F32)],
        compiler_params=pltpu.CompilerParams(dimension_semantics=("arbitrary",), vmem_limit_bytes=VMEM_LIMIT),
    )(x2, g_in, d_xo, *dps, *ws)


def _tail(x2, tgt2, gates, ya, yb, w_a, w_b, w_o, now, *, tr):
    n = x2.shape[0]

    def body(x_ref, t_ref, g_ref, ya_ref, yb_ref, wa_ref, wb_ref, wo_ref, now_ref,
             dya_ref, dyb_ref, dg_ref, dxo_ref, dwa_ref, dwb_ref, dwo_ref, dnow_ref, loss_ref):
        ya16, yb16 = ya_ref[...].astype(BF16), yb_ref[...].astype(BF16)
        ua = jnp.dot(ya16, wa_ref[...], preferred_element_type=F32)
        ub = jnp.dot(yb16, wb_ref[...], preferred_element_type=F32)
        ga = _sigmoid(g_ref[:, :D_MODEL])
        gb = _sigmoid(g_ref[:, D_MODEL:])
        m16 = (ga * ua + gb * ub).astype(BF16)
        xo = x_ref[...] + jnp.dot(m16, wo_ref[...], preferred_element_type=F32)
        rs = lax.rsqrt(jnp.mean(xo * xo, axis=-1, keepdims=True) + NORM_EPS)
        yn = xo * rs
        now_v = now_ref[...]
        err = yn * now_v - t_ref[...]
        dy = err * (1.0 / D_MODEL)
        dyn = dy * now_v
        dxo = rs * (dyn - yn * jnp.mean(dyn * yn, axis=-1, keepdims=True))
        dxo_ref[...] = dxo
        dxo16 = dxo.astype(BF16)
        dm = lax.dot_general(dxo16, wo_ref[...], (((1,), (1,)), ((), ())), preferred_element_type=F32)
        dua16 = (dm * ga).astype(BF16)
        dub16 = (dm * gb).astype(BF16)
        dg_ref[:, :D_MODEL] = (dm * ua * ga * (1.0 - ga)).astype(dg_ref.dtype)
        dg_ref[:, D_MODEL:] = (dm * ub * gb * (1.0 - gb)).astype(dg_ref.dtype)
        dya_ref[...] = lax.dot_general(dua16, wa_ref[...], (((1,), (1,)), ((), ())), preferred_element_type=F32)
        dyb_ref[...] = lax.dot_general(dub16, wb_ref[...], (((1,), (1,)), ((), ())), preferred_element_type=F32)

        @pl.when(pl.program_id(0) == 0)
        def _():
            for ref in (dwa_ref, dwb_ref, dwo_ref, dnow_ref, loss_ref):
                ref[...] = jnp.zeros_like(ref)
        tn = (((0,), (0,)), ((), ()))
        dwo_ref[...] += lax.dot_general(m16, dxo16, tn, preferred_element_type=F32)
        dwa_ref[...] += lax.dot_general(ya16, dua16, tn, preferred_element_type=F32)
        dwb_ref[...] += lax.dot_general(yb16, dub16, tn, preferred_element_type=F32)
        dnow_ref[...] += jnp.sum(dy * yn, axis=0, keepdims=True)
        loss_ref[...] += (0.5 / D_MODEL) * jnp.sum(err * err)

    row = lambda w: pl.BlockSpec((tr, w), lambda i: (i, 0))
    full = lambda a: pl.BlockSpec(a.shape, lambda i: (0, 0))
    return pl.pallas_call(
        body, name="tail", grid=(n // tr,),
        in_specs=[row(D_MODEL), row(D_MODEL), row(2 * D_MODEL), row(RW_W), row(GD_W), full(w_a), full(w_b), full(w_o), full(now)],
        out_specs=[row(RW_W), row(GD_W), row(2 * D_MODEL), row(D_MODEL),
                   pl.BlockSpec((RW_W, D_MODEL), lambda i: (0, 0)), pl.BlockSpec((GD_W, D_MODEL), lambda i: (0, 0)),
                   pl.BlockSpec((D_MODEL, D_MODEL), lambda i: (0, 0)), pl.BlockSpec((1, D_MODEL), lambda i: (0, 0)),
                   pl.BlockSpec((SUB, LANES), lambda i: (0, 0))],
        out_shape=[jax.ShapeDtypeStruct((n, RW_W), F32), jax.ShapeDtypeStruct((n, GD_W), F32),
                   jax.ShapeDtypeStruct((n, 2 * D_MODEL), BF16), jax.ShapeDtypeStruct((n, D_MODEL), F32),
                   jax.ShapeDtypeStruct((RW_W, D_MODEL), F32), jax.ShapeDtypeStruct((GD_W, D_MODEL), F32),
                   jax.ShapeDtypeStruct((D_MODEL, D_MODEL), F32), jax.ShapeDtypeStruct((1, D_MODEL), F32),
                   jax.ShapeDtypeStruct((SUB, LANES), F32)],
        compiler_params=pltpu.CompilerParams(dimension_semantics=("arbitrary",), vmem_limit_bytes=VMEM_LIMIT),
    )(x2, tgt2, gates, ya, yb, w_a, w_b, w_o, now)


def _exchange(name, axes, scatter, gather):
    ns, ng = len(scatter), len(gather)
    na = ns + ng
    gs = 2 ** len(axes)
    arrs = list(scatter) + list(gather)

    def body(*refs):
        src = refs[:na]
        dst = refs[na:2 * na]
        send_sems, recv_sems = refs[2 * na:]
        mine = {ax: lax.axis_index(ax) for ax in ("x", "y", "c")}

        def peer(k):
            co = dict(mine)
            for i, ax in enumerate(axes):
                if (k >> (len(axes) - 1 - i)) & 1:
                    co[ax] = 1 - co[ax]
            idx = 0
            for ax in axes:
                idx = 2 * idx + co[ax]
            return (co["x"], co["y"], co["c"]), idx

        _, me = peer(0)

        def copy(a, k, landing):
            dev, idx = peer(k)
            s = src[a].at[idx] if a < ns else src[a]
            return pltpu.make_async_remote_copy(src_ref=s, dst_ref=dst[a].at[idx if landing else me],
                                                send_sem=send_sems.at[a, k - 1], recv_sem=recv_sems.at[a, k - 1],
                                                device_id=dev, device_id_type=pl.DeviceIdType.MESH)

        sends = [copy(a, k, False) for a in range(na) for k in range(1, gs)]
        for cp in sends:
            cp.start()
        for a in range(na):
            for k in range(1, gs):
                copy(a, k, True).wait_recv()
        for cp in sends:
            cp.wait_send()

    out_shape = [jax.ShapeDtypeStruct(a.shape, a.dtype) for a in scatter] + \
                [jax.ShapeDtypeStruct((gs,) + a.shape, a.dtype) for a in gather]
    anyspec = pl.BlockSpec(memory_space=pl.ANY)
    lands = pl.pallas_call(
        body, name=name,
        in_specs=[anyspec] * na, out_specs=[anyspec] * na, out_shape=out_shape,
        scratch_shapes=[pltpu.SemaphoreType.DMA((na, gs - 1)), pltpu.SemaphoreType.DMA((na, gs - 1))],
    )(*arrs)
    me = 0
    for ax in axes:
        me = 2 * me + lax.axis_index(ax)
    kept = [lax.dynamic_index_in_dim(a, me, 0, keepdims=False) for a in scatter] + list(gather)
    return [lax.dynamic_update_index_in_dim(land, mine, me, 0) for land, mine in zip(lands, kept)]


def _sum_slots(name, land, out_dtype):
    ns, nq, r, c = land.shape

    def body(l_ref, o_ref):
        acc = l_ref[0, 0].astype(F32)
        for s in range(1, ns):
            acc = acc + l_ref[s, 0].astype(F32)
        o_ref[0] = acc.astype(o_ref.dtype)

    return pl.pallas_call(
        body, name=name, grid=(nq,),
        in_specs=[pl.BlockSpec((ns, 1, r, c), lambda i: (0, i, 0, 0))],
        out_specs=pl.BlockSpec((1, r, c), lambda i: (i, 0, 0)),
        out_shape=jax.ShapeDtypeStruct((nq, r, c), out_dtype),
        compiler_params=pltpu.CompilerParams(dimension_semantics=("parallel",), vmem_limit_bytes=VMEM_LIMIT),
    )(land)


def _adam(name, land, w, m, v):
    r, c = w.shape
    nslot = land.shape[0]
    tr = 256 if (r % 256 == 0 and r > 256) else r

    def body(l_ref, w_ref, m_ref, v_ref, g_out, d_out, m_out, v_out):
        g = l_ref[0].astype(F32)
        for s in range(1, nslot):
            g = g + l_ref[s].astype(F32)
        g_out[...] = g
        d_out[...], m_out[...], v_out[...] = _adam_math(g, w_ref[...], m_ref[...], v_ref[...])

    blk = pl.BlockSpec((tr, c), lambda i: (i, 0))
    return pl.pallas_call(
        body, name=name, grid=(r // tr,),
        in_specs=[pl.BlockSpec((nslot, tr, c), lambda i: (0, i, 0)), blk, blk, blk],
        out_specs=[blk] * 4,
        out_shape=[jax.ShapeDtypeStruct((r, c), F32)] * 4,
        compiler_params=pltpu.CompilerParams(dimension_semantics=("parallel",), vmem_limit_bytes=VMEM_LIMIT),
    )(land, w, m, v)


def _adam_math(g, w, m, v):
    c1 = 1.0 / (1.0 - ADAM_B1 ** ADAM_STEP)
    c2 = 1.0 / (1.0 - ADAM_B2 ** ADAM_STEP)
    m_new = ADAM_B1 * m + (1.0 - ADAM_B1) * g
    v_new = ADAM_B2 * v + (1.0 - ADAM_B2) * (g * g)
    return -ADAM_LR * ((m_new * c1) / (jnp.sqrt(v_new * c2) + ADAM_EPS) + ADAM_WD * w), m_new, v_new


def _adam_small(land, ws, ms, vs):
    npar = len(ws)
    nslot = land.shape[0]

    def body(*refs):
        l_ref = refs[0]
        w_refs, m_refs, v_refs = refs[1:1 + npar], refs[1 + npar:1 + 2 * npar], refs[1 + 2 * npar:1 + 3 * npar]
        outs = refs[1 + 3 * npar:1 + 7 * npar]
        loss_ref, g_rows = refs[1 + 7 * npar], refs[2 + 7 * npar]
        g = l_ref[0]
        for s in range(1, nslot):
            g = g + l_ref[s]
        g_rows[...] = g
        row = 0
        for i, (_, size) in enumerate(_SMALL):
            for j in range(-(-size // LANES)):
                width = min(LANES, size - j * LANES)
                cols = slice(j * LANES, j * LANES + width)
                g_ij = g_rows[row:row + 1, 0:width]
                delta, m_new, v_new = _adam_math(g_ij, w_refs[i][:, cols], m_refs[i][:, cols], v_refs[i][:, cols])
                for ref, val in zip(outs[4 * i:4 * i + 4], (g_ij, delta, m_new, v_new)):
                    ref[:, cols] = val
                row += 1
        loss_ref[...] = g_rows[row:row + 1, :]

    full = lambda a: pl.BlockSpec(a.shape, lambda: (0,) * a.ndim)
    res = pl.pallas_call(
        body, name="adam_small",
        in_specs=[full(land)] + [full(a) for a in list(ws) + list(ms) + list(vs)],
        out_specs=[full(w) for w in ws for _ in range(4)] + [pl.BlockSpec((1, LANES), lambda: (0, 0))],
        out_shape=[jax.ShapeDtypeStruct(w.shape, F32) for w in ws for _ in range(4)] + [jax.ShapeDtypeStruct((1, LANES), F32)],
        scratch_shapes=[pltpu.VMEM(land.shape[1:], F32)],
    )(land, *ws, *ms, *vs)
    return [res[4 * i:4 * i + 4] for i in range(npar)], res[4 * npar]


_SMALL = (("norm_in_w", 1024), ("rw_mu", 1664), ("rw_w0", 512), ("rw_a0", 512), ("rw_k_k", 512), ("rw_k_a", 512),
          ("rw_r_k", 512), ("rw_gn_w", 512), ("rw_gn_b", 512), ("gd_A_log", 4), ("gd_dt_bias", 4), ("gd_o_norm_w", 128),
          ("norm_out_w", 1024))
_SMALL_ROWS = 64


def _pack_small(vals, loss_row):
    rows = []
    for (_, size), a in zip(_SMALL, vals, strict=True):
        flat = a.reshape(-1).astype(F32)
        pad = (-size) % LANES
        if pad:
            flat = jnp.concatenate([flat, jnp.zeros((pad,), F32)])
        rows.append(flat.reshape(-1, LANES))
    rows.append(loss_row)
    used = sum(r.shape[0] for r in rows)
    rows.append(jnp.zeros((_SMALL_ROWS - used, LANES), F32))
    return jnp.concatenate(rows, axis=0)


def kernel(x, norm_in_w, w_in, rw_mu, rw_w0, rw_w2, rw_a0, rw_a2, rw_k_k, rw_k_a, rw_r_k, rw_gn_w, rw_gn_b, gd_conv_w, gd_A_log, gd_dt_bias, gd_o_norm_w, w_branch_a, w_branch_b, w_out, norm_out_w, loss_target, m_norm_in_w, m_w_in, m_rw_mu, m_rw_w0, m_rw_w2, m_rw_a0, m_rw_a2, m_rw_k_k, m_rw_k_a, m_rw_r_k, m_rw_gn_w, m_rw_gn_b, m_gd_conv_w, m_gd_A_log, m_gd_dt_bias, m_gd_o_norm_w, m_w_branch_a, m_w_branch_b, m_w_out, m_norm_out_w, v_norm_in_w, v_w_in, v_rw_mu, v_rw_w0, v_rw_w2, v_rw_a0, v_rw_a2, v_rw_k_k, v_rw_k_a, v_rw_r_k, v_rw_gn_w, v_rw_gn_b, v_gd_conv_w, v_gd_A_log, v_gd_dt_bias, v_gd_o_norm_w, v_w_branch_a, v_w_branch_b, v_w_out, v_norm_out_w):
    nb, seq, _ = x.shape
    n = nb * seq
    tm = min(512, n)
    tb = min(512, seq)
    x2 = x.reshape(n, D_MODEL)
    tgt2 = loss_target.reshape(n, D_MODEL)
    cols = w_in.shape[2]
    in_cols = cols * N_DEV

    sharded = [w_in[0].T.astype(BF16), rw_w2[0], rw_a2[0], gd_conv_w[0], w_branch_a[0].astype(BF16),
               w_branch_b[0].astype(BF16), w_out[0].astype(BF16)]
    by_chip = _exchange("gather_chips", ("x", "y"), [], sharded)
    g_win, g_w2, g_a2, g_conv, g_wa, g_wb, g_wo = _exchange("gather_cores", ("c",), [], by_chip)
    unshard_rows = lambda a: jnp.transpose(a, (1, 0, 2, 3)).reshape(N_DEV * a.shape[2], a.shape[3])
    unshard_cols = lambda a: jnp.transpose(a, (2, 1, 0, 3)).reshape(a.shape[2], N_DEV * a.shape[3])
    wt_full = unshard_rows(g_win)
    seg_bounds = ((0, 1664), (1664, 2176), (2176, 3712), (3712, 4224), (4232, in_cols))
    w_rw, w_zrw, w_qkv, w_zgd, w_gates = [wt_full[a:b] for a, b in seg_bounds]
    w_ba = jnp.concatenate([wt_full[4224:4232], jnp.zeros((LANES - 8, D_MODEL), BF16)], axis=0)
    w2_full, a2_full = unshard_cols(g_w2), unshard_cols(g_a2)
    zeros64 = jnp.zeros((64, RW_W), F32)
    w2p = jnp.concatenate([w2_full, zeros64], axis=0)
    a2p = jnp.concatenate([zeros64, a2_full], axis=0)
    conv_full = unshard_cols(g_conv)
    conv_rows = [conv_full[i:i + 1] for i in range(4)]
    wa_full = unshard_cols(g_wa)
    wb_full = unshard_cols(g_wb)
    wo_full = unshard_rows(g_wo)
    a_log_bc = jnp.repeat(gd_A_log, LANES, axis=1)
    dt_bias_bc = jnp.repeat(gd_dt_bias, LANES, axis=1)
    r_k_flat = rw_r_k.reshape(1, RW_W)
    now2 = norm_out_w.reshape(1, D_MODEL)

    h = _norm_in(x2, norm_in_w, tm=tm)
    p_rw = _proj("proj_rw", h, w_rw, tm=tm)
    p_zrw = _proj("proj_zrw", h, w_zrw, tm=tm)
    p_qkv = _proj("proj_qkv", h, w_qkv, tm=tm)
    p_zgd = _proj("proj_zgd", h, w_zgd, tm=tm)
    p_ba = _proj("proj_ba", h, w_ba, tm=tm)
    p_gates = _proj("proj_gates", h, w_gates, tm=tm)

    rw_params = [rw_mu, rw_w0, w2p, rw_a0, a2p, rw_k_k, rw_k_a]
    r_a, lw_a, k_a, v_a, kk_a, b_a = _pw_fwd("rwkv_prep", _rwkv_prep_f, [p_rw], 1, rw_params, [RW_W] * 6, [F32] * 6,
                                             seq=seq, tb=tb)
    y_rec, s_a = _rec_fwd("rwkv_rec", r_a, lw_a, k_a, v_a, kk_a, b_a, seq=seq, nsub=2, scalar_decay=False)
    post_params = [rw_gn_w, rw_gn_b, r_k_flat]
    (y_a,) = _pw_fwd("rwkv_post", _rwkv_post_f, [y_rec, r_a, k_a, v_a, p_zrw], 0, post_params, [RW_W], [F32], seq=seq, tb=tb)

    gd_params = conv_rows + [a_log_bc, dt_bias_bc]
    r_b, lw_b, k_b, v_b, b_b = _pw_fwd("gdn_prep", _gdn_prep_f, [p_qkv, p_ba], 3, gd_params, [GD_W] * 5, [F32] * 5,
                                       seq=seq, tb=tb)
    o_rec, s_b = _rec_fwd("gdn_rec", r_b, lw_b, k_b, v_b, k_b, b_b, seq=seq, nsub=1, scalar_decay=True)
    (y_b,) = _pw_fwd("gdn_post", _gdn_post_f, [o_rec, p_zgd], 0, [gd_o_norm_w], [GD_W], [F32], seq=seq, tb=tb)

    d_ya, d_yb, d_gates, d_xo, dwa, dwb, dwo, d_now, loss_acc = _tail(
        x2, tgt2, p_gates, y_a, y_b, wa_full, wb_full, wo_full, now2, tr=min(256, n))

    (d_o, d_zgd), (d_onw,) = _pw_bwd("gdn_post_bwd", _gdn_post_f, [o_rec, p_zgd], 0, [gd_o_norm_w], [[d_yb]],
                                     [F32, BF16], seq=seq, tb=tb)
    dr_b, dlw_b, dk_b, dv_b, dkk_b, db_b = _rec_bwd("gdn_rec_bwd", r_b, lw_b, k_b, v_b, k_b, b_b, s_b, d_o,
                                                    seq=seq, nsub=1, scalar_decay=True)
    (d_qkv, d_ba), d_gd_params = _pw_bwd("gdn_prep_bwd", _gdn_prep_f, [p_qkv, p_ba], 3, gd_params,
                                         [[dr_b], [dlw_b], [dk_b, dkk_b], [dv_b], [db_b]], [BF16, BF16], seq=seq, tb=tb)

    (d_yrec, dr_p, dk_p, dv_p, d_zrw), d_post_params = _pw_bwd(
        "rwkv_post_bwd", _rwkv_post_f, [y_rec, r_a, k_a, v_a, p_zrw], 0, post_params, [[d_ya]],
        [F32, F32, F32, F32, BF16], seq=seq, tb=tb)
    dr_a, dlw_a, dk_a, dv_a, dkk_a, db_a = _rec_bwd("rwkv_rec_bwd", r_a, lw_a, k_a, v_a, kk_a, b_a, s_a, d_yrec,
                                                    seq=seq, nsub=2, scalar_decay=False)
    (d_prw,), d_rw_params = _pw_bwd("rwkv_prep_bwd", _rwkv_prep_f, [p_rw], 1, rw_params,
                                    [[dr_a, dr_p], [dlw_a], [dk_a, dk_p], [dv_a, dv_p], [dkk_a], [db_a]], [BF16],
                                    seq=seq, tb=tb)

    dps = [d_prw, d_zrw, d_qkv, d_zgd, d_ba, d_gates]
    wsegs = [w_rw, w_zrw, w_qkv, w_zgd, w_ba, w_gates]
    dx2, d_gin = _proj_dx(x2, norm_in_w, d_xo, dps, wsegs, tm=min(256, n))
    dw_rw = _proj_dw("dw_rw", h, d_prw, tm=tm)
    dw_zrw = _proj_dw("dw_zrw", h, d_zrw, tm=tm)
    dw_qkv = _proj_dw("dw_qkv", h, d_qkv, tm=tm)
    dw_zgd = _proj_dw("dw_zgd", h, d_zgd, tm=tm)
    dw_ba = _proj_dw("dw_ba", h, d_ba, tm=tm)
    dw_gates = _proj_dw("dw_gates", h, d_gates, tm=tm)
    dwt_in_full = jnp.concatenate([dw_rw, dw_zrw, dw_qkv, dw_zgd, dw_ba[:8], dw_gates], axis=0)

    shard_cols = lambda a: jnp.transpose(a.reshape(a.shape[0], 4, 2, a.shape[1] // N_DEV), (2, 1, 0, 3))
    shard_rows = lambda a: jnp.transpose(a.reshape(4, 2, a.shape[0] // N_DEV, a.shape[1]), (1, 0, 2, 3))
    d_mu, d_w0, d_w2p, d_a0, d_a2p, d_kk_, d_ka_ = d_rw_params
    d_gnw, d_gnb, d_rk = d_post_params
    d_conv = jnp.concatenate(d_gd_params[:4], axis=0)
    d_alog = d_gd_params[4].reshape(4, LANES).sum(axis=1).reshape(1, 4)
    d_dtb = d_gd_params[5].reshape(4, LANES).sum(axis=1).reshape(1, 4)
    scat = [shard_rows(dwt_in_full), shard_cols(d_w2p[:64]), shard_cols(d_a2p[64:]), shard_cols(d_conv),
            shard_cols(dwa), shard_cols(dwb), shard_rows(dwo)]
    small_g = _pack_small([d_gin, d_mu, d_w0, d_a0, d_kk_, d_ka_, d_rk, d_gnw, d_gnb, d_alog, d_dtb, d_onw, d_now],
                          loss_acc[0:1])
    pair = _exchange("reduce_cores", ("c",), scat, [small_g])
    part = [_sum_slots("pair_sum_%d" % i, a, BF16) for i, a in enumerate(pair[:7])]
    part_small = _sum_slots("pair_sum_small", pair[7][:, None], F32)[0]
    lands = _exchange("reduce_chips", ("x", "y"), part, [part_small])
    gt_w_in = _sum_slots("sum_w_in", lands[0][:, None], F32)[0]
    lands[0] = gt_w_in.T[None]

    small_w = [norm_in_w, rw_mu, rw_w0, rw_a0, rw_k_k, rw_k_a, rw_r_k, rw_gn_w, rw_gn_b, gd_A_log, gd_dt_bias, gd_o_norm_w, norm_out_w]
    small_m = [m_norm_in_w, m_rw_mu, m_rw_w0, m_rw_a0, m_rw_k_k, m_rw_k_a, m_rw_r_k, m_rw_gn_w, m_rw_gn_b, m_gd_A_log, m_gd_dt_bias, m_gd_o_norm_w, m_norm_out_w]
    small_v = [v_norm_in_w, v_rw_mu, v_rw_w0, v_rw_a0, v_rw_k_k, v_rw_k_a, v_rw_r_k, v_rw_gn_w, v_rw_gn_b, v_gd_A_log, v_gd_dt_bias, v_gd_o_norm_w, v_norm_out_w]
    flat = lambda arrs: [a.reshape(1, -1) for a in arrs]
    sm, loss_row = _adam_small(lands[7], flat(small_w), flat(small_m), flat(small_v))
    sm_g, sm_d, sm_m, sm_v = [{nm: res[i].reshape(w.shape) for (nm, _), res, w in zip(_SMALL, sm, small_w)}
                              for i in range(4)]

    big = {}
    for nm, land, w, m, v in (("w_in", lands[0], w_in, m_w_in, v_w_in), ("rw_w2", lands[1], rw_w2, m_rw_w2, v_rw_w2),
                              ("rw_a2", lands[2], rw_a2, m_rw_a2, v_rw_a2),
                              ("gd_conv_w", lands[3], gd_conv_w, m_gd_conv_w, v_gd_conv_w),
                              ("w_branch_a", lands[4], w_branch_a, m_w_branch_a, v_w_branch_a),
                              ("w_branch_b", lands[5], w_branch_b, m_w_branch_b, v_w_branch_b),
                              ("w_out", lands[6], w_out, m_w_out, v_w_out)):
        big[nm] = [o.reshape(w.shape) for o in _adam("adam_" + nm, land, w[0], m[0], v[0])]

    order = ["norm_in_w", "w_in", "rw_mu", "rw_w0", "rw_w2", "rw_a0", "rw_a2", "rw_k_k", "rw_k_a", "rw_r_k", "rw_gn_w",
             "rw_gn_b", "gd_conv_w", "gd_A_log", "gd_dt_bias", "gd_o_norm_w", "w_branch_a", "w_branch_b", "w_out", "norm_out_w"]
    pick = lambda nm, i: big[nm][i] if nm in big else (sm_g, sm_d, sm_m, sm_v)[i][nm]
    loss = loss_row[0, 0]
    grad_x = dx2.reshape(x.shape)
    return (loss, grad_x, *[pick(nm, 0) for nm in order], *[pick(nm, 1) for nm in order],
            *[pick(nm, 2) for nm in order], *[pick(nm, 3) for nm in order])
```

```python
import functools

import jax
import jax.numpy as jnp
from jax import lax
from jax.experimental import pallas as pl
from jax.experimental.pallas import tpu as pltpu

F32 = jnp.float32
BF16 = jnp.bfloat16
HI = lax.Precision.HIGHEST

LANES = 128
SUB = 8
CHUNK = 64
N_DEV = 8
VMEM_LIMIT = 56 * 1024 * 1024

D_MODEL = 1024
RW_W = 512
GD_W = 512
RW_SHIFT = 1664
NORM_EPS = 1e-6
RW_GN_EPS = 64 * 1e-5
ADAM_LR, ADAM_B1, ADAM_B2, ADAM_EPS, ADAM_WD, ADAM_STEP = 0.001, 0.9, 0.999, 1e-8, 0.01, 10


_NN, _NT, _TN = ((1,), (0,)), ((1,), (1,)), ((0,), (0,))


def _dot(a, b, dims, passes):
    precision = lax.Precision.HIGH if passes == 3 else lax.Precision.DEFAULT
    return lax.dot_general(a, b, (dims, ((), ())), precision=precision, preferred_element_type=F32)


def _mm(a, b, passes=3):
    return _dot(a, b, _NN, passes)


def _mm_nt(a, b, passes=3):
    return _dot(a, b, _NT, passes)


def _mm_tn(a, b, passes=3):
    return _dot(a, b, _TN, passes)


P_SUM = 3
P_SCORE = 1
P_INV = 1
P_STATE = 1
P_APPLY = 1
P_UPDATE = 1
P_POINT = 1


def _stack_rows(blocks):
    return jnp.concatenate(blocks, axis=0)


def _split_rows(x, n):
    r = x.shape[0] // n

    @jax.custom_vjp
    def split(x):
        return tuple(x[i * r:(i + 1) * r] for i in range(n))

    split.defvjp(lambda x: (split(x), None), lambda _, gs: (jnp.concatenate(gs, axis=0),))
    return split(x)


def _iota(shape, d):
    return lax.broadcasted_iota(jnp.int32, shape, d)


def _sigmoid(x):
    return 0.5 * (jnp.tanh(0.5 * x) + 1.0)


def _silu(x):
    return x * _sigmoid(x)


def _softplus(x):
    return jnp.maximum(x, 0.0) + jnp.log(1.0 + jnp.exp(-jnp.abs(x)))


def _seg_ones(seg):
    return ((_iota((LANES, LANES), 0) // seg) == (_iota((LANES, LANES), 1) // seg)).astype(F32)


def _sl(g):
    return slice(g * LANES, (g + 1) * LANES)


@jax.custom_vjp
def _tri_inverse(ms):
    return _tri_inverse_chain(ms)


def _tri_inverse_bwd(ts, dts):
    return ([-_mm_nt(_mm_tn(t, dt, P_INV), t, P_INV) for t, dt in zip(ts, dts)],)


def _tri_inverse_chain(ms):
    c = CHUNK
    ri, ci = _iota((c, c), 0), _iota((c, c), 1)
    eye = (ri == ci).astype(F32)
    d16 = (ri // 16) == (ci // 16)
    d32 = (ri // 32) == (ci // 32)
    ps = [jnp.where(d16, -m, 0.0) for m in ms]
    ts = [eye + p for p in ps]
    for _ in range(3):
        ps = [_mm(p, p, P_INV) for p in ps]
        ts = [_mm(t, eye + p, P_INV) for t, p in zip(ts, ps)]
    for off_diagonal in (d32 & (~d16), ~d32):
        tq = [_mm(t, jnp.where(off_diagonal, m, 0.0), P_INV) for t, m in zip(ts, ms)]
        ts = [t - _mm(a, t, P_INV) for t, a in zip(ts, tq)]
    return ts


_tri_inverse.defvjp(lambda ms: (lambda ts: (ts, ts))(_tri_inverse_chain(ms)), _tri_inverse_bwd)


@jax.custom_vjp
def _known_inverse(ms, ts):
    return ts


_known_inverse.defvjp(lambda ms, ts: (ts, ts),
                      lambda ts, dts: (_tri_inverse_bwd(ts, dts)[0], [jnp.zeros_like(t) for t in ts]))


def _chunk_fwd(prims, *, nsub, scalar_decay, inverses=None):
    c = CHUNK
    ng = len(prims)
    s0s, rs, lws, ks, vs, kks, bs = [list(t) for t in zip(*prims)]
    ri, ci = _iota((c, c), 0), _iota((c, c), 1)
    incl = ri >= ci
    strict = ri > ci
    tril = incl.astype(F32)
    hs = LANES // nsub
    lane = _iota((1, LANES), 1)
    masks = [((lane // hs) == s).astype(F32) for s in range(nsub)] if nsub > 1 else [1.0]
    cws = [_mm(tril, lw, P_SUM) for lw in lws]
    cwxs = [cw - lw for cw, lw in zip(cws, lws)]
    ends = [cw[c - 1:c, :] for cw in cws]
    kkds = [kk * jnp.exp(cwx) for kk, cwx in zip(kks, cwxs)]
    rds = [r * jnp.exp(cw) for r, cw in zip(rs, cws)]
    kends = [k * jnp.exp(e - cw) for k, e, cw in zip(ks, ends, cws)]
    bends = [b * jnp.exp(e - cw) for b, e, cw in zip(bs, ends, cws)]
    state_terms = [_split_rows(_mm_nt(_stack_rows([kkd, rd]), s0, P_STATE), 2) for kkd, rd, s0 in zip(kkds, rds, s0s)]
    w0s, y0s = [t[0] for t in state_terms], [t[1] for t in state_terms]
    chains = [(g, s) for g in range(ng) for s in range(nsub)]
    if scalar_decay:
        e0 = (lane == 0).astype(F32) * jnp.ones((c, 1), F32)
        rows = [_mm_nt(e0, cw, P_SUM) for cw in cws]
        dxs = [jnp.where(strict, jnp.exp(jnp.minimum(cwx[:, :c] - row, 0.0)), 0.0) for cwx, row in zip(cwxs, rows)]
        dis = [jnp.where(incl, jnp.exp(jnp.minimum(cw[:, :c] - row, 0.0)), 0.0) for cw, row in zip(cws, rows)]
        lefts = [_stack_rows([kk * m for m in masks] + [r * m for m in masks]) for kk, r in zip(kks, rs)]
        on_b = [_split_rows(_mm_nt(left, b, P_SCORE), 2 * nsub) for left, b in zip(lefts, bs)]
        on_k = [_split_rows(_mm_nt(left, k, P_SCORE), 2 * nsub) for left, k in zip(lefts, ks)]
        m_b = [on_b[g][s] * dxs[g] for g, s in chains]
        m_k = [on_k[g][s] * dxs[g] for g, s in chains]
        n_k = [on_k[g][nsub + s] * dis[g] for g, s in chains]
        n_b = [on_b[g][nsub + s] * dis[g] for g, s in chains]
    else:
        kds = [k * jnp.exp(-cw) for k, cw in zip(ks, cws)]
        bds = [b * jnp.exp(-cw) for b, cw in zip(bs, cws)]
        lefts = [_stack_rows([kkd * m for m in masks] + [rd * m for m in masks]) for kkd, rd in zip(kkds, rds)]
        on_b = [_split_rows(_mm_nt(left, bd, P_SCORE), 2 * nsub) for left, bd in zip(lefts, bds)]
        on_k = [_split_rows(_mm_nt(left, kd, P_SCORE), 2 * nsub) for left, kd in zip(lefts, kds)]
        m_b = [jnp.where(strict, on_b[g][s], 0.0) for g, s in chains]
        m_k = [jnp.where(strict, on_k[g][s], 0.0) for g, s in chains]
        n_k = [jnp.where(incl, on_k[g][nsub + s], 0.0) for g, s in chains]
        n_b = [jnp.where(incl, on_b[g][nsub + s], 0.0) for g, s in chains]
    t_inv = _tri_inverse(m_b) if inverses is None else _known_inverse(m_b, inverses)
    on_v = [_split_rows(_mm(_stack_rows([mk, nk]), vs[g], P_APPLY), 2) for (g, s), mk, nk in zip(chains, m_k, n_k)]
    sa_c = [_mm(t, w0s[g] + mv[0], P_APPLY) for (g, s), t, mv in zip(chains, t_inv, on_v)]
    y_c = [y0s[g] + mv[1] - _mm(nb, sa, P_APPLY) for (g, s), mv, nb, sa in zip(chains, on_v, n_b, sa_c)]
    per_group = lambda xs: [functools.reduce(lambda p, q: p + q, [xs[g * nsub + s] * masks[s] for s in range(nsub)])
                            for g in range(ng)]
    sas, ys = per_group(sa_c), per_group(y_c)
    s_ends = [s0 * jnp.exp(e) + _mm_tn(_stack_rows([v, -sa]), _stack_rows([kend, bend]), P_UPDATE)
              for s0, e, v, kend, sa, bend in zip(s0s, ends, vs, kends, sas, bends)]
    if nsub > 1:
        same_head = (_iota((LANES, LANES), 0) // hs) == (_iota((LANES, LANES), 1) // hs)
        s_ends = [jnp.where(same_head, s_end, 0.0) for s_end in s_ends]
    return list(zip(ys, s_ends)), t_inv


def _rec_fwd(name, r, lw, k, v, kk, b, *, seq, nsub, scalar_decay):
    n, w = r.shape
    ng = w // LANES
    nc = seq // CHUNK
    nb = n // seq
    nt = nb * ng * nsub

    def body(r_ref, lw_ref, k_ref, v_ref, kk_ref, b_ref, y_ref, s_ref, t_ref, state):
        @pl.when(pl.program_id(0) == 0)
        def _():
            state[...] = jnp.zeros_like(state)
        prims = [(state[bi * ng + g], r_ref[bi, :, _sl(g)], lw_ref[bi, :, _sl(g)], k_ref[bi, :, _sl(g)],
                  v_ref[bi, :, _sl(g)], kk_ref[bi, :, _sl(g)], b_ref[bi, :, _sl(g)])
                 for bi in range(nb) for g in range(ng)]
        outs, t_inv = _chunk_fwd(prims, nsub=nsub, scalar_decay=scalar_decay)
        for i, (y, s_end) in enumerate(outs):
            s_ref[0, i] = prims[i][0]
            y_ref[i // ng, :, _sl(i % ng)] = y
            state[i] = s_end
        for i, t in enumerate(t_inv):
            t_ref[0, i] = t

    row = pl.BlockSpec((nb, CHUNK, w), lambda c: (0, c, 0))
    seqs = lambda a: a.reshape(nb, seq, w)
    y, s_save, t_save = pl.pallas_call(
        body, name=name, grid=(nc,),
        in_specs=[row] * 6,
        out_specs=[row, pl.BlockSpec((1, nb * ng, LANES, LANES), lambda c: (c, 0, 0, 0)),
                   pl.BlockSpec((1, nt, CHUNK, CHUNK), lambda c: (c, 0, 0, 0))],
        out_shape=[jax.ShapeDtypeStruct((nb, seq, w), F32), jax.ShapeDtypeStruct((nc, nb * ng, LANES, LANES), F32),
                   jax.ShapeDtypeStruct((nc, nt, CHUNK, CHUNK), F32)],
        scratch_shapes=[pltpu.VMEM((nb * ng, LANES, LANES), F32)],
        compiler_params=pltpu.CompilerParams(dimension_semantics=("arbitrary",), vmem_limit_bytes=VMEM_LIMIT),
    )(seqs(r), seqs(lw), seqs(k), seqs(v), seqs(kk), seqs(b))
    return y.reshape(n, w), (s_save, t_save)


def _rec_bwd(name, r, lw, k, v, kk, b, saved, dy, *, seq, nsub, scalar_decay):
    n, w = r.shape
    ng = w // LANES
    nc = seq // CHUNK
    nb = n // seq
    nt = nb * ng * nsub
    s_save, t_save = saved

    def body(r_ref, lw_ref, k_ref, v_ref, kk_ref, b_ref, s_ref, t_ref, dy_ref,
             dr_ref, dlw_ref, dk_ref, dv_ref, dkk_ref, db_ref, dstate):
        @pl.when(pl.program_id(0) == 0)
        def _():
            dstate[...] = jnp.zeros_like(dstate)
        inverses = [t_ref[0, i] for i in range(nt)]
        f = lambda p: _chunk_fwd(p, nsub=nsub, scalar_decay=scalar_decay, inverses=inverses)[0]
        chains = [(bi, g) for bi in range(nb) for g in range(ng)]
        prims = [(s_ref[0, bi * ng + g], r_ref[bi, :, _sl(g)], lw_ref[bi, :, _sl(g)], k_ref[bi, :, _sl(g)],
                  v_ref[bi, :, _sl(g)], kk_ref[bi, :, _sl(g)], b_ref[bi, :, _sl(g)]) for bi, g in chains]
        _, vjp = jax.vjp(f, prims)
        (d_prims,) = vjp([(dy_ref[bi, :, _sl(g)], dstate[bi * ng + g]) for bi, g in chains])
        for (bi, g), (ds0, dr, dlw, dk, dv, dkk, db) in zip(chains, d_prims):
            dstate[bi * ng + g] = ds0
            dr_ref[bi, :, _sl(g)] = dr
            dlw_ref[bi, :, _sl(g)] = dlw
            dk_ref[bi, :, _sl(g)] = dk
            dv_ref[bi, :, _sl(g)] = dv
            dkk_ref[bi, :, _sl(g)] = dkk
            db_ref[bi, :, _sl(g)] = db

    row = pl.BlockSpec((nb, CHUNK, w), lambda c: (0, nc - 1 - c, 0))
    seqs = lambda a: a.reshape(nb, seq, w)
    grads = pl.pallas_call(
        body, name=name, grid=(nc,),
        in_specs=[row] * 6 + [pl.BlockSpec((1, nb * ng, LANES, LANES), lambda c: (nc - 1 - c, 0, 0, 0)),
                              pl.BlockSpec((1, nt, CHUNK, CHUNK), lambda c: (nc - 1 - c, 0, 0, 0)), row],
        out_specs=[row] * 6,
        out_shape=[jax.ShapeDtypeStruct((nb, seq, w), F32)] * 6,
        scratch_shapes=[pltpu.VMEM((nb * ng, LANES, LANES), F32)],
        compiler_params=pltpu.CompilerParams(dimension_semantics=("arbitrary",), vmem_limit_bytes=VMEM_LIMIT),
    )(seqs(r), seqs(lw), seqs(k), seqs(v), seqs(kk), seqs(b), s_save, t_save, seqs(dy))
    return [g.reshape(n, w) for g in grads]


def _shift_down(a, j, halo, is_start):
    tb = a.shape[0]
    rolled = pltpu.roll(a, j, 0)
    hr = jnp.where(is_start, 0.0, pltpu.roll(halo, j, 0))
    first = jnp.where(_iota((SUB, LANES), 0) < j, hr, rolled[0:SUB])
    if tb == SUB:
        return first
    return jnp.concatenate([first, rolled[SUB:]], axis=0)


def _shift_up(d, j, carry, is_end):
    tb = d.shape[0]
    up = pltpu.roll(d, tb - j, 0)
    cr = jnp.where(is_end, 0.0, pltpu.roll(carry, SUB - j, 0))
    last = jnp.where(_iota((SUB, LANES), 0) >= SUB - j, cr, up[tb - SUB:tb])
    if tb == SUB:
        return last
    return jnp.concatenate([up[:tb - SUB], last], axis=0)


def _ngroups(a):
    return a.shape[1] // LANES


def _pw_fwd(name, f, ins, shift, params, out_widths, out_dtypes, *, seq, tb):
    n = ins[0].shape[0]
    nt, tps = n // tb, seq // tb
    ni, npar = len(ins), len(params)

    def body(*refs):
        in_refs = refs[:ni]
        pos = ni
        halo_ref = None
        if shift:
            halo_ref = refs[pos]
            pos += 1
        p_refs = refs[pos:pos + npar]
        out_refs = refs[pos + npar:]
        is_start = (pl.program_id(0) % tps) == 0
        tiles = [[ref[:, _sl(g)] for g in range(_ngroups(ref))] for ref in in_refs]
        prevs = [[_shift_down(tiles[0][g], j, halo_ref[:, _sl(g)], is_start) for g in range(len(tiles[0]))]
                 for j in range(1, shift + 1)]
        pv = [[ref[:, _sl(g)] for g in range(_ngroups(ref))] for ref in p_refs]
        outs = f(tiles, prevs, pv)
        for o_ref, og in zip(out_refs, outs, strict=True):
            for g, t in enumerate(og):
                o_ref[:, _sl(g)] = t.astype(o_ref.dtype)

    in_specs = [pl.BlockSpec((tb, a.shape[1]), lambda i: (i, 0)) for a in ins]
    args = list(ins)
    if shift:
        in_specs.append(pl.BlockSpec((SUB, ins[0].shape[1]), lambda i: (jnp.maximum(i * (tb // SUB) - 1, 0), 0)))
        args.append(ins[0])
    in_specs += [pl.BlockSpec(p.shape, lambda i: (0, 0)) for p in params]
    args += list(params)
    return pl.pallas_call(
        body, name=name, grid=(nt,),
        in_specs=in_specs,
        out_specs=[pl.BlockSpec((tb, w), lambda i: (i, 0)) for w in out_widths],
        out_shape=[jax.ShapeDtypeStruct((n, w), dt) for w, dt in zip(out_widths, out_dtypes, strict=True)],
        compiler_params=pltpu.CompilerParams(dimension_semantics=("parallel",), vmem_limit_bytes=VMEM_LIMIT),
    )(*args)


def _pw_bwd(name, f, ins, shift, params, douts, din_dtypes, *, seq, tb):
    n = ins[0].shape[0]
    nt, tps = n // tb, seq // tb
    ni, npar = len(ins), len(params)
    flat_douts = [d for ds in douts for d in ds]
    nd = len(flat_douts)
    w0 = ins[0].shape[1]

    def body(*refs):
        in_refs = refs[:ni]
        pos = ni
        halo_ref = None
        if shift:
            halo_ref = refs[pos]
            pos += 1
        p_refs = refs[pos:pos + npar]
        pos += npar
        d_refs = refs[pos:pos + nd]
        pos += nd
        din_refs = refs[pos:pos + ni]
        pos += ni
        dp_refs = refs[pos:pos + npar]
        pos += npar
        carry = refs[pos] if shift else None
        step = pl.program_id(0)
        tile = nt - 1 - step
        is_start = (tile % tps) == 0
        is_end = (tile % tps) == tps - 1
        tiles = [[ref[:, _sl(g)] for g in range(_ngroups(ref))] for ref in in_refs]
        prevs = [[_shift_down(tiles[0][g], j, halo_ref[:, _sl(g)], is_start) for g in range(len(tiles[0]))]
                 for j in range(1, shift + 1)]
        pv = [[ref[:, _sl(g)] for g in range(_ngroups(ref))] for ref in p_refs]
        cot, pos_d = [], 0
        for ds in douts:
            grp = d_refs[pos_d:pos_d + len(ds)]
            pos_d += len(ds)
            cot.append([functools.reduce(lambda p, q: p + q, [ref[:, _sl(g)].astype(F32) for ref in grp])
                        for g in range(_ngroups(grp[0]))])
        _, vjp = jax.vjp(f, tiles, prevs, pv)
        d_tiles, d_prevs, d_pv = vjp(cot)
        for g in range(len(tiles[0])):
            for j in range(1, shift + 1):
                d_tiles[0][g] = d_tiles[0][g] + _shift_up(d_prevs[j - 1][g], j, carry[j - 1, :, _sl(g)], is_end)
            for j in range(1, shift + 1):
                carry[j - 1, :, _sl(g)] = d_prevs[j - 1][g][0:SUB]
        for ref, dg in zip(din_refs, d_tiles, strict=True):
            for g, t in enumerate(dg):
                ref[:, _sl(g)] = t.astype(ref.dtype)

        @pl.when(step == 0)
        def _():
            for ref in dp_refs:
                ref[...] = jnp.zeros_like(ref)
        for ref, dg in zip(dp_refs, d_pv, strict=True):
            for g, t in enumerate(dg):
                ref[:, _sl(g)] += t

    rev = lambda i: (nt - 1 - i, 0)
    in_specs = [pl.BlockSpec((tb, a.shape[1]), rev) for a in ins]
    args = list(ins)
    if shift:
        in_specs.append(pl.BlockSpec((SUB, w0), lambda i: (jnp.maximum((nt - 1 - i) * (tb // SUB) - 1, 0), 0)))
        args.append(ins[0])
    in_specs += [pl.BlockSpec(p.shape, lambda i: (0, 0)) for p in params]
    args += list(params)
    in_specs += [pl.BlockSpec((tb, d.shape[1]), rev) for d in flat_douts]
    args += flat_douts
    out_specs = [pl.BlockSpec((tb, a.shape[1]), rev) for a in ins] + [pl.BlockSpec(p.shape, lambda i: (0, 0)) for p in params]
    out_shape = ([jax.ShapeDtypeStruct(a.shape, dt) for a, dt in zip(ins, din_dtypes, strict=True)]
                 + [jax.ShapeDtypeStruct(p.shape, F32) for p in params])
    res = pl.pallas_call(
        body, name=name, grid=(nt,),
        in_specs=in_specs, out_specs=out_specs, out_shape=out_shape,
        scratch_shapes=[pltpu.VMEM((shift, SUB, w0), F32)] if shift else [],
        compiler_params=pltpu.CompilerParams(dimension_semantics=("arbitrary",), vmem_limit_bytes=VMEM_LIMIT),
    )(*args)
    return res[:ni], res[ni:]


def _rwkv_prep_f(tiles, prevs, params):
    (p,), (prev,) = tiles, prevs
    mu, w0, w2p, a0, a2p, k_k, k_a = params
    xs = [p[g] + (prev[g] - p[g]) * mu[g] for g in range(13)]
    wdad = xs[12]
    tw = jnp.tanh(wdad)
    e64 = _seg_ones(64)
    r, lw, k2, v, kk, b = [], [], [], [], [], []
    for g in range(4):
        k_g = xs[4 + g]
        lo = w0[g] + _mm(tw, w2p[g], P_POINT)
        lw_g = -jnp.exp(-_softplus(-lo) - 0.5)
        a_g = _sigmoid(a0[g] + _mm(wdad, a2p[g], P_POINT))
        kkp = k_g * k_k[g]
        kk_g = kkp * lax.rsqrt(_mm(kkp * kkp, e64, P_POINT) + 1e-12)
        r.append(xs[g])
        lw.append(lw_g)
        k2.append(k_g * (1.0 + (a_g - 1.0) * k_a[g]))
        v.append(xs[8 + g])
        kk.append(kk_g)
        b.append(kk_g * a_g)
    return [r, lw, k2, v, kk, b]


def _rwkv_post_f(tiles, prevs, params):
    yrec, r, k2, v, z = tiles
    gn_w, gn_b, r_k = params
    e64 = _seg_ones(64)
    out = []
    for g in range(4):
        mean = _mm(yrec[g], e64, P_POINT) * (1.0 / 64)
        d = yrec[g] - mean
        var = _mm(d * d, e64, P_POINT) * (1.0 / 64)
        yn = d * lax.rsqrt(var + RW_GN_EPS) * gn_w[g] + gn_b[g]
        bonus = _mm(r[g] * k2[g] * r_k[g], e64, P_POINT) * v[g]
        out.append((yn + bonus) * _silu(z[g]))
    return [out]


def _gdn_prep_f(tiles, prevs, params):
    x, (ba,) = tiles
    p1, p2, p3 = prevs
    cw0, cw1, cw2, cw3, a_log, dt_bias = params
    s = [_silu(cw3[g] * x[g] + cw2[g] * p1[g] + cw1[g] * p2[g] + cw0[g] * p3[g]) for g in range(12)]
    row = _iota((LANES, LANES), 0)
    r, lw, k, vv, b = [], [], [], [], []
    for h in range(4):
        q_h, k_h, v_h = s[h], s[4 + h], s[8 + h]
        qn = q_h * lax.rsqrt(jnp.sum(q_h * q_h, axis=-1, keepdims=True) + 1e-12)
        kn = k_h * lax.rsqrt(jnp.sum(k_h * k_h, axis=-1, keepdims=True) + 1e-12)
        beta = _sigmoid(_mm(ba, (row == h).astype(F32)))
        alpha = _mm(ba, (row == 4 + h).astype(F32))
        g_h = -jnp.exp(a_log[h]) * _softplus(alpha + dt_bias[h])
        r.append(qn * (LANES ** -0.5))
        lw.append(g_h)
        k.append(kn)
        vv.append(beta * v_h)
        b.append(jnp.exp(g_h) * beta * kn)
    return [r, lw, k, vv, b]


def _gdn_post_f(tiles, prevs, params):
    o, z = tiles
    ((onw,),) = params
    out = []
    for h in range(4):
        ms = jnp.mean(o[h] * o[h], axis=-1, keepdims=True)
        out.append(o[h] * lax.rsqrt(ms + NORM_EPS) * onw * _silu(z[h]))
    return [out]


def _norm_in(x2, g_in, *, tm):
    n = x2.shape[0]

    def body(x_ref, g_ref, h_ref):
        x = x_ref[...]
        rs = lax.rsqrt(jnp.mean(x * x, axis=-1, keepdims=True) + NORM_EPS)
        h_ref[...] = (x * rs * g_ref[...]).astype(BF16)

    return pl.pallas_call(
        body, name="norm_in", grid=(n // tm,),
        in_specs=[pl.BlockSpec((tm, D_MODEL), lambda i: (i, 0)), pl.BlockSpec((1, D_MODEL), lambda i: (0, 0))],
        out_specs=pl.BlockSpec((tm, D_MODEL), lambda i: (i, 0)),
        out_shape=jax.ShapeDtypeStruct((n, D_MODEL), BF16),
        compiler_params=pltpu.CompilerParams(dimension_semantics=("parallel",), vmem_limit_bytes=VMEM_LIMIT),
    )(x2, g_in)


def _proj(name, h, wt, *, tm):
    n, ws = h.shape[0], wt.shape[0]

    def body(h_ref, w_ref, o_ref):
        o_ref[...] = lax.dot_general(h_ref[...], w_ref[...], (_NT, ((), ())), preferred_element_type=F32)

    return pl.pallas_call(
        body, name=name, grid=(n // tm,),
        in_specs=[pl.BlockSpec((tm, D_MODEL), lambda i: (i, 0)), pl.BlockSpec((ws, D_MODEL), lambda i: (0, 0))],
        out_specs=pl.BlockSpec((tm, ws), lambda i: (i, 0)),
        out_shape=jax.ShapeDtypeStruct((n, ws), F32),
        compiler_params=pltpu.CompilerParams(dimension_semantics=("parallel",), vmem_limit_bytes=VMEM_LIMIT),
    )(h, wt)


def _proj_dw(name, h, dp, *, tm):
    n, ws = dp.shape

    def body(h_ref, d_ref, o_ref):
        @pl.when(pl.program_id(0) == 0)
        def _():
            o_ref[...] = jnp.zeros_like(o_ref)
        o_ref[...] += lax.dot_general(d_ref[...], h_ref[...], (_TN, ((), ())), preferred_element_type=F32)

    return pl.pallas_call(
        body, name=name, grid=(n // tm,),
        in_specs=[pl.BlockSpec((tm, D_MODEL), lambda i: (i, 0)), pl.BlockSpec((tm, ws), lambda i: (i, 0))],
        out_specs=pl.BlockSpec((ws, D_MODEL), lambda i: (0, 0)),
        out_shape=jax.ShapeDtypeStruct((ws, D_MODEL), F32),
        compiler_params=pltpu.CompilerParams(dimension_semantics=("arbitrary",), vmem_limit_bytes=VMEM_LIMIT),
    )(h, dp)


def _proj_dx(x2, g_in, d_xo, dps, ws, *, tm):
    n = x2.shape[0]
    ns = len(dps)

    def body(*refs):
        x_ref, g_ref, dxo_ref = refs[:3]
        dp_refs = refs[3:3 + ns]
        w_refs = refs[3 + ns:3 + 2 * ns]
        dx_ref, dg_ref = refs[3 + 2 * ns:]
        dh = jnp.zeros((tm, D_MODEL), F32)
        for d_ref, w_ref in zip(dp_refs, w_refs, strict=True):
            dh = dh + jnp.dot(d_ref[...], w_ref[...], preferred_element_type=F32)
        x = x_ref[...]
        rs = lax.rsqrt(jnp.mean(x * x, axis=-1, keepdims=True) + NORM_EPS)
        xn = x * rs
        dxn = dh * g_ref[...]
        dx_ref[...] = dxo_ref[...] + rs * (dxn - xn * jnp.mean(dxn * xn, axis=-1, keepdims=True))

        @pl.when(pl.program_id(0) == 0)
        def _():
            dg_ref[...] = jnp.zeros_like(dg_ref)
        dg_ref[...] += jnp.sum(dh * xn, axis=0, keepdims=True)

    row = pl.BlockSpec((tm, D_MODEL), lambda i: (i, 0))
    return pl.pallas_call(
        body, name="proj_dx", grid=(n // tm,),
        in_specs=([row, pl.BlockSpec((1, D_MODEL), lambda i: (0, 0)), row]
                  + [pl.BlockSpec((tm, d.shape[1]), lambda i: (i, 0)) for d in dps]
                  + [pl.BlockSpec(w.shape, lambda i: (0, 0)) for w in ws]),
        out_specs=[row, pl.BlockSpec((1, D_MODEL), lambda i: (0, 0))],
        out_shape=[jax.ShapeDtypeStruct((n, D_MODEL), F32), jax.ShapeDtypeStruct((1, D_MODEL), F32)],
        compiler_params=pltpu.CompilerParams(dimension_semantics=("arbitrary",), vmem_limit_bytes=VMEM_LIMIT),
    )(x2, g_in, d_xo, *dps, *ws)


def _tail(x2, tgt2, gates, ya, yb, w_a, w_b, w_o, now, *, tr):
    n = x2.shape[0]

    def body(x_ref, t_ref, g_ref, ya_ref, yb_ref, wa_ref, wb_ref, wo_ref, now_ref,
             dya_ref, dyb_ref, dg_ref, dxo_ref, dwa_ref, dwb_ref, dwo_ref, dnow_ref, loss_ref):
        ya16, yb16 = ya_ref[...].astype(BF16), yb_ref[...].astype(BF16)
        ua = jnp.dot(ya16, wa_ref[...], preferred_element_type=F32)
        ub = jnp.dot(yb16, wb_ref[...], preferred_element_type=F32)
        ga = _sigmoid(g_ref[:, :D_MODEL])
        gb = _sigmoid(g_ref[:, D_MODEL:])
        m16 = (ga * ua + gb * ub).astype(BF16)
        xo = x_ref[...] + jnp.dot(m16, wo_ref[...], preferred_element_type=F32)
        rs = lax.rsqrt(jnp.mean(xo * xo, axis=-1, keepdims=True) + NORM_EPS)
        yn = xo * rs
        now_v = now_ref[...]
        err = yn * now_v - t_ref[...]
        dy = err * (1.0 / D_MODEL)
        dyn = dy * now_v
        dxo = rs * (dyn - yn * jnp.mean(dyn * yn, axis=-1, keepdims=True))
        dxo_ref[...] = dxo
        dxo16 = dxo.astype(BF16)
        dm = lax.dot_general(dxo16, wo_ref[...], (((1,), (1,)), ((), ())), preferred_element_type=F32)
        dua16 = (dm * ga).astype(BF16)
        dub16 = (dm * gb).astype(BF16)
        dg_ref[:, :D_MODEL] = (dm * ua * ga * (1.0 - ga)).astype(dg_ref.dtype)
        dg_ref[:, D_MODEL:] = (dm * ub * gb * (1.0 - gb)).astype(dg_ref.dtype)
        dya_ref[...] = lax.dot_general(dua16, wa_ref[...], (((1,), (1,)), ((), ())), preferred_element_type=F32)
        dyb_ref[...] = lax.dot_general(dub16, wb_ref[...], (((1,), (1,)), ((), ())), preferred_element_type=F32)

        @pl.when(pl.program_id(0) == 0)
        def _():
            for ref in (dwa_ref, dwb_ref, dwo_ref, dnow_ref, loss_ref):
                ref[...] = jnp.zeros_like(ref)
        tn = (((0,), (0,)), ((), ()))
        dwo_ref[...] += lax.dot_general(m16, dxo16, tn, preferred_element_type=F32)
        dwa_ref[...] += lax.dot_general(ya16, dua16, tn, preferred_element_type=F32)
        dwb_ref[...] += lax.dot_general(yb16, dub16, tn, preferred_element_type=F32)
        dnow_ref[...] += jnp.sum(dy * yn, axis=0, keepdims=True)
        loss_ref[...] += (0.5 / D_MODEL) * jnp.sum(err * err)

    row = lambda w: pl.BlockSpec((tr, w), lambda i: (i, 0))
    full = lambda a: pl.BlockSpec(a.shape, lambda i: (0, 0))
    return pl.pallas_call(
        body, name="tail", grid=(n // tr,),
        in_specs=[row(D_MODEL), row(D_MODEL), row(2 * D_MODEL), row(RW_W), row(GD_W), full(w_a), full(w_b), full(w_o), full(now)],
        out_specs=[row(RW_W), row(GD_W), row(2 * D_MODEL), row(D_MODEL),
                   pl.BlockSpec((RW_W, D_MODEL), lambda i: (0, 0)), pl.BlockSpec((GD_W, D_MODEL), lambda i: (0, 0)),
                   pl.BlockSpec((D_MODEL, D_MODEL), lambda i: (0, 0)), pl.BlockSpec((1, D_MODEL), lambda i: (0, 0)),
                   pl.BlockSpec((SUB, LANES), lambda i: (0, 0))],
        out_shape=[jax.ShapeDtypeStruct((n, RW_W), F32), jax.ShapeDtypeStruct((n, GD_W), F32),
                   jax.ShapeDtypeStruct((n, 2 * D_MODEL), BF16), jax.ShapeDtypeStruct((n, D_MODEL), F32),
                   jax.ShapeDtypeStruct((RW_W, D_MODEL), F32), jax.ShapeDtypeStruct((GD_W, D_MODEL), F32),
                   jax.ShapeDtypeStruct((D_MODEL, D_MODEL), F32), jax.ShapeDtypeStruct((1, D_MODEL), F32),
                   jax.ShapeDtypeStruct((SUB, LANES), F32)],
        compiler_params=pltpu.CompilerParams(dimension_semantics=("arbitrary",), vmem_limit_bytes=VMEM_LIMIT),
    )(x2, tgt2, gates, ya, yb, w_a, w_b, w_o, now)


def _exchange(name, axes, scatter, gather):
    ns, ng = len(scatter), len(gather)
    na = ns + ng
    gs = 2 ** len(axes)
    arrs = list(scatter) + list(gather)

    def body(*refs):
        src = refs[:na]
        dst = refs[na:2 * na]
        send_sems, recv_sems = refs[2 * na:]
        mine = {ax: lax.axis_index(ax) for ax in ("x", "y", "c")}

        def peer(k):
            co = dict(mine)
            for i, ax in enumerate(axes):
                if (k >> (len(axes) - 1 - i)) & 1:
                    co[ax] = 1 - co[ax]
            idx = 0
            for ax in axes:
                idx = 2 * idx + co[ax]
            return (co["x"], co["y"], co["c"]), idx

        _, me = peer(0)

        def copy(a, k, landing):
            dev, idx = peer(k)
            s = src[a].at[idx] if a < ns else src[a]
            return pltpu.make_async_remote_copy(src_ref=s, dst_ref=dst[a].at[idx if landing else me],
                                                send_sem=send_sems.at[a, k - 1], recv_sem=recv_sems.at[a, k - 1],
                                                device_id=dev, device_id_type=pl.DeviceIdType.MESH)

        sends = [copy(a, k, False) for a in range(na) for k in range(1, gs)]
        for cp in sends:
            cp.start()
        for a in range(na):
            for k in range(1, gs):
                copy(a, k, True).wait_recv()
        for cp in sends:
            cp.wait_send()

    out_shape = [jax.ShapeDtypeStruct(a.shape, a.dtype) for a in scatter] + \
                [jax.ShapeDtypeStruct((gs,) + a.shape, a.dtype) for a in gather]
    anyspec = pl.BlockSpec(memory_space=pl.ANY)
    lands = pl.pallas_call(
        body, name=name,
        in_specs=[anyspec] * na, out_specs=[anyspec] * na, out_shape=out_shape,
        scratch_shapes=[pltpu.SemaphoreType.DMA((na, gs - 1)), pltpu.SemaphoreType.DMA((na, gs - 1))],
    )(*arrs)
    me = 0
    for ax in axes:
        me = 2 * me + lax.axis_index(ax)
    kept = [lax.dynamic_index_in_dim(a, me, 0, keepdims=False) for a in scatter] + list(gather)
    return [lax.dynamic_update_index_in_dim(land, mine, me, 0) for land, mine in zip(lands, kept)]


def _sum_slots(name, land, out_dtype):
    ns, nq, r, c = land.shape

    def body(l_ref, o_ref):
        acc = l_ref[0, 0].astype(F32)
        for s in range(1, ns):
            acc = acc + l_ref[s, 0].astype(F32)
        o_ref[0] = acc.astype(o_ref.dtype)

    return pl.pallas_call(
        body, name=name, grid=(nq,),
        in_specs=[pl.BlockSpec((ns, 1, r, c), lambda i: (0, i, 0, 0))],
        out_specs=pl.BlockSpec((1, r, c), lambda i: (i, 0, 0)),
        out_shape=jax.ShapeDtypeStruct((nq, r, c), out_dtype),
        compiler_params=pltpu.CompilerParams(dimension_semantics=("parallel",), vmem_limit_bytes=VMEM_LIMIT),
    )(land)


def _adam(name, land, w, m, v):
    r, c = w.shape
    nslot = land.shape[0]
    tr = 256 if (r % 256 == 0 and r > 256) else r
    tc = 256 if (tr == r and r > 256 and c % 256 == 0) else c

    def body(l_ref, w_ref, m_ref, v_ref, g_out, d_out, m_out, v_out):
        g = l_ref[0].astype(F32)
        for s in range(1, nslot):
            g = g + l_ref[s].astype(F32)
        g_out[...] = g
        d_out[...], m_out[...], v_out[...] = _adam_math(g, w_ref[...], m_ref[...], v_ref[...])

    blk = pl.BlockSpec((tr, tc), lambda i: (i * tc // c, i % (c // tc)))
    return pl.pallas_call(
        body, name=name, grid=((r // tr) * (c // tc),),
        in_specs=[pl.BlockSpec((nslot, tr, tc), lambda i: (0, i * tc // c, i % (c // tc))), blk, blk, blk],
        out_specs=[blk] * 4,
        out_shape=[jax.ShapeDtypeStruct((r, c), F32)] * 4,
        compiler_params=pltpu.CompilerParams(dimension_semantics=("parallel",), vmem_limit_bytes=VMEM_LIMIT),
    )(land, w, m, v)


def _adam_math(g, w, m, v):
    c1 = 1.0 / (1.0 - ADAM_B1 ** ADAM_STEP)
    c2 = 1.0 / (1.0 - ADAM_B2 ** ADAM_STEP)
    m_new = ADAM_B1 * m + (1.0 - ADAM_B1) * g
    v_new = ADAM_B2 * v + (1.0 - ADAM_B2) * (g * g)
    return -ADAM_LR * ((m_new * c1) / (jnp.sqrt(v_new * c2) + ADAM_EPS) + ADAM_WD * w), m_new, v_new


def _adam_small(land, ws, ms, vs):
    npar = len(ws)
    nslot = land.shape[0]

    def body(*refs):
        l_ref = refs[0]
        w_refs, m_refs, v_refs = refs[1:1 + npar], refs[1 + npar:1 + 2 * npar], refs[1 + 2 * npar:1 + 3 * npar]
        outs = refs[1 + 3 * npar:1 + 7 * npar]
        loss_ref, g_rows = refs[1 + 7 * npar], refs[2 + 7 * npar]
        g = l_ref[0]
        for s in range(1, nslot):
            g = g + l_ref[s]
        g_rows[...] = g
        row = 0
        for i, (_, size) in enumerate(_SMALL):
            for j in range(-(-size // LANES)):
                width = min(LANES, size - j * LANES)
                cols = slice(j * LANES, j * LANES + width)
                g_ij = g_rows[row:row + 1, 0:width]
                delta, m_new, v_new = _adam_math(g_ij, w_refs[i][:, cols], m_refs[i][:, cols], v_refs[i][:, cols])
                for ref, val in zip(outs[4 * i:4 * i + 4], (g_ij, delta, m_new, v_new)):
                    ref[:, cols] = val
                row += 1
        loss_ref[...] = g_rows[row:row + 1, :]

    full = lambda a: pl.BlockSpec(a.shape, lambda: (0,) * a.ndim)
    res = pl.pallas_call(
        body, name="adam_small",
        in_specs=[full(land)] + [full(a) for a in list(ws) + list(ms) + list(vs)],
        out_specs=[full(w) for w in ws for _ in range(4)] + [pl.BlockSpec((1, LANES), lambda: (0, 0))],
        out_shape=[jax.ShapeDtypeStruct(w.shape, F32) for w in ws for _ in range(4)] + [jax.ShapeDtypeStruct((1, LANES), F32)],
        scratch_shapes=[pltpu.VMEM(land.shape[1:], F32)],
    )(land, *ws, *ms, *vs)
    return [res[4 * i:4 * i + 4] for i in range(npar)], res[4 * npar]


_SMALL = (("norm_in_w", 1024), ("rw_mu", 1664), ("rw_w0", 512), ("rw_a0", 512), ("rw_k_k", 512), ("rw_k_a", 512),
          ("rw_r_k", 512), ("rw_gn_w", 512), ("rw_gn_b", 512), ("gd_A_log", 4), ("gd_dt_bias", 4), ("gd_o_norm_w", 128),
          ("norm_out_w", 1024))
_SMALL_ROWS = 64


def _pack_small(vals, loss_row):
    rows = []
    for (_, size), a in zip(_SMALL, vals, strict=True):
        flat = a.reshape(-1).astype(F32)
        pad = (-size) % LANES
        if pad:
            flat = jnp.concatenate([flat, jnp.zeros((pad,), F32)])
        rows.append(flat.reshape(-1, LANES))
    rows.append(loss_row)
    used = sum(r.shape[0] for r in rows)
    rows.append(jnp.zeros((_SMALL_ROWS - used, LANES), F32))
    return jnp.concatenate(rows, axis=0)


def kernel(x, norm_in_w, w_in, rw_mu, rw_w0, rw_w2, rw_a0, rw_a2, rw_k_k, rw_k_a, rw_r_k, rw_gn_w, rw_gn_b, gd_conv_w, gd_A_log, gd_dt_bias, gd_o_norm_w, w_branch_a, w_branch_b, w_out, norm_out_w, loss_target, m_norm_in_w, m_w_in, m_rw_mu, m_rw_w0, m_rw_w2, m_rw_a0, m_rw_a2, m_rw_k_k, m_rw_k_a, m_rw_r_k, m_rw_gn_w, m_rw_gn_b, m_gd_conv_w, m_gd_A_log, m_gd_dt_bias, m_gd_o_norm_w, m_w_branch_a, m_w_branch_b, m_w_out, m_norm_out_w, v_norm_in_w, v_w_in, v_rw_mu, v_rw_w0, v_rw_w2, v_rw_a0, v_rw_a2, v_rw_k_k, v_rw_k_a, v_rw_r_k, v_rw_gn_w, v_rw_gn_b, v_gd_conv_w, v_gd_A_log, v_gd_dt_bias, v_gd_o_norm_w, v_w_branch_a, v_w_branch_b, v_w_out, v_norm_out_w):
    nb, seq, _ = x.shape
    n = nb * seq
    tm = min(512, n)
    tb = min(512, seq)
    x2 = x.reshape(n, D_MODEL)
    tgt2 = loss_target.reshape(n, D_MODEL)
    cols = w_in.shape[2]
    in_cols = cols * N_DEV

    wt_own, mt_own, vt_own = w_in[0].T, m_w_in[0].T, v_w_in[0].T
    sharded = [wt_own.astype(BF16), rw_w2[0], rw_a2[0], gd_conv_w[0], w_branch_a[0].astype(BF16),
               w_branch_b[0].astype(BF16), w_out[0].astype(BF16)]
    by_chip = _exchange("gather_chips", ("x", "y"), [], sharded)
    g_win, g_w2, g_a2, g_conv, g_wa, g_wb, g_wo = _exchange("gather_cores", ("c",), [], by_chip)
    unshard_rows = lambda a: jnp.transpose(a, (1, 0, 2, 3)).reshape(N_DEV * a.shape[2], a.shape[3])
    unshard_cols = lambda a: jnp.transpose(a, (2, 1, 0, 3)).reshape(a.shape[2], N_DEV * a.shape[3])
    wt_full = unshard_rows(g_win)
    seg_bounds = ((0, 1664), (1664, 2176), (2176, 3712), (3712, 4224), (4232, in_cols))
    w_rw, w_zrw, w_qkv, w_zgd, w_gates = [wt_full[a:b] for a, b in seg_bounds]
    w_ba = jnp.concatenate([wt_full[4224:4232], jnp.zeros((LANES - 8, D_MODEL), BF16)], axis=0)
    w2_full, a2_full = unshard_cols(g_w2), unshard_cols(g_a2)
    zeros64 = jnp.zeros((64, RW_W), F32)
    w2p = jnp.concatenate([w2_full, zeros64], axis=0)
    a2p = jnp.concatenate([zeros64, a2_full], axis=0)
    conv_full = unshard_cols(g_conv)
    conv_rows = [conv_full[i:i + 1] for i in range(4)]
    wa_full = unshard_cols(g_wa)
    wb_full = unshard_cols(g_wb)
    wo_full = unshard_rows(g_wo)
    a_log_bc = jnp.repeat(gd_A_log, LANES, axis=1)
    dt_bias_bc = jnp.repeat(gd_dt_bias, LANES, axis=1)
    r_k_flat = rw_r_k.reshape(1, RW_W)
    now2 = norm_out_w.reshape(1, D_MODEL)

    h = _norm_in(x2, norm_in_w, tm=tm)
    p_rw = _proj("proj_rw", h, w_rw, tm=tm)
    p_zrw = _proj("proj_zrw", h, w_zrw, tm=tm)
    p_qkv = _proj("proj_qkv", h, w_qkv, tm=tm)
    p_zgd = _proj("proj_zgd", h, w_zgd, tm=tm)
    p_ba = _proj("proj_ba", h, w_ba, tm=tm)
    p_gates = _proj("proj_gates", h, w_gates, tm=tm)

    rw_params = [rw_mu, rw_w0, w2p, rw_a0, a2p, rw_k_k, rw_k_a]
    r_a, lw_a, k_a, v_a, kk_a, b_a = _pw_fwd("rwkv_prep", _rwkv_prep_f, [p_rw], 1, rw_params, [RW_W] * 6, [F32] * 6,
                                             seq=seq, tb=tb)
    y_rec, s_a = _rec_fwd("rwkv_rec", r_a, lw_a, k_a, v_a, kk_a, b_a, seq=seq, nsub=2, scalar_decay=False)
    post_params = [rw_gn_w, rw_gn_b, r_k_flat]
    (y_a,) = _pw_fwd("rwkv_post", _rwkv_post_f, [y_rec, r_a, k_a, v_a, p_zrw], 0, post_params, [RW_W], [F32], seq=seq, tb=tb)

    gd_params = conv_rows + [a_log_bc, dt_bias_bc]
    r_b, lw_b, k_b, v_b, b_b = _pw_fwd("gdn_prep", _gdn_prep_f, [p_qkv, p_ba], 3, gd_params, [GD_W] * 5, [F32] * 5,
                                       seq=seq, tb=tb)
    o_rec, s_b = _rec_fwd("gdn_rec", r_b, lw_b, k_b, v_b, k_b, b_b, seq=seq, nsub=1, scalar_decay=True)
    (y_b,) = _pw_fwd("gdn_post", _gdn_post_f, [o_rec, p_zgd], 0, [gd_o_norm_w], [GD_W], [F32], seq=seq, tb=tb)

    d_ya, d_yb, d_gates, d_xo, dwa, dwb, dwo, d_now, loss_acc = _tail(
        x2, tgt2, p_gates, y_a, y_b, wa_full, wb_full, wo_full, now2, tr=min(256, n))

    (d_o, d_zgd), (d_onw,) = _pw_bwd("gdn_post_bwd", _gdn_post_f, [o_rec, p_zgd], 0, [gd_o_norm_w], [[d_yb]],
                                     [F32, BF16], seq=seq, tb=tb)
    dr_b, dlw_b, dk_b, dv_b, dkk_b, db_b = _rec_bwd("gdn_rec_bwd", r_b, lw_b, k_b, v_b, k_b, b_b, s_b, d_o,
                                                    seq=seq, nsub=1, scalar_decay=True)
    (d_qkv, d_ba), d_gd_params = _pw_bwd("gdn_prep_bwd", _gdn_prep_f, [p_qkv, p_ba], 3, gd_params,
                                         [[dr_b], [dlw_b], [dk_b, dkk_b], [dv_b], [db_b]], [BF16, BF16], seq=seq, tb=tb)

    (d_yrec, dr_p, dk_p, dv_p, d_zrw), d_post_params = _pw_bwd(
        "rwkv_post_bwd", _rwkv_post_f, [y_rec, r_a, k_a, v_a, p_zrw], 0, post_params, [[d_ya]],
        [F32, F32, F32, F32, BF16], seq=seq, tb=tb)
    dr_a, dlw_a, dk_a, dv_a, dkk_a, db_a = _rec_bwd("rwkv_rec_bwd", r_a, lw_a, k_a, v_a, kk_a, b_a, s_a, d_yrec,
                                                    seq=seq, nsub=2, scalar_decay=False)
    (d_prw,), d_rw_params = _pw_bwd("rwkv_prep_bwd", _rwkv_prep_f, [p_rw], 1, rw_params,
                                    [[dr_a, dr_p], [dlw_a], [dk_a, dk_p], [dv_a, dv_p], [dkk_a], [db_a]], [BF16],
                                    seq=seq, tb=tb)

    dps = [d_prw, d_zrw, d_qkv, d_zgd, d_ba, d_gates]
    wsegs = [w_rw, w_zrw, w_qkv, w_zgd, w_ba, w_gates]
    dx2, d_gin = _proj_dx(x2, norm_in_w, d_xo, dps, wsegs, tm=min(256, n))
    dw_rw = _proj_dw("dw_rw", h, d_prw, tm=tm)
    dw_zrw = _proj_dw("dw_zrw", h, d_zrw, tm=tm)
    dw_qkv = _proj_dw("dw_qkv", h, d_qkv, tm=tm)
    dw_zgd = _proj_dw("dw_zgd", h, d_zgd, tm=tm)
    dw_ba = _proj_dw("dw_ba", h, d_ba, tm=tm)
    dw_gates = _proj_dw("dw_gates", h, d_gates, tm=tm)
    dwt_in_full = jnp.concatenate([dw_rw, dw_zrw, dw_qkv, dw_zgd, dw_ba[:8], dw_gates], axis=0)

    shard_cols = lambda a: jnp.transpose(a.reshape(a.shape[0], 4, 2, a.shape[1] // N_DEV), (2, 1, 0, 3))
    shard_rows = lambda a: jnp.transpose(a.reshape(4, 2, a.shape[0] // N_DEV, a.shape[1]), (1, 0, 2, 3))
    d_mu, d_w0, d_w2p, d_a0, d_a2p, d_kk_, d_ka_ = d_rw_params
    d_gnw, d_gnb, d_rk = d_post_params
    d_conv = jnp.concatenate(d_gd_params[:4], axis=0)
    d_alog = d_gd_params[4].reshape(4, LANES).sum(axis=1).reshape(1, 4)
    d_dtb = d_gd_params[5].reshape(4, LANES).sum(axis=1).reshape(1, 4)
    scat = [shard_rows(dwt_in_full), shard_cols(d_w2p[:64]), shard_cols(d_a2p[64:]), shard_cols(d_conv),
            shard_cols(dwa), shard_cols(dwb), shard_rows(dwo)]
    small_g = _pack_small([d_gin, d_mu, d_w0, d_a0, d_kk_, d_ka_, d_rk, d_gnw, d_gnb, d_alog, d_dtb, d_onw, d_now],
                          loss_acc[0:1])
    pair = _exchange("reduce_cores", ("c",), scat, [small_g])
    part = [_sum_slots("pair_sum_%d" % i, a, BF16) for i, a in enumerate(pair[:7])]
    part_small = _sum_slots("pair_sum_small", pair[7][:, None], F32)[0]
    lands = _exchange("reduce_chips", ("x", "y"), part, [part_small])

    small_w = [norm_in_w, rw_mu, rw_w0, rw_a0, rw_k_k, rw_k_a, rw_r_k, rw_gn_w, rw_gn_b, gd_A_log, gd_dt_bias, gd_o_norm_w, norm_out_w]
    small_m = [m_norm_in_w, m_rw_mu, m_rw_w0, m_rw_a0, m_rw_k_k, m_rw_k_a, m_rw_r_k, m_rw_gn_w, m_rw_gn_b, m_gd_A_log, m_gd_dt_bias, m_gd_o_norm_w, m_norm_out_w]
    small_v = [v_norm_in_w, v_rw_mu, v_rw_w0, v_rw_a0, v_rw_k_k, v_rw_k_a, v_rw_r_k, v_rw_gn_w, v_rw_gn_b, v_gd_A_log, v_gd_dt_bias, v_gd_o_norm_w, v_norm_out_w]
    flat = lambda arrs: [a.reshape(1, -1) for a in arrs]
    sm, loss_row = _adam_small(lands[7], flat(small_w), flat(small_m), flat(small_v))
    sm_g, sm_d, sm_m, sm_v = [{nm: res[i].reshape(w.shape) for (nm, _), res, w in zip(_SMALL, sm, small_w)}
                              for i in range(4)]

    big = {"w_in": [o.T[None] for o in _adam("adam_w_in", lands[0], wt_own, mt_own, vt_own)]}
    for nm, land, w, m, v in (("rw_w2", lands[1], rw_w2, m_rw_w2, v_rw_w2),
                              ("rw_a2", lands[2], rw_a2, m_rw_a2, v_rw_a2),
                              ("gd_conv_w", lands[3], gd_conv_w, m_gd_conv_w, v_gd_conv_w),
                              ("w_branch_a", lands[4], w_branch_a, m_w_branch_a, v_w_branch_a),
                              ("w_branch_b", lands[5], w_branch_b, m_w_branch_b, v_w_branch_b),
                              ("w_out", lands[6], w_out, m_w_out, v_w_out)):
        big[nm] = [o.reshape(w.shape) for o in _adam("adam_" + nm, land, w[0], m[0], v[0])]

    order = ["norm_in_w", "w_in", "rw_mu", "rw_w0", "rw_w2", "rw_a0", "rw_a2", "rw_k_k", "rw_k_a", "rw_r_k", "rw_gn_w",
             "rw_gn_b", "gd_conv_w", "gd_A_log", "gd_dt_bias", "gd_o_norm_w", "w_branch_a", "w_branch_b", "w_out", "norm_out_w"]
    pick = lambda nm, i: big[nm][i] if nm in big else (sm_g, sm_d, sm_m, sm_v)[i][nm]
    loss = loss_row[0, 0]
    grad_x = dx2.reshape(x.shape)
    return (loss, grad_x, *[pick(nm, 0) for nm in order], *[pick(nm, 1) for nm in order],
            *[pick(nm, 2) for nm in order], *[pick(nm, 3) for nm in order])
```

```python
import functools

import jax
import jax.numpy as jnp
from jax import lax
from jax.experimental import pallas as pl
from jax.experimental.pallas import tpu as pltpu

F32 = jnp.float32
BF16 = jnp.bfloat16
HI = lax.Precision.HIGHEST

LANES = 128
SUB = 8
CHUNK = 64
N_DEV = 8
VMEM_LIMIT = 56 * 1024 * 1024

D_MODEL = 1024
RW_W = 512
GD_W = 512
RW_SHIFT = 1664
NORM_EPS = 1e-6
RW_GN_EPS = 64 * 1e-5
ADAM_LR, ADAM_B1, ADAM_B2, ADAM_EPS, ADAM_WD, ADAM_STEP = 0.001, 0.9, 0.999, 1e-8, 0.01, 10


_NN, _NT, _TN = ((1,), (0,)), ((1,), (1,)), ((0,), (0,))


def _dot(a, b, dims, passes):
    precision = lax.Precision.HIGH if passes == 3 else lax.Precision.DEFAULT
    return lax.dot_general(a, b, (dims, ((), ())), precision=precision, preferred_element_type=F32)


def _mm(a, b, passes=3):
    return _dot(a, b, _NN, passes)


def _mm_nt(a, b, passes=3):
    return _dot(a, b, _NT, passes)


def _mm_tn(a, b, passes=3):
    return _dot(a, b, _TN, passes)


P_SUM = 3
P_SCORE = 1
P_INV = 1
P_STATE = 1
P_APPLY = 1
P_UPDATE = 1
P_POINT = 1


def _stack_rows(blocks):
    return jnp.concatenate(blocks, axis=0)


def _split_rows(x, n):
    r = x.shape[0] // n

    @jax.custom_vjp
    def split(x):
        return tuple(x[i * r:(i + 1) * r] for i in range(n))

    split.defvjp(lambda x: (split(x), None), lambda _, gs: (jnp.concatenate(gs, axis=0),))
    return split(x)


def _iota(shape, d):
    return lax.broadcasted_iota(jnp.int32, shape, d)


def _sigmoid(x):
    return 0.5 * (jnp.tanh(0.5 * x) + 1.0)


def _silu(x):
    return x * _sigmoid(x)


def _softplus(x):
    return jnp.maximum(x, 0.0) + jnp.log(1.0 + jnp.exp(-jnp.abs(x)))


def _seg_ones(seg):
    return ((_iota((LANES, LANES), 0) // seg) == (_iota((LANES, LANES), 1) // seg)).astype(F32)


def _sl(g):
    return slice(g * LANES, (g + 1) * LANES)


@jax.custom_vjp
def _tri_inverse(ms):
    return _tri_inverse_chain(ms)


def _tri_inverse_bwd(ts, dts):
    return ([-_mm_nt(_mm_tn(t, dt, P_INV), t, P_INV) for t, dt in zip(ts, dts)],)


def _tri_inverse_chain(ms):
    c = CHUNK
    ri, ci = _iota((c, c), 0), _iota((c, c), 1)
    eye = (ri == ci).astype(F32)
    d16 = (ri // 16) == (ci // 16)
    d32 = (ri // 32) == (ci // 32)
    ps = [jnp.where(d16, -m, 0.0) for m in ms]
    ts = [eye + p for p in ps]
    for _ in range(3):
        ps = [_mm(p, p, P_INV) for p in ps]
        ts = [_mm(t, eye + p, P_INV) for t, p in zip(ts, ps)]
    for off_diagonal in (d32 & (~d16), ~d32):
        tq = [_mm(t, jnp.where(off_diagonal, m, 0.0), P_INV) for t, m in zip(ts, ms)]
        ts = [t - _mm(a, t, P_INV) for t, a in zip(ts, tq)]
    return ts


_tri_inverse.defvjp(lambda ms: (lambda ts: (ts, ts))(_tri_inverse_chain(ms)), _tri_inverse_bwd)


@jax.custom_vjp
def _known_inverse(ms, ts):
    return ts


_known_inverse.defvjp(lambda ms, ts: (ts, ts),
                      lambda ts, dts: (_tri_inverse_bwd(ts, dts)[0], [jnp.zeros_like(t) for t in ts]))


def _chunk_fwd(prims, *, nsub, scalar_decay, inverses=None):
    c = CHUNK
    ng = len(prims)
    s0s, rs, lws, ks, vs, kks, bs = [list(t) for t in zip(*prims)]
    ri, ci = _iota((c, c), 0), _iota((c, c), 1)
    incl = ri >= ci
    strict = ri > ci
    tril = incl.astype(F32)
    hs = LANES // nsub
    lane = _iota((1, LANES), 1)
    masks = [((lane // hs) == s).astype(F32) for s in range(nsub)] if nsub > 1 else [1.0]
    cws = [_mm(tril, lw, P_SUM) for lw in lws]
    cwxs = [cw - lw for cw, lw in zip(cws, lws)]
    ends = [cw[c - 1:c, :] for cw in cws]
    kkds = [kk * jnp.exp(cwx) for kk, cwx in zip(kks, cwxs)]
    rds = [r * jnp.exp(cw) for r, cw in zip(rs, cws)]
    kends = [k * jnp.exp(e - cw) for k, e, cw in zip(ks, ends, cws)]
    bends = [b * jnp.exp(e - cw) for b, e, cw in zip(bs, ends, cws)]
    state_terms = [_split_rows(_mm_nt(_stack_rows([kkd, rd]), s0, P_STATE), 2) for kkd, rd, s0 in zip(kkds, rds, s0s)]
    w0s, y0s = [t[0] for t in state_terms], [t[1] for t in state_terms]
    chains = [(g, s) for g in range(ng) for s in range(nsub)]
    if scalar_decay:
        e0 = (lane == 0).astype(F32) * jnp.ones((c, 1), F32)
        rows = [_mm_nt(e0, cw, P_SUM) for cw in cws]
        dxs = [jnp.where(strict, jnp.exp(jnp.minimum(cwx[:, :c] - row, 0.0)), 0.0) for cwx, row in zip(cwxs, rows)]
        dis = [jnp.where(incl, jnp.exp(jnp.minimum(cw[:, :c] - row, 0.0)), 0.0) for cw, row in zip(cws, rows)]
        lefts = [_stack_rows([kk * m for m in masks] + [r * m for m in masks]) for kk, r in zip(kks, rs)]
        on_b = [_split_rows(_mm_nt(left, b, P_SCORE), 2 * nsub) for left, b in zip(lefts, bs)]
        on_k = [_split_rows(_mm_nt(left, k, P_SCORE), 2 * nsub) for left, k in zip(lefts, ks)]
        m_b = [on_b[g][s] * dxs[g] for g, s in chains]
        m_k = [on_k[g][s] * dxs[g] for g, s in chains]
        n_k = [on_k[g][nsub + s] * dis[g] for g, s in chains]
        n_b = [on_b[g][nsub + s] * dis[g] for g, s in chains]
    else:
        kds = [k * jnp.exp(-cw) for k, cw in zip(ks, cws)]
        bds = [b * jnp.exp(-cw) for b, cw in zip(bs, cws)]
        lefts = [_stack_rows([kkd * m for m in masks] + [rd * m for m in masks]) for kkd, rd in zip(kkds, rds)]
        on_b = [_split_rows(_mm_nt(left, bd, P_SCORE), 2 * nsub) for left, bd in zip(lefts, bds)]
        on_k = [_split_rows(_mm_nt(left, kd, P_SCORE), 2 * nsub) for left, kd in zip(lefts, kds)]
        m_b = [jnp.where(strict, on_b[g][s], 0.0) for g, s in chains]
        m_k = [jnp.where(strict, on_k[g][s], 0.0) for g, s in chains]
        n_k = [jnp.where(incl, on_k[g][nsub + s], 0.0) for g, s in chains]
        n_b = [jnp.where(incl, on_b[g][nsub + s], 0.0) for g, s in chains]
    t_inv = _tri_inverse(m_b) if inverses is None else _known_inverse(m_b, inverses)
    on_v = [_split_rows(_mm(_stack_rows([mk, nk]), vs[g], P_APPLY), 2) for (g, s), mk, nk in zip(chains, m_k, n_k)]
    sa_c = [_mm(t, w0s[g] + mv[0], P_APPLY) for (g, s), t, mv in zip(chains, t_inv, on_v)]
    y_c = [y0s[g] + mv[1] - _mm(nb, sa, P_APPLY) for (g, s), mv, nb, sa in zip(chains, on_v, n_b, sa_c)]
    per_group = lambda xs: [functools.reduce(lambda p, q: p + q, [xs[g * nsub + s] * masks[s] for s in range(nsub)])
                            for g in range(ng)]
    sas, ys = per_group(sa_c), per_group(y_c)
    s_ends = [s0 * jnp.exp(e) + _mm_tn(_stack_rows([v, -sa]), _stack_rows([kend, bend]), P_UPDATE)
              for s0, e, v, kend, sa, bend in zip(s0s, ends, vs, kends, sas, bends)]
    if nsub > 1:
        same_head = (_iota((LANES, LANES), 0) // hs) == (_iota((LANES, LANES), 1) // hs)
        s_ends = [jnp.where(same_head, s_end, 0.0) for s_end in s_ends]
    return list(zip(ys, s_ends)), t_inv


def _rec_fwd(name, r, lw, k, v, kk, b, *, seq, nsub, scalar_decay):
    n, w = r.shape
    ng = w // LANES
    nc = seq // CHUNK
    nb = n // seq
    nt = nb * ng * nsub

    def body(r_ref, lw_ref, k_ref, v_ref, kk_ref, b_ref, y_ref, s_ref, t_ref, state):
        @pl.when(pl.program_id(0) == 0)
        def _():
            state[...] = jnp.zeros_like(state)
        prims = [(state[bi * ng + g], r_ref[bi, :, _sl(g)], lw_ref[bi, :, _sl(g)], k_ref[bi, :, _sl(g)],
                  v_ref[bi, :, _sl(g)], kk_ref[bi, :, _sl(g)], b_ref[bi, :, _sl(g)])
                 for bi in range(nb) for g in range(ng)]
        outs, t_inv = _chunk_fwd(prims, nsub=nsub, scalar_decay=scalar_decay)
        for i, (y, s_end) in enumerate(outs):
            s_ref[0, i] = prims[i][0]
            y_ref[i // ng, :, _sl(i % ng)] = y
            state[i] = s_end
        for i, t in enumerate(t_inv):
            t_ref[0, i] = t

    row = pl.BlockSpec((nb, CHUNK, w), lambda c: (0, c, 0))
    seqs = lambda a: a.reshape(nb, seq, w)
    y, s_save, t_save = pl.pallas_call(
        body, name=name, grid=(nc,),
        in_specs=[row] * 6,
        out_specs=[row, pl.BlockSpec((1, nb * ng, LANES, LANES), lambda c: (c, 0, 0, 0)),
                   pl.BlockSpec((1, nt, CHUNK, CHUNK), lambda c: (c, 0, 0, 0))],
        out_shape=[jax.ShapeDtypeStruct((nb, seq, w), F32), jax.ShapeDtypeStruct((nc, nb * ng, LANES, LANES), F32),
                   jax.ShapeDtypeStruct((nc, nt, CHUNK, CHUNK), F32)],
        scratch_shapes=[pltpu.VMEM((nb * ng, LANES, LANES), F32)],
        compiler_params=pltpu.CompilerParams(dimension_semantics=("arbitrary",), vmem_limit_bytes=VMEM_LIMIT),
    )(seqs(r), seqs(lw), seqs(k), seqs(v), seqs(kk), seqs(b))
    return y.reshape(n, w), (s_save, t_save)


def _rec_bwd(name, r, lw, k, v, kk, b, saved, dy, *, seq, nsub, scalar_decay):
    n, w = r.shape
    ng = w // LANES
    nc = seq // CHUNK
    nb = n // seq
    nt = nb * ng * nsub
    s_save, t_save = saved

    def body(r_ref, lw_ref, k_ref, v_ref, kk_ref, b_ref, s_ref, t_ref, dy_ref,
             dr_ref, dlw_ref, dk_ref, dv_ref, dkk_ref, db_ref, dstate):
        @pl.when(pl.program_id(0) == 0)
        def _():
            dstate[...] = jnp.zeros_like(dstate)
        inverses = [t_ref[0, i] for i in range(nt)]
        f = lambda p: _chunk_fwd(p, nsub=nsub, scalar_decay=scalar_decay, inverses=inverses)[0]
        chains = [(bi, g) for bi in range(nb) for g in range(ng)]
        prims = [(s_ref[0, bi * ng + g], r_ref[bi, :, _sl(g)], lw_ref[bi, :, _sl(g)], k_ref[bi, :, _sl(g)],
                  v_ref[bi, :, _sl(g)], kk_ref[bi, :, _sl(g)], b_ref[bi, :, _sl(g)]) for bi, g in chains]
        _, vjp = jax.vjp(f, prims)
        (d_prims,) = vjp([(dy_ref[bi, :, _sl(g)], dstate[bi * ng + g]) for bi, g in chains])
        for (bi, g), (ds0, dr, dlw, dk, dv, dkk, db) in zip(chains, d_prims):
            dstate[bi * ng + g] = ds0
            dr_ref[bi, :, _sl(g)] = dr
            dlw_ref[bi, :, _sl(g)] = dlw
            dk_ref[bi, :, _sl(g)] = dk
            dv_ref[bi, :, _sl(g)] = dv
            dkk_ref[bi, :, _sl(g)] = dkk
            db_ref[bi, :, _sl(g)] = db

    row = pl.BlockSpec((nb, CHUNK, w), lambda c: (0, nc - 1 - c, 0))
    seqs = lambda a: a.reshape(nb, seq, w)
    grads = pl.pallas_call(
        body, name=name, grid=(nc,),
        in_specs=[row] * 6 + [pl.BlockSpec((1, nb * ng, LANES, LANES), lambda c: (nc - 1 - c, 0, 0, 0)),
                              pl.BlockSpec((1, nt, CHUNK, CHUNK), lambda c: (nc - 1 - c, 0, 0, 0)), row],
        out_specs=[row] * 6,
        out_shape=[jax.ShapeDtypeStruct((nb, seq, w), F32)] * 6,
        scratch_shapes=[pltpu.VMEM((nb * ng, LANES, LANES), F32)],
        compiler_params=pltpu.CompilerParams(dimension_semantics=("arbitrary",), vmem_limit_bytes=VMEM_LIMIT),
    )(seqs(r), seqs(lw), seqs(k), seqs(v), seqs(kk), seqs(b), s_save, t_save, seqs(dy))
    return [g.reshape(n, w) for g in grads]


def _shift_down(a, j, halo, is_start):
    tb = a.shape[0]
    rolled = pltpu.roll(a, j, 0)
    hr = jnp.where(is_start, 0.0, pltpu.roll(halo, j, 0))
    first = jnp.where(_iota((SUB, LANES), 0) < j, hr, rolled[0:SUB])
    if tb == SUB:
        return first
    return jnp.concatenate([first, rolled[SUB:]], axis=0)


def _shift_up(d, j, carry, is_end):
    tb = d.shape[0]
    up = pltpu.roll(d, tb - j, 0)
    cr = jnp.where(is_end, 0.0, pltpu.roll(carry, SUB - j, 0))
    last = jnp.where(_iota((SUB, LANES), 0) >= SUB - j, cr, up[tb - SUB:tb])
    if tb == SUB:
        return last
    return jnp.concatenate([up[:tb - SUB], last], axis=0)


def _ngroups(a):
    return a.shape[1] // LANES


def _pw_fwd(name, f, ins, shift, params, out_widths, out_dtypes, *, seq, tb):
    n = ins[0].shape[0]
    nt, tps = n // tb, seq // tb
    ni, npar = len(ins), len(params)

    def body(*refs):
        in_refs = refs[:ni]
        pos = ni
        halo_ref = None
        if shift:
            halo_ref = refs[pos]
            pos += 1
        p_refs = refs[pos:pos + npar]
        out_refs = refs[pos + npar:]
        is_start = (pl.program_id(0) % tps) == 0
        tiles = [[ref[:, _sl(g)] for g in range(_ngroups(ref))] for ref in in_refs]
        prevs = [[_shift_down(tiles[0][g], j, halo_ref[:, _sl(g)], is_start) for g in range(len(tiles[0]))]
                 for j in range(1, shift + 1)]
        pv = [[ref[:, _sl(g)] for g in range(_ngroups(ref))] for ref in p_refs]
        outs = f(tiles, prevs, pv)
        for o_ref, og in zip(out_refs, outs, strict=True):
            for g, t in enumerate(og):
                o_ref[:, _sl(g)] = t.astype(o_ref.dtype)

    in_specs = [pl.BlockSpec((tb, a.shape[1]), lambda i: (i, 0)) for a in ins]
    args = list(ins)
    if shift:
        in_specs.append(pl.BlockSpec((SUB, ins[0].shape[1]), lambda i: (jnp.maximum(i * (tb // SUB) - 1, 0), 0)))
        args.append(ins[0])
    in_specs += [pl.BlockSpec(p.shape, lambda i: (0, 0)) for p in params]
    args += list(params)
    return pl.pallas_call(
        body, name=name, grid=(nt,),
        in_specs=in_specs,
        out_specs=[pl.BlockSpec((tb, w), lambda i: (i, 0)) for w in out_widths],
        out_shape=[jax.ShapeDtypeStruct((n, w), dt) for w, dt in zip(out_widths, out_dtypes, strict=True)],
        compiler_params=pltpu.CompilerParams(dimension_semantics=("parallel",), vmem_limit_bytes=VMEM_LIMIT),
    )(*args)


def _pw_bwd(name, f, ins, shift, params, douts, din_dtypes, *, seq, tb):
    n = ins[0].shape[0]
    nt, tps = n // tb, seq // tb
    ni, npar = len(ins), len(params)
    flat_douts = [d for ds in douts for d in ds]
    nd = len(flat_douts)
    w0 = ins[0].shape[1]

    def body(*refs):
        in_refs = refs[:ni]
        pos = ni
        halo_ref = None
        if shift:
            halo_ref = refs[pos]
            pos += 1
        p_refs = refs[pos:pos + npar]
        pos += npar
        d_refs = refs[pos:pos + nd]
        pos += nd
        din_refs = refs[pos:pos + ni]
        pos += ni
        dp_refs = refs[pos:pos + npar]
        pos += npar
        carry = refs[pos] if shift else None
        step = pl.program_id(0)
        tile = nt - 1 - step
        is_start = (tile % tps) == 0
        is_end = (tile % tps) == tps - 1
        tiles = [[ref[:, _sl(g)] for g in range(_ngroups(ref))] for ref in in_refs]
        prevs = [[_shift_down(tiles[0][g], j, halo_ref[:, _sl(g)], is_start) for g in range(len(tiles[0]))]
                 for j in range(1, shift + 1)]
        pv = [[ref[:, _sl(g)] for g in range(_ngroups(ref))] for ref in p_refs]
        cot, pos_d = [], 0
        for ds in douts:
            grp = d_refs[pos_d:pos_d + len(ds)]
            pos_d += len(ds)
            cot.append([functools.reduce(lambda p, q: p + q, [ref[:, _sl(g)].astype(F32) for ref in grp])
                        for g in range(_ngroups(grp[0]))])
        _, vjp = jax.vjp(f, tiles, prevs, pv)
        d_tiles, d_prevs, d_pv = vjp(cot)
        for g in range(len(tiles[0])):
            for j in range(1, shift + 1):
                d_tiles[0][g] = d_tiles[0][g] + _shift_up(d_prevs[j - 1][g], j, carry[j - 1, :, _sl(g)], is_end)
            for j in range(1, shift + 1):
                carry[j - 1, :, _sl(g)] = d_prevs[j - 1][g][0:SUB]
        for ref, dg in zip(din_refs, d_tiles, strict=True):
            for g, t in enumerate(dg):
                ref[:, _sl(g)] = t.astype(ref.dtype)

        @pl.when(step == 0)
        def _():
            for ref in dp_refs:
                ref[...] = jnp.zeros_like(ref)
        for ref, dg in zip(dp_refs, d_pv, strict=True):
            for g, t in enumerate(dg):
                ref[:, _sl(g)] += t

    rev = lambda i: (nt - 1 - i, 0)
    in_specs = [pl.BlockSpec((tb, a.shape[1]), rev) for a in ins]
    args = list(ins)
    if shift:
        in_specs.append(pl.BlockSpec((SUB, w0), lambda i: (jnp.maximum((nt - 1 - i) * (tb // SUB) - 1, 0), 0)))
        args.append(ins[0])
    in_specs += [pl.BlockSpec(p.shape, lambda i: (0, 0)) for p in params]
    args += list(params)
    in_specs += [pl.BlockSpec((tb, d.shape[1]), rev) for d in flat_douts]
    args += flat_douts
    out_specs = [pl.BlockSpec((tb, a.shape[1]), rev) for a in ins] + [pl.BlockSpec(p.shape, lambda i: (0, 0)) for p in params]
    out_shape = ([jax.ShapeDtypeStruct(a.shape, dt) for a, dt in zip(ins, din_dtypes, strict=True)]
                 + [jax.ShapeDtypeStruct(p.shape, F32) for p in params])
    res = pl.pallas_call(
        body, name=name, grid=(nt,),
        in_specs=in_specs, out_specs=out_specs, out_shape=out_shape,
        scratch_shapes=[pltpu.VMEM((shift, SUB, w0), F32)] if shift else [],
        compiler_params=pltpu.CompilerParams(dimension_semantics=("arbitrary",), vmem_limit_bytes=VMEM_LIMIT),
    )(*args)
    return res[:ni], res[ni:]


def _rwkv_prep_f(tiles, prevs, params):
    (p,), (prev,) = tiles, prevs
    mu, w0, w2p, a0, a2p, k_k, k_a = params
    xs = [p[g] + (prev[g] - p[g]) * mu[g] for g in range(13)]
    wdad = xs[12]
    tw = jnp.tanh(wdad)
    e64 = _seg_ones(64)
    r, lw, k2, v, kk, b = [], [], [], [], [], []
    for g in range(4):
        k_g = xs[4 + g]
        lo = w0[g] + _mm(tw, w2p[g], P_POINT)
        lw_g = -jnp.exp(-_softplus(-lo) - 0.5)
        a_g = _sigmoid(a0[g] + _mm(wdad, a2p[g], P_POINT))
        kkp = k_g * k_k[g]
        kk_g = kkp * lax.rsqrt(_mm(kkp * kkp, e64, P_POINT) + 1e-12)
        r.append(xs[g])
        lw.append(lw_g)
        k2.append(k_g * (1.0 + (a_g - 1.0) * k_a[g]))
        v.append(xs[8 + g])
        kk.append(kk_g)
        b.append(kk_g * a_g)
    return [r, lw, k2, v, kk, b]


def _rwkv_post_f(tiles, prevs, params):
    yrec, r, k2, v, z = tiles
    gn_w, gn_b, r_k = params
    e64 = _seg_ones(64)
    out = []
    for g in range(4):
        mean = _mm(yrec[g], e64, P_POINT) * (1.0 / 64)
        d = yrec[g] - mean
        var = _mm(d * d, e64, P_POINT) * (1.0 / 64)
        yn = d * lax.rsqrt(var + RW_GN_EPS) * gn_w[g] + gn_b[g]
        bonus = _mm(r[g] * k2[g] * r_k[g], e64, P_POINT) * v[g]
        out.append((yn + bonus) * _silu(z[g]))
    return [out]


def _gdn_prep_f(tiles, prevs, params):
    x, (ba,) = tiles
    p1, p2, p3 = prevs
    cw0, cw1, cw2, cw3, a_log, dt_bias = params
    s = [_silu(cw3[g] * x[g] + cw2[g] * p1[g] + cw1[g] * p2[g] + cw0[g] * p3[g]) for g in range(12)]
    row = _iota((LANES, LANES), 0)
    r, lw, k, vv, b = [], [], [], [], []
    for h in range(4):
        q_h, k_h, v_h = s[h], s[4 + h], s[8 + h]
        qn = q_h * lax.rsqrt(jnp.sum(q_h * q_h, axis=-1, keepdims=True) + 1e-12)
        kn = k_h * lax.rsqrt(jnp.sum(k_h * k_h, axis=-1, keepdims=True) + 1e-12)
        beta = _sigmoid(_mm(ba, (row == h).astype(F32)))
        alpha = _mm(ba, (row == 4 + h).astype(F32))
        g_h = -jnp.exp(a_log[h]) * _softplus(alpha + dt_bias[h])
        r.append(qn * (LANES ** -0.5))
        lw.append(g_h)
        k.append(kn)
        vv.append(beta * v_h)
        b.append(jnp.exp(g_h) * beta * kn)
    return [r, lw, k, vv, b]


def _gdn_post_f(tiles, prevs, params):
    o, z = tiles
    ((onw,),) = params
    out = []
    for h in range(4):
        ms = jnp.mean(o[h] * o[h], axis=-1, keepdims=True)
        out.append(o[h] * lax.rsqrt(ms + NORM_EPS) * onw * _silu(z[h]))
    return [out]


def _norm_in(x2, g_in, *, tm):
    n = x2.shape[0]

    def body(x_ref, g_ref, h_ref):
        x = x_ref[...]
        rs = lax.rsqrt(jnp.mean(x * x, axis=-1, keepdims=True) + NORM_EPS)
        h_ref[...] = (x * rs * g_ref[...]).astype(BF16)

    return pl.pallas_call(
        body, name="norm_in", grid=(n // tm,),
        in_specs=[pl.BlockSpec((tm, D_MODEL), lambda i: (i, 0)), pl.BlockSpec((1, D_MODEL), lambda i: (0, 0))],
        out_specs=pl.BlockSpec((tm, D_MODEL), lambda i: (i, 0)),
        out_shape=jax.ShapeDtypeStruct((n, D_MODEL), BF16),
        compiler_params=pltpu.CompilerParams(dimension_semantics=("parallel",), vmem_limit_bytes=VMEM_LIMIT),
    )(x2, g_in)


def _proj(name, h, wt, *, tm):
    n, ws = h.shape[0], wt.shape[0]

    def body(h_ref, w_ref, o_ref):
        o_ref[...] = lax.dot_general(h_ref[...], w_ref[...], (_NT, ((), ())), preferred_element_type=F32)

    return pl.pallas_call(
        body, name=name, grid=(n // tm,),
        in_specs=[pl.BlockSpec((tm, D_MODEL), lambda i: (i, 0)), pl.BlockSpec((ws, D_MODEL), lambda i: (0, 0))],
        out_specs=pl.BlockSpec((tm, ws), lambda i: (i, 0)),
        out_shape=jax.ShapeDtypeStruct((n, ws), F32),
        compiler_params=pltpu.CompilerParams(dimension_semantics=("parallel",), vmem_limit_bytes=VMEM_LIMIT),
    )(h, wt)


def _proj_dw(name, h, dp, *, tm):
    n, ws = dp.shape

    def body(h_ref, d_ref, o_ref):
        @pl.when(pl.program_id(0) == 0)
        def _():
            o_ref[...] = jnp.zeros_like(o_ref)
        o_ref[...] += lax.dot_general(d_ref[...], h_ref[...], (_TN, ((), ())), preferred_element_type=F32)

    return pl.pallas_call(
        body, name=name, grid=(n // tm,),
        in_specs=[pl.BlockSpec((tm, D_MODEL), lambda i: (i, 0)), pl.BlockSpec((tm, ws), lambda i: (i, 0))],
        out_specs=pl.BlockSpec((ws, D_MODEL), lambda i: (0, 0)),
        out_shape=jax.ShapeDtypeStruct((ws, D_MODEL), F32),
        compiler_params=pltpu.CompilerParams(dimension_semantics=("arbitrary",), vmem_limit_bytes=VMEM_LIMIT),
    )(h, dp)


def _proj_dx(x2, g_in, d_xo, dps, ws, *, tm):
    n = x2.shape[0]
    ns = len(dps)

    def body(*refs):
        x_ref, g_ref, dxo_ref = refs[:3]
        dp_refs = refs[3:3 + ns]
        w_refs = refs[3 + ns:3 + 2 * ns]
        dx_ref, dg_ref = refs[3 + 2 * ns:]
        dh = jnp.zeros((tm, D_MODEL), F32)
        for d_ref, w_ref in zip(dp_refs, w_refs, strict=True):
            dh = dh + jnp.dot(d_ref[...], w_ref[...], preferred_element_type=F32)
        x = x_ref[...]
        rs = lax.rsqrt(jnp.mean(x * x, axis=-1, keepdims=True) + NORM_EPS)
        xn = x * rs
        dxn = dh * g_ref[...]
        dx_ref[...] = dxo_ref[...] + rs * (dxn - xn * jnp.mean(dxn * xn, axis=-1, keepdims=True))

        @pl.when(pl.program_id(0) == 0)
        def _():
            dg_ref[...] = jnp.zeros_like(dg_ref)
        dg_ref[...] += jnp.sum(dh * xn, axis=0, keepdims=True)

    row = pl.BlockSpec((tm, D_MODEL), lambda i: (i, 0))
    return pl.pallas_call(
        body, name="proj_dx", grid=(n // tm,),
        in_specs=([row, pl.BlockSpec((1, D_MODEL), lambda i: (0, 0)), row]
                  + [pl.BlockSpec((tm, d.shape[1]), lambda i: (i, 0)) for d in dps]
                  + [pl.BlockSpec(w.shape, lambda i: (0, 0)) for w in ws]),
        out_specs=[row, pl.BlockSpec((1, D_MODEL), lambda i: (0, 0))],
        out_shape=[jax.ShapeDtypeStruct((n, D_MODEL), F32), jax.ShapeDtypeStruct((1, D_MODEL), F32)],
        compiler_params=pltpu.CompilerParams(dimension_semantics=("arbitrary",), vmem_limit_bytes=VMEM_LIMIT),
    )(x2, g_in, d_xo, *dps, *ws)


def _tail(x2, tgt2, gates, ya, yb, w_a, w_b, w_o, now, *, tr):
    n = x2.shape[0]

    def body(x_ref, t_ref, g_ref, ya_ref, yb_ref, wa_ref, wb_ref, wo_ref, now_ref,
             dya_ref, dyb_ref, dg_ref, dxo_ref, dwa_ref, dwb_ref, dwo_ref, dnow_ref, loss_ref):
        ya16, yb16 = ya_ref[...].astype(BF16), yb_ref[...].astype(BF16)
        ua = jnp.dot(ya16, wa_ref[...], preferred_element_type=F32)
        ub = jnp.dot(yb16, wb_ref[...], preferred_element_type=F32)
        ga = _sigmoid(g_ref[:, :D_MODEL])
        gb = _sigmoid(g_ref[:, D_MODEL:])
        m16 = (ga * ua + gb * ub).astype(BF16)
        xo = x_ref[...] + jnp.dot(m16, wo_ref[...], preferred_element_type=F32)
        rs = lax.rsqrt(jnp.mean(xo * xo, axis=-1, keepdims=True) + NORM_EPS)
        yn = xo * rs
        now_v = now_ref[...]
        err = yn * now_v - t_ref[...]
        dy = err * (1.0 / D_MODEL)
        dyn = dy * now_v
        dxo = rs * (dyn - yn * jnp.mean(dyn * yn, axis=-1, keepdims=True))
        dxo_ref[...] = dxo
        dxo16 = dxo.astype(BF16)
        dm = lax.dot_general(dxo16, wo_ref[...], (((1,), (1,)), ((), ())), preferred_element_type=F32)
        dua16 = (dm * ga).astype(BF16)
        dub16 = (dm * gb).astype(BF16)
        dg_ref[:, :D_MODEL] = (dm * ua * ga * (1.0 - ga)).astype(dg_ref.dtype)
        dg_ref[:, D_MODEL:] = (dm * ub * gb * (1.0 - gb)).astype(dg_ref.dtype)
        dya_ref[...] = lax.dot_general(dua16, wa_ref[...], (((1,), (1,)), ((), ())), preferred_element_type=F32)
        dyb_ref[...] = lax.dot_general(dub16, wb_ref[...], (((1,), (1,)), ((), ())), preferred_element_type=F32)

        @pl.when(pl.program_id(0) == 0)
        def _():
            for ref in (dwa_ref, dwb_ref, dwo_ref, dnow_ref, loss_ref):
                ref[...] = jnp.zeros_like(ref)
        tn = (((0,), (0,)), ((), ()))
        dwo_ref[...] += lax.dot_general(m16, dxo16, tn, preferred_element_type=F32)
        dwa_ref[...] += lax.dot_general(ya16, dua16, tn, preferred_element_type=F32)
        dwb_ref[...] += lax.dot_general(yb16, dub16, tn, preferred_element_type=F32)
        dnow_ref[...] += jnp.sum(dy * yn, axis=0, keepdims=True)
        loss_ref[...] += (0.5 / D_MODEL) * jnp.sum(err * err)

    row = lambda w: pl.BlockSpec((tr, w), lambda i: (i, 0))
    full = lambda a: pl.BlockSpec(a.shape, lambda i: (0, 0))
    return pl.pallas_call(
        body, name="tail", grid=(n // tr,),
        in_specs=[row(D_MODEL), row(D_MODEL), row(2 * D_MODEL), row(RW_W), row(GD_W), full(w_a), full(w_b), full(w_o), full(now)],
        out_specs=[row(RW_W), row(GD_W), row(2 * D_MODEL), row(D_MODEL),
                   pl.BlockSpec((RW_W, D_MODEL), lambda i: (0, 0)), pl.BlockSpec((GD_W, D_MODEL), lambda i: (0, 0)),
                   pl.BlockSpec((D_MODEL, D_MODEL), lambda i: (0, 0)), pl.BlockSpec((1, D_MODEL), lambda i: (0, 0)),
                   pl.BlockSpec((SUB, LANES), lambda i: (0, 0))],
        out_shape=[jax.ShapeDtypeStruct((n, RW_W), F32), jax.ShapeDtypeStruct((n, GD_W), F32),
                   jax.ShapeDtypeStruct((n, 2 * D_MODEL), BF16), jax.ShapeDtypeStruct((n, D_MODEL), F32),
                   jax.ShapeDtypeStruct((RW_W, D_MODEL), F32), jax.ShapeDtypeStruct((GD_W, D_MODEL), F32),
                   jax.ShapeDtypeStruct((D_MODEL, D_MODEL), F32), jax.ShapeDtypeStruct((1, D_MODEL), F32),
                   jax.ShapeDtypeStruct((SUB, LANES), F32)],
        compiler_params=pltpu.CompilerParams(dimension_semantics=("arbitrary",), vmem_limit_bytes=VMEM_LIMIT),
    )(x2, tgt2, gates, ya, yb, w_a, w_b, w_o, now)


def _exchange(name, axes, scatter, gather):
    ns, ng = len(scatter), len(gather)
    na = ns + ng
    gs = 2 ** len(axes)
    arrs = list(scatter) + list(gather)

    def body(*refs):
        src = refs[:na]
        dst = refs[na:2 * na]
        send_sems, recv_sems = refs[2 * na:]
        mine = {ax: lax.axis_index(ax) for ax in ("x", "y", "c")}

        def peer(k):
            co = dict(mine)
            for i, ax in enumerate(axes):
                if (k >> (len(axes) - 1 - i)) & 1:
                    co[ax] = 1 - co[ax]
            idx = 0
            for ax in axes:
                idx = 2 * idx + co[ax]
            return (co["x"], co["y"], co["c"]), idx

        _, me = peer(0)

        def copy(a, k, landing):
            dev, idx = peer(k)
            s = src[a].at[idx] if a < ns else src[a]
            return pltpu.make_async_remote_copy(src_ref=s, dst_ref=dst[a].at[idx if landing else me],
                                                send_sem=send_sems.at[a, k - 1], recv_sem=recv_sems.at[a, k - 1],
                                                device_id=dev, device_id_type=pl.DeviceIdType.MESH)

        sends = [copy(a, k, False) for a in range(na) for k in range(1, gs)]
        for cp in sends:
            cp.start()
        for a in range(na):
            for k in range(1, gs):
                copy(a, k, True).wait_recv()
        for cp in sends:
            cp.wait_send()

    out_shape = [jax.ShapeDtypeStruct(a.shape, a.dtype) for a in scatter] + \
                [jax.ShapeDtypeStruct((gs,) + a.shape, a.dtype) for a in gather]
    anyspec = pl.BlockSpec(memory_space=pl.ANY)
    lands = pl.pallas_call(
        body, name=name,
        in_specs=[anyspec] * na, out_specs=[anyspec] * na, out_shape=out_shape,
        scratch_shapes=[pltpu.SemaphoreType.DMA((na, gs - 1)), pltpu.SemaphoreType.DMA((na, gs - 1))],
    )(*arrs)
    me = 0
    for ax in axes:
        me = 2 * me + lax.axis_index(ax)
    kept = [lax.dynamic_index_in_dim(a, me, 0, keepdims=False) for a in scatter] + list(gather)
    return [lax.dynamic_update_index_in_dim(land, mine, me, 0) for land, mine in zip(lands, kept)]


def _gather_all(name, arrs):
    na = len(arrs)

    def body(*refs):
        src = refs[:na]
        dst = refs[na:2 * na]
        send_sems, recv_sems = refs[2 * na:]
        x, y, c = lax.axis_index("x"), lax.axis_index("y"), lax.axis_index("c")
        sibling = (x, y, 1 - c)
        chips = [(1 - x, y), (x, 1 - y), (1 - x, 1 - y)]

        def copy(a, k, block, to, own=False):
            px, py, pc = block
            slot = dst[a].at[pc, 2 * px + py]
            return pltpu.make_async_remote_copy(src_ref=src[a] if own else slot, dst_ref=slot,
                                                send_sem=send_sems.at[a, k], recv_sem=recv_sems.at[a, k],
                                                device_id=to, device_id_type=pl.DeviceIdType.MESH)

        first = [copy(a, 0, (x, y, c), sibling, own=True) for a in range(na)]
        first += [copy(a, 1 + j, (x, y, c), (*chip, c), own=True) for j, chip in enumerate(chips) for a in range(na)]
        for cp in first:
            cp.start()
        passed = []
        for j, chip in enumerate(chips):
            for a in range(na):
                copy(a, 1 + j, (*chip, c), (x, y, c)).wait_recv()
                passed.append(copy(a, 4 + j, (*chip, c), sibling))
                passed[-1].start()
        for a in range(na):
            copy(a, 0, (x, y, 1 - c), (x, y, c)).wait_recv()
            for j, chip in enumerate(chips):
                copy(a, 4 + j, (*chip, 1 - c), (x, y, c)).wait_recv()
        for cp in first + passed:
            cp.wait_send()

    anyspec = pl.BlockSpec(memory_space=pl.ANY)
    lands = pl.pallas_call(
        body, name=name,
        in_specs=[anyspec] * na, out_specs=[anyspec] * na,
        out_shape=[jax.ShapeDtypeStruct((2, 4) + a.shape, a.dtype) for a in arrs],
        scratch_shapes=[pltpu.SemaphoreType.DMA((na, 7)), pltpu.SemaphoreType.DMA((na, 7))],
    )(*arrs)
    core, chip = lax.axis_index("c"), 2 * lax.axis_index("x") + lax.axis_index("y")
    zero = jnp.zeros((), jnp.int32)
    return [lax.dynamic_update_slice(land, mine[None, None], (core, chip) + (zero,) * mine.ndim)
            for land, mine in zip(lands, arrs)]


def _sum_slots(name, land, out_dtype):
    ns, nq, r, c = land.shape

    def body(l_ref, o_ref):
        acc = l_ref[0, 0].astype(F32)
        for s in range(1, ns):
            acc = acc + l_ref[s, 0].astype(F32)
        o_ref[0] = acc.astype(o_ref.dtype)

    return pl.pallas_call(
        body, name=name, grid=(nq,),
        in_specs=[pl.BlockSpec((ns, 1, r, c), lambda i: (0, i, 0, 0))],
        out_specs=pl.BlockSpec((1, r, c), lambda i: (i, 0, 0)),
        out_shape=jax.ShapeDtypeStruct((nq, r, c), out_dtype),
        compiler_params=pltpu.CompilerParams(dimension_semantics=("parallel",), vmem_limit_bytes=VMEM_LIMIT),
    )(land)


def _adam(name, land, w, m, v):
    r, c = w.shape
    nslot = land.shape[0]
    tr = 256 if (r % 256 == 0 and r > 256) else r
    tc = 256 if (tr == r and r > 256 and c % 256 == 0) else c

    def body(l_ref, w_ref, m_ref, v_ref, g_out, d_out, m_out, v_out):
        g = l_ref[0].astype(F32)
        for s in range(1, nslot):
            g = g + l_ref[s].astype(F32)
        g_out[...] = g
        d_out[...], m_out[...], v_out[...] = _adam_math(g, w_ref[...], m_ref[...], v_ref[...])

    blk = pl.BlockSpec((tr, tc), lambda i: (i * tc // c, i % (c // tc)))
    return pl.pallas_call(
        body, name=name, grid=((r // tr) * (c // tc),),
        in_specs=[pl.BlockSpec((nslot, tr, tc), lambda i: (0, i * tc // c, i % (c // tc))), blk, blk, blk],
        out_specs=[blk] * 4,
        out_shape=[jax.ShapeDtypeStruct((r, c), F32)] * 4,
        compiler_params=pltpu.CompilerParams(dimension_semantics=("parallel",), vmem_limit_bytes=VMEM_LIMIT),
    )(land, w, m, v)


def _adam_math(g, w, m, v):
    c1 = 1.0 / (1.0 - ADAM_B1 ** ADAM_STEP)
    c2 = 1.0 / (1.0 - ADAM_B2 ** ADAM_STEP)
    m_new = ADAM_B1 * m + (1.0 - ADAM_B1) * g
    v_new = ADAM_B2 * v + (1.0 - ADAM_B2) * (g * g)
    return -ADAM_LR * ((m_new * c1) / (jnp.sqrt(v_new * c2) + ADAM_EPS) + ADAM_WD * w), m_new, v_new


def _adam_small(land, ws, ms, vs):
    npar = len(ws)
    nslot = land.shape[0]

    def body(*refs):
        l_ref = refs[0]
        w_refs, m_refs, v_refs = refs[1:1 + npar], refs[1 + npar:1 + 2 * npar], refs[1 + 2 * npar:1 + 3 * npar]
        outs = refs[1 + 3 * npar:1 + 7 * npar]
        loss_ref, g_rows = refs[1 + 7 * npar], refs[2 + 7 * npar]
        g = l_ref[0]
        for s in range(1, nslot):
            g = g + l_ref[s]
        g_rows[...] = g
        row = 0
        for i, (_, size) in enumerate(_SMALL):
            for j in range(-(-size // LANES)):
                width = min(LANES, size - j * LANES)
                cols = slice(j * LANES, j * LANES + width)
                g_ij = g_rows[row:row + 1, 0:width]
                delta, m_new, v_new = _adam_math(g_ij, w_refs[i][:, cols], m_refs[i][:, cols], v_refs[i][:, cols])
                for ref, val in zip(outs[4 * i:4 * i + 4], (g_ij, delta, m_new, v_new)):
                    ref[:, cols] = val
                row += 1
        loss_ref[...] = g_rows[row:row + 1, :]

    full = lambda a: pl.BlockSpec(a.shape, lambda: (0,) * a.ndim)
    res = pl.pallas_call(
        body, name="adam_small",
        in_specs=[full(land)] + [full(a) for a in list(ws) + list(ms) + list(vs)],
        out_specs=[full(w) for w in ws for _ in range(4)] + [pl.BlockSpec((1, LANES), lambda: (0, 0))],
        out_shape=[jax.ShapeDtypeStruct(w.shape, F32) for w in ws for _ in range(4)] + [jax.ShapeDtypeStruct((1, LANES), F32)],
        scratch_shapes=[pltpu.VMEM(land.shape[1:], F32)],
    )(land, *ws, *ms, *vs)
    return [res[4 * i:4 * i + 4] for i in range(npar)], res[4 * npar]


_SMALL = (("norm_in_w", 1024), ("rw_mu", 1664), ("rw_w0", 512), ("rw_a0", 512), ("rw_k_k", 512), ("rw_k_a", 512),
          ("rw_r_k", 512), ("rw_gn_w", 512), ("rw_gn_b", 512), ("gd_A_log", 4), ("gd_dt_bias", 4), ("gd_o_norm_w", 128),
          ("norm_out_w", 1024))
_SMALL_ROWS = 64


def _pack_small(vals, loss_row):
    rows = []
    for (_, size), a in zip(_SMALL, vals, strict=True):
        flat = a.reshape(-1).astype(F32)
        pad = (-size) % LANES
        if pad:
            flat = jnp.concatenate([flat, jnp.zeros((pad,), F32)])
        rows.append(flat.reshape(-1, LANES))
    rows.append(loss_row)
    used = sum(r.shape[0] for r in rows)
    rows.append(jnp.zeros((_SMALL_ROWS - used, LANES), F32))
    return jnp.concatenate(rows, axis=0)


def kernel(x, norm_in_w, w_in, rw_mu, rw_w0, rw_w2, rw_a0, rw_a2, rw_k_k, rw_k_a, rw_r_k, rw_gn_w, rw_gn_b, gd_conv_w, gd_A_log, gd_dt_bias, gd_o_norm_w, w_branch_a, w_branch_b, w_out, norm_out_w, loss_target, m_norm_in_w, m_w_in, m_rw_mu, m_rw_w0, m_rw_w2, m_rw_a0, m_rw_a2, m_rw_k_k, m_rw_k_a, m_rw_r_k, m_rw_gn_w, m_rw_gn_b, m_gd_conv_w, m_gd_A_log, m_gd_dt_bias, m_gd_o_norm_w, m_w_branch_a, m_w_branch_b, m_w_out, m_norm_out_w, v_norm_in_w, v_w_in, v_rw_mu, v_rw_w0, v_rw_w2, v_rw_a0, v_rw_a2, v_rw_k_k, v_rw_k_a, v_rw_r_k, v_rw_gn_w, v_rw_gn_b, v_gd_conv_w, v_gd_A_log, v_gd_dt_bias, v_gd_o_norm_w, v_w_branch_a, v_w_branch_b, v_w_out, v_norm_out_w):
    nb, seq, _ = x.shape
    n = nb * seq
    tm = min(512, n)
    tb = min(512, seq)
    x2 = x.reshape(n, D_MODEL)
    tgt2 = loss_target.reshape(n, D_MODEL)
    cols = w_in.shape[2]
    in_cols = cols * N_DEV

    wt_own, mt_own, vt_own = w_in[0].T, m_w_in[0].T, v_w_in[0].T
    sharded = [wt_own.astype(BF16), rw_w2[0], rw_a2[0], gd_conv_w[0], w_branch_a[0].astype(BF16),
               w_branch_b[0].astype(BF16), w_out[0].astype(BF16)]
    g_win, g_w2, g_a2, g_conv, g_wa, g_wb, g_wo = _gather_all("gather_weights", sharded)
    unshard_rows = lambda a: jnp.transpose(a, (1, 0, 2, 3)).reshape(N_DEV * a.shape[2], a.shape[3])
    unshard_cols = lambda a: jnp.transpose(a, (2, 1, 0, 3)).reshape(a.shape[2], N_DEV * a.shape[3])
    wt_full = unshard_rows(g_win)
    seg_bounds = ((0, 1664), (1664, 2176), (2176, 3712), (3712, 4224), (4232, in_cols))
    w_rw, w_zrw, w_qkv, w_zgd, w_gates = [wt_full[a:b] for a, b in seg_bounds]
    w_ba = jnp.concatenate([wt_full[4224:4232], jnp.zeros((LANES - 8, D_MODEL), BF16)], axis=0)
    w2_full, a2_full = unshard_cols(g_w2), unshard_cols(g_a2)
    zeros64 = jnp.zeros((64, RW_W), F32)
    w2p = jnp.concatenate([w2_full, zeros64], axis=0)
    a2p = jnp.concatenate([zeros64, a2_full], axis=0)
    conv_full = unshard_cols(g_conv)
    conv_rows = [conv_full[i:i + 1] for i in range(4)]
    wa_full = unshard_cols(g_wa)
    wb_full = unshard_cols(g_wb)
    wo_full = unshard_rows(g_wo)
    a_log_bc = jnp.repeat(gd_A_log, LANES, axis=1)
    dt_bias_bc = jnp.repeat(gd_dt_bias, LANES, axis=1)
    r_k_flat = rw_r_k.reshape(1, RW_W)
    now2 = norm_out_w.reshape(1, D_MODEL)

    h = _norm_in(x2, norm_in_w, tm=tm)
    p_rw = _proj("proj_rw", h, w_rw, tm=tm)
    p_zrw = _proj("proj_zrw", h, w_zrw, tm=tm)
    p_qkv = _proj("proj_qkv", h, w_qkv, tm=tm)
    p_zgd = _proj("proj_zgd", h, w_zgd, tm=tm)
    p_ba = _proj("proj_ba", h, w_ba, tm=tm)
    p_gates = _proj("proj_gates", h, w_gates, tm=tm)

    rw_params = [rw_mu, rw_w0, w2p, rw_a0, a2p, rw_k_k, rw_k_a]
    r_a, lw_a, k_a, v_a, kk_a, b_a = _pw_fwd("rwkv_prep", _rwkv_prep_f, [p_rw], 1, rw_params, [RW_W] * 6, [F32] * 6,
                                             seq=seq, tb=tb)
    y_rec, s_a = _rec_fwd("rwkv_rec", r_a, lw_a, k_a, v_a, kk_a, b_a, seq=seq, nsub=2, scalar_decay=False)
    post_params = [rw_gn_w, rw_gn_b, r_k_flat]
    (y_a,) = _pw_fwd("rwkv_post", _rwkv_post_f, [y_rec, r_a, k_a, v_a, p_zrw], 0, post_params, [RW_W], [F32], seq=seq, tb=tb)

    gd_params = conv_rows + [a_log_bc, dt_bias_bc]
    r_b, lw_b, k_b, v_b, b_b = _pw_fwd("gdn_prep", _gdn_prep_f, [p_qkv, p_ba], 3, gd_params, [GD_W] * 5, [F32] * 5,
                                       seq=seq, tb=tb)
    o_rec, s_b = _rec_fwd("gdn_rec", r_b, lw_b, k_b, v_b, k_b, b_b, seq=seq, nsub=1, scalar_decay=True)
    (y_b,) = _pw_fwd("gdn_post", _gdn_post_f, [o_rec, p_zgd], 0, [gd_o_norm_w], [GD_W], [F32], seq=seq, tb=tb)

    d_ya, d_yb, d_gates, d_xo, dwa, dwb, dwo, d_now, loss_acc = _tail(
        x2, tgt2, p_gates, y_a, y_b, wa_full, wb_full, wo_full, now2, tr=min(256, n))

    (d_o, d_zgd), (d_onw,) = _pw_bwd("gdn_post_bwd", _gdn_post_f, [o_rec, p_zgd], 0, [gd_o_norm_w], [[d_yb]],
                                     [F32, BF16], seq=seq, tb=tb)
    dr_b, dlw_b, dk_b, dv_b, dkk_b, db_b = _rec_bwd("gdn_rec_bwd", r_b, lw_b, k_b, v_b, k_b, b_b, s_b, d_o,
                                                    seq=seq, nsub=1, scalar_decay=True)
    (d_qkv, d_ba), d_gd_params = _pw_bwd("gdn_prep_bwd", _gdn_prep_f, [p_qkv, p_ba], 3, gd_params,
                                         [[dr_b], [dlw_b], [dk_b, dkk_b], [dv_b], [db_b]], [BF16, BF16], seq=seq, tb=tb)

    (d_yrec, dr_p, dk_p, dv_p, d_zrw), d_post_params = _pw_bwd(
        "rwkv_post_bwd", _rwkv_post_f, [y_rec, r_a, k_a, v_a, p_zrw], 0, post_params, [[d_ya]],
        [F32, F32, F32, F32, BF16], seq=seq, tb=tb)
    dr_a, dlw_a, dk_a, dv_a, dkk_a, db_a = _rec_bwd("rwkv_rec_bwd", r_a, lw_a, k_a, v_a, kk_a, b_a, s_a, d_yrec,
                                                    seq=seq, nsub=2, scalar_decay=False)
    (d_prw,), d_rw_params = _pw_bwd("rwkv_prep_bwd", _rwkv_prep_f, [p_rw], 1, rw_params,
                                    [[dr_a, dr_p], [dlw_a], [dk_a, dk_p], [dv_a, dv_p], [dkk_a], [db_a]], [BF16],
                                    seq=seq, tb=tb)

    dps = [d_prw, d_zrw, d_qkv, d_zgd, d_ba, d_gates]
    wsegs = [w_rw, w_zrw, w_qkv, w_zgd, w_ba, w_gates]
    dx2, d_gin = _proj_dx(x2, norm_in_w, d_xo, dps, wsegs, tm=min(256, n))
    dw_rw = _proj_dw("dw_rw", h, d_prw, tm=tm)
    dw_zrw = _proj_dw("dw_zrw", h, d_zrw, tm=tm)
    dw_qkv = _proj_dw("dw_qkv", h, d_qkv, tm=tm)
    dw_zgd = _proj_dw("dw_zgd", h, d_zgd, tm=tm)
    dw_ba = _proj_dw("dw_ba", h, d_ba, tm=tm)
    dw_gates = _proj_dw("dw_gates", h, d_gates, tm=tm)
    dwt_in_full = jnp.concatenate([dw_rw, dw_zrw, dw_qkv, dw_zgd, dw_ba[:8], dw_gates], axis=0)

    shard_cols = lambda a: jnp.transpose(a.reshape(a.shape[0], 4, 2, a.shape[1] // N_DEV), (2, 1, 0, 3))
    shard_rows = lambda a: jnp.transpose(a.reshape(4, 2, a.shape[0] // N_DEV, a.shape[1]), (1, 0, 2, 3))
    d_mu, d_w0, d_w2p, d_a0, d_a2p, d_kk_, d_ka_ = d_rw_params
    d_gnw, d_gnb, d_rk = d_post_params
    d_conv = jnp.concatenate(d_gd_params[:4], axis=0)
    d_alog = d_gd_params[4].reshape(4, LANES).sum(axis=1).reshape(1, 4)
    d_dtb = d_gd_params[5].reshape(4, LANES).sum(axis=1).reshape(1, 4)
    scat = [shard_rows(dwt_in_full), shard_cols(d_w2p[:64]), shard_cols(d_a2p[64:]), shard_cols(d_conv),
            shard_cols(dwa), shard_cols(dwb), shard_rows(dwo)]
    small_g = _pack_small([d_gin, d_mu, d_w0, d_a0, d_kk_, d_ka_, d_rk, d_gnw, d_gnb, d_alog, d_dtb, d_onw, d_now],
                          loss_acc[0:1])
    pair = _exchange("reduce_cores", ("c",), scat, [small_g])
    part = [_sum_slots("pair_sum_%d" % i, a, BF16) for i, a in enumerate(pair[:7])]
    part_small = _sum_slots("pair_sum_small", pair[7][:, None], F32)[0]
    lands = _exchange("reduce_chips", ("x", "y"), part, [part_small])

    small_w = [norm_in_w, rw_mu, rw_w0, rw_a0, rw_k_k, rw_k_a, rw_r_k, rw_gn_w, rw_gn_b, gd_A_log, gd_dt_bias, gd_o_norm_w, norm_out_w]
    small_m = [m_norm_in_w, m_rw_mu, m_rw_w0, m_rw_a0, m_rw_k_k, m_rw_k_a, m_rw_r_k, m_rw_gn_w, m_rw_gn_b, m_gd_A_log, m_gd_dt_bias, m_gd_o_norm_w, m_norm_out_w]
    small_v = [v_norm_in_w, v_rw_mu, v_rw_w0, v_rw_a0, v_rw_k_k, v_rw_k_a, v_rw_r_k, v_rw_gn_w, v_rw_gn_b, v_gd_A_log, v_gd_dt_bias, v_gd_o_norm_w, v_norm_out_w]
    flat = lambda arrs: [a.reshape(1, -1) for a in arrs]
    sm, loss_row = _adam_small(lands[7], flat(small_w), flat(small_m), flat(small_v))
    sm_g, sm_d, sm_m, sm_v = [{nm: res[i].reshape(w.shape) for (nm, _), res, w in zip(_SMALL, sm, small_w)}
                              for i in range(4)]

    big = {"w_in": [o.T[None] for o in _adam("adam_w_in", lands[0], wt_own, mt_own, vt_own)]}
    for nm, land, w, m, v in (("rw_w2", lands[1], rw_w2, m_rw_w2, v_rw_w2),
                              ("rw_a2", lands[2], rw_a2, m_rw_a2, v_rw_a2),
                              ("gd_conv_w", lands[3], gd_conv_w, m_gd_conv_w, v_gd_conv_w),
                              ("w_branch_a", lands[4], w_branch_a, m_w_branch_a, v_w_branch_a),
                              ("w_branch_b", lands[5], w_branch_b, m_w_branch_b, v_w_branch_b),
                              ("w_out", lands[6], w_out, m_w_out, v_w_out)):
        big[nm] = [o.reshape(w.shape) for o in _adam("adam_" + nm, land, w[0], m[0], v[0])]

    order = ["norm_in_w", "w_in", "rw_mu", "rw_w0", "rw_w2", "rw_a0", "rw_a2", "rw_k_k", "rw_k_a", "rw_r_k", "rw_gn_w",
             "rw_gn_b", "gd_conv_w", "gd_A_log", "gd_dt_bias", "gd_o_norm_w", "w_branch_a", "w_branch_b", "w_out", "norm_out_w"]
    pick = lambda nm, i: big[nm][i] if nm in big else (sm_g, sm_d, sm_m, sm_v)[i][nm]
    loss = loss_row[0, 0]
    grad_x = dx2.reshape(x.shape)
    return (loss, grad_x, *[pick(nm, 0) for nm in order], *[pick(nm, 1) for nm in order],
            *[pick(nm, 2) for nm in order], *[pick(nm, 3) for nm in order])
```

```python
import functools

import jax
import jax.numpy as jnp
from jax import lax
from jax.experimental import pallas as pl
from jax.experimental.pallas import tpu as pltpu

F32 = jnp.float32
BF16 = jnp.bfloat16
HI = lax.Precision.HIGHEST

LANES = 128
SUB = 8
CHUNK = 64
N_DEV = 8
VMEM_LIMIT = 56 * 1024 * 1024

D_MODEL = 1024
RW_W = 512
GD_W = 512
RW_SHIFT = 1664
NORM_EPS = 1e-6
RW_GN_EPS = 64 * 1e-5
ADAM_LR, ADAM_B1, ADAM_B2, ADAM_EPS, ADAM_WD, ADAM_STEP = 0.001, 0.9, 0.999, 1e-8, 0.01, 10


_NN, _NT, _TN = ((1,), (0,)), ((1,), (1,)), ((0,), (0,))


def _dot(a, b, dims, passes):
    precision = lax.Precision.HIGH if passes == 3 else lax.Precision.DEFAULT
    return lax.dot_general(a, b, (dims, ((), ())), precision=precision, preferred_element_type=F32)


def _mm(a, b, passes=3):
    return _dot(a, b, _NN, passes)


def _mm_nt(a, b, passes=3):
    return _dot(a, b, _NT, passes)


def _mm_tn(a, b, passes=3):
    return _dot(a, b, _TN, passes)


P_SUM = 3
P_SCORE = 1
P_INV = 1
P_STATE = 1
P_APPLY = 1
P_UPDATE = 1
P_POINT = 1


def _stack_rows(blocks):
    return jnp.concatenate(blocks, axis=0)


def _split_rows(x, n):
    r = x.shape[0] // n

    @jax.custom_vjp
    def split(x):
        return tuple(x[i * r:(i + 1) * r] for i in range(n))

    split.defvjp(lambda x: (split(x), None), lambda _, gs: (jnp.concatenate(gs, axis=0),))
    return split(x)


def _iota(shape, d):
    return lax.broadcasted_iota(jnp.int32, shape, d)


def _sigmoid(x):
    return 0.5 * (jnp.tanh(0.5 * x) + 1.0)


def _silu(x):
    return x * _sigmoid(x)


def _softplus(x):
    return jnp.maximum(x, 0.0) + jnp.log(1.0 + jnp.exp(-jnp.abs(x)))


def _seg_ones(seg):
    return ((_iota((LANES, LANES), 0) // seg) == (_iota((LANES, LANES), 1) // seg)).astype(F32)


def _sl(g):
    return slice(g * LANES, (g + 1) * LANES)


@jax.custom_vjp
def _tri_inverse(ms):
    return _tri_inverse_chain(ms)


def _tri_inverse_bwd(ts, dts):
    return ([-_mm_nt(_mm_tn(t, dt, P_INV), t, P_INV) for t, dt in zip(ts, dts)],)


def _tri_inverse_chain(ms):
    c = CHUNK
    ri, ci = _iota((c, c), 0), _iota((c, c), 1)
    eye = (ri == ci).astype(F32)
    d16 = (ri // 16) == (ci // 16)
    d32 = (ri // 32) == (ci // 32)
    ps = [jnp.where(d16, -m, 0.0) for m in ms]
    ts = [eye + p for p in ps]
    for _ in range(3):
        ps = [_mm(p, p, P_INV) for p in ps]
        ts = [_mm(t, eye + p, P_INV) for t, p in zip(ts, ps)]
    for off_diagonal in (d32 & (~d16), ~d32):
        tq = [_mm(t, jnp.where(off_diagonal, m, 0.0), P_INV) for t, m in zip(ts, ms)]
        ts = [t - _mm(a, t, P_INV) for t, a in zip(ts, tq)]
    return ts


_tri_inverse.defvjp(lambda ms: (lambda ts: (ts, ts))(_tri_inverse_chain(ms)), _tri_inverse_bwd)


@jax.custom_vjp
def _known_inverse(ms, ts):
    return ts


_known_inverse.defvjp(lambda ms, ts: (ts, ts),
                      lambda ts, dts: (_tri_inverse_bwd(ts, dts)[0], [jnp.zeros_like(t) for t in ts]))


def _chunk_fwd(prims, *, nsub, scalar_decay, inverses=None):
    c = CHUNK
    ng = len(prims)
    s0s, rs, lws, ks, vs, kks, bs = [list(t) for t in zip(*prims)]
    ri, ci = _iota((c, c), 0), _iota((c, c), 1)
    incl = ri >= ci
    strict = ri > ci
    tril = incl.astype(F32)
    hs = LANES // nsub
    lane = _iota((1, LANES), 1)
    masks = [((lane // hs) == s).astype(F32) for s in range(nsub)] if nsub > 1 else [1.0]
    cws = [_mm(tril, lw, P_SUM) for lw in lws]
    cwxs = [cw - lw for cw, lw in zip(cws, lws)]
    ends = [cw[c - 1:c, :] for cw in cws]
    kkds = [kk * jnp.exp(cwx) for kk, cwx in zip(kks, cwxs)]
    rds = [r * jnp.exp(cw) for r, cw in zip(rs, cws)]
    kends = [k * jnp.exp(e - cw) for k, e, cw in zip(ks, ends, cws)]
    bends = [b * jnp.exp(e - cw) for b, e, cw in zip(bs, ends, cws)]
    state_terms = [_split_rows(_mm_nt(_stack_rows([kkd, rd]), s0, P_STATE), 2) for kkd, rd, s0 in zip(kkds, rds, s0s)]
    w0s, y0s = [t[0] for t in state_terms], [t[1] for t in state_terms]
    chains = [(g, s) for g in range(ng) for s in range(nsub)]
    if scalar_decay:
        e0 = (lane == 0).astype(F32) * jnp.ones((c, 1), F32)
        rows = [_mm_nt(e0, cw, P_SUM) for cw in cws]
        dxs = [jnp.where(strict, jnp.exp(jnp.minimum(cwx[:, :c] - row, 0.0)), 0.0) for cwx, row in zip(cwxs, rows)]
        dis = [jnp.where(incl, jnp.exp(jnp.minimum(cw[:, :c] - row, 0.0)), 0.0) for cw, row in zip(cws, rows)]
        lefts = [_stack_rows([kk * m for m in masks] + [r * m for m in masks]) for kk, r in zip(kks, rs)]
        on_b = [_split_rows(_mm_nt(left, b, P_SCORE), 2 * nsub) for left, b in zip(lefts, bs)]
        on_k = [_split_rows(_mm_nt(left, k, P_SCORE), 2 * nsub) for left, k in zip(lefts, ks)]
        m_b = [on_b[g][s] * dxs[g] for g, s in chains]
        m_k = [on_k[g][s] * dxs[g] for g, s in chains]
        n_k = [on_k[g][nsub + s] * dis[g] for g, s in chains]
        n_b = [on_b[g][nsub + s] * dis[g] for g, s in chains]
    else:
        kds = [k * jnp.exp(-cw) for k, cw in zip(ks, cws)]
        bds = [b * jnp.exp(-cw) for b, cw in zip(bs, cws)]
        lefts = [_stack_rows([kkd * m for m in masks] + [rd * m for m in masks]) for kkd, rd in zip(kkds, rds)]
        on_b = [_split_rows(_mm_nt(left, bd, P_SCORE), 2 * nsub) for left, bd in zip(lefts, bds)]
        on_k = [_split_rows(_mm_nt(left, kd, P_SCORE), 2 * nsub) for left, kd in zip(lefts, kds)]
        m_b = [jnp.where(strict, on_b[g][s], 0.0) for g, s in chains]
        m_k = [jnp.where(strict, on_k[g][s], 0.0) for g, s in chains]
        n_k = [jnp.where(incl, on_k[g][nsub + s], 0.0) for g, s in chains]
        n_b = [jnp.where(incl, on_b[g][nsub + s], 0.0) for g, s in chains]
    t_inv = _tri_inverse(m_b) if inverses is None else _known_inverse(m_b, inverses)
    on_v = [_split_rows(_mm(_stack_rows([mk, nk]), vs[g], P_APPLY), 2) for (g, s), mk, nk in zip(chains, m_k, n_k)]
    sa_c = [_mm(t, w0s[g] + mv[0], P_APPLY) for (g, s), t, mv in zip(chains, t_inv, on_v)]
    y_c = [y0s[g] + mv[1] - _mm(nb, sa, P_APPLY) for (g, s), mv, nb, sa in zip(chains, on_v, n_b, sa_c)]
    per_group = lambda xs: [functools.reduce(lambda p, q: p + q, [xs[g * nsub + s] * masks[s] for s in range(nsub)])
                            for g in range(ng)]
    sas, ys = per_group(sa_c), per_group(y_c)
    s_ends = [s0 * jnp.exp(e) + _mm_tn(_stack_rows([v, -sa]), _stack_rows([kend, bend]), P_UPDATE)
              for s0, e, v, kend, sa, bend in zip(s0s, ends, vs, kends, sas, bends)]
    if nsub > 1:
        same_head = (_iota((LANES, LANES), 0) // hs) == (_iota((LANES, LANES), 1) // hs)
        s_ends = [jnp.where(same_head, s_end, 0.0) for s_end in s_ends]
    return list(zip(ys, s_ends)), t_inv


def _rec_fwd(name, r, lw, k, v, kk, b, *, seq, nsub, scalar_decay):
    n, w = r.shape
    ng = w // LANES
    nc = seq // CHUNK
    nb = n // seq
    nt = nb * ng * nsub

    def body(r_ref, lw_ref, k_ref, v_ref, kk_ref, b_ref, y_ref, s_ref, t_ref, state):
        @pl.when(pl.program_id(0) == 0)
        def _():
            state[...] = jnp.zeros_like(state)
        prims = [(state[bi * ng + g], r_ref[bi, :, _sl(g)], lw_ref[bi, :, _sl(g)], k_ref[bi, :, _sl(g)],
                  v_ref[bi, :, _sl(g)], kk_ref[bi, :, _sl(g)], b_ref[bi, :, _sl(g)])
                 for bi in range(nb) for g in range(ng)]
        outs, t_inv = _chunk_fwd(prims, nsub=nsub, scalar_decay=scalar_decay)
        for i, (y, s_end) in enumerate(outs):
            s_ref[0, i] = prims[i][0]
            y_ref[i // ng, :, _sl(i % ng)] = y
            state[i] = s_end
        for i, t in enumerate(t_inv):
            t_ref[0, i] = t

    row = pl.BlockSpec((nb, CHUNK, w), lambda c: (0, c, 0))
    seqs = lambda a: a.reshape(nb, seq, w)
    y, s_save, t_save = pl.pallas_call(
        body, name=name, grid=(nc,),
        in_specs=[row] * 6,
        out_specs=[row, pl.BlockSpec((1, nb * ng, LANES, LANES), lambda c: (c, 0, 0, 0)),
                   pl.BlockSpec((1, nt, CHUNK, CHUNK), lambda c: (c, 0, 0, 0))],
        out_shape=[jax.ShapeDtypeStruct((nb, seq, w), F32), jax.ShapeDtypeStruct((nc, nb * ng, LANES, LANES), F32),
                   jax.ShapeDtypeStruct((nc, nt, CHUNK, CHUNK), F32)],
        scratch_shapes=[pltpu.VMEM((nb * ng, LANES, LANES), F32)],
        compiler_params=pltpu.CompilerParams(dimension_semantics=("arbitrary",), vmem_limit_bytes=VMEM_LIMIT),
    )(seqs(r), seqs(lw), seqs(k), seqs(v), seqs(kk), seqs(b))
    return y.reshape(n, w), (s_save, t_save)


def _rec_bwd(name, r, lw, k, v, kk, b, saved, dy, *, seq, nsub, scalar_decay):
    n, w = r.shape
    ng = w // LANES
    nc = seq // CHUNK
    nb = n // seq
    nt = nb * ng * nsub
    s_save, t_save = saved

    def body(r_ref, lw_ref, k_ref, v_ref, kk_ref, b_ref, s_ref, t_ref, dy_ref,
             dr_ref, dlw_ref, dk_ref, dv_ref, dkk_ref, db_ref, dstate):
        @pl.when(pl.program_id(0) == 0)
        def _():
            dstate[...] = jnp.zeros_like(dstate)
        inverses = [t_ref[0, i] for i in range(nt)]
        f = lambda p: _chunk_fwd(p, nsub=nsub, scalar_decay=scalar_decay, inverses=inverses)[0]
        chains = [(bi, g) for bi in range(nb) for g in range(ng)]
        prims = [(s_ref[0, bi * ng + g], r_ref[bi, :, _sl(g)], lw_ref[bi, :, _sl(g)], k_ref[bi, :, _sl(g)],
                  v_ref[bi, :, _sl(g)], kk_ref[bi, :, _sl(g)], b_ref[bi, :, _sl(g)]) for bi, g in chains]
        _, vjp = jax.vjp(f, prims)
        (d_prims,) = vjp([(dy_ref[bi, :, _sl(g)], dstate[bi * ng + g]) for bi, g in chains])
        for (bi, g), (ds0, dr, dlw, dk, dv, dkk, db) in zip(chains, d_prims):
            dstate[bi * ng + g] = ds0
            dr_ref[bi, :, _sl(g)] = dr
            dlw_ref[bi, :, _sl(g)] = dlw
            dk_ref[bi, :, _sl(g)] = dk
            dv_ref[bi, :, _sl(g)] = dv
            dkk_ref[bi, :, _sl(g)] = dkk
            db_ref[bi, :, _sl(g)] = db

    row = pl.BlockSpec((nb, CHUNK, w), lambda c: (0, nc - 1 - c, 0))
    seqs = lambda a: a.reshape(nb, seq, w)
    grads = pl.pallas_call(
        body, name=name, grid=(nc,),
        in_specs=[row] * 6 + [pl.BlockSpec((1, nb * ng, LANES, LANES), lambda c: (nc - 1 - c, 0, 0, 0)),
                              pl.BlockSpec((1, nt, CHUNK, CHUNK), lambda c: (nc - 1 - c, 0, 0, 0)), row],
        out_specs=[row] * 6,
        out_shape=[jax.ShapeDtypeStruct((nb, seq, w), F32)] * 6,
        scratch_shapes=[pltpu.VMEM((nb * ng, LANES, LANES), F32)],
        compiler_params=pltpu.CompilerParams(dimension_semantics=("arbitrary",), vmem_limit_bytes=VMEM_LIMIT),
    )(seqs(r), seqs(lw), seqs(k), seqs(v), seqs(kk), seqs(b), s_save, t_save, seqs(dy))
    return [g.reshape(n, w) for g in grads]


def _shift_down(a, j, halo, is_start):
    tb = a.shape[0]
    rolled = pltpu.roll(a, j, 0)
    hr = jnp.where(is_start, 0.0, pltpu.roll(halo, j, 0))
    first = jnp.where(_iota((SUB, LANES), 0) < j, hr, rolled[0:SUB])
    if tb == SUB:
        return first
    return jnp.concatenate([first, rolled[SUB:]], axis=0)


def _shift_up(d, j, carry, is_end):
    tb = d.shape[0]
    up = pltpu.roll(d, tb - j, 0)
    cr = jnp.where(is_end, 0.0, pltpu.roll(carry, SUB - j, 0))
    last = jnp.where(_iota((SUB, LANES), 0) >= SUB - j, cr, up[tb - SUB:tb])
    if tb == SUB:
        return last
    return jnp.concatenate([up[:tb - SUB], last], axis=0)


def _ngroups(a):
    return a.shape[1] // LANES


def _pw_fwd(name, f, ins, shift, params, out_widths, out_dtypes, *, seq, tb):
    n = ins[0].shape[0]
    nt, tps = n // tb, seq // tb
    ni, npar = len(ins), len(params)

    def body(*refs):
        in_refs = refs[:ni]
        pos = ni
        halo_ref = None
        if shift:
            halo_ref = refs[pos]
            pos += 1
        p_refs = refs[pos:pos + npar]
        out_refs = refs[pos + npar:]
        is_start = (pl.program_id(0) % tps) == 0
        tiles = [[ref[:, _sl(g)] for g in range(_ngroups(ref))] for ref in in_refs]
        prevs = [[_shift_down(tiles[0][g], j, halo_ref[:, _sl(g)], is_start) for g in range(len(tiles[0]))]
                 for j in range(1, shift + 1)]
        pv = [[ref[:, _sl(g)] for g in range(_ngroups(ref))] for ref in p_refs]
        outs = f(tiles, prevs, pv)
        for o_ref, og in zip(out_refs, outs, strict=True):
            for g, t in enumerate(og):
                o_ref[:, _sl(g)] = t.astype(o_ref.dtype)

    in_specs = [pl.BlockSpec((tb, a.shape[1]), lambda i: (i, 0)) for a in ins]
    args = list(ins)
    if shift:
        in_specs.append(pl.BlockSpec((SUB, ins[0].shape[1]), lambda i: (jnp.maximum(i * (tb // SUB) - 1, 0), 0)))
        args.append(ins[0])
    in_specs += [pl.BlockSpec(p.shape, lambda i: (0, 0)) for p in params]
    args += list(params)
    return pl.pallas_call(
        body, name=name, grid=(nt,),
        in_specs=in_specs,
        out_specs=[pl.BlockSpec((tb, w), lambda i: (i, 0)) for w in out_widths],
        out_shape=[jax.ShapeDtypeStruct((n, w), dt) for w, dt in zip(out_widths, out_dtypes, strict=True)],
        compiler_params=pltpu.CompilerParams(dimension_semantics=("parallel",), vmem_limit_bytes=VMEM_LIMIT),
    )(*args)


def _pw_bwd(name, f, ins, shift, params, douts, din_dtypes, *, seq, tb):
    n = ins[0].shape[0]
    nt, tps = n // tb, seq // tb
    ni, npar = len(ins), len(params)
    flat_douts = [d for ds in douts for d in ds]
    nd = len(flat_douts)
    w0 = ins[0].shape[1]

    def body(*refs):
        in_refs = refs[:ni]
        pos = ni
        halo_ref = None
        if shift:
            halo_ref = refs[pos]
            pos += 1
        p_refs = refs[pos:pos + npar]
        pos += npar
        d_refs = refs[pos:pos + nd]
        pos += nd
        din_refs = refs[pos:pos + ni]
        pos += ni
        dp_refs = refs[pos:pos + npar]
        pos += npar
        carry = refs[pos] if shift else None
        step = pl.program_id(0)
        tile = nt - 1 - step
        is_start = (tile % tps) == 0
        is_end = (tile % tps) == tps - 1
        tiles = [[ref[:, _sl(g)] for g in range(_ngroups(ref))] for ref in in_refs]
        prevs = [[_shift_down(tiles[0][g], j, halo_ref[:, _sl(g)], is_start) for g in range(len(tiles[0]))]
                 for j in range(1, shift + 1)]
        pv = [[ref[:, _sl(g)] for g in range(_ngroups(ref))] for ref in p_refs]
        cot, pos_d = [], 0
        for ds in douts:
            grp = d_refs[pos_d:pos_d + len(ds)]
            pos_d += len(ds)
            cot.append([functools.reduce(lambda p, q: p + q, [ref[:, _sl(g)].astype(F32) for ref in grp])
                        for g in range(_ngroups(grp[0]))])
        _, vjp = jax.vjp(f, tiles, prevs, pv)
        d_tiles, d_prevs, d_pv = vjp(cot)
        for g in range(len(tiles[0])):
            for j in range(1, shift + 1):
                d_tiles[0][g] = d_tiles[0][g] + _shift_up(d_prevs[j - 1][g], j, carry[j - 1, :, _sl(g)], is_end)
            for j in range(1, shift + 1):
                carry[j - 1, :, _sl(g)] = d_prevs[j - 1][g][0:SUB]
        for ref, dg in zip(din_refs, d_tiles, strict=True):
            for g, t in enumerate(dg):
                ref[:, _sl(g)] = t.astype(ref.dtype)

        @pl.when(step == 0)
        def _():
            for ref in dp_refs:
                ref[...] = jnp.zeros_like(ref)
        for ref, dg in zip(dp_refs, d_pv, strict=True):
            for g, t in enumerate(dg):
                ref[:, _sl(g)] += t

    rev = lambda i: (nt - 1 - i, 0)
    in_specs = [pl.BlockSpec((tb, a.shape[1]), rev) for a in ins]
    args = list(ins)
    if shift:
        in_specs.append(pl.BlockSpec((SUB, w0), lambda i: (jnp.maximum((nt - 1 - i) * (tb // SUB) - 1, 0), 0)))
        args.append(ins[0])
    in_specs += [pl.BlockSpec(p.shape, lambda i: (0, 0)) for p in params]
    args += list(params)
    in_specs += [pl.BlockSpec((tb, d.shape[1]), rev) for d in flat_douts]
    args += flat_douts
    out_specs = [pl.BlockSpec((tb, a.shape[1]), rev) for a in ins] + [pl.BlockSpec(p.shape, lambda i: (0, 0)) for p in params]
    out_shape = ([jax.ShapeDtypeStruct(a.shape, dt) for a, dt in zip(ins, din_dtypes, strict=True)]
                 + [jax.ShapeDtypeStruct(p.shape, F32) for p in params])
    res = pl.pallas_call(
        body, name=name, grid=(nt,),
        in_specs=in_specs, out_specs=out_specs, out_shape=out_shape,
        scratch_shapes=[pltpu.VMEM((shift, SUB, w0), F32)] if shift else [],
        compiler_params=pltpu.CompilerParams(dimension_semantics=("arbitrary",), vmem_limit_bytes=VMEM_LIMIT),
    )(*args)
    return res[:ni], res[ni:]


def _rwkv_prep_f(tiles, prevs, params):
    (p,), (prev,) = tiles, prevs
    mu, w0, w2p, a0, a2p, k_k, k_a = params
    xs = [p[g] + (prev[g] - p[g]) * mu[g] for g in range(13)]
    wdad = xs[12]
    tw = jnp.tanh(wdad)
    e64 = _seg_ones(64)
    r, lw, k2, v, kk, b = [], [], [], [], [], []
    for g in range(4):
        k_g = xs[4 + g]
        lo = w0[g] + _mm(tw, w2p[g], P_POINT)
        lw_g = -jnp.exp(-_softplus(-lo) - 0.5)
        a_g = _sigmoid(a0[g] + _mm(wdad, a2p[g], P_POINT))
        kkp = k_g * k_k[g]
        kk_g = kkp * lax.rsqrt(_mm(kkp * kkp, e64, P_POINT) + 1e-12)
        r.append(xs[g])
        lw.append(lw_g)
        k2.append(k_g * (1.0 + (a_g - 1.0) * k_a[g]))
        v.append(xs[8 + g])
        kk.append(kk_g)
        b.append(kk_g * a_g)
    return [r, lw, k2, v, kk, b]


def _rwkv_post_f(tiles, prevs, params):
    yrec, r, k2, v, z = tiles
    gn_w, gn_b, r_k = params
    e64 = _seg_ones(64)
    out = []
    for g in range(4):
        mean = _mm(yrec[g], e64, P_POINT) * (1.0 / 64)
        d = yrec[g] - mean
        var = _mm(d * d, e64, P_POINT) * (1.0 / 64)
        yn = d * lax.rsqrt(var + RW_GN_EPS) * gn_w[g] + gn_b[g]
        bonus = _mm(r[g] * k2[g] * r_k[g], e64, P_POINT) * v[g]
        out.append((yn + bonus) * _silu(z[g]))
    return [out]


def _gdn_prep_f(tiles, prevs, params):
    x, (ba,) = tiles
    p1, p2, p3 = prevs
    cw0, cw1, cw2, cw3, a_log, dt_bias = params
    s = [_silu(cw3[g] * x[g] + cw2[g] * p1[g] + cw1[g] * p2[g] + cw0[g] * p3[g]) for g in range(12)]
    row = _iota((LANES, LANES), 0)
    r, lw, k, vv, b = [], [], [], [], []
    for h in range(4):
        q_h, k_h, v_h = s[h], s[4 + h], s[8 + h]
        qn = q_h * lax.rsqrt(jnp.sum(q_h * q_h, axis=-1, keepdims=True) + 1e-12)
        kn = k_h * lax.rsqrt(jnp.sum(k_h * k_h, axis=-1, keepdims=True) + 1e-12)
        beta = _sigmoid(_mm(ba, (row == h).astype(F32)))
        alpha = _mm(ba, (row == 4 + h).astype(F32))
        g_h = -jnp.exp(a_log[h]) * _softplus(alpha + dt_bias[h])
        r.append(qn * (LANES ** -0.5))
        lw.append(g_h)
        k.append(kn)
        vv.append(beta * v_h)
        b.append(jnp.exp(g_h) * beta * kn)
    return [r, lw, k, vv, b]


def _gdn_post_f(tiles, prevs, params):
    o, z = tiles
    ((onw,),) = params
    out = []
    for h in range(4):
        ms = jnp.mean(o[h] * o[h], axis=-1, keepdims=True)
        out.append(o[h] * lax.rsqrt(ms + NORM_EPS) * onw * _silu(z[h]))
    return [out]


def _norm_in(x2, g_in, *, tm):
    n = x2.shape[0]

    def body(x_ref, g_ref, h_ref):
        x = x_ref[...]
        rs = lax.rsqrt(jnp.mean(x * x, axis=-1, keepdims=True) + NORM_EPS)
        h_ref[...] = (x * rs * g_ref[...]).astype(BF16)

    return pl.pallas_call(
        body, name="norm_in", grid=(n // tm,),
        in_specs=[pl.BlockSpec((tm, D_MODEL), lambda i: (i, 0)), pl.BlockSpec((1, D_MODEL), lambda i: (0, 0))],
        out_specs=pl.BlockSpec((tm, D_MODEL), lambda i: (i, 0)),
        out_shape=jax.ShapeDtypeStruct((n, D_MODEL), BF16),
        compiler_params=pltpu.CompilerParams(dimension_semantics=("parallel",), vmem_limit_bytes=VMEM_LIMIT),
    )(x2, g_in)


def _proj(name, h, wt, *, tm):
    n, ws = h.shape[0], wt.shape[0]

    def body(h_ref, w_ref, o_ref):
        o_ref[...] = lax.dot_general(h_ref[...], w_ref[...], (_NT, ((), ())), preferred_element_type=F32)

    return pl.pallas_call(
        body, name=name, grid=(n // tm,),
        in_specs=[pl.BlockSpec((tm, D_MODEL), lambda i: (i, 0)), pl.BlockSpec((ws, D_MODEL), lambda i: (0, 0))],
        out_specs=pl.BlockSpec((tm, ws), lambda i: (i, 0)),
        out_shape=jax.ShapeDtypeStruct((n, ws), F32),
        compiler_params=pltpu.CompilerParams(dimension_semantics=("parallel",), vmem_limit_bytes=VMEM_LIMIT),
    )(h, wt)


def _proj_dw(name, h, dp, *, tm):
    n, ws = dp.shape

    def body(h_ref, d_ref, o_ref):
        @pl.when(pl.program_id(0) == 0)
        def _():
            o_ref[...] = jnp.zeros_like(o_ref)
        o_ref[...] += lax.dot_general(d_ref[...], h_ref[...], (_TN, ((), ())), preferred_element_type=F32)

    return pl.pallas_call(
        body, name=name, grid=(n // tm,),
        in_specs=[pl.BlockSpec((tm, D_MODEL), lambda i: (i, 0)), pl.BlockSpec((tm, ws), lambda i: (i, 0))],
        out_specs=pl.BlockSpec((ws, D_MODEL), lambda i: (0, 0)),
        out_shape=jax.ShapeDtypeStruct((ws, D_MODEL), F32),
        compiler_params=pltpu.CompilerParams(dimension_semantics=("arbitrary",), vmem_limit_bytes=VMEM_LIMIT),
    )(h, dp)


def _proj_dx(x2, g_in, d_xo, dps, ws, *, tm):
    n = x2.shape[0]
    ns = len(dps)

    def body(*refs):
        x_ref, g_ref, dxo_ref = refs[:3]
        dp_refs = refs[3:3 + ns]
        w_refs = refs[3 + ns:3 + 2 * ns]
        dx_ref, dg_ref = refs[3 + 2 * ns:]
        dh = jnp.zeros((tm, D_MODEL), F32)
        for d_ref, w_ref in zip(dp_refs, w_refs, strict=True):
            dh = dh + jnp.dot(d_ref[...], w_ref[...], preferred_element_type=F32)
        x = x_ref[...]
        rs = lax.rsqrt(jnp.mean(x * x, axis=-1, keepdims=True) + NORM_EPS)
        xn = x * rs
        dxn = dh * g_ref[...]
        dx_ref[...] = dxo_ref[...] + rs * (dxn - xn * jnp.mean(dxn * xn, axis=-1, keepdims=True))

        @pl.when(pl.program_id(0) == 0)
        def _():
            dg_ref[...] = jnp.zeros_like(dg_ref)
        dg_ref[...] += jnp.sum(dh * xn, axis=0, keepdims=True)

    row = pl.BlockSpec((tm, D_MODEL), lambda i: (i, 0))
    return pl.pallas_call(
        body, name="proj_dx", grid=(n // tm,),
        in_specs=([row, pl.BlockSpec((1, D_MODEL), lambda i: (0, 0)), row]
                  + [pl.BlockSpec((tm, d.shape[1]), lambda i: (i, 0)) for d in dps]
                  + [pl.BlockSpec(w.shape, lambda i: (0, 0)) for w in ws]),
        out_specs=[row, pl.BlockSpec((1, D_MODEL), lambda i: (0, 0))],
        out_shape=[jax.ShapeDtypeStruct((n, D_MODEL), F32), jax.ShapeDtypeStruct((1, D_MODEL), F32)],
        compiler_params=pltpu.CompilerParams(dimension_semantics=("arbitrary",), vmem_limit_bytes=VMEM_LIMIT),
    )(x2, g_in, d_xo, *dps, *ws)


def _tail(x2, tgt2, gates, ya, yb, w_a, w_b, w_o, now, *, tr):
    n = x2.shape[0]

    def body(x_ref, t_ref, g_ref, ya_ref, yb_ref, wa_ref, wb_ref, wo_ref, now_ref,
             dya_ref, dyb_ref, dg_ref, dxo_ref, dwa_ref, dwb_ref, dwo_ref, dnow_ref, loss_ref):
        ya16, yb16 = ya_ref[...].astype(BF16), yb_ref[...].astype(BF16)
        ua = jnp.dot(ya16, wa_ref[...], preferred_element_type=F32)
        ub = jnp.dot(yb16, wb_ref[...], preferred_element_type=F32)
        ga = _sigmoid(g_ref[:, :D_MODEL])
        gb = _sigmoid(g_ref[:, D_MODEL:])
        m16 = (ga * ua + gb * ub).astype(BF16)
        xo = x_ref[...] + jnp.dot(m16, wo_ref[...], preferred_element_type=F32)
        rs = lax.rsqrt(jnp.mean(xo * xo, axis=-1, keepdims=True) + NORM_EPS)
        yn = xo * rs
        now_v = now_ref[...]
        err = yn * now_v - t_ref[...]
        dy = err * (1.0 / D_MODEL)
        dyn = dy * now_v
        dxo = rs * (dyn - yn * jnp.mean(dyn * yn, axis=-1, keepdims=True))
        dxo_ref[...] = dxo
        dxo16 = dxo.astype(BF16)
        dm = lax.dot_general(dxo16, wo_ref[...], (((1,), (1,)), ((), ())), preferred_element_type=F32)
        dua16 = (dm * ga).astype(BF16)
        dub16 = (dm * gb).astype(BF16)
        dg_ref[:, :D_MODEL] = (dm * ua * ga * (1.0 - ga)).astype(dg_ref.dtype)
        dg_ref[:, D_MODEL:] = (dm * ub * gb * (1.0 - gb)).astype(dg_ref.dtype)
        dya_ref[...] = lax.dot_general(dua16, wa_ref[...], (((1,), (1,)), ((), ())), preferred_element_type=F32)
        dyb_ref[...] = lax.dot_general(dub16, wb_ref[...], (((1,), (1,)), ((), ())), preferred_element_type=F32)

        @pl.when(pl.program_id(0) == 0)
        def _():
            for ref in (dwa_ref, dwb_ref, dwo_ref, dnow_ref, loss_ref):
                ref[...] = jnp.zeros_like(ref)
        tn = (((0,), (0,)), ((), ()))
        dwo_ref[...] += lax.dot_general(m16, dxo16, tn, preferred_element_type=F32)
        dwa_ref[...] += lax.dot_general(ya16, dua16, tn, preferred_element_type=F32)
        dwb_ref[...] += lax.dot_general(yb16, dub16, tn, preferred_element_type=F32)
        dnow_ref[...] += jnp.sum(dy * yn, axis=0, keepdims=True)
        loss_ref[...] += (0.5 / D_MODEL) * jnp.sum(err * err)

    row = lambda w: pl.BlockSpec((tr, w), lambda i: (i, 0))
    full = lambda a: pl.BlockSpec(a.shape, lambda i: (0, 0))
    return pl.pallas_call(
        body, name="tail", grid=(n // tr,),
        in_specs=[row(D_MODEL), row(D_MODEL), row(2 * D_MODEL), row(RW_W), row(GD_W), full(w_a), full(w_b), full(w_o), full(now)],
        out_specs=[row(RW_W), row(GD_W), row(2 * D_MODEL), row(D_MODEL),
                   pl.BlockSpec((RW_W, D_MODEL), lambda i: (0, 0)), pl.BlockSpec((GD_W, D_MODEL), lambda i: (0, 0)),
                   pl.BlockSpec((D_MODEL, D_MODEL), lambda i: (0, 0)), pl.BlockSpec((1, D_MODEL), lambda i: (0, 0)),
                   pl.BlockSpec((SUB, LANES), lambda i: (0, 0))],
        out_shape=[jax.ShapeDtypeStruct((n, RW_W), F32), jax.ShapeDtypeStruct((n, GD_W), F32),
                   jax.ShapeDtypeStruct((n, 2 * D_MODEL), BF16), jax.ShapeDtypeStruct((n, D_MODEL), F32),
                   jax.ShapeDtypeStruct((RW_W, D_MODEL), F32), jax.ShapeDtypeStruct((GD_W, D_MODEL), F32),
                   jax.ShapeDtypeStruct((D_MODEL, D_MODEL), F32), jax.ShapeDtypeStruct((1, D_MODEL), F32),
                   jax.ShapeDtypeStruct((SUB, LANES), F32)],
        compiler_params=pltpu.CompilerParams(dimension_semantics=("arbitrary",), vmem_limit_bytes=VMEM_LIMIT),
    )(x2, tgt2, gates, ya, yb, w_a, w_b, w_o, now)


def _exchange(name, axes, scatter, gather, place_own=True):
    ns, ng = len(scatter), len(gather)
    na = ns + ng
    gs = 2 ** len(axes)
    arrs = list(scatter) + list(gather)

    def body(*refs):
        src = refs[:na]
        dst = refs[na:2 * na]
        send_sems, recv_sems = refs[2 * na:]
        mine = {ax: lax.axis_index(ax) for ax in ("x", "y", "c")}

        def peer(k):
            co = dict(mine)
            for i, ax in enumerate(axes):
                if (k >> (len(axes) - 1 - i)) & 1:
                    co[ax] = 1 - co[ax]
            idx = 0
            for ax in axes:
                idx = 2 * idx + co[ax]
            return (co["x"], co["y"], co["c"]), idx

        _, me = peer(0)

        def copy(a, k, landing):
            dev, idx = peer(k)
            s = src[a].at[idx] if a < ns else src[a]
            return pltpu.make_async_remote_copy(src_ref=s, dst_ref=dst[a].at[idx if landing else me],
                                                send_sem=send_sems.at[a, k - 1], recv_sem=recv_sems.at[a, k - 1],
                                                device_id=dev, device_id_type=pl.DeviceIdType.MESH)

        sends = [copy(a, k, False) for a in range(na) for k in range(1, gs)]
        for cp in sends:
            cp.start()
        for a in range(na):
            for k in range(1, gs):
                copy(a, k, True).wait_recv()
        for cp in sends:
            cp.wait_send()

    out_shape = [jax.ShapeDtypeStruct(a.shape, a.dtype) for a in scatter] + \
                [jax.ShapeDtypeStruct((gs,) + a.shape, a.dtype) for a in gather]
    anyspec = pl.BlockSpec(memory_space=pl.ANY)
    lands = pl.pallas_call(
        body, name=name,
        in_specs=[anyspec] * na, out_specs=[anyspec] * na, out_shape=out_shape,
        scratch_shapes=[pltpu.SemaphoreType.DMA((na, gs - 1)), pltpu.SemaphoreType.DMA((na, gs - 1))],
    )(*arrs)
    if not place_own:
        return lands
    me = 0
    for ax in axes:
        me = 2 * me + lax.axis_index(ax)
    kept = [lax.dynamic_index_in_dim(a, me, 0, keepdims=False) for a in scatter] + list(gather)
    return [lax.dynamic_update_index_in_dim(land, mine, me, 0) for land, mine in zip(lands, kept)]


def _gather_all(name, arrs):
    na = len(arrs)

    def body(*refs):
        src = refs[:na]
        dst = refs[na:2 * na]
        send_sems, recv_sems = refs[2 * na:]
        x, y, c = lax.axis_index("x"), lax.axis_index("y"), lax.axis_index("c")
        sibling = (x, y, 1 - c)
        chips = [(1 - x, y), (x, 1 - y), (1 - x, 1 - y)]

        def copy(a, k, block, to, own=False):
            px, py, pc = block
            slot = dst[a].at[pc, 2 * px + py]
            return pltpu.make_async_remote_copy(src_ref=src[a] if own else slot, dst_ref=slot,
                                                send_sem=send_sems.at[a, k], recv_sem=recv_sems.at[a, k],
                                                device_id=to, device_id_type=pl.DeviceIdType.MESH)

        first = [copy(a, 0, (x, y, c), sibling, own=True) for a in range(na)]
        first += [copy(a, 1 + j, (x, y, c), (*chip, c), own=True) for j, chip in enumerate(chips) for a in range(na)]
        for cp in first:
            cp.start()
        passed = []
        for j, chip in enumerate(chips):
            for a in range(na):
                copy(a, 1 + j, (*chip, c), (x, y, c)).wait_recv()
                passed.append(copy(a, 4 + j, (*chip, c), sibling))
                passed[-1].start()
        for a in range(na):
            copy(a, 0, (x, y, 1 - c), (x, y, c)).wait_recv()
            for j, chip in enumerate(chips):
                copy(a, 4 + j, (*chip, 1 - c), (x, y, c)).wait_recv()
        for cp in first + passed:
            cp.wait_send()

    anyspec = pl.BlockSpec(memory_space=pl.ANY)
    lands = pl.pallas_call(
        body, name=name,
        in_specs=[anyspec] * na, out_specs=[anyspec] * na,
        out_shape=[jax.ShapeDtypeStruct((2, 4) + a.shape, a.dtype) for a in arrs],
        scratch_shapes=[pltpu.SemaphoreType.DMA((na, 7)), pltpu.SemaphoreType.DMA((na, 7))],
    )(*arrs)
    core, chip = lax.axis_index("c"), 2 * lax.axis_index("x") + lax.axis_index("y")
    zero = jnp.zeros((), jnp.int32)
    return [lax.dynamic_update_slice(land, mine[None, None], (core, chip) + (zero,) * mine.ndim)
            for land, mine in zip(lands, arrs)]


def _pair_sum(name, own, land, out_dtype):
    _, nq, r, c = own.shape
    core = lax.axis_index("c").astype(jnp.int32).reshape(1)

    def body(core_ref, own_ref, land_ref, o_ref):
        o_ref[0] = (own_ref[0, 0] + land_ref[0, 0]).astype(o_ref.dtype)

    return pl.pallas_call(
        body, name=name,
        grid_spec=pltpu.PrefetchScalarGridSpec(
            num_scalar_prefetch=1, grid=(nq,),
            in_specs=[pl.BlockSpec((1, 1, r, c), lambda i, core_ref: (core_ref[0], i, 0, 0)),
                      pl.BlockSpec((1, 1, r, c), lambda i, core_ref: (1 - core_ref[0], i, 0, 0))],
            out_specs=pl.BlockSpec((1, r, c), lambda i, core_ref: (i, 0, 0))),
        out_shape=jax.ShapeDtypeStruct((nq, r, c), out_dtype),
        compiler_params=pltpu.CompilerParams(dimension_semantics=("parallel",), vmem_limit_bytes=VMEM_LIMIT),
    )(core, own, land)


def _adam(name, land, w, m, v):
    r, c = w.shape
    nslot = land.shape[0]
    tr = 256 if (r % 256 == 0 and r > 256) else r
    tc = 256 if (tr == r and r > 256 and c % 256 == 0) else c

    def body(l_ref, w_ref, m_ref, v_ref, g_out, d_out, m_out, v_out):
        g = l_ref[0].astype(F32)
        for s in range(1, nslot):
            g = g + l_ref[s].astype(F32)
        g_out[...] = g
        d_out[...], m_out[...], v_out[...] = _adam_math(g, w_ref[...], m_ref[...], v_ref[...])

    blk = pl.BlockSpec((tr, tc), lambda i: (i * tc // c, i % (c // tc)))
    return pl.pallas_call(
        body, name=name, grid=((r // tr) * (c // tc),),
        in_specs=[pl.BlockSpec((nslot, tr, tc), lambda i: (0, i * tc // c, i % (c // tc))), blk, blk, blk],
        out_specs=[blk] * 4,
        out_shape=[jax.ShapeDtypeStruct((r, c), F32)] * 4,
        compiler_params=pltpu.CompilerParams(dimension_semantics=("parallel",), vmem_limit_bytes=VMEM_LIMIT),
    )(land, w, m, v)


def _adam_math(g, w, m, v):
    c1 = 1.0 / (1.0 - ADAM_B1 ** ADAM_STEP)
    c2 = 1.0 / (1.0 - ADAM_B2 ** ADAM_STEP)
    m_new = ADAM_B1 * m + (1.0 - ADAM_B1) * g
    v_new = ADAM_B2 * v + (1.0 - ADAM_B2) * (g * g)
    return -ADAM_LR * ((m_new * c1) / (jnp.sqrt(v_new * c2) + ADAM_EPS) + ADAM_WD * w), m_new, v_new


def _adam_small(land, ws, ms, vs):
    npar = len(ws)
    nslot = land.shape[0]

    def body(*refs):
        l_ref = refs[0]
        w_refs, m_refs, v_refs = refs[1:1 + npar], refs[1 + npar:1 + 2 * npar], refs[1 + 2 * npar:1 + 3 * npar]
        outs = refs[1 + 3 * npar:1 + 7 * npar]
        loss_ref, g_rows = refs[1 + 7 * npar], refs[2 + 7 * npar]
        g = l_ref[0]
        for s in range(1, nslot):
            g = g + l_ref[s]
        g_rows[...] = g
        row = 0
        for i, (_, size) in enumerate(_SMALL):
            for j in range(-(-size // LANES)):
                width = min(LANES, size - j * LANES)
                cols = slice(j * LANES, j * LANES + width)
                g_ij = g_rows[row:row + 1, 0:width]
                delta, m_new, v_new = _adam_math(g_ij, w_refs[i][:, cols], m_refs[i][:, cols], v_refs[i][:, cols])
                for ref, val in zip(outs[4 * i:4 * i + 4], (g_ij, delta, m_new, v_new)):
                    ref[:, cols] = val
                row += 1
        loss_ref[...] = g_rows[row:row + 1, :]

    full = lambda a: pl.BlockSpec(a.shape, lambda: (0,) * a.ndim)
    res = pl.pallas_call(
        body, name="adam_small",
        in_specs=[full(land)] + [full(a) for a in list(ws) + list(ms) + list(vs)],
        out_specs=[full(w) for w in ws for _ in range(4)] + [pl.BlockSpec((1, LANES), lambda: (0, 0))],
        out_shape=[jax.ShapeDtypeStruct(w.shape, F32) for w in ws for _ in range(4)] + [jax.ShapeDtypeStruct((1, LANES), F32)],
        scratch_shapes=[pltpu.VMEM(land.shape[1:], F32)],
    )(land, *ws, *ms, *vs)
    return [res[4 * i:4 * i + 4] for i in range(npar)], res[4 * npar]


_SMALL = (("norm_in_w", 1024), ("rw_mu", 1664), ("rw_w0", 512), ("rw_a0", 512), ("rw_k_k", 512), ("rw_k_a", 512),
          ("rw_r_k", 512), ("rw_gn_w", 512), ("rw_gn_b", 512), ("gd_A_log", 4), ("gd_dt_bias", 4), ("gd_o_norm_w", 128),
          ("norm_out_w", 1024))
_SMALL_ROWS = 64


def _pack_small(vals, loss_row):
    rows = []
    for (_, size), a in zip(_SMALL, vals, strict=True):
        flat = a.reshape(-1).astype(F32)
        pad = (-size) % LANES
        if pad:
            flat = jnp.concatenate([flat, jnp.zeros((pad,), F32)])
        rows.append(flat.reshape(-1, LANES))
    rows.append(loss_row)
    used = sum(r.shape[0] for r in rows)
    rows.append(jnp.zeros((_SMALL_ROWS - used, LANES), F32))
    return jnp.concatenate(rows, axis=0)


def kernel(x, norm_in_w, w_in, rw_mu, rw_w0, rw_w2, rw_a0, rw_a2, rw_k_k, rw_k_a, rw_r_k, rw_gn_w, rw_gn_b, gd_conv_w, gd_A_log, gd_dt_bias, gd_o_norm_w, w_branch_a, w_branch_b, w_out, norm_out_w, loss_target, m_norm_in_w, m_w_in, m_rw_mu, m_rw_w0, m_rw_w2, m_rw_a0, m_rw_a2, m_rw_k_k, m_rw_k_a, m_rw_r_k, m_rw_gn_w, m_rw_gn_b, m_gd_conv_w, m_gd_A_log, m_gd_dt_bias, m_gd_o_norm_w, m_w_branch_a, m_w_branch_b, m_w_out, m_norm_out_w, v_norm_in_w, v_w_in, v_rw_mu, v_rw_w0, v_rw_w2, v_rw_a0, v_rw_a2, v_rw_k_k, v_rw_k_a, v_rw_r_k, v_rw_gn_w, v_rw_gn_b, v_gd_conv_w, v_gd_A_log, v_gd_dt_bias, v_gd_o_norm_w, v_w_branch_a, v_w_branch_b, v_w_out, v_norm_out_w):
    nb, seq, _ = x.shape
    n = nb * seq
    tm = min(512, n)
    tb = min(512, seq)
    x2 = x.reshape(n, D_MODEL)
    tgt2 = loss_target.reshape(n, D_MODEL)
    cols = w_in.shape[2]
    in_cols = cols * N_DEV

    wt_own, mt_own, vt_own = w_in[0].T, m_w_in[0].T, v_w_in[0].T
    sharded = [wt_own.astype(BF16), rw_w2[0], rw_a2[0], gd_conv_w[0], w_branch_a[0].astype(BF16),
               w_branch_b[0].astype(BF16), w_out[0].astype(BF16)]
    g_win, g_w2, g_a2, g_conv, g_wa, g_wb, g_wo = _gather_all("gather_weights", sharded)
    unshard_rows = lambda a: jnp.transpose(a, (1, 0, 2, 3)).reshape(N_DEV * a.shape[2], a.shape[3])
    unshard_cols = lambda a: jnp.transpose(a, (2, 1, 0, 3)).reshape(a.shape[2], N_DEV * a.shape[3])
    wt_full = unshard_rows(g_win)
    seg_bounds = ((0, 1664), (1664, 2176), (2176, 3712), (3712, 4224), (4232, in_cols))
    w_rw, w_zrw, w_qkv, w_zgd, w_gates = [wt_full[a:b] for a, b in seg_bounds]
    w_ba = jnp.concatenate([wt_full[4224:4232], jnp.zeros((LANES - 8, D_MODEL), BF16)], axis=0)
    w2_full, a2_full = unshard_cols(g_w2), unshard_cols(g_a2)
    zeros64 = jnp.zeros((64, RW_W), F32)
    w2p = jnp.concatenate([w2_full, zeros64], axis=0)
    a2p = jnp.concatenate([zeros64, a2_full], axis=0)
    conv_full = unshard_cols(g_conv)
    conv_rows = [conv_full[i:i + 1] for i in range(4)]
    wa_full = unshard_cols(g_wa)
    wb_full = unshard_cols(g_wb)
    wo_full = unshard_rows(g_wo)
    a_log_bc = jnp.repeat(gd_A_log, LANES, axis=1)
    dt_bias_bc = jnp.repeat(gd_dt_bias, LANES, axis=1)
    r_k_flat = rw_r_k.reshape(1, RW_W)
    now2 = norm_out_w.reshape(1, D_MODEL)

    h = _norm_in(x2, norm_in_w, tm=tm)
    p_rw = _proj("proj_rw", h, w_rw, tm=tm)
    p_zrw = _proj("proj_zrw", h, w_zrw, tm=tm)
    p_qkv = _proj("proj_qkv", h, w_qkv, tm=tm)
    p_zgd = _proj("proj_zgd", h, w_zgd, tm=tm)
    p_ba = _proj("proj_ba", h, w_ba, tm=tm)
    p_gates = _proj("proj_gates", h, w_gates, tm=tm)

    rw_params = [rw_mu, rw_w0, w2p, rw_a0, a2p, rw_k_k, rw_k_a]
    r_a, lw_a, k_a, v_a, kk_a, b_a = _pw_fwd("rwkv_prep", _rwkv_prep_f, [p_rw], 1, rw_params, [RW_W] * 6, [F32] * 6,
                                             seq=seq, tb=tb)
    y_rec, s_a = _rec_fwd("rwkv_rec", r_a, lw_a, k_a, v_a, kk_a, b_a, seq=seq, nsub=2, scalar_decay=False)
    post_params = [rw_gn_w, rw_gn_b, r_k_flat]
    (y_a,) = _pw_fwd("rwkv_post", _rwkv_post_f, [y_rec, r_a, k_a, v_a, p_zrw], 0, post_params, [RW_W], [F32], seq=seq, tb=tb)

    gd_params = conv_rows + [a_log_bc, dt_bias_bc]
    r_b, lw_b, k_b, v_b, b_b = _pw_fwd("gdn_prep", _gdn_prep_f, [p_qkv, p_ba], 3, gd_params, [GD_W] * 5, [F32] * 5,
                                       seq=seq, tb=tb)
    o_rec, s_b = _rec_fwd("gdn_rec", r_b, lw_b, k_b, v_b, k_b, b_b, seq=seq, nsub=1, scalar_decay=True)
    (y_b,) = _pw_fwd("gdn_post", _gdn_post_f, [o_rec, p_zgd], 0, [gd_o_norm_w], [GD_W], [F32], seq=seq, tb=tb)

    d_ya, d_yb, d_gates, d_xo, dwa, dwb, dwo, d_now, loss_acc = _tail(
        x2, tgt2, p_gates, y_a, y_b, wa_full, wb_full, wo_full, now2, tr=min(256, n))

    (d_o, d_zgd), (d_onw,) = _pw_bwd("gdn_post_bwd", _gdn_post_f, [o_rec, p_zgd], 0, [gd_o_norm_w], [[d_yb]],
                                     [F32, BF16], seq=seq, tb=tb)
    dr_b, dlw_b, dk_b, dv_b, dkk_b, db_b = _rec_bwd("gdn_rec_bwd", r_b, lw_b, k_b, v_b, k_b, b_b, s_b, d_o,
                                                    seq=seq, nsub=1, scalar_decay=True)
    (d_qkv, d_ba), d_gd_params = _pw_bwd("gdn_prep_bwd", _gdn_prep_f, [p_qkv, p_ba], 3, gd_params,
                                         [[dr_b], [dlw_b], [dk_b, dkk_b], [dv_b], [db_b]], [BF16, BF16], seq=seq, tb=tb)

    (d_yrec, dr_p, dk_p, dv_p, d_zrw), d_post_params = _pw_bwd(
        "rwkv_post_bwd", _rwkv_post_f, [y_rec, r_a, k_a, v_a, p_zrw], 0, post_params, [[d_ya]],
        [F32, F32, F32, F32, BF16], seq=seq, tb=tb)
    dr_a, dlw_a, dk_a, dv_a, dkk_a, db_a = _rec_bwd("rwkv_rec_bwd", r_a, lw_a, k_a, v_a, kk_a, b_a, s_a, d_yrec,
                                                    seq=seq, nsub=2, scalar_decay=False)
    (d_prw,), d_rw_params = _pw_bwd("rwkv_prep_bwd", _rwkv_prep_f, [p_rw], 1, rw_params,
                                    [[dr_a, dr_p], [dlw_a], [dk_a, dk_p], [dv_a, dv_p], [dkk_a], [db_a]], [BF16],
                                    seq=seq, tb=tb)

    dps = [d_prw, d_zrw, d_qkv, d_zgd, d_ba, d_gates]
    wsegs = [w_rw, w_zrw, w_qkv, w_zgd, w_ba, w_gates]
    dx2, d_gin = _proj_dx(x2, norm_in_w, d_xo, dps, wsegs, tm=min(256, n))
    dw_rw = _proj_dw("dw_rw", h, d_prw, tm=tm)
    dw_zrw = _proj_dw("dw_zrw", h, d_zrw, tm=tm)
    dw_qkv = _proj_dw("dw_qkv", h, d_qkv, tm=tm)
    dw_zgd = _proj_dw("dw_zgd", h, d_zgd, tm=tm)
    dw_ba = _proj_dw("dw_ba", h, d_ba, tm=tm)
    dw_gates = _proj_dw("dw_gates", h, d_gates, tm=tm)
    dwt_in_full = jnp.concatenate([dw_rw, dw_zrw, dw_qkv, dw_zgd, dw_ba[:8], dw_gates], axis=0)

    shard_cols = lambda a: jnp.transpose(a.reshape(a.shape[0], 4, 2, a.shape[1] // N_DEV), (2, 1, 0, 3))
    shard_rows = lambda a: jnp.transpose(a.reshape(4, 2, a.shape[0] // N_DEV, a.shape[1]), (1, 0, 2, 3))
    d_mu, d_w0, d_w2p, d_a0, d_a2p, d_kk_, d_ka_ = d_rw_params
    d_gnw, d_gnb, d_rk = d_post_params
    d_conv = jnp.concatenate(d_gd_params[:4], axis=0)
    d_alog = d_gd_params[4].reshape(4, LANES).sum(axis=1).reshape(1, 4)
    d_dtb = d_gd_params[5].reshape(4, LANES).sum(axis=1).reshape(1, 4)
    scat = [shard_rows(dwt_in_full), shard_cols(d_w2p[:64]), shard_cols(d_a2p[64:]), shard_cols(d_conv),
            shard_cols(dwa), shard_cols(dwb), shard_rows(dwo)]
    small_g = _pack_small([d_gin, d_mu, d_w0, d_a0, d_kk_, d_ka_, d_rk, d_gnw, d_gnb, d_alog, d_dtb, d_onw, d_now],
                          loss_acc[0:1])
    scat.append(jnp.stack([small_g, small_g])[:, None])
    pair = _exchange("reduce_cores", ("c",), scat, [], place_own=False)
    part = [_pair_sum("pair_sum_%d" % i, own, got, BF16 if i < 7 else F32)
            for i, (own, got) in enumerate(zip(scat, pair))]
    lands = _exchange("reduce_chips", ("x", "y"), part[:7], [part[7][0]])

    small_w = [norm_in_w, rw_mu, rw_w0, rw_a0, rw_k_k, rw_k_a, rw_r_k, rw_gn_w, rw_gn_b, gd_A_log, gd_dt_bias, gd_o_norm_w, norm_out_w]
    small_m = [m_norm_in_w, m_rw_mu, m_rw_w0, m_rw_a0, m_rw_k_k, m_rw_k_a, m_rw_r_k, m_rw_gn_w, m_rw_gn_b, m_gd_A_log, m_gd_dt_bias, m_gd_o_norm_w, m_norm_out_w]
    small_v = [v_norm_in_w, v_rw_mu, v_rw_w0, v_rw_a0, v_rw_k_k, v_rw_k_a, v_rw_r_k, v_rw_gn_w, v_rw_gn_b, v_gd_A_log, v_gd_dt_bias, v_gd_o_norm_w, v_norm_out_w]
    flat = lambda arrs: [a.reshape(1, -1) for a in arrs]
    sm, loss_row = _adam_small(lands[7], flat(small_w), flat(small_m), flat(small_v))
    sm_g, sm_d, sm_m, sm_v = [{nm: res[i].reshape(w.shape) for (nm, _), res, w in zip(_SMALL, sm, small_w)}
                              for i in range(4)]

    big = {"w_in": [o.T[None] for o in _adam("adam_w_in", lands[0], wt_own, mt_own, vt_own)]}
    for nm, land, w, m, v in (("rw_w2", lands[1], rw_w2, m_rw_w2, v_rw_w2),
                              ("rw_a2", lands[2], rw_a2, m_rw_a2, v_rw_a2),
                              ("gd_conv_w", lands[3], gd_conv_w, m_gd_conv_w, v_gd_conv_w),
                              ("w_branch_a", lands[4], w_branch_a, m_w_branch_a, v_w_branch_a),
                              ("w_branch_b", lands[5], w_branch_b, m_w_branch_b, v_w_branch_b),
                              ("w_out", lands[6], w_out, m_w_out, v_w_out)):
        big[nm] = [o.reshape(w.shape) for o in _adam("adam_" + nm, land, w[0], m[0], v[0])]

    order = ["norm_in_w", "w_in", "rw_mu", "rw_w0", "rw_w2", "rw_a0", "rw_a2", "rw_k_k", "rw_k_a", "rw_r_k", "rw_gn_w",
             "rw_gn_b", "gd_conv_w", "gd_A_log", "gd_dt_bias", "gd_o_norm_w", "w_branch_a", "w_branch_b", "w_out", "norm_out_w"]
    pick = lambda nm, i: big[nm][i] if nm in big else (sm_g, sm_d, sm_m, sm_v)[i][nm]
    loss = loss_row[0, 0]
    grad_x = dx2.reshape(x.shape)
    return (loss, grad_x, *[pick(nm, 0) for nm in order], *[pick(nm, 1) for nm in order],
            *[pick(nm, 2) for nm in order], *[pick(nm, 3) for nm in order])
```

```python
import functools

import jax
import jax.numpy as jnp
from jax import lax
from jax.experimental import pallas as pl
from jax.experimental.pallas import tpu as pltpu

F32 = jnp.float32
BF16 = jnp.bfloat16
HI = lax.Precision.HIGHEST

LANES = 128
SUB = 8
CHUNK = 64
N_DEV = 8
VMEM_LIMIT = 56 * 1024 * 1024

D_MODEL = 1024
RW_W = 512
GD_W = 512
RW_SHIFT = 1664
NORM_EPS = 1e-6
RW_GN_EPS = 64 * 1e-5
ADAM_LR, ADAM_B1, ADAM_B2, ADAM_EPS, ADAM_WD, ADAM_STEP = 0.001, 0.9, 0.999, 1e-8, 0.01, 10


_NN, _NT, _TN = ((1,), (0,)), ((1,), (1,)), ((0,), (0,))


def _dot(a, b, dims, passes):
    precision = lax.Precision.HIGH if passes == 3 else lax.Precision.DEFAULT
    return lax.dot_general(a, b, (dims, ((), ())), precision=precision, preferred_element_type=F32)


def _mm(a, b, passes=3):
    return _dot(a, b, _NN, passes)


def _mm_nt(a, b, passes=3):
    return _dot(a, b, _NT, passes)


def _mm_tn(a, b, passes=3):
    return _dot(a, b, _TN, passes)


P_SUM = 3
P_SCORE = 1
P_INV = 1
P_STATE = 1
P_APPLY = 1
P_UPDATE = 1
P_POINT = 1


def _stack_rows(blocks):
    return jnp.concatenate(blocks, axis=0)


def _split_rows(x, n):
    r = x.shape[0] // n

    @jax.custom_vjp
    def split(x):
        return tuple(x[i * r:(i + 1) * r] for i in range(n))

    split.defvjp(lambda x: (split(x), None), lambda _, gs: (jnp.concatenate(gs, axis=0),))
    return split(x)


def _iota(shape, d):
    return lax.broadcasted_iota(jnp.int32, shape, d)


def _sigmoid(x):
    return 0.5 * (jnp.tanh(0.5 * x) + 1.0)


def _silu(x):
    return x * _sigmoid(x)


def _softplus(x):
    return jnp.maximum(x, 0.0) + jnp.log(1.0 + jnp.exp(-jnp.abs(x)))


def _seg_ones(seg):
    return ((_iota((LANES, LANES), 0) // seg) == (_iota((LANES, LANES), 1) // seg)).astype(F32)


def _sl(g):
    return slice(g * LANES, (g + 1) * LANES)


@jax.custom_vjp
def _tri_inverse(ms):
    return _tri_inverse_chain(ms)


def _tri_inverse_bwd(ts, dts):
    return ([-_mm_nt(_mm_tn(t, dt, P_INV), t, P_INV) for t, dt in zip(ts, dts)],)


def _tri_inverse_chain(ms):
    c = CHUNK
    ri, ci = _iota((c, c), 0), _iota((c, c), 1)
    eye = (ri == ci).astype(F32)
    d16 = (ri // 16) == (ci // 16)
    d32 = (ri // 32) == (ci // 32)
    ps = [jnp.where(d16, -m, 0.0) for m in ms]
    ts = [eye + p for p in ps]
    for _ in range(3):
        ps = [_mm(p, p, P_INV) for p in ps]
        ts = [_mm(t, eye + p, P_INV) for t, p in zip(ts, ps)]
    for off_diagonal in (d32 & (~d16), ~d32):
        tq = [_mm(t, jnp.where(off_diagonal, m, 0.0), P_INV) for t, m in zip(ts, ms)]
        ts = [t - _mm(a, t, P_INV) for t, a in zip(ts, tq)]
    return ts


_tri_inverse.defvjp(lambda ms: (lambda ts: (ts, ts))(_tri_inverse_chain(ms)), _tri_inverse_bwd)


@jax.custom_vjp
def _known_inverse(ms, ts):
    return ts


_known_inverse.defvjp(lambda ms, ts: (ts, ts),
                      lambda ts, dts: (_tri_inverse_bwd(ts, dts)[0], [jnp.zeros_like(t) for t in ts]))


def _chunk_fwd(prims, *, nsub=None, scalar_decay=None, kinds=None, inverses=None):
    c = CHUNK
    ng = len(prims)
    kinds = kinds if kinds is not None else [(nsub, scalar_decay)] * ng
    s0s, rs, lws, ks, vs, kks, bs = [list(t) for t in zip(*prims)]
    ri, ci = _iota((c, c), 0), _iota((c, c), 1)
    incl = ri >= ci
    strict = ri > ci
    tril = incl.astype(F32)
    lane = _iota((1, LANES), 1)
    heads = [n for n, _ in kinds]
    scalar = [sc for _, sc in kinds]
    masks = [[((lane // (LANES // n)) == s).astype(F32) for s in range(n)] if n > 1 else [1.0] for n in heads]
    cws = [_mm(tril, lw, P_SUM) for lw in lws]
    cwxs = [cw - lw for cw, lw in zip(cws, lws)]
    ends = [cw[c - 1:c, :] for cw in cws]
    kkds = [kk * jnp.exp(cwx) for kk, cwx in zip(kks, cwxs)]
    rds = [r * jnp.exp(cw) for r, cw in zip(rs, cws)]
    kends = [k * jnp.exp(e - cw) for k, e, cw in zip(ks, ends, cws)]
    bends = [b * jnp.exp(e - cw) for b, e, cw in zip(bs, ends, cws)]
    state_terms = [_split_rows(_mm_nt(_stack_rows([kkd, rd]), s0, P_STATE), 2) for kkd, rd, s0 in zip(kkds, rds, s0s)]
    w0s, y0s = [t[0] for t in state_terms], [t[1] for t in state_terms]
    chains = [(g, s) for g in range(ng) for s in range(heads[g])]
    e0 = (lane == 0).astype(F32) * jnp.ones((c, 1), F32)
    rows = [_mm_nt(e0, cw, P_SUM) if sc else None for cw, sc in zip(cws, scalar)]
    dxs = [jnp.where(strict, jnp.exp(jnp.minimum(cwx[:, :c] - row, 0.0)), 0.0) if sc else None
           for cwx, row, sc in zip(cwxs, rows, scalar)]
    dis = [jnp.where(incl, jnp.exp(jnp.minimum(cw[:, :c] - row, 0.0)), 0.0) if sc else None
           for cw, row, sc in zip(cws, rows, scalar)]
    lefts = [_stack_rows([a * m for m in ms] + [q * m for m in ms])
             for a, q, ms in zip([kk if sc else kkd for kk, kkd, sc in zip(kks, kkds, scalar)],
                                 [r if sc else rd for r, rd, sc in zip(rs, rds, scalar)], masks)]
    rights_b = [b if sc else b * jnp.exp(-cw) for b, cw, sc in zip(bs, cws, scalar)]
    rights_k = [k if sc else k * jnp.exp(-cw) for k, cw, sc in zip(ks, cws, scalar)]
    on_b = [_split_rows(_mm_nt(left, right, P_SCORE), 2 * n) for left, right, n in zip(lefts, rights_b, heads)]
    on_k = [_split_rows(_mm_nt(left, right, P_SCORE), 2 * n) for left, right, n in zip(lefts, rights_k, heads)]
    lower = lambda x, g: x * dxs[g] if scalar[g] else jnp.where(strict, x, 0.0)
    lower_incl = lambda x, g: x * dis[g] if scalar[g] else jnp.where(incl, x, 0.0)
    m_b = [lower(on_b[g][s], g) for g, s in chains]
    m_k = [lower(on_k[g][s], g) for g, s in chains]
    n_k = [lower_incl(on_k[g][heads[g] + s], g) for g, s in chains]
    n_b = [lower_incl(on_b[g][heads[g] + s], g) for g, s in chains]
    t_inv = _tri_inverse(m_b) if inverses is None else _known_inverse(m_b, inverses)
    on_v = [_split_rows(_mm(_stack_rows([mk, nk]), vs[g], P_APPLY), 2) for (g, s), mk, nk in zip(chains, m_k, n_k)]
    sa_c = [_mm(t, w0s[g] + mv[0], P_APPLY) for (g, s), t, mv in zip(chains, t_inv, on_v)]
    y_c = [y0s[g] + mv[1] - _mm(nb, sa, P_APPLY) for (g, s), mv, nb, sa in zip(chains, on_v, n_b, sa_c)]
    first = [sum(heads[:g]) for g in range(ng)]
    per_group = lambda xs: [functools.reduce(lambda p, q: p + q, [xs[first[g] + s] * masks[g][s] for s in range(heads[g])])
                            for g in range(ng)]
    sas, ys = per_group(sa_c), per_group(y_c)
    s_ends = [s0 * jnp.exp(e) + _mm_tn(_stack_rows([v, -sa]), _stack_rows([kend, bend]), P_UPDATE)
              for s0, e, v, kend, sa, bend in zip(s0s, ends, vs, kends, sas, bends)]
    row_head = lambda n: _iota((LANES, LANES), 0) // (LANES // n)
    col_head = lambda n: _iota((LANES, LANES), 1) // (LANES // n)
    s_ends = [jnp.where(row_head(n) == col_head(n), s_end, 0.0) if n > 1 else s_end for s_end, n in zip(s_ends, heads)]
    return list(zip(ys, s_ends)), t_inv


def _rec_fwd(name, branches, *, seq):
    n, w = branches[0][0][0].shape
    ng = w // LANES
    nc = seq // CHUNK
    nb = n // seq
    nbr = len(branches)
    per = nb * ng
    kinds = [(heads, scalar) for _, heads, scalar in branches for _ in range(per)]
    nts = [per * heads for _, heads, _ in branches]

    def body(*refs):
        in_refs = [refs[6 * i:6 * i + 6] for i in range(nbr)]
        out_refs = [refs[6 * nbr + 3 * i:6 * nbr + 3 * i + 3] for i in range(nbr)]
        states = refs[9 * nbr:]

        @pl.when(pl.program_id(0) == 0)
        def _():
            for state in states:
                state[...] = jnp.zeros_like(state)
        where = [(i, bi, g) for i in range(nbr) for bi in range(nb) for g in range(ng)]
        prims = [(states[i][bi * ng + g],) + tuple(ref[bi, :, _sl(g)] for ref in in_refs[i]) for i, bi, g in where]
        outs, t_inv = _chunk_fwd(prims, kinds=kinds)
        for (i, bi, g), prim, (y, s_end) in zip(where, prims, outs):
            y_ref, s_ref, _ = out_refs[i]
            s_ref[0, bi * ng + g] = prim[0]
            y_ref[bi, :, _sl(g)] = y
            states[i][bi * ng + g] = s_end
        pos = 0
        for i in range(nbr):
            for j in range(nts[i]):
                out_refs[i][2][0, j] = t_inv[pos + j]
            pos += nts[i]

    row = pl.BlockSpec((nb, CHUNK, w), lambda c: (0, c, 0))
    seqs = lambda a: a.reshape(nb, seq, w)
    res = pl.pallas_call(
        body, name=name, grid=(nc,),
        in_specs=[row] * (6 * nbr),
        out_specs=[spec for nt in nts for spec in (row, pl.BlockSpec((1, per, LANES, LANES), lambda c: (c, 0, 0, 0)),
                                                   pl.BlockSpec((1, nt, CHUNK, CHUNK), lambda c: (c, 0, 0, 0)))],
        out_shape=[shp for nt in nts for shp in (jax.ShapeDtypeStruct((nb, seq, w), F32),
                                                 jax.ShapeDtypeStruct((nc, per, LANES, LANES), F32),
                                                 jax.ShapeDtypeStruct((nc, nt, CHUNK, CHUNK), F32))],
        scratch_shapes=[pltpu.VMEM((per, LANES, LANES), F32)] * nbr,
        compiler_params=pltpu.CompilerParams(dimension_semantics=("arbitrary",), vmem_limit_bytes=VMEM_LIMIT),
    )(*[seqs(a) for arrs, _, _ in branches for a in arrs])
    return [(res[3 * i].reshape(n, w), (res[3 * i + 1], res[3 * i + 2])) for i in range(nbr)]


def _rec_bwd(name, branches, *, seq):
    n, w = branches[0][0][0].shape
    ng = w // LANES
    nc = seq // CHUNK
    nb = n // seq
    nbr = len(branches)
    per = nb * ng
    kinds = [(heads, scalar) for _, _, _, heads, scalar in branches for _ in range(per)]
    nts = [per * heads for _, _, _, heads, _ in branches]

    def body(*refs):
        in_refs = [refs[9 * i:9 * i + 9] for i in range(nbr)]
        out_refs = [refs[9 * nbr + 6 * i:9 * nbr + 6 * i + 6] for i in range(nbr)]
        dstates = refs[15 * nbr:]

        @pl.when(pl.program_id(0) == 0)
        def _():
            for dstate in dstates:
                dstate[...] = jnp.zeros_like(dstate)
        where = [(i, bi, g) for i in range(nbr) for bi in range(nb) for g in range(ng)]
        inverses = [in_refs[i][7][0, j] for i in range(nbr) for j in range(nts[i])]
        f = lambda p: _chunk_fwd(p, kinds=kinds, inverses=inverses)[0]
        prims = [(in_refs[i][6][0, bi * ng + g],) + tuple(ref[bi, :, _sl(g)] for ref in in_refs[i][:6])
                 for i, bi, g in where]
        _, vjp = jax.vjp(f, prims)
        (d_prims,) = vjp([(in_refs[i][8][bi, :, _sl(g)], dstates[i][bi * ng + g]) for i, bi, g in where])
        for (i, bi, g), d_prim in zip(where, d_prims):
            dstates[i][bi * ng + g] = d_prim[0]
            for ref, d in zip(out_refs[i], d_prim[1:]):
                ref[bi, :, _sl(g)] = d

    row = pl.BlockSpec((nb, CHUNK, w), lambda c: (0, nc - 1 - c, 0))
    seqs = lambda a: a.reshape(nb, seq, w)
    in_specs, args = [], []
    for (arrs, (s_save, t_save), dy, _, _), nt in zip(branches, nts):
        in_specs += [row] * 6 + [pl.BlockSpec((1, per, LANES, LANES), lambda c: (nc - 1 - c, 0, 0, 0)),
                                 pl.BlockSpec((1, nt, CHUNK, CHUNK), lambda c: (nc - 1 - c, 0, 0, 0)), row]
        args += [seqs(a) for a in arrs] + [s_save, t_save, seqs(dy)]
    grads = pl.pallas_call(
        body, name=name, grid=(nc,),
        in_specs=in_specs,
        out_specs=[row] * (6 * nbr),
        out_shape=[jax.ShapeDtypeStruct((nb, seq, w), F32)] * (6 * nbr),
        scratch_shapes=[pltpu.VMEM((per, LANES, LANES), F32)] * nbr,
        compiler_params=pltpu.CompilerParams(dimension_semantics=("arbitrary",), vmem_limit_bytes=VMEM_LIMIT),
    )(*args)
    return [[g.reshape(n, w) for g in grads[6 * i:6 * i + 6]] for i in range(nbr)]


def _shift_down(a, j, halo, is_start):
    tb = a.shape[0]
    rolled = pltpu.roll(a, j, 0)
    hr = jnp.where(is_start, 0.0, pltpu.roll(halo, j, 0))
    first = jnp.where(_iota((SUB, LANES), 0) < j, hr, rolled[0:SUB])
    if tb == SUB:
        return first
    return jnp.concatenate([first, rolled[SUB:]], axis=0)


def _shift_up(d, j, carry, is_end):
    tb = d.shape[0]
    up = pltpu.roll(d, tb - j, 0)
    cr = jnp.where(is_end, 0.0, pltpu.roll(carry, SUB - j, 0))
    last = jnp.where(_iota((SUB, LANES), 0) >= SUB - j, cr, up[tb - SUB:tb])
    if tb == SUB:
        return last
    return jnp.concatenate([up[:tb - SUB], last], axis=0)


def _ngroups(a):
    return a.shape[1] // LANES


def _pw_fwd(name, f, ins, shift, params, out_widths, out_dtypes, *, seq, tb):
    n = ins[0].shape[0]
    nt, tps = n // tb, seq // tb
    ni, npar = len(ins), len(params)

    def body(*refs):
        in_refs = refs[:ni]
        pos = ni
        halo_ref = None
        if shift:
            halo_ref = refs[pos]
            pos += 1
        p_refs = refs[pos:pos + npar]
        out_refs = refs[pos + npar:]
        is_start = (pl.program_id(0) % tps) == 0
        tiles = [[ref[:, _sl(g)] for g in range(_ngroups(ref))] for ref in in_refs]
        prevs = [[_shift_down(tiles[0][g], j, halo_ref[:, _sl(g)], is_start) for g in range(len(tiles[0]))]
                 for j in range(1, shift + 1)]
        pv = [[ref[:, _sl(g)] for g in range(_ngroups(ref))] for ref in p_refs]
        outs = f(tiles, prevs, pv)
        for o_ref, og in zip(out_refs, outs, strict=True):
            for g, t in enumerate(og):
                o_ref[:, _sl(g)] = t.astype(o_ref.dtype)

    in_specs = [pl.BlockSpec((tb, a.shape[1]), lambda i: (i, 0)) for a in ins]
    args = list(ins)
    if shift:
        in_specs.append(pl.BlockSpec((SUB, ins[0].shape[1]), lambda i: (jnp.maximum(i * (tb // SUB) - 1, 0), 0)))
        args.append(ins[0])
    in_specs += [pl.BlockSpec(p.shape, lambda i: (0, 0)) for p in params]
    args += list(params)
    return pl.pallas_call(
        body, name=name, grid=(nt,),
        in_specs=in_specs,
        out_specs=[pl.BlockSpec((tb, w), lambda i: (i, 0)) for w in out_widths],
        out_shape=[jax.ShapeDtypeStruct((n, w), dt) for w, dt in zip(out_widths, out_dtypes, strict=True)],
        compiler_params=pltpu.CompilerParams(dimension_semantics=("parallel",), vmem_limit_bytes=VMEM_LIMIT),
    )(*args)


def _pw_bwd(name, f, ins, shift, params, douts, din_dtypes, *, seq, tb):
    n = ins[0].shape[0]
    nt, tps = n // tb, seq // tb
    ni, npar = len(ins), len(params)
    flat_douts = [d for ds in douts for d in ds]
    nd = len(flat_douts)
    w0 = ins[0].shape[1]

    def body(*refs):
        in_refs = refs[:ni]
        pos = ni
        halo_ref = None
        if shift:
            halo_ref = refs[pos]
            pos += 1
        p_refs = refs[pos:pos + npar]
        pos += npar
        d_refs = refs[pos:pos + nd]
        pos += nd
        din_refs = refs[pos:pos + ni]
        pos += ni
        dp_refs = refs[pos:pos + npar]
        pos += npar
        carry = refs[pos] if shift else None
        step = pl.program_id(0)
        tile = nt - 1 - step
        is_start = (tile % tps) == 0
        is_end = (tile % tps) == tps - 1
        tiles = [[ref[:, _sl(g)] for g in range(_ngroups(ref))] for ref in in_refs]
        prevs = [[_shift_down(tiles[0][g], j, halo_ref[:, _sl(g)], is_start) for g in range(len(tiles[0]))]
                 for j in range(1, shift + 1)]
        pv = [[ref[:, _sl(g)] for g in range(_ngroups(ref))] for ref in p_refs]
        cot, pos_d = [], 0
        for ds in douts:
            grp = d_refs[pos_d:pos_d + len(ds)]
            pos_d += len(ds)
            cot.append([functools.reduce(lambda p, q: p + q, [ref[:, _sl(g)].astype(F32) for ref in grp])
                        for g in range(_ngroups(grp[0]))])
        _, vjp = jax.vjp(f, tiles, prevs, pv)
        d_tiles, d_prevs, d_pv = vjp(cot)
        for g in range(len(tiles[0])):
            for j in range(1, shift + 1):
                d_tiles[0][g] = d_tiles[0][g] + _shift_up(d_prevs[j - 1][g], j, carry[j - 1, :, _sl(g)], is_end)
            for j in range(1, shift + 1):
                carry[j - 1, :, _sl(g)] = d_prevs[j - 1][g][0:SUB]
        for ref, dg in zip(din_refs, d_tiles, strict=True):
            for g, t in enumerate(dg):
                ref[:, _sl(g)] = t.astype(ref.dtype)

        @pl.when(step == 0)
        def _():
            for ref in dp_refs:
                ref[...] = jnp.zeros_like(ref)
        for ref, dg in zip(dp_refs, d_pv, strict=True):
            for g, t in enumerate(dg):
                ref[:, _sl(g)] += t

    rev = lambda i: (nt - 1 - i, 0)
    in_specs = [pl.BlockSpec((tb, a.shape[1]), rev) for a in ins]
    args = list(ins)
    if shift:
        in_specs.append(pl.BlockSpec((SUB, w0), lambda i: (jnp.maximum((nt - 1 - i) * (tb // SUB) - 1, 0), 0)))
        args.append(ins[0])
    in_specs += [pl.BlockSpec(p.shape, lambda i: (0, 0)) for p in params]
    args += list(params)
    in_specs += [pl.BlockSpec((tb, d.shape[1]), rev) for d in flat_douts]
    args += flat_douts
    out_specs = [pl.BlockSpec((tb, a.shape[1]), rev) for a in ins] + [pl.BlockSpec(p.shape, lambda i: (0, 0)) for p in params]
    out_shape = ([jax.ShapeDtypeStruct(a.shape, dt) for a, dt in zip(ins, din_dtypes, strict=True)]
                 + [jax.ShapeDtypeStruct(p.shape, F32) for p in params])
    res = pl.pallas_call(
        body, name=name, grid=(nt,),
        in_specs=in_specs, out_specs=out_specs, out_shape=out_shape,
        scratch_shapes=[pltpu.VMEM((shift, SUB, w0), F32)] if shift else [],
        compiler_params=pltpu.CompilerParams(dimension_semantics=("arbitrary",), vmem_limit_bytes=VMEM_LIMIT),
    )(*args)
    return res[:ni], res[ni:]


def _rwkv_prep_f(tiles, prevs, params):
    (p,), (prev,) = tiles, prevs
    mu, w0, w2p, a0, a2p, k_k, k_a = params
    xs = [p[g] + (prev[g] - p[g]) * mu[g] for g in range(13)]
    wdad = xs[12]
    tw = jnp.tanh(wdad)
    e64 = _seg_ones(64)
    r, lw, k2, v, kk, b = [], [], [], [], [], []
    for g in range(4):
        k_g = xs[4 + g]
        lo = w0[g] + _mm(tw, w2p[g], P_POINT)
        lw_g = -jnp.exp(-_softplus(-lo) - 0.5)
        a_g = _sigmoid(a0[g] + _mm(wdad, a2p[g], P_POINT))
        kkp = k_g * k_k[g]
        kk_g = kkp * lax.rsqrt(_mm(kkp * kkp, e64, P_POINT) + 1e-12)
        r.append(xs[g])
        lw.append(lw_g)
        k2.append(k_g * (1.0 + (a_g - 1.0) * k_a[g]))
        v.append(xs[8 + g])
        kk.append(kk_g)
        b.append(kk_g * a_g)
    return [r, lw, k2, v, kk, b]


def _rwkv_post_f(tiles, prevs, params):
    yrec, r, k2, v, z = tiles
    gn_w, gn_b, r_k = params
    e64 = _seg_ones(64)
    out = []
    for g in range(4):
        mean = _mm(yrec[g], e64, P_POINT) * (1.0 / 64)
        d = yrec[g] - mean
        var = _mm(d * d, e64, P_POINT) * (1.0 / 64)
        yn = d * lax.rsqrt(var + RW_GN_EPS) * gn_w[g] + gn_b[g]
        bonus = _mm(r[g] * k2[g] * r_k[g], e64, P_POINT) * v[g]
        out.append((yn + bonus) * _silu(z[g]))
    return [out]


def _gdn_prep_f(tiles, prevs, params):
    x, (ba,) = tiles
    p1, p2, p3 = prevs
    cw0, cw1, cw2, cw3, a_log, dt_bias = params
    s = [_silu(cw3[g] * x[g] + cw2[g] * p1[g] + cw1[g] * p2[g] + cw0[g] * p3[g]) for g in range(12)]
    row = _iota((LANES, LANES), 0)
    r, lw, k, vv, b = [], [], [], [], []
    for h in range(4):
        q_h, k_h, v_h = s[h], s[4 + h], s[8 + h]
        qn = q_h * lax.rsqrt(jnp.sum(q_h * q_h, axis=-1, keepdims=True) + 1e-12)
        kn = k_h * lax.rsqrt(jnp.sum(k_h * k_h, axis=-1, keepdims=True) + 1e-12)
        beta = _sigmoid(_mm(ba, (row == h).astype(F32)))
        alpha = _mm(ba, (row == 4 + h).astype(F32))
        g_h = -jnp.exp(a_log[h]) * _softplus(alpha + dt_bias[h])
        r.append(qn * (LANES ** -0.5))
        lw.append(g_h)
        k.append(kn)
        vv.append(beta * v_h)
        b.append(jnp.exp(g_h) * beta * kn)
    return [r, lw, k, vv, b]


def _gdn_post_f(tiles, prevs, params):
    o, z = tiles
    ((onw,),) = params
    out = []
    for h in range(4):
        ms = jnp.mean(o[h] * o[h], axis=-1, keepdims=True)
        out.append(o[h] * lax.rsqrt(ms + NORM_EPS) * onw * _silu(z[h]))
    return [out]


def _norm_in(x2, g_in, *, tm):
    n = x2.shape[0]

    def body(x_ref, g_ref, h_ref):
        x = x_ref[...]
        rs = lax.rsqrt(jnp.mean(x * x, axis=-1, keepdims=True) + NORM_EPS)
        h_ref[...] = (x * rs * g_ref[...]).astype(BF16)

    return pl.pallas_call(
        body, name="norm_in", grid=(n // tm,),
        in_specs=[pl.BlockSpec((tm, D_MODEL), lambda i: (i, 0)), pl.BlockSpec((1, D_MODEL), lambda i: (0, 0))],
        out_specs=pl.BlockSpec((tm, D_MODEL), lambda i: (i, 0)),
        out_shape=jax.ShapeDtypeStruct((n, D_MODEL), BF16),
        compiler_params=pltpu.CompilerParams(dimension_semantics=("parallel",), vmem_limit_bytes=VMEM_LIMIT),
    )(x2, g_in)


def _proj(name, h, wt, *, tm):
    n, ws = h.shape[0], wt.shape[0]

    def body(h_ref, w_ref, o_ref):
        o_ref[...] = lax.dot_general(h_ref[...], w_ref[...], (_NT, ((), ())), preferred_element_type=F32)

    return pl.pallas_call(
        body, name=name, grid=(n // tm,),
        in_specs=[pl.BlockSpec((tm, D_MODEL), lambda i: (i, 0)), pl.BlockSpec((ws, D_MODEL), lambda i: (0, 0))],
        out_specs=pl.BlockSpec((tm, ws), lambda i: (i, 0)),
        out_shape=jax.ShapeDtypeStruct((n, ws), F32),
        compiler_params=pltpu.CompilerParams(dimension_semantics=("parallel",), vmem_limit_bytes=VMEM_LIMIT),
    )(h, wt)


def _proj_dw(name, h, dp, *, tm):
    n, ws = dp.shape

    def body(h_ref, d_ref, o_ref):
        @pl.when(pl.program_id(0) == 0)
        def _():
            o_ref[...] = jnp.zeros_like(o_ref)
        o_ref[...] += lax.dot_general(d_ref[...], h_ref[...], (_TN, ((), ())), preferred_element_type=F32)

    return pl.pallas_call(
        body, name=name, grid=(n // tm,),
        in_specs=[pl.BlockSpec((tm, D_MODEL), lambda i: (i, 0)), pl.BlockSpec((tm, ws), lambda i: (i, 0))],
        out_specs=pl.BlockSpec((ws, D_MODEL), lambda i: (0, 0)),
        out_shape=jax.ShapeDtypeStruct((ws, D_MODEL), F32),
        compiler_params=pltpu.CompilerParams(dimension_semantics=("arbitrary",), vmem_limit_bytes=VMEM_LIMIT),
    )(h, dp)


def _proj_dx(x2, g_in, d_xo, dps, ws, *, tm):
    n = x2.shape[0]
    ns = len(dps)

    def body(*refs):
        x_ref, g_ref, dxo_ref = refs[:3]
        dp_refs = refs[3:3 + ns]
        w_refs = refs[3 + ns:3 + 2 * ns]
        dx_ref, dg_ref = refs[3 + 2 * ns:]
        dh = jnp.zeros((tm, D_MODEL), F32)
        for d_ref, w_ref in zip(dp_refs, w_refs, strict=True):
            dh = dh + jnp.dot(d_ref[...], w_ref[...], preferred_element_type=F32)
        x = x_ref[...]
        rs = lax.rsqrt(jnp.mean(x * x, axis=-1, keepdims=True) + NORM_EPS)
        xn = x * rs
        dxn = dh * g_ref[...]
        dx_ref[...] = dxo_ref[...] + rs * (dxn - xn * jnp.mean(dxn * xn, axis=-1, keepdims=True))

        @pl.when(pl.program_id(0) == 0)
        def _():
            dg_ref[...] = jnp.zeros_like(dg_ref)
        dg_ref[...] += jnp.sum(dh * xn, axis=0, keepdims=True)

    row = pl.BlockSpec((tm, D_MODEL), lambda i: (i, 0))
    return pl.pallas_call(
        body, name="proj_dx", grid=(n // tm,),
        in_specs=([row, pl.BlockSpec((1, D_MODEL), lambda i: (0, 0)), row]
                  + [pl.BlockSpec((tm, d.shape[1]), lambda i: (i, 0)) for d in dps]
                  + [pl.BlockSpec(w.shape, lambda i: (0, 0)) for w in ws]),
        out_specs=[row, pl.BlockSpec((1, D_MODEL), lambda i: (0, 0))],
        out_shape=[jax.ShapeDtypeStruct((n, D_MODEL), F32), jax.ShapeDtypeStruct((1, D_MODEL), F32)],
        compiler_params=pltpu.CompilerParams(dimension_semantics=("arbitrary",), vmem_limit_bytes=VMEM_LIMIT),
    )(x2, g_in, d_xo, *dps, *ws)


def _tail(x2, tgt2, gates, ya, yb, w_a, w_b, w_o, now, *, tr):
    n = x2.shape[0]

    def body(x_ref, t_ref, g_ref, ya_ref, yb_ref, wa_ref, wb_ref, wo_ref, now_ref,
             dya_ref, dyb_ref, dg_ref, dxo_ref, dwa_ref, dwb_ref, dwo_ref, dnow_ref, loss_ref):
        ya16, yb16 = ya_ref[...].astype(BF16), yb_ref[...].astype(BF16)
        ua = jnp.dot(ya16, wa_ref[...], preferred_element_type=F32)
        ub = jnp.dot(yb16, wb_ref[...], preferred_element_type=F32)
        ga = _sigmoid(g_ref[:, :D_MODEL])
        gb = _sigmoid(g_ref[:, D_MODEL:])
        m16 = (ga * ua + gb * ub).astype(BF16)
        xo = x_ref[...] + jnp.dot(m16, wo_ref[...], preferred_element_type=F32)
        rs = lax.rsqrt(jnp.mean(xo * xo, axis=-1, keepdims=True) + NORM_EPS)
        yn = xo * rs
        now_v = now_ref[...]
        err = yn * now_v - t_ref[...]
        dy = err * (1.0 / D_MODEL)
        dyn = dy * now_v
        dxo = rs * (dyn - yn * jnp.mean(dyn * yn, axis=-1, keepdims=True))
        dxo_ref[...] = dxo
        dxo16 = dxo.astype(BF16)
        dm = lax.dot_general(dxo16, wo_ref[...], (((1,), (1,)), ((), ())), preferred_element_type=F32)
        dua16 = (dm * ga).astype(BF16)
        dub16 = (dm * gb).astype(BF16)
        dg_ref[:, :D_MODEL] = (dm * ua * ga * (1.0 - ga)).astype(dg_ref.dtype)
        dg_ref[:, D_MODEL:] = (dm * ub * gb * (1.0 - gb)).astype(dg_ref.dtype)
        dya_ref[...] = lax.dot_general(dua16, wa_ref[...], (((1,), (1,)), ((), ())), preferred_element_type=F32)
        dyb_ref[...] = lax.dot_general(dub16, wb_ref[...], (((1,), (1,)), ((), ())), preferred_element_type=F32)

        @pl.when(pl.program_id(0) == 0)
        def _():
            for ref in (dwa_ref, dwb_ref, dwo_ref, dnow_ref, loss_ref):
                ref[...] = jnp.zeros_like(ref)
        tn = (((0,), (0,)), ((), ()))
        dwo_ref[...] += lax.dot_general(m16, dxo16, tn, preferred_element_type=F32)
        dwa_ref[...] += lax.dot_general(ya16, dua16, tn, preferred_element_type=F32)
        dwb_ref[...] += lax.dot_general(yb16, dub16, tn, preferred_element_type=F32)
        dnow_ref[...] += jnp.sum(dy * yn, axis=0, keepdims=True)
        loss_ref[...] += (0.5 / D_MODEL) * jnp.sum(err * err)

    row = lambda w: pl.BlockSpec((tr, w), lambda i: (i, 0))
    full = lambda a: pl.BlockSpec(a.shape, lambda i: (0, 0))
    return pl.pallas_call(
        body, name="tail", grid=(n // tr,),
        in_specs=[row(D_MODEL), row(D_MODEL), row(2 * D_MODEL), row(RW_W), row(GD_W), full(w_a), full(w_b), full(w_o), full(now)],
        out_specs=[row(RW_W), row(GD_W), row(2 * D_MODEL), row(D_MODEL),
                   pl.BlockSpec((RW_W, D_MODEL), lambda i: (0, 0)), pl.BlockSpec((GD_W, D_MODEL), lambda i: (0, 0)),
                   pl.BlockSpec((D_MODEL, D_MODEL), lambda i: (0, 0)), pl.BlockSpec((1, D_MODEL), lambda i: (0, 0)),
                   pl.BlockSpec((SUB, LANES), lambda i: (0, 0))],
        out_shape=[jax.ShapeDtypeStruct((n, RW_W), F32), jax.ShapeDtypeStruct((n, GD_W), F32),
                   jax.ShapeDtypeStruct((n, 2 * D_MODEL), BF16), jax.ShapeDtypeStruct((n, D_MODEL), F32),
                   jax.ShapeDtypeStruct((RW_W, D_MODEL), F32), jax.ShapeDtypeStruct((GD_W, D_MODEL), F32),
                   jax.ShapeDtypeStruct((D_MODEL, D_MODEL), F32), jax.ShapeDtypeStruct((1, D_MODEL), F32),
                   jax.ShapeDtypeStruct((SUB, LANES), F32)],
        compiler_params=pltpu.CompilerParams(dimension_semantics=("arbitrary",), vmem_limit_bytes=VMEM_LIMIT),
    )(x2, tgt2, gates, ya, yb, w_a, w_b, w_o, now)


def _exchange(name, axes, scatter, gather, place_own=True):
    ns, ng = len(scatter), len(gather)
    na = ns + ng
    gs = 2 ** len(axes)
    arrs = list(scatter) + list(gather)

    def body(*refs):
        src = refs[:na]
        dst = refs[na:2 * na]
        send_sems, recv_sems = refs[2 * na:]
        mine = {ax: lax.axis_index(ax) for ax in ("x", "y", "c")}

        def peer(k):
            co = dict(mine)
            for i, ax in enumerate(axes):
                if (k >> (len(axes) - 1 - i)) & 1:
                    co[ax] = 1 - co[ax]
            idx = 0
            for ax in axes:
                idx = 2 * idx + co[ax]
            return (co["x"], co["y"], co["c"]), idx

        _, me = peer(0)

        def copy(a, k, landing):
            dev, idx = peer(k)
            s = src[a].at[idx] if a < ns else src[a]
            return pltpu.make_async_remote_copy(src_ref=s, dst_ref=dst[a].at[idx if landing else me],
                                                send_sem=send_sems.at[a, k - 1], recv_sem=recv_sems.at[a, k - 1],
                                                device_id=dev, device_id_type=pl.DeviceIdType.MESH)

        sends = [copy(a, k, False) for a in range(na) for k in range(1, gs)]
        for cp in sends:
            cp.start()
        for a in range(na):
            for k in range(1, gs):
                copy(a, k, True).wait_recv()
        for cp in sends:
            cp.wait_send()

    out_shape = [jax.ShapeDtypeStruct(a.shape, a.dtype) for a in scatter] + \
                [jax.ShapeDtypeStruct((gs,) + a.shape, a.dtype) for a in gather]
    anyspec = pl.BlockSpec(memory_space=pl.ANY)
    lands = pl.pallas_call(
        body, name=name,
        in_specs=[anyspec] * na, out_specs=[anyspec] * na, out_shape=out_shape,
        scratch_shapes=[pltpu.SemaphoreType.DMA((na, gs - 1)), pltpu.SemaphoreType.DMA((na, gs - 1))],
    )(*arrs)
    if not place_own:
        return lands
    me = 0
    for ax in axes:
        me = 2 * me + lax.axis_index(ax)
    kept = [lax.dynamic_index_in_dim(a, me, 0, keepdims=False) for a in scatter] + list(gather)
    return [lax.dynamic_update_index_in_dim(land, mine, me, 0) for land, mine in zip(lands, kept)]


def _gather_all(name, arrs):
    na = len(arrs)

    def body(*refs):
        src = refs[:na]
        dst = refs[na:2 * na]
        send_sems, recv_sems = refs[2 * na:]
        x, y, c = lax.axis_index("x"), lax.axis_index("y"), lax.axis_index("c")
        sibling = (x, y, 1 - c)
        chips = [(1 - x, y), (x, 1 - y), (1 - x, 1 - y)]

        def copy(a, k, block, to, own=False):
            px, py, pc = block
            slot = dst[a].at[pc, 2 * px + py]
            return pltpu.make_async_remote_copy(src_ref=src[a] if own else slot, dst_ref=slot,
                                                send_sem=send_sems.at[a, k], recv_sem=recv_sems.at[a, k],
                                                device_id=to, device_id_type=pl.DeviceIdType.MESH)

        first = [copy(a, 0, (x, y, c), sibling, own=True) for a in range(na)]
        first += [copy(a, 1 + j, (x, y, c), (*chip, c), own=True) for j, chip in enumerate(chips) for a in range(na)]
        for cp in first:
            cp.start()
        passed = []
        for j, chip in enumerate(chips):
            for a in range(na):
                copy(a, 1 + j, (*chip, c), (x, y, c)).wait_recv()
                passed.append(copy(a, 4 + j, (*chip, c), sibling))
                passed[-1].start()
        for a in range(na):
            copy(a, 0, (x, y, 1 - c), (x, y, c)).wait_recv()
            for j, chip in enumerate(chips):
                copy(a, 4 + j, (*chip, 1 - c), (x, y, c)).wait_recv()
        for cp in first + passed:
            cp.wait_send()

    anyspec = pl.BlockSpec(memory_space=pl.ANY)
    lands = pl.pallas_call(
        body, name=name,
        in_specs=[anyspec] * na, out_specs=[anyspec] * na,
        out_shape=[jax.ShapeDtypeStruct((2, 4) + a.shape, a.dtype) for a in arrs],
        scratch_shapes=[pltpu.SemaphoreType.DMA((na, 7)), pltpu.SemaphoreType.DMA((na, 7))],
    )(*arrs)
    core, chip = lax.axis_index("c"), 2 * lax.axis_index("x") + lax.axis_index("y")
    zero = jnp.zeros((), jnp.int32)
    return [lax.dynamic_update_slice(land, mine[None, None], (core, chip) + (zero,) * mine.ndim)
            for land, mine in zip(lands, arrs)]


def _pair_sum(name, own, land, out_dtype):
    _, nq, r, c = own.shape
    core = lax.axis_index("c").astype(jnp.int32).reshape(1)

    def body(core_ref, own_ref, land_ref, o_ref):
        o_ref[0] = (own_ref[0, 0] + land_ref[0, 0]).astype(o_ref.dtype)

    return pl.pallas_call(
        body, name=name,
        grid_spec=pltpu.PrefetchScalarGridSpec(
            num_scalar_prefetch=1, grid=(nq,),
            in_specs=[pl.BlockSpec((1, 1, r, c), lambda i, core_ref: (core_ref[0], i, 0, 0)),
                      pl.BlockSpec((1, 1, r, c), lambda i, core_ref: (1 - core_ref[0], i, 0, 0))],
            out_specs=pl.BlockSpec((1, r, c), lambda i, core_ref: (i, 0, 0))),
        out_shape=jax.ShapeDtypeStruct((nq, r, c), out_dtype),
        compiler_params=pltpu.CompilerParams(dimension_semantics=("parallel",), vmem_limit_bytes=VMEM_LIMIT),
    )(core, own, land)


def _adam(name, land, w, m, v):
    r, c = w.shape
    nslot = land.shape[0]
    tr = 256 if (r % 256 == 0 and r > 256) else r
    tc = 256 if (tr == r and r > 256 and c % 256 == 0) else c

    def body(l_ref, w_ref, m_ref, v_ref, g_out, d_out, m_out, v_out):
        g = l_ref[0].astype(F32)
        for s in range(1, nslot):
            g = g + l_ref[s].astype(F32)
        g_out[...] = g
        d_out[...], m_out[...], v_out[...] = _adam_math(g, w_ref[...], m_ref[...], v_ref[...])

    blk = pl.BlockSpec((tr, tc), lambda i: (i * tc // c, i % (c // tc)))
    return pl.pallas_call(
        body, name=name, grid=((r // tr) * (c // tc),),
        in_specs=[pl.BlockSpec((nslot, tr, tc), lambda i: (0, i * tc // c, i % (c // tc))), blk, blk, blk],
        out_specs=[blk] * 4,
        out_shape=[jax.ShapeDtypeStruct((r, c), F32)] * 4,
        compiler_params=pltpu.CompilerParams(dimension_semantics=("parallel",), vmem_limit_bytes=VMEM_LIMIT),
    )(land, w, m, v)


def _adam_math(g, w, m, v):
    c1 = 1.0 / (1.0 - ADAM_B1 ** ADAM_STEP)
    c2 = 1.0 / (1.0 - ADAM_B2 ** ADAM_STEP)
    m_new = ADAM_B1 * m + (1.0 - ADAM_B1) * g
    v_new = ADAM_B2 * v + (1.0 - ADAM_B2) * (g * g)
    return -ADAM_LR * ((m_new * c1) / (jnp.sqrt(v_new * c2) + ADAM_EPS) + ADAM_WD * w), m_new, v_new


def _adam_small(land, ws, ms, vs):
    npar = len(ws)
    nslot = land.shape[0]

    def body(*refs):
        l_ref = refs[0]
        w_refs, m_refs, v_refs = refs[1:1 + npar], refs[1 + npar:1 + 2 * npar], refs[1 + 2 * npar:1 + 3 * npar]
        outs = refs[1 + 3 * npar:1 + 7 * npar]
        loss_ref, g_rows = refs[1 + 7 * npar], refs[2 + 7 * npar]
        g = l_ref[0]
        for s in range(1, nslot):
            g = g + l_ref[s]
        g_rows[...] = g
        row = 0
        for i, (_, size) in enumerate(_SMALL):
            for j in range(-(-size // LANES)):
                width = min(LANES, size - j * LANES)
                cols = slice(j * LANES, j * LANES + width)
                g_ij = g_rows[row:row + 1, 0:width]
                delta, m_new, v_new = _adam_math(g_ij, w_refs[i][:, cols], m_refs[i][:, cols], v_refs[i][:, cols])
                for ref, val in zip(outs[4 * i:4 * i + 4], (g_ij, delta, m_new, v_new)):
                    ref[:, cols] = val
                row += 1
        loss_ref[...] = g_rows[row:row + 1, :]

    full = lambda a: pl.BlockSpec(a.shape, lambda: (0,) * a.ndim)
    res = pl.pallas_call(
        body, name="adam_small",
        in_specs=[full(land)] + [full(a) for a in list(ws) + list(ms) + list(vs)],
        out_specs=[full(w) for w in ws for _ in range(4)] + [pl.BlockSpec((1, LANES), lambda: (0, 0))],
        out_shape=[jax.ShapeDtypeStruct(w.shape, F32) for w in ws for _ in range(4)] + [jax.ShapeDtypeStruct((1, LANES), F32)],
        scratch_shapes=[pltpu.VMEM(land.shape[1:], F32)],
    )(land, *ws, *ms, *vs)
    return [res[4 * i:4 * i + 4] for i in range(npar)], res[4 * npar]


_SMALL = (("norm_in_w", 1024), ("rw_mu", 1664), ("rw_w0", 512), ("rw_a0", 512), ("rw_k_k", 512), ("rw_k_a", 512),
          ("rw_r_k", 512), ("rw_gn_w", 512), ("rw_gn_b", 512), ("gd_A_log", 4), ("gd_dt_bias", 4), ("gd_o_norm_w", 128),
          ("norm_out_w", 1024))
_SMALL_ROWS = 64


def _pack_small(vals, loss_row):
    rows = []
    for (_, size), a in zip(_SMALL, vals, strict=True):
        flat = a.reshape(-1).astype(F32)
        pad = (-size) % LANES
        if pad:
            flat = jnp.concatenate([flat, jnp.zeros((pad,), F32)])
        rows.append(flat.reshape(-1, LANES))
    rows.append(loss_row)
    used = sum(r.shape[0] for r in rows)
    rows.append(jnp.zeros((_SMALL_ROWS - used, LANES), F32))
    return jnp.concatenate(rows, axis=0)


def kernel(x, norm_in_w, w_in, rw_mu, rw_w0, rw_w2, rw_a0, rw_a2, rw_k_k, rw_k_a, rw_r_k, rw_gn_w, rw_gn_b, gd_conv_w, gd_A_log, gd_dt_bias, gd_o_norm_w, w_branch_a, w_branch_b, w_out, norm_out_w, loss_target, m_norm_in_w, m_w_in, m_rw_mu, m_rw_w0, m_rw_w2, m_rw_a0, m_rw_a2, m_rw_k_k, m_rw_k_a, m_rw_r_k, m_rw_gn_w, m_rw_gn_b, m_gd_conv_w, m_gd_A_log, m_gd_dt_bias, m_gd_o_norm_w, m_w_branch_a, m_w_branch_b, m_w_out, m_norm_out_w, v_norm_in_w, v_w_in, v_rw_mu, v_rw_w0, v_rw_w2, v_rw_a0, v_rw_a2, v_rw_k_k, v_rw_k_a, v_rw_r_k, v_rw_gn_w, v_rw_gn_b, v_gd_conv_w, v_gd_A_log, v_gd_dt_bias, v_gd_o_norm_w, v_w_branch_a, v_w_branch_b, v_w_out, v_norm_out_w):
    nb, seq, _ = x.shape
    n = nb * seq
    tm = min(1024, n)
    tb = min(512, seq)
    x2 = x.reshape(n, D_MODEL)
    tgt2 = loss_target.reshape(n, D_MODEL)
    cols = w_in.shape[2]
    in_cols = cols * N_DEV

    wt_own, mt_own, vt_own = w_in[0].T, m_w_in[0].T, v_w_in[0].T
    sharded = [wt_own.astype(BF16), rw_w2[0], rw_a2[0], gd_conv_w[0], w_branch_a[0].astype(BF16),
               w_branch_b[0].astype(BF16), w_out[0].astype(BF16)]
    g_win, g_w2, g_a2, g_conv, g_wa, g_wb, g_wo = _gather_all("gather_weights", sharded)
    unshard_rows = lambda a: jnp.transpose(a, (1, 0, 2, 3)).reshape(N_DEV * a.shape[2], a.shape[3])
    unshard_cols = lambda a: jnp.transpose(a, (2, 1, 0, 3)).reshape(a.shape[2], N_DEV * a.shape[3])
    wt_full = unshard_rows(g_win)
    seg_bounds = ((0, 1664), (1664, 2176), (2176, 3712), (3712, 4224), (4232, in_cols))
    w_rw, w_zrw, w_qkv, w_zgd, w_gates = [wt_full[a:b] for a, b in seg_bounds]
    w_ba = jnp.concatenate([wt_full[4224:4232], jnp.zeros((LANES - 8, D_MODEL), BF16)], axis=0)
    w2_full, a2_full = unshard_cols(g_w2), unshard_cols(g_a2)
    zeros64 = jnp.zeros((64, RW_W), F32)
    w2p = jnp.concatenate([w2_full, zeros64], axis=0)
    a2p = jnp.concatenate([zeros64, a2_full], axis=0)
    conv_full = unshard_cols(g_conv)
    conv_rows = [conv_full[i:i + 1] for i in range(4)]
    wa_full = unshard_cols(g_wa)
    wb_full = unshard_cols(g_wb)
    wo_full = unshard_rows(g_wo)
    a_log_bc = jnp.repeat(gd_A_log, LANES, axis=1)
    dt_bias_bc = jnp.repeat(gd_dt_bias, LANES, axis=1)
    r_k_flat = rw_r_k.reshape(1, RW_W)
    now2 = norm_out_w.reshape(1, D_MODEL)

    h = _norm_in(x2, norm_in_w, tm=tm)
    p_rw = _proj("proj_rw", h, w_rw, tm=tm)
    p_zrw = _proj("proj_zrw", h, w_zrw, tm=tm)
    p_qkv = _proj("proj_qkv", h, w_qkv, tm=tm)
    p_zgd = _proj("proj_zgd", h, w_zgd, tm=tm)
    p_ba = _proj("proj_ba", h, w_ba, tm=tm)
    p_gates = _proj("proj_gates", h, w_gates, tm=tm)

    rw_params = [rw_mu, rw_w0, w2p, rw_a0, a2p, rw_k_k, rw_k_a]
    r_a, lw_a, k_a, v_a, kk_a, b_a = _pw_fwd("rwkv_prep", _rwkv_prep_f, [p_rw], 1, rw_params, [RW_W] * 6, [F32] * 6,
                                             seq=seq, tb=tb)
    gd_params = conv_rows + [a_log_bc, dt_bias_bc]
    r_b, lw_b, k_b, v_b, b_b = _pw_fwd("gdn_prep", _gdn_prep_f, [p_qkv, p_ba], 3, gd_params, [GD_W] * 5, [F32] * 5,
                                       seq=seq, tb=tb)
    rw_six, gd_six = (r_a, lw_a, k_a, v_a, kk_a, b_a), (r_b, lw_b, k_b, v_b, k_b, b_b)
    (y_rec, s_a), (o_rec, s_b) = _rec_fwd("rec", [(rw_six, 2, False), (gd_six, 1, True)], seq=seq)
    post_params = [rw_gn_w, rw_gn_b, r_k_flat]
    (y_a,) = _pw_fwd("rwkv_post", _rwkv_post_f, [y_rec, r_a, k_a, v_a, p_zrw], 0, post_params, [RW_W], [F32], seq=seq, tb=tb)
    (y_b,) = _pw_fwd("gdn_post", _gdn_post_f, [o_rec, p_zgd], 0, [gd_o_norm_w], [GD_W], [F32], seq=seq, tb=tb)

    d_ya, d_yb, d_gates, d_xo, dwa, dwb, dwo, d_now, loss_acc = _tail(
        x2, tgt2, p_gates, y_a, y_b, wa_full, wb_full, wo_full, now2, tr=min(256, n))

    (d_o, d_zgd), (d_onw,) = _pw_bwd("gdn_post_bwd", _gdn_post_f, [o_rec, p_zgd], 0, [gd_o_norm_w], [[d_yb]],
                                     [F32, BF16], seq=seq, tb=tb)
    (d_yrec, dr_p, dk_p, dv_p, d_zrw), d_post_params = _pw_bwd(
        "rwkv_post_bwd", _rwkv_post_f, [y_rec, r_a, k_a, v_a, p_zrw], 0, post_params, [[d_ya]],
        [F32, F32, F32, F32, BF16], seq=seq, tb=tb)
    (dr_a, dlw_a, dk_a, dv_a, dkk_a, db_a), (dr_b, dlw_b, dk_b, dv_b, dkk_b, db_b) = _rec_bwd(
        "rec_bwd", [(rw_six, s_a, d_yrec, 2, False), (gd_six, s_b, d_o, 1, True)], seq=seq)
    (d_qkv, d_ba), d_gd_params = _pw_bwd("gdn_prep_bwd", _gdn_prep_f, [p_qkv, p_ba], 3, gd_params,
                                         [[dr_b], [dlw_b], [dk_b, dkk_b], [dv_b], [db_b]], [BF16, BF16], seq=seq, tb=tb)
    (d_prw,), d_rw_params = _pw_bwd("rwkv_prep_bwd", _rwkv_prep_f, [p_rw], 1, rw_params,
                                    [[dr_a, dr_p], [dlw_a], [dk_a, dk_p], [dv_a, dv_p], [dkk_a], [db_a]], [BF16],
                                    seq=seq, tb=tb)

    dps = [d_prw, d_zrw, d_qkv, d_zgd, d_ba, d_gates]
    wsegs = [w_rw, w_zrw, w_qkv, w_zgd, w_ba, w_gates]
    dx2, d_gin = _proj_dx(x2, norm_in_w, d_xo, dps, wsegs, tm=min(256, n))
    dw_rw = _proj_dw("dw_rw", h, d_prw, tm=tm)
    dw_zrw = _proj_dw("dw_zrw", h, d_zrw, tm=tm)
    dw_qkv = _proj_dw("dw_qkv", h, d_qkv, tm=tm)
    dw_zgd = _proj_dw("dw_zgd", h, d_zgd, tm=tm)
    dw_ba = _proj_dw("dw_ba", h, d_ba, tm=tm)
    dw_gates = _proj_dw("dw_gates", h, d_gates, tm=tm)
    dwt_in_full = jnp.concatenate([dw_rw, dw_zrw, dw_qkv, dw_zgd, dw_ba[:8], dw_gates], axis=0)

    shard_cols = lambda a: jnp.transpose(a.reshape(a.shape[0], 4, 2, a.shape[1] // N_DEV), (2, 1, 0, 3))
    shard_rows = lambda a: jnp.transpose(a.reshape(4, 2, a.shape[0] // N_DEV, a.shape[1]), (1, 0, 2, 3))
    d_mu, d_w0, d_w2p, d_a0, d_a2p, d_kk_, d_ka_ = d_rw_params
    d_gnw, d_gnb, d_rk = d_post_params
    d_conv = jnp.concatenate(d_gd_params[:4], axis=0)
    d_alog = d_gd_params[4].reshape(4, LANES).sum(axis=1).reshape(1, 4)
    d_dtb = d_gd_params[5].reshape(4, LANES).sum(axis=1).reshape(1, 4)
    scat = [shard_rows(dwt_in_full), shard_cols(d_w2p[:64]), shard_cols(d_a2p[64:]), shard_cols(d_conv),
            shard_cols(dwa), shard_cols(dwb), shard_rows(dwo)]
    small_g = _pack_small([d_gin, d_mu, d_w0, d_a0, d_kk_, d_ka_, d_rk, d_gnw, d_gnb, d_alog, d_dtb, d_onw, d_now],
                          loss_acc[0:1])
    scat.append(jnp.stack([small_g, small_g])[:, None])
    pair = _exchange("reduce_cores", ("c",), scat, [], place_own=False)
    part = [_pair_sum("pair_sum_%d" % i, own, got, BF16 if i < 7 else F32)
            for i, (own, got) in enumerate(zip(scat, pair))]
    lands = _exchange("reduce_chips", ("x", "y"), part[:7], [part[7][0]])

    small_w = [norm_in_w, rw_mu, rw_w0, rw_a0, rw_k_k, rw_k_a, rw_r_k, rw_gn_w, rw_gn_b, gd_A_log, gd_dt_bias, gd_o_norm_w, norm_out_w]
    small_m = [m_norm_in_w, m_rw_mu, m_rw_w0, m_rw_a0, m_rw_k_k, m_rw_k_a, m_rw_r_k, m_rw_gn_w, m_rw_gn_b, m_gd_A_log, m_gd_dt_bias, m_gd_o_norm_w, m_norm_out_w]
    small_v = [v_norm_in_w, v_rw_mu, v_rw_w0, v_rw_a0, v_rw_k_k, v_rw_k_a, v_rw_r_k, v_rw_gn_w, v_rw_gn_b, v_gd_A_log, v_gd_dt_bias, v_gd_o_norm_w, v_norm_out_w]
    flat = lambda arrs: [a.reshape(1, -1) for a in arrs]
    sm, loss_row = _adam_small(lands[7], flat(small_w), flat(small_m), flat(small_v))
    sm_g, sm_d, sm_m, sm_v = [{nm: res[i].reshape(w.shape) for (nm, _), res, w in zip(_SMALL, sm, small_w)}
                              for i in range(4)]

    big = {"w_in": [o.T[None] for o in _adam("adam_w_in", lands[0], wt_own, mt_own, vt_own)]}
    for nm, land, w, m, v in (("rw_w2", lands[1], rw_w2, m_rw_w2, v_rw_w2),
                              ("rw_a2", lands[2], rw_a2, m_rw_a2, v_rw_a2),
                              ("gd_conv_w", lands[3], gd_conv_w, m_gd_conv_w, v_gd_conv_w),
                              ("w_branch_a", lands[4], w_branch_a, m_w_branch_a, v_w_branch_a),
                              ("w_branch_b", lands[5], w_branch_b, m_w_branch_b, v_w_branch_b),
                              ("w_out", lands[6], w_out, m_w_out, v_w_out)):
        big[nm] = [o.reshape(w.shape) for o in _adam("adam_" + nm, land, w[0], m[0], v[0])]

    order = ["norm_in_w", "w_in", "rw_mu", "rw_w0", "rw_w2", "rw_a0", "rw_a2", "rw_k_k", "rw_k_a", "rw_r_k", "rw_gn_w",
             "rw_gn_b", "gd_conv_w", "gd_A_log", "gd_dt_bias", "gd_o_norm_w", "w_branch_a", "w_branch_b", "w_out", "norm_out_w"]
    pick = lambda nm, i: big[nm][i] if nm in big else (sm_g, sm_d, sm_m, sm_v)[i][nm]
    loss = loss_row[0, 0]
    grad_x = dx2.reshape(x.shape)
    return (loss, grad_x, *[pick(nm, 0) for nm in order], *[pick(nm, 1) for nm in order],
            *[pick(nm, 2) for nm in order], *[pick(nm, 3) for nm in order])
```

```python
import functools

import jax
import jax.numpy as jnp
from jax import lax
from jax.experimental import pallas as pl
from jax.experimental.pallas import tpu as pltpu

F32 = jnp.float32
BF16 = jnp.bfloat16
HI = lax.Precision.HIGHEST

LANES = 128
SUB = 8
CHUNK = 64
N_DEV = 8
VMEM_LIMIT = 56 * 1024 * 1024

D_MODEL = 1024
RW_W = 512
GD_W = 512
RW_SHIFT = 1664
NORM_EPS = 1e-6
RW_GN_EPS = 64 * 1e-5
ADAM_LR, ADAM_B1, ADAM_B2, ADAM_EPS, ADAM_WD, ADAM_STEP = 0.001, 0.9, 0.999, 1e-8, 0.01, 10


_NN, _NT, _TN = ((1,), (0,)), ((1,), (1,)), ((0,), (0,))


def _dot(a, b, dims, passes):
    precision = lax.Precision.HIGH if passes == 3 else lax.Precision.DEFAULT
    return lax.dot_general(a, b, (dims, ((), ())), precision=precision, preferred_element_type=F32)


def _mm(a, b, passes=3):
    return _dot(a, b, _NN, passes)


def _mm_nt(a, b, passes=3):
    return _dot(a, b, _NT, passes)


def _mm_tn(a, b, passes=3):
    return _dot(a, b, _TN, passes)


P_SUM = 3
P_SCORE = 1
P_INV = 1
P_STATE = 1
P_APPLY = 1
P_UPDATE = 1
P_POINT = 1


def _stack_rows(blocks):
    return jnp.concatenate(blocks, axis=0)


def _split_rows(x, n):
    r = x.shape[0] // n

    @jax.custom_vjp
    def split(x):
        return tuple(x[i * r:(i + 1) * r] for i in range(n))

    split.defvjp(lambda x: (split(x), None), lambda _, gs: (jnp.concatenate(gs, axis=0),))
    return split(x)


def _iota(shape, d):
    return lax.broadcasted_iota(jnp.int32, shape, d)


def _sigmoid(x):
    return 0.5 * (jnp.tanh(0.5 * x) + 1.0)


def _silu(x):
    return x * _sigmoid(x)


def _softplus(x):
    return jnp.maximum(x, 0.0) + jnp.log(1.0 + jnp.exp(-jnp.abs(x)))


def _seg_ones(seg):
    return ((_iota((LANES, LANES), 0) // seg) == (_iota((LANES, LANES), 1) // seg)).astype(F32)


def _sl(g):
    return slice(g * LANES, (g + 1) * LANES)


@jax.custom_vjp
def _tri_inverse(ms):
    return _tri_inverse_chain(ms)


def _tri_inverse_bwd(ts, dts):
    return ([-_mm_nt(_mm_tn(t, dt, P_INV), t, P_INV) for t, dt in zip(ts, dts)],)


def _tri_inverse_chain(ms):
    c = CHUNK
    ri, ci = _iota((c, c), 0), _iota((c, c), 1)
    eye = (ri == ci).astype(F32)
    d16 = (ri // 16) == (ci // 16)
    d32 = (ri // 32) == (ci // 32)
    ps = [jnp.where(d16, -m, 0.0) for m in ms]
    ts = [eye + p for p in ps]
    for _ in range(3):
        ps = [_mm(p, p, P_INV) for p in ps]
        ts = [_mm(t, eye + p, P_INV) for t, p in zip(ts, ps)]
    for off_diagonal in (d32 & (~d16), ~d32):
        tq = [_mm(t, jnp.where(off_diagonal, m, 0.0), P_INV) for t, m in zip(ts, ms)]
        ts = [t - _mm(a, t, P_INV) for t, a in zip(ts, tq)]
    return ts


_tri_inverse.defvjp(lambda ms: (lambda ts: (ts, ts))(_tri_inverse_chain(ms)), _tri_inverse_bwd)


@jax.custom_vjp
def _known_inverse(ms, ts):
    return ts


_known_inverse.defvjp(lambda ms, ts: (ts, ts),
                      lambda ts, dts: (_tri_inverse_bwd(ts, dts)[0], [jnp.zeros_like(t) for t in ts]))


def _chunk_fwd(prims, *, nsub=None, scalar_decay=None, kinds=None, inverses=None):
    c = CHUNK
    ng = len(prims)
    kinds = kinds if kinds is not None else [(nsub, scalar_decay)] * ng
    s0s, rs, lws, ks, vs, kks, bs = [list(t) for t in zip(*prims)]
    ri, ci = _iota((c, c), 0), _iota((c, c), 1)
    incl = ri >= ci
    strict = ri > ci
    tril = incl.astype(F32)
    lane = _iota((1, LANES), 1)
    heads = [n for n, _ in kinds]
    scalar = [sc for _, sc in kinds]
    masks = [[((lane // (LANES // n)) == s).astype(F32) for s in range(n)] if n > 1 else [1.0] for n in heads]
    cws = [_mm(tril, lw, P_SUM) for lw in lws]
    cwxs = [cw - lw for cw, lw in zip(cws, lws)]
    ends = [cw[c - 1:c, :] for cw in cws]
    kkds = [kk * jnp.exp(cwx) for kk, cwx in zip(kks, cwxs)]
    rds = [r * jnp.exp(cw) for r, cw in zip(rs, cws)]
    kends = [k * jnp.exp(e - cw) for k, e, cw in zip(ks, ends, cws)]
    bends = [b * jnp.exp(e - cw) for b, e, cw in zip(bs, ends, cws)]
    state_terms = [_split_rows(_mm_nt(_stack_rows([kkd, rd]), s0, P_STATE), 2) for kkd, rd, s0 in zip(kkds, rds, s0s)]
    w0s, y0s = [t[0] for t in state_terms], [t[1] for t in state_terms]
    chains = [(g, s) for g in range(ng) for s in range(heads[g])]
    e0 = (lane == 0).astype(F32) * jnp.ones((c, 1), F32)
    rows = [_mm_nt(e0, cw, P_SUM) if sc else None for cw, sc in zip(cws, scalar)]
    dxs = [jnp.where(strict, jnp.exp(jnp.minimum(cwx[:, :c] - row, 0.0)), 0.0) if sc else None
           for cwx, row, sc in zip(cwxs, rows, scalar)]
    dis = [jnp.where(incl, jnp.exp(jnp.minimum(cw[:, :c] - row, 0.0)), 0.0) if sc else None
           for cw, row, sc in zip(cws, rows, scalar)]
    lefts = [_stack_rows([a * m for m in ms] + [q * m for m in ms])
             for a, q, ms in zip([kk if sc else kkd for kk, kkd, sc in zip(kks, kkds, scalar)],
                                 [r if sc else rd for r, rd, sc in zip(rs, rds, scalar)], masks)]
    rights_b = [b if sc else b * jnp.exp(-cw) for b, cw, sc in zip(bs, cws, scalar)]
    rights_k = [k if sc else k * jnp.exp(-cw) for k, cw, sc in zip(ks, cws, scalar)]
    on_b = [_split_rows(_mm_nt(left, right, P_SCORE), 2 * n) for left, right, n in zip(lefts, rights_b, heads)]
    on_k = [_split_rows(_mm_nt(left, right, P_SCORE), 2 * n) for left, right, n in zip(lefts, rights_k, heads)]
    lower = lambda x, g: x * dxs[g] if scalar[g] else jnp.where(strict, x, 0.0)
    lower_incl = lambda x, g: x * dis[g] if scalar[g] else jnp.where(incl, x, 0.0)
    m_b = [lower(on_b[g][s], g) for g, s in chains]
    m_k = [lower(on_k[g][s], g) for g, s in chains]
    n_k = [lower_incl(on_k[g][heads[g] + s], g) for g, s in chains]
    n_b = [lower_incl(on_b[g][heads[g] + s], g) for g, s in chains]
    t_inv = _tri_inverse(m_b) if inverses is None else _known_inverse(m_b, inverses)
    on_v = [_split_rows(_mm(_stack_rows([mk, nk]), vs[g], P_APPLY), 2) for (g, s), mk, nk in zip(chains, m_k, n_k)]
    sa_c = [_mm(t, w0s[g] + mv[0], P_APPLY) for (g, s), t, mv in zip(chains, t_inv, on_v)]
    y_c = [y0s[g] + mv[1] - _mm(nb, sa, P_APPLY) for (g, s), mv, nb, sa in zip(chains, on_v, n_b, sa_c)]
    first = [sum(heads[:g]) for g in range(ng)]
    per_group = lambda xs: [functools.reduce(lambda p, q: p + q, [xs[first[g] + s] * masks[g][s] for s in range(heads[g])])
                            for g in range(ng)]
    sas, ys = per_group(sa_c), per_group(y_c)
    s_ends = [s0 * jnp.exp(e) + _mm_tn(_stack_rows([v, -sa]), _stack_rows([kend, bend]), P_UPDATE)
              for s0, e, v, kend, sa, bend in zip(s0s, ends, vs, kends, sas, bends)]
    row_head = lambda n: _iota((LANES, LANES), 0) // (LANES // n)
    col_head = lambda n: _iota((LANES, LANES), 1) // (LANES // n)
    s_ends = [jnp.where(row_head(n) == col_head(n), s_end, 0.0) if n > 1 else s_end for s_end, n in zip(s_ends, heads)]
    return list(zip(ys, s_ends)), t_inv


def _rec_fwd(name, branches, *, seq):
    n, w = branches[0][0][0].shape
    ng = w // LANES
    nc = seq // CHUNK
    nb = n // seq
    nbr = len(branches)
    per = nb * ng
    kinds = [(heads, scalar) for _, heads, scalar in branches for _ in range(per)]
    nts = [per * heads for _, heads, _ in branches]

    def body(*refs):
        in_refs = [refs[6 * i:6 * i + 6] for i in range(nbr)]
        out_refs = [refs[6 * nbr + 3 * i:6 * nbr + 3 * i + 3] for i in range(nbr)]
        states = refs[9 * nbr:]

        @pl.when(pl.program_id(0) == 0)
        def _():
            for state in states:
                state[...] = jnp.zeros_like(state)
        where = [(i, bi, g) for i in range(nbr) for bi in range(nb) for g in range(ng)]
        prims = [(states[i][bi * ng + g],) + tuple(ref[bi, :, _sl(g)] for ref in in_refs[i]) for i, bi, g in where]
        outs, t_inv = _chunk_fwd(prims, kinds=kinds)
        for (i, bi, g), prim, (y, s_end) in zip(where, prims, outs):
            y_ref, s_ref, _ = out_refs[i]
            s_ref[0, bi * ng + g] = prim[0]
            y_ref[bi, :, _sl(g)] = y
            states[i][bi * ng + g] = s_end
        pos = 0
        for i in range(nbr):
            for j in range(nts[i]):
                out_refs[i][2][0, j] = t_inv[pos + j]
            pos += nts[i]

    row = pl.BlockSpec((nb, CHUNK, w), lambda c: (0, c, 0))
    seqs = lambda a: a.reshape(nb, seq, w)
    res = pl.pallas_call(
        body, name=name, grid=(nc,),
        in_specs=[row] * (6 * nbr),
        out_specs=[spec for nt in nts for spec in (row, pl.BlockSpec((1, per, LANES, LANES), lambda c: (c, 0, 0, 0)),
                                                   pl.BlockSpec((1, nt, CHUNK, CHUNK), lambda c: (c, 0, 0, 0)))],
        out_shape=[shp for nt in nts for shp in (jax.ShapeDtypeStruct((nb, seq, w), F32),
                                                 jax.ShapeDtypeStruct((nc, per, LANES, LANES), F32),
                                                 jax.ShapeDtypeStruct((nc, nt, CHUNK, CHUNK), F32))],
        scratch_shapes=[pltpu.VMEM((per, LANES, LANES), F32)] * nbr,
        compiler_params=pltpu.CompilerParams(dimension_semantics=("arbitrary",), vmem_limit_bytes=VMEM_LIMIT),
    )(*[seqs(a) for arrs, _, _ in branches for a in arrs])
    return [(res[3 * i].reshape(n, w), (res[3 * i + 1], res[3 * i + 2])) for i in range(nbr)]


def _rec_bwd(name, branches, *, seq):
    n, w = branches[0][0][0].shape
    ng = w // LANES
    nc = seq // CHUNK
    nb = n // seq
    nbr = len(branches)
    per = nb * ng
    kinds = [(heads, scalar) for _, _, _, heads, scalar in branches for _ in range(per)]
    nts = [per * heads for _, _, _, heads, _ in branches]

    def body(*refs):
        in_refs = [refs[9 * i:9 * i + 9] for i in range(nbr)]
        out_refs = [refs[9 * nbr + 6 * i:9 * nbr + 6 * i + 6] for i in range(nbr)]
        dstates = refs[15 * nbr:]

        @pl.when(pl.program_id(0) == 0)
        def _():
            for dstate in dstates:
                dstate[...] = jnp.zeros_like(dstate)
        where = [(i, bi, g) for i in range(nbr) for bi in range(nb) for g in range(ng)]
        inverses = [in_refs[i][7][0, j] for i in range(nbr) for j in range(nts[i])]
        f = lambda p: _chunk_fwd(p, kinds=kinds, inverses=inverses)[0]
        prims = [(in_refs[i][6][0, bi * ng + g],) + tuple(ref[bi, :, _sl(g)] for ref in in_refs[i][:6])
                 for i, bi, g in where]
        _, vjp = jax.vjp(f, prims)
        (d_prims,) = vjp([(in_refs[i][8][bi, :, _sl(g)], dstates[i][bi * ng + g]) for i, bi, g in where])
        for (i, bi, g), d_prim in zip(where, d_prims):
            dstates[i][bi * ng + g] = d_prim[0]
            for ref, d in zip(out_refs[i], d_prim[1:]):
                ref[bi, :, _sl(g)] = d

    row = pl.BlockSpec((nb, CHUNK, w), lambda c: (0, nc - 1 - c, 0))
    seqs = lambda a: a.reshape(nb, seq, w)
    in_specs, args = [], []
    for (arrs, (s_save, t_save), dy, _, _), nt in zip(branches, nts):
        in_specs += [row] * 6 + [pl.BlockSpec((1, per, LANES, LANES), lambda c: (nc - 1 - c, 0, 0, 0)),
                                 pl.BlockSpec((1, nt, CHUNK, CHUNK), lambda c: (nc - 1 - c, 0, 0, 0)), row]
        args += [seqs(a) for a in arrs] + [s_save, t_save, seqs(dy)]
    grads = pl.pallas_call(
        body, name=name, grid=(nc,),
        in_specs=in_specs,
        out_specs=[row] * (6 * nbr),
        out_shape=[jax.ShapeDtypeStruct((nb, seq, w), F32)] * (6 * nbr),
        scratch_shapes=[pltpu.VMEM((per, LANES, LANES), F32)] * nbr,
        compiler_params=pltpu.CompilerParams(dimension_semantics=("arbitrary",), vmem_limit_bytes=VMEM_LIMIT),
    )(*args)
    return [[g.reshape(n, w) for g in grads[6 * i:6 * i + 6]] for i in range(nbr)]


def _shift_down(a, j, halo, is_start):
    tb = a.shape[0]
    rolled = pltpu.roll(a, j, 0)
    hr = jnp.where(is_start, 0.0, pltpu.roll(halo, j, 0))
    first = jnp.where(_iota((SUB, LANES), 0) < j, hr, rolled[0:SUB])
    if tb == SUB:
        return first
    return jnp.concatenate([first, rolled[SUB:]], axis=0)


def _shift_up(d, j, carry, is_end):
    tb = d.shape[0]
    up = pltpu.roll(d, tb - j, 0)
    cr = jnp.where(is_end, 0.0, pltpu.roll(carry, SUB - j, 0))
    last = jnp.where(_iota((SUB, LANES), 0) >= SUB - j, cr, up[tb - SUB:tb])
    if tb == SUB:
        return last
    return jnp.concatenate([up[:tb - SUB], last], axis=0)


def _ngroups(a):
    return a.shape[1] // LANES


def _pw_fwd(name, f, ins, shift, params, out_widths, out_dtypes, *, seq, tb):
    n = ins[0].shape[0]
    nt, tps = n // tb, seq // tb
    ni, npar = len(ins), len(params)

    def body(*refs):
        in_refs = refs[:ni]
        pos = ni
        halo_ref = None
        if shift:
            halo_ref = refs[pos]
            pos += 1
        p_refs = refs[pos:pos + npar]
        out_refs = refs[pos + npar:]
        is_start = (pl.program_id(0) % tps) == 0
        tiles = [[ref[:, _sl(g)] for g in range(_ngroups(ref))] for ref in in_refs]
        prevs = [[_shift_down(tiles[0][g], j, halo_ref[:, _sl(g)], is_start) for g in range(len(tiles[0]))]
                 for j in range(1, shift + 1)]
        pv = [[ref[:, _sl(g)] for g in range(_ngroups(ref))] for ref in p_refs]
        outs = f(tiles, prevs, pv)
        for o_ref, og in zip(out_refs, outs, strict=True):
            for g, t in enumerate(og):
                o_ref[:, _sl(g)] = t.astype(o_ref.dtype)

    in_specs = [pl.BlockSpec((tb, a.shape[1]), lambda i: (i, 0)) for a in ins]
    args = list(ins)
    if shift:
        in_specs.append(pl.BlockSpec((SUB, ins[0].shape[1]), lambda i: (jnp.maximum(i * (tb // SUB) - 1, 0), 0)))
        args.append(ins[0])
    in_specs += [pl.BlockSpec(p.shape, lambda i: (0, 0)) for p in params]
    args += list(params)
    return pl.pallas_call(
        body, name=name, grid=(nt,),
        in_specs=in_specs,
        out_specs=[pl.BlockSpec((tb, w), lambda i: (i, 0)) for w in out_widths],
        out_shape=[jax.ShapeDtypeStruct((n, w), dt) for w, dt in zip(out_widths, out_dtypes, strict=True)],
        compiler_params=pltpu.CompilerParams(dimension_semantics=("parallel",), vmem_limit_bytes=VMEM_LIMIT),
    )(*args)


def _proj_pw_fwd(name, f, h, wts, shift, params, out_widths, *, seq, tb):
    n = h.shape[0]
    nt, tps = n // tb, seq // tb
    nw, npar = len(wts), len(params)

    def body(*refs):
        h_ref = refs[0]
        w_refs = refs[1:1 + nw]
        par_refs = refs[1 + nw:1 + nw + npar]
        p_refs = refs[1 + nw + npar:1 + 2 * nw + npar]
        out_refs = refs[1 + 2 * nw + npar:len(refs) - 1]
        carry = refs[-1]
        is_start = (pl.program_id(0) % tps) == 0
        for w_ref, p_ref in zip(w_refs, p_refs, strict=True):
            p_ref[...] = lax.dot_general(h_ref[...], w_ref[...], (_NT, ((), ())), preferred_element_type=F32)
        tiles = [[ref[:, _sl(g)] for g in range(_ngroups(ref))] for ref in p_refs]
        prevs = [[_shift_down(tiles[0][g], j, carry[:, _sl(g)], is_start) for g in range(len(tiles[0]))]
                 for j in range(1, shift + 1)]
        carry[...] = p_refs[0][tb - SUB:tb, :]
        pv = [[ref[:, _sl(g)] for g in range(_ngroups(ref))] for ref in par_refs]
        outs = f(tiles, prevs, pv)
        for o_ref, og in zip(out_refs, outs, strict=True):
            for g, t in enumerate(og):
                o_ref[:, _sl(g)] = t

    widths = [w.shape[0] for w in wts] + list(out_widths)
    res = pl.pallas_call(
        body, name=name, grid=(nt,),
        in_specs=([pl.BlockSpec((tb, D_MODEL), lambda i: (i, 0))] + [pl.BlockSpec(w.shape, lambda i: (0, 0)) for w in wts]
                  + [pl.BlockSpec(p.shape, lambda i: (0, 0)) for p in params]),
        out_specs=[pl.BlockSpec((tb, w), lambda i: (i, 0)) for w in widths],
        out_shape=[jax.ShapeDtypeStruct((n, w), F32) for w in widths],
        scratch_shapes=[pltpu.VMEM((SUB, wts[0].shape[0]), F32)],
        compiler_params=pltpu.CompilerParams(dimension_semantics=("arbitrary",), vmem_limit_bytes=VMEM_LIMIT),
    )(h, *wts, *params)
    return res[:nw], res[nw:]


def _pw_bwd(name, f, ins, shift, params, douts, din_dtypes, *, seq, tb):
    n = ins[0].shape[0]
    nt, tps = n // tb, seq // tb
    ni, npar = len(ins), len(params)
    flat_douts = [d for ds in douts for d in ds]
    nd = len(flat_douts)
    w0 = ins[0].shape[1]

    def body(*refs):
        in_refs = refs[:ni]
        pos = ni
        halo_ref = None
        if shift:
            halo_ref = refs[pos]
            pos += 1
        p_refs = refs[pos:pos + npar]
        pos += npar
        d_refs = refs[pos:pos + nd]
        pos += nd
        din_refs = refs[pos:pos + ni]
        pos += ni
        dp_refs = refs[pos:pos + npar]
        pos += npar
        carry = refs[pos] if shift else None
        step = pl.program_id(0)
        tile = nt - 1 - step
        is_start = (tile % tps) == 0
        is_end = (tile % tps) == tps - 1
        tiles = [[ref[:, _sl(g)] for g in range(_ngroups(ref))] for ref in in_refs]
        prevs = [[_shift_down(tiles[0][g], j, halo_ref[:, _sl(g)], is_start) for g in range(len(tiles[0]))]
                 for j in range(1, shift + 1)]
        pv = [[ref[:, _sl(g)] for g in range(_ngroups(ref))] for ref in p_refs]
        cot, pos_d = [], 0
        for ds in douts:
            grp = d_refs[pos_d:pos_d + len(ds)]
            pos_d += len(ds)
            cot.append([functools.reduce(lambda p, q: p + q, [ref[:, _sl(g)].astype(F32) for ref in grp])
                        for g in range(_ngroups(grp[0]))])
        _, vjp = jax.vjp(f, tiles, prevs, pv)
        d_tiles, d_prevs, d_pv = vjp(cot)
        for g in range(len(tiles[0])):
            for j in range(1, shift + 1):
                d_tiles[0][g] = d_tiles[0][g] + _shift_up(d_prevs[j - 1][g], j, carry[j - 1, :, _sl(g)], is_end)
            for j in range(1, shift + 1):
                carry[j - 1, :, _sl(g)] = d_prevs[j - 1][g][0:SUB]
        for ref, dg in zip(din_refs, d_tiles, strict=True):
            for g, t in enumerate(dg):
                ref[:, _sl(g)] = t.astype(ref.dtype)

        @pl.when(step == 0)
        def _():
            for ref in dp_refs:
                ref[...] = jnp.zeros_like(ref)
        for ref, dg in zip(dp_refs, d_pv, strict=True):
            for g, t in enumerate(dg):
                ref[:, _sl(g)] += t

    rev = lambda i: (nt - 1 - i, 0)
    in_specs = [pl.BlockSpec((tb, a.shape[1]), rev) for a in ins]
    args = list(ins)
    if shift:
        in_specs.append(pl.BlockSpec((SUB, w0), lambda i: (jnp.maximum((nt - 1 - i) * (tb // SUB) - 1, 0), 0)))
        args.append(ins[0])
    in_specs += [pl.BlockSpec(p.shape, lambda i: (0, 0)) for p in params]
    args += list(params)
    in_specs += [pl.BlockSpec((tb, d.shape[1]), rev) for d in flat_douts]
    args += flat_douts
    out_specs = [pl.BlockSpec((tb, a.shape[1]), rev) for a in ins] + [pl.BlockSpec(p.shape, lambda i: (0, 0)) for p in params]
    out_shape = ([jax.ShapeDtypeStruct(a.shape, dt) for a, dt in zip(ins, din_dtypes, strict=True)]
                 + [jax.ShapeDtypeStruct(p.shape, F32) for p in params])
    res = pl.pallas_call(
        body, name=name, grid=(nt,),
        in_specs=in_specs, out_specs=out_specs, out_shape=out_shape,
        scratch_shapes=[pltpu.VMEM((shift, SUB, w0), F32)] if shift else [],
        compiler_params=pltpu.CompilerParams(dimension_semantics=("arbitrary",), vmem_limit_bytes=VMEM_LIMIT),
    )(*args)
    return res[:ni], res[ni:]


def _rwkv_prep_f(tiles, prevs, params):
    (p,), (prev,) = tiles, prevs
    mu, w0, w2p, a0, a2p, k_k, k_a = params
    xs = [p[g] + (prev[g] - p[g]) * mu[g] for g in range(13)]
    wdad = xs[12]
    tw = jnp.tanh(wdad)
    e64 = _seg_ones(64)
    r, lw, k2, v, kk, b = [], [], [], [], [], []
    for g in range(4):
        k_g = xs[4 + g]
        lo = w0[g] + _mm(tw, w2p[g], P_POINT)
        lw_g = -jnp.exp(-_softplus(-lo) - 0.5)
        a_g = _sigmoid(a0[g] + _mm(wdad, a2p[g], P_POINT))
        kkp = k_g * k_k[g]
        kk_g = kkp * lax.rsqrt(_mm(kkp * kkp, e64, P_POINT) + 1e-12)
        r.append(xs[g])
        lw.append(lw_g)
        k2.append(k_g * (1.0 + (a_g - 1.0) * k_a[g]))
        v.append(xs[8 + g])
        kk.append(kk_g)
        b.append(kk_g * a_g)
    return [r, lw, k2, v, kk, b]


def _rwkv_post_f(tiles, prevs, params):
    yrec, r, k2, v, z = tiles
    gn_w, gn_b, r_k = params
    e64 = _seg_ones(64)
    out = []
    for g in range(4):
        mean = _mm(yrec[g], e64, P_POINT) * (1.0 / 64)
        d = yrec[g] - mean
        var = _mm(d * d, e64, P_POINT) * (1.0 / 64)
        yn = d * lax.rsqrt(var + RW_GN_EPS) * gn_w[g] + gn_b[g]
        bonus = _mm(r[g] * k2[g] * r_k[g], e64, P_POINT) * v[g]
        out.append((yn + bonus) * _silu(z[g]))
    return [out]


def _gdn_prep_f(tiles, prevs, params):
    x, (ba,) = tiles
    p1, p2, p3 = prevs
    cw0, cw1, cw2, cw3, a_log, dt_bias = params
    s = [_silu(cw3[g] * x[g] + cw2[g] * p1[g] + cw1[g] * p2[g] + cw0[g] * p3[g]) for g in range(12)]
    row = _iota((LANES, LANES), 0)
    r, lw, k, vv, b = [], [], [], [], []
    for h in range(4):
        q_h, k_h, v_h = s[h], s[4 + h], s[8 + h]
        qn = q_h * lax.rsqrt(jnp.sum(q_h * q_h, axis=-1, keepdims=True) + 1e-12)
        kn = k_h * lax.rsqrt(jnp.sum(k_h * k_h, axis=-1, keepdims=True) + 1e-12)
        beta = _sigmoid(_mm(ba, (row == h).astype(F32)))
        alpha = _mm(ba, (row == 4 + h).astype(F32))
        g_h = -jnp.exp(a_log[h]) * _softplus(alpha + dt_bias[h])
        r.append(qn * (LANES ** -0.5))
        lw.append(g_h)
        k.append(kn)
        vv.append(beta * v_h)
        b.append(jnp.exp(g_h) * beta * kn)
    return [r, lw, k, vv, b]


def _gdn_post_f(tiles, prevs, params):
    o, z = tiles
    ((onw,),) = params
    out = []
    for h in range(4):
        ms = jnp.mean(o[h] * o[h], axis=-1, keepdims=True)
        out.append(o[h] * lax.rsqrt(ms + NORM_EPS) * onw * _silu(z[h]))
    return [out]


def _norm_in(x2, g_in, *, tm):
    n = x2.shape[0]

    def body(x_ref, g_ref, h_ref):
        x = x_ref[...]
        rs = lax.rsqrt(jnp.mean(x * x, axis=-1, keepdims=True) + NORM_EPS)
        h_ref[...] = (x * rs * g_ref[...]).astype(BF16)

    return pl.pallas_call(
        body, name="norm_in", grid=(n // tm,),
        in_specs=[pl.BlockSpec((tm, D_MODEL), lambda i: (i, 0)), pl.BlockSpec((1, D_MODEL), lambda i: (0, 0))],
        out_specs=pl.BlockSpec((tm, D_MODEL), lambda i: (i, 0)),
        out_shape=jax.ShapeDtypeStruct((n, D_MODEL), BF16),
        compiler_params=pltpu.CompilerParams(dimension_semantics=("parallel",), vmem_limit_bytes=VMEM_LIMIT),
    )(x2, g_in)


def _proj(name, h, wt, *, tm):
    n, ws = h.shape[0], wt.shape[0]

    def body(h_ref, w_ref, o_ref):
        o_ref[...] = lax.dot_general(h_ref[...], w_ref[...], (_NT, ((), ())), preferred_element_type=F32)

    return pl.pallas_call(
        body, name=name, grid=(n // tm,),
        in_specs=[pl.BlockSpec((tm, D_MODEL), lambda i: (i, 0)), pl.BlockSpec((ws, D_MODEL), lambda i: (0, 0))],
        out_specs=pl.BlockSpec((tm, ws), lambda i: (i, 0)),
        out_shape=jax.ShapeDtypeStruct((n, ws), F32),
        compiler_params=pltpu.CompilerParams(dimension_semantics=("parallel",), vmem_limit_bytes=VMEM_LIMIT),
    )(h, wt)


def _proj_dw(name, h, dp, *, tm):
    n, ws = dp.shape

    def body(h_ref, d_ref, o_ref):
        @pl.when(pl.program_id(0) == 0)
        def _():
            o_ref[...] = jnp.zeros_like(o_ref)
        o_ref[...] += lax.dot_general(d_ref[...], h_ref[...], (_TN, ((), ())), preferred_element_type=F32)

    return pl.pallas_call(
        body, name=name, grid=(n // tm,),
        in_specs=[pl.BlockSpec((tm, D_MODEL), lambda i: (i, 0)), pl.BlockSpec((tm, ws), lambda i: (i, 0))],
        out_specs=pl.BlockSpec((ws, D_MODEL), lambda i: (0, 0)),
        out_shape=jax.ShapeDtypeStruct((ws, D_MODEL), F32),
        compiler_params=pltpu.CompilerParams(dimension_semantics=("arbitrary",), vmem_limit_bytes=VMEM_LIMIT),
    )(h, dp)


def _proj_dx(x2, g_in, d_xo, dps, ws, *, tm):
    n = x2.shape[0]
    ns = len(dps)

    def body(*refs):
        x_ref, g_ref, dxo_ref = refs[:3]
        dp_refs = refs[3:3 + ns]
        w_refs = refs[3 + ns:3 + 2 * ns]
        dx_ref, dg_ref = refs[3 + 2 * ns:]
        dh = jnp.zeros((tm, D_MODEL), F32)
        for d_ref, w_ref in zip(dp_refs, w_refs, strict=True):
            dh = dh + jnp.dot(d_ref[...], w_ref[...], preferred_element_type=F32)
        x = x_ref[...]
        rs = lax.rsqrt(jnp.mean(x * x, axis=-1, keepdims=True) + NORM_EPS)
        xn = x * rs
        dxn = dh * g_ref[...]
        dx_ref[...] = dxo_ref[...] + rs * (dxn - xn * jnp.mean(dxn * xn, axis=-1, keepdims=True))

        @pl.when(pl.program_id(0) == 0)
        def _():
            dg_ref[...] = jnp.zeros_like(dg_ref)
        dg_ref[...] += jnp.sum(dh * xn, axis=0, keepdims=True)

    row = pl.BlockSpec((tm, D_MODEL), lambda i: (i, 0))
    return pl.pallas_call(
        body, name="proj_dx", grid=(n // tm,),
        in_specs=([row, pl.BlockSpec((1, D_MODEL), lambda i: (0, 0)), row]
                  + [pl.BlockSpec((tm, d.shape[1]), lambda i: (i, 0)) for d in dps]
                  + [pl.BlockSpec(w.shape, lambda i: (0, 0)) for w in ws]),
        out_specs=[row, pl.BlockSpec((1, D_MODEL), lambda i: (0, 0))],
        out_shape=[jax.ShapeDtypeStruct((n, D_MODEL), F32), jax.ShapeDtypeStruct((1, D_MODEL), F32)],
        compiler_params=pltpu.CompilerParams(dimension_semantics=("arbitrary",), vmem_limit_bytes=VMEM_LIMIT),
    )(x2, g_in, d_xo, *dps, *ws)


def _tail(x2, tgt2, gates, ya, yb, w_a, w_b, w_o, now, *, tr):
    n = x2.shape[0]

    def body(x_ref, t_ref, g_ref, ya_ref, yb_ref, wa_ref, wb_ref, wo_ref, now_ref,
             dya_ref, dyb_ref, dg_ref, dxo_ref, dwa_ref, dwb_ref, dwo_ref, dnow_ref, loss_ref):
        ya16, yb16 = ya_ref[...].astype(BF16), yb_ref[...].astype(BF16)
        ua = jnp.dot(ya16, wa_ref[...], preferred_element_type=F32)
        ub = jnp.dot(yb16, wb_ref[...], preferred_element_type=F32)
        ga = _sigmoid(g_ref[:, :D_MODEL])
        gb = _sigmoid(g_ref[:, D_MODEL:])
        m16 = (ga * ua + gb * ub).astype(BF16)
        xo = x_ref[...] + jnp.dot(m16, wo_ref[...], preferred_element_type=F32)
        rs = lax.rsqrt(jnp.mean(xo * xo, axis=-1, keepdims=True) + NORM_EPS)
        yn = xo * rs
        now_v = now_ref[...]
        err = yn * now_v - t_ref[...]
        dy = err * (1.0 / D_MODEL)
        dyn = dy * now_v
        dxo = rs * (dyn - yn * jnp.mean(dyn * yn, axis=-1, keepdims=True))
        dxo_ref[...] = dxo
        dxo16 = dxo.astype(BF16)
        dm = lax.dot_general(dxo16, wo_ref[...], (((1,), (1,)), ((), ())), preferred_element_type=F32)
        dua16 = (dm * ga).astype(BF16)
        dub16 = (dm * gb).astype(BF16)
        dg_ref[:, :D_MODEL] = (dm * ua * ga * (1.0 - ga)).astype(dg_ref.dtype)
        dg_ref[:, D_MODEL:] = (dm * ub * gb * (1.0 - gb)).astype(dg_ref.dtype)
        dya_ref[...] = lax.dot_general(dua16, wa_ref[...], (((1,), (1,)), ((), ())), preferred_element_type=F32)
        dyb_ref[...] = lax.dot_general(dub16, wb_ref[...], (((1,), (1,)), ((), ())), preferred_element_type=F32)

        @pl.when(pl.program_id(0) == 0)
        def _():
            for ref in (dwa_ref, dwb_ref, dwo_ref, dnow_ref, loss_ref):
                ref[...] = jnp.zeros_like(ref)
        tn = (((0,), (0,)), ((), ()))
        dwo_ref[...] += lax.dot_general(m16, dxo16, tn, preferred_element_type=F32)
        dwa_ref[...] += lax.dot_general(ya16, dua16, tn, preferred_element_type=F32)
        dwb_ref[...] += lax.dot_general(yb16, dub16, tn, preferred_element_type=F32)
        dnow_ref[...] += jnp.sum(dy * yn, axis=0, keepdims=True)
        loss_ref[...] += (0.5 / D_MODEL) * jnp.sum(err * err)

    row = lambda w: pl.BlockSpec((tr, w), lambda i: (i, 0))
    full = lambda a: pl.BlockSpec(a.shape, lambda i: (0, 0))
    return pl.pallas_call(
        body, name="tail", grid=(n // tr,),
        in_specs=[row(D_MODEL), row(D_MODEL), row(2 * D_MODEL), row(RW_W), row(GD_W), full(w_a), full(w_b), full(w_o), full(now)],
        out_specs=[row(RW_W), row(GD_W), row(2 * D_MODEL), row(D_MODEL),
                   pl.BlockSpec((RW_W, D_MODEL), lambda i: (0, 0)), pl.BlockSpec((GD_W, D_MODEL), lambda i: (0, 0)),
                   pl.BlockSpec((D_MODEL, D_MODEL), lambda i: (0, 0)), pl.BlockSpec((1, D_MODEL), lambda i: (0, 0)),
                   pl.BlockSpec((SUB, LANES), lambda i: (0, 0))],
        out_shape=[jax.ShapeDtypeStruct((n, RW_W), F32), jax.ShapeDtypeStruct((n, GD_W), F32),
                   jax.ShapeDtypeStruct((n, 2 * D_MODEL), BF16), jax.ShapeDtypeStruct((n, D_MODEL), F32),
                   jax.ShapeDtypeStruct((RW_W, D_MODEL), F32), jax.ShapeDtypeStruct((GD_W, D_MODEL), F32),
                   jax.ShapeDtypeStruct((D_MODEL, D_MODEL), F32), jax.ShapeDtypeStruct((1, D_MODEL), F32),
                   jax.ShapeDtypeStruct((SUB, LANES), F32)],
        compiler_params=pltpu.CompilerParams(dimension_semantics=("arbitrary",), vmem_limit_bytes=VMEM_LIMIT),
    )(x2, tgt2, gates, ya, yb, w_a, w_b, w_o, now)


def _exchange(name, axes, scatter, gather, place_own=True):
    ns, ng = len(scatter), len(gather)
    na = ns + ng
    gs = 2 ** len(axes)
    arrs = list(scatter) + list(gather)

    def body(*refs):
        src = refs[:na]
        dst = refs[na:2 * na]
        send_sems, recv_sems = refs[2 * na:]
        mine = {ax: lax.axis_index(ax) for ax in ("x", "y", "c")}

        def peer(k):
            co = dict(mine)
            for i, ax in enumerate(axes):
                if (k >> (len(axes) - 1 - i)) & 1:
                    co[ax] = 1 - co[ax]
            idx = 0
            for ax in axes:
                idx = 2 * idx + co[ax]
            return (co["x"], co["y"], co["c"]), idx

        _, me = peer(0)

        def copy(a, k, landing):
            dev, idx = peer(k)
            s = src[a].at[idx] if a < ns else src[a]
            return pltpu.make_async_remote_copy(src_ref=s, dst_ref=dst[a].at[idx if landing else me],
                                                send_sem=send_sems.at[a, k - 1], recv_sem=recv_sems.at[a, k - 1],
                                                device_id=dev, device_id_type=pl.DeviceIdType.MESH)

        sends = [copy(a, k, False) for a in range(na) for k in range(1, gs)]
        for cp in sends:
            cp.start()
        for a in range(na):
            for k in range(1, gs):
                copy(a, k, True).wait_recv()
        for cp in sends:
            cp.wait_send()

    out_shape = [jax.ShapeDtypeStruct(a.shape, a.dtype) for a in scatter] + \
                [jax.ShapeDtypeStruct((gs,) + a.shape, a.dtype) for a in gather]
    anyspec = pl.BlockSpec(memory_space=pl.ANY)
    lands = pl.pallas_call(
        body, name=name,
        in_specs=[anyspec] * na, out_specs=[anyspec] * na, out_shape=out_shape,
        scratch_shapes=[pltpu.SemaphoreType.DMA((na, gs - 1)), pltpu.SemaphoreType.DMA((na, gs - 1))],
    )(*arrs)
    if not place_own:
        return lands
    me = 0
    for ax in axes:
        me = 2 * me + lax.axis_index(ax)
    kept = [lax.dynamic_index_in_dim(a, me, 0, keepdims=False) for a in scatter] + list(gather)
    return [lax.dynamic_update_index_in_dim(land, mine, me, 0) for land, mine in zip(lands, kept)]


def _gather_all(name, arrs):
    na = len(arrs)

    def body(*refs):
        src = refs[:na]
        dst = refs[na:2 * na]
        send_sems, recv_sems = refs[2 * na:]
        x, y, c = lax.axis_index("x"), lax.axis_index("y"), lax.axis_index("c")
        sibling = (x, y, 1 - c)
        chips = [(1 - x, y), (x, 1 - y), (1 - x, 1 - y)]

        def copy(a, k, block, to, own=False):
            px, py, pc = block
            slot = dst[a].at[pc, 2 * px + py]
            return pltpu.make_async_remote_copy(src_ref=src[a] if own else slot, dst_ref=slot,
                                                send_sem=send_sems.at[a, k], recv_sem=recv_sems.at[a, k],
                                                device_id=to, device_id_type=pl.DeviceIdType.MESH)

        first = [copy(a, 0, (x, y, c), sibling, own=True) for a in range(na)]
        first += [copy(a, 1 + j, (x, y, c), (*chip, c), own=True) for j, chip in enumerate(chips) for a in range(na)]
        for cp in first:
            cp.start()
        passed = []
        for j, chip in enumerate(chips):
            for a in range(na):
                copy(a, 1 + j, (*chip, c), (x, y, c)).wait_recv()
                passed.append(copy(a, 4 + j, (*chip, c), sibling))
                passed[-1].start()
        for a in range(na):
            copy(a, 0, (x, y, 1 - c), (x, y, c)).wait_recv()
            for j, chip in enumerate(chips):
                copy(a, 4 + j, (*chip, 1 - c), (x, y, c)).wait_recv()
        for cp in first + passed:
            cp.wait_send()

    anyspec = pl.BlockSpec(memory_space=pl.ANY)
    lands = pl.pallas_call(
        body, name=name,
        in_specs=[anyspec] * na, out_specs=[anyspec] * na,
        out_shape=[jax.ShapeDtypeStruct((2, 4) + a.shape, a.dtype) for a in arrs],
        scratch_shapes=[pltpu.SemaphoreType.DMA((na, 7)), pltpu.SemaphoreType.DMA((na, 7))],
    )(*arrs)
    core, chip = lax.axis_index("c"), 2 * lax.axis_index("x") + lax.axis_index("y")
    zero = jnp.zeros((), jnp.int32)
    return [lax.dynamic_update_slice(land, mine[None, None], (core, chip) + (zero,) * mine.ndim)
            for land, mine in zip(lands, arrs)]


def _pair_sum(name, own, land, out_dtype):
    _, nq, r, c = own.shape
    core = lax.axis_index("c").astype(jnp.int32).reshape(1)

    def body(core_ref, own_ref, land_ref, o_ref):
        o_ref[0] = (own_ref[0, 0] + land_ref[0, 0]).astype(o_ref.dtype)

    return pl.pallas_call(
        body, name=name,
        grid_spec=pltpu.PrefetchScalarGridSpec(
            num_scalar_prefetch=1, grid=(nq,),
            in_specs=[pl.BlockSpec((1, 1, r, c), lambda i, core_ref: (core_ref[0], i, 0, 0)),
                      pl.BlockSpec((1, 1, r, c), lambda i, core_ref: (1 - core_ref[0], i, 0, 0))],
            out_specs=pl.BlockSpec((1, r, c), lambda i, core_ref: (i, 0, 0))),
        out_shape=jax.ShapeDtypeStruct((nq, r, c), out_dtype),
        compiler_params=pltpu.CompilerParams(dimension_semantics=("parallel",), vmem_limit_bytes=VMEM_LIMIT),
    )(core, own, land)


def _adam(name, land, w, m, v):
    r, c = w.shape
    nslot = land.shape[0]
    tr = 256 if (r % 256 == 0 and r > 256) else r
    tc = 256 if (tr == r and r > 256 and c % 256 == 0) else c

    def body(l_ref, w_ref, m_ref, v_ref, g_out, d_out, m_out, v_out):
        g = l_ref[0].astype(F32)
        for s in range(1, nslot):
            g = g + l_ref[s].astype(F32)
        g_out[...] = g
        d_out[...], m_out[...], v_out[...] = _adam_math(g, w_ref[...], m_ref[...], v_ref[...])

    blk = pl.BlockSpec((tr, tc), lambda i: (i * tc // c, i % (c // tc)))
    return pl.pallas_call(
        body, name=name, grid=((r // tr) * (c // tc),),
        in_specs=[pl.BlockSpec((nslot, tr, tc), lambda i: (0, i * tc // c, i % (c // tc))), blk, blk, blk],
        out_specs=[blk] * 4,
        out_shape=[jax.ShapeDtypeStruct((r, c), F32)] * 4,
        compiler_params=pltpu.CompilerParams(dimension_semantics=("parallel",), vmem_limit_bytes=VMEM_LIMIT),
    )(land, w, m, v)


def _adam_math(g, w, m, v):
    c1 = 1.0 / (1.0 - ADAM_B1 ** ADAM_STEP)
    c2 = 1.0 / (1.0 - ADAM_B2 ** ADAM_STEP)
    m_new = ADAM_B1 * m + (1.0 - ADAM_B1) * g
    v_new = ADAM_B2 * v + (1.0 - ADAM_B2) * (g * g)
    return -ADAM_LR * ((m_new * c1) / (jnp.sqrt(v_new * c2) + ADAM_EPS) + ADAM_WD * w), m_new, v_new


def _adam_small(land, ws, ms, vs):
    npar = len(ws)
    nslot = land.shape[0]

    def body(*refs):
        l_ref = refs[0]
        w_refs, m_refs, v_refs = refs[1:1 + npar], refs[1 + npar:1 + 2 * npar], refs[1 + 2 * npar:1 + 3 * npar]
        outs = refs[1 + 3 * npar:1 + 7 * npar]
        loss_ref, g_rows = refs[1 + 7 * npar], refs[2 + 7 * npar]
        g = l_ref[0]
        for s in range(1, nslot):
            g = g + l_ref[s]
        g_rows[...] = g
        row = 0
        for i, (_, size) in enumerate(_SMALL):
            for j in range(-(-size // LANES)):
                width = min(LANES, size - j * LANES)
                cols = slice(j * LANES, j * LANES + width)
                g_ij = g_rows[row:row + 1, 0:width]
                delta, m_new, v_new = _adam_math(g_ij, w_refs[i][:, cols], m_refs[i][:, cols], v_refs[i][:, cols])
                for ref, val in zip(outs[4 * i:4 * i + 4], (g_ij, delta, m_new, v_new)):
                    ref[:, cols] = val
                row += 1
        loss_ref[...] = g_rows[row:row + 1, :]

    full = lambda a: pl.BlockSpec(a.shape, lambda: (0,) * a.ndim)
    res = pl.pallas_call(
        body, name="adam_small",
        in_specs=[full(land)] + [full(a) for a in list(ws) + list(ms) + list(vs)],
        out_specs=[full(w) for w in ws for _ in range(4)] + [pl.BlockSpec((1, LANES), lambda: (0, 0))],
        out_shape=[jax.ShapeDtypeStruct(w.shape, F32) for w in ws for _ in range(4)] + [jax.ShapeDtypeStruct((1, LANES), F32)],
        scratch_shapes=[pltpu.VMEM(land.shape[1:], F32)],
    )(land, *ws, *ms, *vs)
    return [res[4 * i:4 * i + 4] for i in range(npar)], res[4 * npar]


_SMALL = (("norm_in_w", 1024), ("rw_mu", 1664), ("rw_w0", 512), ("rw_a0", 512), ("rw_k_k", 512), ("rw_k_a", 512),
          ("rw_r_k", 512), ("rw_gn_w", 512), ("rw_gn_b", 512), ("gd_A_log", 4), ("gd_dt_bias", 4), ("gd_o_norm_w", 128),
          ("norm_out_w", 1024))
_SMALL_ROWS = 64


def _pack_small(vals, loss_row):
    rows = []
    for (_, size), a in zip(_SMALL, vals, strict=True):
        flat = a.reshape(-1).astype(F32)
        pad = (-size) % LANES
        if pad:
            flat = jnp.concatenate([flat, jnp.zeros((pad,), F32)])
        rows.append(flat.reshape(-1, LANES))
    rows.append(loss_row)
    used = sum(r.shape[0] for r in rows)
    rows.append(jnp.zeros((_SMALL_ROWS - used, LANES), F32))
    return jnp.concatenate(rows, axis=0)


def kernel(x, norm_in_w, w_in, rw_mu, rw_w0, rw_w2, rw_a0, rw_a2, rw_k_k, rw_k_a, rw_r_k, rw_gn_w, rw_gn_b, gd_conv_w, gd_A_log, gd_dt_bias, gd_o_norm_w, w_branch_a, w_branch_b, w_out, norm_out_w, loss_target, m_norm_in_w, m_w_in, m_rw_mu, m_rw_w0, m_rw_w2, m_rw_a0, m_rw_a2, m_rw_k_k, m_rw_k_a, m_rw_r_k, m_rw_gn_w, m_rw_gn_b, m_gd_conv_w, m_gd_A_log, m_gd_dt_bias, m_gd_o_norm_w, m_w_branch_a, m_w_branch_b, m_w_out, m_norm_out_w, v_norm_in_w, v_w_in, v_rw_mu, v_rw_w0, v_rw_w2, v_rw_a0, v_rw_a2, v_rw_k_k, v_rw_k_a, v_rw_r_k, v_rw_gn_w, v_rw_gn_b, v_gd_conv_w, v_gd_A_log, v_gd_dt_bias, v_gd_o_norm_w, v_w_branch_a, v_w_branch_b, v_w_out, v_norm_out_w):
    nb, seq, _ = x.shape
    n = nb * seq
    tm = min(1024, n)
    tb = min(512, seq)
    x2 = x.reshape(n, D_MODEL)
    tgt2 = loss_target.reshape(n, D_MODEL)
    cols = w_in.shape[2]
    in_cols = cols * N_DEV

    wt_own, mt_own, vt_own = w_in[0].T, m_w_in[0].T, v_w_in[0].T
    sharded = [wt_own.astype(BF16), rw_w2[0], rw_a2[0], gd_conv_w[0], w_branch_a[0].astype(BF16),
               w_branch_b[0].astype(BF16), w_out[0].astype(BF16)]
    g_win, g_w2, g_a2, g_conv, g_wa, g_wb, g_wo = _gather_all("gather_weights", sharded)
    unshard_rows = lambda a: jnp.transpose(a, (1, 0, 2, 3)).reshape(N_DEV * a.shape[2], a.shape[3])
    unshard_cols = lambda a: jnp.transpose(a, (2, 1, 0, 3)).reshape(a.shape[2], N_DEV * a.shape[3])
    wt_full = unshard_rows(g_win)
    seg_bounds = ((0, 1664), (1664, 2176), (2176, 3712), (3712, 4224), (4232, in_cols))
    w_rw, w_zrw, w_qkv, w_zgd, w_gates = [wt_full[a:b] for a, b in seg_bounds]
    w_ba = jnp.concatenate([wt_full[4224:4232], jnp.zeros((LANES - 8, D_MODEL), BF16)], axis=0)
    w2_full, a2_full = unshard_cols(g_w2), unshard_cols(g_a2)
    zeros64 = jnp.zeros((64, RW_W), F32)
    w2p = jnp.concatenate([w2_full, zeros64], axis=0)
    a2p = jnp.concatenate([zeros64, a2_full], axis=0)
    conv_full = unshard_cols(g_conv)
    conv_rows = [conv_full[i:i + 1] for i in range(4)]
    wa_full = unshard_cols(g_wa)
    wb_full = unshard_cols(g_wb)
    wo_full = unshard_rows(g_wo)
    a_log_bc = jnp.repeat(gd_A_log, LANES, axis=1)
    dt_bias_bc = jnp.repeat(gd_dt_bias, LANES, axis=1)
    r_k_flat = rw_r_k.reshape(1, RW_W)
    now2 = norm_out_w.reshape(1, D_MODEL)

    h = _norm_in(x2, norm_in_w, tm=tm)
    p_zrw = _proj("proj_zrw", h, w_zrw, tm=tm)
    p_zgd = _proj("proj_zgd", h, w_zgd, tm=tm)
    p_gates = _proj("proj_gates", h, w_gates, tm=tm)
    rw_params = [rw_mu, rw_w0, w2p, rw_a0, a2p, rw_k_k, rw_k_a]
    (p_rw,), (r_a, lw_a, k_a, v_a, kk_a, b_a) = _proj_pw_fwd("proj_rwkv_prep", _rwkv_prep_f, h, [w_rw], 1, rw_params,
                                                             [RW_W] * 6, seq=seq, tb=tb)
    gd_params = conv_rows + [a_log_bc, dt_bias_bc]
    (p_qkv, p_ba), (r_b, lw_b, k_b, v_b, b_b) = _proj_pw_fwd("proj_gdn_prep", _gdn_prep_f, h, [w_qkv, w_ba], 3, gd_params,
                                                             [GD_W] * 5, seq=seq, tb=tb)
    rw_six, gd_six = (r_a, lw_a, k_a, v_a, kk_a, b_a), (r_b, lw_b, k_b, v_b, k_b, b_b)
    (y_rec, s_a), (o_rec, s_b) = _rec_fwd("rec", [(rw_six, 2, False), (gd_six, 1, True)], seq=seq)
    post_params = [rw_gn_w, rw_gn_b, r_k_flat]
    (y_a,) = _pw_fwd("rwkv_post", _rwkv_post_f, [y_rec, r_a, k_a, v_a, p_zrw], 0, post_params, [RW_W], [F32], seq=seq, tb=tb)
    (y_b,) = _pw_fwd("gdn_post", _gdn_post_f, [o_rec, p_zgd], 0, [gd_o_norm_w], [GD_W], [F32], seq=seq, tb=tb)

    d_ya, d_yb, d_gates, d_xo, dwa, dwb, dwo, d_now, loss_acc = _tail(
        x2, tgt2, p_gates, y_a, y_b, wa_full, wb_full, wo_full, now2, tr=min(256, n))

    (d_o, d_zgd), (d_onw,) = _pw_bwd("gdn_post_bwd", _gdn_post_f, [o_rec, p_zgd], 0, [gd_o_norm_w], [[d_yb]],
                                     [F32, BF16], seq=seq, tb=tb)
    (d_yrec, dr_p, dk_p, dv_p, d_zrw), d_post_params = _pw_bwd(
        "rwkv_post_bwd", _rwkv_post_f, [y_rec, r_a, k_a, v_a, p_zrw], 0, post_params, [[d_ya]],
        [F32, F32, F32, F32, BF16], seq=seq, tb=tb)
    (dr_a, dlw_a, dk_a, dv_a, dkk_a, db_a), (dr_b, dlw_b, dk_b, dv_b, dkk_b, db_b) = _rec_bwd(
        "rec_bwd", [(rw_six, s_a, d_yrec, 2, False), (gd_six, s_b, d_o, 1, True)], seq=seq)
    (d_qkv, d_ba), d_gd_params = _pw_bwd("gdn_prep_bwd", _gdn_prep_f, [p_qkv, p_ba], 3, gd_params,
                                         [[dr_b], [dlw_b], [dk_b, dkk_b], [dv_b], [db_b]], [BF16, BF16], seq=seq, tb=tb)
    (d_prw,), d_rw_params = _pw_bwd("rwkv_prep_bwd", _rwkv_prep_f, [p_rw], 1, rw_params,
                                    [[dr_a, dr_p], [dlw_a], [dk_a, dk_p], [dv_a, dv_p], [dkk_a], [db_a]], [BF16],
                                    seq=seq, tb=tb)

    dps = [d_prw, d_zrw, d_qkv, d_zgd, d_ba, d_gates]
    wsegs = [w_rw, w_zrw, w_qkv, w_zgd, w_ba, w_gates]
    dx2, d_gin = _proj_dx(x2, norm_in_w, d_xo, dps, wsegs, tm=min(256, n))
    dw_rw = _proj_dw("dw_rw", h, d_prw, tm=tm)
    dw_zrw = _proj_dw("dw_zrw", h, d_zrw, tm=tm)
    dw_qkv = _proj_dw("dw_qkv", h, d_qkv, tm=tm)
    dw_zgd = _proj_dw("dw_zgd", h, d_zgd, tm=tm)
    dw_ba = _proj_dw("dw_ba", h, d_ba, tm=tm)
    dw_gates = _proj_dw("dw_gates", h, d_gates, tm=tm)
    dwt_in_full = jnp.concatenate([dw_rw, dw_zrw, dw_qkv, dw_zgd, dw_ba[:8], dw_gates], axis=0)

    shard_cols = lambda a: jnp.transpose(a.reshape(a.shape[0], 4, 2, a.shape[1] // N_DEV), (2, 1, 0, 3))
    shard_rows = lambda a: jnp.transpose(a.reshape(4, 2, a.shape[0] // N_DEV, a.shape[1]), (1, 0, 2, 3))
    d_mu, d_w0, d_w2p, d_a0, d_a2p, d_kk_, d_ka_ = d_rw_params
    d_gnw, d_gnb, d_rk = d_post_params
    d_conv = jnp.concatenate(d_gd_params[:4], axis=0)
    d_alog = d_gd_params[4].reshape(4, LANES).sum(axis=1).reshape(1, 4)
    d_dtb = d_gd_params[5].reshape(4, LANES).sum(axis=1).reshape(1, 4)
    scat = [shard_rows(dwt_in_full), shard_cols(d_w2p[:64]), shard_cols(d_a2p[64:]), shard_cols(d_conv),
            shard_cols(dwa), shard_cols(dwb), shard_rows(dwo)]
    small_g = _pack_small([d_gin, d_mu, d_w0, d_a0, d_kk_, d_ka_, d_rk, d_gnw, d_gnb, d_alog, d_dtb, d_onw, d_now],
                          loss_acc[0:1])
    scat.append(jnp.stack([small_g, small_g])[:, None])
    pair = _exchange("reduce_cores", ("c",), scat, [], place_own=False)
    part = [_pair_sum("pair_sum_%d" % i, own, got, BF16 if i < 7 else F32)
            for i, (own, got) in enumerate(zip(scat, pair))]
    lands = _exchange("reduce_chips", ("x", "y"), part[:7], [part[7][0]])

    small_w = [norm_in_w, rw_mu, rw_w0, rw_a0, rw_k_k, rw_k_a, rw_r_k, rw_gn_w, rw_gn_b, gd_A_log, gd_dt_bias, gd_o_norm_w, norm_out_w]
    small_m = [m_norm_in_w, m_rw_mu, m_rw_w0, m_rw_a0, m_rw_k_k, m_rw_k_a, m_rw_r_k, m_rw_gn_w, m_rw_gn_b, m_gd_A_log, m_gd_dt_bias, m_gd_o_norm_w, m_norm_out_w]
    small_v = [v_norm_in_w, v_rw_mu, v_rw_w0, v_rw_a0, v_rw_k_k, v_rw_k_a, v_rw_r_k, v_rw_gn_w, v_rw_gn_b, v_gd_A_log, v_gd_dt_bias, v_gd_o_norm_w, v_norm_out_w]
    flat = lambda arrs: [a.reshape(1, -1) for a in arrs]
    sm, loss_row = _adam_small(lands[7], flat(small_w), flat(small_m), flat(small_v))
    sm_g, sm_d, sm_m, sm_v = [{nm: res[i].reshape(w.shape) for (nm, _), res, w in zip(_SMALL, sm, small_w)}
                              for i in range(4)]

    big = {"w_in": [o.T[None] for o in _adam("adam_w_in", lands[0], wt_own, mt_own, vt_own)]}
    for nm, land, w, m, v in (("rw_w2", lands[1], rw_w2, m_rw_w2, v_rw_w2),
                              ("rw_a2", lands[2], rw_a2, m_rw_a2, v_rw_a2),
                              ("gd_conv_w", lands[3], gd_conv_w, m_gd_conv_w, v_gd_conv_w),
                              ("w_branch_a", lands[4], w_branch_a, m_w_branch_a, v_w_branch_a),
                              ("w_branch_b", lands[5], w_branch_b, m_w_branch_b, v_w_branch_b),
                              ("w_out", lands[6], w_out, m_w_out, v_w_out)):
        big[nm] = [o.reshape(w.shape) for o in _adam("adam_" + nm, land, w[0], m[0], v[0])]

    order = ["norm_in_w", "w_in", "rw_mu", "rw_w0", "rw_w2", "rw_a0", "rw_a2", "rw_k_k", "rw_k_a", "rw_r_k", "rw_gn_w",
             "rw_gn_b", "gd_conv_w", "gd_A_log", "gd_dt_bias", "gd_o_norm_w", "w_branch_a", "w_branch_b", "w_out", "norm_out_w"]
    pick = lambda nm, i: big[nm][i] if nm in big else (sm_g, sm_d, sm_m, sm_v)[i][nm]
    loss = loss_row[0, 0]
    grad_x = dx2.reshape(x.shape)
    return (loss, grad_x, *[pick(nm, 0) for nm in order], *[pick(nm, 1) for nm in order],
            *[pick(nm, 2) for nm in order], *[pick(nm, 3) for nm in order])
```

```python
import functools

import jax
import jax.numpy as jnp
from jax import lax
from jax.experimental import pallas as pl
from jax.experimental.pallas import tpu as pltpu

F32 = jnp.float32
BF16 = jnp.bfloat16
HI = lax.Precision.HIGHEST

LANES = 128
SUB = 8
CHUNK = 64
N_DEV = 8
VMEM_LIMIT = 56 * 1024 * 1024

D_MODEL = 1024
RW_W = 512
GD_W = 512
RW_SHIFT = 1664
NORM_EPS = 1e-6
RW_GN_EPS = 64 * 1e-5
ADAM_LR, ADAM_B1, ADAM_B2, ADAM_EPS, ADAM_WD, ADAM_STEP = 0.001, 0.9, 0.999, 1e-8, 0.01, 10


_NN, _NT, _TN = ((1,), (0,)), ((1,), (1,)), ((0,), (0,))


def _dot(a, b, dims, passes):
    precision = lax.Precision.HIGH if passes == 3 else lax.Precision.DEFAULT
    return lax.dot_general(a, b, (dims, ((), ())), precision=precision, preferred_element_type=F32)


def _mm(a, b, passes=3):
    return _dot(a, b, _NN, passes)


def _mm_nt(a, b, passes=3):
    return _dot(a, b, _NT, passes)


def _mm_tn(a, b, passes=3):
    return _dot(a, b, _TN, passes)


P_SUM = 3
P_SCORE = 1
P_INV = 1
P_STATE = 1
P_APPLY = 1
P_UPDATE = 1
P_POINT = 1


def _stack_rows(blocks):
    return jnp.concatenate(blocks, axis=0)


def _split_rows(x, n):
    r = x.shape[0] // n

    @jax.custom_vjp
    def split(x):
        return tuple(x[i * r:(i + 1) * r] for i in range(n))

    split.defvjp(lambda x: (split(x), None), lambda _, gs: (jnp.concatenate(gs, axis=0),))
    return split(x)


def _iota(shape, d):
    return lax.broadcasted_iota(jnp.int32, shape, d)


def _sigmoid(x):
    return 0.5 * (jnp.tanh(0.5 * x) + 1.0)


def _silu(x):
    return x * _sigmoid(x)


def _softplus(x):
    return jnp.maximum(x, 0.0) + jnp.log(1.0 + jnp.exp(-jnp.abs(x)))


def _seg_ones(seg):
    return ((_iota((LANES, LANES), 0) // seg) == (_iota((LANES, LANES), 1) // seg)).astype(F32)


def _sl(g):
    return slice(g * LANES, (g + 1) * LANES)


@jax.custom_vjp
def _tri_inverse(ms):
    return _tri_inverse_chain(ms)


def _tri_inverse_bwd(ts, dts):
    return ([-_mm_nt(_mm_tn(t, dt, P_INV), t, P_INV) for t, dt in zip(ts, dts)],)


def _tri_inverse_chain(ms):
    c = CHUNK
    ri, ci = _iota((c, c), 0), _iota((c, c), 1)
    eye = (ri == ci).astype(F32)
    d16 = (ri // 16) == (ci // 16)
    d32 = (ri // 32) == (ci // 32)
    ps = [jnp.where(d16, -m, 0.0) for m in ms]
    ts = [eye + p for p in ps]
    for _ in range(3):
        ps = [_mm(p, p, P_INV) for p in ps]
        ts = [_mm(t, eye + p, P_INV) for t, p in zip(ts, ps)]
    for off_diagonal in (d32 & (~d16), ~d32):
        tq = [_mm(t, jnp.where(off_diagonal, m, 0.0), P_INV) for t, m in zip(ts, ms)]
        ts = [t - _mm(a, t, P_INV) for t, a in zip(ts, tq)]
    return ts


_tri_inverse.defvjp(lambda ms: (lambda ts: (ts, ts))(_tri_inverse_chain(ms)), _tri_inverse_bwd)


@jax.custom_vjp
def _known_inverse(ms, ts):
    return ts


_known_inverse.defvjp(lambda ms, ts: (ts, ts),
                      lambda ts, dts: (_tri_inverse_bwd(ts, dts)[0], [jnp.zeros_like(t) for t in ts]))


def _chunk_fwd(prims, *, nsub=None, scalar_decay=None, kinds=None, inverses=None):
    c = CHUNK
    ng = len(prims)
    kinds = kinds if kinds is not None else [(nsub, scalar_decay)] * ng
    s0s, rs, lws, ks, vs, kks, bs = [list(t) for t in zip(*prims)]
    ri, ci = _iota((c, c), 0), _iota((c, c), 1)
    incl = ri >= ci
    strict = ri > ci
    tril = incl.astype(F32)
    lane = _iota((1, LANES), 1)
    heads = [n for n, _ in kinds]
    scalar = [sc for _, sc in kinds]
    masks = [[((lane // (LANES // n)) == s).astype(F32) for s in range(n)] if n > 1 else [1.0] for n in heads]
    cws = [_mm(tril, lw, P_SUM) for lw in lws]
    cwxs = [cw - lw for cw, lw in zip(cws, lws)]
    ends = [cw[c - 1:c, :] for cw in cws]
    kkds = [kk * jnp.exp(cwx) for kk, cwx in zip(kks, cwxs)]
    rds = [r * jnp.exp(cw) for r, cw in zip(rs, cws)]
    kends = [k * jnp.exp(e - cw) for k, e, cw in zip(ks, ends, cws)]
    bends = [b * jnp.exp(e - cw) for b, e, cw in zip(bs, ends, cws)]
    state_terms = [_split_rows(_mm_nt(_stack_rows([kkd, rd]), s0, P_STATE), 2) for kkd, rd, s0 in zip(kkds, rds, s0s)]
    w0s, y0s = [t[0] for t in state_terms], [t[1] for t in state_terms]
    chains = [(g, s) for g in range(ng) for s in range(heads[g])]
    e0 = (lane == 0).astype(F32) * jnp.ones((c, 1), F32)
    rows = [_mm_nt(e0, cw, P_SUM) if sc else None for cw, sc in zip(cws, scalar)]
    dxs = [jnp.where(strict, jnp.exp(jnp.minimum(cwx[:, :c] - row, 0.0)), 0.0) if sc else None
           for cwx, row, sc in zip(cwxs, rows, scalar)]
    dis = [jnp.where(incl, jnp.exp(jnp.minimum(cw[:, :c] - row, 0.0)), 0.0) if sc else None
           for cw, row, sc in zip(cws, rows, scalar)]
    lefts = [_stack_rows([a * m for m in ms] + [q * m for m in ms])
             for a, q, ms in zip([kk if sc else kkd for kk, kkd, sc in zip(kks, kkds, scalar)],
                                 [r if sc else rd for r, rd, sc in zip(rs, rds, scalar)], masks)]
    rights_b = [b if sc else b * jnp.exp(-cw) for b, cw, sc in zip(bs, cws, scalar)]
    rights_k = [k if sc else k * jnp.exp(-cw) for k, cw, sc in zip(ks, cws, scalar)]
    on_b = [_split_rows(_mm_nt(left, right, P_SCORE), 2 * n) for left, right, n in zip(lefts, rights_b, heads)]
    on_k = [_split_rows(_mm_nt(left, right, P_SCORE), 2 * n) for left, right, n in zip(lefts, rights_k, heads)]
    lower = lambda x, g: x * dxs[g] if scalar[g] else jnp.where(strict, x, 0.0)
    lower_incl = lambda x, g: x * dis[g] if scalar[g] else jnp.where(incl, x, 0.0)
    m_b = [lower(on_b[g][s], g) for g, s in chains]
    m_k = [lower(on_k[g][s], g) for g, s in chains]
    n_k = [lower_incl(on_k[g][heads[g] + s], g) for g, s in chains]
    n_b = [lower_incl(on_b[g][heads[g] + s], g) for g, s in chains]
    t_inv = _tri_inverse(m_b) if inverses is None else _known_inverse(m_b, inverses)
    on_v = [_split_rows(_mm(_stack_rows([mk, nk]), vs[g], P_APPLY), 2) for (g, s), mk, nk in zip(chains, m_k, n_k)]
    sa_c = [_mm(t, w0s[g] + mv[0], P_APPLY) for (g, s), t, mv in zip(chains, t_inv, on_v)]
    y_c = [y0s[g] + mv[1] - _mm(nb, sa, P_APPLY) for (g, s), mv, nb, sa in zip(chains, on_v, n_b, sa_c)]
    first = [sum(heads[:g]) for g in range(ng)]
    per_group = lambda xs: [functools.reduce(lambda p, q: p + q, [xs[first[g] + s] * masks[g][s] for s in range(heads[g])])
                            for g in range(ng)]
    sas, ys = per_group(sa_c), per_group(y_c)
    s_ends = [s0 * jnp.exp(e) + _mm_tn(_stack_rows([v, -sa]), _stack_rows([kend, bend]), P_UPDATE)
              for s0, e, v, kend, sa, bend in zip(s0s, ends, vs, kends, sas, bends)]
    row_head = lambda n: _iota((LANES, LANES), 0) // (LANES // n)
    col_head = lambda n: _iota((LANES, LANES), 1) // (LANES // n)
    s_ends = [jnp.where(row_head(n) == col_head(n), s_end, 0.0) if n > 1 else s_end for s_end, n in zip(s_ends, heads)]
    return list(zip(ys, s_ends)), t_inv


def _rec_fwd(name, branches, *, seq):
    n, w = branches[0][0][0].shape
    ng = w // LANES
    nc = seq // CHUNK
    nb = n // seq
    nbr = len(branches)
    per = nb * ng
    kinds = [(heads, scalar) for _, heads, scalar in branches for _ in range(per)]
    nts = [per * heads for _, heads, _ in branches]

    def body(*refs):
        in_refs = [refs[6 * i:6 * i + 6] for i in range(nbr)]
        out_refs = [refs[6 * nbr + 3 * i:6 * nbr + 3 * i + 3] for i in range(nbr)]
        states = refs[9 * nbr:]

        @pl.when(pl.program_id(0) == 0)
        def _():
            for state in states:
                state[...] = jnp.zeros_like(state)
        where = [(i, bi, g) for i in range(nbr) for bi in range(nb) for g in range(ng)]
        prims = [(states[i][bi * ng + g],) + tuple(ref[bi, :, _sl(g)] for ref in in_refs[i]) for i, bi, g in where]
        outs, t_inv = _chunk_fwd(prims, kinds=kinds)
        for (i, bi, g), prim, (y, s_end) in zip(where, prims, outs):
            y_ref, s_ref, _ = out_refs[i]
            s_ref[0, bi * ng + g] = prim[0]
            y_ref[bi, :, _sl(g)] = y
            states[i][bi * ng + g] = s_end
        pos = 0
        for i in range(nbr):
            for j in range(nts[i]):
                out_refs[i][2][0, j] = t_inv[pos + j]
            pos += nts[i]

    row = pl.BlockSpec((nb, CHUNK, w), lambda c: (0, c, 0))
    seqs = lambda a: a.reshape(nb, seq, w)
    res = pl.pallas_call(
        body, name=name, grid=(nc,),
        in_specs=[row] * (6 * nbr),
        out_specs=[spec for nt in nts for spec in (row, pl.BlockSpec((1, per, LANES, LANES), lambda c: (c, 0, 0, 0)),
                                                   pl.BlockSpec((1, nt, CHUNK, CHUNK), lambda c: (c, 0, 0, 0)))],
        out_shape=[shp for nt in nts for shp in (jax.ShapeDtypeStruct((nb, seq, w), F32),
                                                 jax.ShapeDtypeStruct((nc, per, LANES, LANES), F32),
                                                 jax.ShapeDtypeStruct((nc, nt, CHUNK, CHUNK), F32))],
        scratch_shapes=[pltpu.VMEM((per, LANES, LANES), F32)] * nbr,
        compiler_params=pltpu.CompilerParams(dimension_semantics=("arbitrary",), vmem_limit_bytes=VMEM_LIMIT),
    )(*[seqs(a) for arrs, _, _ in branches for a in arrs])
    return [(res[3 * i].reshape(n, w), (res[3 * i + 1], res[3 * i + 2])) for i in range(nbr)]


def _rec_bwd(name, branches, *, seq):
    n, w = branches[0][0][0].shape
    ng = w // LANES
    nc = seq // CHUNK
    nb = n // seq
    nbr = len(branches)
    per = nb * ng
    kinds = [(heads, scalar) for _, _, _, heads, scalar in branches for _ in range(per)]
    nts = [per * heads for _, _, _, heads, _ in branches]

    def body(*refs):
        in_refs = [refs[9 * i:9 * i + 9] for i in range(nbr)]
        out_refs = [refs[9 * nbr + 6 * i:9 * nbr + 6 * i + 6] for i in range(nbr)]
        dstates = refs[15 * nbr:]

        @pl.when(pl.program_id(0) == 0)
        def _():
            for dstate in dstates:
                dstate[...] = jnp.zeros_like(dstate)
        where = [(i, bi, g) for i in range(nbr) for bi in range(nb) for g in range(ng)]
        inverses = [in_refs[i][7][0, j] for i in range(nbr) for j in range(nts[i])]
        f = lambda p: _chunk_fwd(p, kinds=kinds, inverses=inverses)[0]
        prims = [(in_refs[i][6][0, bi * ng + g],) + tuple(ref[bi, :, _sl(g)] for ref in in_refs[i][:6])
                 for i, bi, g in where]
        _, vjp = jax.vjp(f, prims)
        (d_prims,) = vjp([(in_refs[i][8][bi, :, _sl(g)], dstates[i][bi * ng + g]) for i, bi, g in where])
        for (i, bi, g), d_prim in zip(where, d_prims):
            dstates[i][bi * ng + g] = d_prim[0]
            for ref, d in zip(out_refs[i], d_prim[1:]):
                ref[bi, :, _sl(g)] = d

    row = pl.BlockSpec((nb, CHUNK, w), lambda c: (0, nc - 1 - c, 0))
    seqs = lambda a: a.reshape(nb, seq, w)
    in_specs, args = [], []
    for (arrs, (s_save, t_save), dy, _, _), nt in zip(branches, nts):
        in_specs += [row] * 6 + [pl.BlockSpec((1, per, LANES, LANES), lambda c: (nc - 1 - c, 0, 0, 0)),
                                 pl.BlockSpec((1, nt, CHUNK, CHUNK), lambda c: (nc - 1 - c, 0, 0, 0)), row]
        args += [seqs(a) for a in arrs] + [s_save, t_save, seqs(dy)]
    grads = pl.pallas_call(
        body, name=name, grid=(nc,),
        in_specs=in_specs,
        out_specs=[row] * (6 * nbr),
        out_shape=[jax.ShapeDtypeStruct((nb, seq, w), F32)] * (6 * nbr),
        scratch_shapes=[pltpu.VMEM((per, LANES, LANES), F32)] * nbr,
        compiler_params=pltpu.CompilerParams(dimension_semantics=("arbitrary",), vmem_limit_bytes=VMEM_LIMIT),
    )(*args)
    return [[g.reshape(n, w) for g in grads[6 * i:6 * i + 6]] for i in range(nbr)]


def _shift_down(a, j, halo, is_start):
    tb = a.shape[0]
    rolled = pltpu.roll(a, j, 0)
    hr = jnp.where(is_start, 0.0, pltpu.roll(halo, j, 0))
    first = jnp.where(_iota((SUB, LANES), 0) < j, hr, rolled[0:SUB])
    if tb == SUB:
        return first
    return jnp.concatenate([first, rolled[SUB:]], axis=0)


def _shift_up(d, j, carry, is_end):
    tb = d.shape[0]
    up = pltpu.roll(d, tb - j, 0)
    cr = jnp.where(is_end, 0.0, pltpu.roll(carry, SUB - j, 0))
    last = jnp.where(_iota((SUB, LANES), 0) >= SUB - j, cr, up[tb - SUB:tb])
    if tb == SUB:
        return last
    return jnp.concatenate([up[:tb - SUB], last], axis=0)


def _ngroups(a):
    return a.shape[1] // LANES


def _pw_fwd(name, f, ins, shift, params, out_widths, out_dtypes, *, seq, tb):
    n = ins[0].shape[0]
    nt, tps = n // tb, seq // tb
    ni, npar = len(ins), len(params)

    def body(*refs):
        in_refs = refs[:ni]
        pos = ni
        halo_ref = None
        if shift:
            halo_ref = refs[pos]
            pos += 1
        p_refs = refs[pos:pos + npar]
        out_refs = refs[pos + npar:]
        is_start = (pl.program_id(0) % tps) == 0
        tiles = [[ref[:, _sl(g)] for g in range(_ngroups(ref))] for ref in in_refs]
        prevs = [[_shift_down(tiles[0][g], j, halo_ref[:, _sl(g)], is_start) for g in range(len(tiles[0]))]
                 for j in range(1, shift + 1)]
        pv = [[ref[:, _sl(g)] for g in range(_ngroups(ref))] for ref in p_refs]
        outs = f(tiles, prevs, pv)
        for o_ref, og in zip(out_refs, outs, strict=True):
            for g, t in enumerate(og):
                o_ref[:, _sl(g)] = t.astype(o_ref.dtype)

    in_specs = [pl.BlockSpec((tb, a.shape[1]), lambda i: (i, 0)) for a in ins]
    args = list(ins)
    if shift:
        in_specs.append(pl.BlockSpec((SUB, ins[0].shape[1]), lambda i: (jnp.maximum(i * (tb // SUB) - 1, 0), 0)))
        args.append(ins[0])
    in_specs += [pl.BlockSpec(p.shape, lambda i: (0, 0)) for p in params]
    args += list(params)
    return pl.pallas_call(
        body, name=name, grid=(nt,),
        in_specs=in_specs,
        out_specs=[pl.BlockSpec((tb, w), lambda i: (i, 0)) for w in out_widths],
        out_shape=[jax.ShapeDtypeStruct((n, w), dt) for w, dt in zip(out_widths, out_dtypes, strict=True)],
        compiler_params=pltpu.CompilerParams(dimension_semantics=("parallel",), vmem_limit_bytes=VMEM_LIMIT),
    )(*args)


def _proj_pw_fwd(name, f, h, wts, shift, params, out_widths, *, seq, tb):
    n = h.shape[0]
    nt, tps = n // tb, seq // tb
    nw, npar = len(wts), len(params)

    def body(*refs):
        h_ref = refs[0]
        w_refs = refs[1:1 + nw]
        par_refs = refs[1 + nw:1 + nw + npar]
        p_refs = refs[1 + nw + npar:1 + 2 * nw + npar]
        out_refs = refs[1 + 2 * nw + npar:len(refs) - 1]
        carry = refs[-1]
        is_start = (pl.program_id(0) % tps) == 0
        for w_ref, p_ref in zip(w_refs, p_refs, strict=True):
            p_ref[...] = lax.dot_general(h_ref[...], w_ref[...], (_NT, ((), ())), preferred_element_type=F32)
        tiles = [[ref[:, _sl(g)] for g in range(_ngroups(ref))] for ref in p_refs]
        prevs = [[_shift_down(tiles[0][g], j, carry[:, _sl(g)], is_start) for g in range(len(tiles[0]))]
                 for j in range(1, shift + 1)]
        carry[...] = p_refs[0][tb - SUB:tb, :]
        pv = [[ref[:, _sl(g)] for g in range(_ngroups(ref))] for ref in par_refs]
        outs = f(tiles, prevs, pv)
        for o_ref, og in zip(out_refs, outs, strict=True):
            for g, t in enumerate(og):
                o_ref[:, _sl(g)] = t

    widths = [w.shape[0] for w in wts] + list(out_widths)
    res = pl.pallas_call(
        body, name=name, grid=(nt,),
        in_specs=([pl.BlockSpec((tb, D_MODEL), lambda i: (i, 0))] + [pl.BlockSpec(w.shape, lambda i: (0, 0)) for w in wts]
                  + [pl.BlockSpec(p.shape, lambda i: (0, 0)) for p in params]),
        out_specs=[pl.BlockSpec((tb, w), lambda i: (i, 0)) for w in widths],
        out_shape=[jax.ShapeDtypeStruct((n, w), F32) for w in widths],
        scratch_shapes=[pltpu.VMEM((SUB, wts[0].shape[0]), F32)],
        compiler_params=pltpu.CompilerParams(dimension_semantics=("arbitrary",), vmem_limit_bytes=VMEM_LIMIT),
    )(h, *wts, *params)
    return res[:nw], res[nw:]


def _pw_bwd(name, f, ins, shift, params, douts, din_dtypes, *, seq, tb):
    n = ins[0].shape[0]
    nt, tps = n // tb, seq // tb
    ni, npar = len(ins), len(params)
    flat_douts = [d for ds in douts for d in ds]
    nd = len(flat_douts)
    w0 = ins[0].shape[1]

    def body(*refs):
        in_refs = refs[:ni]
        pos = ni
        halo_ref = None
        if shift:
            halo_ref = refs[pos]
            pos += 1
        p_refs = refs[pos:pos + npar]
        pos += npar
        d_refs = refs[pos:pos + nd]
        pos += nd
        din_refs = refs[pos:pos + ni]
        pos += ni
        dp_refs = refs[pos:pos + npar]
        pos += npar
        carry = refs[pos] if shift else None
        step = pl.program_id(0)
        tile = nt - 1 - step
        is_start = (tile % tps) == 0
        is_end = (tile % tps) == tps - 1
        tiles = [[ref[:, _sl(g)] for g in range(_ngroups(ref))] for ref in in_refs]
        prevs = [[_shift_down(tiles[0][g], j, halo_ref[:, _sl(g)], is_start) for g in range(len(tiles[0]))]
                 for j in range(1, shift + 1)]
        pv = [[ref[:, _sl(g)] for g in range(_ngroups(ref))] for ref in p_refs]
        cot, pos_d = [], 0
        for ds in douts:
            grp = d_refs[pos_d:pos_d + len(ds)]
            pos_d += len(ds)
            cot.append([functools.reduce(lambda p, q: p + q, [ref[:, _sl(g)].astype(F32) for ref in grp])
                        for g in range(_ngroups(grp[0]))])
        _, vjp = jax.vjp(f, tiles, prevs, pv)
        d_tiles, d_prevs, d_pv = vjp(cot)
        for g in range(len(tiles[0])):
            for j in range(1, shift + 1):
                d_tiles[0][g] = d_tiles[0][g] + _shift_up(d_prevs[j - 1][g], j, carry[j - 1, :, _sl(g)], is_end)
            for j in range(1, shift + 1):
                carry[j - 1, :, _sl(g)] = d_prevs[j - 1][g][0:SUB]
        for ref, dg in zip(din_refs, d_tiles, strict=True):
            for g, t in enumerate(dg):
                ref[:, _sl(g)] = t.astype(ref.dtype)

        @pl.when(step == 0)
        def _():
            for ref in dp_refs:
                ref[...] = jnp.zeros_like(ref)
        for ref, dg in zip(dp_refs, d_pv, strict=True):
            for g, t in enumerate(dg):
                ref[:, _sl(g)] += t

    rev = lambda i: (nt - 1 - i, 0)
    in_specs = [pl.BlockSpec((tb, a.shape[1]), rev) for a in ins]
    args = list(ins)
    if shift:
        in_specs.append(pl.BlockSpec((SUB, w0), lambda i: (jnp.maximum((nt - 1 - i) * (tb // SUB) - 1, 0), 0)))
        args.append(ins[0])
    in_specs += [pl.BlockSpec(p.shape, lambda i: (0, 0)) for p in params]
    args += list(params)
    in_specs += [pl.BlockSpec((tb, d.shape[1]), rev) for d in flat_douts]
    args += flat_douts
    out_specs = [pl.BlockSpec((tb, a.shape[1]), rev) for a in ins] + [pl.BlockSpec(p.shape, lambda i: (0, 0)) for p in params]
    out_shape = ([jax.ShapeDtypeStruct(a.shape, dt) for a, dt in zip(ins, din_dtypes, strict=True)]
                 + [jax.ShapeDtypeStruct(p.shape, F32) for p in params])
    res = pl.pallas_call(
        body, name=name, grid=(nt,),
        in_specs=in_specs, out_specs=out_specs, out_shape=out_shape,
        scratch_shapes=[pltpu.VMEM((shift, SUB, w0), F32)] if shift else [],
        compiler_params=pltpu.CompilerParams(dimension_semantics=("arbitrary",), vmem_limit_bytes=VMEM_LIMIT),
    )(*args)
    return res[:ni], res[ni:]


def _rwkv_prep_f(tiles, prevs, params):
    (p,), (prev,) = tiles, prevs
    mu, w0, w2p, a0, a2p, k_k, k_a = params
    xs = [p[g] + (prev[g] - p[g]) * mu[g] for g in range(13)]
    wdad = xs[12]
    tw = jnp.tanh(wdad)
    e64 = _seg_ones(64)
    r, lw, k2, v, kk, b = [], [], [], [], [], []
    for g in range(4):
        k_g = xs[4 + g]
        lo = w0[g] + _mm(tw, w2p[g], P_POINT)
        lw_g = -jnp.exp(-_softplus(-lo) - 0.5)
        a_g = _sigmoid(a0[g] + _mm(wdad, a2p[g], P_POINT))
        kkp = k_g * k_k[g]
        kk_g = kkp * lax.rsqrt(_mm(kkp * kkp, e64, P_POINT) + 1e-12)
        r.append(xs[g])
        lw.append(lw_g)
        k2.append(k_g * (1.0 + (a_g - 1.0) * k_a[g]))
        v.append(xs[8 + g])
        kk.append(kk_g)
        b.append(kk_g * a_g)
    return [r, lw, k2, v, kk, b]


def _rwkv_post_f(tiles, prevs, params):
    yrec, r, k2, v, z = tiles
    gn_w, gn_b, r_k = params
    e64 = _seg_ones(64)
    out = []
    for g in range(4):
        mean = _mm(yrec[g], e64, P_POINT) * (1.0 / 64)
        d = yrec[g] - mean
        var = _mm(d * d, e64, P_POINT) * (1.0 / 64)
        yn = d * lax.rsqrt(var + RW_GN_EPS) * gn_w[g] + gn_b[g]
        bonus = _mm(r[g] * k2[g] * r_k[g], e64, P_POINT) * v[g]
        out.append((yn + bonus) * _silu(z[g]))
    return [out]


def _gdn_prep_f(tiles, prevs, params):
    x, (ba,) = tiles
    p1, p2, p3 = prevs
    cw0, cw1, cw2, cw3, a_log, dt_bias = params
    s = [_silu(cw3[g] * x[g] + cw2[g] * p1[g] + cw1[g] * p2[g] + cw0[g] * p3[g]) for g in range(12)]
    row = _iota((LANES, LANES), 0)
    r, lw, k, vv, b = [], [], [], [], []
    for h in range(4):
        q_h, k_h, v_h = s[h], s[4 + h], s[8 + h]
        qn = q_h * lax.rsqrt(jnp.sum(q_h * q_h, axis=-1, keepdims=True) + 1e-12)
        kn = k_h * lax.rsqrt(jnp.sum(k_h * k_h, axis=-1, keepdims=True) + 1e-12)
        beta = _sigmoid(_mm(ba, (row == h).astype(F32)))
        alpha = _mm(ba, (row == 4 + h).astype(F32))
        g_h = -jnp.exp(a_log[h]) * _softplus(alpha + dt_bias[h])
        r.append(qn * (LANES ** -0.5))
        lw.append(g_h)
        k.append(kn)
        vv.append(beta * v_h)
        b.append(jnp.exp(g_h) * beta * kn)
    return [r, lw, k, vv, b]


def _gdn_post_f(tiles, prevs, params):
    o, z = tiles
    ((onw,),) = params
    out = []
    for h in range(4):
        ms = jnp.mean(o[h] * o[h], axis=-1, keepdims=True)
        out.append(o[h] * lax.rsqrt(ms + NORM_EPS) * onw * _silu(z[h]))
    return [out]


def _norm_in(x2, g_in, *, tm):
    n = x2.shape[0]

    def body(x_ref, g_ref, h_ref):
        x = x_ref[...]
        rs = lax.rsqrt(jnp.mean(x * x, axis=-1, keepdims=True) + NORM_EPS)
        h_ref[...] = (x * rs * g_ref[...]).astype(BF16)

    return pl.pallas_call(
        body, name="norm_in", grid=(n // tm,),
        in_specs=[pl.BlockSpec((tm, D_MODEL), lambda i: (i, 0)), pl.BlockSpec((1, D_MODEL), lambda i: (0, 0))],
        out_specs=pl.BlockSpec((tm, D_MODEL), lambda i: (i, 0)),
        out_shape=jax.ShapeDtypeStruct((n, D_MODEL), BF16),
        compiler_params=pltpu.CompilerParams(dimension_semantics=("parallel",), vmem_limit_bytes=VMEM_LIMIT),
    )(x2, g_in)


def _proj(name, h, wt, *, tm):
    n, ws = h.shape[0], wt.shape[0]

    def body(h_ref, w_ref, o_ref):
        o_ref[...] = lax.dot_general(h_ref[...], w_ref[...], (_NT, ((), ())), preferred_element_type=F32)

    return pl.pallas_call(
        body, name=name, grid=(n // tm,),
        in_specs=[pl.BlockSpec((tm, D_MODEL), lambda i: (i, 0)), pl.BlockSpec((ws, D_MODEL), lambda i: (0, 0))],
        out_specs=pl.BlockSpec((tm, ws), lambda i: (i, 0)),
        out_shape=jax.ShapeDtypeStruct((n, ws), F32),
        compiler_params=pltpu.CompilerParams(dimension_semantics=("parallel",), vmem_limit_bytes=VMEM_LIMIT),
    )(h, wt)


def _proj_dw(name, h, dp, *, tm):
    n, ws = dp.shape

    def body(h_ref, d_ref, o_ref):
        @pl.when(pl.program_id(0) == 0)
        def _():
            o_ref[...] = jnp.zeros_like(o_ref)
        o_ref[...] += lax.dot_general(d_ref[...], h_ref[...], (_TN, ((), ())), preferred_element_type=F32)

    return pl.pallas_call(
        body, name=name, grid=(n // tm,),
        in_specs=[pl.BlockSpec((tm, D_MODEL), lambda i: (i, 0)), pl.BlockSpec((tm, ws), lambda i: (i, 0))],
        out_specs=pl.BlockSpec((ws, D_MODEL), lambda i: (0, 0)),
        out_shape=jax.ShapeDtypeStruct((ws, D_MODEL), F32),
        compiler_params=pltpu.CompilerParams(dimension_semantics=("arbitrary",), vmem_limit_bytes=VMEM_LIMIT),
    )(h, dp)


def _proj_dx(x2, g_in, d_xo, dps, ws, *, tm):
    n = x2.shape[0]
    ns = len(dps)

    def body(*refs):
        x_ref, g_ref, dxo_ref = refs[:3]
        dp_refs = refs[3:3 + ns]
        w_refs = refs[3 + ns:3 + 2 * ns]
        dx_ref, dg_ref = refs[3 + 2 * ns:]
        dh = jnp.zeros((tm, D_MODEL), F32)
        for d_ref, w_ref in zip(dp_refs, w_refs, strict=True):
            dh = dh + jnp.dot(d_ref[...], w_ref[...], preferred_element_type=F32)
        x = x_ref[...]
        rs = lax.rsqrt(jnp.mean(x * x, axis=-1, keepdims=True) + NORM_EPS)
        xn = x * rs
        dxn = dh * g_ref[...]
        dx_ref[...] = dxo_ref[...] + rs * (dxn - xn * jnp.mean(dxn * xn, axis=-1, keepdims=True))

        @pl.when(pl.program_id(0) == 0)
        def _():
            dg_ref[...] = jnp.zeros_like(dg_ref)
        dg_ref[...] += jnp.sum(dh * xn, axis=0, keepdims=True)

    row = pl.BlockSpec((tm, D_MODEL), lambda i: (i, 0))
    return pl.pallas_call(
        body, name="proj_dx", grid=(n // tm,),
        in_specs=([row, pl.BlockSpec((1, D_MODEL), lambda i: (0, 0)), row]
                  + [pl.BlockSpec((tm, d.shape[1]), lambda i: (i, 0)) for d in dps]
                  + [pl.BlockSpec(w.shape, lambda i: (0, 0)) for w in ws]),
        out_specs=[row, pl.BlockSpec((1, D_MODEL), lambda i: (0, 0))],
        out_shape=[jax.ShapeDtypeStruct((n, D_MODEL), F32), jax.ShapeDtypeStruct((1, D_MODEL), F32)],
        compiler_params=pltpu.CompilerParams(dimension_semantics=("arbitrary",), vmem_limit_bytes=VMEM_LIMIT),
    )(x2, g_in, d_xo, *dps, *ws)


def _tail(x2, tgt2, gates, rw_post_ins, gd_post_ins, rw_post_params, gd_post_params, w_a, w_b, w_o, now, *, tr):
    n = x2.shape[0]
    n_rw, n_gd = len(rw_post_ins), len(gd_post_ins)
    n_rwp, n_gdp = len(rw_post_params), len(gd_post_params)

    def body(*refs):
        x_ref, t_ref, g_ref = refs[:3]
        pos = 3
        rw_refs, gd_refs = refs[pos:pos + n_rw], refs[pos + n_rw:pos + n_rw + n_gd]
        pos += n_rw + n_gd
        rwp_refs, gdp_refs = refs[pos:pos + n_rwp], refs[pos + n_rwp:pos + n_rwp + n_gdp]
        pos += n_rwp + n_gdp
        wa_ref, wb_ref, wo_ref, now_ref = refs[pos:pos + 4]
        dya_ref, dyb_ref, dg_ref, dxo_ref, dwa_ref, dwb_ref, dwo_ref, dnow_ref, loss_ref = refs[pos + 4:]
        groups = lambda rs: [[ref[:, _sl(g)] for g in range(_ngroups(ref))] for ref in rs]
        (ya_groups,) = _rwkv_post_f(groups(rw_refs), [], groups(rwp_refs))
        (yb_groups,) = _gdn_post_f(groups(gd_refs), [], groups(gdp_refs))
        ya16 = jnp.concatenate(ya_groups, axis=1).astype(BF16)
        yb16 = jnp.concatenate(yb_groups, axis=1).astype(BF16)
        ua = jnp.dot(ya16, wa_ref[...], preferred_element_type=F32)
        ub = jnp.dot(yb16, wb_ref[...], preferred_element_type=F32)
        ga = _sigmoid(g_ref[:, :D_MODEL])
        gb = _sigmoid(g_ref[:, D_MODEL:])
        m16 = (ga * ua + gb * ub).astype(BF16)
        xo = x_ref[...] + jnp.dot(m16, wo_ref[...], preferred_element_type=F32)
        rs = lax.rsqrt(jnp.mean(xo * xo, axis=-1, keepdims=True) + NORM_EPS)
        yn = xo * rs
        now_v = now_ref[...]
        err = yn * now_v - t_ref[...]
        dy = err * (1.0 / D_MODEL)
        dyn = dy * now_v
        dxo = rs * (dyn - yn * jnp.mean(dyn * yn, axis=-1, keepdims=True))
        dxo_ref[...] = dxo
        dxo16 = dxo.astype(BF16)
        dm = lax.dot_general(dxo16, wo_ref[...], (((1,), (1,)), ((), ())), preferred_element_type=F32)
        dua16 = (dm * ga).astype(BF16)
        dub16 = (dm * gb).astype(BF16)
        dg_ref[:, :D_MODEL] = (dm * ua * ga * (1.0 - ga)).astype(dg_ref.dtype)
        dg_ref[:, D_MODEL:] = (dm * ub * gb * (1.0 - gb)).astype(dg_ref.dtype)
        dya_ref[...] = lax.dot_general(dua16, wa_ref[...], (((1,), (1,)), ((), ())), preferred_element_type=F32)
        dyb_ref[...] = lax.dot_general(dub16, wb_ref[...], (((1,), (1,)), ((), ())), preferred_element_type=F32)

        @pl.when(pl.program_id(0) == 0)
        def _():
            for ref in (dwa_ref, dwb_ref, dwo_ref, dnow_ref, loss_ref):
                ref[...] = jnp.zeros_like(ref)
        tn = (((0,), (0,)), ((), ()))
        dwo_ref[...] += lax.dot_general(m16, dxo16, tn, preferred_element_type=F32)
        dwa_ref[...] += lax.dot_general(ya16, dua16, tn, preferred_element_type=F32)
        dwb_ref[...] += lax.dot_general(yb16, dub16, tn, preferred_element_type=F32)
        dnow_ref[...] += jnp.sum(dy * yn, axis=0, keepdims=True)
        loss_ref[...] += (0.5 / D_MODEL) * jnp.sum(err * err)

    row = lambda w: pl.BlockSpec((tr, w), lambda i: (i, 0))
    full = lambda a: pl.BlockSpec(a.shape, lambda i: (0, 0))
    return pl.pallas_call(
        body, name="tail", grid=(n // tr,),
        in_specs=([row(D_MODEL), row(D_MODEL), row(2 * D_MODEL)] + [row(a.shape[1]) for a in rw_post_ins + gd_post_ins]
                  + [full(p) for p in rw_post_params + gd_post_params] + [full(w_a), full(w_b), full(w_o), full(now)]),
        out_specs=[row(RW_W), row(GD_W), row(2 * D_MODEL), row(D_MODEL),
                   pl.BlockSpec((RW_W, D_MODEL), lambda i: (0, 0)), pl.BlockSpec((GD_W, D_MODEL), lambda i: (0, 0)),
                   pl.BlockSpec((D_MODEL, D_MODEL), lambda i: (0, 0)), pl.BlockSpec((1, D_MODEL), lambda i: (0, 0)),
                   pl.BlockSpec((SUB, LANES), lambda i: (0, 0))],
        out_shape=[jax.ShapeDtypeStruct((n, RW_W), F32), jax.ShapeDtypeStruct((n, GD_W), F32),
                   jax.ShapeDtypeStruct((n, 2 * D_MODEL), BF16), jax.ShapeDtypeStruct((n, D_MODEL), F32),
                   jax.ShapeDtypeStruct((RW_W, D_MODEL), F32), jax.ShapeDtypeStruct((GD_W, D_MODEL), F32),
                   jax.ShapeDtypeStruct((D_MODEL, D_MODEL), F32), jax.ShapeDtypeStruct((1, D_MODEL), F32),
                   jax.ShapeDtypeStruct((SUB, LANES), F32)],
        compiler_params=pltpu.CompilerParams(dimension_semantics=("arbitrary",), vmem_limit_bytes=VMEM_LIMIT),
    )(x2, tgt2, gates, *rw_post_ins, *gd_post_ins, *rw_post_params, *gd_post_params, w_a, w_b, w_o, now)


def _exchange(name, axes, scatter, gather, place_own=True):
    ns, ng = len(scatter), len(gather)
    na = ns + ng
    gs = 2 ** len(axes)
    arrs = list(scatter) + list(gather)

    def body(*refs):
        src = refs[:na]
        dst = refs[na:2 * na]
        send_sems, recv_sems = refs[2 * na:]
        mine = {ax: lax.axis_index(ax) for ax in ("x", "y", "c")}

        def peer(k):
            co = dict(mine)
            for i, ax in enumerate(axes):
                if (k >> (len(axes) - 1 - i)) & 1:
                    co[ax] = 1 - co[ax]
            idx = 0
            for ax in axes:
                idx = 2 * idx + co[ax]
            return (co["x"], co["y"], co["c"]), idx

        _, me = peer(0)

        def copy(a, k, landing):
            dev, idx = peer(k)
            s = src[a].at[idx] if a < ns else src[a]
            return pltpu.make_async_remote_copy(src_ref=s, dst_ref=dst[a].at[idx if landing else me],
                                                send_sem=send_sems.at[a, k - 1], recv_sem=recv_sems.at[a, k - 1],
                                                device_id=dev, device_id_type=pl.DeviceIdType.MESH)

        sends = [copy(a, k, False) for a in range(na) for k in range(1, gs)]
        for cp in sends:
            cp.start()
        for a in range(na):
            for k in range(1, gs):
                copy(a, k, True).wait_recv()
        for cp in sends:
            cp.wait_send()

    out_shape = [jax.ShapeDtypeStruct(a.shape, a.dtype) for a in scatter] + \
                [jax.ShapeDtypeStruct((gs,) + a.shape, a.dtype) for a in gather]
    anyspec = pl.BlockSpec(memory_space=pl.ANY)
    lands = pl.pallas_call(
        body, name=name,
        in_specs=[anyspec] * na, out_specs=[anyspec] * na, out_shape=out_shape,
        scratch_shapes=[pltpu.SemaphoreType.DMA((na, gs - 1)), pltpu.SemaphoreType.DMA((na, gs - 1))],
    )(*arrs)
    if not place_own:
        return lands
    me = 0
    for ax in axes:
        me = 2 * me + lax.axis_index(ax)
    kept = [lax.dynamic_index_in_dim(a, me, 0, keepdims=False) for a in scatter] + list(gather)
    return [lax.dynamic_update_index_in_dim(land, mine, me, 0) for land, mine in zip(lands, kept)]


def _gather_all(name, arrs):
    na = len(arrs)

    def body(*refs):
        src = refs[:na]
        dst = refs[na:2 * na]
        send_sems, recv_sems = refs[2 * na:]
        x, y, c = lax.axis_index("x"), lax.axis_index("y"), lax.axis_index("c")
        sibling = (x, y, 1 - c)
        chips = [(1 - x, y), (x, 1 - y), (1 - x, 1 - y)]

        def copy(a, k, block, to, own=False):
            px, py, pc = block
            slot = dst[a].at[pc, 2 * px + py]
            return pltpu.make_async_remote_copy(src_ref=src[a] if own else slot, dst_ref=slot,
                                                send_sem=send_sems.at[a, k], recv_sem=recv_sems.at[a, k],
                                                device_id=to, device_id_type=pl.DeviceIdType.MESH)

        first = [copy(a, 0, (x, y, c), sibling, own=True) for a in range(na)]
        first += [copy(a, 1 + j, (x, y, c), (*chip, c), own=True) for j, chip in enumerate(chips) for a in range(na)]
        for cp in first:
            cp.start()
        passed = []
        for j, chip in enumerate(chips):
            for a in range(na):
                copy(a, 1 + j, (*chip, c), (x, y, c)).wait_recv()
                passed.append(copy(a, 4 + j, (*chip, c), sibling))
                passed[-1].start()
        for a in range(na):
            copy(a, 0, (x, y, 1 - c), (x, y, c)).wait_recv()
            for j, chip in enumerate(chips):
                copy(a, 4 + j, (*chip, 1 - c), (x, y, c)).wait_recv()
        for cp in first + passed:
            cp.wait_send()

    anyspec = pl.BlockSpec(memory_space=pl.ANY)
    lands = pl.pallas_call(
        body, name=name,
        in_specs=[anyspec] * na, out_specs=[anyspec] * na,
        out_shape=[jax.ShapeDtypeStruct((2, 4) + a.shape, a.dtype) for a in arrs],
        scratch_shapes=[pltpu.SemaphoreType.DMA((na, 7)), pltpu.SemaphoreType.DMA((na, 7))],
    )(*arrs)
    core, chip = lax.axis_index("c"), 2 * lax.axis_index("x") + lax.axis_index("y")
    zero = jnp.zeros((), jnp.int32)
    return [lax.dynamic_update_slice(land, mine[None, None], (core, chip) + (zero,) * mine.ndim)
            for land, mine in zip(lands, arrs)]


def _pair_sum(name, own, land, out_dtype):
    _, nq, r, c = own.shape
    core = lax.axis_index("c").astype(jnp.int32).reshape(1)

    def body(core_ref, own_ref, land_ref, o_ref):
        o_ref[0] = (own_ref[0, 0] + land_ref[0, 0]).astype(o_ref.dtype)

    return pl.pallas_call(
        body, name=name,
        grid_spec=pltpu.PrefetchScalarGridSpec(
            num_scalar_prefetch=1, grid=(nq,),
            in_specs=[pl.BlockSpec((1, 1, r, c), lambda i, core_ref: (core_ref[0], i, 0, 0)),
                      pl.BlockSpec((1, 1, r, c), lambda i, core_ref: (1 - core_ref[0], i, 0, 0))],
            out_specs=pl.BlockSpec((1, r, c), lambda i, core_ref: (i, 0, 0))),
        out_shape=jax.ShapeDtypeStruct((nq, r, c), out_dtype),
        compiler_params=pltpu.CompilerParams(dimension_semantics=("parallel",), vmem_limit_bytes=VMEM_LIMIT),
    )(core, own, land)


def _adam(name, land, w, m, v):
    r, c = w.shape
    nslot = land.shape[0]
    tr = 256 if (r % 256 == 0 and r > 256) else r
    tc = 256 if (tr == r and r > 256 and c % 256 == 0) else c

    def body(l_ref, w_ref, m_ref, v_ref, g_out, d_out, m_out, v_out):
        g = l_ref[0].astype(F32)
        for s in range(1, nslot):
            g = g + l_ref[s].astype(F32)
        g_out[...] = g
        d_out[...], m_out[...], v_out[...] = _adam_math(g, w_ref[...], m_ref[...], v_ref[...])

    blk = pl.BlockSpec((tr, tc), lambda i: (i * tc // c, i % (c // tc)))
    return pl.pallas_call(
        body, name=name, grid=((r // tr) * (c // tc),),
        in_specs=[pl.BlockSpec((nslot, tr, tc), lambda i: (0, i * tc // c, i % (c // tc))), blk, blk, blk],
        out_specs=[blk] * 4,
        out_shape=[jax.ShapeDtypeStruct((r, c), F32)] * 4,
        compiler_params=pltpu.CompilerParams(dimension_semantics=("parallel",), vmem_limit_bytes=VMEM_LIMIT),
    )(land, w, m, v)


def _adam_math(g, w, m, v):
    c1 = 1.0 / (1.0 - ADAM_B1 ** ADAM_STEP)
    c2 = 1.0 / (1.0 - ADAM_B2 ** ADAM_STEP)
    m_new = ADAM_B1 * m + (1.0 - ADAM_B1) * g
    v_new = ADAM_B2 * v + (1.0 - ADAM_B2) * (g * g)
    return -ADAM_LR * ((m_new * c1) / (jnp.sqrt(v_new * c2) + ADAM_EPS) + ADAM_WD * w), m_new, v_new


def _adam_small(land, ws, ms, vs):
    npar = len(ws)
    nslot = land.shape[0]

    def body(*refs):
        l_ref = refs[0]
        w_refs, m_refs, v_refs = refs[1:1 + npar], refs[1 + npar:1 + 2 * npar], refs[1 + 2 * npar:1 + 3 * npar]
        outs = refs[1 + 3 * npar:1 + 7 * npar]
        loss_ref, g_rows = refs[1 + 7 * npar], refs[2 + 7 * npar]
        g = l_ref[0]
        for s in range(1, nslot):
            g = g + l_ref[s]
        g_rows[...] = g
        row = 0
        for i, (_, size) in enumerate(_SMALL):
            for j in range(-(-size // LANES)):
                width = min(LANES, size - j * LANES)
                cols = slice(j * LANES, j * LANES + width)
                g_ij = g_rows[row:row + 1, 0:width]
                delta, m_new, v_new = _adam_math(g_ij, w_refs[i][:, cols], m_refs[i][:, cols], v_refs[i][:, cols])
                for ref, val in zip(outs[4 * i:4 * i + 4], (g_ij, delta, m_new, v_new)):
                    ref[:, cols] = val
                row += 1
        loss_ref[...] = g_rows[row:row + 1, :]

    full = lambda a: pl.BlockSpec(a.shape, lambda: (0,) * a.ndim)
    res = pl.pallas_call(
        body, name="adam_small",
        in_specs=[full(land)] + [full(a) for a in list(ws) + list(ms) + list(vs)],
        out_specs=[full(w) for w in ws for _ in range(4)] + [pl.BlockSpec((1, LANES), lambda: (0, 0))],
        out_shape=[jax.ShapeDtypeStruct(w.shape, F32) for w in ws for _ in range(4)] + [jax.ShapeDtypeStruct((1, LANES), F32)],
        scratch_shapes=[pltpu.VMEM(land.shape[1:], F32)],
    )(land, *ws, *ms, *vs)
    return [res[4 * i:4 * i + 4] for i in range(npar)], res[4 * npar]


_SMALL = (("norm_in_w", 1024), ("rw_mu", 1664), ("rw_w0", 512), ("rw_a0", 512), ("rw_k_k", 512), ("rw_k_a", 512),
          ("rw_r_k", 512), ("rw_gn_w", 512), ("rw_gn_b", 512), ("gd_A_log", 4), ("gd_dt_bias", 4), ("gd_o_norm_w", 128),
          ("norm_out_w", 1024))
_SMALL_ROWS = 64


def _pack_small(vals, loss_row):
    rows = []
    for (_, size), a in zip(_SMALL, vals, strict=True):
        flat = a.reshape(-1).astype(F32)
        pad = (-size) % LANES
        if pad:
            flat = jnp.concatenate([flat, jnp.zeros((pad,), F32)])
        rows.append(flat.reshape(-1, LANES))
    rows.append(loss_row)
    used = sum(r.shape[0] for r in rows)
    rows.append(jnp.zeros((_SMALL_ROWS - used, LANES), F32))
    return jnp.concatenate(rows, axis=0)


def kernel(x, norm_in_w, w_in, rw_mu, rw_w0, rw_w2, rw_a0, rw_a2, rw_k_k, rw_k_a, rw_r_k, rw_gn_w, rw_gn_b, gd_conv_w, gd_A_log, gd_dt_bias, gd_o_norm_w, w_branch_a, w_branch_b, w_out, norm_out_w, loss_target, m_norm_in_w, m_w_in, m_rw_mu, m_rw_w0, m_rw_w2, m_rw_a0, m_rw_a2, m_rw_k_k, m_rw_k_a, m_rw_r_k, m_rw_gn_w, m_rw_gn_b, m_gd_conv_w, m_gd_A_log, m_gd_dt_bias, m_gd_o_norm_w, m_w_branch_a, m_w_branch_b, m_w_out, m_norm_out_w, v_norm_in_w, v_w_in, v_rw_mu, v_rw_w0, v_rw_w2, v_rw_a0, v_rw_a2, v_rw_k_k, v_rw_k_a, v_rw_r_k, v_rw_gn_w, v_rw_gn_b, v_gd_conv_w, v_gd_A_log, v_gd_dt_bias, v_gd_o_norm_w, v_w_branch_a, v_w_branch_b, v_w_out, v_norm_out_w):
    nb, seq, _ = x.shape
    n = nb * seq
    tm = min(1024, n)
    tb = min(512, seq)
    x2 = x.reshape(n, D_MODEL)
    tgt2 = loss_target.reshape(n, D_MODEL)
    cols = w_in.shape[2]
    in_cols = cols * N_DEV

    wt_own, mt_own, vt_own = w_in[0].T, m_w_in[0].T, v_w_in[0].T
    sharded = [wt_own.astype(BF16), rw_w2[0], rw_a2[0], gd_conv_w[0], w_branch_a[0].astype(BF16),
               w_branch_b[0].astype(BF16), w_out[0].astype(BF16)]
    g_win, g_w2, g_a2, g_conv, g_wa, g_wb, g_wo = _gather_all("gather_weights", sharded)
    unshard_rows = lambda a: jnp.transpose(a, (1, 0, 2, 3)).reshape(N_DEV * a.shape[2], a.shape[3])
    unshard_cols = lambda a: jnp.transpose(a, (2, 1, 0, 3)).reshape(a.shape[2], N_DEV * a.shape[3])
    wt_full = unshard_rows(g_win)
    seg_bounds = ((0, 1664), (1664, 2176), (2176, 3712), (3712, 4224), (4232, in_cols))
    w_rw, w_zrw, w_qkv, w_zgd, w_gates = [wt_full[a:b] for a, b in seg_bounds]
    w_ba = jnp.concatenate([wt_full[4224:4232], jnp.zeros((LANES - 8, D_MODEL), BF16)], axis=0)
    w2_full, a2_full = unshard_cols(g_w2), unshard_cols(g_a2)
    zeros64 = jnp.zeros((64, RW_W), F32)
    w2p = jnp.concatenate([w2_full, zeros64], axis=0)
    a2p = jnp.concatenate([zeros64, a2_full], axis=0)
    conv_full = unshard_cols(g_conv)
    conv_rows = [conv_full[i:i + 1] for i in range(4)]
    wa_full = unshard_cols(g_wa)
    wb_full = unshard_cols(g_wb)
    wo_full = unshard_rows(g_wo)
    a_log_bc = jnp.repeat(gd_A_log, LANES, axis=1)
    dt_bias_bc = jnp.repeat(gd_dt_bias, LANES, axis=1)
    r_k_flat = rw_r_k.reshape(1, RW_W)
    now2 = norm_out_w.reshape(1, D_MODEL)

    h = _norm_in(x2, norm_in_w, tm=tm)
    p_zrw = _proj("proj_zrw", h, w_zrw, tm=tm)
    p_zgd = _proj("proj_zgd", h, w_zgd, tm=tm)
    p_gates = _proj("proj_gates", h, w_gates, tm=tm)
    rw_params = [rw_mu, rw_w0, w2p, rw_a0, a2p, rw_k_k, rw_k_a]
    (p_rw,), (r_a, lw_a, k_a, v_a, kk_a, b_a) = _proj_pw_fwd("proj_rwkv_prep", _rwkv_prep_f, h, [w_rw], 1, rw_params,
                                                             [RW_W] * 6, seq=seq, tb=tb)
    gd_params = conv_rows + [a_log_bc, dt_bias_bc]
    (p_qkv, p_ba), (r_b, lw_b, k_b, v_b, b_b) = _proj_pw_fwd("proj_gdn_prep", _gdn_prep_f, h, [w_qkv, w_ba], 3, gd_params,
                                                             [GD_W] * 5, seq=seq, tb=tb)
    rw_six, gd_six = (r_a, lw_a, k_a, v_a, kk_a, b_a), (r_b, lw_b, k_b, v_b, k_b, b_b)
    (y_rec, s_a), (o_rec, s_b) = _rec_fwd("rec", [(rw_six, 2, False), (gd_six, 1, True)], seq=seq)
    post_params = [rw_gn_w, rw_gn_b, r_k_flat]
    d_ya, d_yb, d_gates, d_xo, dwa, dwb, dwo, d_now, loss_acc = _tail(
        x2, tgt2, p_gates, [y_rec, r_a, k_a, v_a, p_zrw], [o_rec, p_zgd], post_params, [gd_o_norm_w],
        wa_full, wb_full, wo_full, now2, tr=min(256, n))

    (d_o, d_zgd), (d_onw,) = _pw_bwd("gdn_post_bwd", _gdn_post_f, [o_rec, p_zgd], 0, [gd_o_norm_w], [[d_yb]],
                                     [F32, BF16], seq=seq, tb=tb)
    (d_yrec, dr_p, dk_p, dv_p, d_zrw), d_post_params = _pw_bwd(
        "rwkv_post_bwd", _rwkv_post_f, [y_rec, r_a, k_a, v_a, p_zrw], 0, post_params, [[d_ya]],
        [F32, F32, F32, F32, BF16], seq=seq, tb=tb)
    (dr_a, dlw_a, dk_a, dv_a, dkk_a, db_a), (dr_b, dlw_b, dk_b, dv_b, dkk_b, db_b) = _rec_bwd(
        "rec_bwd", [(rw_six, s_a, d_yrec, 2, False), (gd_six, s_b, d_o, 1, True)], seq=seq)
    (d_qkv, d_ba), d_gd_params = _pw_bwd("gdn_prep_bwd", _gdn_prep_f, [p_qkv, p_ba], 3, gd_params,
                                         [[dr_b], [dlw_b], [dk_b, dkk_b], [dv_b], [db_b]], [BF16, BF16], seq=seq, tb=tb)
    (d_prw,), d_rw_params = _pw_bwd("rwkv_prep_bwd", _rwkv_prep_f, [p_rw], 1, rw_params,
                                    [[dr_a, dr_p], [dlw_a], [dk_a, dk_p], [dv_a, dv_p], [dkk_a], [db_a]], [BF16],
                                    seq=seq, tb=tb)

    dps = [d_prw, d_zrw, d_qkv, d_zgd, d_ba, d_gates]
    wsegs = [w_rw, w_zrw, w_qkv, w_zgd, w_ba, w_gates]
    dx2, d_gin = _proj_dx(x2, norm_in_w, d_xo, dps, wsegs, tm=min(256, n))
    dw_rw = _proj_dw("dw_rw", h, d_prw, tm=tm)
    dw_zrw = _proj_dw("dw_zrw", h, d_zrw, tm=tm)
    dw_qkv = _proj_dw("dw_qkv", h, d_qkv, tm=tm)
    dw_zgd = _proj_dw("dw_zgd", h, d_zgd, tm=tm)
    dw_ba = _proj_dw("dw_ba", h, d_ba, tm=tm)
    dw_gates = _proj_dw("dw_gates", h, d_gates, tm=tm)
    dwt_in_full = jnp.concatenate([dw_rw, dw_zrw, dw_qkv, dw_zgd, dw_ba[:8], dw_gates], axis=0)

    shard_cols = lambda a: jnp.transpose(a.reshape(a.shape[0], 4, 2, a.shape[1] // N_DEV), (2, 1, 0, 3))
    shard_rows = lambda a: jnp.transpose(a.reshape(4, 2, a.shape[0] // N_DEV, a.shape[1]), (1, 0, 2, 3))
    d_mu, d_w0, d_w2p, d_a0, d_a2p, d_kk_, d_ka_ = d_rw_params
    d_gnw, d_gnb, d_rk = d_post_params
    d_conv = jnp.concatenate(d_gd_params[:4], axis=0)
    d_alog = d_gd_params[4].reshape(4, LANES).sum(axis=1).reshape(1, 4)
    d_dtb = d_gd_params[5].reshape(4, LANES).sum(axis=1).reshape(1, 4)
    scat = [shard_rows(dwt_in_full), shard_cols(d_w2p[:64]), shard_cols(d_a2p[64:]), shard_cols(d_conv),
            shard_cols(dwa), shard_cols(dwb), shard_rows(dwo)]
    small_g = _pack_small([d_gin, d_mu, d_w0, d_a0, d_kk_, d_ka_, d_rk, d_gnw, d_gnb, d_alog, d_dtb, d_onw, d_now],
                          loss_acc[0:1])
    scat.append(jnp.stack([small_g, small_g])[:, None])
    pair = _exchange("reduce_cores", ("c",), scat, [], place_own=False)
    part = [_pair_sum("pair_sum_%d" % i, own, got, BF16 if i < 7 else F32)
            for i, (own, got) in enumerate(zip(scat, pair))]
    lands = _exchange("reduce_chips", ("x", "y"), part[:7], [part[7][0]])

    small_w = [norm_in_w, rw_mu, rw_w0, rw_a0, rw_k_k, rw_k_a, rw_r_k, rw_gn_w, rw_gn_b, gd_A_log, gd_dt_bias, gd_o_norm_w, norm_out_w]
    small_m = [m_norm_in_w, m_rw_mu, m_rw_w0, m_rw_a0, m_rw_k_k, m_rw_k_a, m_rw_r_k, m_rw_gn_w, m_rw_gn_b, m_gd_A_log, m_gd_dt_bias, m_gd_o_norm_w, m_norm_out_w]
    small_v = [v_norm_in_w, v_rw_mu, v_rw_w0, v_rw_a0, v_rw_k_k, v_rw_k_a, v_rw_r_k, v_rw_gn_w, v_rw_gn_b, v_gd_A_log, v_gd_dt_bias, v_gd_o_norm_w, v_norm_out_w]
    flat = lambda arrs: [a.reshape(1, -1) for a in arrs]
    sm, loss_row = _adam_small(lands[7], flat(small_w), flat(small_m), flat(small_v))
    sm_g, sm_d, sm_m, sm_v = [{nm: res[i].reshape(w.shape) for (nm, _), res, w in zip(_SMALL, sm, small_w)}
                              for i in range(4)]

    big = {"w_in": [o.T[None] for o in _adam("adam_w_in", lands[0], wt_own, mt_own, vt_own)]}
    for nm, land, w, m, v in (("rw_w2", lands[1], rw_w2, m_rw_w2, v_rw_w2),
                              ("rw_a2", lands[2], rw_a2, m_rw_a2, v_rw_a2),
                              ("gd_conv_w", lands[3], gd_conv_w, m_gd_conv_w, v_gd_conv_w),
                              ("w_branch_a", lands[4], w_branch_a, m_w_branch_a, v_w_branch_a),
                              ("w_branch_b", lands[5], w_branch_b, m_w_branch_b, v_w_branch_b),
                              ("w_out", lands[6], w_out, m_w_out, v_w_out)):
        big[nm] = [o.reshape(w.shape) for o in _adam("adam_" + nm, land, w[0], m[0], v[0])]

    order = ["norm_in_w", "w_in", "rw_mu", "rw_w0", "rw_w2", "rw_a0", "rw_a2", "rw_k_k", "rw_k_a", "rw_r_k", "rw_gn_w",
             "rw_gn_b", "gd_conv_w", "gd_A_log", "gd_dt_bias", "gd_o_norm_w", "w_branch_a", "w_branch_b", "w_out", "norm_out_w"]
    pick = lambda nm, i: big[nm][i] if nm in big else (sm_g, sm_d, sm_m, sm_v)[i][nm]
    loss = loss_row[0, 0]
    grad_x = dx2.reshape(x.shape)
    return (loss, grad_x, *[pick(nm, 0) for nm in order], *[pick(nm, 1) for nm in order],
            *[pick(nm, 2) for nm in order], *[pick(nm, 3) for nm in order])
```

```python
import functools

import jax
import jax.numpy as jnp
from jax import lax
from jax.experimental import pallas as pl
from jax.experimental.pallas import tpu as pltpu

F32 = jnp.float32
BF16 = jnp.bfloat16
HI = lax.Precision.HIGHEST

LANES = 128
SUB = 8
CHUNK = 64
N_DEV = 8
VMEM_LIMIT = 56 * 1024 * 1024

D_MODEL = 1024
RW_W = 512
GD_W = 512
RW_SHIFT = 1664
NORM_EPS = 1e-6
RW_GN_EPS = 64 * 1e-5
ADAM_LR, ADAM_B1, ADAM_B2, ADAM_EPS, ADAM_WD, ADAM_STEP = 0.001, 0.9, 0.999, 1e-8, 0.01, 10


_NN, _NT, _TN = ((1,), (0,)), ((1,), (1,)), ((0,), (0,))


def _dot(a, b, dims, passes):
    precision = lax.Precision.HIGH if passes == 3 else lax.Precision.DEFAULT
    return lax.dot_general(a, b, (dims, ((), ())), precision=precision, preferred_element_type=F32)


def _mm(a, b, passes=3):
    return _dot(a, b, _NN, passes)


def _mm_nt(a, b, passes=3):
    return _dot(a, b, _NT, passes)


def _mm_tn(a, b, passes=3):
    return _dot(a, b, _TN, passes)


P_SUM = 3
P_SCORE = 1
P_INV = 1
P_STATE = 1
P_APPLY = 1
P_UPDATE = 1
P_POINT = 1


def _stack_rows(blocks):
    return jnp.concatenate(blocks, axis=0)


def _split_rows(x, n):
    r = x.shape[0] // n

    @jax.custom_vjp
    def split(x):
        return tuple(x[i * r:(i + 1) * r] for i in range(n))

    split.defvjp(lambda x: (split(x), None), lambda _, gs: (jnp.concatenate(gs, axis=0),))
    return split(x)


def _iota(shape, d):
    return lax.broadcasted_iota(jnp.int32, shape, d)


def _sigmoid(x):
    return 0.5 * (jnp.tanh(0.5 * x) + 1.0)


def _silu(x):
    return x * _sigmoid(x)


def _softplus(x):
    return jnp.maximum(x, 0.0) + jnp.log(1.0 + jnp.exp(-jnp.abs(x)))


def _seg_ones(seg):
    return ((_iota((LANES, LANES), 0) // seg) == (_iota((LANES, LANES), 1) // seg)).astype(F32)


def _sl(g):
    return slice(g * LANES, (g + 1) * LANES)


@jax.custom_vjp
def _tri_inverse(ms):
    return _tri_inverse_chain(ms)


def _tri_inverse_bwd(ts, dts):
    return ([-_mm_nt(_mm_tn(t, dt, P_INV), t, P_INV) for t, dt in zip(ts, dts)],)


def _tri_inverse_chain(ms):
    c = CHUNK
    ri, ci = _iota((c, c), 0), _iota((c, c), 1)
    eye = (ri == ci).astype(F32)
    d16 = (ri // 16) == (ci // 16)
    d32 = (ri // 32) == (ci // 32)
    ps = [jnp.where(d16, -m, 0.0) for m in ms]
    ts = [eye + p for p in ps]
    for _ in range(3):
        ps = [_mm(p, p, P_INV) for p in ps]
        ts = [_mm(t, eye + p, P_INV) for t, p in zip(ts, ps)]
    for off_diagonal in (d32 & (~d16), ~d32):
        tq = [_mm(t, jnp.where(off_diagonal, m, 0.0), P_INV) for t, m in zip(ts, ms)]
        ts = [t - _mm(a, t, P_INV) for t, a in zip(ts, tq)]
    return ts


_tri_inverse.defvjp(lambda ms: (lambda ts: (ts, ts))(_tri_inverse_chain(ms)), _tri_inverse_bwd)


@jax.custom_vjp
def _known_inverse(ms, ts):
    return ts


_known_inverse.defvjp(lambda ms, ts: (ts, ts),
                      lambda ts, dts: (_tri_inverse_bwd(ts, dts)[0], [jnp.zeros_like(t) for t in ts]))


def _chunk_fwd(prims, *, nsub=None, scalar_decay=None, kinds=None, inverses=None):
    c = CHUNK
    ng = len(prims)
    kinds = kinds if kinds is not None else [(nsub, scalar_decay)] * ng
    s0s, rs, lws, ks, vs, kks, bs = [list(t) for t in zip(*prims)]
    ri, ci = _iota((c, c), 0), _iota((c, c), 1)
    incl = ri >= ci
    strict = ri > ci
    tril = incl.astype(F32)
    lane = _iota((1, LANES), 1)
    heads = [n for n, _ in kinds]
    scalar = [sc for _, sc in kinds]
    masks = [[((lane // (LANES // n)) == s).astype(F32) for s in range(n)] if n > 1 else [1.0] for n in heads]
    cws = [_mm(tril, lw, P_SUM) for lw in lws]
    cwxs = [cw - lw for cw, lw in zip(cws, lws)]
    ends = [cw[c - 1:c, :] for cw in cws]
    kkds = [kk * jnp.exp(cwx) for kk, cwx in zip(kks, cwxs)]
    rds = [r * jnp.exp(cw) for r, cw in zip(rs, cws)]
    kends = [k * jnp.exp(e - cw) for k, e, cw in zip(ks, ends, cws)]
    bends = [b * jnp.exp(e - cw) for b, e, cw in zip(bs, ends, cws)]
    state_terms = [_split_rows(_mm_nt(_stack_rows([kkd, rd]), s0, P_STATE), 2) for kkd, rd, s0 in zip(kkds, rds, s0s)]
    w0s, y0s = [t[0] for t in state_terms], [t[1] for t in state_terms]
    chains = [(g, s) for g in range(ng) for s in range(heads[g])]
    e0 = (lane == 0).astype(F32) * jnp.ones((c, 1), F32)
    rows = [_mm_nt(e0, cw, P_SUM) if sc else None for cw, sc in zip(cws, scalar)]
    dxs = [jnp.where(strict, jnp.exp(jnp.minimum(cwx[:, :c] - row, 0.0)), 0.0) if sc else None
           for cwx, row, sc in zip(cwxs, rows, scalar)]
    dis = [jnp.where(incl, jnp.exp(jnp.minimum(cw[:, :c] - row, 0.0)), 0.0) if sc else None
           for cw, row, sc in zip(cws, rows, scalar)]
    lefts = [_stack_rows([a * m for m in ms] + [q * m for m in ms])
             for a, q, ms in zip([kk if sc else kkd for kk, kkd, sc in zip(kks, kkds, scalar)],
                                 [r if sc else rd for r, rd, sc in zip(rs, rds, scalar)], masks)]
    rights_b = [b if sc else b * jnp.exp(-cw) for b, cw, sc in zip(bs, cws, scalar)]
    rights_k = [k if sc else k * jnp.exp(-cw) for k, cw, sc in zip(ks, cws, scalar)]
    on_b = [_split_rows(_mm_nt(left, right, P_SCORE), 2 * n) for left, right, n in zip(lefts, rights_b, heads)]
    on_k = [_split_rows(_mm_nt(left, right, P_SCORE), 2 * n) for left, right, n in zip(lefts, rights_k, heads)]
    lower = lambda x, g: x * dxs[g] if scalar[g] else jnp.where(strict, x, 0.0)
    lower_incl = lambda x, g: x * dis[g] if scalar[g] else jnp.where(incl, x, 0.0)
    m_b = [lower(on_b[g][s], g) for g, s in chains]
    m_k = [lower(on_k[g][s], g) for g, s in chains]
    n_k = [lower_incl(on_k[g][heads[g] + s], g) for g, s in chains]
    n_b = [lower_incl(on_b[g][heads[g] + s], g) for g, s in chains]
    t_inv = _tri_inverse(m_b) if inverses is None else _known_inverse(m_b, inverses)
    on_v = [_split_rows(_mm(_stack_rows([mk, nk]), vs[g], P_APPLY), 2) for (g, s), mk, nk in zip(chains, m_k, n_k)]
    sa_c = [_mm(t, w0s[g] + mv[0], P_APPLY) for (g, s), t, mv in zip(chains, t_inv, on_v)]
    y_c = [y0s[g] + mv[1] - _mm(nb, sa, P_APPLY) for (g, s), mv, nb, sa in zip(chains, on_v, n_b, sa_c)]
    first = [sum(heads[:g]) for g in range(ng)]
    per_group = lambda xs: [functools.reduce(lambda p, q: p + q, [xs[first[g] + s] * masks[g][s] for s in range(heads[g])])
                            for g in range(ng)]
    sas, ys = per_group(sa_c), per_group(y_c)
    s_ends = [s0 * jnp.exp(e) + _mm_tn(_stack_rows([v, -sa]), _stack_rows([kend, bend]), P_UPDATE)
              for s0, e, v, kend, sa, bend in zip(s0s, ends, vs, kends, sas, bends)]
    row_head = lambda n: _iota((LANES, LANES), 0) // (LANES // n)
    col_head = lambda n: _iota((LANES, LANES), 1) // (LANES // n)
    s_ends = [jnp.where(row_head(n) == col_head(n), s_end, 0.0) if n > 1 else s_end for s_end, n in zip(s_ends, heads)]
    return list(zip(ys, s_ends)), t_inv


def _rec_fwd(name, branches, *, seq):
    n, w = branches[0][0][0].shape
    ng = w // LANES
    nc = seq // CHUNK
    nb = n // seq
    nbr = len(branches)
    per = nb * ng
    kinds = [(heads, scalar) for _, heads, scalar in branches for _ in range(per)]
    nts = [per * heads for _, heads, _ in branches]

    def body(*refs):
        in_refs = [refs[6 * i:6 * i + 6] for i in range(nbr)]
        out_refs = [refs[6 * nbr + 3 * i:6 * nbr + 3 * i + 3] for i in range(nbr)]
        states = refs[9 * nbr:]

        @pl.when(pl.program_id(0) == 0)
        def _():
            for state in states:
                state[...] = jnp.zeros_like(state)
        where = [(i, bi, g) for i in range(nbr) for bi in range(nb) for g in range(ng)]
        prims = [(states[i][bi * ng + g],) + tuple(ref[bi, :, _sl(g)] for ref in in_refs[i]) for i, bi, g in where]
        outs, t_inv = _chunk_fwd(prims, kinds=kinds)
        for (i, bi, g), prim, (y, s_end) in zip(where, prims, outs):
            y_ref, s_ref, _ = out_refs[i]
            s_ref[0, bi * ng + g] = prim[0]
            y_ref[bi, :, _sl(g)] = y
            states[i][bi * ng + g] = s_end
        pos = 0
        for i in range(nbr):
            for j in range(nts[i]):
                out_refs[i][2][0, j] = t_inv[pos + j]
            pos += nts[i]

    row = pl.BlockSpec((nb, CHUNK, w), lambda c: (0, c, 0))
    seqs = lambda a: a.reshape(nb, seq, w)
    res = pl.pallas_call(
        body, name=name, grid=(nc,),
        in_specs=[row] * (6 * nbr),
        out_specs=[spec for nt in nts for spec in (row, pl.BlockSpec((1, per, LANES, LANES), lambda c: (c, 0, 0, 0)),
                                                   pl.BlockSpec((1, nt, CHUNK, CHUNK), lambda c: (c, 0, 0, 0)))],
        out_shape=[shp for nt in nts for shp in (jax.ShapeDtypeStruct((nb, seq, w), F32),
                                                 jax.ShapeDtypeStruct((nc, per, LANES, LANES), F32),
                                                 jax.ShapeDtypeStruct((nc, nt, CHUNK, CHUNK), F32))],
        scratch_shapes=[pltpu.VMEM((per, LANES, LANES), F32)] * nbr,
        compiler_params=pltpu.CompilerParams(dimension_semantics=("arbitrary",), vmem_limit_bytes=VMEM_LIMIT),
    )(*[seqs(a) for arrs, _, _ in branches for a in arrs])
    return [(res[3 * i].reshape(n, w), (res[3 * i + 1], res[3 * i + 2])) for i in range(nbr)]


def _rec_bwd(name, branches, *, seq):
    n, w = branches[0][0][0].shape
    ng = w // LANES
    nc = seq // CHUNK
    nb = n // seq
    nbr = len(branches)
    per = nb * ng
    kinds = [(heads, scalar) for _, _, _, heads, scalar in branches for _ in range(per)]
    nts = [per * heads for _, _, _, heads, _ in branches]

    def body(*refs):
        in_refs = [refs[9 * i:9 * i + 9] for i in range(nbr)]
        out_refs = [refs[9 * nbr + 6 * i:9 * nbr + 6 * i + 6] for i in range(nbr)]
        dstates = refs[15 * nbr:]

        @pl.when(pl.program_id(0) == 0)
        def _():
            for dstate in dstates:
                dstate[...] = jnp.zeros_like(dstate)
        where = [(i, bi, g) for i in range(nbr) for bi in range(nb) for g in range(ng)]
        inverses = [in_refs[i][7][0, j] for i in range(nbr) for j in range(nts[i])]
        f = lambda p: _chunk_fwd(p, kinds=kinds, inverses=inverses)[0]
        prims = [(in_refs[i][6][0, bi * ng + g],) + tuple(ref[bi, :, _sl(g)] for ref in in_refs[i][:6])
                 for i, bi, g in where]
        _, vjp = jax.vjp(f, prims)
        (d_prims,) = vjp([(in_refs[i][8][bi, :, _sl(g)], dstates[i][bi * ng + g]) for i, bi, g in where])
        for (i, bi, g), d_prim in zip(where, d_prims):
            dstates[i][bi * ng + g] = d_prim[0]
            for ref, d in zip(out_refs[i], d_prim[1:]):
                ref[bi, :, _sl(g)] = d

    row = pl.BlockSpec((nb, CHUNK, w), lambda c: (0, nc - 1 - c, 0))
    seqs = lambda a: a.reshape(nb, seq, w)
    in_specs, args = [], []
    for (arrs, (s_save, t_save), dy, _, _), nt in zip(branches, nts):
        in_specs += [row] * 6 + [pl.BlockSpec((1, per, LANES, LANES), lambda c: (nc - 1 - c, 0, 0, 0)),
                                 pl.BlockSpec((1, nt, CHUNK, CHUNK), lambda c: (nc - 1 - c, 0, 0, 0)), row]
        args += [seqs(a) for a in arrs] + [s_save, t_save, seqs(dy)]
    grads = pl.pallas_call(
        body, name=name, grid=(nc,),
        in_specs=in_specs,
        out_specs=[row] * (6 * nbr),
        out_shape=[jax.ShapeDtypeStruct((nb, seq, w), F32)] * (6 * nbr),
        scratch_shapes=[pltpu.VMEM((per, LANES, LANES), F32)] * nbr,
        compiler_params=pltpu.CompilerParams(dimension_semantics=("arbitrary",), vmem_limit_bytes=VMEM_LIMIT),
    )(*args)
    return [[g.reshape(n, w) for g in grads[6 * i:6 * i + 6]] for i in range(nbr)]


def _shift_down(a, j, halo, is_start):
    tb = a.shape[0]
    rolled = pltpu.roll(a, j, 0)
    hr = jnp.where(is_start, 0.0, pltpu.roll(halo, j, 0))
    first = jnp.where(_iota((SUB, LANES), 0) < j, hr, rolled[0:SUB])
    if tb == SUB:
        return first
    return jnp.concatenate([first, rolled[SUB:]], axis=0)


def _shift_up(d, j, carry, is_end):
    tb = d.shape[0]
    up = pltpu.roll(d, tb - j, 0)
    cr = jnp.where(is_end, 0.0, pltpu.roll(carry, SUB - j, 0))
    last = jnp.where(_iota((SUB, LANES), 0) >= SUB - j, cr, up[tb - SUB:tb])
    if tb == SUB:
        return last
    return jnp.concatenate([up[:tb - SUB], last], axis=0)


def _ngroups(a):
    return a.shape[1] // LANES


def _pw_fwd(name, f, ins, shift, params, out_widths, out_dtypes, *, seq, tb):
    n = ins[0].shape[0]
    nt, tps = n // tb, seq // tb
    ni, npar = len(ins), len(params)

    def body(*refs):
        in_refs = refs[:ni]
        pos = ni
        halo_ref = None
        if shift:
            halo_ref = refs[pos]
            pos += 1
        p_refs = refs[pos:pos + npar]
        out_refs = refs[pos + npar:]
        is_start = (pl.program_id(0) % tps) == 0
        tiles = [[ref[:, _sl(g)] for g in range(_ngroups(ref))] for ref in in_refs]
        prevs = [[_shift_down(tiles[0][g], j, halo_ref[:, _sl(g)], is_start) for g in range(len(tiles[0]))]
                 for j in range(1, shift + 1)]
        pv = [[ref[:, _sl(g)] for g in range(_ngroups(ref))] for ref in p_refs]
        outs = f(tiles, prevs, pv)
        for o_ref, og in zip(out_refs, outs, strict=True):
            for g, t in enumerate(og):
                o_ref[:, _sl(g)] = t.astype(o_ref.dtype)

    in_specs = [pl.BlockSpec((tb, a.shape[1]), lambda i: (i, 0)) for a in ins]
    args = list(ins)
    if shift:
        in_specs.append(pl.BlockSpec((SUB, ins[0].shape[1]), lambda i: (jnp.maximum(i * (tb // SUB) - 1, 0), 0)))
        args.append(ins[0])
    in_specs += [pl.BlockSpec(p.shape, lambda i: (0, 0)) for p in params]
    args += list(params)
    return pl.pallas_call(
        body, name=name, grid=(nt,),
        in_specs=in_specs,
        out_specs=[pl.BlockSpec((tb, w), lambda i: (i, 0)) for w in out_widths],
        out_shape=[jax.ShapeDtypeStruct((n, w), dt) for w, dt in zip(out_widths, out_dtypes, strict=True)],
        compiler_params=pltpu.CompilerParams(dimension_semantics=("parallel",), vmem_limit_bytes=VMEM_LIMIT),
    )(*args)


def _proj_pw_fwd(name, f, h, wts, shift, params, out_widths, *, seq, tb):
    n = h.shape[0]
    nt, tps = n // tb, seq // tb
    nw, npar = len(wts), len(params)

    def body(*refs):
        h_ref = refs[0]
        w_refs = refs[1:1 + nw]
        par_refs = refs[1 + nw:1 + nw + npar]
        p_refs = refs[1 + nw + npar:1 + 2 * nw + npar]
        out_refs = refs[1 + 2 * nw + npar:len(refs) - 1]
        carry = refs[-1]
        is_start = (pl.program_id(0) % tps) == 0
        for w_ref, p_ref in zip(w_refs, p_refs, strict=True):
            p_ref[...] = lax.dot_general(h_ref[...], w_ref[...], (_NT, ((), ())), preferred_element_type=F32)
        tiles = [[ref[:, _sl(g)] for g in range(_ngroups(ref))] for ref in p_refs]
        prevs = [[_shift_down(tiles[0][g], j, carry[:, _sl(g)], is_start) for g in range(len(tiles[0]))]
                 for j in range(1, shift + 1)]
        carry[...] = p_refs[0][tb - SUB:tb, :]
        pv = [[ref[:, _sl(g)] for g in range(_ngroups(ref))] for ref in par_refs]
        outs = f(tiles, prevs, pv)
        for o_ref, og in zip(out_refs, outs, strict=True):
            for g, t in enumerate(og):
                o_ref[:, _sl(g)] = t

    widths = [w.shape[0] for w in wts] + list(out_widths)
    res = pl.pallas_call(
        body, name=name, grid=(nt,),
        in_specs=([pl.BlockSpec((tb, D_MODEL), lambda i: (i, 0))] + [pl.BlockSpec(w.shape, lambda i: (0, 0)) for w in wts]
                  + [pl.BlockSpec(p.shape, lambda i: (0, 0)) for p in params]),
        out_specs=[pl.BlockSpec((tb, w), lambda i: (i, 0)) for w in widths],
        out_shape=[jax.ShapeDtypeStruct((n, w), F32) for w in widths],
        scratch_shapes=[pltpu.VMEM((SUB, wts[0].shape[0]), F32)],
        compiler_params=pltpu.CompilerParams(dimension_semantics=("arbitrary",), vmem_limit_bytes=VMEM_LIMIT),
    )(h, *wts, *params)
    return res[:nw], res[nw:]


def _pw_bwd(name, f, ins, shift, params, douts, din_dtypes, *, seq, tb):
    n = ins[0].shape[0]
    nt, tps = n // tb, seq // tb
    ni, npar = len(ins), len(params)
    flat_douts = [d for ds in douts for d in ds]
    nd = len(flat_douts)
    w0 = ins[0].shape[1]

    def body(*refs):
        in_refs = refs[:ni]
        pos = ni
        halo_ref = None
        if shift:
            halo_ref = refs[pos]
            pos += 1
        p_refs = refs[pos:pos + npar]
        pos += npar
        d_refs = refs[pos:pos + nd]
        pos += nd
        din_refs = refs[pos:pos + ni]
        pos += ni
        dp_refs = refs[pos:pos + npar]
        pos += npar
        carry = refs[pos] if shift else None
        step = pl.program_id(0)
        tile = nt - 1 - step
        is_start = (tile % tps) == 0
        is_end = (tile % tps) == tps - 1
        tiles = [[ref[:, _sl(g)] for g in range(_ngroups(ref))] for ref in in_refs]
        prevs = [[_shift_down(tiles[0][g], j, halo_ref[:, _sl(g)], is_start) for g in range(len(tiles[0]))]
                 for j in range(1, shift + 1)]
        pv = [[ref[:, _sl(g)] for g in range(_ngroups(ref))] for ref in p_refs]
        cot, pos_d = [], 0
        for ds in douts:
            grp = d_refs[pos_d:pos_d + len(ds)]
            pos_d += len(ds)
            cot.append([functools.reduce(lambda p, q: p + q, [ref[:, _sl(g)].astype(F32) for ref in grp])
                        for g in range(_ngroups(grp[0]))])
        _, vjp = jax.vjp(f, tiles, prevs, pv)
        d_tiles, d_prevs, d_pv = vjp(cot)
        for g in range(len(tiles[0])):
            for j in range(1, shift + 1):
                d_tiles[0][g] = d_tiles[0][g] + _shift_up(d_prevs[j - 1][g], j, carry[j - 1, :, _sl(g)], is_end)
            for j in range(1, shift + 1):
                carry[j - 1, :, _sl(g)] = d_prevs[j - 1][g][0:SUB]
        for ref, dg in zip(din_refs, d_tiles, strict=True):
            for g, t in enumerate(dg):
                ref[:, _sl(g)] = t.astype(ref.dtype)

        @pl.when(step == 0)
        def _():
            for ref in dp_refs:
                ref[...] = jnp.zeros_like(ref)
        for ref, dg in zip(dp_refs, d_pv, strict=True):
            for g, t in enumerate(dg):
                ref[:, _sl(g)] += t

    rev = lambda i: (nt - 1 - i, 0)
    in_specs = [pl.BlockSpec((tb, a.shape[1]), rev) for a in ins]
    args = list(ins)
    if shift:
        in_specs.append(pl.BlockSpec((SUB, w0), lambda i: (jnp.maximum((nt - 1 - i) * (tb // SUB) - 1, 0), 0)))
        args.append(ins[0])
    in_specs += [pl.BlockSpec(p.shape, lambda i: (0, 0)) for p in params]
    args += list(params)
    in_specs += [pl.BlockSpec((tb, d.shape[1]), rev) for d in flat_douts]
    args += flat_douts
    out_specs = [pl.BlockSpec((tb, a.shape[1]), rev) for a in ins] + [pl.BlockSpec(p.shape, lambda i: (0, 0)) for p in params]
    out_shape = ([jax.ShapeDtypeStruct(a.shape, dt) for a, dt in zip(ins, din_dtypes, strict=True)]
                 + [jax.ShapeDtypeStruct(p.shape, F32) for p in params])
    res = pl.pallas_call(
        body, name=name, grid=(nt,),
        in_specs=in_specs, out_specs=out_specs, out_shape=out_shape,
        scratch_shapes=[pltpu.VMEM((shift, SUB, w0), F32)] if shift else [],
        compiler_params=pltpu.CompilerParams(dimension_semantics=("arbitrary",), vmem_limit_bytes=VMEM_LIMIT),
    )(*args)
    return res[:ni], res[ni:]


def _rwkv_prep_f(tiles, prevs, params):
    (p,), (prev,) = tiles, prevs
    mu, w0, w2p, a0, a2p, k_k, k_a = params
    xs = [p[g] + (prev[g] - p[g]) * mu[g] for g in range(13)]
    wdad = xs[12]
    tw = jnp.tanh(wdad)
    e64 = _seg_ones(64)
    r, lw, k2, v, kk, b = [], [], [], [], [], []
    for g in range(4):
        k_g = xs[4 + g]
        lo = w0[g] + _mm(tw, w2p[g], P_POINT)
        lw_g = -jnp.exp(-_softplus(-lo) - 0.5)
        a_g = _sigmoid(a0[g] + _mm(wdad, a2p[g], P_POINT))
        kkp = k_g * k_k[g]
        kk_g = kkp * lax.rsqrt(_mm(kkp * kkp, e64, P_POINT) + 1e-12)
        r.append(xs[g])
        lw.append(lw_g)
        k2.append(k_g * (1.0 + (a_g - 1.0) * k_a[g]))
        v.append(xs[8 + g])
        kk.append(kk_g)
        b.append(kk_g * a_g)
    return [r, lw, k2, v, kk, b]


def _rwkv_post_f(tiles, prevs, params):
    yrec, r, k2, v, z = tiles
    gn_w, gn_b, r_k = params
    e64 = _seg_ones(64)
    out = []
    for g in range(4):
        mean = _mm(yrec[g], e64, P_POINT) * (1.0 / 64)
        d = yrec[g] - mean
        var = _mm(d * d, e64, P_POINT) * (1.0 / 64)
        yn = d * lax.rsqrt(var + RW_GN_EPS) * gn_w[g] + gn_b[g]
        bonus = _mm(r[g] * k2[g] * r_k[g], e64, P_POINT) * v[g]
        out.append((yn + bonus) * _silu(z[g]))
    return [out]


def _gdn_prep_f(tiles, prevs, params):
    x, (ba,) = tiles
    p1, p2, p3 = prevs
    cw0, cw1, cw2, cw3, a_log, dt_bias = params
    s = [_silu(cw3[g] * x[g] + cw2[g] * p1[g] + cw1[g] * p2[g] + cw0[g] * p3[g]) for g in range(12)]
    row = _iota((LANES, LANES), 0)
    r, lw, k, vv, b = [], [], [], [], []
    for h in range(4):
        q_h, k_h, v_h = s[h], s[4 + h], s[8 + h]
        qn = q_h * lax.rsqrt(jnp.sum(q_h * q_h, axis=-1, keepdims=True) + 1e-12)
        kn = k_h * lax.rsqrt(jnp.sum(k_h * k_h, axis=-1, keepdims=True) + 1e-12)
        beta = _sigmoid(_mm(ba, (row == h).astype(F32)))
        alpha = _mm(ba, (row == 4 + h).astype(F32))
        g_h = -jnp.exp(a_log[h]) * _softplus(alpha + dt_bias[h])
        r.append(qn * (LANES ** -0.5))
        lw.append(g_h)
        k.append(kn)
        vv.append(beta * v_h)
        b.append(jnp.exp(g_h) * beta * kn)
    return [r, lw, k, vv, b]


def _gdn_post_f(tiles, prevs, params):
    o, z = tiles
    ((onw,),) = params
    out = []
    for h in range(4):
        ms = jnp.mean(o[h] * o[h], axis=-1, keepdims=True)
        out.append(o[h] * lax.rsqrt(ms + NORM_EPS) * onw * _silu(z[h]))
    return [out]


def _norm_in(x2, g_in, *, tm):
    n = x2.shape[0]

    def body(x_ref, g_ref, h_ref):
        x = x_ref[...]
        rs = lax.rsqrt(jnp.mean(x * x, axis=-1, keepdims=True) + NORM_EPS)
        h_ref[...] = (x * rs * g_ref[...]).astype(BF16)

    return pl.pallas_call(
        body, name="norm_in", grid=(n // tm,),
        in_specs=[pl.BlockSpec((tm, D_MODEL), lambda i: (i, 0)), pl.BlockSpec((1, D_MODEL), lambda i: (0, 0))],
        out_specs=pl.BlockSpec((tm, D_MODEL), lambda i: (i, 0)),
        out_shape=jax.ShapeDtypeStruct((n, D_MODEL), BF16),
        compiler_params=pltpu.CompilerParams(dimension_semantics=("parallel",), vmem_limit_bytes=VMEM_LIMIT),
    )(x2, g_in)


def _proj(name, h, wt, *, tm):
    n, ws = h.shape[0], wt.shape[0]

    def body(h_ref, w_ref, o_ref):
        o_ref[...] = lax.dot_general(h_ref[...], w_ref[...], (_NT, ((), ())), preferred_element_type=F32)

    return pl.pallas_call(
        body, name=name, grid=(n // tm,),
        in_specs=[pl.BlockSpec((tm, D_MODEL), lambda i: (i, 0)), pl.BlockSpec((ws, D_MODEL), lambda i: (0, 0))],
        out_specs=pl.BlockSpec((tm, ws), lambda i: (i, 0)),
        out_shape=jax.ShapeDtypeStruct((n, ws), F32),
        compiler_params=pltpu.CompilerParams(dimension_semantics=("parallel",), vmem_limit_bytes=VMEM_LIMIT),
    )(h, wt)


def _proj_dw(name, h, dp, *, tm):
    n, ws = dp.shape

    def body(h_ref, d_ref, o_ref):
        @pl.when(pl.program_id(0) == 0)
        def _():
            o_ref[...] = jnp.zeros_like(o_ref)
        o_ref[...] += lax.dot_general(d_ref[...], h_ref[...], (_TN, ((), ())), preferred_element_type=F32)

    return pl.pallas_call(
        body, name=name, grid=(n // tm,),
        in_specs=[pl.BlockSpec((tm, D_MODEL), lambda i: (i, 0)), pl.BlockSpec((tm, ws), lambda i: (i, 0))],
        out_specs=pl.BlockSpec((ws, D_MODEL), lambda i: (0, 0)),
        out_shape=jax.ShapeDtypeStruct((ws, D_MODEL), F32),
        compiler_params=pltpu.CompilerParams(dimension_semantics=("arbitrary",), vmem_limit_bytes=VMEM_LIMIT),
    )(h, dp)


def _proj_dx(x2, g_in, d_xo, dps, ws, *, tm):
    n = x2.shape[0]
    ns = len(dps)

    def body(*refs):
        x_ref, g_ref, dxo_ref = refs[:3]
        dp_refs = refs[3:3 + ns]
        w_refs = refs[3 + ns:3 + 2 * ns]
        dx_ref, dg_ref = refs[3 + 2 * ns:]
        dh = jnp.zeros((tm, D_MODEL), F32)
        for d_ref, w_ref in zip(dp_refs, w_refs, strict=True):
            dh = dh + jnp.dot(d_ref[...], w_ref[...], preferred_element_type=F32)
        x = x_ref[...]
        rs = lax.rsqrt(jnp.mean(x * x, axis=-1, keepdims=True) + NORM_EPS)
        xn = x * rs
        dxn = dh * g_ref[...]
        dx_ref[...] = dxo_ref[...] + rs * (dxn - xn * jnp.mean(dxn * xn, axis=-1, keepdims=True))

        @pl.when(pl.program_id(0) == 0)
        def _():
            dg_ref[...] = jnp.zeros_like(dg_ref)
        dg_ref[...] += jnp.sum(dh * xn, axis=0, keepdims=True)

    row = pl.BlockSpec((tm, D_MODEL), lambda i: (i, 0))
    return pl.pallas_call(
        body, name="proj_dx", grid=(n // tm,),
        in_specs=([row, pl.BlockSpec((1, D_MODEL), lambda i: (0, 0)), row]
                  + [pl.BlockSpec((tm, d.shape[1]), lambda i: (i, 0)) for d in dps]
                  + [pl.BlockSpec(w.shape, lambda i: (0, 0)) for w in ws]),
        out_specs=[row, pl.BlockSpec((1, D_MODEL), lambda i: (0, 0))],
        out_shape=[jax.ShapeDtypeStruct((n, D_MODEL), F32), jax.ShapeDtypeStruct((1, D_MODEL), F32)],
        compiler_params=pltpu.CompilerParams(dimension_semantics=("arbitrary",), vmem_limit_bytes=VMEM_LIMIT),
    )(x2, g_in, d_xo, *dps, *ws)


def _tail(x2, tgt2, gates, rw_post_ins, gd_post_ins, rw_post_params, gd_post_params, din_dtypes, w_a, w_b, w_o, now, *, tr):
    n = x2.shape[0]
    n_rw, n_gd = len(rw_post_ins), len(gd_post_ins)
    n_rwp, n_gdp = len(rw_post_params), len(gd_post_params)

    def body(*refs):
        x_ref, t_ref, g_ref = refs[:3]
        pos = 3
        rw_refs, gd_refs = refs[pos:pos + n_rw], refs[pos + n_rw:pos + n_rw + n_gd]
        pos += n_rw + n_gd
        rwp_refs, gdp_refs = refs[pos:pos + n_rwp], refs[pos + n_rwp:pos + n_rwp + n_gdp]
        pos += n_rwp + n_gdp
        wa_ref, wb_ref, wo_ref, now_ref = refs[pos:pos + 4]
        pos += 4
        d_rw_refs, d_gd_refs = refs[pos:pos + n_rw], refs[pos + n_rw:pos + n_rw + n_gd]
        pos += n_rw + n_gd
        dg_ref, dxo_ref, dwa_ref, dwb_ref, dwo_ref, dnow_ref, loss_ref = refs[pos:pos + 7]
        d_rwp_refs, d_gdp_refs = refs[pos + 7:pos + 7 + n_rwp], refs[pos + 7 + n_rwp:]
        groups = lambda rs: [[ref[:, _sl(g)] for g in range(_ngroups(ref))] for ref in rs]
        (ya_groups,), rw_vjp = jax.vjp(lambda t, p: _rwkv_post_f(t, [], p), groups(rw_refs), groups(rwp_refs))
        (yb_groups,), gd_vjp = jax.vjp(lambda t, p: _gdn_post_f(t, [], p), groups(gd_refs), groups(gdp_refs))
        ya16 = jnp.concatenate(ya_groups, axis=1).astype(BF16)
        yb16 = jnp.concatenate(yb_groups, axis=1).astype(BF16)
        ua = jnp.dot(ya16, wa_ref[...], preferred_element_type=F32)
        ub = jnp.dot(yb16, wb_ref[...], preferred_element_type=F32)
        ga = _sigmoid(g_ref[:, :D_MODEL])
        gb = _sigmoid(g_ref[:, D_MODEL:])
        m16 = (ga * ua + gb * ub).astype(BF16)
        xo = x_ref[...] + jnp.dot(m16, wo_ref[...], preferred_element_type=F32)
        rs = lax.rsqrt(jnp.mean(xo * xo, axis=-1, keepdims=True) + NORM_EPS)
        yn = xo * rs
        now_v = now_ref[...]
        err = yn * now_v - t_ref[...]
        dy = err * (1.0 / D_MODEL)
        dyn = dy * now_v
        dxo = rs * (dyn - yn * jnp.mean(dyn * yn, axis=-1, keepdims=True))
        dxo_ref[...] = dxo
        dxo16 = dxo.astype(BF16)
        dm = lax.dot_general(dxo16, wo_ref[...], (((1,), (1,)), ((), ())), preferred_element_type=F32)
        dua16 = (dm * ga).astype(BF16)
        dub16 = (dm * gb).astype(BF16)
        dg_ref[:, :D_MODEL] = (dm * ua * ga * (1.0 - ga)).astype(dg_ref.dtype)
        dg_ref[:, D_MODEL:] = (dm * ub * gb * (1.0 - gb)).astype(dg_ref.dtype)
        dya = lax.dot_general(dua16, wa_ref[...], (((1,), (1,)), ((), ())), preferred_element_type=F32)
        dyb = lax.dot_general(dub16, wb_ref[...], (((1,), (1,)), ((), ())), preferred_element_type=F32)
        d_rw_tiles, d_rw_pv = rw_vjp([[dya[:, _sl(g)] for g in range(RW_W // LANES)]])
        d_gd_tiles, d_gd_pv = gd_vjp([[dyb[:, _sl(g)] for g in range(GD_W // LANES)]])
        for ref, dgroups in zip(d_rw_refs + d_gd_refs, d_rw_tiles + d_gd_tiles, strict=True):
            for g, t in enumerate(dgroups):
                ref[:, _sl(g)] = t.astype(ref.dtype)

        @pl.when(pl.program_id(0) == 0)
        def _():
            for ref in (dwa_ref, dwb_ref, dwo_ref, dnow_ref, loss_ref) + d_rwp_refs + d_gdp_refs:
                ref[...] = jnp.zeros_like(ref)
        for ref, dgroups in zip(d_rwp_refs + d_gdp_refs, d_rw_pv + d_gd_pv, strict=True):
            for g, t in enumerate(dgroups):
                ref[:, _sl(g)] += t
        tn = (((0,), (0,)), ((), ()))
        dwo_ref[...] += lax.dot_general(m16, dxo16, tn, preferred_element_type=F32)
        dwa_ref[...] += lax.dot_general(ya16, dua16, tn, preferred_element_type=F32)
        dwb_ref[...] += lax.dot_general(yb16, dub16, tn, preferred_element_type=F32)
        dnow_ref[...] += jnp.sum(dy * yn, axis=0, keepdims=True)
        loss_ref[...] += (0.5 / D_MODEL) * jnp.sum(err * err)

    row = lambda w: pl.BlockSpec((tr, w), lambda i: (i, 0))
    full = lambda a: pl.BlockSpec(a.shape, lambda i: (0, 0))
    res = pl.pallas_call(
        body, name="tail", grid=(n // tr,),
        in_specs=([row(D_MODEL), row(D_MODEL), row(2 * D_MODEL)] + [row(a.shape[1]) for a in rw_post_ins + gd_post_ins]
                  + [full(p) for p in rw_post_params + gd_post_params] + [full(w_a), full(w_b), full(w_o), full(now)]),
        out_specs=([row(a.shape[1]) for a in rw_post_ins + gd_post_ins] + [row(2 * D_MODEL), row(D_MODEL),
                   pl.BlockSpec((RW_W, D_MODEL), lambda i: (0, 0)), pl.BlockSpec((GD_W, D_MODEL), lambda i: (0, 0)),
                   pl.BlockSpec((D_MODEL, D_MODEL), lambda i: (0, 0)), pl.BlockSpec((1, D_MODEL), lambda i: (0, 0)),
                   pl.BlockSpec((SUB, LANES), lambda i: (0, 0))] + [full(p) for p in rw_post_params + gd_post_params]),
        out_shape=([jax.ShapeDtypeStruct(a.shape, dt) for a, dt in zip(rw_post_ins + gd_post_ins, din_dtypes, strict=True)]
                   + [jax.ShapeDtypeStruct((n, 2 * D_MODEL), BF16), jax.ShapeDtypeStruct((n, D_MODEL), F32),
                      jax.ShapeDtypeStruct((RW_W, D_MODEL), F32), jax.ShapeDtypeStruct((GD_W, D_MODEL), F32),
                      jax.ShapeDtypeStruct((D_MODEL, D_MODEL), F32), jax.ShapeDtypeStruct((1, D_MODEL), F32),
                      jax.ShapeDtypeStruct((SUB, LANES), F32)]
                   + [jax.ShapeDtypeStruct(p.shape, F32) for p in rw_post_params + gd_post_params]),
        compiler_params=pltpu.CompilerParams(dimension_semantics=("arbitrary",), vmem_limit_bytes=VMEM_LIMIT),
    )(x2, tgt2, gates, *rw_post_ins, *gd_post_ins, *rw_post_params, *gd_post_params, w_a, w_b, w_o, now)
    ni, npar = n_rw + n_gd, n_rwp + n_gdp
    return res[:ni], res[ni:ni + 7], res[ni + 7:ni + 7 + npar]


def _exchange(name, axes, scatter, gather, place_own=True):
    ns, ng = len(scatter), len(gather)
    na = ns + ng
    gs = 2 ** len(axes)
    arrs = list(scatter) + list(gather)

    def body(*refs):
        src = refs[:na]
        dst = refs[na:2 * na]
        send_sems, recv_sems = refs[2 * na:]
        mine = {ax: lax.axis_index(ax) for ax in ("x", "y", "c")}

        def peer(k):
            co = dict(mine)
            for i, ax in enumerate(axes):
                if (k >> (len(axes) - 1 - i)) & 1:
                    co[ax] = 1 - co[ax]
            idx = 0
            for ax in axes:
                idx = 2 * idx + co[ax]
            return (co["x"], co["y"], co["c"]), idx

        _, me = peer(0)

        def copy(a, k, landing):
            dev, idx = peer(k)
            s = src[a].at[idx] if a < ns else src[a]
            return pltpu.make_async_remote_copy(src_ref=s, dst_ref=dst[a].at[idx if landing else me],
                                                send_sem=send_sems.at[a, k - 1], recv_sem=recv_sems.at[a, k - 1],
                                                device_id=dev, device_id_type=pl.DeviceIdType.MESH)

        sends = [copy(a, k, False) for a in range(na) for k in range(1, gs)]
        for cp in sends:
            cp.start()
        for a in range(na):
            for k in range(1, gs):
                copy(a, k, True).wait_recv()
        for cp in sends:
            cp.wait_send()

    out_shape = [jax.ShapeDtypeStruct(a.shape, a.dtype) for a in scatter] + \
                [jax.ShapeDtypeStruct((gs,) + a.shape, a.dtype) for a in gather]
    anyspec = pl.BlockSpec(memory_space=pl.ANY)
    lands = pl.pallas_call(
        body, name=name,
        in_specs=[anyspec] * na, out_specs=[anyspec] * na, out_shape=out_shape,
        scratch_shapes=[pltpu.SemaphoreType.DMA((na, gs - 1)), pltpu.SemaphoreType.DMA((na, gs - 1))],
    )(*arrs)
    if not place_own:
        return lands
    me = 0
    for ax in axes:
        me = 2 * me + lax.axis_index(ax)
    kept = [lax.dynamic_index_in_dim(a, me, 0, keepdims=False) for a in scatter] + list(gather)
    return [lax.dynamic_update_index_in_dim(land, mine, me, 0) for land, mine in zip(lands, kept)]


def _gather_all(name, arrs):
    na = len(arrs)

    def body(*refs):
        src = refs[:na]
        dst = refs[na:2 * na]
        send_sems, recv_sems = refs[2 * na:]
        x, y, c = lax.axis_index("x"), lax.axis_index("y"), lax.axis_index("c")
        sibling = (x, y, 1 - c)
        chips = [(1 - x, y), (x, 1 - y), (1 - x, 1 - y)]

        def copy(a, k, block, to, own=False):
            px, py, pc = block
            slot = dst[a].at[pc, 2 * px + py]
            return pltpu.make_async_remote_copy(src_ref=src[a] if own else slot, dst_ref=slot,
                                                send_sem=send_sems.at[a, k], recv_sem=recv_sems.at[a, k],
                                                device_id=to, device_id_type=pl.DeviceIdType.MESH)

        first = [copy(a, 0, (x, y, c), sibling, own=True) for a in range(na)]
        first += [copy(a, 1 + j, (x, y, c), (*chip, c), own=True) for j, chip in enumerate(chips) for a in range(na)]
        for cp in first:
            cp.start()
        passed = []
        for j, chip in enumerate(chips):
            for a in range(na):
                copy(a, 1 + j, (*chip, c), (x, y, c)).wait_recv()
                passed.append(copy(a, 4 + j, (*chip, c), sibling))
                passed[-1].start()
        for a in range(na):
            copy(a, 0, (x, y, 1 - c), (x, y, c)).wait_recv()
            for j, chip in enumerate(chips):
                copy(a, 4 + j, (*chip, 1 - c), (x, y, c)).wait_recv()
        for cp in first + passed:
            cp.wait_send()

    anyspec = pl.BlockSpec(memory_space=pl.ANY)
    lands = pl.pallas_call(
        body, name=name,
        in_specs=[anyspec] * na, out_specs=[anyspec] * na,
        out_shape=[jax.ShapeDtypeStruct((2, 4) + a.shape, a.dtype) for a in arrs],
        scratch_shapes=[pltpu.SemaphoreType.DMA((na, 7)), pltpu.SemaphoreType.DMA((na, 7))],
    )(*arrs)
    core, chip = lax.axis_index("c"), 2 * lax.axis_index("x") + lax.axis_index("y")
    zero = jnp.zeros((), jnp.int32)
    return [lax.dynamic_update_slice(land, mine[None, None], (core, chip) + (zero,) * mine.ndim)
            for land, mine in zip(lands, arrs)]


def _pair_sum(name, own, land, out_dtype):
    _, nq, r, c = own.shape
    core = lax.axis_index("c").astype(jnp.int32).reshape(1)

    def body(core_ref, own_ref, land_ref, o_ref):
        o_ref[0] = (own_ref[0, 0] + land_ref[0, 0]).astype(o_ref.dtype)

    return pl.pallas_call(
        body, name=name,
        grid_spec=pltpu.PrefetchScalarGridSpec(
            num_scalar_prefetch=1, grid=(nq,),
            in_specs=[pl.BlockSpec((1, 1, r, c), lambda i, core_ref: (core_ref[0], i, 0, 0)),
                      pl.BlockSpec((1, 1, r, c), lambda i, core_ref: (1 - core_ref[0], i, 0, 0))],
            out_specs=pl.BlockSpec((1, r, c), lambda i, core_ref: (i, 0, 0))),
        out_shape=jax.ShapeDtypeStruct((nq, r, c), out_dtype),
        compiler_params=pltpu.CompilerParams(dimension_semantics=("parallel",), vmem_limit_bytes=VMEM_LIMIT),
    )(core, own, land)


def _adam(name, land, w, m, v):
    r, c = w.shape
    nslot = land.shape[0]
    tr = 256 if (r % 256 == 0 and r > 256) else r
    tc = 256 if (tr == r and r > 256 and c % 256 == 0) else c

    def body(l_ref, w_ref, m_ref, v_ref, g_out, d_out, m_out, v_out):
        g = l_ref[0].astype(F32)
        for s in range(1, nslot):
            g = g + l_ref[s].astype(F32)
        g_out[...] = g
        d_out[...], m_out[...], v_out[...] = _adam_math(g, w_ref[...], m_ref[...], v_ref[...])

    blk = pl.BlockSpec((tr, tc), lambda i: (i * tc // c, i % (c // tc)))
    return pl.pallas_call(
        body, name=name, grid=((r // tr) * (c // tc),),
        in_specs=[pl.BlockSpec((nslot, tr, tc), lambda i: (0, i * tc // c, i % (c // tc))), blk, blk, blk],
        out_specs=[blk] * 4,
        out_shape=[jax.ShapeDtypeStruct((r, c), F32)] * 4,
        compiler_params=pltpu.CompilerParams(dimension_semantics=("parallel",), vmem_limit_bytes=VMEM_LIMIT),
    )(land, w, m, v)


def _adam_math(g, w, m, v):
    c1 = 1.0 / (1.0 - ADAM_B1 ** ADAM_STEP)
    c2 = 1.0 / (1.0 - ADAM_B2 ** ADAM_STEP)
    m_new = ADAM_B1 * m + (1.0 - ADAM_B1) * g
    v_new = ADAM_B2 * v + (1.0 - ADAM_B2) * (g * g)
    return -ADAM_LR * ((m_new * c1) / (jnp.sqrt(v_new * c2) + ADAM_EPS) + ADAM_WD * w), m_new, v_new


def _adam_small(land, ws, ms, vs):
    npar = len(ws)
    nslot = land.shape[0]

    def body(*refs):
        l_ref = refs[0]
        w_refs, m_refs, v_refs = refs[1:1 + npar], refs[1 + npar:1 + 2 * npar], refs[1 + 2 * npar:1 + 3 * npar]
        outs = refs[1 + 3 * npar:1 + 7 * npar]
        loss_ref, g_rows = refs[1 + 7 * npar], refs[2 + 7 * npar]
        g = l_ref[0]
        for s in range(1, nslot):
            g = g + l_ref[s]
        g_rows[...] = g
        row = 0
        for i, (_, size) in enumerate(_SMALL):
            for j in range(-(-size // LANES)):
                width = min(LANES, size - j * LANES)
                cols = slice(j * LANES, j * LANES + width)
                g_ij = g_rows[row:row + 1, 0:width]
                delta, m_new, v_new = _adam_math(g_ij, w_refs[i][:, cols], m_refs[i][:, cols], v_refs[i][:, cols])
                for ref, val in zip(outs[4 * i:4 * i + 4], (g_ij, delta, m_new, v_new)):
                    ref[:, cols] = val
                row += 1
        loss_ref[...] = g_rows[row:row + 1, :]

    full = lambda a: pl.BlockSpec(a.shape, lambda: (0,) * a.ndim)
    res = pl.pallas_call(
        body, name="adam_small",
        in_specs=[full(land)] + [full(a) for a in list(ws) + list(ms) + list(vs)],
        out_specs=[full(w) for w in ws for _ in range(4)] + [pl.BlockSpec((1, LANES), lambda: (0, 0))],
        out_shape=[jax.ShapeDtypeStruct(w.shape, F32) for w in ws for _ in range(4)] + [jax.ShapeDtypeStruct((1, LANES), F32)],
        scratch_shapes=[pltpu.VMEM(land.shape[1:], F32)],
    )(land, *ws, *ms, *vs)
    return [res[4 * i:4 * i + 4] for i in range(npar)], res[4 * npar]


_SMALL = (("norm_in_w", 1024), ("rw_mu", 1664), ("rw_w0", 512), ("rw_a0", 512), ("rw_k_k", 512), ("rw_k_a", 512),
          ("rw_r_k", 512), ("rw_gn_w", 512), ("rw_gn_b", 512), ("gd_A_log", 4), ("gd_dt_bias", 4), ("gd_o_norm_w", 128),
          ("norm_out_w", 1024))
_SMALL_ROWS = 64


def _pack_small(vals, loss_row):
    rows = []
    for (_, size), a in zip(_SMALL, vals, strict=True):
        flat = a.reshape(-1).astype(F32)
        pad = (-size) % LANES
        if pad:
            flat = jnp.concatenate([flat, jnp.zeros((pad,), F32)])
        rows.append(flat.reshape(-1, LANES))
    rows.append(loss_row)
    used = sum(r.shape[0] for r in rows)
    rows.append(jnp.zeros((_SMALL_ROWS - used, LANES), F32))
    return jnp.concatenate(rows, axis=0)


def kernel(x, norm_in_w, w_in, rw_mu, rw_w0, rw_w2, rw_a0, rw_a2, rw_k_k, rw_k_a, rw_r_k, rw_gn_w, rw_gn_b, gd_conv_w, gd_A_log, gd_dt_bias, gd_o_norm_w, w_branch_a, w_branch_b, w_out, norm_out_w, loss_target, m_norm_in_w, m_w_in, m_rw_mu, m_rw_w0, m_rw_w2, m_rw_a0, m_rw_a2, m_rw_k_k, m_rw_k_a, m_rw_r_k, m_rw_gn_w, m_rw_gn_b, m_gd_conv_w, m_gd_A_log, m_gd_dt_bias, m_gd_o_norm_w, m_w_branch_a, m_w_branch_b, m_w_out, m_norm_out_w, v_norm_in_w, v_w_in, v_rw_mu, v_rw_w0, v_rw_w2, v_rw_a0, v_rw_a2, v_rw_k_k, v_rw_k_a, v_rw_r_k, v_rw_gn_w, v_rw_gn_b, v_gd_conv_w, v_gd_A_log, v_gd_dt_bias, v_gd_o_norm_w, v_w_branch_a, v_w_branch_b, v_w_out, v_norm_out_w):
    nb, seq, _ = x.shape
    n = nb * seq
    tm = min(1024, n)
    tb = min(512, seq)
    x2 = x.reshape(n, D_MODEL)
    tgt2 = loss_target.reshape(n, D_MODEL)
    cols = w_in.shape[2]
    in_cols = cols * N_DEV

    wt_own, mt_own, vt_own = w_in[0].T, m_w_in[0].T, v_w_in[0].T
    sharded = [wt_own.astype(BF16), rw_w2[0], rw_a2[0], gd_conv_w[0], w_branch_a[0].astype(BF16),
               w_branch_b[0].astype(BF16), w_out[0].astype(BF16)]
    g_win, g_w2, g_a2, g_conv, g_wa, g_wb, g_wo = _gather_all("gather_weights", sharded)
    unshard_rows = lambda a: jnp.transpose(a, (1, 0, 2, 3)).reshape(N_DEV * a.shape[2], a.shape[3])
    unshard_cols = lambda a: jnp.transpose(a, (2, 1, 0, 3)).reshape(a.shape[2], N_DEV * a.shape[3])
    wt_full = unshard_rows(g_win)
    seg_bounds = ((0, 1664), (1664, 2176), (2176, 3712), (3712, 4224), (4232, in_cols))
    w_rw, w_zrw, w_qkv, w_zgd, w_gates = [wt_full[a:b] for a, b in seg_bounds]
    w_ba = jnp.concatenate([wt_full[4224:4232], jnp.zeros((LANES - 8, D_MODEL), BF16)], axis=0)
    w2_full, a2_full = unshard_cols(g_w2), unshard_cols(g_a2)
    zeros64 = jnp.zeros((64, RW_W), F32)
    w2p = jnp.concatenate([w2_full, zeros64], axis=0)
    a2p = jnp.concatenate([zeros64, a2_full], axis=0)
    conv_full = unshard_cols(g_conv)
    conv_rows = [conv_full[i:i + 1] for i in range(4)]
    wa_full = unshard_cols(g_wa)
    wb_full = unshard_cols(g_wb)
    wo_full = unshard_rows(g_wo)
    a_log_bc = jnp.repeat(gd_A_log, LANES, axis=1)
    dt_bias_bc = jnp.repeat(gd_dt_bias, LANES, axis=1)
    r_k_flat = rw_r_k.reshape(1, RW_W)
    now2 = norm_out_w.reshape(1, D_MODEL)

    h = _norm_in(x2, norm_in_w, tm=tm)
    p_zrw = _proj("proj_zrw", h, w_zrw, tm=tm)
    p_zgd = _proj("proj_zgd", h, w_zgd, tm=tm)
    p_gates = _proj("proj_gates", h, w_gates, tm=tm)
    rw_params = [rw_mu, rw_w0, w2p, rw_a0, a2p, rw_k_k, rw_k_a]
    (p_rw,), (r_a, lw_a, k_a, v_a, kk_a, b_a) = _proj_pw_fwd("proj_rwkv_prep", _rwkv_prep_f, h, [w_rw], 1, rw_params,
                                                             [RW_W] * 6, seq=seq, tb=tb)
    gd_params = conv_rows + [a_log_bc, dt_bias_bc]
    (p_qkv, p_ba), (r_b, lw_b, k_b, v_b, b_b) = _proj_pw_fwd("proj_gdn_prep", _gdn_prep_f, h, [w_qkv, w_ba], 3, gd_params,
                                                             [GD_W] * 5, seq=seq, tb=tb)
    rw_six, gd_six = (r_a, lw_a, k_a, v_a, kk_a, b_a), (r_b, lw_b, k_b, v_b, k_b, b_b)
    (y_rec, s_a), (o_rec, s_b) = _rec_fwd("rec", [(rw_six, 2, False), (gd_six, 1, True)], seq=seq)
    post_params = [rw_gn_w, rw_gn_b, r_k_flat]
    ((d_yrec, dr_p, dk_p, dv_p, d_zrw, d_o, d_zgd), (d_gates, d_xo, dwa, dwb, dwo, d_now, loss_acc),
     (*d_post_params, d_onw)) = _tail(
        x2, tgt2, p_gates, [y_rec, r_a, k_a, v_a, p_zrw], [o_rec, p_zgd], post_params, [gd_o_norm_w],
        [F32, F32, F32, F32, BF16, F32, BF16], wa_full, wb_full, wo_full, now2, tr=min(256, n))

    (dr_a, dlw_a, dk_a, dv_a, dkk_a, db_a), (dr_b, dlw_b, dk_b, dv_b, dkk_b, db_b) = _rec_bwd(
        "rec_bwd", [(rw_six, s_a, d_yrec, 2, False), (gd_six, s_b, d_o, 1, True)], seq=seq)
    (d_qkv, d_ba), d_gd_params = _pw_bwd("gdn_prep_bwd", _gdn_prep_f, [p_qkv, p_ba], 3, gd_params,
                                         [[dr_b], [dlw_b], [dk_b, dkk_b], [dv_b], [db_b]], [BF16, BF16], seq=seq, tb=tb)
    (d_prw,), d_rw_params = _pw_bwd("rwkv_prep_bwd", _rwkv_prep_f, [p_rw], 1, rw_params,
                                    [[dr_a, dr_p], [dlw_a], [dk_a, dk_p], [dv_a, dv_p], [dkk_a], [db_a]], [BF16],
                                    seq=seq, tb=tb)

    dps = [d_prw, d_zrw, d_qkv, d_zgd, d_ba, d_gates]
    wsegs = [w_rw, w_zrw, w_qkv, w_zgd, w_ba, w_gates]
    dx2, d_gin = _proj_dx(x2, norm_in_w, d_xo, dps, wsegs, tm=min(256, n))
    dw_rw = _proj_dw("dw_rw", h, d_prw, tm=tm)
    dw_zrw = _proj_dw("dw_zrw", h, d_zrw, tm=tm)
    dw_qkv = _proj_dw("dw_qkv", h, d_qkv, tm=tm)
    dw_zgd = _proj_dw("dw_zgd", h, d_zgd, tm=tm)
    dw_ba = _proj_dw("dw_ba", h, d_ba, tm=tm)
    dw_gates = _proj_dw("dw_gates", h, d_gates, tm=tm)
    dwt_in_full = jnp.concatenate([dw_rw, dw_zrw, dw_qkv, dw_zgd, dw_ba[:8], dw_gates], axis=0)

    shard_cols = lambda a: jnp.transpose(a.reshape(a.shape[0], 4, 2, a.shape[1] // N_DEV), (2, 1, 0, 3))
    shard_rows = lambda a: jnp.transpose(a.reshape(4, 2, a.shape[0] // N_DEV, a.shape[1]), (1, 0, 2, 3))
    d_mu, d_w0, d_w2p, d_a0, d_a2p, d_kk_, d_ka_ = d_rw_params
    d_gnw, d_gnb, d_rk = d_post_params
    d_conv = jnp.concatenate(d_gd_params[:4], axis=0)
    d_alog = d_gd_params[4].reshape(4, LANES).sum(axis=1).reshape(1, 4)
    d_dtb = d_gd_params[5].reshape(4, LANES).sum(axis=1).reshape(1, 4)
    scat = [shard_rows(dwt_in_full), shard_cols(d_w2p[:64]), shard_cols(d_a2p[64:]), shard_cols(d_conv),
            shard_cols(dwa), shard_cols(dwb), shard_rows(dwo)]
    small_g = _pack_small([d_gin, d_mu, d_w0, d_a0, d_kk_, d_ka_, d_rk, d_gnw, d_gnb, d_alog, d_dtb, d_onw, d_now],
                          loss_acc[0:1])
    scat.append(jnp.stack([small_g, small_g])[:, None])
    pair = _exchange("reduce_cores", ("c",), scat, [], place_own=False)
    part = [_pair_sum("pair_sum_%d" % i, own, got, BF16 if i < 7 else F32)
            for i, (own, got) in enumerate(zip(scat, pair))]
    lands = _exchange("reduce_chips", ("x", "y"), part[:7], [part[7][0]])

    small_w = [norm_in_w, rw_mu, rw_w0, rw_a0, rw_k_k, rw_k_a, rw_r_k, rw_gn_w, rw_gn_b, gd_A_log, gd_dt_bias, gd_o_norm_w, norm_out_w]
    small_m = [m_norm_in_w, m_rw_mu, m_rw_w0, m_rw_a0, m_rw_k_k, m_rw_k_a, m_rw_r_k, m_rw_gn_w, m_rw_gn_b, m_gd_A_log, m_gd_dt_bias, m_gd_o_norm_w, m_norm_out_w]
    small_v = [v_norm_in_w, v_rw_mu, v_rw_w0, v_rw_a0, v_rw_k_k, v_rw_k_a, v_rw_r_k, v_rw_gn_w, v_rw_gn_b, v_gd_A_log, v_gd_dt_bias, v_gd_o_norm_w, v_norm_out_w]
    flat = lambda arrs: [a.reshape(1, -1) for a in arrs]
    sm, loss_row = _adam_small(lands[7], flat(small_w), flat(small_m), flat(small_v))
    sm_g, sm_d, sm_m, sm_v = [{nm: res[i].reshape(w.shape) for (nm, _), res, w in zip(_SMALL, sm, small_w)}
                              for i in range(4)]

    big = {"w_in": [o.T[None] for o in _adam("adam_w_in", lands[0], wt_own, mt_own, vt_own)]}
    for nm, land, w, m, v in (("rw_w2", lands[1], rw_w2, m_rw_w2, v_rw_w2),
                              ("rw_a2", lands[2], rw_a2, m_rw_a2, v_rw_a2),
                              ("gd_conv_w", lands[3], gd_conv_w, m_gd_conv_w, v_gd_conv_w),
                              ("w_branch_a", lands[4], w_branch_a, m_w_branch_a, v_w_branch_a),
                              ("w_branch_b", lands[5], w_branch_b, m_w_branch_b, v_w_branch_b),
                              ("w_out", lands[6], w_out, m_w_out, v_w_out)):
        big[nm] = [o.reshape(w.shape) for o in _adam("adam_" + nm, land, w[0], m[0], v[0])]

    order = ["norm_in_w", "w_in", "rw_mu", "rw_w0", "rw_w2", "rw_a0", "rw_a2", "rw_k_k", "rw_k_a", "rw_r_k", "rw_gn_w",
             "rw_gn_b", "gd_conv_w", "gd_A_log", "gd_dt_bias", "gd_o_norm_w", "w_branch_a", "w_branch_b", "w_out", "norm_out_w"]
    pick = lambda nm, i: big[nm][i] if nm in big else (sm_g, sm_d, sm_m, sm_v)[i][nm]
    loss = loss_row[0, 0]
    grad_x = dx2.reshape(x.shape)
    return (loss, grad_x, *[pick(nm, 0) for nm in order], *[pick(nm, 1) for nm in order],
            *[pick(nm, 2) for nm in order], *[pick(nm, 3) for nm in order])
```

```python
import functools

import jax
import jax.numpy as jnp
from jax import lax
from jax.experimental import pallas as pl
from jax.experimental.pallas import tpu as pltpu

F32 = jnp.float32
BF16 = jnp.bfloat16
HI = lax.Precision.HIGHEST

LANES = 128
SUB = 8
CHUNK = 64
N_DEV = 8
VMEM_LIMIT = 56 * 1024 * 1024

D_MODEL = 1024
RW_W = 512
GD_W = 512
RW_SHIFT = 1664
NORM_EPS = 1e-6
RW_GN_EPS = 64 * 1e-5
ADAM_LR, ADAM_B1, ADAM_B2, ADAM_EPS, ADAM_WD, ADAM_STEP = 0.001, 0.9, 0.999, 1e-8, 0.01, 10


_NN, _NT, _TN = ((1,), (0,)), ((1,), (1,)), ((0,), (0,))


def _dot(a, b, dims, passes):
    precision = lax.Precision.HIGH if passes == 3 else lax.Precision.DEFAULT
    return lax.dot_general(a, b, (dims, ((), ())), precision=precision, preferred_element_type=F32)


def _mm(a, b, passes=3):
    return _dot(a, b, _NN, passes)


def _mm_nt(a, b, passes=3):
    return _dot(a, b, _NT, passes)


def _mm_tn(a, b, passes=3):
    return _dot(a, b, _TN, passes)


P_SUM = 3
P_SCORE = 1
P_INV = 1
P_STATE = 1
P_APPLY = 1
P_UPDATE = 1
P_POINT = 1


def _stack_rows(blocks):
    return jnp.concatenate(blocks, axis=0)


def _split_rows(x, n):
    r = x.shape[0] // n

    @jax.custom_vjp
    def split(x):
        return tuple(x[i * r:(i + 1) * r] for i in range(n))

    split.defvjp(lambda x: (split(x), None), lambda _, gs: (jnp.concatenate(gs, axis=0),))
    return split(x)


def _iota(shape, d):
    return lax.broadcasted_iota(jnp.int32, shape, d)


def _sigmoid(x):
    return 0.5 * (jnp.tanh(0.5 * x) + 1.0)


def _silu(x):
    return x * _sigmoid(x)


def _softplus(x):
    return jnp.maximum(x, 0.0) + jnp.log(1.0 + jnp.exp(-jnp.abs(x)))


def _seg_ones(seg):
    return ((_iota((LANES, LANES), 0) // seg) == (_iota((LANES, LANES), 1) // seg)).astype(F32)


def _sl(g):
    return slice(g * LANES, (g + 1) * LANES)


@jax.custom_vjp
def _tri_inverse(ms):
    return _tri_inverse_chain(ms)


def _tri_inverse_bwd(ts, dts):
    return ([-_mm_nt(_mm_tn(t, dt, P_INV), t, P_INV) for t, dt in zip(ts, dts)],)


def _tri_inverse_chain(ms):
    c = CHUNK
    ri, ci = _iota((c, c), 0), _iota((c, c), 1)
    eye = (ri == ci).astype(F32)
    d16 = (ri // 16) == (ci // 16)
    d32 = (ri // 32) == (ci // 32)
    ps = [jnp.where(d16, -m, 0.0) for m in ms]
    ts = [eye + p for p in ps]
    for _ in range(3):
        ps = [_mm(p, p, P_INV) for p in ps]
        ts = [_mm(t, eye + p, P_INV) for t, p in zip(ts, ps)]
    for off_diagonal in (d32 & (~d16), ~d32):
        tq = [_mm(t, jnp.where(off_diagonal, m, 0.0), P_INV) for t, m in zip(ts, ms)]
        ts = [t - _mm(a, t, P_INV) for t, a in zip(ts, tq)]
    return ts


_tri_inverse.defvjp(lambda ms: (lambda ts: (ts, ts))(_tri_inverse_chain(ms)), _tri_inverse_bwd)


@jax.custom_vjp
def _known_inverse(ms, ts):
    return ts


_known_inverse.defvjp(lambda ms, ts: (ts, ts),
                      lambda ts, dts: (_tri_inverse_bwd(ts, dts)[0], [jnp.zeros_like(t) for t in ts]))


def _chunk_fwd(prims, *, nsub=None, scalar_decay=None, kinds=None, inverses=None):
    c = CHUNK
    ng = len(prims)
    kinds = kinds if kinds is not None else [(nsub, scalar_decay)] * ng
    s0s, rs, lws, ks, vs, kks, bs = [list(t) for t in zip(*prims)]
    ri, ci = _iota((c, c), 0), _iota((c, c), 1)
    incl = ri >= ci
    strict = ri > ci
    tril = incl.astype(F32)
    lane = _iota((1, LANES), 1)
    heads = [n for n, _ in kinds]
    scalar = [sc for _, sc in kinds]
    masks = [[((lane // (LANES // n)) == s).astype(F32) for s in range(n)] if n > 1 else [1.0] for n in heads]
    cws = [_mm(tril, lw, P_SUM) for lw in lws]
    cwxs = [cw - lw for cw, lw in zip(cws, lws)]
    ends = [cw[c - 1:c, :] for cw in cws]
    kkds = [kk * jnp.exp(cwx) for kk, cwx in zip(kks, cwxs)]
    rds = [r * jnp.exp(cw) for r, cw in zip(rs, cws)]
    kends = [k * jnp.exp(e - cw) for k, e, cw in zip(ks, ends, cws)]
    bends = [b * jnp.exp(e - cw) for b, e, cw in zip(bs, ends, cws)]
    state_terms = [_split_rows(_mm_nt(_stack_rows([kkd, rd]), s0, P_STATE), 2) for kkd, rd, s0 in zip(kkds, rds, s0s)]
    w0s, y0s = [t[0] for t in state_terms], [t[1] for t in state_terms]
    chains = [(g, s) for g in range(ng) for s in range(heads[g])]
    rows = [_split_rows(jnp.transpose(cw), LANES // c)[0] if sc else None for cw, sc in zip(cws, scalar)]
    dxs = [jnp.where(strict, jnp.exp(jnp.minimum(cwx[:, :c] - row, 0.0)), 0.0) if sc else None
           for cwx, row, sc in zip(cwxs, rows, scalar)]
    dis = [jnp.where(incl, jnp.exp(jnp.minimum(cw[:, :c] - row, 0.0)), 0.0) if sc else None
           for cw, row, sc in zip(cws, rows, scalar)]
    lefts = [_stack_rows([a * m for m in ms] + [q * m for m in ms])
             for a, q, ms in zip([kk if sc else kkd for kk, kkd, sc in zip(kks, kkds, scalar)],
                                 [r if sc else rd for r, rd, sc in zip(rs, rds, scalar)], masks)]
    rights_b = [b if sc else b * jnp.exp(-cw) for b, cw, sc in zip(bs, cws, scalar)]
    rights_k = [k if sc else k * jnp.exp(-cw) for k, cw, sc in zip(ks, cws, scalar)]
    on_b = [_split_rows(_mm_nt(left, right, P_SCORE), 2 * n) for left, right, n in zip(lefts, rights_b, heads)]
    on_k = [_split_rows(_mm_nt(left, right, P_SCORE), 2 * n) for left, right, n in zip(lefts, rights_k, heads)]
    lower = lambda x, g: x * dxs[g] if scalar[g] else jnp.where(strict, x, 0.0)
    lower_incl = lambda x, g: x * dis[g] if scalar[g] else jnp.where(incl, x, 0.0)
    m_b = [lower(on_b[g][s], g) for g, s in chains]
    m_k = [lower(on_k[g][s], g) for g, s in chains]
    n_k = [lower_incl(on_k[g][heads[g] + s], g) for g, s in chains]
    n_b = [lower_incl(on_b[g][heads[g] + s], g) for g, s in chains]
    t_inv = _tri_inverse(m_b) if inverses is None else _known_inverse(m_b, inverses)
    on_v = [_split_rows(_mm(_stack_rows([mk, nk]), vs[g], P_APPLY), 2) for (g, s), mk, nk in zip(chains, m_k, n_k)]
    sa_c = [_mm(t, w0s[g] + mv[0], P_APPLY) for (g, s), t, mv in zip(chains, t_inv, on_v)]
    y_c = [y0s[g] + mv[1] - _mm(nb, sa, P_APPLY) for (g, s), mv, nb, sa in zip(chains, on_v, n_b, sa_c)]
    first = [sum(heads[:g]) for g in range(ng)]
    per_group = lambda xs: [functools.reduce(lambda p, q: p + q, [xs[first[g] + s] * masks[g][s] for s in range(heads[g])])
                            for g in range(ng)]
    sas, ys = per_group(sa_c), per_group(y_c)
    s_ends = [s0 * jnp.exp(e) + _mm_tn(_stack_rows([v, -sa]), _stack_rows([kend, bend]), P_UPDATE)
              for s0, e, v, kend, sa, bend in zip(s0s, ends, vs, kends, sas, bends)]
    row_head = lambda n: _iota((LANES, LANES), 0) // (LANES // n)
    col_head = lambda n: _iota((LANES, LANES), 1) // (LANES // n)
    s_ends = [jnp.where(row_head(n) == col_head(n), s_end, 0.0) if n > 1 else s_end for s_end, n in zip(s_ends, heads)]
    return list(zip(ys, s_ends)), t_inv


def _rec_fwd(name, branches, *, seq):
    n, w = branches[0][0][0].shape
    ng = w // LANES
    nc = seq // CHUNK
    nb = n // seq
    nbr = len(branches)
    per = nb * ng
    kinds = [(heads, scalar) for _, heads, scalar in branches for _ in range(per)]
    nts = [per * heads for _, heads, _ in branches]

    def body(*refs):
        in_refs = [refs[6 * i:6 * i + 6] for i in range(nbr)]
        out_refs = [refs[6 * nbr + 3 * i:6 * nbr + 3 * i + 3] for i in range(nbr)]
        states = refs[9 * nbr:]

        @pl.when(pl.program_id(0) == 0)
        def _():
            for state in states:
                state[...] = jnp.zeros_like(state)
        where = [(i, bi, g) for i in range(nbr) for bi in range(nb) for g in range(ng)]
        prims = [(states[i][bi * ng + g],) + tuple(ref[bi, :, _sl(g)] for ref in in_refs[i]) for i, bi, g in where]
        outs, t_inv = _chunk_fwd(prims, kinds=kinds)
        for (i, bi, g), prim, (y, s_end) in zip(where, prims, outs):
            y_ref, s_ref, _ = out_refs[i]
            s_ref[0, bi * ng + g] = prim[0]
            y_ref[bi, :, _sl(g)] = y
            states[i][bi * ng + g] = s_end
        pos = 0
        for i in range(nbr):
            for j in range(nts[i]):
                out_refs[i][2][0, j] = t_inv[pos + j]
            pos += nts[i]

    row = pl.BlockSpec((nb, CHUNK, w), lambda c: (0, c, 0))
    seqs = lambda a: a.reshape(nb, seq, w)
    res = pl.pallas_call(
        body, name=name, grid=(nc,),
        in_specs=[row] * (6 * nbr),
        out_specs=[spec for nt in nts for spec in (row, pl.BlockSpec((1, per, LANES, LANES), lambda c: (c, 0, 0, 0)),
                                                   pl.BlockSpec((1, nt, CHUNK, CHUNK), lambda c: (c, 0, 0, 0)))],
        out_shape=[shp for nt in nts for shp in (jax.ShapeDtypeStruct((nb, seq, w), F32),
                                                 jax.ShapeDtypeStruct((nc, per, LANES, LANES), F32),
                                                 jax.ShapeDtypeStruct((nc, nt, CHUNK, CHUNK), F32))],
        scratch_shapes=[pltpu.VMEM((per, LANES, LANES), F32)] * nbr,
        compiler_params=pltpu.CompilerParams(dimension_semantics=("arbitrary",), vmem_limit_bytes=VMEM_LIMIT),
    )(*[seqs(a) for arrs, _, _ in branches for a in arrs])
    return [(res[3 * i].reshape(n, w), (res[3 * i + 1], res[3 * i + 2])) for i in range(nbr)]


def _rec_bwd(name, branches, *, seq):
    n, w = branches[0][0][0].shape
    ng = w // LANES
    nc = seq // CHUNK
    nb = n // seq
    nbr = len(branches)
    per = nb * ng
    kinds = [(heads, scalar) for _, _, _, heads, scalar in branches for _ in range(per)]
    nts = [per * heads for _, _, _, heads, _ in branches]

    def body(*refs):
        in_refs = [refs[9 * i:9 * i + 9] for i in range(nbr)]
        out_refs = [refs[9 * nbr + 6 * i:9 * nbr + 6 * i + 6] for i in range(nbr)]
        dstates = refs[15 * nbr:]

        @pl.when(pl.program_id(0) == 0)
        def _():
            for dstate in dstates:
                dstate[...] = jnp.zeros_like(dstate)
        where = [(i, bi, g) for i in range(nbr) for bi in range(nb) for g in range(ng)]
        inverses = [in_refs[i][7][0, j] for i in range(nbr) for j in range(nts[i])]
        f = lambda p: _chunk_fwd(p, kinds=kinds, inverses=inverses)[0]
        prims = [(in_refs[i][6][0, bi * ng + g],) + tuple(ref[bi, :, _sl(g)] for ref in in_refs[i][:6])
                 for i, bi, g in where]
        _, vjp = jax.vjp(f, prims)
        (d_prims,) = vjp([(in_refs[i][8][bi, :, _sl(g)], dstates[i][bi * ng + g]) for i, bi, g in where])
        for (i, bi, g), d_prim in zip(where, d_prims):
            dstates[i][bi * ng + g] = d_prim[0]
            for ref, d in zip(out_refs[i], d_prim[1:]):
                ref[bi, :, _sl(g)] = d

    row = pl.BlockSpec((nb, CHUNK, w), lambda c: (0, nc - 1 - c, 0))
    seqs = lambda a: a.reshape(nb, seq, w)
    in_specs, args = [], []
    for (arrs, (s_save, t_save), dy, _, _), nt in zip(branches, nts):
        in_specs += [row] * 6 + [pl.BlockSpec((1, per, LANES, LANES), lambda c: (nc - 1 - c, 0, 0, 0)),
                                 pl.BlockSpec((1, nt, CHUNK, CHUNK), lambda c: (nc - 1 - c, 0, 0, 0)), row]
        args += [seqs(a) for a in arrs] + [s_save, t_save, seqs(dy)]
    grads = pl.pallas_call(
        body, name=name, grid=(nc,),
        in_specs=in_specs,
        out_specs=[row] * (6 * nbr),
        out_shape=[jax.ShapeDtypeStruct((nb, seq, w), F32)] * (6 * nbr),
        scratch_shapes=[pltpu.VMEM((per, LANES, LANES), F32)] * nbr,
        compiler_params=pltpu.CompilerParams(dimension_semantics=("arbitrary",), vmem_limit_bytes=VMEM_LIMIT),
    )(*args)
    return [[g.reshape(n, w) for g in grads[6 * i:6 * i + 6]] for i in range(nbr)]


def _shift_down(a, j, halo, is_start):
    tb = a.shape[0]
    rolled = pltpu.roll(a, j, 0)
    hr = jnp.where(is_start, 0.0, pltpu.roll(halo, j, 0))
    first = jnp.where(_iota((SUB, LANES), 0) < j, hr, rolled[0:SUB])
    if tb == SUB:
        return first
    return jnp.concatenate([first, rolled[SUB:]], axis=0)


def _shift_up(d, j, carry, is_end):
    tb = d.shape[0]
    up = pltpu.roll(d, tb - j, 0)
    cr = jnp.where(is_end, 0.0, pltpu.roll(carry, SUB - j, 0))
    last = jnp.where(_iota((SUB, LANES), 0) >= SUB - j, cr, up[tb - SUB:tb])
    if tb == SUB:
        return last
    return jnp.concatenate([up[:tb - SUB], last], axis=0)


def _ngroups(a):
    return a.shape[1] // LANES


def _pw_fwd(name, f, ins, shift, params, out_widths, out_dtypes, *, seq, tb):
    n = ins[0].shape[0]
    nt, tps = n // tb, seq // tb
    ni, npar = len(ins), len(params)

    def body(*refs):
        in_refs = refs[:ni]
        pos = ni
        halo_ref = None
        if shift:
            halo_ref = refs[pos]
            pos += 1
        p_refs = refs[pos:pos + npar]
        out_refs = refs[pos + npar:]
        is_start = (pl.program_id(0) % tps) == 0
        tiles = [[ref[:, _sl(g)] for g in range(_ngroups(ref))] for ref in in_refs]
        prevs = [[_shift_down(tiles[0][g], j, halo_ref[:, _sl(g)], is_start) for g in range(len(tiles[0]))]
                 for j in range(1, shift + 1)]
        pv = [[ref[:, _sl(g)] for g in range(_ngroups(ref))] for ref in p_refs]
        outs = f(tiles, prevs, pv)
        for o_ref, og in zip(out_refs, outs, strict=True):
            for g, t in enumerate(og):
                o_ref[:, _sl(g)] = t.astype(o_ref.dtype)

    in_specs = [pl.BlockSpec((tb, a.shape[1]), lambda i: (i, 0)) for a in ins]
    args = list(ins)
    if shift:
        in_specs.append(pl.BlockSpec((SUB, ins[0].shape[1]), lambda i: (jnp.maximum(i * (tb // SUB) - 1, 0), 0)))
        args.append(ins[0])
    in_specs += [pl.BlockSpec(p.shape, lambda i: (0, 0)) for p in params]
    args += list(params)
    return pl.pallas_call(
        body, name=name, grid=(nt,),
        in_specs=in_specs,
        out_specs=[pl.BlockSpec((tb, w), lambda i: (i, 0)) for w in out_widths],
        out_shape=[jax.ShapeDtypeStruct((n, w), dt) for w, dt in zip(out_widths, out_dtypes, strict=True)],
        compiler_params=pltpu.CompilerParams(dimension_semantics=("parallel",), vmem_limit_bytes=VMEM_LIMIT),
    )(*args)


def _proj_pw_fwd(name, f, h, wts, shift, params, out_widths, *, seq, tb):
    n = h.shape[0]
    nt, tps = n // tb, seq // tb
    nw, npar = len(wts), len(params)

    def body(*refs):
        h_ref = refs[0]
        w_refs = refs[1:1 + nw]
        par_refs = refs[1 + nw:1 + nw + npar]
        p_refs = refs[1 + nw + npar:1 + 2 * nw + npar]
        out_refs = refs[1 + 2 * nw + npar:len(refs) - 1]
        carry = refs[-1]
        is_start = (pl.program_id(0) % tps) == 0
        for w_ref, p_ref in zip(w_refs, p_refs, strict=True):
            p_ref[...] = lax.dot_general(h_ref[...], w_ref[...], (_NT, ((), ())), preferred_element_type=F32)
        tiles = [[ref[:, _sl(g)] for g in range(_ngroups(ref))] for ref in p_refs]
        prevs = [[_shift_down(tiles[0][g], j, carry[:, _sl(g)], is_start) for g in range(len(tiles[0]))]
                 for j in range(1, shift + 1)]
        carry[...] = p_refs[0][tb - SUB:tb, :]
        pv = [[ref[:, _sl(g)] for g in range(_ngroups(ref))] for ref in par_refs]
        outs = f(tiles, prevs, pv)
        for o_ref, og in zip(out_refs, outs, strict=True):
            for g, t in enumerate(og):
                o_ref[:, _sl(g)] = t

    widths = [w.shape[0] for w in wts] + list(out_widths)
    res = pl.pallas_call(
        body, name=name, grid=(nt,),
        in_specs=([pl.BlockSpec((tb, D_MODEL), lambda i: (i, 0))] + [pl.BlockSpec(w.shape, lambda i: (0, 0)) for w in wts]
                  + [pl.BlockSpec(p.shape, lambda i: (0, 0)) for p in params]),
        out_specs=[pl.BlockSpec((tb, w), lambda i: (i, 0)) for w in widths],
        out_shape=[jax.ShapeDtypeStruct((n, w), F32) for w in widths],
        scratch_shapes=[pltpu.VMEM((SUB, wts[0].shape[0]), F32)],
        compiler_params=pltpu.CompilerParams(dimension_semantics=("arbitrary",), vmem_limit_bytes=VMEM_LIMIT),
    )(h, *wts, *params)
    return res[:nw], res[nw:]


def _pw_bwd(name, f, ins, shift, params, douts, din_dtypes, *, seq, tb):
    n = ins[0].shape[0]
    nt, tps = n // tb, seq // tb
    ni, npar = len(ins), len(params)
    flat_douts = [d for ds in douts for d in ds]
    nd = len(flat_douts)
    w0 = ins[0].shape[1]

    def body(*refs):
        in_refs = refs[:ni]
        pos = ni
        halo_ref = None
        if shift:
            halo_ref = refs[pos]
            pos += 1
        p_refs = refs[pos:pos + npar]
        pos += npar
        d_refs = refs[pos:pos + nd]
        pos += nd
        din_refs = refs[pos:pos + ni]
        pos += ni
        dp_refs = refs[pos:pos + npar]
        pos += npar
        carry = refs[pos] if shift else None
        step = pl.program_id(0)
        tile = nt - 1 - step
        is_start = (tile % tps) == 0
        is_end = (tile % tps) == tps - 1
        tiles = [[ref[:, _sl(g)] for g in range(_ngroups(ref))] for ref in in_refs]
        prevs = [[_shift_down(tiles[0][g], j, halo_ref[:, _sl(g)], is_start) for g in range(len(tiles[0]))]
                 for j in range(1, shift + 1)]
        pv = [[ref[:, _sl(g)] for g in range(_ngroups(ref))] for ref in p_refs]
        cot, pos_d = [], 0
        for ds in douts:
            grp = d_refs[pos_d:pos_d + len(ds)]
            pos_d += len(ds)
            cot.append([functools.reduce(lambda p, q: p + q, [ref[:, _sl(g)].astype(F32) for ref in grp])
                        for g in range(_ngroups(grp[0]))])
        _, vjp = jax.vjp(f, tiles, prevs, pv)
        d_tiles, d_prevs, d_pv = vjp(cot)
        for g in range(len(tiles[0])):
            for j in range(1, shift + 1):
                d_tiles[0][g] = d_tiles[0][g] + _shift_up(d_prevs[j - 1][g], j, carry[j - 1, :, _sl(g)], is_end)
            for j in range(1, shift + 1):
                carry[j - 1, :, _sl(g)] = d_prevs[j - 1][g][0:SUB]
        for ref, dg in zip(din_refs, d_tiles, strict=True):
            for g, t in enumerate(dg):
                ref[:, _sl(g)] = t.astype(ref.dtype)

        @pl.when(step == 0)
        def _():
            for ref in dp_refs:
                ref[...] = jnp.zeros_like(ref)
        for ref, dg in zip(dp_refs, d_pv, strict=True):
            for g, t in enumerate(dg):
                ref[:, _sl(g)] += t

    rev = lambda i: (nt - 1 - i, 0)
    in_specs = [pl.BlockSpec((tb, a.shape[1]), rev) for a in ins]
    args = list(ins)
    if shift:
        in_specs.append(pl.BlockSpec((SUB, w0), lambda i: (jnp.maximum((nt - 1 - i) * (tb // SUB) - 1, 0), 0)))
        args.append(ins[0])
    in_specs += [pl.BlockSpec(p.shape, lambda i: (0, 0)) for p in params]
    args += list(params)
    in_specs += [pl.BlockSpec((tb, d.shape[1]), rev) for d in flat_douts]
    args += flat_douts
    out_specs = [pl.BlockSpec((tb, a.shape[1]), rev) for a in ins] + [pl.BlockSpec(p.shape, lambda i: (0, 0)) for p in params]
    out_shape = ([jax.ShapeDtypeStruct(a.shape, dt) for a, dt in zip(ins, din_dtypes, strict=True)]
                 + [jax.ShapeDtypeStruct(p.shape, F32) for p in params])
    res = pl.pallas_call(
        body, name=name, grid=(nt,),
        in_specs=in_specs, out_specs=out_specs, out_shape=out_shape,
        scratch_shapes=[pltpu.VMEM((shift, SUB, w0), F32)] if shift else [],
        compiler_params=pltpu.CompilerParams(dimension_semantics=("arbitrary",), vmem_limit_bytes=VMEM_LIMIT),
    )(*args)
    return res[:ni], res[ni:]


def _rwkv_prep_f(tiles, prevs, params):
    (p,), (prev,) = tiles, prevs
    mu, w0, w2p, a0, a2p, k_k, k_a = params
    xs = [p[g] + (prev[g] - p[g]) * mu[g] for g in range(13)]
    wdad = xs[12]
    tw = jnp.tanh(wdad)
    e64 = _seg_ones(64)
    r, lw, k2, v, kk, b = [], [], [], [], [], []
    for g in range(4):
        k_g = xs[4 + g]
        lo = w0[g] + _mm(tw, w2p[g], P_POINT)
        lw_g = -jnp.exp(-_softplus(-lo) - 0.5)
        a_g = _sigmoid(a0[g] + _mm(wdad, a2p[g], P_POINT))
        kkp = k_g * k_k[g]
        kk_g = kkp * lax.rsqrt(_mm(kkp * kkp, e64, P_POINT) + 1e-12)
        r.append(xs[g])
        lw.append(lw_g)
        k2.append(k_g * (1.0 + (a_g - 1.0) * k_a[g]))
        v.append(xs[8 + g])
        kk.append(kk_g)
        b.append(kk_g * a_g)
    return [r, lw, k2, v, kk, b]


def _rwkv_post_f(tiles, prevs, params):
    yrec, r, k2, v, z = tiles
    gn_w, gn_b, r_k = params
    e64 = _seg_ones(64)
    out = []
    for g in range(4):
        mean = _mm(yrec[g], e64, P_POINT) * (1.0 / 64)
        d = yrec[g] - mean
        var = _mm(d * d, e64, P_POINT) * (1.0 / 64)
        yn = d * lax.rsqrt(var + RW_GN_EPS) * gn_w[g] + gn_b[g]
        bonus = _mm(r[g] * k2[g] * r_k[g], e64, P_POINT) * v[g]
        out.append((yn + bonus) * _silu(z[g]))
    return [out]


def _gdn_prep_f(tiles, prevs, params):
    x, (ba,) = tiles
    p1, p2, p3 = prevs
    cw0, cw1, cw2, cw3, a_log, dt_bias = params
    s = [_silu(cw3[g] * x[g] + cw2[g] * p1[g] + cw1[g] * p2[g] + cw0[g] * p3[g]) for g in range(12)]
    row = _iota((LANES, LANES), 0)
    r, lw, k, vv, b = [], [], [], [], []
    for h in range(4):
        q_h, k_h, v_h = s[h], s[4 + h], s[8 + h]
        qn = q_h * lax.rsqrt(jnp.sum(q_h * q_h, axis=-1, keepdims=True) + 1e-12)
        kn = k_h * lax.rsqrt(jnp.sum(k_h * k_h, axis=-1, keepdims=True) + 1e-12)
        beta = _sigmoid(_mm(ba, (row == h).astype(F32)))
        alpha = _mm(ba, (row == 4 + h).astype(F32))
        g_h = -jnp.exp(a_log[h]) * _softplus(alpha + dt_bias[h])
        r.append(qn * (LANES ** -0.5))
        lw.append(g_h)
        k.append(kn)
        vv.append(beta * v_h)
        b.append(jnp.exp(g_h) * beta * kn)
    return [r, lw, k, vv, b]


def _gdn_post_f(tiles, prevs, params):
    o, z = tiles
    ((onw,),) = params
    out = []
    for h in range(4):
        ms = jnp.mean(o[h] * o[h], axis=-1, keepdims=True)
        out.append(o[h] * lax.rsqrt(ms + NORM_EPS) * onw * _silu(z[h]))
    return [out]


def _norm_in(x2, g_in, *, tm):
    n = x2.shape[0]

    def body(x_ref, g_ref, h_ref):
        x = x_ref[...]
        rs = lax.rsqrt(jnp.mean(x * x, axis=-1, keepdims=True) + NORM_EPS)
        h_ref[...] = (x * rs * g_ref[...]).astype(BF16)

    return pl.pallas_call(
        body, name="norm_in", grid=(n // tm,),
        in_specs=[pl.BlockSpec((tm, D_MODEL), lambda i: (i, 0)), pl.BlockSpec((1, D_MODEL), lambda i: (0, 0))],
        out_specs=pl.BlockSpec((tm, D_MODEL), lambda i: (i, 0)),
        out_shape=jax.ShapeDtypeStruct((n, D_MODEL), BF16),
        compiler_params=pltpu.CompilerParams(dimension_semantics=("parallel",), vmem_limit_bytes=VMEM_LIMIT),
    )(x2, g_in)


def _proj(name, h, wt, *, tm):
    n, ws = h.shape[0], wt.shape[0]

    def body(h_ref, w_ref, o_ref):
        o_ref[...] = lax.dot_general(h_ref[...], w_ref[...], (_NT, ((), ())), preferred_element_type=F32)

    return pl.pallas_call(
        body, name=name, grid=(n // tm,),
        in_specs=[pl.BlockSpec((tm, D_MODEL), lambda i: (i, 0)), pl.BlockSpec((ws, D_MODEL), lambda i: (0, 0))],
        out_specs=pl.BlockSpec((tm, ws), lambda i: (i, 0)),
        out_shape=jax.ShapeDtypeStruct((n, ws), F32),
        compiler_params=pltpu.CompilerParams(dimension_semantics=("parallel",), vmem_limit_bytes=VMEM_LIMIT),
    )(h, wt)


def _proj_dw(name, h, dp, *, tm):
    n, ws = dp.shape

    def body(h_ref, d_ref, o_ref):
        @pl.when(pl.program_id(0) == 0)
        def _():
            o_ref[...] = jnp.zeros_like(o_ref)
        o_ref[...] += lax.dot_general(d_ref[...], h_ref[...], (_TN, ((), ())), preferred_element_type=F32)

    return pl.pallas_call(
        body, name=name, grid=(n // tm,),
        in_specs=[pl.BlockSpec((tm, D_MODEL), lambda i: (i, 0)), pl.BlockSpec((tm, ws), lambda i: (i, 0))],
        out_specs=pl.BlockSpec((ws, D_MODEL), lambda i: (0, 0)),
        out_shape=jax.ShapeDtypeStruct((ws, D_MODEL), F32),
        compiler_params=pltpu.CompilerParams(dimension_semantics=("arbitrary",), vmem_limit_bytes=VMEM_LIMIT),
    )(h, dp)


def _proj_dx(x2, g_in, d_xo, dps, ws, *, tm):
    n = x2.shape[0]
    ns = len(dps)

    def body(*refs):
        x_ref, g_ref, dxo_ref = refs[:3]
        dp_refs = refs[3:3 + ns]
        w_refs = refs[3 + ns:3 + 2 * ns]
        dx_ref, dg_ref = refs[3 + 2 * ns:]
        dh = jnp.zeros((tm, D_MODEL), F32)
        for d_ref, w_ref in zip(dp_refs, w_refs, strict=True):
            dh = dh + jnp.dot(d_ref[...], w_ref[...], preferred_element_type=F32)
        x = x_ref[...]
        rs = lax.rsqrt(jnp.mean(x * x, axis=-1, keepdims=True) + NORM_EPS)
        xn = x * rs
        dxn = dh * g_ref[...]
        dx_ref[...] = dxo_ref[...] + rs * (dxn - xn * jnp.mean(dxn * xn, axis=-1, keepdims=True))

        @pl.when(pl.program_id(0) == 0)
        def _():
            dg_ref[...] = jnp.zeros_like(dg_ref)
        dg_ref[...] += jnp.sum(dh * xn, axis=0, keepdims=True)

    row = pl.BlockSpec((tm, D_MODEL), lambda i: (i, 0))
    return pl.pallas_call(
        body, name="proj_dx", grid=(n // tm,),
        in_specs=([row, pl.BlockSpec((1, D_MODEL), lambda i: (0, 0)), row]
                  + [pl.BlockSpec((tm, d.shape[1]), lambda i: (i, 0)) for d in dps]
                  + [pl.BlockSpec(w.shape, lambda i: (0, 0)) for w in ws]),
        out_specs=[row, pl.BlockSpec((1, D_MODEL), lambda i: (0, 0))],
        out_shape=[jax.ShapeDtypeStruct((n, D_MODEL), F32), jax.ShapeDtypeStruct((1, D_MODEL), F32)],
        compiler_params=pltpu.CompilerParams(dimension_semantics=("arbitrary",), vmem_limit_bytes=VMEM_LIMIT),
    )(x2, g_in, d_xo, *dps, *ws)


def _tail(x2, tgt2, gates, rw_post_ins, gd_post_ins, rw_post_params, gd_post_params, din_dtypes, w_a, w_b, w_o, now, *, tr):
    n = x2.shape[0]
    n_rw, n_gd = len(rw_post_ins), len(gd_post_ins)
    n_rwp, n_gdp = len(rw_post_params), len(gd_post_params)

    def body(*refs):
        x_ref, t_ref, g_ref = refs[:3]
        pos = 3
        rw_refs, gd_refs = refs[pos:pos + n_rw], refs[pos + n_rw:pos + n_rw + n_gd]
        pos += n_rw + n_gd
        rwp_refs, gdp_refs = refs[pos:pos + n_rwp], refs[pos + n_rwp:pos + n_rwp + n_gdp]
        pos += n_rwp + n_gdp
        wa_ref, wb_ref, wo_ref, now_ref = refs[pos:pos + 4]
        pos += 4
        d_rw_refs, d_gd_refs = refs[pos:pos + n_rw], refs[pos + n_rw:pos + n_rw + n_gd]
        pos += n_rw + n_gd
        dg_ref, dxo_ref, dwa_ref, dwb_ref, dwo_ref, dnow_ref, loss_ref = refs[pos:pos + 7]
        d_rwp_refs, d_gdp_refs = refs[pos + 7:pos + 7 + n_rwp], refs[pos + 7 + n_rwp:]
        groups = lambda rs: [[ref[:, _sl(g)] for g in range(_ngroups(ref))] for ref in rs]
        (ya_groups,), rw_vjp = jax.vjp(lambda t, p: _rwkv_post_f(t, [], p), groups(rw_refs), groups(rwp_refs))
        (yb_groups,), gd_vjp = jax.vjp(lambda t, p: _gdn_post_f(t, [], p), groups(gd_refs), groups(gdp_refs))
        ya16 = jnp.concatenate(ya_groups, axis=1).astype(BF16)
        yb16 = jnp.concatenate(yb_groups, axis=1).astype(BF16)
        ua = jnp.dot(ya16, wa_ref[...], preferred_element_type=F32)
        ub = jnp.dot(yb16, wb_ref[...], preferred_element_type=F32)
        ga = _sigmoid(g_ref[:, :D_MODEL])
        gb = _sigmoid(g_ref[:, D_MODEL:])
        m16 = (ga * ua + gb * ub).astype(BF16)
        xo = x_ref[...] + jnp.dot(m16, wo_ref[...], preferred_element_type=F32)
        rs = lax.rsqrt(jnp.mean(xo * xo, axis=-1, keepdims=True) + NORM_EPS)
        yn = xo * rs
        now_v = now_ref[...]
        err = yn * now_v - t_ref[...]
        dy = err * (1.0 / D_MODEL)
        dyn = dy * now_v
        dxo = rs * (dyn - yn * jnp.mean(dyn * yn, axis=-1, keepdims=True))
        dxo_ref[...] = dxo
        dxo16 = dxo.astype(BF16)
        dm = lax.dot_general(dxo16, wo_ref[...], (((1,), (1,)), ((), ())), preferred_element_type=F32)
        dua16 = (dm * ga).astype(BF16)
        dub16 = (dm * gb).astype(BF16)
        dg_ref[:, :D_MODEL] = (dm * ua * ga * (1.0 - ga)).astype(dg_ref.dtype)
        dg_ref[:, D_MODEL:] = (dm * ub * gb * (1.0 - gb)).astype(dg_ref.dtype)
        dya = lax.dot_general(dua16, wa_ref[...], (((1,), (1,)), ((), ())), preferred_element_type=F32)
        dyb = lax.dot_general(dub16, wb_ref[...], (((1,), (1,)), ((), ())), preferred_element_type=F32)
        d_rw_tiles, d_rw_pv = rw_vjp([[dya[:, _sl(g)] for g in range(RW_W // LANES)]])
        d_gd_tiles, d_gd_pv = gd_vjp([[dyb[:, _sl(g)] for g in range(GD_W // LANES)]])
        for ref, dgroups in zip(d_rw_refs + d_gd_refs, d_rw_tiles + d_gd_tiles, strict=True):
            for g, t in enumerate(dgroups):
                ref[:, _sl(g)] = t.astype(ref.dtype)

        @pl.when(pl.program_id(0) == 0)
        def _():
            for ref in (dwa_ref, dwb_ref, dwo_ref, dnow_ref, loss_ref) + d_rwp_refs + d_gdp_refs:
                ref[...] = jnp.zeros_like(ref)
        for ref, dgroups in zip(d_rwp_refs + d_gdp_refs, d_rw_pv + d_gd_pv, strict=True):
            for g, t in enumerate(dgroups):
                ref[:, _sl(g)] += t
        tn = (((0,), (0,)), ((), ()))
        dwo_ref[...] += lax.dot_general(m16, dxo16, tn, preferred_element_type=F32)
        dwa_ref[...] += lax.dot_general(ya16, dua16, tn, preferred_element_type=F32)
        dwb_ref[...] += lax.dot_general(yb16, dub16, tn, preferred_element_type=F32)
        dnow_ref[...] += jnp.sum(dy * yn, axis=0, keepdims=True)
        loss_ref[...] += (0.5 / D_MODEL) * jnp.sum(err * err)

    row = lambda w: pl.BlockSpec((tr, w), lambda i: (i, 0))
    full = lambda a: pl.BlockSpec(a.shape, lambda i: (0, 0))
    res = pl.pallas_call(
        body, name="tail", grid=(n // tr,),
        in_specs=([row(D_MODEL), row(D_MODEL), row(2 * D_MODEL)] + [row(a.shape[1]) for a in rw_post_ins + gd_post_ins]
                  + [full(p) for p in rw_post_params + gd_post_params] + [full(w_a), full(w_b), full(w_o), full(now)]),
        out_specs=([row(a.shape[1]) for a in rw_post_ins + gd_post_ins] + [row(2 * D_MODEL), row(D_MODEL),
                   pl.BlockSpec((RW_W, D_MODEL), lambda i: (0, 0)), pl.BlockSpec((GD_W, D_MODEL), lambda i: (0, 0)),
                   pl.BlockSpec((D_MODEL, D_MODEL), lambda i: (0, 0)), pl.BlockSpec((1, D_MODEL), lambda i: (0, 0)),
                   pl.BlockSpec((SUB, LANES), lambda i: (0, 0))] + [full(p) for p in rw_post_params + gd_post_params]),
        out_shape=([jax.ShapeDtypeStruct(a.shape, dt) for a, dt in zip(rw_post_ins + gd_post_ins, din_dtypes, strict=True)]
                   + [jax.ShapeDtypeStruct((n, 2 * D_MODEL), BF16), jax.ShapeDtypeStruct((n, D_MODEL), F32),
                      jax.ShapeDtypeStruct((RW_W, D_MODEL), F32), jax.ShapeDtypeStruct((GD_W, D_MODEL), F32),
                      jax.ShapeDtypeStruct((D_MODEL, D_MODEL), F32), jax.ShapeDtypeStruct((1, D_MODEL), F32),
                      jax.ShapeDtypeStruct((SUB, LANES), F32)]
                   + [jax.ShapeDtypeStruct(p.shape, F32) for p in rw_post_params + gd_post_params]),
        compiler_params=pltpu.CompilerParams(dimension_semantics=("arbitrary",), vmem_limit_bytes=VMEM_LIMIT),
    )(x2, tgt2, gates, *rw_post_ins, *gd_post_ins, *rw_post_params, *gd_post_params, w_a, w_b, w_o, now)
    ni, npar = n_rw + n_gd, n_rwp + n_gdp
    return res[:ni], res[ni:ni + 7], res[ni + 7:ni + 7 + npar]


def _exchange(name, axes, scatter, gather, place_own=True):
    ns, ng = len(scatter), len(gather)
    na = ns + ng
    gs = 2 ** len(axes)
    arrs = list(scatter) + list(gather)

    def body(*refs):
        src = refs[:na]
        dst = refs[na:2 * na]
        send_sems, recv_sems = refs[2 * na:]
        mine = {ax: lax.axis_index(ax) for ax in ("x", "y", "c")}

        def peer(k):
            co = dict(mine)
            for i, ax in enumerate(axes):
                if (k >> (len(axes) - 1 - i)) & 1:
                    co[ax] = 1 - co[ax]
            idx = 0
            for ax in axes:
                idx = 2 * idx + co[ax]
            return (co["x"], co["y"], co["c"]), idx

        _, me = peer(0)

        def copy(a, k, landing):
            dev, idx = peer(k)
            s = src[a].at[idx] if a < ns else src[a]
            return pltpu.make_async_remote_copy(src_ref=s, dst_ref=dst[a].at[idx if landing else me],
                                                send_sem=send_sems.at[a, k - 1], recv_sem=recv_sems.at[a, k - 1],
                                                device_id=dev, device_id_type=pl.DeviceIdType.MESH)

        sends = [copy(a, k, False) for a in range(na) for k in range(1, gs)]
        for cp in sends:
            cp.start()
        for a in range(na):
            for k in range(1, gs):
                copy(a, k, True).wait_recv()
        for cp in sends:
            cp.wait_send()

    out_shape = [jax.ShapeDtypeStruct(a.shape, a.dtype) for a in scatter] + \
                [jax.ShapeDtypeStruct((gs,) + a.shape, a.dtype) for a in gather]
    anyspec = pl.BlockSpec(memory_space=pl.ANY)
    lands = pl.pallas_call(
        body, name=name,
        in_specs=[anyspec] * na, out_specs=[anyspec] * na, out_shape=out_shape,
        scratch_shapes=[pltpu.SemaphoreType.DMA((na, gs - 1)), pltpu.SemaphoreType.DMA((na, gs - 1))],
    )(*arrs)
    if not place_own:
        return lands
    me = 0
    for ax in axes:
        me = 2 * me + lax.axis_index(ax)
    kept = [lax.dynamic_index_in_dim(a, me, 0, keepdims=False) for a in scatter] + list(gather)
    return [lax.dynamic_update_index_in_dim(land, mine, me, 0) for land, mine in zip(lands, kept)]


def _gather_all(name, arrs):
    na = len(arrs)

    def body(*refs):
        src = refs[:na]
        dst = refs[na:2 * na]
        send_sems, recv_sems = refs[2 * na:]
        x, y, c = lax.axis_index("x"), lax.axis_index("y"), lax.axis_index("c")
        sibling = (x, y, 1 - c)
        chips = [(1 - x, y), (x, 1 - y), (1 - x, 1 - y)]

        def copy(a, k, block, to, own=False):
            px, py, pc = block
            slot = dst[a].at[pc, 2 * px + py]
            return pltpu.make_async_remote_copy(src_ref=src[a] if own else slot, dst_ref=slot,
                                                send_sem=send_sems.at[a, k], recv_sem=recv_sems.at[a, k],
                                                device_id=to, device_id_type=pl.DeviceIdType.MESH)

        first = [copy(a, 0, (x, y, c), sibling, own=True) for a in range(na)]
        first += [copy(a, 1 + j, (x, y, c), (*chip, c), own=True) for j, chip in enumerate(chips) for a in range(na)]
        for cp in first:
            cp.start()
        passed = []
        for j, chip in enumerate(chips):
            for a in range(na):
                copy(a, 1 + j, (*chip, c), (x, y, c)).wait_recv()
                passed.append(copy(a, 4 + j, (*chip, c), sibling))
                passed[-1].start()
        for a in range(na):
            copy(a, 0, (x, y, 1 - c), (x, y, c)).wait_recv()
            for j, chip in enumerate(chips):
                copy(a, 4 + j, (*chip, 1 - c), (x, y, c)).wait_recv()
        for cp in first + passed:
            cp.wait_send()

    anyspec = pl.BlockSpec(memory_space=pl.ANY)
    lands = pl.pallas_call(
        body, name=name,
        in_specs=[anyspec] * na, out_specs=[anyspec] * na,
        out_shape=[jax.ShapeDtypeStruct((2, 4) + a.shape, a.dtype) for a in arrs],
        scratch_shapes=[pltpu.SemaphoreType.DMA((na, 7)), pltpu.SemaphoreType.DMA((na, 7))],
    )(*arrs)
    core, chip = lax.axis_index("c"), 2 * lax.axis_index("x") + lax.axis_index("y")
    zero = jnp.zeros((), jnp.int32)
    return [lax.dynamic_update_slice(land, mine[None, None], (core, chip) + (zero,) * mine.ndim)
            for land, mine in zip(lands, arrs)]


def _pair_sum(name, own, land, out_dtype):
    _, nq, r, c = own.shape
    core = lax.axis_index("c").astype(jnp.int32).reshape(1)

    def body(core_ref, own_ref, land_ref, o_ref):
        o_ref[0] = (own_ref[0, 0] + land_ref[0, 0]).astype(o_ref.dtype)

    return pl.pallas_call(
        body, name=name,
        grid_spec=pltpu.PrefetchScalarGridSpec(
            num_scalar_prefetch=1, grid=(nq,),
            in_specs=[pl.BlockSpec((1, 1, r, c), lambda i, core_ref: (core_ref[0], i, 0, 0)),
                      pl.BlockSpec((1, 1, r, c), lambda i, core_ref: (1 - core_ref[0], i, 0, 0))],
            out_specs=pl.BlockSpec((1, r, c), lambda i, core_ref: (i, 0, 0))),
        out_shape=jax.ShapeDtypeStruct((nq, r, c), out_dtype),
        compiler_params=pltpu.CompilerParams(dimension_semantics=("parallel",), vmem_limit_bytes=VMEM_LIMIT),
    )(core, own, land)


def _adam(name, land, w, m, v):
    r, c = w.shape
    nslot = land.shape[0]
    tr = 256 if (r % 256 == 0 and r > 256) else r
    tc = 256 if (tr == r and r > 256 and c % 256 == 0) else c

    def body(l_ref, w_ref, m_ref, v_ref, g_out, d_out, m_out, v_out):
        g = l_ref[0].astype(F32)
        for s in range(1, nslot):
            g = g + l_ref[s].astype(F32)
        g_out[...] = g
        d_out[...], m_out[...], v_out[...] = _adam_math(g, w_ref[...], m_ref[...], v_ref[...])

    blk = pl.BlockSpec((tr, tc), lambda i: (i * tc // c, i % (c // tc)))
    return pl.pallas_call(
        body, name=name, grid=((r // tr) * (c // tc),),
        in_specs=[pl.BlockSpec((nslot, tr, tc), lambda i: (0, i * tc // c, i % (c // tc))), blk, blk, blk],
        out_specs=[blk] * 4,
        out_shape=[jax.ShapeDtypeStruct((r, c), F32)] * 4,
        compiler_params=pltpu.CompilerParams(dimension_semantics=("parallel",), vmem_limit_bytes=VMEM_LIMIT),
    )(land, w, m, v)


def _adam_math(g, w, m, v):
    c1 = 1.0 / (1.0 - ADAM_B1 ** ADAM_STEP)
    c2 = 1.0 / (1.0 - ADAM_B2 ** ADAM_STEP)
    m_new = ADAM_B1 * m + (1.0 - ADAM_B1) * g
    v_new = ADAM_B2 * v + (1.0 - ADAM_B2) * (g * g)
    return -ADAM_LR * ((m_new * c1) / (jnp.sqrt(v_new * c2) + ADAM_EPS) + ADAM_WD * w), m_new, v_new


def _adam_small(land, ws, ms, vs):
    npar = len(ws)
    nslot = land.shape[0]

    def body(*refs):
        l_ref = refs[0]
        w_refs, m_refs, v_refs = refs[1:1 + npar], refs[1 + npar:1 + 2 * npar], refs[1 + 2 * npar:1 + 3 * npar]
        outs = refs[1 + 3 * npar:1 + 7 * npar]
        loss_ref, g_rows = refs[1 + 7 * npar], refs[2 + 7 * npar]
        g = l_ref[0]
        for s in range(1, nslot):
            g = g + l_ref[s]
        g_rows[...] = g
        row = 0
        for i, (_, size) in enumerate(_SMALL):
            for j in range(-(-size // LANES)):
                width = min(LANES, size - j * LANES)
                cols = slice(j * LANES, j * LANES + width)
                g_ij = g_rows[row:row + 1, 0:width]
                delta, m_new, v_new = _adam_math(g_ij, w_refs[i][:, cols], m_refs[i][:, cols], v_refs[i][:, cols])
                for ref, val in zip(outs[4 * i:4 * i + 4], (g_ij, delta, m_new, v_new)):
                    ref[:, cols] = val
                row += 1
        loss_ref[...] = g_rows[row:row + 1, :]

    full = lambda a: pl.BlockSpec(a.shape, lambda: (0,) * a.ndim)
    res = pl.pallas_call(
        body, name="adam_small",
        in_specs=[full(land)] + [full(a) for a in list(ws) + list(ms) + list(vs)],
        out_specs=[full(w) for w in ws for _ in range(4)] + [pl.BlockSpec((1, LANES), lambda: (0, 0))],
        out_shape=[jax.ShapeDtypeStruct(w.shape, F32) for w in ws for _ in range(4)] + [jax.ShapeDtypeStruct((1, LANES), F32)],
        scratch_shapes=[pltpu.VMEM(land.shape[1:], F32)],
    )(land, *ws, *ms, *vs)
    return [res[4 * i:4 * i + 4] for i in range(npar)], res[4 * npar]


_SMALL = (("norm_in_w", 1024), ("rw_mu", 1664), ("rw_w0", 512), ("rw_a0", 512), ("rw_k_k", 512), ("rw_k_a", 512),
          ("rw_r_k", 512), ("rw_gn_w", 512), ("rw_gn_b", 512), ("gd_A_log", 4), ("gd_dt_bias", 4), ("gd_o_norm_w", 128),
          ("norm_out_w", 1024))
_SMALL_ROWS = 64


def _pack_small(vals, loss_row):
    rows = []
    for (_, size), a in zip(_SMALL, vals, strict=True):
        flat = a.reshape(-1).astype(F32)
        pad = (-size) % LANES
        if pad:
            flat = jnp.concatenate([flat, jnp.zeros((pad,), F32)])
        rows.append(flat.reshape(-1, LANES))
    rows.append(loss_row)
    used = sum(r.shape[0] for r in rows)
    rows.append(jnp.zeros((_SMALL_ROWS - used, LANES), F32))
    return jnp.concatenate(rows, axis=0)


def kernel(x, norm_in_w, w_in, rw_mu, rw_w0, rw_w2, rw_a0, rw_a2, rw_k_k, rw_k_a, rw_r_k, rw_gn_w, rw_gn_b, gd_conv_w, gd_A_log, gd_dt_bias, gd_o_norm_w, w_branch_a, w_branch_b, w_out, norm_out_w, loss_target, m_norm_in_w, m_w_in, m_rw_mu, m_rw_w0, m_rw_w2, m_rw_a0, m_rw_a2, m_rw_k_k, m_rw_k_a, m_rw_r_k, m_rw_gn_w, m_rw_gn_b, m_gd_conv_w, m_gd_A_log, m_gd_dt_bias, m_gd_o_norm_w, m_w_branch_a, m_w_branch_b, m_w_out, m_norm_out_w, v_norm_in_w, v_w_in, v_rw_mu, v_rw_w0, v_rw_w2, v_rw_a0, v_rw_a2, v_rw_k_k, v_rw_k_a, v_rw_r_k, v_rw_gn_w, v_rw_gn_b, v_gd_conv_w, v_gd_A_log, v_gd_dt_bias, v_gd_o_norm_w, v_w_branch_a, v_w_branch_b, v_w_out, v_norm_out_w):
    nb, seq, _ = x.shape
    n = nb * seq
    tm = min(1024, n)
    tb = min(512, seq)
    x2 = x.reshape(n, D_MODEL)
    tgt2 = loss_target.reshape(n, D_MODEL)
    cols = w_in.shape[2]
    in_cols = cols * N_DEV

    wt_own, mt_own, vt_own = w_in[0].T, m_w_in[0].T, v_w_in[0].T
    sharded = [wt_own.astype(BF16), rw_w2[0], rw_a2[0], gd_conv_w[0], w_branch_a[0].astype(BF16),
               w_branch_b[0].astype(BF16), w_out[0].astype(BF16)]
    g_win, g_w2, g_a2, g_conv, g_wa, g_wb, g_wo = _gather_all("gather_weights", sharded)
    unshard_rows = lambda a: jnp.transpose(a, (1, 0, 2, 3)).reshape(N_DEV * a.shape[2], a.shape[3])
    unshard_cols = lambda a: jnp.transpose(a, (2, 1, 0, 3)).reshape(a.shape[2], N_DEV * a.shape[3])
    wt_full = unshard_rows(g_win)
    seg_bounds = ((0, 1664), (1664, 2176), (2176, 3712), (3712, 4224), (4232, in_cols))
    w_rw, w_zrw, w_qkv, w_zgd, w_gates = [wt_full[a:b] for a, b in seg_bounds]
    w_ba = jnp.concatenate([wt_full[4224:4232], jnp.zeros((LANES - 8, D_MODEL), BF16)], axis=0)
    w2_full, a2_full = unshard_cols(g_w2), unshard_cols(g_a2)
    zeros64 = jnp.zeros((64, RW_W), F32)
    w2p = jnp.concatenate([w2_full, zeros64], axis=0)
    a2p = jnp.concatenate([zeros64, a2_full], axis=0)
    conv_full = unshard_cols(g_conv)
    conv_rows = [conv_full[i:i + 1] for i in range(4)]
    wa_full = unshard_cols(g_wa)
    wb_full = unshard_cols(g_wb)
    wo_full = unshard_rows(g_wo)
    a_log_bc = jnp.repeat(gd_A_log, LANES, axis=1)
    dt_bias_bc = jnp.repeat(gd_dt_bias, LANES, axis=1)
    r_k_flat = rw_r_k.reshape(1, RW_W)
    now2 = norm_out_w.reshape(1, D_MODEL)

    h = _norm_in(x2, norm_in_w, tm=tm)
    p_zrw = _proj("proj_zrw", h, w_zrw, tm=tm)
    p_zgd = _proj("proj_zgd", h, w_zgd, tm=tm)
    p_gates = _proj("proj_gates", h, w_gates, tm=tm)
    rw_params = [rw_mu, rw_w0, w2p, rw_a0, a2p, rw_k_k, rw_k_a]
    (p_rw,), (r_a, lw_a, k_a, v_a, kk_a, b_a) = _proj_pw_fwd("proj_rwkv_prep", _rwkv_prep_f, h, [w_rw], 1, rw_params,
                                                             [RW_W] * 6, seq=seq, tb=tb)
    gd_params = conv_rows + [a_log_bc, dt_bias_bc]
    (p_qkv, p_ba), (r_b, lw_b, k_b, v_b, b_b) = _proj_pw_fwd("proj_gdn_prep", _gdn_prep_f, h, [w_qkv, w_ba], 3, gd_params,
                                                             [GD_W] * 5, seq=seq, tb=tb)
    rw_six, gd_six = (r_a, lw_a, k_a, v_a, kk_a, b_a), (r_b, lw_b, k_b, v_b, k_b, b_b)
    (y_rec, s_a), (o_rec, s_b) = _rec_fwd("rec", [(rw_six, 2, False), (gd_six, 1, True)], seq=seq)
    post_params = [rw_gn_w, rw_gn_b, r_k_flat]
    ((d_yrec, dr_p, dk_p, dv_p, d_zrw, d_o, d_zgd), (d_gates, d_xo, dwa, dwb, dwo, d_now, loss_acc),
     (*d_post_params, d_onw)) = _tail(
        x2, tgt2, p_gates, [y_rec, r_a, k_a, v_a, p_zrw], [o_rec, p_zgd], post_params, [gd_o_norm_w],
        [F32, F32, F32, F32, BF16, F32, BF16], wa_full, wb_full, wo_full, now2, tr=min(256, n))

    (dr_a, dlw_a, dk_a, dv_a, dkk_a, db_a), (dr_b, dlw_b, dk_b, dv_b, dkk_b, db_b) = _rec_bwd(
        "rec_bwd", [(rw_six, s_a, d_yrec, 2, False), (gd_six, s_b, d_o, 1, True)], seq=seq)
    (d_qkv, d_ba), d_gd_params = _pw_bwd("gdn_prep_bwd", _gdn_prep_f, [p_qkv, p_ba], 3, gd_params,
                                         [[dr_b], [dlw_b], [dk_b, dkk_b], [dv_b], [db_b]], [BF16, BF16], seq=seq, tb=tb)
    (d_prw,), d_rw_params = _pw_bwd("rwkv_prep_bwd", _rwkv_prep_f, [p_rw], 1, rw_params,
                                    [[dr_a, dr_p], [dlw_a], [dk_a, dk_p], [dv_a, dv_p], [dkk_a], [db_a]], [BF16],
                                    seq=seq, tb=tb)

    dps = [d_prw, d_zrw, d_qkv, d_zgd, d_ba, d_gates]
    wsegs = [w_rw, w_zrw, w_qkv, w_zgd, w_ba, w_gates]
    dx2, d_gin = _proj_dx(x2, norm_in_w, d_xo, dps, wsegs, tm=min(256, n))
    tm_dw = min(2 * tm, n)
    dw_rw = _proj_dw("dw_rw", h, d_prw, tm=tm_dw)
    dw_zrw = _proj_dw("dw_zrw", h, d_zrw, tm=tm_dw)
    dw_qkv = _proj_dw("dw_qkv", h, d_qkv, tm=tm_dw)
    dw_zgd = _proj_dw("dw_zgd", h, d_zgd, tm=tm_dw)
    dw_ba = _proj_dw("dw_ba", h, d_ba, tm=tm_dw)
    dw_gates = _proj_dw("dw_gates", h, d_gates, tm=tm_dw)
    dwt_in_full = jnp.concatenate([dw_rw, dw_zrw, dw_qkv, dw_zgd, dw_ba[:8], dw_gates], axis=0)

    shard_cols = lambda a: jnp.transpose(a.reshape(a.shape[0], 4, 2, a.shape[1] // N_DEV), (2, 1, 0, 3))
    shard_rows = lambda a: jnp.transpose(a.reshape(4, 2, a.shape[0] // N_DEV, a.shape[1]), (1, 0, 2, 3))
    d_mu, d_w0, d_w2p, d_a0, d_a2p, d_kk_, d_ka_ = d_rw_params
    d_gnw, d_gnb, d_rk = d_post_params
    d_conv = jnp.concatenate(d_gd_params[:4], axis=0)
    d_alog = d_gd_params[4].reshape(4, LANES).sum(axis=1).reshape(1, 4)
    d_dtb = d_gd_params[5].reshape(4, LANES).sum(axis=1).reshape(1, 4)
    scat = [shard_rows(dwt_in_full), shard_cols(d_w2p[:64]), shard_cols(d_a2p[64:]), shard_cols(d_conv),
            shard_cols(dwa), shard_cols(dwb), shard_rows(dwo)]
    small_g = _pack_small([d_gin, d_mu, d_w0, d_a0, d_kk_, d_ka_, d_rk, d_gnw, d_gnb, d_alog, d_dtb, d_onw, d_now],
                          loss_acc[0:1])
    scat.append(jnp.stack([small_g, small_g])[:, None])
    pair = _exchange("reduce_cores", ("c",), scat, [], place_own=False)
    part = [_pair_sum("pair_sum_%d" % i, own, got, BF16 if i < 7 else F32)
            for i, (own, got) in enumerate(zip(scat, pair))]
    lands = _exchange("reduce_chips", ("x", "y"), part[:7], [part[7][0]])

    small_w = [norm_in_w, rw_mu, rw_w0, rw_a0, rw_k_k, rw_k_a, rw_r_k, rw_gn_w, rw_gn_b, gd_A_log, gd_dt_bias, gd_o_norm_w, norm_out_w]
    small_m = [m_norm_in_w, m_rw_mu, m_rw_w0, m_rw_a0, m_rw_k_k, m_rw_k_a, m_rw_r_k, m_rw_gn_w, m_rw_gn_b, m_gd_A_log, m_gd_dt_bias, m_gd_o_norm_w, m_norm_out_w]
    small_v = [v_norm_in_w, v_rw_mu, v_rw_w0, v_rw_a0, v_rw_k_k, v_rw_k_a, v_rw_r_k, v_rw_gn_w, v_rw_gn_b, v_gd_A_log, v_gd_dt_bias, v_gd_o_norm_w, v_norm_out_w]
    flat = lambda arrs: [a.reshape(1, -1) for a in arrs]
    sm, loss_row = _adam_small(lands[7], flat(small_w), flat(small_m), flat(small_v))
    sm_g, sm_d, sm_m, sm_v = [{nm: res[i].reshape(w.shape) for (nm, _), res, w in zip(_SMALL, sm, small_w)}
                              for i in range(4)]

    big = {"w_in": [o.T[None] for o in _adam("adam_w_in", lands[0], wt_own, mt_own, vt_own)]}
    for nm, land, w, m, v in (("rw_w2", lands[1], rw_w2, m_rw_w2, v_rw_w2),
                              ("rw_a2", lands[2], rw_a2, m_rw_a2, v_rw_a2),
                              ("gd_conv_w", lands[3], gd_conv_w, m_gd_conv_w, v_gd_conv_w),
                              ("w_branch_a", lands[4], w_branch_a, m_w_branch_a, v_w_branch_a),
                              ("w_branch_b", lands[5], w_branch_b, m_w_branch_b, v_w_branch_b),
                              ("w_out", lands[6], w_out, m_w_out, v_w_out)):
        big[nm] = [o.reshape(w.shape) for o in _adam("adam_" + nm, land, w[0], m[0], v[0])]

    order = ["norm_in_w", "w_in", "rw_mu", "rw_w0", "rw_w2", "rw_a0", "rw_a2", "rw_k_k", "rw_k_a", "rw_r_k", "rw_gn_w",
             "rw_gn_b", "gd_conv_w", "gd_A_log", "gd_dt_bias", "gd_o_norm_w", "w_branch_a", "w_branch_b", "w_out", "norm_out_w"]
    pick = lambda nm, i: big[nm][i] if nm in big else (sm_g, sm_d, sm_m, sm_v)[i][nm]
    loss = loss_row[0, 0]
    grad_x = dx2.reshape(x.shape)
    return (loss, grad_x, *[pick(nm, 0) for nm in order], *[pick(nm, 1) for nm in order],
            *[pick(nm, 2) for nm in order], *[pick(nm, 3) for nm in order])
```

```python
import functools

import jax
import jax.numpy as jnp
from jax import lax
from jax.experimental import pallas as pl
from jax.experimental.pallas import tpu as pltpu

F32 = jnp.float32
BF16 = jnp.bfloat16
HI = lax.Precision.HIGHEST

LANES = 128
SUB = 8
CHUNK = 64
N_DEV = 8
VMEM_LIMIT = 56 * 1024 * 1024

D_MODEL = 1024
RW_W = 512
GD_W = 512
RW_SHIFT = 1664
NORM_EPS = 1e-6
RW_GN_EPS = 64 * 1e-5
ADAM_LR, ADAM_B1, ADAM_B2, ADAM_EPS, ADAM_WD, ADAM_STEP = 0.001, 0.9, 0.999, 1e-8, 0.01, 10


_NN, _NT, _TN = ((1,), (0,)), ((1,), (1,)), ((0,), (0,))


def _dot(a, b, dims, passes):
    precision = lax.Precision.HIGH if passes == 3 else lax.Precision.DEFAULT
    return lax.dot_general(a, b, (dims, ((), ())), precision=precision, preferred_element_type=F32)


def _mm(a, b, passes=3):
    return _dot(a, b, _NN, passes)


def _mm_nt(a, b, passes=3):
    return _dot(a, b, _NT, passes)


def _mm_tn(a, b, passes=3):
    return _dot(a, b, _TN, passes)


P_SUM = 3
P_SCORE = 1
P_INV = 1
P_STATE = 1
P_APPLY = 1
P_UPDATE = 1
P_POINT = 1


def _stack_rows(blocks):
    return jnp.concatenate(blocks, axis=0)


def _split_rows(x, n):
    r = x.shape[0] // n

    @jax.custom_vjp
    def split(x):
        return tuple(x[i * r:(i + 1) * r] for i in range(n))

    split.defvjp(lambda x: (split(x), None), lambda _, gs: (jnp.concatenate(gs, axis=0),))
    return split(x)


def _iota(shape, d):
    return lax.broadcasted_iota(jnp.int32, shape, d)


def _sigmoid(x):
    return 0.5 * (jnp.tanh(0.5 * x) + 1.0)


def _silu(x):
    return x * _sigmoid(x)


def _softplus(x):
    return jnp.maximum(x, 0.0) + jnp.log(1.0 + jnp.exp(-jnp.abs(x)))


def _seg_ones(seg):
    return ((_iota((LANES, LANES), 0) // seg) == (_iota((LANES, LANES), 1) // seg)).astype(F32)


def _sl(g):
    return slice(g * LANES, (g + 1) * LANES)


@jax.custom_vjp
def _tri_inverse(ms):
    return _tri_inverse_chain(ms)


def _tri_inverse_bwd(ts, dts):
    return ([-_mm_nt(_mm_tn(t, dt, P_INV), t, P_INV) for t, dt in zip(ts, dts)],)


def _tri_inverse_chain(ms):
    c = CHUNK
    ri, ci = _iota((c, c), 0), _iota((c, c), 1)
    eye = (ri == ci).astype(F32)
    d16 = (ri // 16) == (ci // 16)
    d32 = (ri // 32) == (ci // 32)
    ps = [jnp.where(d16, -m, 0.0) for m in ms]
    ts = [eye + p for p in ps]
    for _ in range(3):
        ps = [_mm(p, p, P_INV) for p in ps]
        ts = [_mm(t, eye + p, P_INV) for t, p in zip(ts, ps)]
    for off_diagonal in (d32 & (~d16), ~d32):
        tq = [_mm(t, jnp.where(off_diagonal, m, 0.0), P_INV) for t, m in zip(ts, ms)]
        ts = [t - _mm(a, t, P_INV) for t, a in zip(ts, tq)]
    return ts


_tri_inverse.defvjp(lambda ms: (lambda ts: (ts, ts))(_tri_inverse_chain(ms)), _tri_inverse_bwd)


@jax.custom_vjp
def _known_inverse(ms, ts):
    return ts


_known_inverse.defvjp(lambda ms, ts: (ts, ts),
                      lambda ts, dts: (_tri_inverse_bwd(ts, dts)[0], [jnp.zeros_like(t) for t in ts]))


def _scan_rows(x, reverse):
    c = x.shape[0]
    row = _iota(x.shape, 0)
    k = 1
    while k < c:
        if reverse:
            x = x + jnp.where(row < c - k, pltpu.roll(x, c - k, 0), 0.0)
        else:
            x = x + jnp.where(row >= k, pltpu.roll(x, k, 0), 0.0)
        k *= 2
    return x


@jax.custom_vjp
def _running_sum(x):
    return _scan_rows(x, False)


_running_sum.defvjp(lambda x: (_scan_rows(x, False), None), lambda _, g: (_scan_rows(g, True),))


def _chunk_fwd(prims, *, nsub=None, scalar_decay=None, kinds=None, inverses=None):
    c = CHUNK
    ng = len(prims)
    kinds = kinds if kinds is not None else [(nsub, scalar_decay)] * ng
    s0s, rs, lws, ks, vs, kks, bs = [list(t) for t in zip(*prims)]
    ri, ci = _iota((c, c), 0), _iota((c, c), 1)
    incl = ri >= ci
    strict = ri > ci
    tril = incl.astype(F32)
    lane = _iota((1, LANES), 1)
    heads = [n for n, _ in kinds]
    scalar = [sc for _, sc in kinds]
    masks = [[((lane // (LANES // n)) == s).astype(F32) for s in range(n)] if n > 1 else [1.0] for n in heads]
    cws = [_running_sum(lw) for lw in lws]
    cwxs = [cw - lw for cw, lw in zip(cws, lws)]
    ends = [cw[c - 1:c, :] for cw in cws]
    kkds = [kk * jnp.exp(cwx) for kk, cwx in zip(kks, cwxs)]
    rds = [r * jnp.exp(cw) for r, cw in zip(rs, cws)]
    kends = [k * jnp.exp(e - cw) for k, e, cw in zip(ks, ends, cws)]
    bends = [b * jnp.exp(e - cw) for b, e, cw in zip(bs, ends, cws)]
    state_terms = [_split_rows(_mm_nt(_stack_rows([kkd, rd]), s0, P_STATE), 2) for kkd, rd, s0 in zip(kkds, rds, s0s)]
    w0s, y0s = [t[0] for t in state_terms], [t[1] for t in state_terms]
    chains = [(g, s) for g in range(ng) for s in range(heads[g])]
    rows = [_split_rows(jnp.transpose(cw), LANES // c)[0] if sc else None for cw, sc in zip(cws, scalar)]
    dxs = [jnp.where(strict, jnp.exp(jnp.minimum(cwx[:, :c] - row, 0.0)), 0.0) if sc else None
           for cwx, row, sc in zip(cwxs, rows, scalar)]
    dis = [jnp.where(incl, jnp.exp(jnp.minimum(cw[:, :c] - row, 0.0)), 0.0) if sc else None
           for cw, row, sc in zip(cws, rows, scalar)]
    lefts = [_stack_rows([a * m for m in ms] + [q * m for m in ms])
             for a, q, ms in zip([kk if sc else kkd for kk, kkd, sc in zip(kks, kkds, scalar)],
                                 [r if sc else rd for r, rd, sc in zip(rs, rds, scalar)], masks)]
    rights_b = [b if sc else b * jnp.exp(-cw) for b, cw, sc in zip(bs, cws, scalar)]
    rights_k = [k if sc else k * jnp.exp(-cw) for k, cw, sc in zip(ks, cws, scalar)]
    on_b = [_split_rows(_mm_nt(left, right, P_SCORE), 2 * n) for left, right, n in zip(lefts, rights_b, heads)]
    on_k = [_split_rows(_mm_nt(left, right, P_SCORE), 2 * n) for left, right, n in zip(lefts, rights_k, heads)]
    lower = lambda x, g: x * dxs[g] if scalar[g] else jnp.where(strict, x, 0.0)
    lower_incl = lambda x, g: x * dis[g] if scalar[g] else jnp.where(incl, x, 0.0)
    m_b = [lower(on_b[g][s], g) for g, s in chains]
    m_k = [lower(on_k[g][s], g) for g, s in chains]
    n_k = [lower_incl(on_k[g][heads[g] + s], g) for g, s in chains]
    n_b = [lower_incl(on_b[g][heads[g] + s], g) for g, s in chains]
    t_inv = _tri_inverse(m_b) if inverses is None else _known_inverse(m_b, inverses)
    on_v = [_split_rows(_mm(_stack_rows([mk, nk]), vs[g], P_APPLY), 2) for (g, s), mk, nk in zip(chains, m_k, n_k)]
    sa_c = [_mm(t, w0s[g] + mv[0], P_APPLY) for (g, s), t, mv in zip(chains, t_inv, on_v)]
    y_c = [y0s[g] + mv[1] - _mm(nb, sa, P_APPLY) for (g, s), mv, nb, sa in zip(chains, on_v, n_b, sa_c)]
    first = [sum(heads[:g]) for g in range(ng)]
    per_group = lambda xs: [functools.reduce(lambda p, q: p + q, [xs[first[g] + s] * masks[g][s] for s in range(heads[g])])
                            for g in range(ng)]
    sas, ys = per_group(sa_c), per_group(y_c)
    s_ends = [s0 * jnp.exp(e) + _mm_tn(_stack_rows([v, -sa]), _stack_rows([kend, bend]), P_UPDATE)
              for s0, e, v, kend, sa, bend in zip(s0s, ends, vs, kends, sas, bends)]
    row_head = lambda n: _iota((LANES, LANES), 0) // (LANES // n)
    col_head = lambda n: _iota((LANES, LANES), 1) // (LANES // n)
    s_ends = [jnp.where(row_head(n) == col_head(n), s_end, 0.0) if n > 1 else s_end for s_end, n in zip(s_ends, heads)]
    return list(zip(ys, s_ends)), t_inv


def _rec_fwd(name, branches, *, seq):
    n, w = branches[0][0][0].shape
    ng = w // LANES
    nc = seq // CHUNK
    nb = n // seq
    nbr = len(branches)
    per = nb * ng
    kinds = [(heads, scalar) for _, heads, scalar in branches for _ in range(per)]
    nts = [per * heads for _, heads, _ in branches]

    def body(*refs):
        in_refs = [refs[6 * i:6 * i + 6] for i in range(nbr)]
        out_refs = [refs[6 * nbr + 3 * i:6 * nbr + 3 * i + 3] for i in range(nbr)]
        states = refs[9 * nbr:]

        @pl.when(pl.program_id(0) == 0)
        def _():
            for state in states:
                state[...] = jnp.zeros_like(state)
        where = [(i, bi, g) for i in range(nbr) for bi in range(nb) for g in range(ng)]
        prims = [(states[i][bi * ng + g],) + tuple(ref[bi, :, _sl(g)] for ref in in_refs[i]) for i, bi, g in where]
        outs, t_inv = _chunk_fwd(prims, kinds=kinds)
        for (i, bi, g), prim, (y, s_end) in zip(where, prims, outs):
            y_ref, s_ref, _ = out_refs[i]
            s_ref[0, bi * ng + g] = prim[0]
            y_ref[bi, :, _sl(g)] = y
            states[i][bi * ng + g] = s_end
        pos = 0
        for i in range(nbr):
            for j in range(nts[i]):
                out_refs[i][2][0, j] = t_inv[pos + j]
            pos += nts[i]

    row = pl.BlockSpec((nb, CHUNK, w), lambda c: (0, c, 0))
    seqs = lambda a: a.reshape(nb, seq, w)
    res = pl.pallas_call(
        body, name=name, grid=(nc,),
        in_specs=[row] * (6 * nbr),
        out_specs=[spec for nt in nts for spec in (row, pl.BlockSpec((1, per, LANES, LANES), lambda c: (c, 0, 0, 0)),
                                                   pl.BlockSpec((1, nt, CHUNK, CHUNK), lambda c: (c, 0, 0, 0)))],
        out_shape=[shp for nt in nts for shp in (jax.ShapeDtypeStruct((nb, seq, w), F32),
                                                 jax.ShapeDtypeStruct((nc, per, LANES, LANES), F32),
                                                 jax.ShapeDtypeStruct((nc, nt, CHUNK, CHUNK), F32))],
        scratch_shapes=[pltpu.VMEM((per, LANES, LANES), F32)] * nbr,
        compiler_params=pltpu.CompilerParams(dimension_semantics=("arbitrary",), vmem_limit_bytes=VMEM_LIMIT),
    )(*[seqs(a) for arrs, _, _ in branches for a in arrs])
    return [(res[3 * i].reshape(n, w), (res[3 * i + 1], res[3 * i + 2])) for i in range(nbr)]


def _rec_bwd(name, branches, *, seq):
    n, w = branches[0][0][0].shape
    ng = w // LANES
    nc = seq // CHUNK
    nb = n // seq
    nbr = len(branches)
    per = nb * ng
    kinds = [(heads, scalar) for _, _, _, heads, scalar in branches for _ in range(per)]
    nts = [per * heads for _, _, _, heads, _ in branches]

    def body(*refs):
        in_refs = [refs[9 * i:9 * i + 9] for i in range(nbr)]
        out_refs = [refs[9 * nbr + 6 * i:9 * nbr + 6 * i + 6] for i in range(nbr)]
        dstates = refs[15 * nbr:]

        @pl.when(pl.program_id(0) == 0)
        def _():
            for dstate in dstates:
                dstate[...] = jnp.zeros_like(dstate)
        where = [(i, bi, g) for i in range(nbr) for bi in range(nb) for g in range(ng)]
        inverses = [in_refs[i][7][0, j] for i in range(nbr) for j in range(nts[i])]
        f = lambda p: _chunk_fwd(p, kinds=kinds, inverses=inverses)[0]
        prims = [(in_refs[i][6][0, bi * ng + g],) + tuple(ref[bi, :, _sl(g)] for ref in in_refs[i][:6])
                 for i, bi, g in where]
        _, vjp = jax.vjp(f, prims)
        (d_prims,) = vjp([(in_refs[i][8][bi, :, _sl(g)], dstates[i][bi * ng + g]) for i, bi, g in where])
        for (i, bi, g), d_prim in zip(where, d_prims):
            dstates[i][bi * ng + g] = d_prim[0]
            for ref, d in zip(out_refs[i], d_prim[1:]):
                ref[bi, :, _sl(g)] = d

    row = pl.BlockSpec((nb, CHUNK, w), lambda c: (0, nc - 1 - c, 0))
    seqs = lambda a: a.reshape(nb, seq, w)
    in_specs, args = [], []
    for (arrs, (s_save, t_save), dy, _, _), nt in zip(branches, nts):
        in_specs += [row] * 6 + [pl.BlockSpec((1, per, LANES, LANES), lambda c: (nc - 1 - c, 0, 0, 0)),
                                 pl.BlockSpec((1, nt, CHUNK, CHUNK), lambda c: (nc - 1 - c, 0, 0, 0)), row]
        args += [seqs(a) for a in arrs] + [s_save, t_save, seqs(dy)]
    grads = pl.pallas_call(
        body, name=name, grid=(nc,),
        in_specs=in_specs,
        out_specs=[row] * (6 * nbr),
        out_shape=[jax.ShapeDtypeStruct((nb, seq, w), F32)] * (6 * nbr),
        scratch_shapes=[pltpu.VMEM((per, LANES, LANES), F32)] * nbr,
        compiler_params=pltpu.CompilerParams(dimension_semantics=("arbitrary",), vmem_limit_bytes=VMEM_LIMIT),
    )(*args)
    return [[g.reshape(n, w) for g in grads[6 * i:6 * i + 6]] for i in range(nbr)]


def _shift_down(a, j, halo, is_start):
    tb = a.shape[0]
    rolled = pltpu.roll(a, j, 0)
    hr = jnp.where(is_start, 0.0, pltpu.roll(halo, j, 0))
    first = jnp.where(_iota((SUB, LANES), 0) < j, hr, rolled[0:SUB])
    if tb == SUB:
        return first
    return jnp.concatenate([first, rolled[SUB:]], axis=0)


def _shift_up(d, j, carry, is_end):
    tb = d.shape[0]
    up = pltpu.roll(d, tb - j, 0)
    cr = jnp.where(is_end, 0.0, pltpu.roll(carry, SUB - j, 0))
    last = jnp.where(_iota((SUB, LANES), 0) >= SUB - j, cr, up[tb - SUB:tb])
    if tb == SUB:
        return last
    return jnp.concatenate([up[:tb - SUB], last], axis=0)


def _ngroups(a):
    return a.shape[1] // LANES


def _pw_fwd(name, f, ins, shift, params, out_widths, out_dtypes, *, seq, tb):
    n = ins[0].shape[0]
    nt, tps = n // tb, seq // tb
    ni, npar = len(ins), len(params)

    def body(*refs):
        in_refs = refs[:ni]
        pos = ni
        halo_ref = None
        if shift:
            halo_ref = refs[pos]
            pos += 1
        p_refs = refs[pos:pos + npar]
        out_refs = refs[pos + npar:]
        is_start = (pl.program_id(0) % tps) == 0
        tiles = [[ref[:, _sl(g)] for g in range(_ngroups(ref))] for ref in in_refs]
        prevs = [[_shift_down(tiles[0][g], j, halo_ref[:, _sl(g)], is_start) for g in range(len(tiles[0]))]
                 for j in range(1, shift + 1)]
        pv = [[ref[:, _sl(g)] for g in range(_ngroups(ref))] for ref in p_refs]
        outs = f(tiles, prevs, pv)
        for o_ref, og in zip(out_refs, outs, strict=True):
            for g, t in enumerate(og):
                o_ref[:, _sl(g)] = t.astype(o_ref.dtype)

    in_specs = [pl.BlockSpec((tb, a.shape[1]), lambda i: (i, 0)) for a in ins]
    args = list(ins)
    if shift:
        in_specs.append(pl.BlockSpec((SUB, ins[0].shape[1]), lambda i: (jnp.maximum(i * (tb // SUB) - 1, 0), 0)))
        args.append(ins[0])
    in_specs += [pl.BlockSpec(p.shape, lambda i: (0, 0)) for p in params]
    args += list(params)
    return pl.pallas_call(
        body, name=name, grid=(nt,),
        in_specs=in_specs,
        out_specs=[pl.BlockSpec((tb, w), lambda i: (i, 0)) for w in out_widths],
        out_shape=[jax.ShapeDtypeStruct((n, w), dt) for w, dt in zip(out_widths, out_dtypes, strict=True)],
        compiler_params=pltpu.CompilerParams(dimension_semantics=("parallel",), vmem_limit_bytes=VMEM_LIMIT),
    )(*args)


def _proj_pw_fwd(name, f, h, wts, shift, params, out_widths, *, seq, tb):
    n = h.shape[0]
    nt, tps = n // tb, seq // tb
    nw, npar = len(wts), len(params)

    def body(*refs):
        h_ref = refs[0]
        w_refs = refs[1:1 + nw]
        par_refs = refs[1 + nw:1 + nw + npar]
        p_refs = refs[1 + nw + npar:1 + 2 * nw + npar]
        out_refs = refs[1 + 2 * nw + npar:len(refs) - 1]
        carry = refs[-1]
        is_start = (pl.program_id(0) % tps) == 0
        for w_ref, p_ref in zip(w_refs, p_refs, strict=True):
            p_ref[...] = lax.dot_general(h_ref[...], w_ref[...], (_NT, ((), ())), preferred_element_type=F32)
        tiles = [[ref[:, _sl(g)] for g in range(_ngroups(ref))] for ref in p_refs]
        prevs = [[_shift_down(tiles[0][g], j, carry[:, _sl(g)], is_start) for g in range(len(tiles[0]))]
                 for j in range(1, shift + 1)]
        carry[...] = p_refs[0][tb - SUB:tb, :]
        pv = [[ref[:, _sl(g)] for g in range(_ngroups(ref))] for ref in par_refs]
        outs = f(tiles, prevs, pv)
        for o_ref, og in zip(out_refs, outs, strict=True):
            for g, t in enumerate(og):
                o_ref[:, _sl(g)] = t

    widths = [w.shape[0] for w in wts] + list(out_widths)
    res = pl.pallas_call(
        body, name=name, grid=(nt,),
        in_specs=([pl.BlockSpec((tb, D_MODEL), lambda i: (i, 0))] + [pl.BlockSpec(w.shape, lambda i: (0, 0)) for w in wts]
                  + [pl.BlockSpec(p.shape, lambda i: (0, 0)) for p in params]),
        out_specs=[pl.BlockSpec((tb, w), lambda i: (i, 0)) for w in widths],
        out_shape=[jax.ShapeDtypeStruct((n, w), F32) for w in widths],
        scratch_shapes=[pltpu.VMEM((SUB, wts[0].shape[0]), F32)],
        compiler_params=pltpu.CompilerParams(dimension_semantics=("arbitrary",), vmem_limit_bytes=VMEM_LIMIT),
    )(h, *wts, *params)
    return res[:nw], res[nw:]


def _pw_bwd(name, f, ins, shift, params, douts, din_dtypes, *, seq, tb):
    n = ins[0].shape[0]
    nt, tps = n // tb, seq // tb
    ni, npar = len(ins), len(params)
    flat_douts = [d for ds in douts for d in ds]
    nd = len(flat_douts)
    w0 = ins[0].shape[1]

    def body(*refs):
        in_refs = refs[:ni]
        pos = ni
        halo_ref = None
        if shift:
            halo_ref = refs[pos]
            pos += 1
        p_refs = refs[pos:pos + npar]
        pos += npar
        d_refs = refs[pos:pos + nd]
        pos += nd
        din_refs = refs[pos:pos + ni]
        pos += ni
        dp_refs = refs[pos:pos + npar]
        pos += npar
        carry = refs[pos] if shift else None
        step = pl.program_id(0)
        tile = nt - 1 - step
        is_start = (tile % tps) == 0
        is_end = (tile % tps) == tps - 1
        tiles = [[ref[:, _sl(g)] for g in range(_ngroups(ref))] for ref in in_refs]
        prevs = [[_shift_down(tiles[0][g], j, halo_ref[:, _sl(g)], is_start) for g in range(len(tiles[0]))]
                 for j in range(1, shift + 1)]
        pv = [[ref[:, _sl(g)] for g in range(_ngroups(ref))] for ref in p_refs]
        cot, pos_d = [], 0
        for ds in douts:
            grp = d_refs[pos_d:pos_d + len(ds)]
            pos_d += len(ds)
            cot.append([functools.reduce(lambda p, q: p + q, [ref[:, _sl(g)].astype(F32) for ref in grp])
                        for g in range(_ngroups(grp[0]))])
        _, vjp = jax.vjp(f, tiles, prevs, pv)
        d_tiles, d_prevs, d_pv = vjp(cot)
        for g in range(len(tiles[0])):
            for j in range(1, shift + 1):
                d_tiles[0][g] = d_tiles[0][g] + _shift_up(d_prevs[j - 1][g], j, carry[j - 1, :, _sl(g)], is_end)
            for j in range(1, shift + 1):
                carry[j - 1, :, _sl(g)] = d_prevs[j - 1][g][0:SUB]
        for ref, dg in zip(din_refs, d_tiles, strict=True):
            for g, t in enumerate(dg):
                ref[:, _sl(g)] = t.astype(ref.dtype)

        @pl.when(step == 0)
        def _():
            for ref in dp_refs:
                ref[...] = jnp.zeros_like(ref)
        for ref, dg in zip(dp_refs, d_pv, strict=True):
            for g, t in enumerate(dg):
                ref[:, _sl(g)] += t

    rev = lambda i: (nt - 1 - i, 0)
    in_specs = [pl.BlockSpec((tb, a.shape[1]), rev) for a in ins]
    args = list(ins)
    if shift:
        in_specs.append(pl.BlockSpec((SUB, w0), lambda i: (jnp.maximum((nt - 1 - i) * (tb // SUB) - 1, 0), 0)))
        args.append(ins[0])
    in_specs += [pl.BlockSpec(p.shape, lambda i: (0, 0)) for p in params]
    args += list(params)
    in_specs += [pl.BlockSpec((tb, d.shape[1]), rev) for d in flat_douts]
    args += flat_douts
    out_specs = [pl.BlockSpec((tb, a.shape[1]), rev) for a in ins] + [pl.BlockSpec(p.shape, lambda i: (0, 0)) for p in params]
    out_shape = ([jax.ShapeDtypeStruct(a.shape, dt) for a, dt in zip(ins, din_dtypes, strict=True)]
                 + [jax.ShapeDtypeStruct(p.shape, F32) for p in params])
    res = pl.pallas_call(
        body, name=name, grid=(nt,),
        in_specs=in_specs, out_specs=out_specs, out_shape=out_shape,
        scratch_shapes=[pltpu.VMEM((shift, SUB, w0), F32)] if shift else [],
        compiler_params=pltpu.CompilerParams(dimension_semantics=("arbitrary",), vmem_limit_bytes=VMEM_LIMIT),
    )(*args)
    return res[:ni], res[ni:]


def _rwkv_prep_f(tiles, prevs, params):
    (p,), (prev,) = tiles, prevs
    mu, w0, w2p, a0, a2p, k_k, k_a = params
    xs = [p[g] + (prev[g] - p[g]) * mu[g] for g in range(13)]
    wdad = xs[12]
    tw = jnp.tanh(wdad)
    e64 = _seg_ones(64)
    r, lw, k2, v, kk, b = [], [], [], [], [], []
    for g in range(4):
        k_g = xs[4 + g]
        lo = w0[g] + _mm(tw, w2p[g], P_POINT)
        lw_g = -jnp.exp(-_softplus(-lo) - 0.5)
        a_g = _sigmoid(a0[g] + _mm(wdad, a2p[g], P_POINT))
        kkp = k_g * k_k[g]
        kk_g = kkp * lax.rsqrt(_mm(kkp * kkp, e64, P_POINT) + 1e-12)
        r.append(xs[g])
        lw.append(lw_g)
        k2.append(k_g * (1.0 + (a_g - 1.0) * k_a[g]))
        v.append(xs[8 + g])
        kk.append(kk_g)
        b.append(kk_g * a_g)
    return [r, lw, k2, v, kk, b]


def _rwkv_post_f(tiles, prevs, params):
    yrec, r, k2, v, z = tiles
    gn_w, gn_b, r_k = params
    e64 = _seg_ones(64)
    out = []
    for g in range(4):
        mean = _mm(yrec[g], e64, P_POINT) * (1.0 / 64)
        d = yrec[g] - mean
        var = _mm(d * d, e64, P_POINT) * (1.0 / 64)
        yn = d * lax.rsqrt(var + RW_GN_EPS) * gn_w[g] + gn_b[g]
        bonus = _mm(r[g] * k2[g] * r_k[g], e64, P_POINT) * v[g]
        out.append((yn + bonus) * _silu(z[g]))
    return [out]


def _gdn_prep_f(tiles, prevs, params):
    x, (ba,) = tiles
    p1, p2, p3 = prevs
    cw0, cw1, cw2, cw3, a_log, dt_bias = params
    s = [_silu(cw3[g] * x[g] + cw2[g] * p1[g] + cw1[g] * p2[g] + cw0[g] * p3[g]) for g in range(12)]
    row = _iota((LANES, LANES), 0)
    r, lw, k, vv, b = [], [], [], [], []
    for h in range(4):
        q_h, k_h, v_h = s[h], s[4 + h], s[8 + h]
        qn = q_h * lax.rsqrt(jnp.sum(q_h * q_h, axis=-1, keepdims=True) + 1e-12)
        kn = k_h * lax.rsqrt(jnp.sum(k_h * k_h, axis=-1, keepdims=True) + 1e-12)
        beta = _sigmoid(_mm(ba, (row == h).astype(F32)))
        alpha = _mm(ba, (row == 4 + h).astype(F32))
        g_h = -jnp.exp(a_log[h]) * _softplus(alpha + dt_bias[h])
        r.append(qn * (LANES ** -0.5))
        lw.append(g_h)
        k.append(kn)
        vv.append(beta * v_h)
        b.append(jnp.exp(g_h) * beta * kn)
    return [r, lw, k, vv, b]


def _gdn_post_f(tiles, prevs, params):
    o, z = tiles
    ((onw,),) = params
    out = []
    for h in range(4):
        ms = jnp.mean(o[h] * o[h], axis=-1, keepdims=True)
        out.append(o[h] * lax.rsqrt(ms + NORM_EPS) * onw * _silu(z[h]))
    return [out]


def _norm_in(x2, g_in, *, tm):
    n = x2.shape[0]

    def body(x_ref, g_ref, h_ref):
        x = x_ref[...]
        rs = lax.rsqrt(jnp.mean(x * x, axis=-1, keepdims=True) + NORM_EPS)
        h_ref[...] = (x * rs * g_ref[...]).astype(BF16)

    return pl.pallas_call(
        body, name="norm_in", grid=(n // tm,),
        in_specs=[pl.BlockSpec((tm, D_MODEL), lambda i: (i, 0)), pl.BlockSpec((1, D_MODEL), lambda i: (0, 0))],
        out_specs=pl.BlockSpec((tm, D_MODEL), lambda i: (i, 0)),
        out_shape=jax.ShapeDtypeStruct((n, D_MODEL), BF16),
        compiler_params=pltpu.CompilerParams(dimension_semantics=("parallel",), vmem_limit_bytes=VMEM_LIMIT),
    )(x2, g_in)


def _proj(name, h, wt, *, tm):
    n, ws = h.shape[0], wt.shape[0]

    def body(h_ref, w_ref, o_ref):
        o_ref[...] = lax.dot_general(h_ref[...], w_ref[...], (_NT, ((), ())), preferred_element_type=F32)

    return pl.pallas_call(
        body, name=name, grid=(n // tm,),
        in_specs=[pl.BlockSpec((tm, D_MODEL), lambda i: (i, 0)), pl.BlockSpec((ws, D_MODEL), lambda i: (0, 0))],
        out_specs=pl.BlockSpec((tm, ws), lambda i: (i, 0)),
        out_shape=jax.ShapeDtypeStruct((n, ws), F32),
        compiler_params=pltpu.CompilerParams(dimension_semantics=("parallel",), vmem_limit_bytes=VMEM_LIMIT),
    )(h, wt)


def _proj_dw(name, h, dp, *, tm):
    n, ws = dp.shape

    def body(h_ref, d_ref, o_ref):
        @pl.when(pl.program_id(0) == 0)
        def _():
            o_ref[...] = jnp.zeros_like(o_ref)
        o_ref[...] += lax.dot_general(d_ref[...], h_ref[...], (_TN, ((), ())), preferred_element_type=F32)

    return pl.pallas_call(
        body, name=name, grid=(n // tm,),
        in_specs=[pl.BlockSpec((tm, D_MODEL), lambda i: (i, 0)), pl.BlockSpec((tm, ws), lambda i: (i, 0))],
        out_specs=pl.BlockSpec((ws, D_MODEL), lambda i: (0, 0)),
        out_shape=jax.ShapeDtypeStruct((ws, D_MODEL), F32),
        compiler_params=pltpu.CompilerParams(dimension_semantics=("arbitrary",), vmem_limit_bytes=VMEM_LIMIT),
    )(h, dp)


def _proj_dx(x2, g_in, d_xo, dps, ws, *, tm):
    n = x2.shape[0]
    ns = len(dps)

    def body(*refs):
        x_ref, g_ref, dxo_ref = refs[:3]
        dp_refs = refs[3:3 + ns]
        w_refs = refs[3 + ns:3 + 2 * ns]
        dx_ref, dg_ref = refs[3 + 2 * ns:]
        dh = jnp.zeros((tm, D_MODEL), F32)
        for d_ref, w_ref in zip(dp_refs, w_refs, strict=True):
            dh = dh + jnp.dot(d_ref[...], w_ref[...], preferred_element_type=F32)
        x = x_ref[...]
        rs = lax.rsqrt(jnp.mean(x * x, axis=-1, keepdims=True) + NORM_EPS)
        xn = x * rs
        dxn = dh * g_ref[...]
        dx_ref[...] = dxo_ref[...] + rs * (dxn - xn * jnp.mean(dxn * xn, axis=-1, keepdims=True))

        @pl.when(pl.program_id(0) == 0)
        def _():
            dg_ref[...] = jnp.zeros_like(dg_ref)
        dg_ref[...] += jnp.sum(dh * xn, axis=0, keepdims=True)

    row = pl.BlockSpec((tm, D_MODEL), lambda i: (i, 0))
    return pl.pallas_call(
        body, name="proj_dx", grid=(n // tm,),
        in_specs=([row, pl.BlockSpec((1, D_MODEL), lambda i: (0, 0)), row]
                  + [pl.BlockSpec((tm, d.shape[1]), lambda i: (i, 0)) for d in dps]
                  + [pl.BlockSpec(w.shape, lambda i: (0, 0)) for w in ws]),
        out_specs=[row, pl.BlockSpec((1, D_MODEL), lambda i: (0, 0))],
        out_shape=[jax.ShapeDtypeStruct((n, D_MODEL), F32), jax.ShapeDtypeStruct((1, D_MODEL), F32)],
        compiler_params=pltpu.CompilerParams(dimension_semantics=("arbitrary",), vmem_limit_bytes=VMEM_LIMIT),
    )(x2, g_in, d_xo, *dps, *ws)


def _tail(x2, tgt2, gates, rw_post_ins, gd_post_ins, rw_post_params, gd_post_params, din_dtypes, w_a, w_b, w_o, now, *, tr):
    n = x2.shape[0]
    n_rw, n_gd = len(rw_post_ins), len(gd_post_ins)
    n_rwp, n_gdp = len(rw_post_params), len(gd_post_params)

    def body(*refs):
        x_ref, t_ref, g_ref = refs[:3]
        pos = 3
        rw_refs, gd_refs = refs[pos:pos + n_rw], refs[pos + n_rw:pos + n_rw + n_gd]
        pos += n_rw + n_gd
        rwp_refs, gdp_refs = refs[pos:pos + n_rwp], refs[pos + n_rwp:pos + n_rwp + n_gdp]
        pos += n_rwp + n_gdp
        wa_ref, wb_ref, wo_ref, now_ref = refs[pos:pos + 4]
        pos += 4
        d_rw_refs, d_gd_refs = refs[pos:pos + n_rw], refs[pos + n_rw:pos + n_rw + n_gd]
        pos += n_rw + n_gd
        dg_ref, dxo_ref, dwa_ref, dwb_ref, dwo_ref, dnow_ref, loss_ref = refs[pos:pos + 7]
        d_rwp_refs, d_gdp_refs = refs[pos + 7:pos + 7 + n_rwp], refs[pos + 7 + n_rwp:]
        groups = lambda rs: [[ref[:, _sl(g)] for g in range(_ngroups(ref))] for ref in rs]
        (ya_groups,), rw_vjp = jax.vjp(lambda t, p: _rwkv_post_f(t, [], p), groups(rw_refs), groups(rwp_refs))
        (yb_groups,), gd_vjp = jax.vjp(lambda t, p: _gdn_post_f(t, [], p), groups(gd_refs), groups(gdp_refs))
        ya16 = jnp.concatenate(ya_groups, axis=1).astype(BF16)
        yb16 = jnp.concatenate(yb_groups, axis=1).astype(BF16)
        ua = jnp.dot(ya16, wa_ref[...], preferred_element_type=F32)
        ub = jnp.dot(yb16, wb_ref[...], preferred_element_type=F32)
        ga = _sigmoid(g_ref[:, :D_MODEL])
        gb = _sigmoid(g_ref[:, D_MODEL:])
        m16 = (ga * ua + gb * ub).astype(BF16)
        xo = x_ref[...] + jnp.dot(m16, wo_ref[...], preferred_element_type=F32)
        rs = lax.rsqrt(jnp.mean(xo * xo, axis=-1, keepdims=True) + NORM_EPS)
        yn = xo * rs
        now_v = now_ref[...]
        err = yn * now_v - t_ref[...]
        dy = err * (1.0 / D_MODEL)
        dyn = dy * now_v
        dxo = rs * (dyn - yn * jnp.mean(dyn * yn, axis=-1, keepdims=True))
        dxo_ref[...] = dxo
        dxo16 = dxo.astype(BF16)
        dm = lax.dot_general(dxo16, wo_ref[...], (((1,), (1,)), ((), ())), preferred_element_type=F32)
        dua16 = (dm * ga).astype(BF16)
        dub16 = (dm * gb).astype(BF16)
        dg_ref[:, :D_MODEL] = (dm * ua * ga * (1.0 - ga)).astype(dg_ref.dtype)
        dg_ref[:, D_MODEL:] = (dm * ub * gb * (1.0 - gb)).astype(dg_ref.dtype)
        dya = lax.dot_general(dua16, wa_ref[...], (((1,), (1,)), ((), ())), preferred_element_type=F32)
        dyb = lax.dot_general(dub16, wb_ref[...], (((1,), (1,)), ((), ())), preferred_element_type=F32)
        d_rw_tiles, d_rw_pv = rw_vjp([[dya[:, _sl(g)] for g in range(RW_W // LANES)]])
        d_gd_tiles, d_gd_pv = gd_vjp([[dyb[:, _sl(g)] for g in range(GD_W // LANES)]])
        for ref, dgroups in zip(d_rw_refs + d_gd_refs, d_rw_tiles + d_gd_tiles, strict=True):
            for g, t in enumerate(dgroups):
                ref[:, _sl(g)] = t.astype(ref.dtype)

        @pl.when(pl.program_id(0) == 0)
        def _():
            for ref in (dwa_ref, dwb_ref, dwo_ref, dnow_ref, loss_ref) + d_rwp_refs + d_gdp_refs:
                ref[...] = jnp.zeros_like(ref)
        for ref, dgroups in zip(d_rwp_refs + d_gdp_refs, d_rw_pv + d_gd_pv, strict=True):
            for g, t in enumerate(dgroups):
                ref[:, _sl(g)] += t
        tn = (((0,), (0,)), ((), ()))
        dwo_ref[...] += lax.dot_general(m16, dxo16, tn, preferred_element_type=F32)
        dwa_ref[...] += lax.dot_general(ya16, dua16, tn, preferred_element_type=F32)
        dwb_ref[...] += lax.dot_general(yb16, dub16, tn, preferred_element_type=F32)
        dnow_ref[...] += jnp.sum(dy * yn, axis=0, keepdims=True)
        loss_ref[...] += (0.5 / D_MODEL) * jnp.sum(err * err)

    row = lambda w: pl.BlockSpec((tr, w), lambda i: (i, 0))
    full = lambda a: pl.BlockSpec(a.shape, lambda i: (0, 0))
    res = pl.pallas_call(
        body, name="tail", grid=(n // tr,),
        in_specs=([row(D_MODEL), row(D_MODEL), row(2 * D_MODEL)] + [row(a.shape[1]) for a in rw_post_ins + gd_post_ins]
                  + [full(p) for p in rw_post_params + gd_post_params] + [full(w_a), full(w_b), full(w_o), full(now)]),
        out_specs=([row(a.shape[1]) for a in rw_post_ins + gd_post_ins] + [row(2 * D_MODEL), row(D_MODEL),
                   pl.BlockSpec((RW_W, D_MODEL), lambda i: (0, 0)), pl.BlockSpec((GD_W, D_MODEL), lambda i: (0, 0)),
                   pl.BlockSpec((D_MODEL, D_MODEL), lambda i: (0, 0)), pl.BlockSpec((1, D_MODEL), lambda i: (0, 0)),
                   pl.BlockSpec((SUB, LANES), lambda i: (0, 0))] + [full(p) for p in rw_post_params + gd_post_params]),
        out_shape=([jax.ShapeDtypeStruct(a.shape, dt) for a, dt in zip(rw_post_ins + gd_post_ins, din_dtypes, strict=True)]
                   + [jax.ShapeDtypeStruct((n, 2 * D_MODEL), BF16), jax.ShapeDtypeStruct((n, D_MODEL), F32),
                      jax.ShapeDtypeStruct((RW_W, D_MODEL), F32), jax.ShapeDtypeStruct((GD_W, D_MODEL), F32),
                      jax.ShapeDtypeStruct((D_MODEL, D_MODEL), F32), jax.ShapeDtypeStruct((1, D_MODEL), F32),
                      jax.ShapeDtypeStruct((SUB, LANES), F32)]
                   + [jax.ShapeDtypeStruct(p.shape, F32) for p in rw_post_params + gd_post_params]),
        compiler_params=pltpu.CompilerParams(dimension_semantics=("arbitrary",), vmem_limit_bytes=VMEM_LIMIT),
    )(x2, tgt2, gates, *rw_post_ins, *gd_post_ins, *rw_post_params, *gd_post_params, w_a, w_b, w_o, now)
    ni, npar = n_rw + n_gd, n_rwp + n_gdp
    return res[:ni], res[ni:ni + 7], res[ni + 7:ni + 7 + npar]


def _exchange(name, axes, scatter, gather, place_own=True):
    ns, ng = len(scatter), len(gather)
    na = ns + ng
    gs = 2 ** len(axes)
    arrs = list(scatter) + list(gather)

    def body(*refs):
        src = refs[:na]
        dst = refs[na:2 * na]
        send_sems, recv_sems = refs[2 * na:]
        mine = {ax: lax.axis_index(ax) for ax in ("x", "y", "c")}

        def peer(k):
            co = dict(mine)
            for i, ax in enumerate(axes):
                if (k >> (len(axes) - 1 - i)) & 1:
                    co[ax] = 1 - co[ax]
            idx = 0
            for ax in axes:
                idx = 2 * idx + co[ax]
            return (co["x"], co["y"], co["c"]), idx

        _, me = peer(0)

        def copy(a, k, landing):
            dev, idx = peer(k)
            s = src[a].at[idx] if a < ns else src[a]
            return pltpu.make_async_remote_copy(src_ref=s, dst_ref=dst[a].at[idx if landing else me],
                                                send_sem=send_sems.at[a, k - 1], recv_sem=recv_sems.at[a, k - 1],
                                                device_id=dev, device_id_type=pl.DeviceIdType.MESH)

        sends = [copy(a, k, False) for a in range(na) for k in range(1, gs)]
        for cp in sends:
            cp.start()
        for a in range(na):
            for k in range(1, gs):
                copy(a, k, True).wait_recv()
        for cp in sends:
            cp.wait_send()

    out_shape = [jax.ShapeDtypeStruct(a.shape, a.dtype) for a in scatter] + \
                [jax.ShapeDtypeStruct((gs,) + a.shape, a.dtype) for a in gather]
    anyspec = pl.BlockSpec(memory_space=pl.ANY)
    lands = pl.pallas_call(
        body, name=name,
        in_specs=[anyspec] * na, out_specs=[anyspec] * na, out_shape=out_shape,
        scratch_shapes=[pltpu.SemaphoreType.DMA((na, gs - 1)), pltpu.SemaphoreType.DMA((na, gs - 1))],
    )(*arrs)
    if not place_own:
        return lands
    me = 0
    for ax in axes:
        me = 2 * me + lax.axis_index(ax)
    kept = [lax.dynamic_index_in_dim(a, me, 0, keepdims=False) for a in scatter] + list(gather)
    return [lax.dynamic_update_index_in_dim(land, mine, me, 0) for land, mine in zip(lands, kept)]


def _gather_all(name, arrs):
    na = len(arrs)

    def body(*refs):
        src = refs[:na]
        dst = refs[na:2 * na]
        send_sems, recv_sems = refs[2 * na:]
        x, y, c = lax.axis_index("x"), lax.axis_index("y"), lax.axis_index("c")
        sibling = (x, y, 1 - c)
        chips = [(1 - x, y), (x, 1 - y), (1 - x, 1 - y)]

        def copy(a, k, block, to, own=False):
            px, py, pc = block
            slot = dst[a].at[pc, 2 * px + py]
            return pltpu.make_async_remote_copy(src_ref=src[a] if own else slot, dst_ref=slot,
                                                send_sem=send_sems.at[a, k], recv_sem=recv_sems.at[a, k],
                                                device_id=to, device_id_type=pl.DeviceIdType.MESH)

        first = [copy(a, 0, (x, y, c), sibling, own=True) for a in range(na)]
        first += [copy(a, 1 + j, (x, y, c), (*chip, c), own=True) for j, chip in enumerate(chips) for a in range(na)]
        for cp in first:
            cp.start()
        passed = []
        for j, chip in enumerate(chips):
            for a in range(na):
                copy(a, 1 + j, (*chip, c), (x, y, c)).wait_recv()
                passed.append(copy(a, 4 + j, (*chip, c), sibling))
                passed[-1].start()
        for a in range(na):
            copy(a, 0, (x, y, 1 - c), (x, y, c)).wait_recv()
            for j, chip in enumerate(chips):
                copy(a, 4 + j, (*chip, 1 - c), (x, y, c)).wait_recv()
        for cp in first + passed:
            cp.wait_send()

    anyspec = pl.BlockSpec(memory_space=pl.ANY)
    lands = pl.pallas_call(
        body, name=name,
        in_specs=[anyspec] * na, out_specs=[anyspec] * na,
        out_shape=[jax.ShapeDtypeStruct((2, 4) + a.shape, a.dtype) for a in arrs],
        scratch_shapes=[pltpu.SemaphoreType.DMA((na, 7)), pltpu.SemaphoreType.DMA((na, 7))],
    )(*arrs)
    core, chip = lax.axis_index("c"), 2 * lax.axis_index("x") + lax.axis_index("y")
    zero = jnp.zeros((), jnp.int32)
    return [lax.dynamic_update_slice(land, mine[None, None], (core, chip) + (zero,) * mine.ndim)
            for land, mine in zip(lands, arrs)]


def _pair_sum(name, own, land, out_dtype):
    _, nq, r, c = own.shape
    core = lax.axis_index("c").astype(jnp.int32).reshape(1)

    def body(core_ref, own_ref, land_ref, o_ref):
        o_ref[0] = (own_ref[0, 0] + land_ref[0, 0]).astype(o_ref.dtype)

    return pl.pallas_call(
        body, name=name,
        grid_spec=pltpu.PrefetchScalarGridSpec(
            num_scalar_prefetch=1, grid=(nq,),
            in_specs=[pl.BlockSpec((1, 1, r, c), lambda i, core_ref: (core_ref[0], i, 0, 0)),
                      pl.BlockSpec((1, 1, r, c), lambda i, core_ref: (1 - core_ref[0], i, 0, 0))],
            out_specs=pl.BlockSpec((1, r, c), lambda i, core_ref: (i, 0, 0))),
        out_shape=jax.ShapeDtypeStruct((nq, r, c), out_dtype),
        compiler_params=pltpu.CompilerParams(dimension_semantics=("parallel",), vmem_limit_bytes=VMEM_LIMIT),
    )(core, own, land)


def _adam(name, land, w, m, v):
    r, c = w.shape
    nslot = land.shape[0]
    tr = 256 if (r % 256 == 0 and r > 256) else r
    tc = 256 if (tr == r and r > 256 and c % 256 == 0) else c

    def body(l_ref, w_ref, m_ref, v_ref, g_out, d_out, m_out, v_out):
        g = l_ref[0].astype(F32)
        for s in range(1, nslot):
            g = g + l_ref[s].astype(F32)
        g_out[...] = g
        d_out[...], m_out[...], v_out[...] = _adam_math(g, w_ref[...], m_ref[...], v_ref[...])

    blk = pl.BlockSpec((tr, tc), lambda i: (i * tc // c, i % (c // tc)))
    return pl.pallas_call(
        body, name=name, grid=((r // tr) * (c // tc),),
        in_specs=[pl.BlockSpec((nslot, tr, tc), lambda i: (0, i * tc // c, i % (c // tc))), blk, blk, blk],
        out_specs=[blk] * 4,
        out_shape=[jax.ShapeDtypeStruct((r, c), F32)] * 4,
        compiler_params=pltpu.CompilerParams(dimension_semantics=("parallel",), vmem_limit_bytes=VMEM_LIMIT),
    )(land, w, m, v)


def _adam_math(g, w, m, v):
    c1 = 1.0 / (1.0 - ADAM_B1 ** ADAM_STEP)
    c2 = 1.0 / (1.0 - ADAM_B2 ** ADAM_STEP)
    m_new = ADAM_B1 * m + (1.0 - ADAM_B1) * g
    v_new = ADAM_B2 * v + (1.0 - ADAM_B2) * (g * g)
    return -ADAM_LR * ((m_new * c1) / (jnp.sqrt(v_new * c2) + ADAM_EPS) + ADAM_WD * w), m_new, v_new


def _adam_small(land, ws, ms, vs):
    npar = len(ws)
    nslot = land.shape[0]

    def body(*refs):
        l_ref = refs[0]
        w_refs, m_refs, v_refs = refs[1:1 + npar], refs[1 + npar:1 + 2 * npar], refs[1 + 2 * npar:1 + 3 * npar]
        outs = refs[1 + 3 * npar:1 + 7 * npar]
        loss_ref, g_rows = refs[1 + 7 * npar], refs[2 + 7 * npar]
        g = l_ref[0]
        for s in range(1, nslot):
            g = g + l_ref[s]
        g_rows[...] = g
        row = 0
        for i, (_, size) in enumerate(_SMALL):
            for j in range(-(-size // LANES)):
                width = min(LANES, size - j * LANES)
                cols = slice(j * LANES, j * LANES + width)
                g_ij = g_rows[row:row + 1, 0:width]
                delta, m_new, v_new = _adam_math(g_ij, w_refs[i][:, cols], m_refs[i][:, cols], v_refs[i][:, cols])
                for ref, val in zip(outs[4 * i:4 * i + 4], (g_ij, delta, m_new, v_new)):
                    ref[:, cols] = val
                row += 1
        loss_ref[...] = g_rows[row:row + 1, :]

    full = lambda a: pl.BlockSpec(a.shape, lambda: (0,) * a.ndim)
    res = pl.pallas_call(
        body, name="adam_small",
        in_specs=[full(land)] + [full(a) for a in list(ws) + list(ms) + list(vs)],
        out_specs=[full(w) for w in ws for _ in range(4)] + [pl.BlockSpec((1, LANES), lambda: (0, 0))],
        out_shape=[jax.ShapeDtypeStruct(w.shape, F32) for w in ws for _ in range(4)] + [jax.ShapeDtypeStruct((1, LANES), F32)],
        scratch_shapes=[pltpu.VMEM(land.shape[1:], F32)],
    )(land, *ws, *ms, *vs)
    return [res[4 * i:4 * i + 4] for i in range(npar)], res[4 * npar]


_SMALL = (("norm_in_w", 1024), ("rw_mu", 1664), ("rw_w0", 512), ("rw_a0", 512), ("rw_k_k", 512), ("rw_k_a", 512),
          ("rw_r_k", 512), ("rw_gn_w", 512), ("rw_gn_b", 512), ("gd_A_log", 4), ("gd_dt_bias", 4), ("gd_o_norm_w", 128),
          ("norm_out_w", 1024))
_SMALL_ROWS = 64


def _pack_small(vals, loss_row):
    rows = []
    for (_, size), a in zip(_SMALL, vals, strict=True):
        flat = a.reshape(-1).astype(F32)
        pad = (-size) % LANES
        if pad:
            flat = jnp.concatenate([flat, jnp.zeros((pad,), F32)])
        rows.append(flat.reshape(-1, LANES))
    rows.append(loss_row)
    used = sum(r.shape[0] for r in rows)
    rows.append(jnp.zeros((_SMALL_ROWS - used, LANES), F32))
    return jnp.concatenate(rows, axis=0)


def kernel(x, norm_in_w, w_in, rw_mu, rw_w0, rw_w2, rw_a0, rw_a2, rw_k_k, rw_k_a, rw_r_k, rw_gn_w, rw_gn_b, gd_conv_w, gd_A_log, gd_dt_bias, gd_o_norm_w, w_branch_a, w_branch_b, w_out, norm_out_w, loss_target, m_norm_in_w, m_w_in, m_rw_mu, m_rw_w0, m_rw_w2, m_rw_a0, m_rw_a2, m_rw_k_k, m_rw_k_a, m_rw_r_k, m_rw_gn_w, m_rw_gn_b, m_gd_conv_w, m_gd_A_log, m_gd_dt_bias, m_gd_o_norm_w, m_w_branch_a, m_w_branch_b, m_w_out, m_norm_out_w, v_norm_in_w, v_w_in, v_rw_mu, v_rw_w0, v_rw_w2, v_rw_a0, v_rw_a2, v_rw_k_k, v_rw_k_a, v_rw_r_k, v_rw_gn_w, v_rw_gn_b, v_gd_conv_w, v_gd_A_log, v_gd_dt_bias, v_gd_o_norm_w, v_w_branch_a, v_w_branch_b, v_w_out, v_norm_out_w):
    nb, seq, _ = x.shape
    n = nb * seq
    tm = min(1024, n)
    tb = min(512, seq)
    x2 = x.reshape(n, D_MODEL)
    tgt2 = loss_target.reshape(n, D_MODEL)
    cols = w_in.shape[2]
    in_cols = cols * N_DEV

    wt_own, mt_own, vt_own = w_in[0].T, m_w_in[0].T, v_w_in[0].T
    sharded = [wt_own.astype(BF16), rw_w2[0], rw_a2[0], gd_conv_w[0], w_branch_a[0].astype(BF16),
               w_branch_b[0].astype(BF16), w_out[0].astype(BF16)]
    g_win, g_w2, g_a2, g_conv, g_wa, g_wb, g_wo = _gather_all("gather_weights", sharded)
    unshard_rows = lambda a: jnp.transpose(a, (1, 0, 2, 3)).reshape(N_DEV * a.shape[2], a.shape[3])
    unshard_cols = lambda a: jnp.transpose(a, (2, 1, 0, 3)).reshape(a.shape[2], N_DEV * a.shape[3])
    wt_full = unshard_rows(g_win)
    seg_bounds = ((0, 1664), (1664, 2176), (2176, 3712), (3712, 4224), (4232, in_cols))
    w_rw, w_zrw, w_qkv, w_zgd, w_gates = [wt_full[a:b] for a, b in seg_bounds]
    w_ba = jnp.concatenate([wt_full[4224:4232], jnp.zeros((LANES - 8, D_MODEL), BF16)], axis=0)
    w2_full, a2_full = unshard_cols(g_w2), unshard_cols(g_a2)
    zeros64 = jnp.zeros((64, RW_W), F32)
    w2p = jnp.concatenate([w2_full, zeros64], axis=0)
    a2p = jnp.concatenate([zeros64, a2_full], axis=0)
    conv_full = unshard_cols(g_conv)
    conv_rows = [conv_full[i:i + 1] for i in range(4)]
    wa_full = unshard_cols(g_wa)
    wb_full = unshard_cols(g_wb)
    wo_full = unshard_rows(g_wo)
    a_log_bc = jnp.repeat(gd_A_log, LANES, axis=1)
    dt_bias_bc = jnp.repeat(gd_dt_bias, LANES, axis=1)
    r_k_flat = rw_r_k.reshape(1, RW_W)
    now2 = norm_out_w.reshape(1, D_MODEL)

    h = _norm_in(x2, norm_in_w, tm=tm)
    p_zrw = _proj("proj_zrw", h, w_zrw, tm=tm)
    p_zgd = _proj("proj_zgd", h, w_zgd, tm=tm)
    p_gates = _proj("proj_gates", h, w_gates, tm=tm)
    rw_params = [rw_mu, rw_w0, w2p, rw_a0, a2p, rw_k_k, rw_k_a]
    (p_rw,), (r_a, lw_a, k_a, v_a, kk_a, b_a) = _proj_pw_fwd("proj_rwkv_prep", _rwkv_prep_f, h, [w_rw], 1, rw_params,
                                                             [RW_W] * 6, seq=seq, tb=tb)
    gd_params = conv_rows + [a_log_bc, dt_bias_bc]
    (p_qkv, p_ba), (r_b, lw_b, k_b, v_b, b_b) = _proj_pw_fwd("proj_gdn_prep", _gdn_prep_f, h, [w_qkv, w_ba], 3, gd_params,
                                                             [GD_W] * 5, seq=seq, tb=tb)
    rw_six, gd_six = (r_a, lw_a, k_a, v_a, kk_a, b_a), (r_b, lw_b, k_b, v_b, k_b, b_b)
    (y_rec, s_a), (o_rec, s_b) = _rec_fwd("rec", [(rw_six, 2, False), (gd_six, 1, True)], seq=seq)
    post_params = [rw_gn_w, rw_gn_b, r_k_flat]
    ((d_yrec, dr_p, dk_p, dv_p, d_zrw, d_o, d_zgd), (d_gates, d_xo, dwa, dwb, dwo, d_now, loss_acc),
     (*d_post_params, d_onw)) = _tail(
        x2, tgt2, p_gates, [y_rec, r_a, k_a, v_a, p_zrw], [o_rec, p_zgd], post_params, [gd_o_norm_w],
        [F32, F32, F32, F32, BF16, F32, BF16], wa_full, wb_full, wo_full, now2, tr=min(256, n))

    (dr_a, dlw_a, dk_a, dv_a, dkk_a, db_a), (dr_b, dlw_b, dk_b, dv_b, dkk_b, db_b) = _rec_bwd(
        "rec_bwd", [(rw_six, s_a, d_yrec, 2, False), (gd_six, s_b, d_o, 1, True)], seq=seq)
    (d_qkv, d_ba), d_gd_params = _pw_bwd("gdn_prep_bwd", _gdn_prep_f, [p_qkv, p_ba], 3, gd_params,
                                         [[dr_b], [dlw_b], [dk_b, dkk_b], [dv_b], [db_b]], [BF16, BF16], seq=seq, tb=tb)
    (d_prw,), d_rw_params = _pw_bwd("rwkv_prep_bwd", _rwkv_prep_f, [p_rw], 1, rw_params,
                                    [[dr_a, dr_p], [dlw_a], [dk_a, dk_p], [dv_a, dv_p], [dkk_a], [db_a]], [BF16],
                                    seq=seq, tb=tb)

    dps = [d_prw, d_zrw, d_qkv, d_zgd, d_ba, d_gates]
    wsegs = [w_rw, w_zrw, w_qkv, w_zgd, w_ba, w_gates]
    dx2, d_gin = _proj_dx(x2, norm_in_w, d_xo, dps, wsegs, tm=min(256, n))
    tm_dw = min(2 * tm, n)
    dw_rw = _proj_dw("dw_rw", h, d_prw, tm=tm_dw)
    dw_zrw = _proj_dw("dw_zrw", h, d_zrw, tm=tm_dw)
    dw_qkv = _proj_dw("dw_qkv", h, d_qkv, tm=tm_dw)
    dw_zgd = _proj_dw("dw_zgd", h, d_zgd, tm=tm_dw)
    dw_ba = _proj_dw("dw_ba", h, d_ba, tm=tm_dw)
    dw_gates = _proj_dw("dw_gates", h, d_gates, tm=tm_dw)
    dwt_in_full = jnp.concatenate([dw_rw, dw_zrw, dw_qkv, dw_zgd, dw_ba[:8], dw_gates], axis=0)

    shard_cols = lambda a: jnp.transpose(a.reshape(a.shape[0], 4, 2, a.shape[1] // N_DEV), (2, 1, 0, 3))
    shard_rows = lambda a: jnp.transpose(a.reshape(4, 2, a.shape[0] // N_DEV, a.shape[1]), (1, 0, 2, 3))
    d_mu, d_w0, d_w2p, d_a0, d_a2p, d_kk_, d_ka_ = d_rw_params
    d_gnw, d_gnb, d_rk = d_post_params
    d_conv = jnp.concatenate(d_gd_params[:4], axis=0)
    d_alog = d_gd_params[4].reshape(4, LANES).sum(axis=1).reshape(1, 4)
    d_dtb = d_gd_params[5].reshape(4, LANES).sum(axis=1).reshape(1, 4)
    scat = [shard_rows(dwt_in_full), shard_cols(d_w2p[:64]), shard_cols(d_a2p[64:]), shard_cols(d_conv),
            shard_cols(dwa), shard_cols(dwb), shard_rows(dwo)]
    small_g = _pack_small([d_gin, d_mu, d_w0, d_a0, d_kk_, d_ka_, d_rk, d_gnw, d_gnb, d_alog, d_dtb, d_onw, d_now],
                          loss_acc[0:1])
    scat.append(jnp.stack([small_g, small_g])[:, None])
    pair = _exchange("reduce_cores", ("c",), scat, [], place_own=False)
    part = [_pair_sum("pair_sum_%d" % i, own, got, BF16 if i < 7 else F32)
            for i, (own, got) in enumerate(zip(scat, pair))]
    lands = _exchange("reduce_chips", ("x", "y"), part[:7], [part[7][0]])

    small_w = [norm_in_w, rw_mu, rw_w0, rw_a0, rw_k_k, rw_k_a, rw_r_k, rw_gn_w, rw_gn_b, gd_A_log, gd_dt_bias, gd_o_norm_w, norm_out_w]
    small_m = [m_norm_in_w, m_rw_mu, m_rw_w0, m_rw_a0, m_rw_k_k, m_rw_k_a, m_rw_r_k, m_rw_gn_w, m_rw_gn_b, m_gd_A_log, m_gd_dt_bias, m_gd_o_norm_w, m_norm_out_w]
    small_v = [v_norm_in_w, v_rw_mu, v_rw_w0, v_rw_a0, v_rw_k_k, v_rw_k_a, v_rw_r_k, v_rw_gn_w, v_rw_gn_b, v_gd_A_log, v_gd_dt_bias, v_gd_o_norm_w, v_norm_out_w]
    flat = lambda arrs: [a.reshape(1, -1) for a in arrs]
    sm, loss_row = _adam_small(lands[7], flat(small_w), flat(small_m), flat(small_v))
    sm_g, sm_d, sm_m, sm_v = [{nm: res[i].reshape(w.shape) for (nm, _), res, w in zip(_SMALL, sm, small_w)}
                              for i in range(4)]

    big = {"w_in": [o.T[None] for o in _adam("adam_w_in", lands[0], wt_own, mt_own, vt_own)]}
    for nm, land, w, m, v in (("rw_w2", lands[1], rw_w2, m_rw_w2, v_rw_w2),
                              ("rw_a2", lands[2], rw_a2, m_rw_a2, v_rw_a2),
                              ("gd_conv_w", lands[3], gd_conv_w, m_gd_conv_w, v_gd_conv_w),
                              ("w_branch_a", lands[4], w_branch_a, m_w_branch_a, v_w_branch_a),
                              ("w_branch_b", lands[5], w_branch_b, m_w_branch_b, v_w_branch_b),
                              ("w_out", lands[6], w_out, m_w_out, v_w_out)):
        big[nm] = [o.reshape(w.shape) for o in _adam("adam_" + nm, land, w[0], m[0], v[0])]

    order = ["norm_in_w", "w_in", "rw_mu", "rw_w0", "rw_w2", "rw_a0", "rw_a2", "rw_k_k", "rw_k_a", "rw_r_k", "rw_gn_w",
             "rw_gn_b", "gd_conv_w", "gd_A_log", "gd_dt_bias", "gd_o_norm_w", "w_branch_a", "w_branch_b", "w_out", "norm_out_w"]
    pick = lambda nm, i: big[nm][i] if nm in big else (sm_g, sm_d, sm_m, sm_v)[i][nm]
    loss = loss_row[0, 0]
    grad_x = dx2.reshape(x.shape)
    return (loss, grad_x, *[pick(nm, 0) for nm in order], *[pick(nm, 1) for nm in order],
            *[pick(nm, 2) for nm in order], *[pick(nm, 3) for nm in order])
```

```python
import functools

import jax
import jax.numpy as jnp
from jax import lax
from jax.experimental import pallas as pl
from jax.experimental.pallas import tpu as pltpu

F32 = jnp.float32
BF16 = jnp.bfloat16

LANES = 128
SUB = 8
CHUNK = 64
N_DEV = 8
VMEM_LIMIT = 56 * 1024 * 1024

D_MODEL = 1024
RW_W = 512
GD_W = 512
RW_SHIFT = 1664
NORM_EPS = 1e-6
RW_GN_EPS = 64 * 1e-5
ADAM_LR, ADAM_B1, ADAM_B2, ADAM_EPS, ADAM_WD, ADAM_STEP = 0.001, 0.9, 0.999, 1e-8, 0.01, 10


_NN, _NT, _TN = ((1,), (0,)), ((1,), (1,)), ((0,), (0,))


def _dot(a, b, dims, passes):
    precision = lax.Precision.HIGH if passes == 3 else lax.Precision.DEFAULT
    return lax.dot_general(a, b, (dims, ((), ())), precision=precision, preferred_element_type=F32)


def _mm(a, b, passes=3):
    return _dot(a, b, _NN, passes)


def _mm_nt(a, b, passes=3):
    return _dot(a, b, _NT, passes)


def _mm_tn(a, b, passes=3):
    return _dot(a, b, _TN, passes)


P_SCORE = 1
P_INV = 1
P_STATE = 1
P_APPLY = 1
P_UPDATE = 1
P_POINT = 1


def _stack_rows(blocks):
    return jnp.concatenate(blocks, axis=0)


def _split_rows(x, n):
    r = x.shape[0] // n

    @jax.custom_vjp
    def split(x):
        return tuple(x[i * r:(i + 1) * r] for i in range(n))

    split.defvjp(lambda x: (split(x), None), lambda _, gs: (jnp.concatenate(gs, axis=0),))
    return split(x)


def _iota(shape, d):
    return lax.broadcasted_iota(jnp.int32, shape, d)


def _sigmoid(x):
    return 0.5 * (jnp.tanh(0.5 * x) + 1.0)


def _silu(x):
    return x * _sigmoid(x)


def _softplus(x):
    return jnp.maximum(x, 0.0) + jnp.log(1.0 + jnp.exp(-jnp.abs(x)))


def _seg_ones(seg):
    return ((_iota((LANES, LANES), 0) // seg) == (_iota((LANES, LANES), 1) // seg)).astype(F32)


def _sl(g):
    return slice(g * LANES, (g + 1) * LANES)


@jax.custom_vjp
def _tri_inverse(ms):
    return _tri_inverse_chain(ms)


def _tri_inverse_bwd(ts, dts):
    return ([-_mm_nt(_mm_tn(t, dt, P_INV), t, P_INV) for t, dt in zip(ts, dts)],)


def _tri_inverse_chain(ms):
    c = CHUNK
    ri, ci = _iota((c, c), 0), _iota((c, c), 1)
    eye = (ri == ci).astype(F32)
    d16 = (ri // 16) == (ci // 16)
    d32 = (ri // 32) == (ci // 32)
    ps = [jnp.where(d16, -m, 0.0) for m in ms]
    ts = [eye + p for p in ps]
    for _ in range(3):
        ps = [_mm(p, p, P_INV) for p in ps]
        ts = [_mm(t, eye + p, P_INV) for t, p in zip(ts, ps)]
    for off_diagonal in (d32 & (~d16), ~d32):
        tq = [_mm(t, jnp.where(off_diagonal, m, 0.0), P_INV) for t, m in zip(ts, ms)]
        ts = [t - _mm(a, t, P_INV) for t, a in zip(ts, tq)]
    return ts


_tri_inverse.defvjp(lambda ms: (lambda ts: (ts, ts))(_tri_inverse_chain(ms)), _tri_inverse_bwd)


@jax.custom_vjp
def _known_inverse(ms, ts):
    return ts


_known_inverse.defvjp(lambda ms, ts: (ts, ts),
                      lambda ts, dts: (_tri_inverse_bwd(ts, dts)[0], [jnp.zeros_like(t) for t in ts]))


def _scan_rows(x, reverse):
    c = x.shape[0]
    row = _iota(x.shape, 0)
    k = 1
    while k < c:
        if reverse:
            x = x + jnp.where(row < c - k, pltpu.roll(x, c - k, 0), 0.0)
        else:
            x = x + jnp.where(row >= k, pltpu.roll(x, k, 0), 0.0)
        k *= 2
    return x


@jax.custom_vjp
def _running_sum(x):
    return _scan_rows(x, False)


_running_sum.defvjp(lambda x: (_scan_rows(x, False), None), lambda _, g: (_scan_rows(g, True),))


def _chunk_fwd(prims, *, nsub=None, scalar_decay=None, kinds=None, inverses=None):
    c = CHUNK
    ng = len(prims)
    kinds = kinds if kinds is not None else [(nsub, scalar_decay)] * ng
    s0s, rs, lws, ks, vs, kks, bs = [list(t) for t in zip(*prims)]
    ri, ci = _iota((c, c), 0), _iota((c, c), 1)
    incl = ri >= ci
    strict = ri > ci
    lane = _iota((1, LANES), 1)
    heads = [n for n, _ in kinds]
    scalar = [sc for _, sc in kinds]
    masks = [[((lane // (LANES // n)) == s).astype(F32) for s in range(n)] if n > 1 else [1.0] for n in heads]
    cws = [_running_sum(lw) for lw in lws]
    cwxs = [cw - lw for cw, lw in zip(cws, lws)]
    ends = [cw[c - 1:c, :] for cw in cws]
    kkds = [kk * jnp.exp(cwx) for kk, cwx in zip(kks, cwxs)]
    rds = [r * jnp.exp(cw) for r, cw in zip(rs, cws)]
    kends = [k * jnp.exp(e - cw) for k, e, cw in zip(ks, ends, cws)]
    bends = [b * jnp.exp(e - cw) for b, e, cw in zip(bs, ends, cws)]
    state_terms = [_split_rows(_mm_nt(_stack_rows([kkd, rd]), s0, P_STATE), 2) for kkd, rd, s0 in zip(kkds, rds, s0s)]
    w0s, y0s = [t[0] for t in state_terms], [t[1] for t in state_terms]
    chains = [(g, s) for g in range(ng) for s in range(heads[g])]
    rows = [_split_rows(jnp.transpose(cw), LANES // c)[0] if sc else None for cw, sc in zip(cws, scalar)]
    dxs = [jnp.where(strict, jnp.exp(jnp.minimum(cwx[:, :c] - row, 0.0)), 0.0) if sc else None
           for cwx, row, sc in zip(cwxs, rows, scalar)]
    dis = [jnp.where(incl, jnp.exp(jnp.minimum(cw[:, :c] - row, 0.0)), 0.0) if sc else None
           for cw, row, sc in zip(cws, rows, scalar)]
    lefts = [_stack_rows([a * m for m in ms] + [q * m for m in ms])
             for a, q, ms in zip([kk if sc else kkd for kk, kkd, sc in zip(kks, kkds, scalar)],
                                 [r if sc else rd for r, rd, sc in zip(rs, rds, scalar)], masks)]
    rights_b = [b if sc else b * jnp.exp(-cw) for b, cw, sc in zip(bs, cws, scalar)]
    rights_k = [k if sc else k * jnp.exp(-cw) for k, cw, sc in zip(ks, cws, scalar)]
    on_b = [_split_rows(_mm_nt(left, right, P_SCORE), 2 * n) for left, right, n in zip(lefts, rights_b, heads)]
    on_k = [_split_rows(_mm_nt(left, right, P_SCORE), 2 * n) for left, right, n in zip(lefts, rights_k, heads)]
    lower = lambda x, g: x * dxs[g] if scalar[g] else jnp.where(strict, x, 0.0)
    lower_incl = lambda x, g: x * dis[g] if scalar[g] else jnp.where(incl, x, 0.0)
    m_b = [lower(on_b[g][s], g) for g, s in chains]
    m_k = [lower(on_k[g][s], g) for g, s in chains]
    n_k = [lower_incl(on_k[g][heads[g] + s], g) for g, s in chains]
    n_b = [lower_incl(on_b[g][heads[g] + s], g) for g, s in chains]
    t_inv = _tri_inverse(m_b) if inverses is None else _known_inverse(m_b, inverses)
    on_v = [_split_rows(_mm(_stack_rows([mk, nk]), vs[g], P_APPLY), 2) for (g, s), mk, nk in zip(chains, m_k, n_k)]
    sa_c = [_mm(t, w0s[g] + mv[0], P_APPLY) for (g, s), t, mv in zip(chains, t_inv, on_v)]
    y_c = [y0s[g] + mv[1] - _mm(nb, sa, P_APPLY) for (g, s), mv, nb, sa in zip(chains, on_v, n_b, sa_c)]
    first = [sum(heads[:g]) for g in range(ng)]
    per_group = lambda xs: [functools.reduce(lambda p, q: p + q, [xs[first[g] + s] * masks[g][s] for s in range(heads[g])])
                            for g in range(ng)]
    sas, ys = per_group(sa_c), per_group(y_c)
    s_ends = [s0 * jnp.exp(e) + _mm_tn(_stack_rows([v, -sa]), _stack_rows([kend, bend]), P_UPDATE)
              for s0, e, v, kend, sa, bend in zip(s0s, ends, vs, kends, sas, bends)]
    row_head = lambda n: _iota((LANES, LANES), 0) // (LANES // n)
    col_head = lambda n: _iota((LANES, LANES), 1) // (LANES // n)
    s_ends = [jnp.where(row_head(n) == col_head(n), s_end, 0.0) if n > 1 else s_end for s_end, n in zip(s_ends, heads)]
    return list(zip(ys, s_ends)), t_inv


def _rec_fwd(name, branches, *, seq):
    n, w = branches[0][0][0].shape
    ng = w // LANES
    nc = seq // CHUNK
    nb = n // seq
    nbr = len(branches)
    per = nb * ng
    kinds = [(heads, scalar) for _, heads, scalar in branches for _ in range(per)]
    nts = [per * heads for _, heads, _ in branches]

    def body(*refs):
        in_refs = [refs[6 * i:6 * i + 6] for i in range(nbr)]
        out_refs = [refs[6 * nbr + 3 * i:6 * nbr + 3 * i + 3] for i in range(nbr)]
        states = refs[9 * nbr:]

        @pl.when(pl.program_id(0) == 0)
        def _():
            for state in states:
                state[...] = jnp.zeros_like(state)
        where = [(i, bi, g) for i in range(nbr) for bi in range(nb) for g in range(ng)]
        prims = [(states[i][bi * ng + g],) + tuple(ref[bi, :, _sl(g)] for ref in in_refs[i]) for i, bi, g in where]
        outs, t_inv = _chunk_fwd(prims, kinds=kinds)
        for (i, bi, g), prim, (y, s_end) in zip(where, prims, outs):
            y_ref, s_ref, _ = out_refs[i]
            s_ref[0, bi * ng + g] = prim[0]
            y_ref[bi, :, _sl(g)] = y
            states[i][bi * ng + g] = s_end
        pos = 0
        for i in range(nbr):
            for j in range(nts[i]):
                out_refs[i][2][0, j] = t_inv[pos + j]
            pos += nts[i]

    row = pl.BlockSpec((nb, CHUNK, w), lambda c: (0, c, 0))
    seqs = lambda a: a.reshape(nb, seq, w)
    res = pl.pallas_call(
        body, name=name, grid=(nc,),
        in_specs=[row] * (6 * nbr),
        out_specs=[spec for nt in nts for spec in (row, pl.BlockSpec((1, per, LANES, LANES), lambda c: (c, 0, 0, 0)),
                                                   pl.BlockSpec((1, nt, CHUNK, CHUNK), lambda c: (c, 0, 0, 0)))],
        out_shape=[shp for nt in nts for shp in (jax.ShapeDtypeStruct((nb, seq, w), F32),
                                                 jax.ShapeDtypeStruct((nc, per, LANES, LANES), F32),
                                                 jax.ShapeDtypeStruct((nc, nt, CHUNK, CHUNK), F32))],
        scratch_shapes=[pltpu.VMEM((per, LANES, LANES), F32)] * nbr,
        compiler_params=pltpu.CompilerParams(dimension_semantics=("arbitrary",), vmem_limit_bytes=VMEM_LIMIT),
    )(*[seqs(a) for arrs, _, _ in branches for a in arrs])
    return [(res[3 * i].reshape(n, w), (res[3 * i + 1], res[3 * i + 2])) for i in range(nbr)]


def _rec_bwd(name, branches, *, seq):
    n, w = branches[0][0][0].shape
    ng = w // LANES
    nc = seq // CHUNK
    nb = n // seq
    nbr = len(branches)
    per = nb * ng
    kinds = [(heads, scalar) for _, _, _, heads, scalar in branches for _ in range(per)]
    nts = [per * heads for _, _, _, heads, _ in branches]

    def body(*refs):
        in_refs = [refs[9 * i:9 * i + 9] for i in range(nbr)]
        out_refs = [refs[9 * nbr + 6 * i:9 * nbr + 6 * i + 6] for i in range(nbr)]
        dstates = refs[15 * nbr:]

        @pl.when(pl.program_id(0) == 0)
        def _():
            for dstate in dstates:
                dstate[...] = jnp.zeros_like(dstate)
        where = [(i, bi, g) for i in range(nbr) for bi in range(nb) for g in range(ng)]
        inverses = [in_refs[i][7][0, j] for i in range(nbr) for j in range(nts[i])]
        f = lambda p: _chunk_fwd(p, kinds=kinds, inverses=inverses)[0]
        prims = [(in_refs[i][6][0, bi * ng + g],) + tuple(ref[bi, :, _sl(g)] for ref in in_refs[i][:6])
                 for i, bi, g in where]
        _, vjp = jax.vjp(f, prims)
        (d_prims,) = vjp([(in_refs[i][8][bi, :, _sl(g)], dstates[i][bi * ng + g]) for i, bi, g in where])
        for (i, bi, g), d_prim in zip(where, d_prims):
            dstates[i][bi * ng + g] = d_prim[0]
            for ref, d in zip(out_refs[i], d_prim[1:]):
                ref[bi, :, _sl(g)] = d

    row = pl.BlockSpec((nb, CHUNK, w), lambda c: (0, nc - 1 - c, 0))
    seqs = lambda a: a.reshape(nb, seq, w)
    in_specs, args = [], []
    for (arrs, (s_save, t_save), dy, _, _), nt in zip(branches, nts):
        in_specs += [row] * 6 + [pl.BlockSpec((1, per, LANES, LANES), lambda c: (nc - 1 - c, 0, 0, 0)),
                                 pl.BlockSpec((1, nt, CHUNK, CHUNK), lambda c: (nc - 1 - c, 0, 0, 0)), row]
        args += [seqs(a) for a in arrs] + [s_save, t_save, seqs(dy)]
    grads = pl.pallas_call(
        body, name=name, grid=(nc,),
        in_specs=in_specs,
        out_specs=[row] * (6 * nbr),
        out_shape=[jax.ShapeDtypeStruct((nb, seq, w), F32)] * (6 * nbr),
        scratch_shapes=[pltpu.VMEM((per, LANES, LANES), F32)] * nbr,
        compiler_params=pltpu.CompilerParams(dimension_semantics=("arbitrary",), vmem_limit_bytes=VMEM_LIMIT),
    )(*args)
    return [[g.reshape(n, w) for g in grads[6 * i:6 * i + 6]] for i in range(nbr)]


def _shift_down(a, j, halo, is_start):
    tb = a.shape[0]
    rolled = pltpu.roll(a, j, 0)
    hr = jnp.where(is_start, 0.0, pltpu.roll(halo, j, 0))
    first = jnp.where(_iota((SUB, LANES), 0) < j, hr, rolled[0:SUB])
    if tb == SUB:
        return first
    return jnp.concatenate([first, rolled[SUB:]], axis=0)


def _shift_up(d, j, carry, is_end):
    tb = d.shape[0]
    up = pltpu.roll(d, tb - j, 0)
    cr = jnp.where(is_end, 0.0, pltpu.roll(carry, SUB - j, 0))
    last = jnp.where(_iota((SUB, LANES), 0) >= SUB - j, cr, up[tb - SUB:tb])
    if tb == SUB:
        return last
    return jnp.concatenate([up[:tb - SUB], last], axis=0)


def _ngroups(a):
    return a.shape[1] // LANES


def _pw_fwd(name, f, ins, shift, params, out_widths, out_dtypes, *, seq, tb):
    n = ins[0].shape[0]
    nt, tps = n // tb, seq // tb
    ni, npar = len(ins), len(params)

    def body(*refs):
        in_refs = refs[:ni]
        pos = ni
        halo_ref = None
        if shift:
            halo_ref = refs[pos]
            pos += 1
        p_refs = refs[pos:pos + npar]
        out_refs = refs[pos + npar:]
        is_start = (pl.program_id(0) % tps) == 0
        tiles = [[ref[:, _sl(g)] for g in range(_ngroups(ref))] for ref in in_refs]
        prevs = [[_shift_down(tiles[0][g], j, halo_ref[:, _sl(g)], is_start) for g in range(len(tiles[0]))]
                 for j in range(1, shift + 1)]
        pv = [[ref[:, _sl(g)] for g in range(_ngroups(ref))] for ref in p_refs]
        outs = f(tiles, prevs, pv)
        for o_ref, og in zip(out_refs, outs, strict=True):
            for g, t in enumerate(og):
                o_ref[:, _sl(g)] = t.astype(o_ref.dtype)

    in_specs = [pl.BlockSpec((tb, a.shape[1]), lambda i: (i, 0)) for a in ins]
    args = list(ins)
    if shift:
        in_specs.append(pl.BlockSpec((SUB, ins[0].shape[1]), lambda i: (jnp.maximum(i * (tb // SUB) - 1, 0), 0)))
        args.append(ins[0])
    in_specs += [pl.BlockSpec(p.shape, lambda i: (0, 0)) for p in params]
    args += list(params)
    return pl.pallas_call(
        body, name=name, grid=(nt,),
        in_specs=in_specs,
        out_specs=[pl.BlockSpec((tb, w), lambda i: (i, 0)) for w in out_widths],
        out_shape=[jax.ShapeDtypeStruct((n, w), dt) for w, dt in zip(out_widths, out_dtypes, strict=True)],
        compiler_params=pltpu.CompilerParams(dimension_semantics=("parallel",), vmem_limit_bytes=VMEM_LIMIT),
    )(*args)


def _proj_pw_fwd(name, f, h, wts, shift, params, out_widths, *, seq, tb):
    n = h.shape[0]
    nt, tps = n // tb, seq // tb
    nw, npar = len(wts), len(params)

    def body(*refs):
        h_ref = refs[0]
        w_refs = refs[1:1 + nw]
        par_refs = refs[1 + nw:1 + nw + npar]
        p_refs = refs[1 + nw + npar:1 + 2 * nw + npar]
        out_refs = refs[1 + 2 * nw + npar:len(refs) - 1]
        carry = refs[-1]
        is_start = (pl.program_id(0) % tps) == 0
        for w_ref, p_ref in zip(w_refs, p_refs, strict=True):
            p_ref[...] = lax.dot_general(h_ref[...], w_ref[...], (_NT, ((), ())), preferred_element_type=F32)
        tiles = [[ref[:, _sl(g)] for g in range(_ngroups(ref))] for ref in p_refs]
        prevs = [[_shift_down(tiles[0][g], j, carry[:, _sl(g)], is_start) for g in range(len(tiles[0]))]
                 for j in range(1, shift + 1)]
        carry[...] = p_refs[0][tb - SUB:tb, :]
        pv = [[ref[:, _sl(g)] for g in range(_ngroups(ref))] for ref in par_refs]
        outs = f(tiles, prevs, pv)
        for o_ref, og in zip(out_refs, outs, strict=True):
            for g, t in enumerate(og):
                o_ref[:, _sl(g)] = t

    widths = [w.shape[0] for w in wts] + list(out_widths)
    res = pl.pallas_call(
        body, name=name, grid=(nt,),
        in_specs=([pl.BlockSpec((tb, D_MODEL), lambda i: (i, 0))] + [pl.BlockSpec(w.shape, lambda i: (0, 0)) for w in wts]
                  + [pl.BlockSpec(p.shape, lambda i: (0, 0)) for p in params]),
        out_specs=[pl.BlockSpec((tb, w), lambda i: (i, 0)) for w in widths],
        out_shape=[jax.ShapeDtypeStruct((n, w), F32) for w in widths],
        scratch_shapes=[pltpu.VMEM((SUB, wts[0].shape[0]), F32)],
        compiler_params=pltpu.CompilerParams(dimension_semantics=("arbitrary",), vmem_limit_bytes=VMEM_LIMIT),
    )(h, *wts, *params)
    return res[:nw], res[nw:]


def _pw_bwd(name, f, ins, shift, params, douts, din_dtypes, *, seq, tb):
    n = ins[0].shape[0]
    nt, tps = n // tb, seq // tb
    ni, npar = len(ins), len(params)
    flat_douts = [d for ds in douts for d in ds]
    nd = len(flat_douts)
    w0 = ins[0].shape[1]

    def body(*refs):
        in_refs = refs[:ni]
        pos = ni
        halo_ref = None
        if shift:
            halo_ref = refs[pos]
            pos += 1
        p_refs = refs[pos:pos + npar]
        pos += npar
        d_refs = refs[pos:pos + nd]
        pos += nd
        din_refs = refs[pos:pos + ni]
        pos += ni
        dp_refs = refs[pos:pos + npar]
        pos += npar
        carry = refs[pos] if shift else None
        step = pl.program_id(0)
        tile = nt - 1 - step
        is_start = (tile % tps) == 0
        is_end = (tile % tps) == tps - 1
        tiles = [[ref[:, _sl(g)] for g in range(_ngroups(ref))] for ref in in_refs]
        prevs = [[_shift_down(tiles[0][g], j, halo_ref[:, _sl(g)], is_start) for g in range(len(tiles[0]))]
                 for j in range(1, shift + 1)]
        pv = [[ref[:, _sl(g)] for g in range(_ngroups(ref))] for ref in p_refs]
        cot, pos_d = [], 0
        for ds in douts:
            grp = d_refs[pos_d:pos_d + len(ds)]
            pos_d += len(ds)
            cot.append([functools.reduce(lambda p, q: p + q, [ref[:, _sl(g)].astype(F32) for ref in grp])
                        for g in range(_ngroups(grp[0]))])
        _, vjp = jax.vjp(f, tiles, prevs, pv)
        d_tiles, d_prevs, d_pv = vjp(cot)
        for g in range(len(tiles[0])):
            for j in range(1, shift + 1):
                d_tiles[0][g] = d_tiles[0][g] + _shift_up(d_prevs[j - 1][g], j, carry[j - 1, :, _sl(g)], is_end)
            for j in range(1, shift + 1):
                carry[j - 1, :, _sl(g)] = d_prevs[j - 1][g][0:SUB]
        for ref, dg in zip(din_refs, d_tiles, strict=True):
            for g, t in enumerate(dg):
                ref[:, _sl(g)] = t.astype(ref.dtype)

        @pl.when(step == 0)
        def _():
            for ref in dp_refs:
                ref[...] = jnp.zeros_like(ref)
        for ref, dg in zip(dp_refs, d_pv, strict=True):
            for g, t in enumerate(dg):
                ref[:, _sl(g)] += t

    rev = lambda i: (nt - 1 - i, 0)
    in_specs = [pl.BlockSpec((tb, a.shape[1]), rev) for a in ins]
    args = list(ins)
    if shift:
        in_specs.append(pl.BlockSpec((SUB, w0), lambda i: (jnp.maximum((nt - 1 - i) * (tb // SUB) - 1, 0), 0)))
        args.append(ins[0])
    in_specs += [pl.BlockSpec(p.shape, lambda i: (0, 0)) for p in params]
    args += list(params)
    in_specs += [pl.BlockSpec((tb, d.shape[1]), rev) for d in flat_douts]
    args += flat_douts
    out_specs = [pl.BlockSpec((tb, a.shape[1]), rev) for a in ins] + [pl.BlockSpec(p.shape, lambda i: (0, 0)) for p in params]
    out_shape = ([jax.ShapeDtypeStruct(a.shape, dt) for a, dt in zip(ins, din_dtypes, strict=True)]
                 + [jax.ShapeDtypeStruct(p.shape, F32) for p in params])
    res = pl.pallas_call(
        body, name=name, grid=(nt,),
        in_specs=in_specs, out_specs=out_specs, out_shape=out_shape,
        scratch_shapes=[pltpu.VMEM((shift, SUB, w0), F32)] if shift else [],
        compiler_params=pltpu.CompilerParams(dimension_semantics=("arbitrary",), vmem_limit_bytes=VMEM_LIMIT),
    )(*args)
    return res[:ni], res[ni:]


def _rwkv_prep_f(tiles, prevs, params):
    (p,), (prev,) = tiles, prevs
    mu, w0, w2p, a0, a2p, k_k, k_a = params
    xs = [p[g] + (prev[g] - p[g]) * mu[g] for g in range(13)]
    wdad = xs[12]
    tw = jnp.tanh(wdad)
    e64 = _seg_ones(64)
    r, lw, k2, v, kk, b = [], [], [], [], [], []
    for g in range(4):
        k_g = xs[4 + g]
        lo = w0[g] + _mm(tw, w2p[g], P_POINT)
        lw_g = -jnp.exp(-_softplus(-lo) - 0.5)
        a_g = _sigmoid(a0[g] + _mm(wdad, a2p[g], P_POINT))
        kkp = k_g * k_k[g]
        kk_g = kkp * lax.rsqrt(_mm(kkp * kkp, e64, P_POINT) + 1e-12)
        r.append(xs[g])
        lw.append(lw_g)
        k2.append(k_g * (1.0 + (a_g - 1.0) * k_a[g]))
        v.append(xs[8 + g])
        kk.append(kk_g)
        b.append(kk_g * a_g)
    return [r, lw, k2, v, kk, b]


def _rwkv_post_f(tiles, prevs, params):
    yrec, r, k2, v, z = tiles
    gn_w, gn_b, r_k = params
    e64 = _seg_ones(64)
    out = []
    for g in range(4):
        mean = _mm(yrec[g], e64, P_POINT) * (1.0 / 64)
        d = yrec[g] - mean
        var = _mm(d * d, e64, P_POINT) * (1.0 / 64)
        yn = d * lax.rsqrt(var + RW_GN_EPS) * gn_w[g] + gn_b[g]
        bonus = _mm(r[g] * k2[g] * r_k[g], e64, P_POINT) * v[g]
        out.append((yn + bonus) * _silu(z[g]))
    return [out]


def _gdn_prep_f(tiles, prevs, params):
    x, (ba,) = tiles
    p1, p2, p3 = prevs
    cw0, cw1, cw2, cw3, a_log, dt_bias = params
    s = [_silu(cw3[g] * x[g] + cw2[g] * p1[g] + cw1[g] * p2[g] + cw0[g] * p3[g]) for g in range(12)]
    row = _iota((LANES, LANES), 0)
    r, lw, k, vv, b = [], [], [], [], []
    for h in range(4):
        q_h, k_h, v_h = s[h], s[4 + h], s[8 + h]
        qn = q_h * lax.rsqrt(jnp.sum(q_h * q_h, axis=-1, keepdims=True) + 1e-12)
        kn = k_h * lax.rsqrt(jnp.sum(k_h * k_h, axis=-1, keepdims=True) + 1e-12)
        beta = _sigmoid(_mm(ba, (row == h).astype(F32)))
        alpha = _mm(ba, (row == 4 + h).astype(F32))
        g_h = -jnp.exp(a_log[h]) * _softplus(alpha + dt_bias[h])
        r.append(qn * (LANES ** -0.5))
        lw.append(g_h)
        k.append(kn)
        vv.append(beta * v_h)
        b.append(jnp.exp(g_h) * beta * kn)
    return [r, lw, k, vv, b]


def _gdn_post_f(tiles, prevs, params):
    o, z = tiles
    ((onw,),) = params
    out = []
    for h in range(4):
        ms = jnp.mean(o[h] * o[h], axis=-1, keepdims=True)
        out.append(o[h] * lax.rsqrt(ms + NORM_EPS) * onw * _silu(z[h]))
    return [out]


def _norm_in(x2, g_in, *, tm):
    n = x2.shape[0]

    def body(x_ref, g_ref, h_ref):
        x = x_ref[...]
        rs = lax.rsqrt(jnp.mean(x * x, axis=-1, keepdims=True) + NORM_EPS)
        h_ref[...] = (x * rs * g_ref[...]).astype(BF16)

    return pl.pallas_call(
        body, name="norm_in", grid=(n // tm,),
        in_specs=[pl.BlockSpec((tm, D_MODEL), lambda i: (i, 0)), pl.BlockSpec((1, D_MODEL), lambda i: (0, 0))],
        out_specs=pl.BlockSpec((tm, D_MODEL), lambda i: (i, 0)),
        out_shape=jax.ShapeDtypeStruct((n, D_MODEL), BF16),
        compiler_params=pltpu.CompilerParams(dimension_semantics=("parallel",), vmem_limit_bytes=VMEM_LIMIT),
    )(x2, g_in)


def _proj(name, h, wt, *, tm):
    n, ws = h.shape[0], wt.shape[0]

    def body(h_ref, w_ref, o_ref):
        o_ref[...] = lax.dot_general(h_ref[...], w_ref[...], (_NT, ((), ())), preferred_element_type=F32)

    return pl.pallas_call(
        body, name=name, grid=(n // tm,),
        in_specs=[pl.BlockSpec((tm, D_MODEL), lambda i: (i, 0)), pl.BlockSpec((ws, D_MODEL), lambda i: (0, 0))],
        out_specs=pl.BlockSpec((tm, ws), lambda i: (i, 0)),
        out_shape=jax.ShapeDtypeStruct((n, ws), F32),
        compiler_params=pltpu.CompilerParams(dimension_semantics=("parallel",), vmem_limit_bytes=VMEM_LIMIT),
    )(h, wt)


def _proj_dw(name, h, dp, *, tm):
    n, ws = dp.shape

    def body(h_ref, d_ref, o_ref):
        @pl.when(pl.program_id(0) == 0)
        def _():
            o_ref[...] = jnp.zeros_like(o_ref)
        o_ref[...] += lax.dot_general(d_ref[...], h_ref[...], (_TN, ((), ())), preferred_element_type=F32)

    return pl.pallas_call(
        body, name=name, grid=(n // tm,),
        in_specs=[pl.BlockSpec((tm, D_MODEL), lambda i: (i, 0)), pl.BlockSpec((tm, ws), lambda i: (i, 0))],
        out_specs=pl.BlockSpec((ws, D_MODEL), lambda i: (0, 0)),
        out_shape=jax.ShapeDtypeStruct((ws, D_MODEL), F32),
        compiler_params=pltpu.CompilerParams(dimension_semantics=("arbitrary",), vmem_limit_bytes=VMEM_LIMIT),
    )(h, dp)


def _proj_dx(x2, g_in, d_xo, dps, ws, *, tm):
    n = x2.shape[0]
    ns = len(dps)

    def body(*refs):
        x_ref, g_ref, dxo_ref = refs[:3]
        dp_refs = refs[3:3 + ns]
        w_refs = refs[3 + ns:3 + 2 * ns]
        dx_ref, dg_ref = refs[3 + 2 * ns:]
        dh = jnp.zeros((tm, D_MODEL), F32)
        for d_ref, w_ref in zip(dp_refs, w_refs, strict=True):
            dh = dh + jnp.dot(d_ref[...], w_ref[...], preferred_element_type=F32)
        x = x_ref[...]
        rs = lax.rsqrt(jnp.mean(x * x, axis=-1, keepdims=True) + NORM_EPS)
        xn = x * rs
        dxn = dh * g_ref[...]
        dx_ref[...] = dxo_ref[...] + rs * (dxn - xn * jnp.mean(dxn * xn, axis=-1, keepdims=True))

        @pl.when(pl.program_id(0) == 0)
        def _():
            dg_ref[...] = jnp.zeros_like(dg_ref)
        dg_ref[...] += jnp.sum(dh * xn, axis=0, keepdims=True)

    row = pl.BlockSpec((tm, D_MODEL), lambda i: (i, 0))
    return pl.pallas_call(
        body, name="proj_dx", grid=(n // tm,),
        in_specs=([row, pl.BlockSpec((1, D_MODEL), lambda i: (0, 0)), row]
                  + [pl.BlockSpec((tm, d.shape[1]), lambda i: (i, 0)) for d in dps]
                  + [pl.BlockSpec(w.shape, lambda i: (0, 0)) for w in ws]),
        out_specs=[row, pl.BlockSpec((1, D_MODEL), lambda i: (0, 0))],
        out_shape=[jax.ShapeDtypeStruct((n, D_MODEL), F32), jax.ShapeDtypeStruct((1, D_MODEL), F32)],
        compiler_params=pltpu.CompilerParams(dimension_semantics=("arbitrary",), vmem_limit_bytes=VMEM_LIMIT),
    )(x2, g_in, d_xo, *dps, *ws)


def _tail(x2, tgt2, gates, rw_post_ins, gd_post_ins, rw_post_params, gd_post_params, din_dtypes, w_a, w_b, w_o, now, *, tr):
    n = x2.shape[0]
    n_rw, n_gd = len(rw_post_ins), len(gd_post_ins)
    n_rwp, n_gdp = len(rw_post_params), len(gd_post_params)

    def body(*refs):
        x_ref, t_ref, g_ref = refs[:3]
        pos = 3
        rw_refs, gd_refs = refs[pos:pos + n_rw], refs[pos + n_rw:pos + n_rw + n_gd]
        pos += n_rw + n_gd
        rwp_refs, gdp_refs = refs[pos:pos + n_rwp], refs[pos + n_rwp:pos + n_rwp + n_gdp]
        pos += n_rwp + n_gdp
        wa_ref, wb_ref, wo_ref, now_ref = refs[pos:pos + 4]
        pos += 4
        d_rw_refs, d_gd_refs = refs[pos:pos + n_rw], refs[pos + n_rw:pos + n_rw + n_gd]
        pos += n_rw + n_gd
        dg_ref, dxo_ref, dwa_ref, dwb_ref, dwo_ref, dnow_ref, loss_ref = refs[pos:pos + 7]
        d_rwp_refs, d_gdp_refs = refs[pos + 7:pos + 7 + n_rwp], refs[pos + 7 + n_rwp:]
        groups = lambda rs: [[ref[:, _sl(g)] for g in range(_ngroups(ref))] for ref in rs]
        (ya_groups,), rw_vjp = jax.vjp(lambda t, p: _rwkv_post_f(t, [], p), groups(rw_refs), groups(rwp_refs))
        (yb_groups,), gd_vjp = jax.vjp(lambda t, p: _gdn_post_f(t, [], p), groups(gd_refs), groups(gdp_refs))
        ya16 = jnp.concatenate(ya_groups, axis=1).astype(BF16)
        yb16 = jnp.concatenate(yb_groups, axis=1).astype(BF16)
        ua = jnp.dot(ya16, wa_ref[...], preferred_element_type=F32)
        ub = jnp.dot(yb16, wb_ref[...], preferred_element_type=F32)
        ga = _sigmoid(g_ref[:, :D_MODEL])
        gb = _sigmoid(g_ref[:, D_MODEL:])
        m16 = (ga * ua + gb * ub).astype(BF16)
        xo = x_ref[...] + jnp.dot(m16, wo_ref[...], preferred_element_type=F32)
        rs = lax.rsqrt(jnp.mean(xo * xo, axis=-1, keepdims=True) + NORM_EPS)
        yn = xo * rs
        now_v = now_ref[...]
        err = yn * now_v - t_ref[...]
        dy = err * (1.0 / D_MODEL)
        dyn = dy * now_v
        dxo = rs * (dyn - yn * jnp.mean(dyn * yn, axis=-1, keepdims=True))
        dxo_ref[...] = dxo
        dxo16 = dxo.astype(BF16)
        dm = lax.dot_general(dxo16, wo_ref[...], (((1,), (1,)), ((), ())), preferred_element_type=F32)
        dua16 = (dm * ga).astype(BF16)
        dub16 = (dm * gb).astype(BF16)
        dg_ref[:, :D_MODEL] = (dm * ua * ga * (1.0 - ga)).astype(dg_ref.dtype)
        dg_ref[:, D_MODEL:] = (dm * ub * gb * (1.0 - gb)).astype(dg_ref.dtype)
        dya = lax.dot_general(dua16, wa_ref[...], (((1,), (1,)), ((), ())), preferred_element_type=F32)
        dyb = lax.dot_general(dub16, wb_ref[...], (((1,), (1,)), ((), ())), preferred_element_type=F32)
        d_rw_tiles, d_rw_pv = rw_vjp([[dya[:, _sl(g)] for g in range(RW_W // LANES)]])
        d_gd_tiles, d_gd_pv = gd_vjp([[dyb[:, _sl(g)] for g in range(GD_W // LANES)]])
        for ref, dgroups in zip(d_rw_refs + d_gd_refs, d_rw_tiles + d_gd_tiles, strict=True):
            for g, t in enumerate(dgroups):
                ref[:, _sl(g)] = t.astype(ref.dtype)

        @pl.when(pl.program_id(0) == 0)
        def _():
            for ref in (dwa_ref, dwb_ref, dwo_ref, dnow_ref, loss_ref) + d_rwp_refs + d_gdp_refs:
                ref[...] = jnp.zeros_like(ref)
        for ref, dgroups in zip(d_rwp_refs + d_gdp_refs, d_rw_pv + d_gd_pv, strict=True):
            for g, t in enumerate(dgroups):
                ref[:, _sl(g)] += t
        tn = (((0,), (0,)), ((), ()))
        dwo_ref[...] += lax.dot_general(m16, dxo16, tn, preferred_element_type=F32)
        dwa_ref[...] += lax.dot_general(ya16, dua16, tn, preferred_element_type=F32)
        dwb_ref[...] += lax.dot_general(yb16, dub16, tn, preferred_element_type=F32)
        dnow_ref[...] += jnp.sum(dy * yn, axis=0, keepdims=True)
        loss_ref[...] += (0.5 / D_MODEL) * jnp.sum(err * err)

    row = lambda w: pl.BlockSpec((tr, w), lambda i: (i, 0))
    full = lambda a: pl.BlockSpec(a.shape, lambda i: (0, 0))
    res = pl.pallas_call(
        body, name="tail", grid=(n // tr,),
        in_specs=([row(D_MODEL), row(D_MODEL), row(2 * D_MODEL)] + [row(a.shape[1]) for a in rw_post_ins + gd_post_ins]
                  + [full(p) for p in rw_post_params + gd_post_params] + [full(w_a), full(w_b), full(w_o), full(now)]),
        out_specs=([row(a.shape[1]) for a in rw_post_ins + gd_post_ins] + [row(2 * D_MODEL), row(D_MODEL),
                   pl.BlockSpec((RW_W, D_MODEL), lambda i: (0, 0)), pl.BlockSpec((GD_W, D_MODEL), lambda i: (0, 0)),
                   pl.BlockSpec((D_MODEL, D_MODEL), lambda i: (0, 0)), pl.BlockSpec((1, D_MODEL), lambda i: (0, 0)),
                   pl.BlockSpec((SUB, LANES), lambda i: (0, 0))] + [full(p) for p in rw_post_params + gd_post_params]),
        out_shape=([jax.ShapeDtypeStruct(a.shape, dt) for a, dt in zip(rw_post_ins + gd_post_ins, din_dtypes, strict=True)]
                   + [jax.ShapeDtypeStruct((n, 2 * D_MODEL), BF16), jax.ShapeDtypeStruct((n, D_MODEL), F32),
                      jax.ShapeDtypeStruct((RW_W, D_MODEL), F32), jax.ShapeDtypeStruct((GD_W, D_MODEL), F32),
                      jax.ShapeDtypeStruct((D_MODEL, D_MODEL), F32), jax.ShapeDtypeStruct((1, D_MODEL), F32),
                      jax.ShapeDtypeStruct((SUB, LANES), F32)]
                   + [jax.ShapeDtypeStruct(p.shape, F32) for p in rw_post_params + gd_post_params]),
        compiler_params=pltpu.CompilerParams(dimension_semantics=("arbitrary",), vmem_limit_bytes=VMEM_LIMIT),
    )(x2, tgt2, gates, *rw_post_ins, *gd_post_ins, *rw_post_params, *gd_post_params, w_a, w_b, w_o, now)
    ni, npar = n_rw + n_gd, n_rwp + n_gdp
    return res[:ni], res[ni:ni + 7], res[ni + 7:ni + 7 + npar]


def _exchange(name, axes, scatter, gather, place_own=True):
    ns, ng = len(scatter), len(gather)
    na = ns + ng
    gs = 2 ** len(axes)
    arrs = list(scatter) + list(gather)

    def body(*refs):
        src = refs[:na]
        dst = refs[na:2 * na]
        send_sems, recv_sems = refs[2 * na:]
        mine = {ax: lax.axis_index(ax) for ax in ("x", "y", "c")}

        def peer(k):
            co = dict(mine)
            for i, ax in enumerate(axes):
                if (k >> (len(axes) - 1 - i)) & 1:
                    co[ax] = 1 - co[ax]
            idx = 0
            for ax in axes:
                idx = 2 * idx + co[ax]
            return (co["x"], co["y"], co["c"]), idx

        _, me = peer(0)

        def copy(a, k, landing):
            dev, idx = peer(k)
            s = src[a].at[idx] if a < ns else src[a]
            return pltpu.make_async_remote_copy(src_ref=s, dst_ref=dst[a].at[idx if landing else me],
                                                send_sem=send_sems.at[a, k - 1], recv_sem=recv_sems.at[a, k - 1],
                                                device_id=dev, device_id_type=pl.DeviceIdType.MESH)

        sends = [copy(a, k, False) for a in range(na) for k in range(1, gs)]
        for cp in sends:
            cp.start()
        for a in range(na):
            for k in range(1, gs):
                copy(a, k, True).wait_recv()
        for cp in sends:
            cp.wait_send()

    out_shape = [jax.ShapeDtypeStruct(a.shape, a.dtype) for a in scatter] + \
                [jax.ShapeDtypeStruct((gs,) + a.shape, a.dtype) for a in gather]
    anyspec = pl.BlockSpec(memory_space=pl.ANY)
    lands = pl.pallas_call(
        body, name=name,
        in_specs=[anyspec] * na, out_specs=[anyspec] * na, out_shape=out_shape,
        scratch_shapes=[pltpu.SemaphoreType.DMA((na, gs - 1)), pltpu.SemaphoreType.DMA((na, gs - 1))],
    )(*arrs)
    if not place_own:
        return lands
    me = 0
    for ax in axes:
        me = 2 * me + lax.axis_index(ax)
    kept = [lax.dynamic_index_in_dim(a, me, 0, keepdims=False) for a in scatter] + list(gather)
    return [lax.dynamic_update_index_in_dim(land, mine, me, 0) for land, mine in zip(lands, kept)]


def _gather_all(name, arrs):
    na = len(arrs)

    def body(*refs):
        src = refs[:na]
        dst = refs[na:2 * na]
        send_sems, recv_sems = refs[2 * na:]
        x, y, c = lax.axis_index("x"), lax.axis_index("y"), lax.axis_index("c")
        sibling = (x, y, 1 - c)
        chips = [(1 - x, y), (x, 1 - y), (1 - x, 1 - y)]

        def copy(a, k, block, to, own=False):
            px, py, pc = block
            slot = dst[a].at[pc, 2 * px + py]
            return pltpu.make_async_remote_copy(src_ref=src[a] if own else slot, dst_ref=slot,
                                                send_sem=send_sems.at[a, k], recv_sem=recv_sems.at[a, k],
                                                device_id=to, device_id_type=pl.DeviceIdType.MESH)

        first = [copy(a, 0, (x, y, c), sibling, own=True) for a in range(na)]
        first += [copy(a, 1 + j, (x, y, c), (*chip, c), own=True) for j, chip in enumerate(chips) for a in range(na)]
        for cp in first:
            cp.start()
        passed = []
        for j, chip in enumerate(chips):
            for a in range(na):
                copy(a, 1 + j, (*chip, c), (x, y, c)).wait_recv()
                passed.append(copy(a, 4 + j, (*chip, c), sibling))
                passed[-1].start()
        for a in range(na):
            copy(a, 0, (x, y, 1 - c), (x, y, c)).wait_recv()
            for j, chip in enumerate(chips):
                copy(a, 4 + j, (*chip, 1 - c), (x, y, c)).wait_recv()
        for cp in first + passed:
            cp.wait_send()

    anyspec = pl.BlockSpec(memory_space=pl.ANY)
    lands = pl.pallas_call(
        body, name=name,
        in_specs=[anyspec] * na, out_specs=[anyspec] * na,
        out_shape=[jax.ShapeDtypeStruct((2, 4) + a.shape, a.dtype) for a in arrs],
        scratch_shapes=[pltpu.SemaphoreType.DMA((na, 7)), pltpu.SemaphoreType.DMA((na, 7))],
    )(*arrs)
    core, chip = lax.axis_index("c"), 2 * lax.axis_index("x") + lax.axis_index("y")
    zero = jnp.zeros((), jnp.int32)
    return [lax.dynamic_update_slice(land, mine[None, None], (core, chip) + (zero,) * mine.ndim)
            for land, mine in zip(lands, arrs)]


def _pair_sum(name, own, land, out_dtype):
    _, nq, r, c = own.shape
    core = lax.axis_index("c").astype(jnp.int32).reshape(1)

    def body(core_ref, own_ref, land_ref, o_ref):
        o_ref[0] = (own_ref[0, 0] + land_ref[0, 0]).astype(o_ref.dtype)

    return pl.pallas_call(
        body, name=name,
        grid_spec=pltpu.PrefetchScalarGridSpec(
            num_scalar_prefetch=1, grid=(nq,),
            in_specs=[pl.BlockSpec((1, 1, r, c), lambda i, core_ref: (core_ref[0], i, 0, 0)),
                      pl.BlockSpec((1, 1, r, c), lambda i, core_ref: (1 - core_ref[0], i, 0, 0))],
            out_specs=pl.BlockSpec((1, r, c), lambda i, core_ref: (i, 0, 0))),
        out_shape=jax.ShapeDtypeStruct((nq, r, c), out_dtype),
        compiler_params=pltpu.CompilerParams(dimension_semantics=("parallel",), vmem_limit_bytes=VMEM_LIMIT),
    )(core, own, land)


def _adam(name, land, w, m, v):
    r, c = w.shape
    nslot = land.shape[0]
    tr = 256 if (r % 256 == 0 and r > 256) else r
    tc = 256 if (tr == r and r > 256 and c % 256 == 0) else c

    def body(l_ref, w_ref, m_ref, v_ref, g_out, d_out, m_out, v_out):
        g = l_ref[0].astype(F32)
        for s in range(1, nslot):
            g = g + l_ref[s].astype(F32)
        g_out[...] = g
        d_out[...], m_out[...], v_out[...] = _adam_math(g, w_ref[...], m_ref[...], v_ref[...])

    blk = pl.BlockSpec((tr, tc), lambda i: (i * tc // c, i % (c // tc)))
    return pl.pallas_call(
        body, name=name, grid=((r // tr) * (c // tc),),
        in_specs=[pl.BlockSpec((nslot, tr, tc), lambda i: (0, i * tc // c, i % (c // tc))), blk, blk, blk],
        out_specs=[blk] * 4,
        out_shape=[jax.ShapeDtypeStruct((r, c), F32)] * 4,
        compiler_params=pltpu.CompilerParams(dimension_semantics=("parallel",), vmem_limit_bytes=VMEM_LIMIT),
    )(land, w, m, v)


def _adam_math(g, w, m, v):
    c1 = 1.0 / (1.0 - ADAM_B1 ** ADAM_STEP)
    c2 = 1.0 / (1.0 - ADAM_B2 ** ADAM_STEP)
    m_new = ADAM_B1 * m + (1.0 - ADAM_B1) * g
    v_new = ADAM_B2 * v + (1.0 - ADAM_B2) * (g * g)
    return -ADAM_LR * ((m_new * c1) / (jnp.sqrt(v_new * c2) + ADAM_EPS) + ADAM_WD * w), m_new, v_new


def _adam_small(land, ws, ms, vs):
    npar = len(ws)
    nslot = land.shape[0]

    def body(*refs):
        l_ref = refs[0]
        w_refs, m_refs, v_refs = refs[1:1 + npar], refs[1 + npar:1 + 2 * npar], refs[1 + 2 * npar:1 + 3 * npar]
        outs = refs[1 + 3 * npar:1 + 7 * npar]
        loss_ref, g_rows = refs[1 + 7 * npar], refs[2 + 7 * npar]
        g = l_ref[0]
        for s in range(1, nslot):
            g = g + l_ref[s]
        g_rows[...] = g
        row = 0
        for i, (_, size) in enumerate(_SMALL):
            for j in range(-(-size // LANES)):
                width = min(LANES, size - j * LANES)
                cols = slice(j * LANES, j * LANES + width)
                g_ij = g_rows[row:row + 1, 0:width]
                delta, m_new, v_new = _adam_math(g_ij, w_refs[i][:, cols], m_refs[i][:, cols], v_refs[i][:, cols])
                for ref, val in zip(outs[4 * i:4 * i + 4], (g_ij, delta, m_new, v_new)):
                    ref[:, cols] = val
                row += 1
        loss_ref[...] = g_rows[row:row + 1, :]

    full = lambda a: pl.BlockSpec(a.shape, lambda: (0,) * a.ndim)
    res = pl.pallas_call(
        body, name="adam_small",
        in_specs=[full(land)] + [full(a) for a in list(ws) + list(ms) + list(vs)],
        out_specs=[full(w) for w in ws for _ in range(4)] + [pl.BlockSpec((1, LANES), lambda: (0, 0))],
        out_shape=[jax.ShapeDtypeStruct(w.shape, F32) for w in ws for _ in range(4)] + [jax.ShapeDtypeStruct((1, LANES), F32)],
        scratch_shapes=[pltpu.VMEM(land.shape[1:], F32)],
    )(land, *ws, *ms, *vs)
    return [res[4 * i:4 * i + 4] for i in range(npar)], res[4 * npar]


_SMALL = (("norm_in_w", 1024), ("rw_mu", 1664), ("rw_w0", 512), ("rw_a0", 512), ("rw_k_k", 512), ("rw_k_a", 512),
          ("rw_r_k", 512), ("rw_gn_w", 512), ("rw_gn_b", 512), ("gd_A_log", 4), ("gd_dt_bias", 4), ("gd_o_norm_w", 128),
          ("norm_out_w", 1024))
_SMALL_ROWS = 64


def _pack_small(vals, loss_row):
    rows = []
    for (_, size), a in zip(_SMALL, vals, strict=True):
        flat = a.reshape(-1).astype(F32)
        pad = (-size) % LANES
        if pad:
            flat = jnp.concatenate([flat, jnp.zeros((pad,), F32)])
        rows.append(flat.reshape(-1, LANES))
    rows.append(loss_row)
    used = sum(r.shape[0] for r in rows)
    rows.append(jnp.zeros((_SMALL_ROWS - used, LANES), F32))
    return jnp.concatenate(rows, axis=0)


def kernel(x, norm_in_w, w_in, rw_mu, rw_w0, rw_w2, rw_a0, rw_a2, rw_k_k, rw_k_a, rw_r_k, rw_gn_w, rw_gn_b, gd_conv_w, gd_A_log, gd_dt_bias, gd_o_norm_w, w_branch_a, w_branch_b, w_out, norm_out_w, loss_target, m_norm_in_w, m_w_in, m_rw_mu, m_rw_w0, m_rw_w2, m_rw_a0, m_rw_a2, m_rw_k_k, m_rw_k_a, m_rw_r_k, m_rw_gn_w, m_rw_gn_b, m_gd_conv_w, m_gd_A_log, m_gd_dt_bias, m_gd_o_norm_w, m_w_branch_a, m_w_branch_b, m_w_out, m_norm_out_w, v_norm_in_w, v_w_in, v_rw_mu, v_rw_w0, v_rw_w2, v_rw_a0, v_rw_a2, v_rw_k_k, v_rw_k_a, v_rw_r_k, v_rw_gn_w, v_rw_gn_b, v_gd_conv_w, v_gd_A_log, v_gd_dt_bias, v_gd_o_norm_w, v_w_branch_a, v_w_branch_b, v_w_out, v_norm_out_w):
    nb, seq, _ = x.shape
    n = nb * seq
    tm = min(1024, n)
    tb = min(512, seq)
    x2 = x.reshape(n, D_MODEL)
    tgt2 = loss_target.reshape(n, D_MODEL)
    cols = w_in.shape[2]
    in_cols = cols * N_DEV

    wt_own, mt_own, vt_own = w_in[0].T, m_w_in[0].T, v_w_in[0].T
    sharded = [wt_own.astype(BF16), rw_w2[0], rw_a2[0], gd_conv_w[0], w_branch_a[0].astype(BF16),
               w_branch_b[0].astype(BF16), w_out[0].astype(BF16)]
    g_win, g_w2, g_a2, g_conv, g_wa, g_wb, g_wo = _gather_all("gather_weights", sharded)
    unshard_rows = lambda a: jnp.transpose(a, (1, 0, 2, 3)).reshape(N_DEV * a.shape[2], a.shape[3])
    unshard_cols = lambda a: jnp.transpose(a, (2, 1, 0, 3)).reshape(a.shape[2], N_DEV * a.shape[3])
    wt_full = unshard_rows(g_win)
    seg_bounds = ((0, 1664), (1664, 2176), (2176, 3712), (3712, 4224), (4232, in_cols))
    w_rw, w_zrw, w_qkv, w_zgd, w_gates = [wt_full[a:b] for a, b in seg_bounds]
    w_ba = jnp.concatenate([wt_full[4224:4232], jnp.zeros((LANES - 8, D_MODEL), BF16)], axis=0)
    w2_full, a2_full = unshard_cols(g_w2), unshard_cols(g_a2)
    zeros64 = jnp.zeros((64, RW_W), F32)
    w2p = jnp.concatenate([w2_full, zeros64], axis=0)
    a2p = jnp.concatenate([zeros64, a2_full], axis=0)
    conv_full = unshard_cols(g_conv)
    conv_rows = [conv_full[i:i + 1] for i in range(4)]
    wa_full = unshard_cols(g_wa)
    wb_full = unshard_cols(g_wb)
    wo_full = unshard_rows(g_wo)
    a_log_bc = jnp.repeat(gd_A_log, LANES, axis=1)
    dt_bias_bc = jnp.repeat(gd_dt_bias, LANES, axis=1)
    r_k_flat = rw_r_k.reshape(1, RW_W)
    now2 = norm_out_w.reshape(1, D_MODEL)

    h = _norm_in(x2, norm_in_w, tm=tm)
    p_zrw = _proj("proj_zrw", h, w_zrw, tm=tm)
    p_zgd = _proj("proj_zgd", h, w_zgd, tm=tm)
    p_gates = _proj("proj_gates", h, w_gates, tm=tm)
    rw_params = [rw_mu, rw_w0, w2p, rw_a0, a2p, rw_k_k, rw_k_a]
    (p_rw,), (r_a, lw_a, k_a, v_a, kk_a, b_a) = _proj_pw_fwd("proj_rwkv_prep", _rwkv_prep_f, h, [w_rw], 1, rw_params,
                                                             [RW_W] * 6, seq=seq, tb=tb)
    gd_params = conv_rows + [a_log_bc, dt_bias_bc]
    (p_qkv, p_ba), (r_b, lw_b, k_b, v_b, b_b) = _proj_pw_fwd("proj_gdn_prep", _gdn_prep_f, h, [w_qkv, w_ba], 3, gd_params,
                                                             [GD_W] * 5, seq=seq, tb=tb)
    rw_six, gd_six = (r_a, lw_a, k_a, v_a, kk_a, b_a), (r_b, lw_b, k_b, v_b, k_b, b_b)
    (y_rec, s_a), (o_rec, s_b) = _rec_fwd("rec", [(rw_six, 2, False), (gd_six, 1, True)], seq=seq)
    post_params = [rw_gn_w, rw_gn_b, r_k_flat]
    ((d_yrec, dr_p, dk_p, dv_p, d_zrw, d_o, d_zgd), (d_gates, d_xo, dwa, dwb, dwo, d_now, loss_acc),
     (*d_post_params, d_onw)) = _tail(
        x2, tgt2, p_gates, [y_rec, r_a, k_a, v_a, p_zrw], [o_rec, p_zgd], post_params, [gd_o_norm_w],
        [F32, F32, F32, F32, BF16, F32, BF16], wa_full, wb_full, wo_full, now2, tr=min(256, n))

    (dr_a, dlw_a, dk_a, dv_a, dkk_a, db_a), (dr_b, dlw_b, dk_b, dv_b, dkk_b, db_b) = _rec_bwd(
        "rec_bwd", [(rw_six, s_a, d_yrec, 2, False), (gd_six, s_b, d_o, 1, True)], seq=seq)
    (d_qkv, d_ba), d_gd_params = _pw_bwd("gdn_prep_bwd", _gdn_prep_f, [p_qkv, p_ba], 3, gd_params,
                                         [[dr_b], [dlw_b], [dk_b, dkk_b], [dv_b], [db_b]], [BF16, BF16], seq=seq, tb=tb)
    (d_prw,), d_rw_params = _pw_bwd("rwkv_prep_bwd", _rwkv_prep_f, [p_rw], 1, rw_params,
                                    [[dr_a, dr_p], [dlw_a], [dk_a, dk_p], [dv_a, dv_p], [dkk_a], [db_a]], [BF16],
                                    seq=seq, tb=tb)

    dps = [d_prw, d_zrw, d_qkv, d_zgd, d_ba, d_gates]
    wsegs = [w_rw, w_zrw, w_qkv, w_zgd, w_ba, w_gates]
    dx2, d_gin = _proj_dx(x2, norm_in_w, d_xo, dps, wsegs, tm=min(256, n))
    dw_rw = _proj_dw("dw_rw", h, d_prw, tm=tm)
    dw_zrw = _proj_dw("dw_zrw", h, d_zrw, tm=tm)
    dw_qkv = _proj_dw("dw_qkv", h, d_qkv, tm=tm)
    dw_zgd = _proj_dw("dw_zgd", h, d_zgd, tm=tm)
    dw_ba = _proj_dw("dw_ba", h, d_ba, tm=tm)
    dw_gates = _proj_dw("dw_gates", h, d_gates, tm=tm)
    dwt_in_full = jnp.concatenate([dw_rw, dw_zrw, dw_qkv, dw_zgd, dw_ba[:8], dw_gates], axis=0)

    shard_cols = lambda a: jnp.transpose(a.reshape(a.shape[0], 4, 2, a.shape[1] // N_DEV), (2, 1, 0, 3))
    shard_rows = lambda a: jnp.transpose(a.reshape(4, 2, a.shape[0] // N_DEV, a.shape[1]), (1, 0, 2, 3))
    d_mu, d_w0, d_w2p, d_a0, d_a2p, d_kk_, d_ka_ = d_rw_params
    d_gnw, d_gnb, d_rk = d_post_params
    d_conv = jnp.concatenate(d_gd_params[:4], axis=0)
    d_alog = d_gd_params[4].reshape(4, LANES).sum(axis=1).reshape(1, 4)
    d_dtb = d_gd_params[5].reshape(4, LANES).sum(axis=1).reshape(1, 4)
    scat = [shard_rows(dwt_in_full), shard_cols(d_w2p[:64]), shard_cols(d_a2p[64:]), shard_cols(d_conv),
            shard_cols(dwa), shard_cols(dwb), shard_rows(dwo)]
    small_g = _pack_small([d_gin, d_mu, d_w0, d_a0, d_kk_, d_ka_, d_rk, d_gnw, d_gnb, d_alog, d_dtb, d_onw, d_now],
                          loss_acc[0:1])
    scat.append(jnp.stack([small_g, small_g])[:, None])
    pair = _exchange("reduce_cores", ("c",), scat, [], place_own=False)
    part = [_pair_sum("pair_sum_%d" % i, own, got, BF16 if i < 7 else F32)
            for i, (own, got) in enumerate(zip(scat, pair))]
    lands = _exchange("reduce_chips", ("x", "y"), part[:7], [part[7][0]])

    small_w = [norm_in_w, rw_mu, rw_w0, rw_a0, rw_k_k, rw_k_a, rw_r_k, rw_gn_w, rw_gn_b, gd_A_log, gd_dt_bias, gd_o_norm_w, norm_out_w]
    small_m = [m_norm_in_w, m_rw_mu, m_rw_w0, m_rw_a0, m_rw_k_k, m_rw_k_a, m_rw_r_k, m_rw_gn_w, m_rw_gn_b, m_gd_A_log, m_gd_dt_bias, m_gd_o_norm_w, m_norm_out_w]
    small_v = [v_norm_in_w, v_rw_mu, v_rw_w0, v_rw_a0, v_rw_k_k, v_rw_k_a, v_rw_r_k, v_rw_gn_w, v_rw_gn_b, v_gd_A_log, v_gd_dt_bias, v_gd_o_norm_w, v_norm_out_w]
    flat = lambda arrs: [a.reshape(1, -1) for a in arrs]
    sm, loss_row = _adam_small(lands[7], flat(small_w), flat(small_m), flat(small_v))
    sm_g, sm_d, sm_m, sm_v = [{nm: res[i].reshape(w.shape) for (nm, _), res, w in zip(_SMALL, sm, small_w)}
                              for i in range(4)]

    big = {"w_in": [o.T[None] for o in _adam("adam_w_in", lands[0], wt_own, mt_own, vt_own)]}
    for nm, land, w, m, v in (("rw_w2", lands[1], rw_w2, m_rw_w2, v_rw_w2),
                              ("rw_a2", lands[2], rw_a2, m_rw_a2, v_rw_a2),
                              ("gd_conv_w", lands[3], gd_conv_w, m_gd_conv_w, v_gd_conv_w),
                              ("w_branch_a", lands[4], w_branch_a, m_w_branch_a, v_w_branch_a),
                              ("w_branch_b", lands[5], w_branch_b, m_w_branch_b, v_w_branch_b),
                              ("w_out", lands[6], w_out, m_w_out, v_w_out)):
        big[nm] = [o.reshape(w.shape) for o in _adam("adam_" + nm, land, w[0], m[0], v[0])]

    order = ["norm_in_w", "w_in", "rw_mu", "rw_w0", "rw_w2", "rw_a0", "rw_a2", "rw_k_k", "rw_k_a", "rw_r_k", "rw_gn_w",
             "rw_gn_b", "gd_conv_w", "gd_A_log", "gd_dt_bias", "gd_o_norm_w", "w_branch_a", "w_branch_b", "w_out", "norm_out_w"]
    pick = lambda nm, i: big[nm][i] if nm in big else (sm_g, sm_d, sm_m, sm_v)[i][nm]
    loss = loss_row[0, 0]
    grad_x = dx2.reshape(x.shape)
    return (loss, grad_x, *[pick(nm, 0) for nm in order], *[pick(nm, 1) for nm in order],
            *[pick(nm, 2) for nm in order], *[pick(nm, 3) for nm in order])
```

```python
import functools

import jax
import jax.numpy as jnp
from jax import lax
from jax.experimental import pallas as pl
from jax.experimental.pallas import tpu as pltpu

F32 = jnp.float32
BF16 = jnp.bfloat16

LANES = 128
SUB = 8
CHUNK = 64
N_DEV = 8
VMEM_LIMIT = 56 * 1024 * 1024

D_MODEL = 1024
RW_W = 512
GD_W = 512
RW_SHIFT = 1664
NORM_EPS = 1e-6
RW_GN_EPS = 64 * 1e-5
ADAM_LR, ADAM_B1, ADAM_B2, ADAM_EPS, ADAM_WD, ADAM_STEP = 0.001, 0.9, 0.999, 1e-8, 0.01, 10


_NN, _NT, _TN = ((1,), (0,)), ((1,), (1,)), ((0,), (0,))


def _dot(a, b, dims, passes):
    precision = lax.Precision.HIGH if passes == 3 else lax.Precision.DEFAULT
    return lax.dot_general(a, b, (dims, ((), ())), precision=precision, preferred_element_type=F32)


def _mm(a, b, passes=3):
    return _dot(a, b, _NN, passes)


def _mm_nt(a, b, passes=3):
    return _dot(a, b, _NT, passes)


def _mm_tn(a, b, passes=3):
    return _dot(a, b, _TN, passes)


P_SCORE = 1
P_INV = 1
P_STATE = 1
P_APPLY = 1
P_UPDATE = 1
P_POINT = 1


def _stack_rows(blocks):
    return jnp.concatenate(blocks, axis=0)


def _split_rows(x, n):
    r = x.shape[0] // n

    @jax.custom_vjp
    def split(x):
        return tuple(x[i * r:(i + 1) * r] for i in range(n))

    split.defvjp(lambda x: (split(x), None), lambda _, gs: (jnp.concatenate(gs, axis=0),))
    return split(x)


def _iota(shape, d):
    return lax.broadcasted_iota(jnp.int32, shape, d)


def _sigmoid(x):
    return 0.5 * (jnp.tanh(0.5 * x) + 1.0)


def _silu(x):
    return x * _sigmoid(x)


def _softplus(x):
    return jnp.maximum(x, 0.0) + jnp.log(1.0 + jnp.exp(-jnp.abs(x)))


def _seg_ones(seg):
    return ((_iota((LANES, LANES), 0) // seg) == (_iota((LANES, LANES), 1) // seg)).astype(F32)


def _sl(g):
    return slice(g * LANES, (g + 1) * LANES)


@jax.custom_vjp
def _tri_inverse(ms):
    return _tri_inverse_chain(ms)


def _tri_inverse_bwd(ts, dts):
    return ([-_mm_nt(_mm_tn(t, dt, P_INV), t, P_INV) for t, dt in zip(ts, dts)],)


def _tri_inverse_chain(ms):
    c = CHUNK
    ri, ci = _iota((c, c), 0), _iota((c, c), 1)
    eye = (ri == ci).astype(F32)
    d16 = (ri // 16) == (ci // 16)
    d32 = (ri // 32) == (ci // 32)
    ps = [jnp.where(d16, -m, 0.0) for m in ms]
    ts = [eye + p for p in ps]
    for _ in range(3):
        ps = [_mm(p, p, P_INV) for p in ps]
        ts = [_mm(t, eye + p, P_INV) for t, p in zip(ts, ps)]
    for off_diagonal in (d32 & (~d16), ~d32):
        tq = [_mm(t, jnp.where(off_diagonal, m, 0.0), P_INV) for t, m in zip(ts, ms)]
        ts = [t - _mm(a, t, P_INV) for t, a in zip(ts, tq)]
    return ts


_tri_inverse.defvjp(lambda ms: (lambda ts: (ts, ts))(_tri_inverse_chain(ms)), _tri_inverse_bwd)


@jax.custom_vjp
def _known_inverse(ms, ts):
    return ts


_known_inverse.defvjp(lambda ms, ts: (ts, ts),
                      lambda ts, dts: (_tri_inverse_bwd(ts, dts)[0], [jnp.zeros_like(t) for t in ts]))


def _scan_rows(x, reverse):
    c = x.shape[0]
    row = _iota(x.shape, 0)
    k = 1
    while k < c:
        if reverse:
            x = x + jnp.where(row < c - k, pltpu.roll(x, c - k, 0), 0.0)
        else:
            x = x + jnp.where(row >= k, pltpu.roll(x, k, 0), 0.0)
        k *= 2
    return x


@jax.custom_vjp
def _running_sum(x):
    return _scan_rows(x, False)


_running_sum.defvjp(lambda x: (_scan_rows(x, False), None), lambda _, g: (_scan_rows(g, True),))


def _chunk_fwd(prims, *, nsub=None, scalar_decay=None, kinds=None, inverses=None):
    c = CHUNK
    ng = len(prims)
    kinds = kinds if kinds is not None else [(nsub, scalar_decay)] * ng
    s0s, rs, lws, ks, vs, kks, bs = [list(t) for t in zip(*prims)]
    ri, ci = _iota((c, c), 0), _iota((c, c), 1)
    incl = ri >= ci
    strict = ri > ci
    lane = _iota((1, LANES), 1)
    heads = [n for n, _ in kinds]
    scalar = [sc for _, sc in kinds]
    masks = [[((lane // (LANES // n)) == s).astype(F32) for s in range(n)] if n > 1 else [1.0] for n in heads]
    cws = [_running_sum(lw) for lw in lws]
    cwxs = [cw - lw for cw, lw in zip(cws, lws)]
    ends = [cw[c - 1:c, :] for cw in cws]
    kkds = [kk * jnp.exp(cwx) for kk, cwx in zip(kks, cwxs)]
    rds = [r * jnp.exp(cw) for r, cw in zip(rs, cws)]
    kends = [k * jnp.exp(e - cw) for k, e, cw in zip(ks, ends, cws)]
    bends = [b * jnp.exp(e - cw) for b, e, cw in zip(bs, ends, cws)]
    state_terms = [_split_rows(_mm_nt(_stack_rows([kkd, rd]), s0, P_STATE), 2) for kkd, rd, s0 in zip(kkds, rds, s0s)]
    w0s, y0s = [t[0] for t in state_terms], [t[1] for t in state_terms]
    chains = [(g, s) for g in range(ng) for s in range(heads[g])]
    rows = [_split_rows(jnp.transpose(cw), LANES // c)[0] if sc else None for cw, sc in zip(cws, scalar)]
    dxs = [jnp.where(strict, jnp.exp(jnp.minimum(cwx[:, :c] - row, 0.0)), 0.0) if sc else None
           for cwx, row, sc in zip(cwxs, rows, scalar)]
    dis = [jnp.where(incl, jnp.exp(jnp.minimum(cw[:, :c] - row, 0.0)), 0.0) if sc else None
           for cw, row, sc in zip(cws, rows, scalar)]
    lefts = [_stack_rows([a * m for m in ms] + [q * m for m in ms])
             for a, q, ms in zip([kk if sc else kkd for kk, kkd, sc in zip(kks, kkds, scalar)],
                                 [r if sc else rd for r, rd, sc in zip(rs, rds, scalar)], masks)]
    rights_b = [b if sc else b * jnp.exp(-cw) for b, cw, sc in zip(bs, cws, scalar)]
    rights_k = [k if sc else k * jnp.exp(-cw) for k, cw, sc in zip(ks, cws, scalar)]
    on_b = [_split_rows(_mm_nt(left, right, P_SCORE), 2 * n) for left, right, n in zip(lefts, rights_b, heads)]
    on_k = [_split_rows(_mm_nt(left, right, P_SCORE), 2 * n) for left, right, n in zip(lefts, rights_k, heads)]
    lower = lambda x, g: x * dxs[g] if scalar[g] else jnp.where(strict, x, 0.0)
    lower_incl = lambda x, g: x * dis[g] if scalar[g] else jnp.where(incl, x, 0.0)
    m_b = [lower(on_b[g][s], g) for g, s in chains]
    m_k = [lower(on_k[g][s], g) for g, s in chains]
    n_k = [lower_incl(on_k[g][heads[g] + s], g) for g, s in chains]
    n_b = [lower_incl(on_b[g][heads[g] + s], g) for g, s in chains]
    t_inv = _tri_inverse(m_b) if inverses is None else _known_inverse(m_b, inverses)
    on_v = [_split_rows(_mm(_stack_rows([mk, nk]), vs[g], P_APPLY), 2) for (g, s), mk, nk in zip(chains, m_k, n_k)]
    sa_c = [_mm(t, w0s[g] + mv[0], P_APPLY) for (g, s), t, mv in zip(chains, t_inv, on_v)]
    y_c = [y0s[g] + mv[1] - _mm(nb, sa, P_APPLY) for (g, s), mv, nb, sa in zip(chains, on_v, n_b, sa_c)]
    first = [sum(heads[:g]) for g in range(ng)]
    per_group = lambda xs: [functools.reduce(lambda p, q: p + q, [xs[first[g] + s] * masks[g][s] for s in range(heads[g])])
                            for g in range(ng)]
    sas, ys = per_group(sa_c), per_group(y_c)
    s_ends = [s0 * jnp.exp(e) + _mm_tn(_stack_rows([v, -sa]), _stack_rows([kend, bend]), P_UPDATE)
              for s0, e, v, kend, sa, bend in zip(s0s, ends, vs, kends, sas, bends)]
    row_head = lambda n: _iota((LANES, LANES), 0) // (LANES // n)
    col_head = lambda n: _iota((LANES, LANES), 1) // (LANES // n)
    s_ends = [jnp.where(row_head(n) == col_head(n), s_end, 0.0) if n > 1 else s_end for s_end, n in zip(s_ends, heads)]
    return list(zip(ys, s_ends)), t_inv


def _rec_fwd(name, branches, *, seq, late=()):
    n, w = branches[0][0][0].shape
    ng = w // LANES
    nc = seq // CHUNK
    nb = n // seq
    nbr = len(branches)
    nl = len(late)
    per = nb * ng
    kinds = [(heads, scalar) for _, heads, scalar in branches for _ in range(per)]
    nts = [per * heads for _, heads, _ in branches]

    def body(*refs):
        in_refs = [refs[6 * i:6 * i + 6] for i in range(nbr)]
        late_src = refs[6 * nbr:6 * nbr + nl]
        pos = 6 * nbr + nl
        out_refs = [refs[pos + 3 * i:pos + 3 * i + 3] for i in range(nbr)]
        late_dst = refs[pos + 3 * nbr:pos + 3 * nbr + nl]
        states = refs[pos + 3 * nbr + nl:pos + 4 * nbr + nl]

        def late_copy(a, k, landing):
            send_sems, recv_sems = refs[pos + 4 * nbr + nl:]
            x, y, c = lax.axis_index("x"), lax.axis_index("y"), lax.axis_index("c")
            px, py, pc = (1 - x if k & 4 else x), (1 - y if k & 2 else y), (1 - c if k & 1 else c)
            slot = late_dst[a].at[pc, 2 * px + py] if landing else late_dst[a].at[c, 2 * x + y]
            return pltpu.make_async_remote_copy(src_ref=late_src[a], dst_ref=slot, send_sem=send_sems.at[a, k - 1],
                                                recv_sem=recv_sems.at[a, k - 1], device_id=(px, py, pc),
                                                device_id_type=pl.DeviceIdType.MESH)

        @pl.when(pl.program_id(0) == 0)
        def _():
            for state in states:
                state[...] = jnp.zeros_like(state)
            for a in range(nl):
                for k in range(1, N_DEV):
                    late_copy(a, k, False).start()
        where = [(i, bi, g) for i in range(nbr) for bi in range(nb) for g in range(ng)]
        prims = [(states[i][bi * ng + g],) + tuple(ref[bi, :, _sl(g)] for ref in in_refs[i]) for i, bi, g in where]
        outs, t_inv = _chunk_fwd(prims, kinds=kinds)
        for (i, bi, g), prim, (y, s_end) in zip(where, prims, outs):
            y_ref, s_ref, _ = out_refs[i]
            s_ref[0, bi * ng + g] = prim[0]
            y_ref[bi, :, _sl(g)] = y
            states[i][bi * ng + g] = s_end
        first = 0
        for i in range(nbr):
            for j in range(nts[i]):
                out_refs[i][2][0, j] = t_inv[first + j]
            first += nts[i]

        @pl.when(pl.program_id(0) == nc - 1)
        def _():
            for a in range(nl):
                for k in range(1, N_DEV):
                    late_copy(a, k, True).wait_recv()
            for a in range(nl):
                for k in range(1, N_DEV):
                    late_copy(a, k, False).wait_send()

    row = pl.BlockSpec((nb, CHUNK, w), lambda c: (0, c, 0))
    anyspec = pl.BlockSpec(memory_space=pl.ANY)
    seqs = lambda a: a.reshape(nb, seq, w)
    res = pl.pallas_call(
        body, name=name, grid=(nc,),
        in_specs=[row] * (6 * nbr) + [anyspec] * nl,
        out_specs=[spec for nt in nts for spec in (row, pl.BlockSpec((1, per, LANES, LANES), lambda c: (c, 0, 0, 0)),
                                                   pl.BlockSpec((1, nt, CHUNK, CHUNK), lambda c: (c, 0, 0, 0)))]
                  + [anyspec] * nl,
        out_shape=[shp for nt in nts for shp in (jax.ShapeDtypeStruct((nb, seq, w), F32),
                                                 jax.ShapeDtypeStruct((nc, per, LANES, LANES), F32),
                                                 jax.ShapeDtypeStruct((nc, nt, CHUNK, CHUNK), F32))]
                  + [jax.ShapeDtypeStruct((2, 4) + a.shape, a.dtype) for a in late],
        scratch_shapes=[pltpu.VMEM((per, LANES, LANES), F32)] * nbr
                       + ([pltpu.SemaphoreType.DMA((nl, N_DEV - 1)), pltpu.SemaphoreType.DMA((nl, N_DEV - 1))] if nl else []),
        compiler_params=pltpu.CompilerParams(dimension_semantics=("arbitrary",), vmem_limit_bytes=VMEM_LIMIT),
    )(*[seqs(a) for arrs, _, _ in branches for a in arrs], *late)
    core, chip = lax.axis_index("c"), 2 * lax.axis_index("x") + lax.axis_index("y")
    zero = jnp.zeros((), jnp.int32)
    landed = [lax.dynamic_update_slice(land, mine[None, None], (core, chip) + (zero,) * mine.ndim)
              for land, mine in zip(res[3 * nbr:], late)]
    return [(res[3 * i].reshape(n, w), (res[3 * i + 1], res[3 * i + 2])) for i in range(nbr)], landed


def _rec_bwd(name, branches, *, seq):
    n, w = branches[0][0][0].shape
    ng = w // LANES
    nc = seq // CHUNK
    nb = n // seq
    nbr = len(branches)
    per = nb * ng
    kinds = [(heads, scalar) for _, _, _, heads, scalar in branches for _ in range(per)]
    nts = [per * heads for _, _, _, heads, _ in branches]

    def body(*refs):
        in_refs = [refs[9 * i:9 * i + 9] for i in range(nbr)]
        out_refs = [refs[9 * nbr + 6 * i:9 * nbr + 6 * i + 6] for i in range(nbr)]
        dstates = refs[15 * nbr:]

        @pl.when(pl.program_id(0) == 0)
        def _():
            for dstate in dstates:
                dstate[...] = jnp.zeros_like(dstate)
        where = [(i, bi, g) for i in range(nbr) for bi in range(nb) for g in range(ng)]
        inverses = [in_refs[i][7][0, j] for i in range(nbr) for j in range(nts[i])]
        f = lambda p: _chunk_fwd(p, kinds=kinds, inverses=inverses)[0]
        prims = [(in_refs[i][6][0, bi * ng + g],) + tuple(ref[bi, :, _sl(g)] for ref in in_refs[i][:6])
                 for i, bi, g in where]
        _, vjp = jax.vjp(f, prims)
        (d_prims,) = vjp([(in_refs[i][8][bi, :, _sl(g)], dstates[i][bi * ng + g]) for i, bi, g in where])
        for (i, bi, g), d_prim in zip(where, d_prims):
            dstates[i][bi * ng + g] = d_prim[0]
            for ref, d in zip(out_refs[i], d_prim[1:]):
                ref[bi, :, _sl(g)] = d

    row = pl.BlockSpec((nb, CHUNK, w), lambda c: (0, nc - 1 - c, 0))
    seqs = lambda a: a.reshape(nb, seq, w)
    in_specs, args = [], []
    for (arrs, (s_save, t_save), dy, _, _), nt in zip(branches, nts):
        in_specs += [row] * 6 + [pl.BlockSpec((1, per, LANES, LANES), lambda c: (nc - 1 - c, 0, 0, 0)),
                                 pl.BlockSpec((1, nt, CHUNK, CHUNK), lambda c: (nc - 1 - c, 0, 0, 0)), row]
        args += [seqs(a) for a in arrs] + [s_save, t_save, seqs(dy)]
    grads = pl.pallas_call(
        body, name=name, grid=(nc,),
        in_specs=in_specs,
        out_specs=[row] * (6 * nbr),
        out_shape=[jax.ShapeDtypeStruct((nb, seq, w), F32)] * (6 * nbr),
        scratch_shapes=[pltpu.VMEM((per, LANES, LANES), F32)] * nbr,
        compiler_params=pltpu.CompilerParams(dimension_semantics=("arbitrary",), vmem_limit_bytes=VMEM_LIMIT),
    )(*args)
    return [[g.reshape(n, w) for g in grads[6 * i:6 * i + 6]] for i in range(nbr)]


def _shift_down(a, j, halo, is_start):
    tb = a.shape[0]
    rolled = pltpu.roll(a, j, 0)
    hr = jnp.where(is_start, 0.0, pltpu.roll(halo, j, 0))
    first = jnp.where(_iota((SUB, LANES), 0) < j, hr, rolled[0:SUB])
    if tb == SUB:
        return first
    return jnp.concatenate([first, rolled[SUB:]], axis=0)


def _shift_up(d, j, carry, is_end):
    tb = d.shape[0]
    up = pltpu.roll(d, tb - j, 0)
    cr = jnp.where(is_end, 0.0, pltpu.roll(carry, SUB - j, 0))
    last = jnp.where(_iota((SUB, LANES), 0) >= SUB - j, cr, up[tb - SUB:tb])
    if tb == SUB:
        return last
    return jnp.concatenate([up[:tb - SUB], last], axis=0)


def _ngroups(a):
    return a.shape[1] // LANES


def _pw_fwd(name, f, ins, shift, params, out_widths, out_dtypes, *, seq, tb):
    n = ins[0].shape[0]
    nt, tps = n // tb, seq // tb
    ni, npar = len(ins), len(params)

    def body(*refs):
        in_refs = refs[:ni]
        pos = ni
        halo_ref = None
        if shift:
            halo_ref = refs[pos]
            pos += 1
        p_refs = refs[pos:pos + npar]
        out_refs = refs[pos + npar:]
        is_start = (pl.program_id(0) % tps) == 0
        tiles = [[ref[:, _sl(g)] for g in range(_ngroups(ref))] for ref in in_refs]
        prevs = [[_shift_down(tiles[0][g], j, halo_ref[:, _sl(g)], is_start) for g in range(len(tiles[0]))]
                 for j in range(1, shift + 1)]
        pv = [[ref[:, _sl(g)] for g in range(_ngroups(ref))] for ref in p_refs]
        outs = f(tiles, prevs, pv)
        for o_ref, og in zip(out_refs, outs, strict=True):
            for g, t in enumerate(og):
                o_ref[:, _sl(g)] = t.astype(o_ref.dtype)

    in_specs = [pl.BlockSpec((tb, a.shape[1]), lambda i: (i, 0)) for a in ins]
    args = list(ins)
    if shift:
        in_specs.append(pl.BlockSpec((SUB, ins[0].shape[1]), lambda i: (jnp.maximum(i * (tb // SUB) - 1, 0), 0)))
        args.append(ins[0])
    in_specs += [pl.BlockSpec(p.shape, lambda i: (0, 0)) for p in params]
    args += list(params)
    return pl.pallas_call(
        body, name=name, grid=(nt,),
        in_specs=in_specs,
        out_specs=[pl.BlockSpec((tb, w), lambda i: (i, 0)) for w in out_widths],
        out_shape=[jax.ShapeDtypeStruct((n, w), dt) for w, dt in zip(out_widths, out_dtypes, strict=True)],
        compiler_params=pltpu.CompilerParams(dimension_semantics=("parallel",), vmem_limit_bytes=VMEM_LIMIT),
    )(*args)


def _proj_pw_fwd(name, f, h, wts, shift, params, out_widths, *, seq, tb):
    n = h.shape[0]
    nt, tps = n // tb, seq // tb
    nw, npar = len(wts), len(params)

    def body(*refs):
        h_ref = refs[0]
        w_refs = refs[1:1 + nw]
        par_refs = refs[1 + nw:1 + nw + npar]
        p_refs = refs[1 + nw + npar:1 + 2 * nw + npar]
        out_refs = refs[1 + 2 * nw + npar:len(refs) - 1]
        carry = refs[-1]
        is_start = (pl.program_id(0) % tps) == 0
        for w_ref, p_ref in zip(w_refs, p_refs, strict=True):
            p_ref[...] = lax.dot_general(h_ref[...], w_ref[...], (_NT, ((), ())), preferred_element_type=F32)
        tiles = [[ref[:, _sl(g)] for g in range(_ngroups(ref))] for ref in p_refs]
        prevs = [[_shift_down(tiles[0][g], j, carry[:, _sl(g)], is_start) for g in range(len(tiles[0]))]
                 for j in range(1, shift + 1)]
        carry[...] = p_refs[0][tb - SUB:tb, :]
        pv = [[ref[:, _sl(g)] for g in range(_ngroups(ref))] for ref in par_refs]
        outs = f(tiles, prevs, pv)
        for o_ref, og in zip(out_refs, outs, strict=True):
            for g, t in enumerate(og):
                o_ref[:, _sl(g)] = t

    widths = [w.shape[0] for w in wts] + list(out_widths)
    res = pl.pallas_call(
        body, name=name, grid=(nt,),
        in_specs=([pl.BlockSpec((tb, D_MODEL), lambda i: (i, 0))] + [pl.BlockSpec(w.shape, lambda i: (0, 0)) for w in wts]
                  + [pl.BlockSpec(p.shape, lambda i: (0, 0)) for p in params]),
        out_specs=[pl.BlockSpec((tb, w), lambda i: (i, 0)) for w in widths],
        out_shape=[jax.ShapeDtypeStruct((n, w), F32) for w in widths],
        scratch_shapes=[pltpu.VMEM((SUB, wts[0].shape[0]), F32)],
        compiler_params=pltpu.CompilerParams(dimension_semantics=("arbitrary",), vmem_limit_bytes=VMEM_LIMIT),
    )(h, *wts, *params)
    return res[:nw], res[nw:]


def _pw_bwd(name, f, ins, shift, params, douts, din_dtypes, *, seq, tb):
    n = ins[0].shape[0]
    nt, tps = n // tb, seq // tb
    ni, npar = len(ins), len(params)
    flat_douts = [d for ds in douts for d in ds]
    nd = len(flat_douts)
    w0 = ins[0].shape[1]

    def body(*refs):
        in_refs = refs[:ni]
        pos = ni
        halo_ref = None
        if shift:
            halo_ref = refs[pos]
            pos += 1
        p_refs = refs[pos:pos + npar]
        pos += npar
        d_refs = refs[pos:pos + nd]
        pos += nd
        din_refs = refs[pos:pos + ni]
        pos += ni
        dp_refs = refs[pos:pos + npar]
        pos += npar
        carry = refs[pos] if shift else None
        step = pl.program_id(0)
        tile = nt - 1 - step
        is_start = (tile % tps) == 0
        is_end = (tile % tps) == tps - 1
        tiles = [[ref[:, _sl(g)] for g in range(_ngroups(ref))] for ref in in_refs]
        prevs = [[_shift_down(tiles[0][g], j, halo_ref[:, _sl(g)], is_start) for g in range(len(tiles[0]))]
                 for j in range(1, shift + 1)]
        pv = [[ref[:, _sl(g)] for g in range(_ngroups(ref))] for ref in p_refs]
        cot, pos_d = [], 0
        for ds in douts:
            grp = d_refs[pos_d:pos_d + len(ds)]
            pos_d += len(ds)
            cot.append([functools.reduce(lambda p, q: p + q, [ref[:, _sl(g)].astype(F32) for ref in grp])
                        for g in range(_ngroups(grp[0]))])
        _, vjp = jax.vjp(f, tiles, prevs, pv)
        d_tiles, d_prevs, d_pv = vjp(cot)
        for g in range(len(tiles[0])):
            for j in range(1, shift + 1):
                d_tiles[0][g] = d_tiles[0][g] + _shift_up(d_prevs[j - 1][g], j, carry[j - 1, :, _sl(g)], is_end)
            for j in range(1, shift + 1):
                carry[j - 1, :, _sl(g)] = d_prevs[j - 1][g][0:SUB]
        for ref, dg in zip(din_refs, d_tiles, strict=True):
            for g, t in enumerate(dg):
                ref[:, _sl(g)] = t.astype(ref.dtype)

        @pl.when(step == 0)
        def _():
            for ref in dp_refs:
                ref[...] = jnp.zeros_like(ref)
        for ref, dg in zip(dp_refs, d_pv, strict=True):
            for g, t in enumerate(dg):
                ref[:, _sl(g)] += t

    rev = lambda i: (nt - 1 - i, 0)
    in_specs = [pl.BlockSpec((tb, a.shape[1]), rev) for a in ins]
    args = list(ins)
    if shift:
        in_specs.append(pl.BlockSpec((SUB, w0), lambda i: (jnp.maximum((nt - 1 - i) * (tb // SUB) - 1, 0), 0)))
        args.append(ins[0])
    in_specs += [pl.BlockSpec(p.shape, lambda i: (0, 0)) for p in params]
    args += list(params)
    in_specs += [pl.BlockSpec((tb, d.shape[1]), rev) for d in flat_douts]
    args += flat_douts
    out_specs = [pl.BlockSpec((tb, a.shape[1]), rev) for a in ins] + [pl.BlockSpec(p.shape, lambda i: (0, 0)) for p in params]
    out_shape = ([jax.ShapeDtypeStruct(a.shape, dt) for a, dt in zip(ins, din_dtypes, strict=True)]
                 + [jax.ShapeDtypeStruct(p.shape, F32) for p in params])
    res = pl.pallas_call(
        body, name=name, grid=(nt,),
        in_specs=in_specs, out_specs=out_specs, out_shape=out_shape,
        scratch_shapes=[pltpu.VMEM((shift, SUB, w0), F32)] if shift else [],
        compiler_params=pltpu.CompilerParams(dimension_semantics=("arbitrary",), vmem_limit_bytes=VMEM_LIMIT),
    )(*args)
    return res[:ni], res[ni:]


def _rwkv_prep_f(tiles, prevs, params):
    (p,), (prev,) = tiles, prevs
    mu, w0, w2p, a0, a2p, k_k, k_a = params
    xs = [p[g] + (prev[g] - p[g]) * mu[g] for g in range(13)]
    wdad = xs[12]
    tw = jnp.tanh(wdad)
    e64 = _seg_ones(64)
    r, lw, k2, v, kk, b = [], [], [], [], [], []
    for g in range(4):
        k_g = xs[4 + g]
        lo = w0[g] + _mm(tw, w2p[g], P_POINT)
        lw_g = -jnp.exp(-_softplus(-lo) - 0.5)
        a_g = _sigmoid(a0[g] + _mm(wdad, a2p[g], P_POINT))
        kkp = k_g * k_k[g]
        kk_g = kkp * lax.rsqrt(_mm(kkp * kkp, e64, P_POINT) + 1e-12)
        r.append(xs[g])
        lw.append(lw_g)
        k2.append(k_g * (1.0 + (a_g - 1.0) * k_a[g]))
        v.append(xs[8 + g])
        kk.append(kk_g)
        b.append(kk_g * a_g)
    return [r, lw, k2, v, kk, b]


def _rwkv_post_f(tiles, prevs, params):
    yrec, r, k2, v, z = tiles
    gn_w, gn_b, r_k = params
    e64 = _seg_ones(64)
    out = []
    for g in range(4):
        mean = _mm(yrec[g], e64, P_POINT) * (1.0 / 64)
        d = yrec[g] - mean
        var = _mm(d * d, e64, P_POINT) * (1.0 / 64)
        yn = d * lax.rsqrt(var + RW_GN_EPS) * gn_w[g] + gn_b[g]
        bonus = _mm(r[g] * k2[g] * r_k[g], e64, P_POINT) * v[g]
        out.append((yn + bonus) * _silu(z[g]))
    return [out]


def _gdn_prep_f(tiles, prevs, params):
    x, (ba,) = tiles
    p1, p2, p3 = prevs
    cw0, cw1, cw2, cw3, a_log, dt_bias = params
    s = [_silu(cw3[g] * x[g] + cw2[g] * p1[g] + cw1[g] * p2[g] + cw0[g] * p3[g]) for g in range(12)]
    row = _iota((LANES, LANES), 0)
    r, lw, k, vv, b = [], [], [], [], []
    for h in range(4):
        q_h, k_h, v_h = s[h], s[4 + h], s[8 + h]
        qn = q_h * lax.rsqrt(jnp.sum(q_h * q_h, axis=-1, keepdims=True) + 1e-12)
        kn = k_h * lax.rsqrt(jnp.sum(k_h * k_h, axis=-1, keepdims=True) + 1e-12)
        beta = _sigmoid(_mm(ba, (row == h).astype(F32)))
        alpha = _mm(ba, (row == 4 + h).astype(F32))
        g_h = -jnp.exp(a_log[h]) * _softplus(alpha + dt_bias[h])
        r.append(qn * (LANES ** -0.5))
        lw.append(g_h)
        k.append(kn)
        vv.append(beta * v_h)
        b.append(jnp.exp(g_h) * beta * kn)
    return [r, lw, k, vv, b]


def _gdn_post_f(tiles, prevs, params):
    o, z = tiles
    ((onw,),) = params
    out = []
    for h in range(4):
        ms = jnp.mean(o[h] * o[h], axis=-1, keepdims=True)
        out.append(o[h] * lax.rsqrt(ms + NORM_EPS) * onw * _silu(z[h]))
    return [out]


def _norm_in(x2, g_in, *, tm):
    n = x2.shape[0]

    def body(x_ref, g_ref, h_ref):
        x = x_ref[...]
        rs = lax.rsqrt(jnp.mean(x * x, axis=-1, keepdims=True) + NORM_EPS)
        h_ref[...] = (x * rs * g_ref[...]).astype(BF16)

    return pl.pallas_call(
        body, name="norm_in", grid=(n // tm,),
        in_specs=[pl.BlockSpec((tm, D_MODEL), lambda i: (i, 0)), pl.BlockSpec((1, D_MODEL), lambda i: (0, 0))],
        out_specs=pl.BlockSpec((tm, D_MODEL), lambda i: (i, 0)),
        out_shape=jax.ShapeDtypeStruct((n, D_MODEL), BF16),
        compiler_params=pltpu.CompilerParams(dimension_semantics=("parallel",), vmem_limit_bytes=VMEM_LIMIT),
    )(x2, g_in)


def _proj(name, h, wt, *, tm):
    n, ws = h.shape[0], wt.shape[0]

    def body(h_ref, w_ref, o_ref):
        o_ref[...] = lax.dot_general(h_ref[...], w_ref[...], (_NT, ((), ())), preferred_element_type=F32)

    return pl.pallas_call(
        body, name=name, grid=(n // tm,),
        in_specs=[pl.BlockSpec((tm, D_MODEL), lambda i: (i, 0)), pl.BlockSpec((ws, D_MODEL), lambda i: (0, 0))],
        out_specs=pl.BlockSpec((tm, ws), lambda i: (i, 0)),
        out_shape=jax.ShapeDtypeStruct((n, ws), F32),
        compiler_params=pltpu.CompilerParams(dimension_semantics=("parallel",), vmem_limit_bytes=VMEM_LIMIT),
    )(h, wt)


def _proj_dw(name, h, dp, *, tm):
    n, ws = dp.shape

    def body(h_ref, d_ref, o_ref):
        @pl.when(pl.program_id(0) == 0)
        def _():
            o_ref[...] = jnp.zeros_like(o_ref)
        o_ref[...] += lax.dot_general(d_ref[...], h_ref[...], (_TN, ((), ())), preferred_element_type=F32)

    return pl.pallas_call(
        body, name=name, grid=(n // tm,),
        in_specs=[pl.BlockSpec((tm, D_MODEL), lambda i: (i, 0)), pl.BlockSpec((tm, ws), lambda i: (i, 0))],
        out_specs=pl.BlockSpec((ws, D_MODEL), lambda i: (0, 0)),
        out_shape=jax.ShapeDtypeStruct((ws, D_MODEL), F32),
        compiler_params=pltpu.CompilerParams(dimension_semantics=("arbitrary",), vmem_limit_bytes=VMEM_LIMIT),
    )(h, dp)


def _proj_dx(x2, g_in, d_xo, dps, ws, *, tm):
    n = x2.shape[0]
    ns = len(dps)

    def body(*refs):
        x_ref, g_ref, dxo_ref = refs[:3]
        dp_refs = refs[3:3 + ns]
        w_refs = refs[3 + ns:3 + 2 * ns]
        dx_ref, dg_ref = refs[3 + 2 * ns:]
        dh = jnp.zeros((tm, D_MODEL), F32)
        for d_ref, w_ref in zip(dp_refs, w_refs, strict=True):
            dh = dh + jnp.dot(d_ref[...], w_ref[...], preferred_element_type=F32)
        x = x_ref[...]
        rs = lax.rsqrt(jnp.mean(x * x, axis=-1, keepdims=True) + NORM_EPS)
        xn = x * rs
        dxn = dh * g_ref[...]
        dx_ref[...] = dxo_ref[...] + rs * (dxn - xn * jnp.mean(dxn * xn, axis=-1, keepdims=True))

        @pl.when(pl.program_id(0) == 0)
        def _():
            dg_ref[...] = jnp.zeros_like(dg_ref)
        dg_ref[...] += jnp.sum(dh * xn, axis=0, keepdims=True)

    row = pl.BlockSpec((tm, D_MODEL), lambda i: (i, 0))
    return pl.pallas_call(
        body, name="proj_dx", grid=(n // tm,),
        in_specs=([row, pl.BlockSpec((1, D_MODEL), lambda i: (0, 0)), row]
                  + [pl.BlockSpec((tm, d.shape[1]), lambda i: (i, 0)) for d in dps]
                  + [pl.BlockSpec(w.shape, lambda i: (0, 0)) for w in ws]),
        out_specs=[row, pl.BlockSpec((1, D_MODEL), lambda i: (0, 0))],
        out_shape=[jax.ShapeDtypeStruct((n, D_MODEL), F32), jax.ShapeDtypeStruct((1, D_MODEL), F32)],
        compiler_params=pltpu.CompilerParams(dimension_semantics=("arbitrary",), vmem_limit_bytes=VMEM_LIMIT),
    )(x2, g_in, d_xo, *dps, *ws)


def _tail(x2, tgt2, gates, rw_post_ins, gd_post_ins, rw_post_params, gd_post_params, din_dtypes, w_a, w_b, w_o, now, *, tr):
    n = x2.shape[0]
    n_rw, n_gd = len(rw_post_ins), len(gd_post_ins)
    n_rwp, n_gdp = len(rw_post_params), len(gd_post_params)

    def body(*refs):
        x_ref, t_ref, g_ref = refs[:3]
        pos = 3
        rw_refs, gd_refs = refs[pos:pos + n_rw], refs[pos + n_rw:pos + n_rw + n_gd]
        pos += n_rw + n_gd
        rwp_refs, gdp_refs = refs[pos:pos + n_rwp], refs[pos + n_rwp:pos + n_rwp + n_gdp]
        pos += n_rwp + n_gdp
        wa_ref, wb_ref, wo_ref, now_ref = refs[pos:pos + 4]
        pos += 4
        d_rw_refs, d_gd_refs = refs[pos:pos + n_rw], refs[pos + n_rw:pos + n_rw + n_gd]
        pos += n_rw + n_gd
        dg_ref, dxo_ref, dwa_ref, dwb_ref, dwo_ref, dnow_ref, loss_ref = refs[pos:pos + 7]
        d_rwp_refs, d_gdp_refs = refs[pos + 7:pos + 7 + n_rwp], refs[pos + 7 + n_rwp:]
        groups = lambda rs: [[ref[:, _sl(g)] for g in range(_ngroups(ref))] for ref in rs]
        (ya_groups,), rw_vjp = jax.vjp(lambda t, p: _rwkv_post_f(t, [], p), groups(rw_refs), groups(rwp_refs))
        (yb_groups,), gd_vjp = jax.vjp(lambda t, p: _gdn_post_f(t, [], p), groups(gd_refs), groups(gdp_refs))
        ya16 = jnp.concatenate(ya_groups, axis=1).astype(BF16)
        yb16 = jnp.concatenate(yb_groups, axis=1).astype(BF16)
        ua = jnp.dot(ya16, wa_ref[...], preferred_element_type=F32)
        ub = jnp.dot(yb16, wb_ref[...], preferred_element_type=F32)
        ga = _sigmoid(g_ref[:, :D_MODEL])
        gb = _sigmoid(g_ref[:, D_MODEL:])
        m16 = (ga * ua + gb * ub).astype(BF16)
        xo = x_ref[...] + jnp.dot(m16, wo_ref[...], preferred_element_type=F32)
        rs = lax.rsqrt(jnp.mean(xo * xo, axis=-1, keepdims=True) + NORM_EPS)
        yn = xo * rs
        now_v = now_ref[...]
        err = yn * now_v - t_ref[...]
        dy = err * (1.0 / D_MODEL)
        dyn = dy * now_v
        dxo = rs * (dyn - yn * jnp.mean(dyn * yn, axis=-1, keepdims=True))
        dxo_ref[...] = dxo
        dxo16 = dxo.astype(BF16)
        dm = lax.dot_general(dxo16, wo_ref[...], (((1,), (1,)), ((), ())), preferred_element_type=F32)
        dua16 = (dm * ga).astype(BF16)
        dub16 = (dm * gb).astype(BF16)
        dg_ref[:, :D_MODEL] = (dm * ua * ga * (1.0 - ga)).astype(dg_ref.dtype)
        dg_ref[:, D_MODEL:] = (dm * ub * gb * (1.0 - gb)).astype(dg_ref.dtype)
        dya = lax.dot_general(dua16, wa_ref[...], (((1,), (1,)), ((), ())), preferred_element_type=F32)
        dyb = lax.dot_general(dub16, wb_ref[...], (((1,), (1,)), ((), ())), preferred_element_type=F32)
        d_rw_tiles, d_rw_pv = rw_vjp([[dya[:, _sl(g)] for g in range(RW_W // LANES)]])
        d_gd_tiles, d_gd_pv = gd_vjp([[dyb[:, _sl(g)] for g in range(GD_W // LANES)]])
        for ref, dgroups in zip(d_rw_refs + d_gd_refs, d_rw_tiles + d_gd_tiles, strict=True):
            for g, t in enumerate(dgroups):
                ref[:, _sl(g)] = t.astype(ref.dtype)

        @pl.when(pl.program_id(0) == 0)
        def _():
            for ref in (dwa_ref, dwb_ref, dwo_ref, dnow_ref, loss_ref) + d_rwp_refs + d_gdp_refs:
                ref[...] = jnp.zeros_like(ref)
        for ref, dgroups in zip(d_rwp_refs + d_gdp_refs, d_rw_pv + d_gd_pv, strict=True):
            for g, t in enumerate(dgroups):
                ref[:, _sl(g)] += t
        tn = (((0,), (0,)), ((), ()))
        dwo_ref[...] += lax.dot_general(m16, dxo16, tn, preferred_element_type=F32)
        dwa_ref[...] += lax.dot_general(ya16, dua16, tn, preferred_element_type=F32)
        dwb_ref[...] += lax.dot_general(yb16, dub16, tn, preferred_element_type=F32)
        dnow_ref[...] += jnp.sum(dy * yn, axis=0, keepdims=True)
        loss_ref[...] += (0.5 / D_MODEL) * jnp.sum(err * err)

    row = lambda w: pl.BlockSpec((tr, w), lambda i: (i, 0))
    full = lambda a: pl.BlockSpec(a.shape, lambda i: (0, 0))
    res = pl.pallas_call(
        body, name="tail", grid=(n // tr,),
        in_specs=([row(D_MODEL), row(D_MODEL), row(2 * D_MODEL)] + [row(a.shape[1]) for a in rw_post_ins + gd_post_ins]
                  + [full(p) for p in rw_post_params + gd_post_params] + [full(w_a), full(w_b), full(w_o), full(now)]),
        out_specs=([row(a.shape[1]) for a in rw_post_ins + gd_post_ins] + [row(2 * D_MODEL), row(D_MODEL),
                   pl.BlockSpec((RW_W, D_MODEL), lambda i: (0, 0)), pl.BlockSpec((GD_W, D_MODEL), lambda i: (0, 0)),
                   pl.BlockSpec((D_MODEL, D_MODEL), lambda i: (0, 0)), pl.BlockSpec((1, D_MODEL), lambda i: (0, 0)),
                   pl.BlockSpec((SUB, LANES), lambda i: (0, 0))] + [full(p) for p in rw_post_params + gd_post_params]),
        out_shape=([jax.ShapeDtypeStruct(a.shape, dt) for a, dt in zip(rw_post_ins + gd_post_ins, din_dtypes, strict=True)]
                   + [jax.ShapeDtypeStruct((n, 2 * D_MODEL), BF16), jax.ShapeDtypeStruct((n, D_MODEL), F32),
                      jax.ShapeDtypeStruct((RW_W, D_MODEL), F32), jax.ShapeDtypeStruct((GD_W, D_MODEL), F32),
                      jax.ShapeDtypeStruct((D_MODEL, D_MODEL), F32), jax.ShapeDtypeStruct((1, D_MODEL), F32),
                      jax.ShapeDtypeStruct((SUB, LANES), F32)]
                   + [jax.ShapeDtypeStruct(p.shape, F32) for p in rw_post_params + gd_post_params]),
        compiler_params=pltpu.CompilerParams(dimension_semantics=("arbitrary",), vmem_limit_bytes=VMEM_LIMIT),
    )(x2, tgt2, gates, *rw_post_ins, *gd_post_ins, *rw_post_params, *gd_post_params, w_a, w_b, w_o, now)
    ni, npar = n_rw + n_gd, n_rwp + n_gdp
    return res[:ni], res[ni:ni + 7], res[ni + 7:ni + 7 + npar]


def _exchange(name, axes, scatter, gather, place_own=True):
    ns, ng = len(scatter), len(gather)
    na = ns + ng
    gs = 2 ** len(axes)
    arrs = list(scatter) + list(gather)

    def body(*refs):
        src = refs[:na]
        dst = refs[na:2 * na]
        send_sems, recv_sems = refs[2 * na:]
        mine = {ax: lax.axis_index(ax) for ax in ("x", "y", "c")}

        def peer(k):
            co = dict(mine)
            for i, ax in enumerate(axes):
                if (k >> (len(axes) - 1 - i)) & 1:
                    co[ax] = 1 - co[ax]
            idx = 0
            for ax in axes:
                idx = 2 * idx + co[ax]
            return (co["x"], co["y"], co["c"]), idx

        _, me = peer(0)

        def copy(a, k, landing):
            dev, idx = peer(k)
            s = src[a].at[idx] if a < ns else src[a]
            return pltpu.make_async_remote_copy(src_ref=s, dst_ref=dst[a].at[idx if landing else me],
                                                send_sem=send_sems.at[a, k - 1], recv_sem=recv_sems.at[a, k - 1],
                                                device_id=dev, device_id_type=pl.DeviceIdType.MESH)

        sends = [copy(a, k, False) for a in range(na) for k in range(1, gs)]
        for cp in sends:
            cp.start()
        for a in range(na):
            for k in range(1, gs):
                copy(a, k, True).wait_recv()
        for cp in sends:
            cp.wait_send()

    out_shape = [jax.ShapeDtypeStruct(a.shape, a.dtype) for a in scatter] + \
                [jax.ShapeDtypeStruct((gs,) + a.shape, a.dtype) for a in gather]
    anyspec = pl.BlockSpec(memory_space=pl.ANY)
    lands = pl.pallas_call(
        body, name=name,
        in_specs=[anyspec] * na, out_specs=[anyspec] * na, out_shape=out_shape,
        scratch_shapes=[pltpu.SemaphoreType.DMA((na, gs - 1)), pltpu.SemaphoreType.DMA((na, gs - 1))],
    )(*arrs)
    if not place_own:
        return lands
    me = 0
    for ax in axes:
        me = 2 * me + lax.axis_index(ax)
    kept = [lax.dynamic_index_in_dim(a, me, 0, keepdims=False) for a in scatter] + list(gather)
    return [lax.dynamic_update_index_in_dim(land, mine, me, 0) for land, mine in zip(lands, kept)]


def _gather_all(name, arrs):
    na = len(arrs)

    def body(*refs):
        src = refs[:na]
        dst = refs[na:2 * na]
        send_sems, recv_sems = refs[2 * na:]
        x, y, c = lax.axis_index("x"), lax.axis_index("y"), lax.axis_index("c")
        sibling = (x, y, 1 - c)
        chips = [(1 - x, y), (x, 1 - y), (1 - x, 1 - y)]

        def copy(a, k, block, to, own=False):
            px, py, pc = block
            slot = dst[a].at[pc, 2 * px + py]
            return pltpu.make_async_remote_copy(src_ref=src[a] if own else slot, dst_ref=slot,
                                                send_sem=send_sems.at[a, k], recv_sem=recv_sems.at[a, k],
                                                device_id=to, device_id_type=pl.DeviceIdType.MESH)

        first = [copy(a, 0, (x, y, c), sibling, own=True) for a in range(na)]
        first += [copy(a, 1 + j, (x, y, c), (*chip, c), own=True) for j, chip in enumerate(chips) for a in range(na)]
        for cp in first:
            cp.start()
        passed = []
        for j, chip in enumerate(chips):
            for a in range(na):
                copy(a, 1 + j, (*chip, c), (x, y, c)).wait_recv()
                passed.append(copy(a, 4 + j, (*chip, c), sibling))
                passed[-1].start()
        for a in range(na):
            copy(a, 0, (x, y, 1 - c), (x, y, c)).wait_recv()
            for j, chip in enumerate(chips):
                copy(a, 4 + j, (*chip, 1 - c), (x, y, c)).wait_recv()
        for cp in first + passed:
            cp.wait_send()

    anyspec = pl.BlockSpec(memory_space=pl.ANY)
    lands = pl.pallas_call(
        body, name=name,
        in_specs=[anyspec] * na, out_specs=[anyspec] * na,
        out_shape=[jax.ShapeDtypeStruct((2, 4) + a.shape, a.dtype) for a in arrs],
        scratch_shapes=[pltpu.SemaphoreType.DMA((na, 7)), pltpu.SemaphoreType.DMA((na, 7))],
    )(*arrs)
    core, chip = lax.axis_index("c"), 2 * lax.axis_index("x") + lax.axis_index("y")
    zero = jnp.zeros((), jnp.int32)
    return [lax.dynamic_update_slice(land, mine[None, None], (core, chip) + (zero,) * mine.ndim)
            for land, mine in zip(lands, arrs)]


def _pair_sum(name, own, land, out_dtype):
    _, nq, r, c = own.shape
    core = lax.axis_index("c").astype(jnp.int32).reshape(1)

    def body(core_ref, own_ref, land_ref, o_ref):
        o_ref[0] = (own_ref[0, 0] + land_ref[0, 0]).astype(o_ref.dtype)

    return pl.pallas_call(
        body, name=name,
        grid_spec=pltpu.PrefetchScalarGridSpec(
            num_scalar_prefetch=1, grid=(nq,),
            in_specs=[pl.BlockSpec((1, 1, r, c), lambda i, core_ref: (core_ref[0], i, 0, 0)),
                      pl.BlockSpec((1, 1, r, c), lambda i, core_ref: (1 - core_ref[0], i, 0, 0))],
            out_specs=pl.BlockSpec((1, r, c), lambda i, core_ref: (i, 0, 0))),
        out_shape=jax.ShapeDtypeStruct((nq, r, c), out_dtype),
        compiler_params=pltpu.CompilerParams(dimension_semantics=("parallel",), vmem_limit_bytes=VMEM_LIMIT),
    )(core, own, land)


def _adam(name, land, w, m, v):
    r, c = w.shape
    nslot = land.shape[0]
    tr = 256 if (r % 256 == 0 and r > 256) else r
    tc = 256 if (tr == r and r > 256 and c % 256 == 0) else c

    def body(l_ref, w_ref, m_ref, v_ref, g_out, d_out, m_out, v_out):
        g = l_ref[0].astype(F32)
        for s in range(1, nslot):
            g = g + l_ref[s].astype(F32)
        g_out[...] = g
        d_out[...], m_out[...], v_out[...] = _adam_math(g, w_ref[...], m_ref[...], v_ref[...])

    blk = pl.BlockSpec((tr, tc), lambda i: (i * tc // c, i % (c // tc)))
    return pl.pallas_call(
        body, name=name, grid=((r // tr) * (c // tc),),
        in_specs=[pl.BlockSpec((nslot, tr, tc), lambda i: (0, i * tc // c, i % (c // tc))), blk, blk, blk],
        out_specs=[blk] * 4,
        out_shape=[jax.ShapeDtypeStruct((r, c), F32)] * 4,
        compiler_params=pltpu.CompilerParams(dimension_semantics=("parallel",), vmem_limit_bytes=VMEM_LIMIT),
    )(land, w, m, v)


def _adam_math(g, w, m, v):
    c1 = 1.0 / (1.0 - ADAM_B1 ** ADAM_STEP)
    c2 = 1.0 / (1.0 - ADAM_B2 ** ADAM_STEP)
    m_new = ADAM_B1 * m + (1.0 - ADAM_B1) * g
    v_new = ADAM_B2 * v + (1.0 - ADAM_B2) * (g * g)
    return -ADAM_LR * ((m_new * c1) / (jnp.sqrt(v_new * c2) + ADAM_EPS) + ADAM_WD * w), m_new, v_new


def _adam_small(land, ws, ms, vs):
    npar = len(ws)
    nslot = land.shape[0]

    def body(*refs):
        l_ref = refs[0]
        w_refs, m_refs, v_refs = refs[1:1 + npar], refs[1 + npar:1 + 2 * npar], refs[1 + 2 * npar:1 + 3 * npar]
        outs = refs[1 + 3 * npar:1 + 7 * npar]
        loss_ref, g_rows = refs[1 + 7 * npar], refs[2 + 7 * npar]
        g = l_ref[0]
        for s in range(1, nslot):
            g = g + l_ref[s]
        g_rows[...] = g
        row = 0
        for i, (_, size) in enumerate(_SMALL):
            for j in range(-(-size // LANES)):
                width = min(LANES, size - j * LANES)
                cols = slice(j * LANES, j * LANES + width)
                g_ij = g_rows[row:row + 1, 0:width]
                delta, m_new, v_new = _adam_math(g_ij, w_refs[i][:, cols], m_refs[i][:, cols], v_refs[i][:, cols])
                for ref, val in zip(outs[4 * i:4 * i + 4], (g_ij, delta, m_new, v_new)):
                    ref[:, cols] = val
                row += 1
        loss_ref[...] = g_rows[row:row + 1, :]

    full = lambda a: pl.BlockSpec(a.shape, lambda: (0,) * a.ndim)
    res = pl.pallas_call(
        body, name="adam_small",
        in_specs=[full(land)] + [full(a) for a in list(ws) + list(ms) + list(vs)],
        out_specs=[full(w) for w in ws for _ in range(4)] + [pl.BlockSpec((1, LANES), lambda: (0, 0))],
        out_shape=[jax.ShapeDtypeStruct(w.shape, F32) for w in ws for _ in range(4)] + [jax.ShapeDtypeStruct((1, LANES), F32)],
        scratch_shapes=[pltpu.VMEM(land.shape[1:], F32)],
    )(land, *ws, *ms, *vs)
    return [res[4 * i:4 * i + 4] for i in range(npar)], res[4 * npar]


_SMALL = (("norm_in_w", 1024), ("rw_mu", 1664), ("rw_w0", 512), ("rw_a0", 512), ("rw_k_k", 512), ("rw_k_a", 512),
          ("rw_r_k", 512), ("rw_gn_w", 512), ("rw_gn_b", 512), ("gd_A_log", 4), ("gd_dt_bias", 4), ("gd_o_norm_w", 128),
          ("norm_out_w", 1024))
_SMALL_ROWS = 64


def _pack_small(vals, loss_row):
    rows = []
    for (_, size), a in zip(_SMALL, vals, strict=True):
        flat = a.reshape(-1).astype(F32)
        pad = (-size) % LANES
        if pad:
            flat = jnp.concatenate([flat, jnp.zeros((pad,), F32)])
        rows.append(flat.reshape(-1, LANES))
    rows.append(loss_row)
    used = sum(r.shape[0] for r in rows)
    rows.append(jnp.zeros((_SMALL_ROWS - used, LANES), F32))
    return jnp.concatenate(rows, axis=0)


def kernel(x, norm_in_w, w_in, rw_mu, rw_w0, rw_w2, rw_a0, rw_a2, rw_k_k, rw_k_a, rw_r_k, rw_gn_w, rw_gn_b, gd_conv_w, gd_A_log, gd_dt_bias, gd_o_norm_w, w_branch_a, w_branch_b, w_out, norm_out_w, loss_target, m_norm_in_w, m_w_in, m_rw_mu, m_rw_w0, m_rw_w2, m_rw_a0, m_rw_a2, m_rw_k_k, m_rw_k_a, m_rw_r_k, m_rw_gn_w, m_rw_gn_b, m_gd_conv_w, m_gd_A_log, m_gd_dt_bias, m_gd_o_norm_w, m_w_branch_a, m_w_branch_b, m_w_out, m_norm_out_w, v_norm_in_w, v_w_in, v_rw_mu, v_rw_w0, v_rw_w2, v_rw_a0, v_rw_a2, v_rw_k_k, v_rw_k_a, v_rw_r_k, v_rw_gn_w, v_rw_gn_b, v_gd_conv_w, v_gd_A_log, v_gd_dt_bias, v_gd_o_norm_w, v_w_branch_a, v_w_branch_b, v_w_out, v_norm_out_w):
    nb, seq, _ = x.shape
    n = nb * seq
    tm = min(1024, n)
    tb = min(512, seq)
    x2 = x.reshape(n, D_MODEL)
    tgt2 = loss_target.reshape(n, D_MODEL)
    cols = w_in.shape[2]
    in_cols = cols * N_DEV

    wt_own, mt_own, vt_own = w_in[0].T, m_w_in[0].T, v_w_in[0].T
    g_win, g_w2, g_a2, g_conv = _gather_all("gather_weights", [wt_own.astype(BF16), rw_w2[0], rw_a2[0], gd_conv_w[0]])
    late_w = [w_branch_a[0].astype(BF16), w_branch_b[0].astype(BF16), w_out[0].astype(BF16)]
    unshard_rows = lambda a: jnp.transpose(a, (1, 0, 2, 3)).reshape(N_DEV * a.shape[2], a.shape[3])
    unshard_cols = lambda a: jnp.transpose(a, (2, 1, 0, 3)).reshape(a.shape[2], N_DEV * a.shape[3])
    wt_full = unshard_rows(g_win)
    seg_bounds = ((0, 1664), (1664, 2176), (2176, 3712), (3712, 4224), (4232, in_cols))
    w_rw, w_zrw, w_qkv, w_zgd, w_gates = [wt_full[a:b] for a, b in seg_bounds]
    w_ba = jnp.concatenate([wt_full[4224:4232], jnp.zeros((LANES - 8, D_MODEL), BF16)], axis=0)
    w2_full, a2_full = unshard_cols(g_w2), unshard_cols(g_a2)
    zeros64 = jnp.zeros((64, RW_W), F32)
    w2p = jnp.concatenate([w2_full, zeros64], axis=0)
    a2p = jnp.concatenate([zeros64, a2_full], axis=0)
    conv_full = unshard_cols(g_conv)
    conv_rows = [conv_full[i:i + 1] for i in range(4)]
    a_log_bc = jnp.repeat(gd_A_log, LANES, axis=1)
    dt_bias_bc = jnp.repeat(gd_dt_bias, LANES, axis=1)
    r_k_flat = rw_r_k.reshape(1, RW_W)
    now2 = norm_out_w.reshape(1, D_MODEL)

    h = _norm_in(x2, norm_in_w, tm=tm)
    p_zrw = _proj("proj_zrw", h, w_zrw, tm=tm)
    p_zgd = _proj("proj_zgd", h, w_zgd, tm=tm)
    p_gates = _proj("proj_gates", h, w_gates, tm=tm)
    rw_params = [rw_mu, rw_w0, w2p, rw_a0, a2p, rw_k_k, rw_k_a]
    (p_rw,), (r_a, lw_a, k_a, v_a, kk_a, b_a) = _proj_pw_fwd("proj_rwkv_prep", _rwkv_prep_f, h, [w_rw], 1, rw_params,
                                                             [RW_W] * 6, seq=seq, tb=tb)
    gd_params = conv_rows + [a_log_bc, dt_bias_bc]
    (p_qkv, p_ba), (r_b, lw_b, k_b, v_b, b_b) = _proj_pw_fwd("proj_gdn_prep", _gdn_prep_f, h, [w_qkv, w_ba], 3, gd_params,
                                                             [GD_W] * 5, seq=seq, tb=tb)
    rw_six, gd_six = (r_a, lw_a, k_a, v_a, kk_a, b_a), (r_b, lw_b, k_b, v_b, k_b, b_b)
    ((y_rec, s_a), (o_rec, s_b)), (g_wa, g_wb, g_wo) = _rec_fwd(
        "rec", [(rw_six, 2, False), (gd_six, 1, True)], seq=seq, late=late_w)
    wa_full, wb_full, wo_full = unshard_cols(g_wa), unshard_cols(g_wb), unshard_rows(g_wo)
    post_params = [rw_gn_w, rw_gn_b, r_k_flat]
    ((d_yrec, dr_p, dk_p, dv_p, d_zrw, d_o, d_zgd), (d_gates, d_xo, dwa, dwb, dwo, d_now, loss_acc),
     (*d_post_params, d_onw)) = _tail(
        x2, tgt2, p_gates, [y_rec, r_a, k_a, v_a, p_zrw], [o_rec, p_zgd], post_params, [gd_o_norm_w],
        [F32, F32, F32, F32, BF16, F32, BF16], wa_full, wb_full, wo_full, now2, tr=min(256, n))

    (dr_a, dlw_a, dk_a, dv_a, dkk_a, db_a), (dr_b, dlw_b, dk_b, dv_b, dkk_b, db_b) = _rec_bwd(
        "rec_bwd", [(rw_six, s_a, d_yrec, 2, False), (gd_six, s_b, d_o, 1, True)], seq=seq)
    (d_qkv, d_ba), d_gd_params = _pw_bwd("gdn_prep_bwd", _gdn_prep_f, [p_qkv, p_ba], 3, gd_params,
                                         [[dr_b], [dlw_b], [dk_b, dkk_b], [dv_b], [db_b]], [BF16, BF16], seq=seq, tb=tb)
    (d_prw,), d_rw_params = _pw_bwd("rwkv_prep_bwd", _rwkv_prep_f, [p_rw], 1, rw_params,
                                    [[dr_a, dr_p], [dlw_a], [dk_a, dk_p], [dv_a, dv_p], [dkk_a], [db_a]], [BF16],
                                    seq=seq, tb=tb)

    dps = [d_prw, d_zrw, d_qkv, d_zgd, d_ba, d_gates]
    wsegs = [w_rw, w_zrw, w_qkv, w_zgd, w_ba, w_gates]
    dx2, d_gin = _proj_dx(x2, norm_in_w, d_xo, dps, wsegs, tm=min(256, n))
    dw_rw = _proj_dw("dw_rw", h, d_prw, tm=tm)
    dw_zrw = _proj_dw("dw_zrw", h, d_zrw, tm=tm)
    dw_qkv = _proj_dw("dw_qkv", h, d_qkv, tm=tm)
    dw_zgd = _proj_dw("dw_zgd", h, d_zgd, tm=tm)
    dw_ba = _proj_dw("dw_ba", h, d_ba, tm=tm)
    dw_gates = _proj_dw("dw_gates", h, d_gates, tm=tm)
    dwt_in_full = jnp.concatenate([dw_rw, dw_zrw, dw_qkv, dw_zgd, dw_ba[:8], dw_gates], axis=0)

    shard_cols = lambda a: jnp.transpose(a.reshape(a.shape[0], 4, 2, a.shape[1] // N_DEV), (2, 1, 0, 3))
    shard_rows = lambda a: jnp.transpose(a.reshape(4, 2, a.shape[0] // N_DEV, a.shape[1]), (1, 0, 2, 3))
    d_mu, d_w0, d_w2p, d_a0, d_a2p, d_kk_, d_ka_ = d_rw_params
    d_gnw, d_gnb, d_rk = d_post_params
    d_conv = jnp.concatenate(d_gd_params[:4], axis=0)
    d_alog = d_gd_params[4].reshape(4, LANES).sum(axis=1).reshape(1, 4)
    d_dtb = d_gd_params[5].reshape(4, LANES).sum(axis=1).reshape(1, 4)
    scat = [shard_rows(dwt_in_full), shard_cols(d_w2p[:64]), shard_cols(d_a2p[64:]), shard_cols(d_conv),
            shard_cols(dwa), shard_cols(dwb), shard_rows(dwo)]
    small_g = _pack_small([d_gin, d_mu, d_w0, d_a0, d_kk_, d_ka_, d_rk, d_gnw, d_gnb, d_alog, d_dtb, d_onw, d_now],
                          loss_acc[0:1])
    scat.append(jnp.stack([small_g, small_g])[:, None])
    pair = _exchange("reduce_cores", ("c",), scat, [], place_own=False)
    part = [_pair_sum("pair_sum_%d" % i, own, got, BF16 if i < 7 else F32)
            for i, (own, got) in enumerate(zip(scat, pair))]
    lands = _exchange("reduce_chips", ("x", "y"), part[:7], [part[7][0]])

    small_w = [norm_in_w, rw_mu, rw_w0, rw_a0, rw_k_k, rw_k_a, rw_r_k, rw_gn_w, rw_gn_b, gd_A_log, gd_dt_bias, gd_o_norm_w, norm_out_w]
    small_m = [m_norm_in_w, m_rw_mu, m_rw_w0, m_rw_a0, m_rw_k_k, m_rw_k_a, m_rw_r_k, m_rw_gn_w, m_rw_gn_b, m_gd_A_log, m_gd_dt_bias, m_gd_o_norm_w, m_norm_out_w]
    small_v = [v_norm_in_w, v_rw_mu, v_rw_w0, v_rw_a0, v_rw_k_k, v_rw_k_a, v_rw_r_k, v_rw_gn_w, v_rw_gn_b, v_gd_A_log, v_gd_dt_bias, v_gd_o_norm_w, v_norm_out_w]
    flat = lambda arrs: [a.reshape(1, -1) for a in arrs]
    sm, loss_row = _adam_small(lands[7], flat(small_w), flat(small_m), flat(small_v))
    sm_g, sm_d, sm_m, sm_v = [{nm: res[i].reshape(w.shape) for (nm, _), res, w in zip(_SMALL, sm, small_w)}
                              for i in range(4)]

    big = {"w_in": [o.T[None] for o in _adam("adam_w_in", lands[0], wt_own, mt_own, vt_own)]}
    for nm, land, w, m, v in (("rw_w2", lands[1], rw_w2, m_rw_w2, v_rw_w2),
                              ("rw_a2", lands[2], rw_a2, m_rw_a2, v_rw_a2),
                              ("gd_conv_w", lands[3], gd_conv_w, m_gd_conv_w, v_gd_conv_w),
                              ("w_branch_a", lands[4], w_branch_a, m_w_branch_a, v_w_branch_a),
                              ("w_branch_b", lands[5], w_branch_b, m_w_branch_b, v_w_branch_b),
                              ("w_out", lands[6], w_out, m_w_out, v_w_out)):
        big[nm] = [o.reshape(w.shape) for o in _adam("adam_" + nm, land, w[0], m[0], v[0])]

    order = ["norm_in_w", "w_in", "rw_mu", "rw_w0", "rw_w2", "rw_a0", "rw_a2", "rw_k_k", "rw_k_a", "rw_r_k", "rw_gn_w",
             "rw_gn_b", "gd_conv_w", "gd_A_log", "gd_dt_bias", "gd_o_norm_w", "w_branch_a", "w_branch_b", "w_out", "norm_out_w"]
    pick = lambda nm, i: big[nm][i] if nm in big else (sm_g, sm_d, sm_m, sm_v)[i][nm]
    loss = loss_row[0, 0]
    grad_x = dx2.reshape(x.shape)
    return (loss, grad_x, *[pick(nm, 0) for nm in order], *[pick(nm, 1) for nm in order],
            *[pick(nm, 2) for nm in order], *[pick(nm, 3) for nm in order])
```

```python
import functools

import jax
import jax.numpy as jnp
from jax import lax
from jax.experimental import pallas as pl
from jax.experimental.pallas import tpu as pltpu

F32 = jnp.float32
BF16 = jnp.bfloat16

LANES = 128
SUB = 8
CHUNK = 64
N_DEV = 8
VMEM_LIMIT = 56 * 1024 * 1024

D_MODEL = 1024
RW_W = 512
GD_W = 512
RW_SHIFT = 1664
NORM_EPS = 1e-6
RW_GN_EPS = 64 * 1e-5
ADAM_LR, ADAM_B1, ADAM_B2, ADAM_EPS, ADAM_WD, ADAM_STEP = 0.001, 0.9, 0.999, 1e-8, 0.01, 10


_NN, _NT, _TN = ((1,), (0,)), ((1,), (1,)), ((0,), (0,))


def _dot(a, b, dims, passes):
    precision = lax.Precision.HIGH if passes == 3 else lax.Precision.DEFAULT
    return lax.dot_general(a, b, (dims, ((), ())), precision=precision, preferred_element_type=F32)


def _mm(a, b, passes=3):
    return _dot(a, b, _NN, passes)


def _mm_nt(a, b, passes=3):
    return _dot(a, b, _NT, passes)


def _mm_tn(a, b, passes=3):
    return _dot(a, b, _TN, passes)


P_SCORE = 1
P_INV = 1
P_STATE = 1
P_APPLY = 1
P_UPDATE = 1
P_POINT = 1


def _stack_rows(blocks):
    return jnp.concatenate(blocks, axis=0)


def _split_rows(x, n):
    r = x.shape[0] // n

    @jax.custom_vjp
    def split(x):
        return tuple(x[i * r:(i + 1) * r] for i in range(n))

    split.defvjp(lambda x: (split(x), None), lambda _, gs: (jnp.concatenate(gs, axis=0),))
    return split(x)


def _iota(shape, d):
    return lax.broadcasted_iota(jnp.int32, shape, d)


def _sigmoid(x):
    return 0.5 * (jnp.tanh(0.5 * x) + 1.0)


def _silu(x):
    return x * _sigmoid(x)


def _softplus(x):
    return jnp.maximum(x, 0.0) + jnp.log(1.0 + jnp.exp(-jnp.abs(x)))


def _seg_ones(seg):
    return ((_iota((LANES, LANES), 0) // seg) == (_iota((LANES, LANES), 1) // seg)).astype(F32)


def _sl(g):
    return slice(g * LANES, (g + 1) * LANES)


@jax.custom_vjp
def _tri_inverse(ms):
    return _tri_inverse_chain(ms)


def _tri_inverse_bwd(ts, dts):
    return ([-_mm_nt(_mm_tn(t, dt, P_INV), t, P_INV) for t, dt in zip(ts, dts)],)


def _tri_inverse_chain(ms):
    c = CHUNK
    ri, ci = _iota((c, c), 0), _iota((c, c), 1)
    eye = (ri == ci).astype(F32)
    d16 = (ri // 16) == (ci // 16)
    d32 = (ri // 32) == (ci // 32)
    ps = [jnp.where(d16, -m, 0.0) for m in ms]
    ts = [eye + p for p in ps]
    for _ in range(3):
        ps = [_mm(p, p, P_INV) for p in ps]
        ts = [_mm(t, eye + p, P_INV) for t, p in zip(ts, ps)]
    for off_diagonal in (d32 & (~d16), ~d32):
        tq = [_mm(t, jnp.where(off_diagonal, m, 0.0), P_INV) for t, m in zip(ts, ms)]
        ts = [t - _mm(a, t, P_INV) for t, a in zip(ts, tq)]
    return ts


_tri_inverse.defvjp(lambda ms: (lambda ts: (ts, ts))(_tri_inverse_chain(ms)), _tri_inverse_bwd)


@jax.custom_vjp
def _known_inverse(ms, ts):
    return ts


_known_inverse.defvjp(lambda ms, ts: (ts, ts),
                      lambda ts, dts: (_tri_inverse_bwd(ts, dts)[0], [jnp.zeros_like(t) for t in ts]))


def _scan_rows(x, reverse):
    c = x.shape[0]
    row = _iota(x.shape, 0)
    k = 1
    while k < c:
        if reverse:
            x = x + jnp.where(row < c - k, pltpu.roll(x, c - k, 0), 0.0)
        else:
            x = x + jnp.where(row >= k, pltpu.roll(x, k, 0), 0.0)
        k *= 2
    return x


@jax.custom_vjp
def _running_sum(x):
    return _scan_rows(x, False)


_running_sum.defvjp(lambda x: (_scan_rows(x, False), None), lambda _, g: (_scan_rows(g, True),))


def _chunk_fwd(prims, *, nsub=None, scalar_decay=None, kinds=None, inverses=None):
    c = CHUNK
    ng = len(prims)
    kinds = kinds if kinds is not None else [(nsub, scalar_decay)] * ng
    s0s, rs, lws, ks, vs, kks, bs = [list(t) for t in zip(*prims)]
    ri, ci = _iota((c, c), 0), _iota((c, c), 1)
    incl = ri >= ci
    strict = ri > ci
    lane = _iota((1, LANES), 1)
    heads = [n for n, _ in kinds]
    scalar = [sc for _, sc in kinds]
    masks = [[((lane // (LANES // n)) == s).astype(F32) for s in range(n)] if n > 1 else [1.0] for n in heads]
    cws = [_running_sum(lw) for lw in lws]
    cwxs = [cw - lw for cw, lw in zip(cws, lws)]
    ends = [cw[c - 1:c, :] for cw in cws]
    kkds = [kk * jnp.exp(cwx) for kk, cwx in zip(kks, cwxs)]
    rds = [r * jnp.exp(cw) for r, cw in zip(rs, cws)]
    kends = [k * jnp.exp(e - cw) for k, e, cw in zip(ks, ends, cws)]
    bends = [b * jnp.exp(e - cw) for b, e, cw in zip(bs, ends, cws)]
    state_terms = [_split_rows(_mm_nt(_stack_rows([kkd, rd]), s0, P_STATE), 2) for kkd, rd, s0 in zip(kkds, rds, s0s)]
    w0s, y0s = [t[0] for t in state_terms], [t[1] for t in state_terms]
    chains = [(g, s) for g in range(ng) for s in range(heads[g])]
    rows = [_split_rows(jnp.transpose(cw), LANES // c)[0] if sc else None for cw, sc in zip(cws, scalar)]
    dxs = [jnp.where(strict, jnp.exp(jnp.minimum(cwx[:, :c] - row, 0.0)), 0.0) if sc else None
           for cwx, row, sc in zip(cwxs, rows, scalar)]
    dis = [jnp.where(incl, jnp.exp(jnp.minimum(cw[:, :c] - row, 0.0)), 0.0) if sc else None
           for cw, row, sc in zip(cws, rows, scalar)]
    lefts = [_stack_rows([a * m for m in ms] + [q * m for m in ms])
             for a, q, ms in zip([kk if sc else kkd for kk, kkd, sc in zip(kks, kkds, scalar)],
                                 [r if sc else rd for r, rd, sc in zip(rs, rds, scalar)], masks)]
    rights_b = [b if sc else b * jnp.exp(-cw) for b, cw, sc in zip(bs, cws, scalar)]
    rights_k = [k if sc else k * jnp.exp(-cw) for k, cw, sc in zip(ks, cws, scalar)]
    on_b = [_split_rows(_mm_nt(left, right, P_SCORE), 2 * n) for left, right, n in zip(lefts, rights_b, heads)]
    on_k = [_split_rows(_mm_nt(left, right, P_SCORE), 2 * n) for left, right, n in zip(lefts, rights_k, heads)]
    lower = lambda x, g: x * dxs[g] if scalar[g] else jnp.where(strict, x, 0.0)
    lower_incl = lambda x, g: x * dis[g] if scalar[g] else jnp.where(incl, x, 0.0)
    m_b = [lower(on_b[g][s], g) for g, s in chains]
    m_k = [lower(on_k[g][s], g) for g, s in chains]
    n_k = [lower_incl(on_k[g][heads[g] + s], g) for g, s in chains]
    n_b = [lower_incl(on_b[g][heads[g] + s], g) for g, s in chains]
    t_inv = _tri_inverse(m_b) if inverses is None else _known_inverse(m_b, inverses)
    on_v = [_split_rows(_mm(_stack_rows([mk, nk]), vs[g], P_APPLY), 2) for (g, s), mk, nk in zip(chains, m_k, n_k)]
    sa_c = [_mm(t, w0s[g] + mv[0], P_APPLY) for (g, s), t, mv in zip(chains, t_inv, on_v)]
    y_c = [y0s[g] + mv[1] - _mm(nb, sa, P_APPLY) for (g, s), mv, nb, sa in zip(chains, on_v, n_b, sa_c)]
    first = [sum(heads[:g]) for g in range(ng)]
    per_group = lambda xs: [functools.reduce(lambda p, q: p + q, [xs[first[g] + s] * masks[g][s] for s in range(heads[g])])
                            for g in range(ng)]
    sas, ys = per_group(sa_c), per_group(y_c)
    s_ends = [s0 * jnp.exp(e) + _mm_tn(_stack_rows([v, -sa]), _stack_rows([kend, bend]), P_UPDATE)
              for s0, e, v, kend, sa, bend in zip(s0s, ends, vs, kends, sas, bends)]
    row_head = lambda n: _iota((LANES, LANES), 0) // (LANES // n)
    col_head = lambda n: _iota((LANES, LANES), 1) // (LANES // n)
    s_ends = [jnp.where(row_head(n) == col_head(n), s_end, 0.0) if n > 1 else s_end for s_end, n in zip(s_ends, heads)]
    return list(zip(ys, s_ends)), t_inv


def _rec_fwd(name, branches, *, seq, late=()):
    n, w = branches[0][0][0].shape
    ng = w // LANES
    nc = seq // CHUNK
    nb = n // seq
    nbr = len(branches)
    nl = len(late)
    per = nb * ng
    kinds = [(heads, scalar) for _, heads, scalar in branches for _ in range(per)]
    nts = [per * heads for _, heads, _ in branches]

    def body(*refs):
        in_refs = [refs[6 * i:6 * i + 6] for i in range(nbr)]
        late_src = refs[6 * nbr:6 * nbr + nl]
        pos = 6 * nbr + nl
        out_refs = [refs[pos + 3 * i:pos + 3 * i + 3] for i in range(nbr)]
        late_dst = refs[pos + 3 * nbr:pos + 3 * nbr + nl]
        states = refs[pos + 3 * nbr + nl:pos + 4 * nbr + nl]

        def late_copy(a, k, landing):
            send_sems, recv_sems = refs[pos + 4 * nbr + nl:]
            x, y, c = lax.axis_index("x"), lax.axis_index("y"), lax.axis_index("c")
            px, py, pc = (1 - x if k & 4 else x), (1 - y if k & 2 else y), (1 - c if k & 1 else c)
            slot = late_dst[a].at[pc, 2 * px + py] if landing else late_dst[a].at[c, 2 * x + y]
            return pltpu.make_async_remote_copy(src_ref=late_src[a], dst_ref=slot, send_sem=send_sems.at[a, k - 1],
                                                recv_sem=recv_sems.at[a, k - 1], device_id=(px, py, pc),
                                                device_id_type=pl.DeviceIdType.MESH)

        @pl.when(pl.program_id(0) == 0)
        def _():
            for state in states:
                state[...] = jnp.zeros_like(state)
            for a in range(nl):
                for k in range(1, N_DEV):
                    late_copy(a, k, False).start()
        where = [(i, bi, g) for i in range(nbr) for bi in range(nb) for g in range(ng)]
        prims = [(states[i][bi * ng + g],) + tuple(ref[bi, :, _sl(g)] for ref in in_refs[i]) for i, bi, g in where]
        outs, t_inv = _chunk_fwd(prims, kinds=kinds)
        for (i, bi, g), prim, (y, s_end) in zip(where, prims, outs):
            y_ref, s_ref, _ = out_refs[i]
            s_ref[0, bi * ng + g] = prim[0]
            y_ref[bi, :, _sl(g)] = y
            states[i][bi * ng + g] = s_end
        first = 0
        for i in range(nbr):
            for j in range(nts[i]):
                out_refs[i][2][0, j] = t_inv[first + j]
            first += nts[i]

        @pl.when(pl.program_id(0) == nc - 1)
        def _():
            for a in range(nl):
                for k in range(1, N_DEV):
                    late_copy(a, k, True).wait_recv()
            for a in range(nl):
                for k in range(1, N_DEV):
                    late_copy(a, k, False).wait_send()

    row = pl.BlockSpec((nb, CHUNK, w), lambda c: (0, c, 0))
    anyspec = pl.BlockSpec(memory_space=pl.ANY)
    seqs = lambda a: a.reshape(nb, seq, w)
    res = pl.pallas_call(
        body, name=name, grid=(nc,),
        in_specs=[row] * (6 * nbr) + [anyspec] * nl,
        out_specs=[spec for nt in nts for spec in (row, pl.BlockSpec((1, per, LANES, LANES), lambda c: (c, 0, 0, 0)),
                                                   pl.BlockSpec((1, nt, CHUNK, CHUNK), lambda c: (c, 0, 0, 0)))]
                  + [anyspec] * nl,
        out_shape=[shp for nt in nts for shp in (jax.ShapeDtypeStruct((nb, seq, w), F32),
                                                 jax.ShapeDtypeStruct((nc, per, LANES, LANES), F32),
                                                 jax.ShapeDtypeStruct((nc, nt, CHUNK, CHUNK), F32))]
                  + [jax.ShapeDtypeStruct((2, 4) + a.shape, a.dtype) for a in late],
        scratch_shapes=[pltpu.VMEM((per, LANES, LANES), F32)] * nbr
                       + ([pltpu.SemaphoreType.DMA((nl, N_DEV - 1)), pltpu.SemaphoreType.DMA((nl, N_DEV - 1))] if nl else []),
        compiler_params=pltpu.CompilerParams(dimension_semantics=("arbitrary",), vmem_limit_bytes=VMEM_LIMIT),
    )(*[seqs(a) for arrs, _, _ in branches for a in arrs], *late)
    core, chip = lax.axis_index("c"), 2 * lax.axis_index("x") + lax.axis_index("y")
    zero = jnp.zeros((), jnp.int32)
    landed = [lax.dynamic_update_slice(land, mine[None, None], (core, chip) + (zero,) * mine.ndim)
              for land, mine in zip(res[3 * nbr:], late)]
    return [(res[3 * i].reshape(n, w), (res[3 * i + 1], res[3 * i + 2])) for i in range(nbr)], landed


def _rec_bwd(name, branches, *, seq, late=()):
    n, w = branches[0][0][0].shape
    ng = w // LANES
    nc = seq // CHUNK
    nb = n // seq
    nbr = len(branches)
    nl = len(late)
    per = nb * ng
    kinds = [(heads, scalar) for _, _, _, heads, scalar in branches for _ in range(per)]
    nts = [per * heads for _, _, _, heads, _ in branches]

    def body(*refs):
        in_refs = [refs[9 * i:9 * i + 9] for i in range(nbr)]
        late_src = refs[9 * nbr:9 * nbr + nl]
        base = 9 * nbr + nl
        out_refs = [refs[base + 6 * i:base + 6 * i + 6] for i in range(nbr)]
        late_dst = refs[base + 6 * nbr:base + 6 * nbr + nl]
        dstates = refs[base + 6 * nbr + nl:base + 7 * nbr + nl]

        def late_copy(a, k, landing):
            send_sems, recv_sems = refs[base + 7 * nbr + nl:]
            x, y, c = lax.axis_index("x"), lax.axis_index("y"), lax.axis_index("c")
            px, py, pc = (1 - x if k & 4 else x), (1 - y if k & 2 else y), (1 - c if k & 1 else c)
            slot = late_dst[a].at[pc, 2 * px + py] if landing else late_dst[a].at[c, 2 * x + y]
            return pltpu.make_async_remote_copy(src_ref=late_src[a].at[pc, 2 * px + py], dst_ref=slot,
                                                send_sem=send_sems.at[a, k - 1], recv_sem=recv_sems.at[a, k - 1],
                                                device_id=(px, py, pc), device_id_type=pl.DeviceIdType.MESH)

        @pl.when(pl.program_id(0) == 0)
        def _():
            for dstate in dstates:
                dstate[...] = jnp.zeros_like(dstate)
            for a in range(nl):
                for k in range(1, N_DEV):
                    late_copy(a, k, False).start()
        where = [(i, bi, g) for i in range(nbr) for bi in range(nb) for g in range(ng)]
        inverses = [in_refs[i][7][0, j] for i in range(nbr) for j in range(nts[i])]
        f = lambda p: _chunk_fwd(p, kinds=kinds, inverses=inverses)[0]
        prims = [(in_refs[i][6][0, bi * ng + g],) + tuple(ref[bi, :, _sl(g)] for ref in in_refs[i][:6])
                 for i, bi, g in where]
        _, vjp = jax.vjp(f, prims)
        (d_prims,) = vjp([(in_refs[i][8][bi, :, _sl(g)], dstates[i][bi * ng + g]) for i, bi, g in where])
        for (i, bi, g), d_prim in zip(where, d_prims):
            dstates[i][bi * ng + g] = d_prim[0]
            for ref, d in zip(out_refs[i], d_prim[1:]):
                ref[bi, :, _sl(g)] = d

        @pl.when(pl.program_id(0) == nc - 1)
        def _():
            for a in range(nl):
                for k in range(1, N_DEV):
                    late_copy(a, k, True).wait_recv()
            for a in range(nl):
                for k in range(1, N_DEV):
                    late_copy(a, k, False).wait_send()

    row = pl.BlockSpec((nb, CHUNK, w), lambda c: (0, nc - 1 - c, 0))
    anyspec = pl.BlockSpec(memory_space=pl.ANY)
    seqs = lambda a: a.reshape(nb, seq, w)
    in_specs, args = [], []
    for (arrs, (s_save, t_save), dy, _, _), nt in zip(branches, nts):
        in_specs += [row] * 6 + [pl.BlockSpec((1, per, LANES, LANES), lambda c: (nc - 1 - c, 0, 0, 0)),
                                 pl.BlockSpec((1, nt, CHUNK, CHUNK), lambda c: (nc - 1 - c, 0, 0, 0)), row]
        args += [seqs(a) for a in arrs] + [s_save, t_save, seqs(dy)]
    res = pl.pallas_call(
        body, name=name, grid=(nc,),
        in_specs=in_specs + [anyspec] * nl,
        out_specs=[row] * (6 * nbr) + [anyspec] * nl,
        out_shape=[jax.ShapeDtypeStruct((nb, seq, w), F32)] * (6 * nbr) + [jax.ShapeDtypeStruct(a.shape, a.dtype) for a in late],
        scratch_shapes=[pltpu.VMEM((per, LANES, LANES), F32)] * nbr
                       + ([pltpu.SemaphoreType.DMA((nl, N_DEV - 1)), pltpu.SemaphoreType.DMA((nl, N_DEV - 1))] if nl else []),
        compiler_params=pltpu.CompilerParams(dimension_semantics=("arbitrary",), vmem_limit_bytes=VMEM_LIMIT),
    )(*args, *late)
    grads = res[:6 * nbr]
    core, chip = lax.axis_index("c"), 2 * lax.axis_index("x") + lax.axis_index("y")
    zero = jnp.zeros((), jnp.int32)
    landed = [lax.dynamic_update_slice(land, lax.dynamic_slice(mine, (core, chip) + (zero,) * (mine.ndim - 2),
                                                               (1, 1) + mine.shape[2:]),
                                       (core, chip) + (zero,) * (mine.ndim - 2))
              for land, mine in zip(res[6 * nbr:], late)]
    return [[g.reshape(n, w) for g in grads[6 * i:6 * i + 6]] for i in range(nbr)], landed


def _shift_down(a, j, halo, is_start):
    tb = a.shape[0]
    rolled = pltpu.roll(a, j, 0)
    hr = jnp.where(is_start, 0.0, pltpu.roll(halo, j, 0))
    first = jnp.where(_iota((SUB, LANES), 0) < j, hr, rolled[0:SUB])
    if tb == SUB:
        return first
    return jnp.concatenate([first, rolled[SUB:]], axis=0)


def _shift_up(d, j, carry, is_end):
    tb = d.shape[0]
    up = pltpu.roll(d, tb - j, 0)
    cr = jnp.where(is_end, 0.0, pltpu.roll(carry, SUB - j, 0))
    last = jnp.where(_iota((SUB, LANES), 0) >= SUB - j, cr, up[tb - SUB:tb])
    if tb == SUB:
        return last
    return jnp.concatenate([up[:tb - SUB], last], axis=0)


def _ngroups(a):
    return a.shape[1] // LANES


def _pw_fwd(name, f, ins, shift, params, out_widths, out_dtypes, *, seq, tb):
    n = ins[0].shape[0]
    nt, tps = n // tb, seq // tb
    ni, npar = len(ins), len(params)

    def body(*refs):
        in_refs = refs[:ni]
        pos = ni
        halo_ref = None
        if shift:
            halo_ref = refs[pos]
            pos += 1
        p_refs = refs[pos:pos + npar]
        out_refs = refs[pos + npar:]
        is_start = (pl.program_id(0) % tps) == 0
        tiles = [[ref[:, _sl(g)] for g in range(_ngroups(ref))] for ref in in_refs]
        prevs = [[_shift_down(tiles[0][g], j, halo_ref[:, _sl(g)], is_start) for g in range(len(tiles[0]))]
                 for j in range(1, shift + 1)]
        pv = [[ref[:, _sl(g)] for g in range(_ngroups(ref))] for ref in p_refs]
        outs = f(tiles, prevs, pv)
        for o_ref, og in zip(out_refs, outs, strict=True):
            for g, t in enumerate(og):
                o_ref[:, _sl(g)] = t.astype(o_ref.dtype)

    in_specs = [pl.BlockSpec((tb, a.shape[1]), lambda i: (i, 0)) for a in ins]
    args = list(ins)
    if shift:
        in_specs.append(pl.BlockSpec((SUB, ins[0].shape[1]), lambda i: (jnp.maximum(i * (tb // SUB) - 1, 0), 0)))
        args.append(ins[0])
    in_specs += [pl.BlockSpec(p.shape, lambda i: (0, 0)) for p in params]
    args += list(params)
    return pl.pallas_call(
        body, name=name, grid=(nt,),
        in_specs=in_specs,
        out_specs=[pl.BlockSpec((tb, w), lambda i: (i, 0)) for w in out_widths],
        out_shape=[jax.ShapeDtypeStruct((n, w), dt) for w, dt in zip(out_widths, out_dtypes, strict=True)],
        compiler_params=pltpu.CompilerParams(dimension_semantics=("parallel",), vmem_limit_bytes=VMEM_LIMIT),
    )(*args)


def _proj_pw_fwd(name, f, h, wts, shift, params, out_widths, *, seq, tb):
    n = h.shape[0]
    nt, tps = n // tb, seq // tb
    nw, npar = len(wts), len(params)

    def body(*refs):
        h_ref = refs[0]
        w_refs = refs[1:1 + nw]
        par_refs = refs[1 + nw:1 + nw + npar]
        p_refs = refs[1 + nw + npar:1 + 2 * nw + npar]
        out_refs = refs[1 + 2 * nw + npar:len(refs) - 1]
        carry = refs[-1]
        is_start = (pl.program_id(0) % tps) == 0
        for w_ref, p_ref in zip(w_refs, p_refs, strict=True):
            p_ref[...] = lax.dot_general(h_ref[...], w_ref[...], (_NT, ((), ())), preferred_element_type=F32)
        tiles = [[ref[:, _sl(g)] for g in range(_ngroups(ref))] for ref in p_refs]
        prevs = [[_shift_down(tiles[0][g], j, carry[:, _sl(g)], is_start) for g in range(len(tiles[0]))]
                 for j in range(1, shift + 1)]
        carry[...] = p_refs[0][tb - SUB:tb, :]
        pv = [[ref[:, _sl(g)] for g in range(_ngroups(ref))] for ref in par_refs]
        outs = f(tiles, prevs, pv)
        for o_ref, og in zip(out_refs, outs, strict=True):
            for g, t in enumerate(og):
                o_ref[:, _sl(g)] = t

    widths = [w.shape[0] for w in wts] + list(out_widths)
    res = pl.pallas_call(
        body, name=name, grid=(nt,),
        in_specs=([pl.BlockSpec((tb, D_MODEL), lambda i: (i, 0))] + [pl.BlockSpec(w.shape, lambda i: (0, 0)) for w in wts]
                  + [pl.BlockSpec(p.shape, lambda i: (0, 0)) for p in params]),
        out_specs=[pl.BlockSpec((tb, w), lambda i: (i, 0)) for w in widths],
        out_shape=[jax.ShapeDtypeStruct((n, w), F32) for w in widths],
        scratch_shapes=[pltpu.VMEM((SUB, wts[0].shape[0]), F32)],
        compiler_params=pltpu.CompilerParams(dimension_semantics=("arbitrary",), vmem_limit_bytes=VMEM_LIMIT),
    )(h, *wts, *params)
    return res[:nw], res[nw:]


def _pw_bwd(name, f, ins, shift, params, douts, din_dtypes, *, seq, tb):
    n = ins[0].shape[0]
    nt, tps = n // tb, seq // tb
    ni, npar = len(ins), len(params)
    flat_douts = [d for ds in douts for d in ds]
    nd = len(flat_douts)
    w0 = ins[0].shape[1]

    def body(*refs):
        in_refs = refs[:ni]
        pos = ni
        halo_ref = None
        if shift:
            halo_ref = refs[pos]
            pos += 1
        p_refs = refs[pos:pos + npar]
        pos += npar
        d_refs = refs[pos:pos + nd]
        pos += nd
        din_refs = refs[pos:pos + ni]
        pos += ni
        dp_refs = refs[pos:pos + npar]
        pos += npar
        carry = refs[pos] if shift else None
        step = pl.program_id(0)
        tile = nt - 1 - step
        is_start = (tile % tps) == 0
        is_end = (tile % tps) == tps - 1
        tiles = [[ref[:, _sl(g)] for g in range(_ngroups(ref))] for ref in in_refs]
        prevs = [[_shift_down(tiles[0][g], j, halo_ref[:, _sl(g)], is_start) for g in range(len(tiles[0]))]
                 for j in range(1, shift + 1)]
        pv = [[ref[:, _sl(g)] for g in range(_ngroups(ref))] for ref in p_refs]
        cot, pos_d = [], 0
        for ds in douts:
            grp = d_refs[pos_d:pos_d + len(ds)]
            pos_d += len(ds)
            cot.append([functools.reduce(lambda p, q: p + q, [ref[:, _sl(g)].astype(F32) for ref in grp])
                        for g in range(_ngroups(grp[0]))])
        _, vjp = jax.vjp(f, tiles, prevs, pv)
        d_tiles, d_prevs, d_pv = vjp(cot)
        for g in range(len(tiles[0])):
            for j in range(1, shift + 1):
                d_tiles[0][g] = d_tiles[0][g] + _shift_up(d_prevs[j - 1][g], j, carry[j - 1, :, _sl(g)], is_end)
            for j in range(1, shift + 1):
                carry[j - 1, :, _sl(g)] = d_prevs[j - 1][g][0:SUB]
        for ref, dg in zip(din_refs, d_tiles, strict=True):
            for g, t in enumerate(dg):
                ref[:, _sl(g)] = t.astype(ref.dtype)

        @pl.when(step == 0)
        def _():
            for ref in dp_refs:
                ref[...] = jnp.zeros_like(ref)
        for ref, dg in zip(dp_refs, d_pv, strict=True):
            for g, t in enumerate(dg):
                ref[:, _sl(g)] += t

    rev = lambda i: (nt - 1 - i, 0)
    in_specs = [pl.BlockSpec((tb, a.shape[1]), rev) for a in ins]
    args = list(ins)
    if shift:
        in_specs.append(pl.BlockSpec((SUB, w0), lambda i: (jnp.maximum((nt - 1 - i) * (tb // SUB) - 1, 0), 0)))
        args.append(ins[0])
    in_specs += [pl.BlockSpec(p.shape, lambda i: (0, 0)) for p in params]
    args += list(params)
    in_specs += [pl.BlockSpec((tb, d.shape[1]), rev) for d in flat_douts]
    args += flat_douts
    out_specs = [pl.BlockSpec((tb, a.shape[1]), rev) for a in ins] + [pl.BlockSpec(p.shape, lambda i: (0, 0)) for p in params]
    out_shape = ([jax.ShapeDtypeStruct(a.shape, dt) for a, dt in zip(ins, din_dtypes, strict=True)]
                 + [jax.ShapeDtypeStruct(p.shape, F32) for p in params])
    res = pl.pallas_call(
        body, name=name, grid=(nt,),
        in_specs=in_specs, out_specs=out_specs, out_shape=out_shape,
        scratch_shapes=[pltpu.VMEM((shift, SUB, w0), F32)] if shift else [],
        compiler_params=pltpu.CompilerParams(dimension_semantics=("arbitrary",), vmem_limit_bytes=VMEM_LIMIT),
    )(*args)
    return res[:ni], res[ni:]


def _rwkv_prep_f(tiles, prevs, params):
    (p,), (prev,) = tiles, prevs
    mu, w0, w2p, a0, a2p, k_k, k_a = params
    xs = [p[g] + (prev[g] - p[g]) * mu[g] for g in range(13)]
    wdad = xs[12]
    tw = jnp.tanh(wdad)
    e64 = _seg_ones(64)
    r, lw, k2, v, kk, b = [], [], [], [], [], []
    for g in range(4):
        k_g = xs[4 + g]
        lo = w0[g] + _mm(tw, w2p[g], P_POINT)
        lw_g = -jnp.exp(-_softplus(-lo) - 0.5)
        a_g = _sigmoid(a0[g] + _mm(wdad, a2p[g], P_POINT))
        kkp = k_g * k_k[g]
        kk_g = kkp * lax.rsqrt(_mm(kkp * kkp, e64, P_POINT) + 1e-12)
        r.append(xs[g])
        lw.append(lw_g)
        k2.append(k_g * (1.0 + (a_g - 1.0) * k_a[g]))
        v.append(xs[8 + g])
        kk.append(kk_g)
        b.append(kk_g * a_g)
    return [r, lw, k2, v, kk, b]


def _rwkv_post_f(tiles, prevs, params):
    yrec, r, k2, v, z = tiles
    gn_w, gn_b, r_k = params
    e64 = _seg_ones(64)
    out = []
    for g in range(4):
        mean = _mm(yrec[g], e64, P_POINT) * (1.0 / 64)
        d = yrec[g] - mean
        var = _mm(d * d, e64, P_POINT) * (1.0 / 64)
        yn = d * lax.rsqrt(var + RW_GN_EPS) * gn_w[g] + gn_b[g]
        bonus = _mm(r[g] * k2[g] * r_k[g], e64, P_POINT) * v[g]
        out.append((yn + bonus) * _silu(z[g]))
    return [out]


def _gdn_prep_f(tiles, prevs, params):
    x, (ba,) = tiles
    p1, p2, p3 = prevs
    cw0, cw1, cw2, cw3, a_log, dt_bias = params
    s = [_silu(cw3[g] * x[g] + cw2[g] * p1[g] + cw1[g] * p2[g] + cw0[g] * p3[g]) for g in range(12)]
    row = _iota((LANES, LANES), 0)
    r, lw, k, vv, b = [], [], [], [], []
    for h in range(4):
        q_h, k_h, v_h = s[h], s[4 + h], s[8 + h]
        qn = q_h * lax.rsqrt(jnp.sum(q_h * q_h, axis=-1, keepdims=True) + 1e-12)
        kn = k_h * lax.rsqrt(jnp.sum(k_h * k_h, axis=-1, keepdims=True) + 1e-12)
        beta = _sigmoid(_mm(ba, (row == h).astype(F32)))
        alpha = _mm(ba, (row == 4 + h).astype(F32))
        g_h = -jnp.exp(a_log[h]) * _softplus(alpha + dt_bias[h])
        r.append(qn * (LANES ** -0.5))
        lw.append(g_h)
        k.append(kn)
        vv.append(beta * v_h)
        b.append(jnp.exp(g_h) * beta * kn)
    return [r, lw, k, vv, b]


def _gdn_post_f(tiles, prevs, params):
    o, z = tiles
    ((onw,),) = params
    out = []
    for h in range(4):
        ms = jnp.mean(o[h] * o[h], axis=-1, keepdims=True)
        out.append(o[h] * lax.rsqrt(ms + NORM_EPS) * onw * _silu(z[h]))
    return [out]


def _norm_in(x2, g_in, *, tm):
    n = x2.shape[0]

    def body(x_ref, g_ref, h_ref):
        x = x_ref[...]
        rs = lax.rsqrt(jnp.mean(x * x, axis=-1, keepdims=True) + NORM_EPS)
        h_ref[...] = (x * rs * g_ref[...]).astype(BF16)

    return pl.pallas_call(
        body, name="norm_in", grid=(n // tm,),
        in_specs=[pl.BlockSpec((tm, D_MODEL), lambda i: (i, 0)), pl.BlockSpec((1, D_MODEL), lambda i: (0, 0))],
        out_specs=pl.BlockSpec((tm, D_MODEL), lambda i: (i, 0)),
        out_shape=jax.ShapeDtypeStruct((n, D_MODEL), BF16),
        compiler_params=pltpu.CompilerParams(dimension_semantics=("parallel",), vmem_limit_bytes=VMEM_LIMIT),
    )(x2, g_in)


def _proj(name, h, wt, *, tm):
    n, ws = h.shape[0], wt.shape[0]

    def body(h_ref, w_ref, o_ref):
        o_ref[...] = lax.dot_general(h_ref[...], w_ref[...], (_NT, ((), ())), preferred_element_type=F32)

    return pl.pallas_call(
        body, name=name, grid=(n // tm,),
        in_specs=[pl.BlockSpec((tm, D_MODEL), lambda i: (i, 0)), pl.BlockSpec((ws, D_MODEL), lambda i: (0, 0))],
        out_specs=pl.BlockSpec((tm, ws), lambda i: (i, 0)),
        out_shape=jax.ShapeDtypeStruct((n, ws), F32),
        compiler_params=pltpu.CompilerParams(dimension_semantics=("parallel",), vmem_limit_bytes=VMEM_LIMIT),
    )(h, wt)


def _proj_dw(name, h, dp, *, tm):
    n, ws = dp.shape

    def body(h_ref, d_ref, o_ref):
        @pl.when(pl.program_id(0) == 0)
        def _():
            o_ref[...] = jnp.zeros_like(o_ref)
        o_ref[...] += lax.dot_general(d_ref[...], h_ref[...], (_TN, ((), ())), preferred_element_type=F32)

    return pl.pallas_call(
        body, name=name, grid=(n // tm,),
        in_specs=[pl.BlockSpec((tm, D_MODEL), lambda i: (i, 0)), pl.BlockSpec((tm, ws), lambda i: (i, 0))],
        out_specs=pl.BlockSpec((ws, D_MODEL), lambda i: (0, 0)),
        out_shape=jax.ShapeDtypeStruct((ws, D_MODEL), F32),
        compiler_params=pltpu.CompilerParams(dimension_semantics=("arbitrary",), vmem_limit_bytes=VMEM_LIMIT),
    )(h, dp)


def _proj_dx(x2, g_in, d_xo, dps, ws, *, tm):
    n = x2.shape[0]
    ns = len(dps)

    def body(*refs):
        x_ref, g_ref, dxo_ref = refs[:3]
        dp_refs = refs[3:3 + ns]
        w_refs = refs[3 + ns:3 + 2 * ns]
        dx_ref, dg_ref = refs[3 + 2 * ns:]
        dh = jnp.zeros((tm, D_MODEL), F32)
        for d_ref, w_ref in zip(dp_refs, w_refs, strict=True):
            dh = dh + jnp.dot(d_ref[...], w_ref[...], preferred_element_type=F32)
        x = x_ref[...]
        rs = lax.rsqrt(jnp.mean(x * x, axis=-1, keepdims=True) + NORM_EPS)
        xn = x * rs
        dxn = dh * g_ref[...]
        dx_ref[...] = dxo_ref[...] + rs * (dxn - xn * jnp.mean(dxn * xn, axis=-1, keepdims=True))

        @pl.when(pl.program_id(0) == 0)
        def _():
            dg_ref[...] = jnp.zeros_like(dg_ref)
        dg_ref[...] += jnp.sum(dh * xn, axis=0, keepdims=True)

    row = pl.BlockSpec((tm, D_MODEL), lambda i: (i, 0))
    return pl.pallas_call(
        body, name="proj_dx", grid=(n // tm,),
        in_specs=([row, pl.BlockSpec((1, D_MODEL), lambda i: (0, 0)), row]
                  + [pl.BlockSpec((tm, d.shape[1]), lambda i: (i, 0)) for d in dps]
                  + [pl.BlockSpec(w.shape, lambda i: (0, 0)) for w in ws]),
        out_specs=[row, pl.BlockSpec((1, D_MODEL), lambda i: (0, 0))],
        out_shape=[jax.ShapeDtypeStruct((n, D_MODEL), F32), jax.ShapeDtypeStruct((1, D_MODEL), F32)],
        compiler_params=pltpu.CompilerParams(dimension_semantics=("arbitrary",), vmem_limit_bytes=VMEM_LIMIT),
    )(x2, g_in, d_xo, *dps, *ws)


def _tail(x2, tgt2, gates, rw_post_ins, gd_post_ins, rw_post_params, gd_post_params, din_dtypes, w_a, w_b, w_o, now, *, tr):
    n = x2.shape[0]
    n_rw, n_gd = len(rw_post_ins), len(gd_post_ins)
    n_rwp, n_gdp = len(rw_post_params), len(gd_post_params)

    def body(*refs):
        x_ref, t_ref, g_ref = refs[:3]
        pos = 3
        rw_refs, gd_refs = refs[pos:pos + n_rw], refs[pos + n_rw:pos + n_rw + n_gd]
        pos += n_rw + n_gd
        rwp_refs, gdp_refs = refs[pos:pos + n_rwp], refs[pos + n_rwp:pos + n_rwp + n_gdp]
        pos += n_rwp + n_gdp
        wa_ref, wb_ref, wo_ref, now_ref = refs[pos:pos + 4]
        pos += 4
        d_rw_refs, d_gd_refs = refs[pos:pos + n_rw], refs[pos + n_rw:pos + n_rw + n_gd]
        pos += n_rw + n_gd
        dg_ref, dxo_ref, dwa_ref, dwb_ref, dwo_ref, dnow_ref, loss_ref = refs[pos:pos + 7]
        d_rwp_refs, d_gdp_refs = refs[pos + 7:pos + 7 + n_rwp], refs[pos + 7 + n_rwp:]
        groups = lambda rs: [[ref[:, _sl(g)] for g in range(_ngroups(ref))] for ref in rs]
        (ya_groups,), rw_vjp = jax.vjp(lambda t, p: _rwkv_post_f(t, [], p), groups(rw_refs), groups(rwp_refs))
        (yb_groups,), gd_vjp = jax.vjp(lambda t, p: _gdn_post_f(t, [], p), groups(gd_refs), groups(gdp_refs))
        ya16 = jnp.concatenate(ya_groups, axis=1).astype(BF16)
        yb16 = jnp.concatenate(yb_groups, axis=1).astype(BF16)
        ua = jnp.dot(ya16, wa_ref[...], preferred_element_type=F32)
        ub = jnp.dot(yb16, wb_ref[...], preferred_element_type=F32)
        ga = _sigmoid(g_ref[:, :D_MODEL])
        gb = _sigmoid(g_ref[:, D_MODEL:])
        m16 = (ga * ua + gb * ub).astype(BF16)
        xo = x_ref[...] + jnp.dot(m16, wo_ref[...], preferred_element_type=F32)
        rs = lax.rsqrt(jnp.mean(xo * xo, axis=-1, keepdims=True) + NORM_EPS)
        yn = xo * rs
        now_v = now_ref[...]
        err = yn * now_v - t_ref[...]
        dy = err * (1.0 / D_MODEL)
        dyn = dy * now_v
        dxo = rs * (dyn - yn * jnp.mean(dyn * yn, axis=-1, keepdims=True))
        dxo_ref[...] = dxo
        dxo16 = dxo.astype(BF16)
        dm = lax.dot_general(dxo16, wo_ref[...], (((1,), (1,)), ((), ())), preferred_element_type=F32)
        dua16 = (dm * ga).astype(BF16)
        dub16 = (dm * gb).astype(BF16)
        dg_ref[:, :D_MODEL] = (dm * ua * ga * (1.0 - ga)).astype(dg_ref.dtype)
        dg_ref[:, D_MODEL:] = (dm * ub * gb * (1.0 - gb)).astype(dg_ref.dtype)
        dya = lax.dot_general(dua16, wa_ref[...], (((1,), (1,)), ((), ())), preferred_element_type=F32)
        dyb = lax.dot_general(dub16, wb_ref[...], (((1,), (1,)), ((), ())), preferred_element_type=F32)
        d_rw_tiles, d_rw_pv = rw_vjp([[dya[:, _sl(g)] for g in range(RW_W // LANES)]])
        d_gd_tiles, d_gd_pv = gd_vjp([[dyb[:, _sl(g)] for g in range(GD_W // LANES)]])
        for ref, dgroups in zip(d_rw_refs + d_gd_refs, d_rw_tiles + d_gd_tiles, strict=True):
            for g, t in enumerate(dgroups):
                ref[:, _sl(g)] = t.astype(ref.dtype)

        @pl.when(pl.program_id(0) == 0)
        def _():
            for ref in (dwa_ref, dwb_ref, dwo_ref, dnow_ref, loss_ref) + d_rwp_refs + d_gdp_refs:
                ref[...] = jnp.zeros_like(ref)
        for ref, dgroups in zip(d_rwp_refs + d_gdp_refs, d_rw_pv + d_gd_pv, strict=True):
            for g, t in enumerate(dgroups):
                ref[:, _sl(g)] += t
        tn = (((0,), (0,)), ((), ()))
        dwo_ref[...] += lax.dot_general(m16, dxo16, tn, preferred_element_type=F32)
        dwa_ref[...] += lax.dot_general(ya16, dua16, tn, preferred_element_type=F32)
        dwb_ref[...] += lax.dot_general(yb16, dub16, tn, preferred_element_type=F32)
        dnow_ref[...] += jnp.sum(dy * yn, axis=0, keepdims=True)
        loss_ref[...] += (0.5 / D_MODEL) * jnp.sum(err * err)

    row = lambda w: pl.BlockSpec((tr, w), lambda i: (i, 0))
    full = lambda a: pl.BlockSpec(a.shape, lambda i: (0, 0))
    res = pl.pallas_call(
        body, name="tail", grid=(n // tr,),
        in_specs=([row(D_MODEL), row(D_MODEL), row(2 * D_MODEL)] + [row(a.shape[1]) for a in rw_post_ins + gd_post_ins]
                  + [full(p) for p in rw_post_params + gd_post_params] + [full(w_a), full(w_b), full(w_o), full(now)]),
        out_specs=([row(a.shape[1]) for a in rw_post_ins + gd_post_ins] + [row(2 * D_MODEL), row(D_MODEL),
                   pl.BlockSpec((RW_W, D_MODEL), lambda i: (0, 0)), pl.BlockSpec((GD_W, D_MODEL), lambda i: (0, 0)),
                   pl.BlockSpec((D_MODEL, D_MODEL), lambda i: (0, 0)), pl.BlockSpec((1, D_MODEL), lambda i: (0, 0)),
                   pl.BlockSpec((SUB, LANES), lambda i: (0, 0))] + [full(p) for p in rw_post_params + gd_post_params]),
        out_shape=([jax.ShapeDtypeStruct(a.shape, dt) for a, dt in zip(rw_post_ins + gd_post_ins, din_dtypes, strict=True)]
                   + [jax.ShapeDtypeStruct((n, 2 * D_MODEL), BF16), jax.ShapeDtypeStruct((n, D_MODEL), F32),
                      jax.ShapeDtypeStruct((RW_W, D_MODEL), F32), jax.ShapeDtypeStruct((GD_W, D_MODEL), F32),
                      jax.ShapeDtypeStruct((D_MODEL, D_MODEL), F32), jax.ShapeDtypeStruct((1, D_MODEL), F32),
                      jax.ShapeDtypeStruct((SUB, LANES), F32)]
                   + [jax.ShapeDtypeStruct(p.shape, F32) for p in rw_post_params + gd_post_params]),
        compiler_params=pltpu.CompilerParams(dimension_semantics=("arbitrary",), vmem_limit_bytes=VMEM_LIMIT),
    )(x2, tgt2, gates, *rw_post_ins, *gd_post_ins, *rw_post_params, *gd_post_params, w_a, w_b, w_o, now)
    ni, npar = n_rw + n_gd, n_rwp + n_gdp
    return res[:ni], res[ni:ni + 7], res[ni + 7:ni + 7 + npar]


def _exchange(name, axes, scatter, gather, place_own=True):
    ns, ng = len(scatter), len(gather)
    na = ns + ng
    gs = 2 ** len(axes)
    arrs = list(scatter) + list(gather)

    def body(*refs):
        src = refs[:na]
        dst = refs[na:2 * na]
        send_sems, recv_sems = refs[2 * na:]
        mine = {ax: lax.axis_index(ax) for ax in ("x", "y", "c")}

        def peer(k):
            co = dict(mine)
            for i, ax in enumerate(axes):
                if (k >> (len(axes) - 1 - i)) & 1:
                    co[ax] = 1 - co[ax]
            idx = 0
            for ax in axes:
                idx = 2 * idx + co[ax]
            return (co["x"], co["y"], co["c"]), idx

        _, me = peer(0)

        def copy(a, k, landing):
            dev, idx = peer(k)
            s = src[a].at[idx] if a < ns else src[a]
            return pltpu.make_async_remote_copy(src_ref=s, dst_ref=dst[a].at[idx if landing else me],
                                                send_sem=send_sems.at[a, k - 1], recv_sem=recv_sems.at[a, k - 1],
                                                device_id=dev, device_id_type=pl.DeviceIdType.MESH)

        sends = [copy(a, k, False) for a in range(na) for k in range(1, gs)]
        for cp in sends:
            cp.start()
        for a in range(na):
            for k in range(1, gs):
                copy(a, k, True).wait_recv()
        for cp in sends:
            cp.wait_send()

    out_shape = [jax.ShapeDtypeStruct(a.shape, a.dtype) for a in scatter] + \
                [jax.ShapeDtypeStruct((gs,) + a.shape, a.dtype) for a in gather]
    anyspec = pl.BlockSpec(memory_space=pl.ANY)
    lands = pl.pallas_call(
        body, name=name,
        in_specs=[anyspec] * na, out_specs=[anyspec] * na, out_shape=out_shape,
        scratch_shapes=[pltpu.SemaphoreType.DMA((na, gs - 1)), pltpu.SemaphoreType.DMA((na, gs - 1))],
    )(*arrs)
    if not place_own:
        return lands
    me = 0
    for ax in axes:
        me = 2 * me + lax.axis_index(ax)
    kept = [lax.dynamic_index_in_dim(a, me, 0, keepdims=False) for a in scatter] + list(gather)
    return [lax.dynamic_update_index_in_dim(land, mine, me, 0) for land, mine in zip(lands, kept)]


def _gather_all(name, arrs):
    na = len(arrs)

    def body(*refs):
        src = refs[:na]
        dst = refs[na:2 * na]
        send_sems, recv_sems = refs[2 * na:]
        x, y, c = lax.axis_index("x"), lax.axis_index("y"), lax.axis_index("c")
        sibling = (x, y, 1 - c)
        chips = [(1 - x, y), (x, 1 - y), (1 - x, 1 - y)]

        def copy(a, k, block, to, own=False):
            px, py, pc = block
            slot = dst[a].at[pc, 2 * px + py]
            return pltpu.make_async_remote_copy(src_ref=src[a] if own else slot, dst_ref=slot,
                                                send_sem=send_sems.at[a, k], recv_sem=recv_sems.at[a, k],
                                                device_id=to, device_id_type=pl.DeviceIdType.MESH)

        first = [copy(a, 0, (x, y, c), sibling, own=True) for a in range(na)]
        first += [copy(a, 1 + j, (x, y, c), (*chip, c), own=True) for j, chip in enumerate(chips) for a in range(na)]
        for cp in first:
            cp.start()
        passed = []
        for j, chip in enumerate(chips):
            for a in range(na):
                copy(a, 1 + j, (*chip, c), (x, y, c)).wait_recv()
                passed.append(copy(a, 4 + j, (*chip, c), sibling))
                passed[-1].start()
        for a in range(na):
            copy(a, 0, (x, y, 1 - c), (x, y, c)).wait_recv()
            for j, chip in enumerate(chips):
                copy(a, 4 + j, (*chip, 1 - c), (x, y, c)).wait_recv()
        for cp in first + passed:
            cp.wait_send()

    anyspec = pl.BlockSpec(memory_space=pl.ANY)
    lands = pl.pallas_call(
        body, name=name,
        in_specs=[anyspec] * na, out_specs=[anyspec] * na,
        out_shape=[jax.ShapeDtypeStruct((2, 4) + a.shape, a.dtype) for a in arrs],
        scratch_shapes=[pltpu.SemaphoreType.DMA((na, 7)), pltpu.SemaphoreType.DMA((na, 7))],
    )(*arrs)
    core, chip = lax.axis_index("c"), 2 * lax.axis_index("x") + lax.axis_index("y")
    zero = jnp.zeros((), jnp.int32)
    return [lax.dynamic_update_slice(land, mine[None, None], (core, chip) + (zero,) * mine.ndim)
            for land, mine in zip(lands, arrs)]


def _pair_sum(name, own, land, out_dtype):
    _, nq, r, c = own.shape
    core = lax.axis_index("c").astype(jnp.int32).reshape(1)

    def body(core_ref, own_ref, land_ref, o_ref):
        o_ref[0] = (own_ref[0, 0] + land_ref[0, 0]).astype(o_ref.dtype)

    return pl.pallas_call(
        body, name=name,
        grid_spec=pltpu.PrefetchScalarGridSpec(
            num_scalar_prefetch=1, grid=(nq,),
            in_specs=[pl.BlockSpec((1, 1, r, c), lambda i, core_ref: (core_ref[0], i, 0, 0)),
                      pl.BlockSpec((1, 1, r, c), lambda i, core_ref: (1 - core_ref[0], i, 0, 0))],
            out_specs=pl.BlockSpec((1, r, c), lambda i, core_ref: (i, 0, 0))),
        out_shape=jax.ShapeDtypeStruct((nq, r, c), out_dtype),
        compiler_params=pltpu.CompilerParams(dimension_semantics=("parallel",), vmem_limit_bytes=VMEM_LIMIT),
    )(core, own, land)


def _adam(name, land, w, m, v):
    r, c = w.shape
    nslot = land.shape[0]
    tr = 256 if (r % 256 == 0 and r > 256) else r
    tc = 256 if (tr == r and r > 256 and c % 256 == 0) else c

    def body(l_ref, w_ref, m_ref, v_ref, g_out, d_out, m_out, v_out):
        g = l_ref[0].astype(F32)
        for s in range(1, nslot):
            g = g + l_ref[s].astype(F32)
        g_out[...] = g
        d_out[...], m_out[...], v_out[...] = _adam_math(g, w_ref[...], m_ref[...], v_ref[...])

    blk = pl.BlockSpec((tr, tc), lambda i: (i * tc // c, i % (c // tc)))
    return pl.pallas_call(
        body, name=name, grid=((r // tr) * (c // tc),),
        in_specs=[pl.BlockSpec((nslot, tr, tc), lambda i: (0, i * tc // c, i % (c // tc))), blk, blk, blk],
        out_specs=[blk] * 4,
        out_shape=[jax.ShapeDtypeStruct((r, c), F32)] * 4,
        compiler_params=pltpu.CompilerParams(dimension_semantics=("parallel",), vmem_limit_bytes=VMEM_LIMIT),
    )(land, w, m, v)


def _adam_math(g, w, m, v):
    c1 = 1.0 / (1.0 - ADAM_B1 ** ADAM_STEP)
    c2 = 1.0 / (1.0 - ADAM_B2 ** ADAM_STEP)
    m_new = ADAM_B1 * m + (1.0 - ADAM_B1) * g
    v_new = ADAM_B2 * v + (1.0 - ADAM_B2) * (g * g)
    return -ADAM_LR * ((m_new * c1) / (jnp.sqrt(v_new * c2) + ADAM_EPS) + ADAM_WD * w), m_new, v_new


def _adam_small(land, ws, ms, vs):
    npar = len(ws)
    nslot = land.shape[0]

    def body(*refs):
        l_ref = refs[0]
        w_refs, m_refs, v_refs = refs[1:1 + npar], refs[1 + npar:1 + 2 * npar], refs[1 + 2 * npar:1 + 3 * npar]
        outs = refs[1 + 3 * npar:1 + 7 * npar]
        loss_ref, g_rows = refs[1 + 7 * npar], refs[2 + 7 * npar]
        g = l_ref[0]
        for s in range(1, nslot):
            g = g + l_ref[s]
        g_rows[...] = g
        row = 0
        for i, (_, size) in enumerate(_SMALL):
            for j in range(-(-size // LANES)):
                width = min(LANES, size - j * LANES)
                cols = slice(j * LANES, j * LANES + width)
                g_ij = g_rows[row:row + 1, 0:width]
                delta, m_new, v_new = _adam_math(g_ij, w_refs[i][:, cols], m_refs[i][:, cols], v_refs[i][:, cols])
                for ref, val in zip(outs[4 * i:4 * i + 4], (g_ij, delta, m_new, v_new)):
                    ref[:, cols] = val
                row += 1
        loss_ref[...] = g_rows[row:row + 1, :]

    full = lambda a: pl.BlockSpec(a.shape, lambda: (0,) * a.ndim)
    res = pl.pallas_call(
        body, name="adam_small",
        in_specs=[full(land)] + [full(a) for a in list(ws) + list(ms) + list(vs)],
        out_specs=[full(w) for w in ws for _ in range(4)] + [pl.BlockSpec((1, LANES), lambda: (0, 0))],
        out_shape=[jax.ShapeDtypeStruct(w.shape, F32) for w in ws for _ in range(4)] + [jax.ShapeDtypeStruct((1, LANES), F32)],
        scratch_shapes=[pltpu.VMEM(land.shape[1:], F32)],
    )(land, *ws, *ms, *vs)
    return [res[4 * i:4 * i + 4] for i in range(npar)], res[4 * npar]


_SMALL = (("norm_in_w", 1024), ("rw_mu", 1664), ("rw_w0", 512), ("rw_a0", 512), ("rw_k_k", 512), ("rw_k_a", 512),
          ("rw_r_k", 512), ("rw_gn_w", 512), ("rw_gn_b", 512), ("gd_A_log", 4), ("gd_dt_bias", 4), ("gd_o_norm_w", 128),
          ("norm_out_w", 1024))
_SMALL_ROWS = 64


def _pack_small(vals, loss_row):
    rows = []
    for (_, size), a in zip(_SMALL, vals, strict=True):
        flat = a.reshape(-1).astype(F32)
        pad = (-size) % LANES
        if pad:
            flat = jnp.concatenate([flat, jnp.zeros((pad,), F32)])
        rows.append(flat.reshape(-1, LANES))
    rows.append(loss_row)
    used = sum(r.shape[0] for r in rows)
    rows.append(jnp.zeros((_SMALL_ROWS - used, LANES), F32))
    return jnp.concatenate(rows, axis=0)


def kernel(x, norm_in_w, w_in, rw_mu, rw_w0, rw_w2, rw_a0, rw_a2, rw_k_k, rw_k_a, rw_r_k, rw_gn_w, rw_gn_b, gd_conv_w, gd_A_log, gd_dt_bias, gd_o_norm_w, w_branch_a, w_branch_b, w_out, norm_out_w, loss_target, m_norm_in_w, m_w_in, m_rw_mu, m_rw_w0, m_rw_w2, m_rw_a0, m_rw_a2, m_rw_k_k, m_rw_k_a, m_rw_r_k, m_rw_gn_w, m_rw_gn_b, m_gd_conv_w, m_gd_A_log, m_gd_dt_bias, m_gd_o_norm_w, m_w_branch_a, m_w_branch_b, m_w_out, m_norm_out_w, v_norm_in_w, v_w_in, v_rw_mu, v_rw_w0, v_rw_w2, v_rw_a0, v_rw_a2, v_rw_k_k, v_rw_k_a, v_rw_r_k, v_rw_gn_w, v_rw_gn_b, v_gd_conv_w, v_gd_A_log, v_gd_dt_bias, v_gd_o_norm_w, v_w_branch_a, v_w_branch_b, v_w_out, v_norm_out_w):
    nb, seq, _ = x.shape
    n = nb * seq
    tm = min(1024, n)
    tb = min(512, seq)
    x2 = x.reshape(n, D_MODEL)
    tgt2 = loss_target.reshape(n, D_MODEL)
    cols = w_in.shape[2]
    in_cols = cols * N_DEV

    wt_own, mt_own, vt_own = w_in[0].T, m_w_in[0].T, v_w_in[0].T
    g_win, g_w2, g_a2, g_conv = _gather_all("gather_weights", [wt_own.astype(BF16), rw_w2[0], rw_a2[0], gd_conv_w[0]])
    late_w = [w_branch_a[0].astype(BF16), w_branch_b[0].astype(BF16), w_out[0].astype(BF16)]
    unshard_rows = lambda a: jnp.transpose(a, (1, 0, 2, 3)).reshape(N_DEV * a.shape[2], a.shape[3])
    unshard_cols = lambda a: jnp.transpose(a, (2, 1, 0, 3)).reshape(a.shape[2], N_DEV * a.shape[3])
    wt_full = unshard_rows(g_win)
    seg_bounds = ((0, 1664), (1664, 2176), (2176, 3712), (3712, 4224), (4232, in_cols))
    w_rw, w_zrw, w_qkv, w_zgd, w_gates = [wt_full[a:b] for a, b in seg_bounds]
    w_ba = jnp.concatenate([wt_full[4224:4232], jnp.zeros((LANES - 8, D_MODEL), BF16)], axis=0)
    w2_full, a2_full = unshard_cols(g_w2), unshard_cols(g_a2)
    zeros64 = jnp.zeros((64, RW_W), F32)
    w2p = jnp.concatenate([w2_full, zeros64], axis=0)
    a2p = jnp.concatenate([zeros64, a2_full], axis=0)
    conv_full = unshard_cols(g_conv)
    conv_rows = [conv_full[i:i + 1] for i in range(4)]
    a_log_bc = jnp.repeat(gd_A_log, LANES, axis=1)
    dt_bias_bc = jnp.repeat(gd_dt_bias, LANES, axis=1)
    r_k_flat = rw_r_k.reshape(1, RW_W)
    now2 = norm_out_w.reshape(1, D_MODEL)

    h = _norm_in(x2, norm_in_w, tm=tm)
    p_zrw = _proj("proj_zrw", h, w_zrw, tm=tm)
    p_zgd = _proj("proj_zgd", h, w_zgd, tm=tm)
    p_gates = _proj("proj_gates", h, w_gates, tm=tm)
    rw_params = [rw_mu, rw_w0, w2p, rw_a0, a2p, rw_k_k, rw_k_a]
    (p_rw,), (r_a, lw_a, k_a, v_a, kk_a, b_a) = _proj_pw_fwd("proj_rwkv_prep", _rwkv_prep_f, h, [w_rw], 1, rw_params,
                                                             [RW_W] * 6, seq=seq, tb=tb)
    gd_params = conv_rows + [a_log_bc, dt_bias_bc]
    (p_qkv, p_ba), (r_b, lw_b, k_b, v_b, b_b) = _proj_pw_fwd("proj_gdn_prep", _gdn_prep_f, h, [w_qkv, w_ba], 3, gd_params,
                                                             [GD_W] * 5, seq=seq, tb=tb)
    rw_six, gd_six = (r_a, lw_a, k_a, v_a, kk_a, b_a), (r_b, lw_b, k_b, v_b, k_b, b_b)
    ((y_rec, s_a), (o_rec, s_b)), (g_wa, g_wb, g_wo) = _rec_fwd(
        "rec", [(rw_six, 2, False), (gd_six, 1, True)], seq=seq, late=late_w)
    wa_full, wb_full, wo_full = unshard_cols(g_wa), unshard_cols(g_wb), unshard_rows(g_wo)
    post_params = [rw_gn_w, rw_gn_b, r_k_flat]
    ((d_yrec, dr_p, dk_p, dv_p, d_zrw, d_o, d_zgd), (d_gates, d_xo, dwa, dwb, dwo, d_now, loss_acc),
     (*d_post_params, d_onw)) = _tail(
        x2, tgt2, p_gates, [y_rec, r_a, k_a, v_a, p_zrw], [o_rec, p_zgd], post_params, [gd_o_norm_w],
        [F32, F32, F32, F32, BF16, F32, BF16], wa_full, wb_full, wo_full, now2, tr=min(256, n))

    shard_cols = lambda a: jnp.transpose(a.reshape(a.shape[0], 4, 2, a.shape[1] // N_DEV), (2, 1, 0, 3))
    shard_rows = lambda a: jnp.transpose(a.reshape(4, 2, a.shape[0] // N_DEV, a.shape[1]), (1, 0, 2, 3))
    ((dr_a, dlw_a, dk_a, dv_a, dkk_a, db_a), (dr_b, dlw_b, dk_b, dv_b, dkk_b, db_b)), late_lands = _rec_bwd(
        "rec_bwd", [(rw_six, s_a, d_yrec, 2, False), (gd_six, s_b, d_o, 1, True)], seq=seq,
        late=[shard_cols(dwa), shard_cols(dwb), shard_rows(dwo)])
    (d_qkv, d_ba), d_gd_params = _pw_bwd("gdn_prep_bwd", _gdn_prep_f, [p_qkv, p_ba], 3, gd_params,
                                         [[dr_b], [dlw_b], [dk_b, dkk_b], [dv_b], [db_b]], [BF16, BF16], seq=seq, tb=tb)
    (d_prw,), d_rw_params = _pw_bwd("rwkv_prep_bwd", _rwkv_prep_f, [p_rw], 1, rw_params,
                                    [[dr_a, dr_p], [dlw_a], [dk_a, dk_p], [dv_a, dv_p], [dkk_a], [db_a]], [BF16],
                                    seq=seq, tb=tb)

    dps = [d_prw, d_zrw, d_qkv, d_zgd, d_ba, d_gates]
    wsegs = [w_rw, w_zrw, w_qkv, w_zgd, w_ba, w_gates]
    dx2, d_gin = _proj_dx(x2, norm_in_w, d_xo, dps, wsegs, tm=min(256, n))
    dw_rw = _proj_dw("dw_rw", h, d_prw, tm=tm)
    dw_zrw = _proj_dw("dw_zrw", h, d_zrw, tm=tm)
    dw_qkv = _proj_dw("dw_qkv", h, d_qkv, tm=tm)
    dw_zgd = _proj_dw("dw_zgd", h, d_zgd, tm=tm)
    dw_ba = _proj_dw("dw_ba", h, d_ba, tm=tm)
    dw_gates = _proj_dw("dw_gates", h, d_gates, tm=tm)
    dwt_in_full = jnp.concatenate([dw_rw, dw_zrw, dw_qkv, dw_zgd, dw_ba[:8], dw_gates], axis=0)

    shard_cols = lambda a: jnp.transpose(a.reshape(a.shape[0], 4, 2, a.shape[1] // N_DEV), (2, 1, 0, 3))
    shard_rows = lambda a: jnp.transpose(a.reshape(4, 2, a.shape[0] // N_DEV, a.shape[1]), (1, 0, 2, 3))
    d_mu, d_w0, d_w2p, d_a0, d_a2p, d_kk_, d_ka_ = d_rw_params
    d_gnw, d_gnb, d_rk = d_post_params
    d_conv = jnp.concatenate(d_gd_params[:4], axis=0)
    d_alog = d_gd_params[4].reshape(4, LANES).sum(axis=1).reshape(1, 4)
    d_dtb = d_gd_params[5].reshape(4, LANES).sum(axis=1).reshape(1, 4)
    scat = [shard_rows(dwt_in_full), shard_cols(d_w2p[:64]), shard_cols(d_a2p[64:]), shard_cols(d_conv)]
    small_g = _pack_small([d_gin, d_mu, d_w0, d_a0, d_kk_, d_ka_, d_rk, d_gnw, d_gnb, d_alog, d_dtb, d_onw, d_now],
                          loss_acc[0:1])
    scat.append(jnp.stack([small_g, small_g])[:, None])
    pair = _exchange("reduce_cores", ("c",), scat, [], place_own=False)
    part = [_pair_sum("pair_sum_%d" % i, own, got, BF16 if i < 4 else F32)
            for i, (own, got) in enumerate(zip(scat, pair))]
    lands = _exchange("reduce_chips", ("x", "y"), part[:4], [part[4][0]])
    lands = lands[:4] + [a.reshape((N_DEV,) + a.shape[2:]) for a in late_lands] + lands[4:]

    small_w = [norm_in_w, rw_mu, rw_w0, rw_a0, rw_k_k, rw_k_a, rw_r_k, rw_gn_w, rw_gn_b, gd_A_log, gd_dt_bias, gd_o_norm_w, norm_out_w]
    small_m = [m_norm_in_w, m_rw_mu, m_rw_w0, m_rw_a0, m_rw_k_k, m_rw_k_a, m_rw_r_k, m_rw_gn_w, m_rw_gn_b, m_gd_A_log, m_gd_dt_bias, m_gd_o_norm_w, m_norm_out_w]
    small_v = [v_norm_in_w, v_rw_mu, v_rw_w0, v_rw_a0, v_rw_k_k, v_rw_k_a, v_rw_r_k, v_rw_gn_w, v_rw_gn_b, v_gd_A_log, v_gd_dt_bias, v_gd_o_norm_w, v_norm_out_w]
    flat = lambda arrs: [a.reshape(1, -1) for a in arrs]
    sm, loss_row = _adam_small(lands[7], flat(small_w), flat(small_m), flat(small_v))
    sm_g, sm_d, sm_m, sm_v = [{nm: res[i].reshape(w.shape) for (nm, _), res, w in zip(_SMALL, sm, small_w)}
                              for i in range(4)]

    big = {"w_in": [o.T[None] for o in _adam("adam_w_in", lands[0], wt_own, mt_own, vt_own)]}
    for nm, land, w, m, v in (("rw_w2", lands[1], rw_w2, m_rw_w2, v_rw_w2),
                              ("rw_a2", lands[2], rw_a2, m_rw_a2, v_rw_a2),
                              ("gd_conv_w", lands[3], gd_conv_w, m_gd_conv_w, v_gd_conv_w),
                              ("w_branch_a", lands[4], w_branch_a, m_w_branch_a, v_w_branch_a),
                              ("w_branch_b", lands[5], w_branch_b, m_w_branch_b, v_w_branch_b),
                              ("w_out", lands[6], w_out, m_w_out, v_w_out)):
        big[nm] = [o.reshape(w.shape) for o in _adam("adam_" + nm, land, w[0], m[0], v[0])]

    order = ["norm_in_w", "w_in", "rw_mu", "rw_w0", "rw_w2", "rw_a0", "rw_a2", "rw_k_k", "rw_k_a", "rw_r_k", "rw_gn_w",
             "rw_gn_b", "gd_conv_w", "gd_A_log", "gd_dt_bias", "gd_o_norm_w", "w_branch_a", "w_branch_b", "w_out", "norm_out_w"]
    pick = lambda nm, i: big[nm][i] if nm in big else (sm_g, sm_d, sm_m, sm_v)[i][nm]
    loss = loss_row[0, 0]
    grad_x = dx2.reshape(x.shape)
    return (loss, grad_x, *[pick(nm, 0) for nm in order], *[pick(nm, 1) for nm in order],
            *[pick(nm, 2) for nm in order], *[pick(nm, 3) for nm in order])
```
